```python
import jax, jax.numpy as jnp
from jax import lax
import numpy as np

D_MODEL = 1024
BATCH = 8
SEQ = 4096
DEPTH = 4

CHUNK = 64
N_META = 16
D_CONV = D_MODEL // 2
N_SB_HEADS = 8
SB_HEAD_DIM = 64
D_SB = N_SB_HEADS * SB_HEAD_DIM
D_MIX = D_CONV + D_SB
CONV_WIDTH = 31
Q_BLOCK = 128
D_IN_PROJ = 3 * D_CONV + 4 * D_SB
RMS_EPS = 1e-6
LN_EPS = 1e-5

kernel_name = "hymba_conformer_stickbreaking_trunk"


def rms_norm(x, g):
    xf = x.astype(jnp.float32)
    y = xf * lax.rsqrt(jnp.mean(xf * xf, axis=-1, keepdims=True) + RMS_EPS)
    return (y * g.astype(jnp.float32)).astype(x.dtype)


def layer_norm(x, g, b):
    xf = x.astype(jnp.float32)
    mu = jnp.mean(xf, axis=-1, keepdims=True)
    var = jnp.mean(jnp.square(xf - mu), axis=-1, keepdims=True)
    y = (xf - mu) * lax.rsqrt(var + LN_EPS)
    return (y * g.astype(jnp.float32) + b.astype(jnp.float32)).astype(x.dtype)


def causal_depthwise_conv(x, w, b):
    c = x.shape[-1]
    y = lax.conv_general_dilated(
        x, w.astype(x.dtype)[:, None, :],
        window_strides=(1,), padding=[(CONV_WIDTH - 1, 0)],
        dimension_numbers=("NWC", "WIO", "NWC"), feature_group_count=c)
    return y + b.astype(x.dtype)


def stick_breaking_attention(q, k, v):
    bsz, seq_len, n_heads, head_dim = q.shape
    lp = -(-seq_len // Q_BLOCK) * Q_BLOCK
    pad = ((0, 0), (0, lp - seq_len), (0, 0), (0, 0))
    q, k, v = (jnp.pad(t, pad).transpose(0, 2, 1, 3) for t in (q, k, v))
    n_blocks = lp // Q_BLOCK
    q_blocks = q.reshape(bsz, n_heads, n_blocks, Q_BLOCK, head_dim).transpose(2, 0, 1, 3, 4)
    key_pos = jnp.arange(lp)
    scale = head_dim ** -0.5

    def one_block(args):
        q_blk, blk = args
        z = jnp.einsum("bhqd,bhkd->bhqk", q_blk, k).astype(jnp.float32) * scale
        q_pos = blk * Q_BLOCK + jnp.arange(Q_BLOCK)
        visible = key_pos[None, :] < q_pos[:, None]
        log_stay = jnp.where(visible, jax.nn.log_sigmoid(-z), 0.0)
        tail = lax.cumsum(log_stay, axis=3, reverse=True) - log_stay
        a = jnp.where(visible, jnp.exp(jax.nn.log_sigmoid(z) + tail), 0.0)
        return jnp.einsum("bhqk,bhkd->bhqd", a.astype(v.dtype), v)

    out = lax.map(one_block, (q_blocks, jnp.arange(n_blocks)))
    out = out.transpose(1, 0, 3, 2, 4).reshape(bsz, lp, n_heads, head_dim)
    return out[:, :seq_len]


def hybrid_layer(h, pre_g, post_g, w_in, conv_w, conv_b, conv_ln_g, conv_ln_b, w_pw2, b_pw2, w_out):
    bsz, seq_len, _ = h.shape
    u = rms_norm(h, pre_g)
    proj = u @ w_in.astype(u.dtype)
    glu_a, glu_b, conv_gate, q, k, v, sb_gate = jnp.split(
        proj, np.cumsum([D_CONV, D_CONV, D_CONV, D_SB, D_SB, D_SB]).tolist(), axis=-1)

    c = glu_a * jax.nn.sigmoid(glu_b)
    c = causal_depthwise_conv(c, conv_w, conv_b)
    c = jax.nn.silu(layer_norm(c, conv_ln_g, conv_ln_b))
    c = c @ w_pw2.astype(c.dtype) + b_pw2.astype(c.dtype)
    c = c * jax.nn.silu(conv_gate)

    heads = lambda t: t.reshape(bsz, seq_len, N_SB_HEADS, SB_HEAD_DIM)
    s = stick_breaking_attention(heads(q), heads(k), heads(v)).reshape(bsz, seq_len, D_SB)
    s = s * jax.nn.silu(sb_gate)

    mixed = jnp.concatenate([c, s], axis=-1) @ w_out.astype(h.dtype)
    return h + rms_norm(mixed, post_g)


def _fwd_setup_inputs(seed: int = 0) -> dict:
    key = jax.random.key(seed)
    ks = jax.random.split(key, 12)
    f32 = jnp.float32
    nrm = lambda k, shape, s: jax.random.normal(k, shape, f32) * s
    return {
        "x": nrm(ks[0], (BATCH, SEQ, D_MODEL), 1.0),
        "meta_tokens": nrm(ks[1], (N_META, D_MODEL), 1.0),
        "pre_norm_g": 1.0 + nrm(ks[2], (DEPTH, D_MODEL), 0.02),
        "post_norm_g": 1.0 + nrm(ks[3], (DEPTH, D_MODEL), 0.02),
        "w_in": nrm(ks[4], (DEPTH, D_MODEL, D_IN_PROJ), D_MODEL ** -0.5),
        "conv_w": nrm(ks[5], (DEPTH, CONV_WIDTH, D_CONV), CONV_WIDTH ** -0.5),
        "conv_b": nrm(ks[6], (DEPTH, D_CONV), 0.02),
        "conv_ln_g": 1.0 + nrm(ks[7], (DEPTH, D_CONV), 0.02),
        "conv_ln_b": nrm(ks[8], (DEPTH, D_CONV), 0.02),
        "w_pw2": nrm(ks[9], (DEPTH, D_CONV, D_CONV), D_CONV ** -0.5),
        "b_pw2": nrm(ks[10], (DEPTH, D_CONV), 0.02),
        "w_out": nrm(ks[11], (DEPTH, D_MIX, D_MODEL), D_MIX ** -0.5),
    }


def _fwd_reference(x, meta_tokens, pre_norm_g, post_norm_g, w_in, conv_w, conv_b, conv_ln_g,
              conv_ln_b, w_pw2, b_pw2, w_out):
    bsz = x.shape[0]
    meta = jnp.broadcast_to(meta_tokens.astype(x.dtype)[None], (bsz, N_META, D_MODEL))
    h = jnp.concatenate([meta, x], axis=1)
    for l in range(DEPTH):
        h = hybrid_layer(h, pre_norm_g[l], post_norm_g[l], w_in[l], conv_w[l], conv_b[l],
                         conv_ln_g[l], conv_ln_b[l], w_pw2[l], b_pw2[l], w_out[l])
    return h[:, N_META:]


import jax as _jax
import jax.numpy as _jnp

TWIN_FORMAT = 'train_step'
FWD_PARAMS = ['x', 'meta_tokens', 'pre_norm_g', 'post_norm_g', 'w_in', 'conv_w', 'conv_b', 'conv_ln_g', 'conv_ln_b', 'w_pw2', 'b_pw2', 'w_out']
TWIN_WEIGHTS = ['meta_tokens', 'pre_norm_g', 'post_norm_g', 'w_in', 'conv_w', 'conv_b', 'conv_ln_g', 'conv_ln_b', 'w_pw2', 'b_pw2', 'w_out']
TWIN_DIFF_INPUT = 'x'
TWIN_INPUTS = ['x', 'meta_tokens', 'pre_norm_g', 'post_norm_g', 'w_in', 'conv_w', 'conv_b', 'conv_ln_g', 'conv_ln_b', 'w_pw2', 'b_pw2', 'w_out', 'loss_target', 'm_meta_tokens', 'm_pre_norm_g', 'm_post_norm_g', 'm_w_in', 'm_conv_w', 'm_conv_b', 'm_conv_ln_g', 'm_conv_ln_b', 'm_w_pw2', 'm_b_pw2', 'm_w_out', 'v_meta_tokens', 'v_pre_norm_g', 'v_post_norm_g', 'v_w_in', 'v_conv_w', 'v_conv_b', 'v_conv_ln_g', 'v_conv_ln_b', 'v_w_pw2', 'v_b_pw2', 'v_w_out']
TWIN_OUTPUTS = ['loss', 'grad_x', 'grad_meta_tokens', 'grad_pre_norm_g', 'grad_post_norm_g', 'grad_w_in', 'grad_conv_w', 'grad_conv_b', 'grad_conv_ln_g', 'grad_conv_ln_b', 'grad_w_pw2', 'grad_b_pw2', 'grad_w_out', 'delta_meta_tokens', 'delta_pre_norm_g', 'delta_post_norm_g', 'delta_w_in', 'delta_conv_w', 'delta_conv_b', 'delta_conv_ln_g', 'delta_conv_ln_b', 'delta_w_pw2', 'delta_b_pw2', 'delta_w_out', 'new_m_meta_tokens', 'new_m_pre_norm_g', 'new_m_post_norm_g', 'new_m_w_in', 'new_m_conv_w', 'new_m_conv_b', 'new_m_conv_ln_g', 'new_m_conv_ln_b', 'new_m_w_pw2', 'new_m_b_pw2', 'new_m_w_out', 'new_v_meta_tokens', 'new_v_pre_norm_g', 'new_v_post_norm_g', 'new_v_w_in', 'new_v_conv_w', 'new_v_conv_b', 'new_v_conv_ln_g', 'new_v_conv_ln_b', 'new_v_w_pw2', 'new_v_b_pw2', 'new_v_w_out']
TWIN_LEAF_KINDS = {'loss': 'loss', 'grad_x': 'grad_x', 'grad_meta_tokens': 'grad_w', 'grad_pre_norm_g': 'grad_w', 'grad_post_norm_g': 'grad_w', 'grad_w_in': 'grad_w', 'grad_conv_w': 'grad_w', 'grad_conv_b': 'grad_w', 'grad_conv_ln_g': 'grad_w', 'grad_conv_ln_b': 'grad_w', 'grad_w_pw2': 'grad_w', 'grad_b_pw2': 'grad_w', 'grad_w_out': 'grad_w', 'delta_meta_tokens': 'delta_w', 'delta_pre_norm_g': 'delta_w', 'delta_post_norm_g': 'delta_w', 'delta_w_in': 'delta_w', 'delta_conv_w': 'delta_w', 'delta_conv_b': 'delta_w', 'delta_conv_ln_g': 'delta_w', 'delta_conv_ln_b': 'delta_w', 'delta_w_pw2': 'delta_w', 'delta_b_pw2': 'delta_w', 'delta_w_out': 'delta_w', 'new_m_meta_tokens': 'new_m', 'new_m_pre_norm_g': 'new_m', 'new_m_post_norm_g': 'new_m', 'new_m_w_in': 'new_m', 'new_m_conv_w': 'new_m', 'new_m_conv_b': 'new_m', 'new_m_conv_ln_g': 'new_m', 'new_m_conv_ln_b': 'new_m', 'new_m_w_pw2': 'new_m', 'new_m_b_pw2': 'new_m', 'new_m_w_out': 'new_m', 'new_v_meta_tokens': 'new_v', 'new_v_pre_norm_g': 'new_v', 'new_v_post_norm_g': 'new_v', 'new_v_w_in': 'new_v', 'new_v_conv_w': 'new_v', 'new_v_conv_b': 'new_v', 'new_v_conv_ln_g': 'new_v', 'new_v_conv_ln_b': 'new_v', 'new_v_w_pw2': 'new_v', 'new_v_b_pw2': 'new_v', 'new_v_w_out': 'new_v'}


def _forward(args):
    return _fwd_reference(*[args[k] for k in FWD_PARAMS])


def _output_shape():
    def fwd():
        inp = _fwd_setup_inputs(0)
        return _fwd_reference(*[inp[k] for k in FWD_PARAMS])
    out = _jax.eval_shape(fwd)
    return out.shape, out.dtype

N_MICROBATCH = 1
ADAM_LR = 0.001
ADAM_B1 = 0.9
ADAM_B2 = 0.999
ADAM_EPS = 1e-08
ADAM_WD = 0.01
ADAM_STEP = 10
PER_EXAMPLE_BATCH_AXIS = {'x': 0, 'loss_target': 0}
SHARED_INPUTS = []
_WEIGHT_DTYPES = {'meta_tokens': _jnp.float32, 'pre_norm_g': _jnp.float32, 'post_norm_g': _jnp.float32, 'w_in': _jnp.float32, 'conv_w': _jnp.float32, 'conv_b': _jnp.float32, 'conv_ln_g': _jnp.float32, 'conv_ln_b': _jnp.float32, 'w_pw2': _jnp.float32, 'b_pw2': _jnp.float32, 'w_out': _jnp.float32}
MOMENT_SCALE = {'meta_tokens': 8.123068e-02, 'pre_norm_g': 1.086757e+00, 'post_norm_g': 3.196771e+01, 'w_in': 5.869759e-01, 'conv_w': 6.992002e-01, 'conv_b': 3.202279e+00, 'conv_ln_g': 1.304474e+00, 'conv_ln_b': 1.779560e+00, 'w_pw2': 8.993278e-01, 'b_pw2': 3.621994e+00, 'w_out': 8.101478e-01}


def _to_microbatches(a, axis):
    t = _jnp.moveaxis(a, axis, 0)
    t = t.reshape((N_MICROBATCH, t.shape[0] // N_MICROBATCH) + t.shape[1:])
    return _jnp.moveaxis(t, 1, axis + 1)


def setup_inputs(seed: int = 0) -> dict:
    inp = _fwd_setup_inputs(seed)
    key = _jax.random.fold_in(_jax.random.key(seed), 7919)
    shape, _ = _output_shape()
    out = dict(inp)
    out["loss_target"] = _jax.random.normal(_jax.random.fold_in(key, 0), shape, _jnp.float32)
    for i, name in enumerate(TWIN_WEIGHTS):
        w = inp[name].astype(_jnp.float32)
        if MOMENT_SCALE is None:
            s = _jnp.sqrt(_jnp.mean(_jnp.square(w)) + 1e-30)
        else:
            s = MOMENT_SCALE[name]
        km, kv = _jax.random.split(_jax.random.fold_in(key, i + 1))
        out[name] = w
        out["m_" + name] = s * _jax.random.normal(km, w.shape, _jnp.float32)
        out["v_" + name] = (s * s) * _jax.random.uniform(kv, w.shape, _jnp.float32, 0.5, 1.5)
    if N_MICROBATCH > 1:
        for name, axis in PER_EXAMPLE_BATCH_AXIS.items():
            out[name] = _to_microbatches(out[name], axis)
    return {'x': out['x'], 'meta_tokens': out['meta_tokens'], 'pre_norm_g': out['pre_norm_g'], 'post_norm_g': out['post_norm_g'], 'w_in': out['w_in'], 'conv_w': out['conv_w'], 'conv_b': out['conv_b'], 'conv_ln_g': out['conv_ln_g'], 'conv_ln_b': out['conv_ln_b'], 'w_pw2': out['w_pw2'], 'b_pw2': out['b_pw2'], 'w_out': out['w_out'], 'loss_target': out['loss_target'], 'm_meta_tokens': out['m_meta_tokens'], 'm_pre_norm_g': out['m_pre_norm_g'], 'm_post_norm_g': out['m_post_norm_g'], 'm_w_in': out['m_w_in'], 'm_conv_w': out['m_conv_w'], 'm_conv_b': out['m_conv_b'], 'm_conv_ln_g': out['m_conv_ln_g'], 'm_conv_ln_b': out['m_conv_ln_b'], 'm_w_pw2': out['m_w_pw2'], 'm_b_pw2': out['m_b_pw2'], 'm_w_out': out['m_w_out'], 'v_meta_tokens': out['v_meta_tokens'], 'v_pre_norm_g': out['v_pre_norm_g'], 'v_post_norm_g': out['v_post_norm_g'], 'v_w_in': out['v_w_in'], 'v_conv_w': out['v_conv_w'], 'v_conv_b': out['v_conv_b'], 'v_conv_ln_g': out['v_conv_ln_g'], 'v_conv_ln_b': out['v_conv_ln_b'], 'v_w_pw2': out['v_w_pw2'], 'v_b_pw2': out['v_b_pw2'], 'v_w_out': out['v_w_out']}


def _loss(weights, diff, rest, loss_target):
    with _jax.named_scope("forward"):
        args = {**rest, TWIN_DIFF_INPUT: diff, **{k: w.astype(_WEIGHT_DTYPES[k]) for k, w in weights.items()}}
        y = _forward(args)
    with _jax.named_scope("loss_head"):
        err = _jnp.square(y.astype(_jnp.float32) - loss_target)
        return 0.5 * _jnp.sum(_jnp.mean(err, axis=-1)) if err.ndim else 0.5 * err


def _adamw(w, g, m, v):
    m = ADAM_B1 * m + (1.0 - ADAM_B1) * g
    v = ADAM_B2 * v + (1.0 - ADAM_B2) * _jnp.square(g)
    m_hat = m / (1.0 - ADAM_B1 ** ADAM_STEP)
    v_hat = v / (1.0 - ADAM_B2 ** ADAM_STEP)
    delta = -ADAM_LR * (m_hat / (_jnp.sqrt(v_hat) + ADAM_EPS) + ADAM_WD * w)
    return delta, m, v


def reference(x, meta_tokens, pre_norm_g, post_norm_g, w_in, conv_w, conv_b, conv_ln_g, conv_ln_b, w_pw2, b_pw2, w_out, loss_target, m_meta_tokens, m_pre_norm_g, m_post_norm_g, m_w_in, m_conv_w, m_conv_b, m_conv_ln_g, m_conv_ln_b, m_w_pw2, m_b_pw2, m_w_out, v_meta_tokens, v_pre_norm_g, v_post_norm_g, v_w_in, v_conv_w, v_conv_b, v_conv_ln_g, v_conv_ln_b, v_w_pw2, v_b_pw2, v_w_out):
    given = dict(x=x, meta_tokens=meta_tokens, pre_norm_g=pre_norm_g, post_norm_g=post_norm_g, w_in=w_in, conv_w=conv_w, conv_b=conv_b, conv_ln_g=conv_ln_g, conv_ln_b=conv_ln_b, w_pw2=w_pw2, b_pw2=b_pw2, w_out=w_out, loss_target=loss_target, m_meta_tokens=m_meta_tokens, m_pre_norm_g=m_pre_norm_g, m_post_norm_g=m_post_norm_g, m_w_in=m_w_in, m_conv_w=m_conv_w, m_conv_b=m_conv_b, m_conv_ln_g=m_conv_ln_g, m_conv_ln_b=m_conv_ln_b, m_w_pw2=m_w_pw2, m_b_pw2=m_b_pw2, m_w_out=m_w_out, v_meta_tokens=v_meta_tokens, v_pre_norm_g=v_pre_norm_g, v_post_norm_g=v_post_norm_g, v_w_in=v_w_in, v_conv_w=v_conv_w, v_conv_b=v_conv_b, v_conv_ln_g=v_conv_ln_g, v_conv_ln_b=v_conv_ln_b, v_w_pw2=v_w_pw2, v_b_pw2=v_b_pw2, v_w_out=v_w_out)
    weights = {n: given[n] for n in TWIN_WEIGHTS}
    shared = {n: given[n] for n in SHARED_INPUTS}
    per_example = {n: given[n] for n in ['x']}
    grad_fn = _jax.value_and_grad(_loss, argnums=(0, 1))

    def one_microbatch(ex, loss_target):
        ex = dict(ex)
        diff = ex.pop(TWIN_DIFF_INPUT)
        return grad_fn(weights, diff, {**shared, **ex}, loss_target)

    if N_MICROBATCH == 1:
        loss, (grad_w, grad_x) = one_microbatch(per_example, given["loss_target"])
    else:
        def body(carry, xs):
            loss_sum, grad_sum = carry
            l_k, (gw_k, gx_k) = one_microbatch(xs[0], xs[1])
            with _jax.named_scope("update"):
                return (loss_sum + l_k, _jax.tree.map(_jnp.add, grad_sum, gw_k)), gx_k

        init = (_jnp.zeros((), _jnp.float32), _jax.tree.map(_jnp.zeros_like, weights))
        (loss, grad_w), grad_x = _jax.lax.scan(body, init, (per_example, given["loss_target"]))
    with _jax.named_scope("update"):
        delta_w, new_m, new_v = {}, {}, {}
        for n in TWIN_WEIGHTS:
            delta_w[n], new_m[n], new_v[n] = _adamw(weights[n], grad_w[n], given["m_" + n], given["v_" + n])
    return (loss, grad_x, *[grad_w[n] for n in TWIN_WEIGHTS], *[delta_w[n] for n in TWIN_WEIGHTS],
            *[new_m[n] for n in TWIN_WEIGHTS], *[new_v[n] for n in TWIN_WEIGHTS])
```

```python
import functools

import jax
import jax.numpy as jnp
from jax import lax
from jax.experimental import pallas as pl
from jax.experimental.pallas import tpu as pltpu

F32 = jnp.float32
BF16 = jnp.bfloat16

D_MODEL = 1024
D_CONV = 512
D_SB = 512
HEAD_DIM = 64
HEADS_PER_BLOCK = 2
CONV_WIDTH = 31
N_META = 16
D_IN = 3 * D_CONV + 4 * D_SB
RMS_EPS = 1e-6
LN_EPS = 1e-5
SB_SCALE = HEAD_DIM ** -0.5

ADAM_LR = 0.001
ADAM_B1 = 0.9
ADAM_B2 = 0.999
ADAM_EPS = 1e-08
ADAM_WD = 0.01
ADAM_STEP = 10

N_DEV = 8
LANES = 128
ROW_BLK = 256
HALO = 32
VMEM_LIMIT = 56 * 1024 * 1024
MESH = pl.DeviceIdType.MESH


def _cparams(*sem):
    return pltpu.CompilerParams(dimension_semantics=sem, vmem_limit_bytes=VMEM_LIMIT)


def _rows(n_cols, col=0):
    return pl.BlockSpec((ROW_BLK, n_cols), lambda i, col=col: (i, col))


def _whole(shape):
    return pl.BlockSpec(shape, lambda i: (0,) * len(shape))


def _sigmoid(x):
    return jax.nn.sigmoid(x)


def _dsilu(x, s):
    return s * (1.0 + x * (1.0 - s))


def _dot(a, b):
    return jnp.dot(a, b, preferred_element_type=F32)


def _dot_nt(a, b):
    return lax.dot_general(a, b, (((1,), (1,)), ((), ())), preferred_element_type=F32)


def _dot_tn(a, b):
    return lax.dot_general(a, b, (((0,), (0,)), ((), ())), preferred_element_type=F32)


def _inproj_fwd(h, g_pre, w_in):
    t = h.shape[0]

    def body(h_ref, g_ref, w_ref, pc_ref, qkv_ref, sbg_ref, u_ref):
        x = h_ref[...]
        r = lax.rsqrt(jnp.mean(x * x, axis=-1, keepdims=True) + RMS_EPS)
        u = (x * r * g_ref[...]).astype(BF16)
        u_ref[...] = u
        pc_ref[...] = _dot(u, w_ref[:, 0:1536])
        qkv_ref[...] = _dot(u, w_ref[:, 1536:3072]).astype(BF16)
        sbg_ref[...] = _dot(u, w_ref[:, 3072:3584])

    return pl.pallas_call(
        body, name="inproj_fwd", grid=(t // ROW_BLK,),
        in_specs=[_rows(D_MODEL), _whole((1, D_MODEL)), _whole((D_MODEL, D_IN))],
        out_specs=[_rows(1536), _rows(1536), _rows(D_SB), _rows(D_MODEL)],
        out_shape=[jax.ShapeDtypeStruct((t, 1536), F32), jax.ShapeDtypeStruct((t, 1536), BF16),
                   jax.ShapeDtypeStruct((t, D_SB), F32), jax.ShapeDtypeStruct((t, D_MODEL), BF16)],
        compiler_params=_cparams("parallel"),
    )(h, g_pre, w_in)


def _prev_halo(col):
    per = ROW_BLK // HALO
    return pl.BlockSpec((HALO, D_CONV), lambda i, col=col: (jnp.maximum(i * per - 1, 0), col))


def _fill_glu(buf, i, a_ref, b_ref, ha_ref, hb_ref):
    halo = ha_ref[...] * _sigmoid(hb_ref[...])
    buf[0:HALO, :] = jnp.where(i > 0, halo, 0.0)
    buf[HALO:HALO + ROW_BLK, :] = a_ref[...] * _sigmoid(b_ref[...])


def _layer_norm_stats(cv):
    mu = jnp.mean(cv, axis=-1, keepdims=True)
    xc = cv - mu
    rstd = lax.rsqrt(jnp.mean(xc * xc, axis=-1, keepdims=True) + LN_EPS)
    return xc * rstd, rstd


def _conv_fwd(pc, conv_w, conv_b, ln_g, ln_b, w_pw2, b_pw2):
    t = pc.shape[0]

    def body(a_ref, b_ref, gate_ref, ha_ref, hb_ref, cw_ref, cb_ref, lg_ref, lb_ref, wp_ref, bp_ref,
             cout_ref, cv_ref, p_ref, sl_ref, buf):
        i = pl.program_id(0)
        _fill_glu(buf, i, a_ref, b_ref, ha_ref, hb_ref)
        acc = jnp.zeros((ROW_BLK, D_CONV), F32) + cb_ref[...]
        for j in range(CONV_WIDTH):
            acc = acc + cw_ref[j:j + 1, :] * buf[pl.ds(HALO - (CONV_WIDTH - 1) + j, ROW_BLK), :]
        cv_ref[...] = acc
        xh, _ = _layer_norm_stats(acc)
        ln = xh * lg_ref[...] + lb_ref[...]
        sl = (ln * _sigmoid(ln)).astype(BF16)
        sl_ref[...] = sl
        p = _dot(sl, wp_ref[...]) + bp_ref[...]
        p_ref[...] = p
        gate = gate_ref[...]
        cout_ref[...] = (p * (gate * _sigmoid(gate))).astype(BF16)

    vec = _whole((1, D_CONV))
    return pl.pallas_call(
        body, name="conv_fwd", grid=(t // ROW_BLK,),
        in_specs=[_rows(D_CONV, 0), _rows(D_CONV, 1), _rows(D_CONV, 2), _prev_halo(0), _prev_halo(1),
                  _whole((CONV_WIDTH, D_CONV)), vec, vec, vec, _whole((D_CONV, D_CONV)), vec],
        out_specs=[_rows(D_CONV)] * 4,
        out_shape=[jax.ShapeDtypeStruct((t, D_CONV), BF16), jax.ShapeDtypeStruct((t, D_CONV), F32),
                   jax.ShapeDtypeStruct((t, D_CONV), F32), jax.ShapeDtypeStruct((t, D_CONV), BF16)],
        scratch_shapes=[pltpu.VMEM((HALO + ROW_BLK, D_CONV), F32)],
        compiler_params=_cparams("parallel"),
    )(pc, pc, pc, pc, pc, conv_w, conv_b, ln_g, ln_b, w_pw2, b_pw2)


def _lower_triangle():
    row = lax.broadcasted_iota(jnp.int32, (ROW_BLK, ROW_BLK), 0)
    col = lax.broadcasted_iota(jnp.int32, (ROW_BLK, ROW_BLK), 1)
    return row > col


def _lower_triangle_t():
    row = lax.broadcasted_iota(jnp.int32, (ROW_BLK, ROW_BLK), 0)
    col = lax.broadcasted_iota(jnp.int32, (ROW_BLK, ROW_BLK), 1)
    return row < col


def _tri_sum(x, umat):
    hi = x.astype(BF16)
    lo = (x - hi.astype(F32)).astype(BF16)
    return _dot(hi, umat) + _dot(lo, umat)


def _log_gates(z):
    sp = jnp.log(1.0 + jnp.exp(-jnp.abs(z)))
    return -(jnp.maximum(z, 0.0) + sp), jnp.minimum(z, 0.0) - sp


def _head_lanes(hh):
    lane = lax.broadcasted_iota(jnp.int32, (ROW_BLK, LANES), 1)
    return (lane >= HEAD_DIM * hh) & (lane < HEAD_DIM * (hh + 1))


def _qkv_specs(t):
    n_blk = D_SB // LANES
    return [pl.BlockSpec((ROW_BLK, LANES), lambda hp, i: (i, hp)),
            pl.BlockSpec((t, LANES), lambda hp, i: (0, n_blk + hp)),
            pl.BlockSpec((t, LANES), lambda hp, i: (0, 2 * n_blk + hp))]


def _carry_spec():
    return pl.BlockSpec((HEADS_PER_BLOCK, ROW_BLK, LANES), lambda hp, i: (hp, i, 0))


def _attn_fwd(qkv):
    t = qkv.shape[0]
    assert t // ROW_BLK <= LANES

    def body(q_ref, k_ref, v_ref, o_ref, c_ref, acc_ref):
        i = pl.program_id(1)
        lower = _lower_triangle()
        umat = jnp.where(lower, 1.0, 0.0).astype(BF16)
        lane = lax.broadcasted_iota(jnp.int32, (ROW_BLK, LANES), 1)
        q = q_ref[...]
        for hh in range(HEADS_PER_BLOCK):
            qm = jnp.where(_head_lanes(hh), q, jnp.zeros_like(q)) * jnp.asarray(SB_SCALE, BF16)
            acc_ref[hh] = jnp.zeros((ROW_BLK, LANES), F32)
            c_ref[hh] = jnp.zeros((ROW_BLK, LANES), F32)

            def block(jb, carry, diagonal, qm=qm, hh=hh):
                start = pl.multiple_of(jb * ROW_BLK, ROW_BLK)
                kb = k_ref[pl.ds(start, ROW_BLK), :]
                vb = v_ref[pl.ds(start, ROW_BLK), :]
                ls, lb = _log_gates(_dot_nt(qm, kb))
                if diagonal:
                    ls = jnp.where(lower, ls, 0.0)
                else:
                    c_ref[hh] = jnp.where(lane == jb, carry, c_ref[hh])
                a = jnp.exp(lb + carry + _tri_sum(ls, umat))
                if diagonal:
                    a = jnp.where(lower, a, 0.0)
                acc_ref[hh] += _dot(a.astype(BF16), vb)
                return carry + jnp.sum(ls, axis=1, keepdims=True)

            carry = block(i, jnp.zeros((ROW_BLK, 1), F32), True)
            lax.fori_loop(0, i, lambda n, c: block(i - 1 - n, c, False), carry)
        o_ref[...] = jnp.where(_head_lanes(0), acc_ref[0], acc_ref[1])

    return pl.pallas_call(
        body, name="attn_fwd", grid=(D_SB // LANES, t // ROW_BLK),
        in_specs=_qkv_specs(t),
        out_specs=[pl.BlockSpec((ROW_BLK, LANES), lambda hp, i: (i, hp)), _carry_spec()],
        out_shape=[jax.ShapeDtypeStruct((t, D_SB), F32),
                   jax.ShapeDtypeStruct((D_SB // HEAD_DIM, t, LANES), F32)],
        scratch_shapes=[pltpu.VMEM((HEADS_PER_BLOCK, ROW_BLK, LANES), F32)],
        compiler_params=_cparams("arbitrary", "arbitrary"),
    )(qkv, qkv, qkv)


def _outproj_fwd(h, cout, sraw, sbg, w_out, g_post):
    t = h.shape[0]

    def body(h_ref, c_ref, s_ref, g_ref, w_ref, gp_ref, hn_ref, mixed_ref, mix_ref):
        gate = g_ref[...]
        mix_ref[:, 0:D_CONV] = c_ref[...]
        mix_ref[:, D_CONV:] = (s_ref[...] * (gate * _sigmoid(gate))).astype(BF16)
        mixed = _dot(mix_ref[...], w_ref[...])
        mixed_ref[...] = mixed
        r = lax.rsqrt(jnp.mean(mixed * mixed, axis=-1, keepdims=True) + RMS_EPS)
        hn_ref[...] = h_ref[...] + mixed * r * gp_ref[...]

    return pl.pallas_call(
        body, name="outproj_fwd", grid=(t // ROW_BLK,),
        in_specs=[_rows(D_MODEL), _rows(D_CONV), _rows(D_SB), _rows(D_SB), _whole((D_MODEL, D_MODEL)),
                  _whole((1, D_MODEL))],
        out_specs=[_rows(D_MODEL)] * 3,
        out_shape=[jax.ShapeDtypeStruct((t, D_MODEL), F32), jax.ShapeDtypeStruct((t, D_MODEL), F32),
                   jax.ShapeDtypeStruct((t, D_MODEL), BF16)],
        compiler_params=_cparams("parallel"),
    )(h, cout, sraw, sbg, w_out, g_post)


def _loss_and_grad(h, target, seq):
    t = h.shape[0]

    def body(h_ref, t_ref, loss_ref, dh_ref):
        i = pl.program_id(0)

        @pl.when(i == 0)
        def _():
            loss_ref[...] = jnp.zeros_like(loss_ref)

        row = i * ROW_BLK + lax.broadcasted_iota(jnp.int32, (ROW_BLK, D_MODEL), 0)
        real = (row >= N_META) & (row < N_META + seq)
        diff = jnp.where(real, h_ref[...] - t_ref[...], 0.0)
        sq = jnp.sum(jnp.sum(diff * diff, axis=1, keepdims=True), axis=0, keepdims=True)
        loss_ref[...] += (0.5 / D_MODEL) * sq
        dh_ref[...] = diff * (1.0 / D_MODEL)

    return pl.pallas_call(
        body, name="loss", grid=(t // ROW_BLK,),
        in_specs=[_rows(D_MODEL), _rows(D_MODEL)],
        out_specs=[_whole((1, 1)), _rows(D_MODEL)],
        out_shape=[jax.ShapeDtypeStruct((1, 1), F32), jax.ShapeDtypeStruct((t, D_MODEL), F32)],
        compiler_params=_cparams("arbitrary"),
    )(h, target)


def _outproj_bwd(dh, mixed, g_post, sraw, sbg, w_out_t):
    t = dh.shape[0]

    def body(dh_ref, mixed_ref, gp_ref, s_ref, g_ref, wt_ref, dc_ref, ds_ref, dg_ref, dmb_ref, dgp_ref):
        @pl.when(pl.program_id(0) == 0)
        def _():
            dgp_ref[...] = jnp.zeros_like(dgp_ref)

        mixed = mixed_ref[...]
        r = lax.rsqrt(jnp.mean(mixed * mixed, axis=-1, keepdims=True) + RMS_EPS)
        nh = mixed * r
        dy = dh_ref[...]
        dgp_ref[...] += jnp.sum(dy * nh, axis=0, keepdims=True)
        dn = dy * gp_ref[...]
        dmixed = (r * (dn - nh * jnp.mean(dn * nh, axis=-1, keepdims=True))).astype(BF16)
        dmb_ref[...] = dmixed
        dmix = _dot(dmixed, wt_ref[...])
        dc_ref[...] = dmix[:, 0:D_CONV]
        dsg = dmix[:, D_CONV:]
        gate = g_ref[...]
        sg = _sigmoid(gate)
        ds_ref[...] = dsg * (gate * sg)
        dg_ref[...] = dsg * s_ref[...] * _dsilu(gate, sg)

    return pl.pallas_call(
        body, name="outproj_bwd", grid=(t // ROW_BLK,),
        in_specs=[_rows(D_MODEL), _rows(D_MODEL), _whole((1, D_MODEL)), _rows(D_SB), _rows(D_SB),
                  _whole((D_MODEL, D_MODEL))],
        out_specs=[_rows(D_CONV), _rows(D_SB), _rows(D_SB), _rows(D_MODEL), _whole((1, D_MODEL))],
        out_shape=[jax.ShapeDtypeStruct((t, D_CONV), F32), jax.ShapeDtypeStruct((t, D_SB), F32),
                   jax.ShapeDtypeStruct((t, D_SB), F32), jax.ShapeDtypeStruct((t, D_MODEL), BF16),
                   jax.ShapeDtypeStruct((1, D_MODEL), F32)],
        compiler_params=_cparams("arbitrary"),
    )(dh, mixed, g_post, sraw, sbg, w_out_t)


def _attn_bwd(qkv, carries, do):
    t = qkv.shape[0]

    def body(q_ref, k_ref, v_ref, c_ref, do_ref, dq_ref, dk_ref, dv_ref, acc_ref):
        i = pl.program_id(1)

        @pl.when(i == 0)
        def _():
            dk_ref[...] = jnp.zeros_like(dk_ref)
            dv_ref[...] = jnp.zeros_like(dv_ref)

        lower = _lower_triangle()
        umat = jnp.where(lower, 1.0, 0.0).astype(BF16)
        umat_t = jnp.where(_lower_triangle_t(), 1.0, 0.0).astype(BF16)
        lane = lax.broadcasted_iota(jnp.int32, (ROW_BLK, LANES), 1)
        q = q_ref[...]
        dof = do_ref[...]
        for hh in range(HEADS_PER_BLOCK):
            in_head = _head_lanes(hh)
            qm = jnp.where(in_head, q, jnp.zeros_like(q)) * jnp.asarray(SB_SCALE, BF16)
            dom = jnp.where(in_head, dof, 0.0).astype(BF16)
            acc_ref[hh] = jnp.zeros((ROW_BLK, LANES), F32)

            def block(jb, seen, diagonal, qm=qm, dom=dom, hh=hh):
                start = pl.multiple_of(jb * ROW_BLK, ROW_BLK)
                kb = k_ref[pl.ds(start, ROW_BLK), :]
                vb = v_ref[pl.ds(start, ROW_BLK), :]
                ls, lb = _log_gates(_dot_nt(qm, kb))
                if diagonal:
                    ls = jnp.where(lower, ls, 0.0)
                    tail = _tri_sum(ls, umat)
                else:
                    right = jnp.sum(jnp.where(lane == jb, c_ref[hh], 0.0), axis=1, keepdims=True)
                    tail = right + _tri_sum(ls, umat)
                a = jnp.exp(lb + tail)
                if diagonal:
                    a = jnp.where(lower, a, 0.0)
                g = _dot_nt(dom, vb) * a
                before = seen + _tri_sum(g, umat_t)
                dz = g - jnp.exp(lb) * (g + before)
                if diagonal:
                    dz = jnp.where(lower, dz, 0.0)
                dzb = dz.astype(BF16)
                acc_ref[hh] += _dot(dzb, kb)
                dk_ref[pl.ds(start, ROW_BLK), :] += _dot_tn(dzb, qm)
                dv_ref[pl.ds(start, ROW_BLK), :] += _dot_tn(a.astype(BF16), dom)
                return seen + jnp.sum(g, axis=1, keepdims=True)

            seen = lax.fori_loop(0, i, lambda jb, s: block(jb, s, False), jnp.zeros((ROW_BLK, 1), F32))
            block(i, seen, True)
        dq_ref[...] = jnp.where(_head_lanes(0), acc_ref[0], acc_ref[1]) * SB_SCALE

    blk = pl.BlockSpec((ROW_BLK, LANES), lambda hp, i: (i, hp))
    full = pl.BlockSpec((t, LANES), lambda hp, i: (0, hp))
    return pl.pallas_call(
        body, name="attn_bwd", grid=(D_SB // LANES, t // ROW_BLK),
        in_specs=_qkv_specs(t) + [_carry_spec(), blk],
        out_specs=[blk, full, full],
        out_shape=[jax.ShapeDtypeStruct((t, D_SB), F32)] * 3,
        scratch_shapes=[pltpu.VMEM((HEADS_PER_BLOCK, ROW_BLK, LANES), F32)],
        compiler_params=_cparams("arbitrary", "arbitrary"),
    )(qkv, qkv, qkv, carries, do)


def _conv_bwd_rows(dcout, pc, cv, p, ln_g, ln_b, w_pw2_t):
    t = dcout.shape[0]

    def body(dc_ref, gate_ref, cv_ref, p_ref, lg_ref, lb_ref, wt_ref, dcv_ref, dgate_ref, dpb_ref, vec_ref):
        @pl.when(pl.program_id(0) == 0)
        def _():
            vec_ref[...] = jnp.zeros_like(vec_ref)

        dc = dc_ref[...]
        gate = gate_ref[...]
        sg = _sigmoid(gate)
        dp = dc * (gate * sg)
        dgate_ref[...] = dc * p_ref[...] * _dsilu(gate, sg)
        dpb = dp.astype(BF16)
        dpb_ref[...] = dpb
        xh, rstd = _layer_norm_stats(cv_ref[...])
        ln = xh * lg_ref[...] + lb_ref[...]
        s2 = _sigmoid(ln)
        dln = _dot(dpb, wt_ref[...]) * _dsilu(ln, s2)
        dxh = dln * lg_ref[...]
        dcv = rstd * (dxh - jnp.mean(dxh, axis=-1, keepdims=True)
                      - xh * jnp.mean(dxh * xh, axis=-1, keepdims=True))
        dcv_ref[...] = dcv
        vec_ref[0:1, :] += jnp.sum(dp, axis=0, keepdims=True)
        vec_ref[1:2, :] += jnp.sum(dln * xh, axis=0, keepdims=True)
        vec_ref[2:3, :] += jnp.sum(dln, axis=0, keepdims=True)
        vec_ref[3:4, :] += jnp.sum(dcv, axis=0, keepdims=True)

    vec = _whole((1, D_CONV))
    return pl.pallas_call(
        body, name="conv_bwd_rows", grid=(t // ROW_BLK,),
        in_specs=[_rows(D_CONV), _rows(D_CONV, 2), _rows(D_CONV), _rows(D_CONV), vec, vec,
                  _whole((D_CONV, D_CONV))],
        out_specs=[_rows(D_CONV), _rows(D_CONV), _rows(D_CONV), _whole((8, D_CONV))],
        out_shape=[jax.ShapeDtypeStruct((t, D_CONV), F32), jax.ShapeDtypeStruct((t, D_CONV), F32),
                   jax.ShapeDtypeStruct((t, D_CONV), BF16), jax.ShapeDtypeStruct((8, D_CONV), F32)],
        compiler_params=_cparams("arbitrary"),
    )(dcout, pc, cv, p, ln_g, ln_b, w_pw2_t)


def _conv_bwd_taps(dcv, pc, conv_w):
    t = dcv.shape[0]
    n_halo = t // HALO
    per = ROW_BLK // HALO

    def body(d_ref, dn_ref, a_ref, b_ref, ha_ref, hb_ref, cw_ref, da_ref, db_ref, dw_ref, cbuf, dbuf):
        i = pl.program_id(0)

        @pl.when(i == 0)
        def _():
            dw_ref[...] = jnp.zeros_like(dw_ref)

        _fill_glu(cbuf, i, a_ref, b_ref, ha_ref, hb_ref)
        dcv = d_ref[...]
        dbuf[0:ROW_BLK, :] = dcv
        dbuf[ROW_BLK:ROW_BLK + HALO, :] = jnp.where(i < pl.num_programs(0) - 1, dn_ref[...], 0.0)
        acc = jnp.zeros((ROW_BLK, D_CONV), F32)
        for j in range(CONV_WIDTH):
            acc = acc + cw_ref[j:j + 1, :] * dbuf[pl.ds(CONV_WIDTH - 1 - j, ROW_BLK), :]
            seen = cbuf[pl.ds(HALO - (CONV_WIDTH - 1) + j, ROW_BLK), :]
            dw_ref[j:j + 1, :] += jnp.sum(dcv * seen, axis=0, keepdims=True)
        a = a_ref[...]
        sb = _sigmoid(b_ref[...])
        da_ref[...] = acc * sb
        db_ref[...] = acc * a * sb * (1.0 - sb)

    return pl.pallas_call(
        body, name="conv_bwd_taps", grid=(t // ROW_BLK,),
        in_specs=[_rows(D_CONV),
                  pl.BlockSpec((HALO, D_CONV), lambda i: (jnp.minimum((i + 1) * per, n_halo - 1), 0)),
                  _rows(D_CONV, 0), _rows(D_CONV, 1), _prev_halo(0), _prev_halo(1),
                  _whole((CONV_WIDTH, D_CONV))],
        out_specs=[_rows(D_CONV), _rows(D_CONV), _whole((32, D_CONV))],
        out_shape=[jax.ShapeDtypeStruct((t, D_CONV), F32), jax.ShapeDtypeStruct((t, D_CONV), F32),
                   jax.ShapeDtypeStruct((32, D_CONV), F32)],
        scratch_shapes=[pltpu.VMEM((HALO + ROW_BLK, D_CONV), F32), pltpu.VMEM((ROW_BLK + HALO, D_CONV), F32)],
        compiler_params=_cparams("arbitrary"),
    )(dcv, dcv, pc, pc, pc, pc, conv_w)


def _inproj_bwd(dh_out, h, g_pre, pieces, w_in_t):
    t = h.shape[0]

    def body(dh_ref, h_ref, g_ref, *rest):
        piece_refs, (wt_ref, dhin_ref, dproj_ref, dg_ref) = rest[:7], rest[7:]

        @pl.when(pl.program_id(0) == 0)
        def _():
            dg_ref[...] = jnp.zeros_like(dg_ref)

        for k, ref in enumerate(piece_refs):
            dproj_ref[:, 512 * k:512 * (k + 1)] = ref[...].astype(BF16)
        du = _dot(dproj_ref[...], wt_ref[...])
        x = h_ref[...]
        r = lax.rsqrt(jnp.mean(x * x, axis=-1, keepdims=True) + RMS_EPS)
        xh = x * r
        dg_ref[...] += jnp.sum(du * xh, axis=0, keepdims=True)
        dxh = du * g_ref[...]
        dhin_ref[...] = dh_ref[...] + r * (dxh - xh * jnp.mean(dxh * xh, axis=-1, keepdims=True))

    return pl.pallas_call(
        body, name="inproj_bwd", grid=(t // ROW_BLK,),
        in_specs=[_rows(D_MODEL), _rows(D_MODEL), _whole((1, D_MODEL))] + [_rows(512)] * 7
                 + [_whole((D_IN, D_MODEL))],
        out_specs=[_rows(D_MODEL), _rows(D_IN), _whole((1, D_MODEL))],
        out_shape=[jax.ShapeDtypeStruct((t, D_MODEL), F32), jax.ShapeDtypeStruct((t, D_IN), BF16),
                   jax.ShapeDtypeStruct((1, D_MODEL), F32)],
        compiler_params=_cparams("arbitrary"),
    )(dh_out, h, g_pre, *pieces, w_in_t)


def _weight_grad(xb, dyb, name):
    t, k = xb.shape
    n = dyb.shape[1]
    tn = min(n, 512)

    def body(x_ref, dy_ref, o_ref):
        @pl.when(pl.program_id(1) == 0)
        def _():
            o_ref[...] = jnp.zeros_like(o_ref)

        o_ref[...] += _dot_tn(x_ref[...], dy_ref[...])

    return pl.pallas_call(
        body, name=name, grid=(n // tn, t // ROW_BLK),
        in_specs=[pl.BlockSpec((ROW_BLK, k), lambda j, i: (i, 0)), pl.BlockSpec((ROW_BLK, tn), lambda j, i: (i, j))],
        out_specs=pl.BlockSpec((k, tn), lambda j, i: (0, j)),
        out_shape=jax.ShapeDtypeStruct((k, n), F32),
        compiler_params=_cparams("parallel", "arbitrary"),
    )(xb, dyb)


def _position():
    return lax.axis_index("x"), lax.axis_index("y"), lax.axis_index("c")


def _all_gather(block, name):
    def body(x_ref, out_ref, send_sems, recv_sems, local_sem):
        x, y, c = _position()
        me, sibling = (x, y, c), (x, y, 1 - c)
        chips = [(1 - x, y), (x, 1 - y), (1 - x, 1 - y)]

        def slot(px, py, pc):
            return out_ref.at[4 * px + 2 * py + pc]

        def copy(k, origin, to, src=None):
            return pltpu.make_async_remote_copy(
                src_ref=slot(*origin) if src is None else src, dst_ref=slot(*origin),
                send_sem=send_sems.at[k], recv_sem=recv_sems.at[k], device_id=to, device_id_type=MESH)

        mine = pltpu.make_async_copy(x_ref, slot(*me), local_sem)
        mine.start()
        first = [copy(0, me, sibling, src=x_ref)]
        first += [copy(1 + j, me, (*chip, c), src=x_ref) for j, chip in enumerate(chips)]
        for cp in first:
            cp.start()
        passed = [copy(4 + j, (*chip, c), sibling) for j, chip in enumerate(chips)]
        for j, chip in enumerate(chips):
            copy(1 + j, (*chip, c), me).wait_recv()
            passed[j].start()
        copy(0, sibling, me).wait_recv()
        for j, chip in enumerate(chips):
            copy(4 + j, (*chip, 1 - c), me).wait_recv()
        for cp in first + passed:
            cp.wait_send()
        mine.wait()

    hbm = pl.BlockSpec(memory_space=pltpu.HBM)
    return pl.pallas_call(
        body, name=name, in_specs=[hbm], out_specs=hbm,
        out_shape=jax.ShapeDtypeStruct((N_DEV,) + block.shape, block.dtype),
        scratch_shapes=[pltpu.SemaphoreType.DMA((7,)), pltpu.SemaphoreType.DMA((7,)), pltpu.SemaphoreType.DMA(())],
    )(block)


def _all_to_all(slabs, name):
    def body(g_ref, land_ref, send_sems, recv_sems, local_sem):
        x, y, c = _position()
        me = 4 * x + 2 * y + c
        mine = pltpu.make_async_copy(g_ref.at[me], land_ref.at[me], local_sem)
        mine.start()
        peers = []
        for k in range(1, N_DEV):
            px = 1 - x if k & 4 else x
            py = 1 - y if k & 2 else y
            pc = 1 - c if k & 1 else c
            peers.append((k - 1, (px, py, pc), 4 * px + 2 * py + pc))
        sends = [pltpu.make_async_remote_copy(
            src_ref=g_ref.at[idx], dst_ref=land_ref.at[me], send_sem=send_sems.at[s], recv_sem=recv_sems.at[s],
            device_id=dev, device_id_type=MESH) for s, dev, idx in peers]
        for cp in sends:
            cp.start()
        for s, dev, idx in peers:
            pltpu.make_async_remote_copy(
                src_ref=g_ref.at[idx], dst_ref=land_ref.at[idx], send_sem=send_sems.at[s], recv_sem=recv_sems.at[s],
                device_id=dev, device_id_type=MESH).wait_recv()
        for cp in sends:
            cp.wait_send()
        mine.wait()

    hbm = pl.BlockSpec(memory_space=pltpu.HBM)
    return pl.pallas_call(
        body, name=name, in_specs=[hbm], out_specs=hbm,
        out_shape=jax.ShapeDtypeStruct(slabs.shape, slabs.dtype),
        scratch_shapes=[pltpu.SemaphoreType.DMA((7,)), pltpu.SemaphoreType.DMA((7,)), pltpu.SemaphoreType.DMA(())],
    )(slabs)


def _block_rows(r, cap=1024):
    return max(d for d in range(8, min(r, cap) + 1, 8) if r % d == 0)


def _sum_adamw(parts, w, m, v, name):
    n_parts, r, _ = parts.shape
    br = _block_rows(r)

    def body(p_ref, w_ref, m_ref, v_ref, g_out, d_out, m_out, v_out):
        g = p_ref[0]
        for s in range(1, n_parts):
            g = g + p_ref[s]
        m_new = ADAM_B1 * m_ref[...] + (1.0 - ADAM_B1) * g
        v_new = ADAM_B2 * v_ref[...] + (1.0 - ADAM_B2) * (g * g)
        m_hat = m_new / (1.0 - ADAM_B1 ** ADAM_STEP)
        v_hat = v_new / (1.0 - ADAM_B2 ** ADAM_STEP)
        g_out[...] = g
        d_out[...] = -ADAM_LR * (m_hat / (jnp.sqrt(v_hat) + ADAM_EPS) + ADAM_WD * w_ref[...])
        m_out[...] = m_new
        v_out[...] = v_new

    row = pl.BlockSpec((br, LANES), lambda i: (i, 0))
    return pl.pallas_call(
        body, name=name, grid=(r // br,),
        in_specs=[pl.BlockSpec((n_parts, br, LANES), lambda i: (0, i, 0)), row, row, row],
        out_specs=[row] * 4, out_shape=[jax.ShapeDtypeStruct((r, LANES), F32)] * 4,
        compiler_params=_cparams("parallel"),
    )(parts, w, m, v)


def _pack(arrays, lead=()):
    n_lead = len(lead)
    flat = jnp.concatenate([a.reshape(lead + (-1,)) for a in arrays], axis=n_lead)
    pad = -flat.shape[-1] % (16 * LANES)
    if pad:
        flat = jnp.pad(flat, [(0, 0)] * n_lead + [(0, pad)])
    return flat.reshape(lead + (-1, LANES))


def _unpack(buf, shapes, lead=()):
    flat = buf.reshape(lead + (-1,))
    out, at = [], 0
    for shape in shapes:
        size = 1
        for d in shape:
            size *= d
        out.append(lax.slice_in_dim(flat, at, at + size, axis=len(lead)).reshape(lead + tuple(shape)))
        at += size
    return out


def _local_step(x, target, meta, pre_g, post_g, w_in, conv_w, conv_b, ln_g, ln_b, w_pw2, b_pw2, w_out):
    depth = w_in.shape[0]
    seq = x.shape[0]
    t = -(-(N_META + seq) // ROW_BLK) * ROW_BLK
    tail = t - N_META - seq
    h = jnp.concatenate([meta, x, jnp.zeros((tail, D_MODEL), F32)], axis=0)
    target = jnp.pad(target, ((N_META, tail), (0, 0)))
    w_in_t = jnp.swapaxes(w_in, 1, 2)
    w_pw2_t = jnp.swapaxes(w_pw2, 1, 2)
    w_out_t = jnp.swapaxes(w_out, 1, 2)
    row = lambda a, l: a[l][None, :]

    saved = []
    for l in range(depth):
        pc, qkv, sbg, u = _inproj_fwd(h, row(pre_g, l), w_in[l])
        cout, cv, p, sl = _conv_fwd(pc, conv_w[l], row(conv_b, l), row(ln_g, l), row(ln_b, l), w_pw2[l],
                                    row(b_pw2, l))
        sraw, carries = _attn_fwd(qkv)
        h_new, mixed, mix = _outproj_fwd(h, cout, sraw, sbg, w_out[l], row(post_g, l))
        saved.append((h, pc, qkv, sbg, u, cv, p, sl, sraw, carries, mixed, mix))
        h = h_new

    loss, dh = _loss_and_grad(h, target, seq)

    grads = {k: [None] * depth for k in ("pre_g", "post_g", "w_in", "conv_w", "conv_b", "ln_g", "ln_b", "w_pw2",
                                          "b_pw2", "w_out")}
    for l in reversed(range(depth)):
        h_in, pc, qkv, sbg, u, cv, p, sl, sraw, carries, mixed, mix = saved[l]
        dcout, dsraw, dsbg, dmixed, dg_post = _outproj_bwd(dh, mixed, row(post_g, l), sraw, sbg, w_out_t[l])
        dq, dk, dv = _attn_bwd(qkv, carries, dsraw)
        dcv, dgate, dpb, vecs = _conv_bwd_rows(dcout, pc, cv, p, row(ln_g, l), row(ln_b, l), w_pw2_t[l])
        da, db, dconv_w = _conv_bwd_taps(dcv, pc, conv_w[l])
        dh, dproj, dg_pre = _inproj_bwd(dh, h_in, row(pre_g, l), (da, db, dgate, dq, dk, dv, dsbg), w_in_t[l])
        grads["w_in"][l] = _weight_grad(u, dproj, "w_in_grad")
        grads["w_out"][l] = _weight_grad(mix, dmixed, "w_out_grad")
        grads["w_pw2"][l] = _weight_grad(sl, dpb, "w_pw2_grad")
        grads["pre_g"][l] = dg_pre[0]
        grads["post_g"][l] = dg_post[0]
        grads["b_pw2"][l], grads["ln_g"][l], grads["ln_b"][l], grads["conv_b"][l] = vecs[0], vecs[1], vecs[2], vecs[3]
        grads["conv_w"][l] = dconv_w[:CONV_WIDTH]

    grads = {k: jnp.stack(v) for k, v in grads.items()}
    grads["meta"] = dh[:N_META]
    return loss[0, 0], dh[N_META:N_META + seq], grads


def _shard_major(full, axis):
    shape = full.shape
    split = full.reshape(shape[:axis] + (N_DEV, shape[axis] // N_DEV) + shape[axis + 1:])
    return jnp.moveaxis(split, axis, 0)


def _whole_from_shards(shards, axis):
    moved = jnp.moveaxis(shards, 0, axis)
    shape = moved.shape
    return moved.reshape(shape[:axis] + (shape[axis] * shape[axis + 1],) + shape[axis + 2:])


def kernel(x, meta_tokens, pre_norm_g, post_norm_g, w_in, conv_w, conv_b, conv_ln_g, conv_ln_b, w_pw2, b_pw2, w_out, loss_target, m_meta_tokens, m_pre_norm_g, m_post_norm_g, m_w_in, m_conv_w, m_conv_b, m_conv_ln_g, m_conv_ln_b, m_w_pw2, m_b_pw2, m_w_out, v_meta_tokens, v_pre_norm_g, v_post_norm_g, v_w_in, v_conv_w, v_conv_b, v_conv_ln_g, v_conv_ln_b, v_w_pw2, v_b_pw2, v_w_out):
    me = 4 * lax.axis_index("x") + 2 * lax.axis_index("y") + lax.axis_index("c")
    lead = (N_DEV,)

    big = (w_in, w_out, w_pw2)
    big_shapes = [a.shape for a in big]
    gathered = _all_gather(_pack([a.astype(BF16) for a in big]), "gather_matmul_weights")
    w_in_s, w_out_s, w_pw2_s = _unpack(gathered, big_shapes, lead)
    w_in_full = _whole_from_shards(w_in_s, 2)
    w_out_full = _whole_from_shards(w_out_s, 1)
    w_pw2_full = _whole_from_shards(w_pw2_s, 1)
    gathered = _all_gather(_pack([conv_w, meta_tokens]), "gather_f32_weights")
    conv_w_s, meta_s = _unpack(gathered, [conv_w.shape, meta_tokens.shape], lead)
    conv_w_full = _whole_from_shards(conv_w_s, 2)
    meta_full = _whole_from_shards(meta_s, 1)

    loss, dx, grads = _local_step(x[0], loss_target[0], meta_full, pre_norm_g, post_norm_g, w_in_full, conv_w_full,
                                  conv_b, conv_ln_g, conv_ln_b, w_pw2_full, b_pw2, w_out_full)
    loss = lax.psum(loss, ("x", "y", "c"))

    slabs = _pack([_shard_major(grads["w_in"], 2), _shard_major(grads["w_out"], 1), _shard_major(grads["w_pw2"], 1)],
                  lead)
    landed = _all_to_all(slabs, "exchange_matmul_grads")
    outs = _sum_adamw(landed, _pack(big), _pack([m_w_in, m_w_out, m_w_pw2]), _pack([v_w_in, v_w_out, v_w_pw2]),
                      "adamw_matmul_weights")
    (g_w_in, g_w_out, g_w_pw2), (d_w_in, d_w_out, d_w_pw2), (nm_w_in, nm_w_out, nm_w_pw2), \
        (nv_w_in, nv_w_out, nv_w_pw2) = [_unpack(o, big_shapes) for o in outs]

    small_names = ("pre_g", "post_g", "conv_b", "ln_g", "ln_b", "b_pw2", "conv_w", "meta")
    small_full = [grads[k] for k in small_names]
    gathered = _all_gather(_pack(small_full), "gather_small_grads")
    zeros = jnp.zeros(gathered.shape[1:], F32)
    summed = _sum_adamw(gathered, zeros, zeros, zeros, "sum_small_grads")[0]
    g_small = dict(zip(small_names, _unpack(summed, [a.shape for a in small_full])))
    g_small["conv_w"] = lax.dynamic_slice_in_dim(g_small["conv_w"], me * conv_w.shape[2], conv_w.shape[2], axis=2)
    g_small["meta"] = lax.dynamic_slice_in_dim(g_small["meta"], me * meta_tokens.shape[1], meta_tokens.shape[1], axis=1)
    small_w = dict(zip(small_names, (pre_norm_g, post_norm_g, conv_b, conv_ln_g, conv_ln_b, b_pw2, conv_w, meta_tokens)))
    small_m = (m_pre_norm_g, m_post_norm_g, m_conv_b, m_conv_ln_g, m_conv_ln_b, m_b_pw2, m_conv_w, m_meta_tokens)
    small_v = (v_pre_norm_g, v_post_norm_g, v_conv_b, v_conv_ln_g, v_conv_ln_b, v_b_pw2, v_conv_w, v_meta_tokens)
    small_shapes = [small_w[k].shape for k in small_names]
    outs = _sum_adamw(_pack([g_small[k] for k in small_names])[None], _pack([small_w[k] for k in small_names]),
                      _pack(small_m), _pack(small_v), "adamw_small_weights")
    g_s, d_s, nm_s, nv_s = [dict(zip(small_names, _unpack(o, small_shapes))) for o in outs]

    def ordered(s, w_in_, w_pw2_, w_out_):
        return (s["meta"], s["pre_g"], s["post_g"], w_in_, s["conv_w"], s["conv_b"], s["ln_g"], s["ln_b"], w_pw2_,
                s["b_pw2"], w_out_)

    return (loss, dx[None], *ordered(g_s, g_w_in, g_w_pw2, g_w_out), *ordered(d_s, d_w_in, d_w_pw2, d_w_out),
            *ordered(nm_s, nm_w_in, nm_w_pw2, nm_w_out), *ordered(nv_s, nv_w_in, nv_w_pw2, nv_w_out))
```

```python
import functools

import jax
import jax.numpy as jnp
from jax import lax
from jax.experimental import pallas as pl
from jax.experimental.pallas import tpu as pltpu

F32 = jnp.float32
BF16 = jnp.bfloat16

D_MODEL = 1024
D_CONV = 512
D_SB = 512
HEAD_DIM = 64
HEADS_PER_BLOCK = 2
CONV_WIDTH = 31
N_META = 16
D_IN = 3 * D_CONV + 4 * D_SB
RMS_EPS = 1e-6
LN_EPS = 1e-5
SB_SCALE = HEAD_DIM ** -0.5

ADAM_LR = 0.001
ADAM_B1 = 0.9
ADAM_B2 = 0.999
ADAM_EPS = 1e-08
ADAM_WD = 0.01
ADAM_STEP = 10

N_DEV = 8
LANES = 128
ROW_BLK = 256
HALO = 32
VMEM_LIMIT = 56 * 1024 * 1024
MESH = pl.DeviceIdType.MESH


def _cparams(*sem):
    return pltpu.CompilerParams(dimension_semantics=sem, vmem_limit_bytes=VMEM_LIMIT)


def _rows(n_cols, col=0):
    return pl.BlockSpec((ROW_BLK, n_cols), lambda i, col=col: (i, col))


def _whole(shape):
    return pl.BlockSpec(shape, lambda i: (0,) * len(shape))


def _sigmoid(x):
    return jax.nn.sigmoid(x)


def _dsilu(x, s):
    return s * (1.0 + x * (1.0 - s))


def _dot(a, b):
    return jnp.dot(a, b, preferred_element_type=F32)


def _dot_nt(a, b):
    return lax.dot_general(a, b, (((1,), (1,)), ((), ())), preferred_element_type=F32)


def _dot_tn(a, b):
    return lax.dot_general(a, b, (((0,), (0,)), ((), ())), preferred_element_type=F32)


def _inproj_fwd(h, g_pre, w_in):
    t = h.shape[0]

    def body(h_ref, g_ref, w_ref, pc_ref, qkv_ref, sbg_ref, u_ref):
        x = h_ref[...]
        r = lax.rsqrt(jnp.mean(x * x, axis=-1, keepdims=True) + RMS_EPS)
        u = (x * r * g_ref[...]).astype(BF16)
        u_ref[...] = u
        pc_ref[...] = _dot(u, w_ref[:, 0:1536])
        qkv_ref[...] = _dot(u, w_ref[:, 1536:3072]).astype(BF16)
        sbg_ref[...] = _dot(u, w_ref[:, 3072:3584])

    return pl.pallas_call(
        body, name="inproj_fwd", grid=(t // ROW_BLK,),
        in_specs=[_rows(D_MODEL), _whole((1, D_MODEL)), _whole((D_MODEL, D_IN))],
        out_specs=[_rows(1536), _rows(1536), _rows(D_SB), _rows(D_MODEL)],
        out_shape=[jax.ShapeDtypeStruct((t, 1536), F32), jax.ShapeDtypeStruct((t, 1536), BF16),
                   jax.ShapeDtypeStruct((t, D_SB), F32), jax.ShapeDtypeStruct((t, D_MODEL), BF16)],
        compiler_params=_cparams("parallel"),
    )(h, g_pre, w_in)


def _prev_halo(col):
    per = ROW_BLK // HALO
    return pl.BlockSpec((HALO, D_CONV), lambda i, col=col: (jnp.maximum(i * per - 1, 0), col))


def _fill_glu(buf, i, a_ref, b_ref, ha_ref, hb_ref):
    halo = ha_ref[...] * _sigmoid(hb_ref[...])
    buf[0:HALO, :] = jnp.where(i > 0, halo, 0.0)
    buf[HALO:HALO + ROW_BLK, :] = a_ref[...] * _sigmoid(b_ref[...])


def _layer_norm_stats(cv):
    mu = jnp.mean(cv, axis=-1, keepdims=True)
    xc = cv - mu
    rstd = lax.rsqrt(jnp.mean(xc * xc, axis=-1, keepdims=True) + LN_EPS)
    return xc * rstd, rstd


def _conv_fwd(pc, conv_w, conv_b, ln_g, ln_b, w_pw2, b_pw2):
    t = pc.shape[0]

    def body(a_ref, b_ref, gate_ref, ha_ref, hb_ref, cw_ref, cb_ref, lg_ref, lb_ref, wp_ref, bp_ref,
             cout_ref, cv_ref, p_ref, sl_ref, buf):
        i = pl.program_id(0)
        _fill_glu(buf, i, a_ref, b_ref, ha_ref, hb_ref)
        acc = jnp.zeros((ROW_BLK, D_CONV), F32) + cb_ref[...]
        for j in range(CONV_WIDTH):
            acc = acc + cw_ref[j:j + 1, :] * buf[pl.ds(HALO - (CONV_WIDTH - 1) + j, ROW_BLK), :]
        cv_ref[...] = acc
        xh, _ = _layer_norm_stats(acc)
        ln = xh * lg_ref[...] + lb_ref[...]
        sl = (ln * _sigmoid(ln)).astype(BF16)
        sl_ref[...] = sl
        p = _dot(sl, wp_ref[...]) + bp_ref[...]
        p_ref[...] = p
        gate = gate_ref[...]
        cout_ref[...] = (p * (gate * _sigmoid(gate))).astype(BF16)

    vec = _whole((1, D_CONV))
    return pl.pallas_call(
        body, name="conv_fwd", grid=(t // ROW_BLK,),
        in_specs=[_rows(D_CONV, 0), _rows(D_CONV, 1), _rows(D_CONV, 2), _prev_halo(0), _prev_halo(1),
                  _whole((CONV_WIDTH, D_CONV)), vec, vec, vec, _whole((D_CONV, D_CONV)), vec],
        out_specs=[_rows(D_CONV)] * 4,
        out_shape=[jax.ShapeDtypeStruct((t, D_CONV), BF16), jax.ShapeDtypeStruct((t, D_CONV), F32),
                   jax.ShapeDtypeStruct((t, D_CONV), F32), jax.ShapeDtypeStruct((t, D_CONV), BF16)],
        scratch_shapes=[pltpu.VMEM((HALO + ROW_BLK, D_CONV), F32)],
        compiler_params=_cparams("parallel"),
    )(pc, pc, pc, pc, pc, conv_w, conv_b, ln_g, ln_b, w_pw2, b_pw2)


def _lower_triangle():
    row = lax.broadcasted_iota(jnp.int32, (ROW_BLK, ROW_BLK), 0)
    col = lax.broadcasted_iota(jnp.int32, (ROW_BLK, ROW_BLK), 1)
    return row > col


def _lower_triangle_t():
    row = lax.broadcasted_iota(jnp.int32, (ROW_BLK, ROW_BLK), 0)
    col = lax.broadcasted_iota(jnp.int32, (ROW_BLK, ROW_BLK), 1)
    return row < col


def _tri_sum(x, umat):
    hi = x.astype(BF16)
    lo = (x - hi.astype(F32)).astype(BF16)
    return _dot(hi, umat) + _dot(lo, umat)


def _log_gates(z):
    ls = -(jnp.maximum(z, 0.0) + jnp.log(1.0 + jnp.exp(-jnp.abs(z))))
    return ls, z + ls


def _head_lanes(hh):
    lane = lax.broadcasted_iota(jnp.int32, (ROW_BLK, LANES), 1)
    return (lane >= HEAD_DIM * hh) & (lane < HEAD_DIM * (hh + 1))


def _qkv_specs(t):
    n_blk = D_SB // LANES
    return [pl.BlockSpec((ROW_BLK, LANES), lambda hp, i: (i, hp)),
            pl.BlockSpec((t, LANES), lambda hp, i: (0, n_blk + hp)),
            pl.BlockSpec((t, LANES), lambda hp, i: (0, 2 * n_blk + hp))]


def _carry_spec():
    return pl.BlockSpec((HEADS_PER_BLOCK, ROW_BLK, LANES), lambda hp, i: (hp, i, 0))


def _attn_fwd(qkv):
    t = qkv.shape[0]
    assert t // ROW_BLK <= LANES

    def body(q_ref, k_ref, v_ref, o_ref, c_ref, acc_ref):
        i = pl.program_id(1)
        lower = _lower_triangle()
        umat = jnp.where(lower, 1.0, 0.0).astype(BF16)
        lane = lax.broadcasted_iota(jnp.int32, (ROW_BLK, LANES), 1)
        q = q_ref[...]
        heads = range(HEADS_PER_BLOCK)
        qms = [jnp.where(_head_lanes(hh), q, jnp.zeros_like(q)) * jnp.asarray(SB_SCALE, BF16) for hh in heads]
        acc_ref[...] = jnp.zeros_like(acc_ref)
        c_ref[...] = jnp.zeros_like(c_ref)

        def block(jb, carries, diagonal):
            start = pl.multiple_of(jb * ROW_BLK, ROW_BLK)
            kb = k_ref[pl.ds(start, ROW_BLK), :]
            vb = v_ref[pl.ds(start, ROW_BLK), :]
            out = []
            for hh in heads:
                carry = carries[hh]
                ls, lb = _log_gates(_dot_nt(qms[hh], kb))
                if diagonal:
                    ls = jnp.where(lower, ls, 0.0)
                else:
                    c_ref[hh] = jnp.where(lane == jb, carry, c_ref[hh])
                a = jnp.exp(lb + carry + _tri_sum(ls, umat))
                if diagonal:
                    a = jnp.where(lower, a, 0.0)
                acc_ref[hh] += _dot(a.astype(BF16), vb)
                out.append(carry + jnp.sum(ls, axis=1, keepdims=True))
            return tuple(out)

        zero = jnp.zeros((ROW_BLK, 1), F32)
        carries = block(i, (zero,) * HEADS_PER_BLOCK, True)
        lax.fori_loop(0, i, lambda n, c: block(i - 1 - n, c, False), carries)
        o_ref[...] = jnp.where(_head_lanes(0), acc_ref[0], acc_ref[1])

    return pl.pallas_call(
        body, name="attn_fwd", grid=(D_SB // LANES, t // ROW_BLK),
        in_specs=_qkv_specs(t),
        out_specs=[pl.BlockSpec((ROW_BLK, LANES), lambda hp, i: (i, hp)), _carry_spec()],
        out_shape=[jax.ShapeDtypeStruct((t, D_SB), F32),
                   jax.ShapeDtypeStruct((D_SB // HEAD_DIM, t, LANES), F32)],
        scratch_shapes=[pltpu.VMEM((HEADS_PER_BLOCK, ROW_BLK, LANES), F32)],
        compiler_params=_cparams("arbitrary", "arbitrary"),
    )(qkv, qkv, qkv)


def _outproj_fwd(h, cout, sraw, sbg, w_out, g_post):
    t = h.shape[0]

    def body(h_ref, c_ref, s_ref, g_ref, w_ref, gp_ref, hn_ref, mixed_ref, mix_ref):
        gate = g_ref[...]
        mix_ref[:, 0:D_CONV] = c_ref[...]
        mix_ref[:, D_CONV:] = (s_ref[...] * (gate * _sigmoid(gate))).astype(BF16)
        mixed = _dot(mix_ref[...], w_ref[...])
        mixed_ref[...] = mixed
        r = lax.rsqrt(jnp.mean(mixed * mixed, axis=-1, keepdims=True) + RMS_EPS)
        hn_ref[...] = h_ref[...] + mixed * r * gp_ref[...]

    return pl.pallas_call(
        body, name="outproj_fwd", grid=(t // ROW_BLK,),
        in_specs=[_rows(D_MODEL), _rows(D_CONV), _rows(D_SB), _rows(D_SB), _whole((D_MODEL, D_MODEL)),
                  _whole((1, D_MODEL))],
        out_specs=[_rows(D_MODEL)] * 3,
        out_shape=[jax.ShapeDtypeStruct((t, D_MODEL), F32), jax.ShapeDtypeStruct((t, D_MODEL), F32),
                   jax.ShapeDtypeStruct((t, D_MODEL), BF16)],
        compiler_params=_cparams("parallel"),
    )(h, cout, sraw, sbg, w_out, g_post)


def _loss_and_grad(h, target, seq):
    t = h.shape[0]

    def body(h_ref, t_ref, loss_ref, dh_ref):
        i = pl.program_id(0)

        @pl.when(i == 0)
        def _():
            loss_ref[...] = jnp.zeros_like(loss_ref)

        row = i * ROW_BLK + lax.broadcasted_iota(jnp.int32, (ROW_BLK, D_MODEL), 0)
        real = (row >= N_META) & (row < N_META + seq)
        diff = jnp.where(real, h_ref[...] - t_ref[...], 0.0)
        sq = jnp.sum(jnp.sum(diff * diff, axis=1, keepdims=True), axis=0, keepdims=True)
        loss_ref[...] += (0.5 / D_MODEL) * sq
        dh_ref[...] = diff * (1.0 / D_MODEL)

    return pl.pallas_call(
        body, name="loss", grid=(t // ROW_BLK,),
        in_specs=[_rows(D_MODEL), _rows(D_MODEL)],
        out_specs=[_whole((1, 1)), _rows(D_MODEL)],
        out_shape=[jax.ShapeDtypeStruct((1, 1), F32), jax.ShapeDtypeStruct((t, D_MODEL), F32)],
        compiler_params=_cparams("arbitrary"),
    )(h, target)


def _outproj_bwd(dh, mixed, g_post, sraw, sbg, w_out_t):
    t = dh.shape[0]

    def body(dh_ref, mixed_ref, gp_ref, s_ref, g_ref, wt_ref, dc_ref, ds_ref, dg_ref, dmb_ref, dgp_ref):
        @pl.when(pl.program_id(0) == 0)
        def _():
            dgp_ref[...] = jnp.zeros_like(dgp_ref)

        mixed = mixed_ref[...]
        r = lax.rsqrt(jnp.mean(mixed * mixed, axis=-1, keepdims=True) + RMS_EPS)
        nh = mixed * r
        dy = dh_ref[...]
        dgp_ref[...] += jnp.sum(dy * nh, axis=0, keepdims=True)
        dn = dy * gp_ref[...]
        dmixed = (r * (dn - nh * jnp.mean(dn * nh, axis=-1, keepdims=True))).astype(BF16)
        dmb_ref[...] = dmixed
        dmix = _dot(dmixed, wt_ref[...])
        dc_ref[...] = dmix[:, 0:D_CONV]
        dsg = dmix[:, D_CONV:]
        gate = g_ref[...]
        sg = _sigmoid(gate)
        ds_ref[...] = dsg * (gate * sg)
        dg_ref[...] = dsg * s_ref[...] * _dsilu(gate, sg)

    return pl.pallas_call(
        body, name="outproj_bwd", grid=(t // ROW_BLK,),
        in_specs=[_rows(D_MODEL), _rows(D_MODEL), _whole((1, D_MODEL)), _rows(D_SB), _rows(D_SB),
                  _whole((D_MODEL, D_MODEL))],
        out_specs=[_rows(D_CONV), _rows(D_SB), _rows(D_SB), _rows(D_MODEL), _whole((1, D_MODEL))],
        out_shape=[jax.ShapeDtypeStruct((t, D_CONV), F32), jax.ShapeDtypeStruct((t, D_SB), F32),
                   jax.ShapeDtypeStruct((t, D_SB), F32), jax.ShapeDtypeStruct((t, D_MODEL), BF16),
                   jax.ShapeDtypeStruct((1, D_MODEL), F32)],
        compiler_params=_cparams("arbitrary"),
    )(dh, mixed, g_post, sraw, sbg, w_out_t)


def _attn_bwd(qkv, carries, do):
    t = qkv.shape[0]

    def body(q_ref, k_ref, v_ref, c_ref, do_ref, dq_ref, dk_ref, dv_ref, acc_ref):
        i = pl.program_id(1)

        @pl.when(i == 0)
        def _():
            dk_ref[...] = jnp.zeros_like(dk_ref)
            dv_ref[...] = jnp.zeros_like(dv_ref)

        lower = _lower_triangle()
        umat = jnp.where(lower, 1.0, 0.0).astype(BF16)
        umat_t = jnp.where(_lower_triangle_t(), 1.0, 0.0).astype(BF16)
        lane = lax.broadcasted_iota(jnp.int32, (ROW_BLK, LANES), 1)
        q = q_ref[...]
        dof = do_ref[...]
        heads = range(HEADS_PER_BLOCK)
        qms = [jnp.where(_head_lanes(hh), q, jnp.zeros_like(q)) * jnp.asarray(SB_SCALE, BF16) for hh in heads]
        doms = [jnp.where(_head_lanes(hh), dof, 0.0).astype(BF16) for hh in heads]
        acc_ref[...] = jnp.zeros_like(acc_ref)

        def block(jb, seens, diagonal):
            start = pl.multiple_of(jb * ROW_BLK, ROW_BLK)
            kb = k_ref[pl.ds(start, ROW_BLK), :]
            vb = v_ref[pl.ds(start, ROW_BLK), :]
            out = []
            dk = dv = None
            for hh in heads:
                ls, lb = _log_gates(_dot_nt(qms[hh], kb))
                if diagonal:
                    ls = jnp.where(lower, ls, 0.0)
                    tail = _tri_sum(ls, umat)
                else:
                    right = jnp.sum(jnp.where(lane == jb, c_ref[hh], 0.0), axis=1, keepdims=True)
                    tail = right + _tri_sum(ls, umat)
                a = jnp.exp(lb + tail)
                if diagonal:
                    a = jnp.where(lower, a, 0.0)
                g = _dot_nt(doms[hh], vb) * a
                before = seens[hh] + _tri_sum(g, umat_t)
                dz = g - jnp.exp(lb) * (g + before)
                if diagonal:
                    dz = jnp.where(lower, dz, 0.0)
                dzb = dz.astype(BF16)
                acc_ref[hh] += _dot(dzb, kb)
                dk_h = _dot_tn(dzb, qms[hh])
                dv_h = _dot_tn(a.astype(BF16), doms[hh])
                dk = dk_h if dk is None else dk + dk_h
                dv = dv_h if dv is None else dv + dv_h
                out.append(seens[hh] + jnp.sum(g, axis=1, keepdims=True))
            dk_ref[pl.ds(start, ROW_BLK), :] += dk
            dv_ref[pl.ds(start, ROW_BLK), :] += dv
            return tuple(out)

        zero = jnp.zeros((ROW_BLK, 1), F32)
        seens = lax.fori_loop(0, i, lambda jb, s: block(jb, s, False), (zero,) * HEADS_PER_BLOCK)
        block(i, seens, True)
        dq_ref[...] = jnp.where(_head_lanes(0), acc_ref[0], acc_ref[1]) * SB_SCALE

    blk = pl.BlockSpec((ROW_BLK, LANES), lambda hp, i: (i, hp))
    full = pl.BlockSpec((t, LANES), lambda hp, i: (0, hp))
    return pl.pallas_call(
        body, name="attn_bwd", grid=(D_SB // LANES, t // ROW_BLK),
        in_specs=_qkv_specs(t) + [_carry_spec(), blk],
        out_specs=[blk, full, full],
        out_shape=[jax.ShapeDtypeStruct((t, D_SB), F32)] * 3,
        scratch_shapes=[pltpu.VMEM((HEADS_PER_BLOCK, ROW_BLK, LANES), F32)],
        compiler_params=_cparams("arbitrary", "arbitrary"),
    )(qkv, qkv, qkv, carries, do)


def _conv_bwd_rows(dcout, pc, cv, p, ln_g, ln_b, w_pw2_t):
    t = dcout.shape[0]

    def body(dc_ref, gate_ref, cv_ref, p_ref, lg_ref, lb_ref, wt_ref, dcv_ref, dgate_ref, dpb_ref, vec_ref):
        @pl.when(pl.program_id(0) == 0)
        def _():
            vec_ref[...] = jnp.zeros_like(vec_ref)

        dc = dc_ref[...]
        gate = gate_ref[...]
        sg = _sigmoid(gate)
        dp = dc * (gate * sg)
        dgate_ref[...] = dc * p_ref[...] * _dsilu(gate, sg)
        dpb = dp.astype(BF16)
        dpb_ref[...] = dpb
        xh, rstd = _layer_norm_stats(cv_ref[...])
        ln = xh * lg_ref[...] + lb_ref[...]
        s2 = _sigmoid(ln)
        dln = _dot(dpb, wt_ref[...]) * _dsilu(ln, s2)
        dxh = dln * lg_ref[...]
        dcv = rstd * (dxh - jnp.mean(dxh, axis=-1, keepdims=True)
                      - xh * jnp.mean(dxh * xh, axis=-1, keepdims=True))
        dcv_ref[...] = dcv
        vec_ref[0:1, :] += jnp.sum(dp, axis=0, keepdims=True)
        vec_ref[1:2, :] += jnp.sum(dln * xh, axis=0, keepdims=True)
        vec_ref[2:3, :] += jnp.sum(dln, axis=0, keepdims=True)
        vec_ref[3:4, :] += jnp.sum(dcv, axis=0, keepdims=True)

    vec = _whole((1, D_CONV))
    return pl.pallas_call(
        body, name="conv_bwd_rows", grid=(t // ROW_BLK,),
        in_specs=[_rows(D_CONV), _rows(D_CONV, 2), _rows(D_CONV), _rows(D_CONV), vec, vec,
                  _whole((D_CONV, D_CONV))],
        out_specs=[_rows(D_CONV), _rows(D_CONV), _rows(D_CONV), _whole((8, D_CONV))],
        out_shape=[jax.ShapeDtypeStruct((t, D_CONV), F32), jax.ShapeDtypeStruct((t, D_CONV), F32),
                   jax.ShapeDtypeStruct((t, D_CONV), BF16), jax.ShapeDtypeStruct((8, D_CONV), F32)],
        compiler_params=_cparams("arbitrary"),
    )(dcout, pc, cv, p, ln_g, ln_b, w_pw2_t)


def _conv_bwd_taps(dcv, pc, conv_w):
    t = dcv.shape[0]
    n_halo = t // HALO
    per = ROW_BLK // HALO

    def body(d_ref, dn_ref, a_ref, b_ref, ha_ref, hb_ref, cw_ref, da_ref, db_ref, dw_ref, cbuf, dbuf):
        i = pl.program_id(0)

        @pl.when(i == 0)
        def _():
            dw_ref[...] = jnp.zeros_like(dw_ref)

        _fill_glu(cbuf, i, a_ref, b_ref, ha_ref, hb_ref)
        dcv = d_ref[...]
        dbuf[0:ROW_BLK, :] = dcv
        dbuf[ROW_BLK:ROW_BLK + HALO, :] = jnp.where(i < pl.num_programs(0) - 1, dn_ref[...], 0.0)
        acc = jnp.zeros((ROW_BLK, D_CONV), F32)
        for j in range(CONV_WIDTH):
            acc = acc + cw_ref[j:j + 1, :] * dbuf[pl.ds(CONV_WIDTH - 1 - j, ROW_BLK), :]
            seen = cbuf[pl.ds(HALO - (CONV_WIDTH - 1) + j, ROW_BLK), :]
            dw_ref[j:j + 1, :] += jnp.sum(dcv * seen, axis=0, keepdims=True)
        a = a_ref[...]
        sb = _sigmoid(b_ref[...])
        da_ref[...] = acc * sb
        db_ref[...] = acc * a * sb * (1.0 - sb)

    return pl.pallas_call(
        body, name="conv_bwd_taps", grid=(t // ROW_BLK,),
        in_specs=[_rows(D_CONV),
                  pl.BlockSpec((HALO, D_CONV), lambda i: (jnp.minimum((i + 1) * per, n_halo - 1), 0)),
                  _rows(D_CONV, 0), _rows(D_CONV, 1), _prev_halo(0), _prev_halo(1),
                  _whole((CONV_WIDTH, D_CONV))],
        out_specs=[_rows(D_CONV), _rows(D_CONV), _whole((32, D_CONV))],
        out_shape=[jax.ShapeDtypeStruct((t, D_CONV), F32), jax.ShapeDtypeStruct((t, D_CONV), F32),
                   jax.ShapeDtypeStruct((32, D_CONV), F32)],
        scratch_shapes=[pltpu.VMEM((HALO + ROW_BLK, D_CONV), F32), pltpu.VMEM((ROW_BLK + HALO, D_CONV), F32)],
        compiler_params=_cparams("arbitrary"),
    )(dcv, dcv, pc, pc, pc, pc, conv_w)


def _inproj_bwd(dh_out, h, g_pre, pieces, w_in_t):
    t = h.shape[0]

    def body(dh_ref, h_ref, g_ref, *rest):
        piece_refs, (wt_ref, dhin_ref, dproj_ref, dg_ref) = rest[:7], rest[7:]

        @pl.when(pl.program_id(0) == 0)
        def _():
            dg_ref[...] = jnp.zeros_like(dg_ref)

        for k, ref in enumerate(piece_refs):
            dproj_ref[:, 512 * k:512 * (k + 1)] = ref[...].astype(BF16)
        du = _dot(dproj_ref[...], wt_ref[...])
        x = h_ref[...]
        r = lax.rsqrt(jnp.mean(x * x, axis=-1, keepdims=True) + RMS_EPS)
        xh = x * r
        dg_ref[...] += jnp.sum(du * xh, axis=0, keepdims=True)
        dxh = du * g_ref[...]
        dhin_ref[...] = dh_ref[...] + r * (dxh - xh * jnp.mean(dxh * xh, axis=-1, keepdims=True))

    return pl.pallas_call(
        body, name="inproj_bwd", grid=(t // ROW_BLK,),
        in_specs=[_rows(D_MODEL), _rows(D_MODEL), _whole((1, D_MODEL))] + [_rows(512)] * 7
                 + [_whole((D_IN, D_MODEL))],
        out_specs=[_rows(D_MODEL), _rows(D_IN), _whole((1, D_MODEL))],
        out_shape=[jax.ShapeDtypeStruct((t, D_MODEL), F32), jax.ShapeDtypeStruct((t, D_IN), BF16),
                   jax.ShapeDtypeStruct((1, D_MODEL), F32)],
        compiler_params=_cparams("arbitrary"),
    )(dh_out, h, g_pre, *pieces, w_in_t)


def _weight_grad(xb, dyb, name):
    t, k = xb.shape
    n = dyb.shape[1]
    tn = min(n, 512)

    def body(x_ref, dy_ref, o_ref):
        @pl.when(pl.program_id(1) == 0)
        def _():
            o_ref[...] = jnp.zeros_like(o_ref)

        o_ref[...] += _dot_tn(x_ref[...], dy_ref[...])

    return pl.pallas_call(
        body, name=name, grid=(n // tn, t // ROW_BLK),
        in_specs=[pl.BlockSpec((ROW_BLK, k), lambda j, i: (i, 0)), pl.BlockSpec((ROW_BLK, tn), lambda j, i: (i, j))],
        out_specs=pl.BlockSpec((k, tn), lambda j, i: (0, j)),
        out_shape=jax.ShapeDtypeStruct((k, n), F32),
        compiler_params=_cparams("parallel", "arbitrary"),
    )(xb, dyb)


def _position():
    return lax.axis_index("x"), lax.axis_index("y"), lax.axis_index("c")


def _comm_call(body, name, ins, out_shapes):
    n = len(ins)
    hbm = pl.BlockSpec(memory_space=pltpu.HBM)
    return pl.pallas_call(
        functools.partial(body, n), name=name, in_specs=[hbm] * n, out_specs=[hbm] * n, out_shape=out_shapes,
        scratch_shapes=[pltpu.SemaphoreType.DMA((n, N_DEV - 1)), pltpu.SemaphoreType.DMA((n, N_DEV - 1)),
                        pltpu.SemaphoreType.DMA((n,))],
    )(*ins)


def _all_gather(blocks, name):
    def body(n, *refs):
        x_refs, out_refs, (send_sems, recv_sems, local_sems) = refs[:n], refs[n:2 * n], refs[2 * n:]
        x, y, c = _position()
        me, sibling = (x, y, c), (x, y, 1 - c)
        chips = [(1 - x, y), (x, 1 - y), (1 - x, 1 - y)]

        def slot(a, px, py, pc):
            return out_refs[a].at[4 * px + 2 * py + pc]

        def copy(a, k, origin, to, own=False):
            return pltpu.make_async_remote_copy(
                src_ref=x_refs[a] if own else slot(a, *origin), dst_ref=slot(a, *origin),
                send_sem=send_sems.at[a, k], recv_sem=recv_sems.at[a, k], device_id=to, device_id_type=MESH)

        arrays = range(n)
        mine = [pltpu.make_async_copy(x_refs[a], slot(a, *me), local_sems.at[a]) for a in arrays]
        first = [copy(a, 1 + j, me, (*chip, c), own=True) for j, chip in enumerate(chips) for a in arrays]
        first += [copy(a, 0, me, sibling, own=True) for a in arrays]
        for cp in mine + first:
            cp.start()
        passed = []
        for j, chip in enumerate(chips):
            for a in arrays:
                copy(a, 1 + j, (*chip, c), me).wait_recv()
                passed.append(copy(a, 4 + j, (*chip, c), sibling))
                passed[-1].start()
        for a in arrays:
            copy(a, 0, sibling, me).wait_recv()
            for j, chip in enumerate(chips):
                copy(a, 4 + j, (*chip, 1 - c), me).wait_recv()
        for cp in first + passed:
            cp.wait_send()
        for cp in mine:
            cp.wait()

    return _comm_call(body, name, blocks, [jax.ShapeDtypeStruct((N_DEV,) + b.shape, b.dtype) for b in blocks])


def _all_to_all(slabs, name):
    def body(n, *refs):
        g_refs, land_refs, (send_sems, recv_sems, local_sems) = refs[:n], refs[n:2 * n], refs[2 * n:]
        x, y, c = _position()
        me = 4 * x + 2 * y + c
        arrays = range(n)
        mine = [pltpu.make_async_copy(g_refs[a].at[me], land_refs[a].at[me], local_sems.at[a]) for a in arrays]
        peers = []
        for k in range(1, N_DEV):
            px = 1 - x if k & 4 else x
            py = 1 - y if k & 2 else y
            pc = 1 - c if k & 1 else c
            peers.append((k - 1, (px, py, pc), 4 * px + 2 * py + pc))

        def copy(a, s, dev, idx, landing):
            return pltpu.make_async_remote_copy(
                src_ref=g_refs[a].at[idx], dst_ref=land_refs[a].at[landing], send_sem=send_sems.at[a, s],
                recv_sem=recv_sems.at[a, s], device_id=dev, device_id_type=MESH)

        sends = [copy(a, s, dev, idx, me) for s, dev, idx in peers for a in arrays]
        for cp in mine + sends:
            cp.start()
        for s, dev, idx in peers:
            for a in arrays:
                copy(a, s, dev, idx, idx).wait_recv()
        for cp in sends:
            cp.wait_send()
        for cp in mine:
            cp.wait()

    return _comm_call(body, name, slabs, [jax.ShapeDtypeStruct(g.shape, g.dtype) for g in slabs])


def _block_rows(r, row_bytes, budget=1 << 20):
    cap = max(8, budget // row_bytes)
    return max(d for d in range(8, min(r, cap) + 1, 8) if r % d == 0)


def _sum_adamw(parts, w, m, v, name):
    n_parts, r, c = parts.shape
    br = _block_rows(r, 4 * c)

    def body(p_ref, w_ref, m_ref, v_ref, g_out, d_out, m_out, v_out):
        g = p_ref[0]
        for s in range(1, n_parts):
            g = g + p_ref[s]
        m_new = ADAM_B1 * m_ref[...] + (1.0 - ADAM_B1) * g
        v_new = ADAM_B2 * v_ref[...] + (1.0 - ADAM_B2) * (g * g)
        m_hat = m_new / (1.0 - ADAM_B1 ** ADAM_STEP)
        v_hat = v_new / (1.0 - ADAM_B2 ** ADAM_STEP)
        g_out[...] = g
        d_out[...] = -ADAM_LR * (m_hat / (jnp.sqrt(v_hat) + ADAM_EPS) + ADAM_WD * w_ref[...])
        m_out[...] = m_new
        v_out[...] = v_new

    row = pl.BlockSpec((br, c), lambda i: (i, 0))
    return pl.pallas_call(
        body, name=name, grid=(r // br,),
        in_specs=[pl.BlockSpec((n_parts, br, c), lambda i: (0, i, 0)), row, row, row],
        out_specs=[row] * 4, out_shape=[jax.ShapeDtypeStruct((r, c), F32)] * 4,
        compiler_params=_cparams("parallel"),
    )(parts, w, m, v)


def _sum_parts(parts, name):
    n_parts, r, c = parts.shape

    def body(p_ref, o_ref):
        g = p_ref[0]
        for s in range(1, n_parts):
            g = g + p_ref[s]
        o_ref[...] = g

    return pl.pallas_call(
        body, name=name, in_specs=[pl.BlockSpec(memory_space=pltpu.VMEM)],
        out_specs=pl.BlockSpec(memory_space=pltpu.VMEM), out_shape=jax.ShapeDtypeStruct((r, c), F32),
    )(parts)


def _pack(arrays):
    flat = jnp.concatenate([a.reshape(-1) for a in arrays])
    pad = -flat.shape[0] % (8 * LANES)
    if pad:
        flat = jnp.pad(flat, (0, pad))
    return flat.reshape(-1, LANES)


def _unpack(buf, shapes):
    flat = buf.reshape(-1)
    out, at = [], 0
    for shape in shapes:
        size = 1
        for d in shape:
            size *= d
        out.append(lax.slice_in_dim(flat, at, at + size).reshape(shape))
        at += size
    return out


def _local_step(x, target, meta, pre_g, post_g, w_in, conv_w, conv_b, ln_g, ln_b, w_pw2, b_pw2, w_out):
    depth = w_in.shape[0]
    seq = x.shape[0]
    t = -(-(N_META + seq) // ROW_BLK) * ROW_BLK
    tail = t - N_META - seq
    h = jnp.concatenate([meta, x, jnp.zeros((tail, D_MODEL), F32)], axis=0)
    target = jnp.pad(target, ((N_META, tail), (0, 0)))
    w_in_t = jnp.swapaxes(w_in, 1, 2)
    w_pw2_t = jnp.swapaxes(w_pw2, 1, 2)
    w_out_t = jnp.swapaxes(w_out, 1, 2)
    row = lambda a, l: a[l][None, :]

    saved = []
    for l in range(depth):
        pc, qkv, sbg, u = _inproj_fwd(h, row(pre_g, l), w_in[l])
        cout, cv, p, sl = _conv_fwd(pc, conv_w[l], row(conv_b, l), row(ln_g, l), row(ln_b, l), w_pw2[l],
                                    row(b_pw2, l))
        sraw, carries = _attn_fwd(qkv)
        h_new, mixed, mix = _outproj_fwd(h, cout, sraw, sbg, w_out[l], row(post_g, l))
        saved.append((h, pc, qkv, sbg, u, cv, p, sl, sraw, carries, mixed, mix))
        h = h_new

    loss, dh = _loss_and_grad(h, target, seq)

    grads = {k: [None] * depth for k in ("pre_g", "post_g", "w_in", "conv_w", "conv_b", "ln_g", "ln_b", "w_pw2",
                                          "b_pw2", "w_out")}
    for l in reversed(range(depth)):
        h_in, pc, qkv, sbg, u, cv, p, sl, sraw, carries, mixed, mix = saved[l]
        dcout, dsraw, dsbg, dmixed, dg_post = _outproj_bwd(dh, mixed, row(post_g, l), sraw, sbg, w_out_t[l])
        dq, dk, dv = _attn_bwd(qkv, carries, dsraw)
        dcv, dgate, dpb, vecs = _conv_bwd_rows(dcout, pc, cv, p, row(ln_g, l), row(ln_b, l), w_pw2_t[l])
        da, db, dconv_w = _conv_bwd_taps(dcv, pc, conv_w[l])
        dh, dproj, dg_pre = _inproj_bwd(dh, h_in, row(pre_g, l), (da, db, dgate, dq, dk, dv, dsbg), w_in_t[l])
        grads["w_in"][l] = _weight_grad(u, dproj, "w_in_grad")
        grads["w_out"][l] = _weight_grad(mix, dmixed, "w_out_grad")
        grads["w_pw2"][l] = _weight_grad(sl, dpb, "w_pw2_grad")
        grads["pre_g"][l] = dg_pre[0]
        grads["post_g"][l] = dg_post[0]
        grads["b_pw2"][l], grads["ln_g"][l], grads["ln_b"][l], grads["conv_b"][l] = vecs[0], vecs[1], vecs[2], vecs[3]
        grads["conv_w"][l] = dconv_w[:CONV_WIDTH]

    grads = {k: jnp.stack(v) for k, v in grads.items()}
    grads["meta"] = dh[:N_META]
    return loss[0, 0], dh[N_META:N_META + seq], grads


def _shard_major(full, axis):
    shape = full.shape
    split = full.reshape(shape[:axis] + (N_DEV, shape[axis] // N_DEV) + shape[axis + 1:])
    return jnp.moveaxis(split, axis, 0)


def _whole_from_shards(shards, axis):
    moved = jnp.moveaxis(shards, 0, axis)
    shape = moved.shape
    return moved.reshape(shape[:axis] + (shape[axis] * shape[axis + 1],) + shape[axis + 2:])


def kernel(x, meta_tokens, pre_norm_g, post_norm_g, w_in, conv_w, conv_b, conv_ln_g, conv_ln_b, w_pw2, b_pw2, w_out, loss_target, m_meta_tokens, m_pre_norm_g, m_post_norm_g, m_w_in, m_conv_w, m_conv_b, m_conv_ln_g, m_conv_ln_b, m_w_pw2, m_b_pw2, m_w_out, v_meta_tokens, v_pre_norm_g, v_post_norm_g, v_w_in, v_conv_w, v_conv_b, v_conv_ln_g, v_conv_ln_b, v_w_pw2, v_b_pw2, v_w_out):
    me = 4 * lax.axis_index("x") + 2 * lax.axis_index("y") + lax.axis_index("c")

    w_in_s, w_out_s, w_pw2_s, conv_w_s, meta_s = _all_gather(
        [w_in.astype(BF16), w_out.astype(BF16), w_pw2.astype(BF16), conv_w, meta_tokens], "gather_weights")
    w_in_full = _whole_from_shards(w_in_s, 2)
    w_out_full = _whole_from_shards(w_out_s, 1)
    w_pw2_full = _whole_from_shards(w_pw2_s, 1)
    conv_w_full = _whole_from_shards(conv_w_s, 2)
    meta_full = _whole_from_shards(meta_s, 1)

    loss, dx, grads = _local_step(x[0], loss_target[0], meta_full, pre_norm_g, post_norm_g, w_in_full, conv_w_full,
                                  conv_b, conv_ln_g, conv_ln_b, w_pw2_full, b_pw2, w_out_full)
    loss = lax.psum(loss, ("x", "y", "c"))

    landed = _all_to_all([_shard_major(grads["w_in"], 2), _shard_major(grads["w_out"], 1),
                          _shard_major(grads["w_pw2"], 1)], "exchange_matmul_grads")

    def update(parts, w, m, v, name):
        rows = lambda a: a.reshape(-1, a.shape[-1])
        outs = _sum_adamw(parts.reshape(N_DEV, -1, parts.shape[-1]), rows(w), rows(m), rows(v), name)
        return [o.reshape(w.shape) for o in outs]

    g_w_in, d_w_in, nm_w_in, nv_w_in = update(landed[0], w_in, m_w_in, v_w_in, "adamw_w_in")
    g_w_out, d_w_out, nm_w_out, nv_w_out = update(landed[1], w_out, m_w_out, v_w_out, "adamw_w_out")
    g_w_pw2, d_w_pw2, nm_w_pw2, nv_w_pw2 = update(landed[2], w_pw2, m_w_pw2, v_w_pw2, "adamw_w_pw2")

    small_names = ("pre_g", "post_g", "conv_b", "ln_g", "ln_b", "b_pw2", "conv_w", "meta")
    small_full = [grads[k] for k in small_names]
    gathered, = _all_gather([_pack(small_full)], "gather_small_grads")
    summed = _sum_parts(gathered, "sum_small_grads")
    g_small = dict(zip(small_names, _unpack(summed, [a.shape for a in small_full])))
    g_small["conv_w"] = lax.dynamic_slice_in_dim(g_small["conv_w"], me * conv_w.shape[2], conv_w.shape[2], axis=2)
    g_small["meta"] = lax.dynamic_slice_in_dim(g_small["meta"], me * meta_tokens.shape[1], meta_tokens.shape[1], axis=1)
    small_w = dict(zip(small_names, (pre_norm_g, post_norm_g, conv_b, conv_ln_g, conv_ln_b, b_pw2, conv_w, meta_tokens)))
    small_m = (m_pre_norm_g, m_post_norm_g, m_conv_b, m_conv_ln_g, m_conv_ln_b, m_b_pw2, m_conv_w, m_meta_tokens)
    small_v = (v_pre_norm_g, v_post_norm_g, v_conv_b, v_conv_ln_g, v_conv_ln_b, v_b_pw2, v_conv_w, v_meta_tokens)
    small_shapes = [small_w[k].shape for k in small_names]
    outs = _sum_adamw(_pack([g_small[k] for k in small_names])[None], _pack([small_w[k] for k in small_names]),
                      _pack(small_m), _pack(small_v), "adamw_small_weights")
    g_s, d_s, nm_s, nv_s = [dict(zip(small_names, _unpack(o, small_shapes))) for o in outs]

    def ordered(s, w_in_, w_pw2_, w_out_):
        return (s["meta"], s["pre_g"], s["post_g"], w_in_, s["conv_w"], s["conv_b"], s["ln_g"], s["ln_b"], w_pw2_,
                s["b_pw2"], w_out_)

    return (loss, dx[None], *ordered(g_s, g_w_in, g_w_pw2, g_w_out), *ordered(d_s, d_w_in, d_w_pw2, d_w_out),
            *ordered(nm_s, nm_w_in, nm_w_pw2, nm_w_out), *ordered(nv_s, nv_w_in, nv_w_pw2, nv_w_out))
```

```python
import functools

import jax
import jax.numpy as jnp
from jax import lax
from jax.experimental import pallas as pl
from jax.experimental.pallas import tpu as pltpu

F32 = jnp.float32
BF16 = jnp.bfloat16

D_MODEL = 1024
D_CONV = 512
D_SB = 512
HEAD_DIM = 64
HEADS_PER_BLOCK = 4
HEAD_BLK = HEADS_PER_BLOCK * HEAD_DIM
CONV_WIDTH = 31
N_META = 16
D_IN = 3 * D_CONV + 4 * D_SB
RMS_EPS = 1e-6
LN_EPS = 1e-5
SB_SCALE = HEAD_DIM ** -0.5

ADAM_LR = 0.001
ADAM_B1 = 0.9
ADAM_B2 = 0.999
ADAM_EPS = 1e-08
ADAM_WD = 0.01
ADAM_STEP = 10

N_DEV = 8
LANES = 128
ROW_BLK = 256
HALO = 32
VMEM_LIMIT = 56 * 1024 * 1024
MESH = pl.DeviceIdType.MESH


def _cparams(*sem):
    return pltpu.CompilerParams(dimension_semantics=sem, vmem_limit_bytes=VMEM_LIMIT)


def _rows(n_cols, col=0):
    return pl.BlockSpec((ROW_BLK, n_cols), lambda i, col=col: (i, col))


def _whole(shape):
    return pl.BlockSpec(shape, lambda i: (0,) * len(shape))


def _sigmoid(x):
    return jax.nn.sigmoid(x)


def _dsilu(x, s):
    return s * (1.0 + x * (1.0 - s))


def _dot(a, b):
    return jnp.dot(a, b, preferred_element_type=F32)


def _dot_nt(a, b):
    return lax.dot_general(a, b, (((1,), (1,)), ((), ())), preferred_element_type=F32)


def _dot_tn(a, b):
    return lax.dot_general(a, b, (((0,), (0,)), ((), ())), preferred_element_type=F32)


def _inproj_fwd(h, g_pre, w_in):
    t = h.shape[0]

    def body(h_ref, g_ref, w_ref, pc_ref, qkv_ref, sbg_ref, u_ref):
        x = h_ref[...]
        r = lax.rsqrt(jnp.mean(x * x, axis=-1, keepdims=True) + RMS_EPS)
        u = (x * r * g_ref[...]).astype(BF16)
        u_ref[...] = u
        pc_ref[...] = _dot(u, w_ref[:, 0:1536])
        qkv_ref[...] = _dot(u, w_ref[:, 1536:3072]).astype(BF16)
        sbg_ref[...] = _dot(u, w_ref[:, 3072:3584])

    return pl.pallas_call(
        body, name="inproj_fwd", grid=(t // ROW_BLK,),
        in_specs=[_rows(D_MODEL), _whole((1, D_MODEL)), _whole((D_MODEL, D_IN))],
        out_specs=[_rows(1536), _rows(1536), _rows(D_SB), _rows(D_MODEL)],
        out_shape=[jax.ShapeDtypeStruct((t, 1536), F32), jax.ShapeDtypeStruct((t, 1536), BF16),
                   jax.ShapeDtypeStruct((t, D_SB), F32), jax.ShapeDtypeStruct((t, D_MODEL), BF16)],
        compiler_params=_cparams("parallel"),
    )(h, g_pre, w_in)


def _prev_halo(col):
    per = ROW_BLK // HALO
    return pl.BlockSpec((HALO, D_CONV), lambda i, col=col: (jnp.maximum(i * per - 1, 0), col))


def _fill_glu(buf, i, a_ref, b_ref, ha_ref, hb_ref):
    halo = ha_ref[...] * _sigmoid(hb_ref[...])
    buf[0:HALO, :] = jnp.where(i > 0, halo, 0.0)
    buf[HALO:HALO + ROW_BLK, :] = a_ref[...] * _sigmoid(b_ref[...])


def _layer_norm_stats(cv):
    mu = jnp.mean(cv, axis=-1, keepdims=True)
    xc = cv - mu
    rstd = lax.rsqrt(jnp.mean(xc * xc, axis=-1, keepdims=True) + LN_EPS)
    return xc * rstd, rstd


def _conv_fwd(pc, conv_w, conv_b, ln_g, ln_b, w_pw2, b_pw2):
    t = pc.shape[0]

    def body(a_ref, b_ref, gate_ref, ha_ref, hb_ref, cw_ref, cb_ref, lg_ref, lb_ref, wp_ref, bp_ref,
             cout_ref, cv_ref, p_ref, sl_ref, buf):
        i = pl.program_id(0)
        _fill_glu(buf, i, a_ref, b_ref, ha_ref, hb_ref)
        acc = jnp.zeros((ROW_BLK, D_CONV), F32) + cb_ref[...]
        for j in range(CONV_WIDTH):
            acc = acc + cw_ref[j:j + 1, :] * buf[pl.ds(HALO - (CONV_WIDTH - 1) + j, ROW_BLK), :]
        cv_ref[...] = acc
        xh, _ = _layer_norm_stats(acc)
        ln = xh * lg_ref[...] + lb_ref[...]
        sl = (ln * _sigmoid(ln)).astype(BF16)
        sl_ref[...] = sl
        p = _dot(sl, wp_ref[...]) + bp_ref[...]
        p_ref[...] = p
        gate = gate_ref[...]
        cout_ref[...] = (p * (gate * _sigmoid(gate))).astype(BF16)

    vec = _whole((1, D_CONV))
    return pl.pallas_call(
        body, name="conv_fwd", grid=(t // ROW_BLK,),
        in_specs=[_rows(D_CONV, 0), _rows(D_CONV, 1), _rows(D_CONV, 2), _prev_halo(0), _prev_halo(1),
                  _whole((CONV_WIDTH, D_CONV)), vec, vec, vec, _whole((D_CONV, D_CONV)), vec],
        out_specs=[_rows(D_CONV)] * 4,
        out_shape=[jax.ShapeDtypeStruct((t, D_CONV), BF16), jax.ShapeDtypeStruct((t, D_CONV), F32),
                   jax.ShapeDtypeStruct((t, D_CONV), F32), jax.ShapeDtypeStruct((t, D_CONV), BF16)],
        scratch_shapes=[pltpu.VMEM((HALO + ROW_BLK, D_CONV), F32)],
        compiler_params=_cparams("parallel"),
    )(pc, pc, pc, pc, pc, conv_w, conv_b, ln_g, ln_b, w_pw2, b_pw2)


def _lower_triangle():
    row = lax.broadcasted_iota(jnp.int32, (ROW_BLK, ROW_BLK), 0)
    col = lax.broadcasted_iota(jnp.int32, (ROW_BLK, ROW_BLK), 1)
    return row > col


def _lower_triangle_t():
    row = lax.broadcasted_iota(jnp.int32, (ROW_BLK, ROW_BLK), 0)
    col = lax.broadcasted_iota(jnp.int32, (ROW_BLK, ROW_BLK), 1)
    return row < col


def _tri_sum(x, umat):
    hi = x.astype(BF16)
    lo = (x - hi.astype(F32)).astype(BF16)
    return _dot(hi, umat) + _dot(lo, umat)


def _log_gates(z):
    ls = -(jnp.maximum(z, 0.0) + jnp.log(1.0 + jnp.exp(-jnp.abs(z))))
    return ls, z + ls


def _head_lanes(hh):
    lane = lax.broadcasted_iota(jnp.int32, (ROW_BLK, HEAD_BLK), 1)
    return (lane >= HEAD_DIM * hh) & (lane < HEAD_DIM * (hh + 1))


def _merge_heads(acc_ref):
    out = acc_ref[HEADS_PER_BLOCK - 1]
    for hh in range(HEADS_PER_BLOCK - 1):
        out = jnp.where(_head_lanes(hh), acc_ref[hh], out)
    return out


def _qkv_specs(t):
    n_blk = D_SB // HEAD_BLK
    return [pl.BlockSpec((ROW_BLK, HEAD_BLK), lambda hp, i: (i, hp)),
            pl.BlockSpec((t, HEAD_BLK), lambda hp, i: (0, n_blk + hp)),
            pl.BlockSpec((t, HEAD_BLK), lambda hp, i: (0, 2 * n_blk + hp))]


def _carry_spec():
    return pl.BlockSpec((HEADS_PER_BLOCK, ROW_BLK, LANES), lambda hp, i: (hp, i, 0))


def _attn_fwd(qkv):
    t = qkv.shape[0]
    assert t // ROW_BLK <= LANES

    def body(q_ref, k_ref, v_ref, o_ref, c_ref, acc_ref, run_ref, qm_ref):
        i = pl.program_id(1)
        lower = _lower_triangle()
        umat = jnp.where(lower, 1.0, 0.0).astype(BF16)
        lane = lax.broadcasted_iota(jnp.int32, (ROW_BLK, LANES), 1)
        q = q_ref[...]
        heads = range(HEADS_PER_BLOCK)
        for hh in heads:
            qm_ref[hh] = jnp.where(_head_lanes(hh), q, jnp.zeros_like(q)) * jnp.asarray(SB_SCALE, BF16)
        acc_ref[...] = jnp.zeros_like(acc_ref)
        c_ref[...] = jnp.zeros_like(c_ref)
        run_ref[...] = jnp.zeros_like(run_ref)

        def block(jb, diagonal):
            start = pl.multiple_of(jb * ROW_BLK, ROW_BLK)
            kb = k_ref[pl.ds(start, ROW_BLK), :]
            vb = v_ref[pl.ds(start, ROW_BLK), :]
            zs = [_dot_nt(qm_ref[hh], kb) for hh in heads]
            logits = []
            for hh in heads:
                ls, lb = _log_gates(zs[hh])
                if diagonal:
                    ls = jnp.where(lower, ls, 0.0)
                run = run_ref[hh]
                if not diagonal:
                    c_ref[hh] = jnp.where(lane == jb, run, c_ref[hh])
                logits.append(lb + jnp.concatenate([run, run], axis=1) + _tri_sum(ls, umat))
                run_ref[hh] = run + jnp.sum(ls, axis=1, keepdims=True)
            for hh in heads:
                a = jnp.exp(logits[hh])
                if diagonal:
                    a = jnp.where(lower, a, 0.0)
                acc_ref[hh] += _dot(a.astype(BF16), vb)

        block(i, True)

        @pl.loop(0, i)
        def _(n):
            block(i - 1 - n, False)

        o_ref[...] = _merge_heads(acc_ref)

    per_head = (HEADS_PER_BLOCK, ROW_BLK, HEAD_BLK)
    return pl.pallas_call(
        body, name="attn_fwd", grid=(D_SB // HEAD_BLK, t // ROW_BLK),
        in_specs=_qkv_specs(t),
        out_specs=[pl.BlockSpec((ROW_BLK, HEAD_BLK), lambda hp, i: (i, hp)), _carry_spec()],
        out_shape=[jax.ShapeDtypeStruct((t, D_SB), F32),
                   jax.ShapeDtypeStruct((D_SB // HEAD_DIM, t, LANES), F32)],
        scratch_shapes=[pltpu.VMEM(per_head, F32), pltpu.VMEM((HEADS_PER_BLOCK, ROW_BLK, LANES), F32),
                        pltpu.VMEM(per_head, BF16)],
        compiler_params=_cparams("arbitrary", "arbitrary"),
    )(qkv, qkv, qkv)


def _outproj_fwd(h, cout, sraw, sbg, w_out, g_post):
    t = h.shape[0]

    def body(h_ref, c_ref, s_ref, g_ref, w_ref, gp_ref, hn_ref, mixed_ref, mix_ref):
        gate = g_ref[...]
        mix_ref[:, 0:D_CONV] = c_ref[...]
        mix_ref[:, D_CONV:] = (s_ref[...] * (gate * _sigmoid(gate))).astype(BF16)
        mixed = _dot(mix_ref[...], w_ref[...])
        mixed_ref[...] = mixed
        r = lax.rsqrt(jnp.mean(mixed * mixed, axis=-1, keepdims=True) + RMS_EPS)
        hn_ref[...] = h_ref[...] + mixed * r * gp_ref[...]

    return pl.pallas_call(
        body, name="outproj_fwd", grid=(t // ROW_BLK,),
        in_specs=[_rows(D_MODEL), _rows(D_CONV), _rows(D_SB), _rows(D_SB), _whole((D_MODEL, D_MODEL)),
                  _whole((1, D_MODEL))],
        out_specs=[_rows(D_MODEL)] * 3,
        out_shape=[jax.ShapeDtypeStruct((t, D_MODEL), F32), jax.ShapeDtypeStruct((t, D_MODEL), F32),
                   jax.ShapeDtypeStruct((t, D_MODEL), BF16)],
        compiler_params=_cparams("parallel"),
    )(h, cout, sraw, sbg, w_out, g_post)


def _loss_and_grad(h, target, seq):
    t = h.shape[0]

    def body(h_ref, t_ref, loss_ref, dh_ref):
        i = pl.program_id(0)

        @pl.when(i == 0)
        def _():
            loss_ref[...] = jnp.zeros_like(loss_ref)

        row = i * ROW_BLK + lax.broadcasted_iota(jnp.int32, (ROW_BLK, D_MODEL), 0)
        real = (row >= N_META) & (row < N_META + seq)
        diff = jnp.where(real, h_ref[...] - t_ref[...], 0.0)
        sq = jnp.sum(jnp.sum(diff * diff, axis=1, keepdims=True), axis=0, keepdims=True)
        loss_ref[...] += (0.5 / D_MODEL) * sq
        dh_ref[...] = diff * (1.0 / D_MODEL)

    return pl.pallas_call(
        body, name="loss", grid=(t // ROW_BLK,),
        in_specs=[_rows(D_MODEL), _rows(D_MODEL)],
        out_specs=[_whole((1, 1)), _rows(D_MODEL)],
        out_shape=[jax.ShapeDtypeStruct((1, 1), F32), jax.ShapeDtypeStruct((t, D_MODEL), F32)],
        compiler_params=_cparams("arbitrary"),
    )(h, target)


def _outproj_bwd(dh, mixed, g_post, sraw, sbg, w_out_t):
    t = dh.shape[0]

    def body(dh_ref, mixed_ref, gp_ref, s_ref, g_ref, wt_ref, dc_ref, ds_ref, dg_ref, dmb_ref, dgp_ref):
        @pl.when(pl.program_id(0) == 0)
        def _():
            dgp_ref[...] = jnp.zeros_like(dgp_ref)

        mixed = mixed_ref[...]
        r = lax.rsqrt(jnp.mean(mixed * mixed, axis=-1, keepdims=True) + RMS_EPS)
        nh = mixed * r
        dy = dh_ref[...]
        dgp_ref[...] += jnp.sum(dy * nh, axis=0, keepdims=True)
        dn = dy * gp_ref[...]
        dmixed = (r * (dn - nh * jnp.mean(dn * nh, axis=-1, keepdims=True))).astype(BF16)
        dmb_ref[...] = dmixed
        dmix = _dot(dmixed, wt_ref[...])
        dc_ref[...] = dmix[:, 0:D_CONV]
        dsg = dmix[:, D_CONV:]
        gate = g_ref[...]
        sg = _sigmoid(gate)
        ds_ref[...] = dsg * (gate * sg)
        dg_ref[...] = dsg * s_ref[...] * _dsilu(gate, sg)

    return pl.pallas_call(
        body, name="outproj_bwd", grid=(t // ROW_BLK,),
        in_specs=[_rows(D_MODEL), _rows(D_MODEL), _whole((1, D_MODEL)), _rows(D_SB), _rows(D_SB),
                  _whole((D_MODEL, D_MODEL))],
        out_specs=[_rows(D_CONV), _rows(D_SB), _rows(D_SB), _rows(D_MODEL), _whole((1, D_MODEL))],
        out_shape=[jax.ShapeDtypeStruct((t, D_CONV), F32), jax.ShapeDtypeStruct((t, D_SB), F32),
                   jax.ShapeDtypeStruct((t, D_SB), F32), jax.ShapeDtypeStruct((t, D_MODEL), BF16),
                   jax.ShapeDtypeStruct((1, D_MODEL), F32)],
        compiler_params=_cparams("arbitrary"),
    )(dh, mixed, g_post, sraw, sbg, w_out_t)


def _attn_bwd(qkv, carries, do):
    t = qkv.shape[0]

    def body(q_ref, k_ref, v_ref, c_ref, do_ref, dq_ref, dk_ref, dv_ref, acc_ref, seen_ref, qm_ref, dom_ref):
        i = pl.program_id(1)

        @pl.when(i == 0)
        def _():
            dk_ref[...] = jnp.zeros_like(dk_ref)
            dv_ref[...] = jnp.zeros_like(dv_ref)

        lower = _lower_triangle()
        umat = jnp.where(lower, 1.0, 0.0).astype(BF16)
        umat_t = jnp.where(_lower_triangle_t(), 1.0, 0.0).astype(BF16)
        lane = lax.broadcasted_iota(jnp.int32, (ROW_BLK, LANES), 1)
        q = q_ref[...]
        dof = do_ref[...]
        heads = range(HEADS_PER_BLOCK)
        for hh in heads:
            qm_ref[hh] = jnp.where(_head_lanes(hh), q, jnp.zeros_like(q)) * jnp.asarray(SB_SCALE, BF16)
            dom_ref[hh] = jnp.where(_head_lanes(hh), dof, 0.0).astype(BF16)
        acc_ref[...] = jnp.zeros_like(acc_ref)
        seen_ref[...] = jnp.zeros_like(seen_ref)

        def block(jb, diagonal):
            start = pl.multiple_of(jb * ROW_BLK, ROW_BLK)
            kb = k_ref[pl.ds(start, ROW_BLK), :]
            vb = v_ref[pl.ds(start, ROW_BLK), :]
            zs = [_dot_nt(qm_ref[hh], kb) for hh in heads]
            das = [_dot_nt(dom_ref[hh], vb) for hh in heads]
            lbs, logits = [], []
            for hh in heads:
                ls, lb = _log_gates(zs[hh])
                if diagonal:
                    ls = jnp.where(lower, ls, 0.0)
                    logits.append(lb + _tri_sum(ls, umat))
                else:
                    right = jnp.sum(jnp.where(lane == jb, c_ref[hh], 0.0), axis=1, keepdims=True)
                    logits.append(lb + right + _tri_sum(ls, umat))
                lbs.append(lb)
            abs_, gs, befores = [], [], []
            for hh in heads:
                a = jnp.exp(logits[hh])
                if diagonal:
                    a = jnp.where(lower, a, 0.0)
                g = das[hh] * a
                seen = seen_ref[hh]
                befores.append(jnp.concatenate([seen, seen], axis=1) + _tri_sum(g, umat_t))
                seen_ref[hh] = seen + jnp.sum(g, axis=1, keepdims=True)
                abs_.append(a.astype(BF16))
                gs.append(g)
            dk = dv = None
            for hh in heads:
                dz = gs[hh] - jnp.exp(lbs[hh]) * (gs[hh] + befores[hh])
                if diagonal:
                    dz = jnp.where(lower, dz, 0.0)
                dzb = dz.astype(BF16)
                acc_ref[hh] += _dot(dzb, kb)
                dk_h = _dot_tn(dzb, qm_ref[hh])
                dv_h = _dot_tn(abs_[hh], dom_ref[hh])
                dk = dk_h if dk is None else dk + dk_h
                dv = dv_h if dv is None else dv + dv_h
            dk_ref[pl.ds(start, ROW_BLK), :] += dk
            dv_ref[pl.ds(start, ROW_BLK), :] += dv

        @pl.loop(0, i)
        def _(jb):
            block(jb, False)

        block(i, True)
        dq_ref[...] = _merge_heads(acc_ref) * SB_SCALE

    blk = pl.BlockSpec((ROW_BLK, HEAD_BLK), lambda hp, i: (i, hp))
    full = pl.BlockSpec((t, HEAD_BLK), lambda hp, i: (0, hp))
    per_head = (HEADS_PER_BLOCK, ROW_BLK, HEAD_BLK)
    return pl.pallas_call(
        body, name="attn_bwd", grid=(D_SB // HEAD_BLK, t // ROW_BLK),
        in_specs=_qkv_specs(t) + [_carry_spec(), blk],
        out_specs=[blk, full, full],
        out_shape=[jax.ShapeDtypeStruct((t, D_SB), F32)] * 3,
        scratch_shapes=[pltpu.VMEM(per_head, F32), pltpu.VMEM((HEADS_PER_BLOCK, ROW_BLK, LANES), F32),
                        pltpu.VMEM(per_head, BF16), pltpu.VMEM(per_head, BF16)],
        compiler_params=_cparams("arbitrary", "arbitrary"),
    )(qkv, qkv, qkv, carries, do)


def _conv_bwd_rows(dcout, pc, cv, p, ln_g, ln_b, w_pw2_t):
    t = dcout.shape[0]

    def body(dc_ref, gate_ref, cv_ref, p_ref, lg_ref, lb_ref, wt_ref, dcv_ref, dgate_ref, dpb_ref, vec_ref):
        @pl.when(pl.program_id(0) == 0)
        def _():
            vec_ref[...] = jnp.zeros_like(vec_ref)

        dc = dc_ref[...]
        gate = gate_ref[...]
        sg = _sigmoid(gate)
        dp = dc * (gate * sg)
        dgate_ref[...] = dc * p_ref[...] * _dsilu(gate, sg)
        dpb = dp.astype(BF16)
        dpb_ref[...] = dpb
        xh, rstd = _layer_norm_stats(cv_ref[...])
        ln = xh * lg_ref[...] + lb_ref[...]
        s2 = _sigmoid(ln)
        dln = _dot(dpb, wt_ref[...]) * _dsilu(ln, s2)
        dxh = dln * lg_ref[...]
        dcv = rstd * (dxh - jnp.mean(dxh, axis=-1, keepdims=True)
                      - xh * jnp.mean(dxh * xh, axis=-1, keepdims=True))
        dcv_ref[...] = dcv
        vec_ref[0:1, :] += jnp.sum(dp, axis=0, keepdims=True)
        vec_ref[1:2, :] += jnp.sum(dln * xh, axis=0, keepdims=True)
        vec_ref[2:3, :] += jnp.sum(dln, axis=0, keepdims=True)
        vec_ref[3:4, :] += jnp.sum(dcv, axis=0, keepdims=True)

    vec = _whole((1, D_CONV))
    return pl.pallas_call(
        body, name="conv_bwd_rows", grid=(t // ROW_BLK,),
        in_specs=[_rows(D_CONV), _rows(D_CONV, 2), _rows(D_CONV), _rows(D_CONV), vec, vec,
                  _whole((D_CONV, D_CONV))],
        out_specs=[_rows(D_CONV), _rows(D_CONV), _rows(D_CONV), _whole((8, D_CONV))],
        out_shape=[jax.ShapeDtypeStruct((t, D_CONV), F32), jax.ShapeDtypeStruct((t, D_CONV), F32),
                   jax.ShapeDtypeStruct((t, D_CONV), BF16), jax.ShapeDtypeStruct((8, D_CONV), F32)],
        compiler_params=_cparams("arbitrary"),
    )(dcout, pc, cv, p, ln_g, ln_b, w_pw2_t)


def _conv_bwd_taps(dcv, pc, conv_w):
    t = dcv.shape[0]
    n_halo = t // HALO
    per = ROW_BLK // HALO

    def body(d_ref, dn_ref, a_ref, b_ref, ha_ref, hb_ref, cw_ref, da_ref, db_ref, dw_ref, cbuf, dbuf):
        i = pl.program_id(0)

        @pl.when(i == 0)
        def _():
            dw_ref[...] = jnp.zeros_like(dw_ref)

        _fill_glu(cbuf, i, a_ref, b_ref, ha_ref, hb_ref)
        dcv = d_ref[...]
        dbuf[0:ROW_BLK, :] = dcv
        dbuf[ROW_BLK:ROW_BLK + HALO, :] = jnp.where(i < pl.num_programs(0) - 1, dn_ref[...], 0.0)
        acc = jnp.zeros((ROW_BLK, D_CONV), F32)
        for j in range(CONV_WIDTH):
            acc = acc + cw_ref[j:j + 1, :] * dbuf[pl.ds(CONV_WIDTH - 1 - j, ROW_BLK), :]
            seen = cbuf[pl.ds(HALO - (CONV_WIDTH - 1) + j, ROW_BLK), :]
            dw_ref[j:j + 1, :] += jnp.sum(dcv * seen, axis=0, keepdims=True)
        a = a_ref[...]
        sb = _sigmoid(b_ref[...])
        da_ref[...] = acc * sb
        db_ref[...] = acc * a * sb * (1.0 - sb)

    return pl.pallas_call(
        body, name="conv_bwd_taps", grid=(t // ROW_BLK,),
        in_specs=[_rows(D_CONV),
                  pl.BlockSpec((HALO, D_CONV), lambda i: (jnp.minimum((i + 1) * per, n_halo - 1), 0)),
                  _rows(D_CONV, 0), _rows(D_CONV, 1), _prev_halo(0), _prev_halo(1),
                  _whole((CONV_WIDTH, D_CONV))],
        out_specs=[_rows(D_CONV), _rows(D_CONV), _whole((32, D_CONV))],
        out_shape=[jax.ShapeDtypeStruct((t, D_CONV), F32), jax.ShapeDtypeStruct((t, D_CONV), F32),
                   jax.ShapeDtypeStruct((32, D_CONV), F32)],
        scratch_shapes=[pltpu.VMEM((HALO + ROW_BLK, D_CONV), F32), pltpu.VMEM((ROW_BLK + HALO, D_CONV), F32)],
        compiler_params=_cparams("arbitrary"),
    )(dcv, dcv, pc, pc, pc, pc, conv_w)


def _inproj_bwd(dh_out, h, g_pre, pieces, w_in_t):
    t = h.shape[0]

    def body(dh_ref, h_ref, g_ref, *rest):
        piece_refs, (wt_ref, dhin_ref, dproj_ref, dg_ref) = rest[:7], rest[7:]

        @pl.when(pl.program_id(0) == 0)
        def _():
            dg_ref[...] = jnp.zeros_like(dg_ref)

        for k, ref in enumerate(piece_refs):
            dproj_ref[:, 512 * k:512 * (k + 1)] = ref[...].astype(BF16)
        du = _dot(dproj_ref[...], wt_ref[...])
        x = h_ref[...]
        r = lax.rsqrt(jnp.mean(x * x, axis=-1, keepdims=True) + RMS_EPS)
        xh = x * r
        dg_ref[...] += jnp.sum(du * xh, axis=0, keepdims=True)
        dxh = du * g_ref[...]
        dhin_ref[...] = dh_ref[...] + r * (dxh - xh * jnp.mean(dxh * xh, axis=-1, keepdims=True))

    return pl.pallas_call(
        body, name="inproj_bwd", grid=(t // ROW_BLK,),
        in_specs=[_rows(D_MODEL), _rows(D_MODEL), _whole((1, D_MODEL))] + [_rows(512)] * 7
                 + [_whole((D_IN, D_MODEL))],
        out_specs=[_rows(D_MODEL), _rows(D_IN), _whole((1, D_MODEL))],
        out_shape=[jax.ShapeDtypeStruct((t, D_MODEL), F32), jax.ShapeDtypeStruct((t, D_IN), BF16),
                   jax.ShapeDtypeStruct((1, D_MODEL), F32)],
        compiler_params=_cparams("arbitrary"),
    )(dh_out, h, g_pre, *pieces, w_in_t)


def _weight_grad(xb, dyb, name):
    t, k = xb.shape
    n = dyb.shape[1]
    tn = min(n, 512)

    def body(x_ref, dy_ref, o_ref):
        @pl.when(pl.program_id(1) == 0)
        def _():
            o_ref[...] = jnp.zeros_like(o_ref)

        o_ref[...] += _dot_tn(x_ref[...], dy_ref[...])

    return pl.pallas_call(
        body, name=name, grid=(n // tn, t // ROW_BLK),
        in_specs=[pl.BlockSpec((ROW_BLK, k), lambda j, i: (i, 0)), pl.BlockSpec((ROW_BLK, tn), lambda j, i: (i, j))],
        out_specs=pl.BlockSpec((k, tn), lambda j, i: (0, j)),
        out_shape=jax.ShapeDtypeStruct((k, n), F32),
        compiler_params=_cparams("parallel", "arbitrary"),
    )(xb, dyb)


def _position():
    return lax.axis_index("x"), lax.axis_index("y"), lax.axis_index("c")


def _comm_call(body, name, ins, out_shapes):
    n = len(ins)
    hbm = pl.BlockSpec(memory_space=pltpu.HBM)
    return pl.pallas_call(
        functools.partial(body, n), name=name, in_specs=[hbm] * n, out_specs=[hbm] * n, out_shape=out_shapes,
        scratch_shapes=[pltpu.SemaphoreType.DMA((n, N_DEV - 1)), pltpu.SemaphoreType.DMA((n, N_DEV - 1)),
                        pltpu.SemaphoreType.DMA((n,))],
    )(*ins)


def _all_gather(blocks, name):
    def body(n, *refs):
        x_refs, out_refs, (send_sems, recv_sems, local_sems) = refs[:n], refs[n:2 * n], refs[2 * n:]
        x, y, c = _position()
        me, sibling = (x, y, c), (x, y, 1 - c)
        chips = [(1 - x, y), (x, 1 - y), (1 - x, 1 - y)]

        def slot(a, px, py, pc):
            return out_refs[a].at[4 * px + 2 * py + pc]

        def copy(a, k, origin, to, own=False):
            return pltpu.make_async_remote_copy(
                src_ref=x_refs[a] if own else slot(a, *origin), dst_ref=slot(a, *origin),
                send_sem=send_sems.at[a, k], recv_sem=recv_sems.at[a, k], device_id=to, device_id_type=MESH)

        arrays = range(n)
        mine = [pltpu.make_async_copy(x_refs[a], slot(a, *me), local_sems.at[a]) for a in arrays]
        first = [copy(a, 1 + j, me, (*chip, c), own=True) for j, chip in enumerate(chips) for a in arrays]
        first += [copy(a, 0, me, sibling, own=True) for a in arrays]
        for cp in mine + first:
            cp.start()
        passed = []
        for j, chip in enumerate(chips):
            for a in arrays:
                copy(a, 1 + j, (*chip, c), me).wait_recv()
                passed.append(copy(a, 4 + j, (*chip, c), sibling))
                passed[-1].start()
        for a in arrays:
            copy(a, 0, sibling, me).wait_recv()
            for j, chip in enumerate(chips):
                copy(a, 4 + j, (*chip, 1 - c), me).wait_recv()
        for cp in first + passed:
            cp.wait_send()
        for cp in mine:
            cp.wait()

    return _comm_call(body, name, blocks, [jax.ShapeDtypeStruct((N_DEV,) + b.shape, b.dtype) for b in blocks])


def _all_to_all(slabs, name):
    def body(n, *refs):
        g_refs, land_refs, (send_sems, recv_sems, local_sems) = refs[:n], refs[n:2 * n], refs[2 * n:]
        x, y, c = _position()
        me = 4 * x + 2 * y + c
        arrays = range(n)
        mine = [pltpu.make_async_copy(g_refs[a].at[me], land_refs[a].at[me], local_sems.at[a]) for a in arrays]
        peers = []
        for k in range(1, N_DEV):
            px = 1 - x if k & 4 else x
            py = 1 - y if k & 2 else y
            pc = 1 - c if k & 1 else c
            peers.append((k - 1, (px, py, pc), 4 * px + 2 * py + pc))

        def copy(a, s, dev, idx, landing):
            return pltpu.make_async_remote_copy(
                src_ref=g_refs[a].at[idx], dst_ref=land_refs[a].at[landing], send_sem=send_sems.at[a, s],
                recv_sem=recv_sems.at[a, s], device_id=dev, device_id_type=MESH)

        sends = [copy(a, s, dev, idx, me) for s, dev, idx in peers for a in arrays]
        for cp in mine + sends:
            cp.start()
        for s, dev, idx in peers:
            for a in arrays:
                copy(a, s, dev, idx, idx).wait_recv()
        for cp in sends:
            cp.wait_send()
        for cp in mine:
            cp.wait()

    return _comm_call(body, name, slabs, [jax.ShapeDtypeStruct(g.shape, g.dtype) for g in slabs])


def _block_rows(r, row_bytes, budget=1 << 20):
    cap = max(8, budget // row_bytes)
    return max(d for d in range(8, min(r, cap) + 1, 8) if r % d == 0)


def _sum_adamw(parts, w, m, v, name):
    n_parts, r, c = parts.shape
    br = _block_rows(r, 4 * c)

    def body(p_ref, w_ref, m_ref, v_ref, g_out, d_out, m_out, v_out):
        g = p_ref[0]
        for s in range(1, n_parts):
            g = g + p_ref[s]
        m_new = ADAM_B1 * m_ref[...] + (1.0 - ADAM_B1) * g
        v_new = ADAM_B2 * v_ref[...] + (1.0 - ADAM_B2) * (g * g)
        m_hat = m_new / (1.0 - ADAM_B1 ** ADAM_STEP)
        v_hat = v_new / (1.0 - ADAM_B2 ** ADAM_STEP)
        g_out[...] = g
        d_out[...] = -ADAM_LR * (m_hat / (jnp.sqrt(v_hat) + ADAM_EPS) + ADAM_WD * w_ref[...])
        m_out[...] = m_new
        v_out[...] = v_new

    row = pl.BlockSpec((br, c), lambda i: (i, 0))
    return pl.pallas_call(
        body, name=name, grid=(r // br,),
        in_specs=[pl.BlockSpec((n_parts, br, c), lambda i: (0, i, 0)), row, row, row],
        out_specs=[row] * 4, out_shape=[jax.ShapeDtypeStruct((r, c), F32)] * 4,
        compiler_params=_cparams("parallel"),
    )(parts, w, m, v)


def _sum_parts(parts, name):
    n_parts, r, c = parts.shape

    def body(p_ref, o_ref):
        g = p_ref[0]
        for s in range(1, n_parts):
            g = g + p_ref[s]
        o_ref[...] = g

    return pl.pallas_call(
        body, name=name, in_specs=[pl.BlockSpec(memory_space=pltpu.VMEM)],
        out_specs=pl.BlockSpec(memory_space=pltpu.VMEM), out_shape=jax.ShapeDtypeStruct((r, c), F32),
    )(parts)


def _pack(arrays):
    flat = jnp.concatenate([a.reshape(-1) for a in arrays])
    pad = -flat.shape[0] % (8 * LANES)
    if pad:
        flat = jnp.pad(flat, (0, pad))
    return flat.reshape(-1, LANES)


def _unpack(buf, shapes):
    flat = buf.reshape(-1)
    out, at = [], 0
    for shape in shapes:
        size = 1
        for d in shape:
            size *= d
        out.append(lax.slice_in_dim(flat, at, at + size).reshape(shape))
        at += size
    return out


def _local_step(x, target, meta, pre_g, post_g, w_in, conv_w, conv_b, ln_g, ln_b, w_pw2, b_pw2, w_out):
    depth = w_in.shape[0]
    seq = x.shape[0]
    t = -(-(N_META + seq) // ROW_BLK) * ROW_BLK
    tail = t - N_META - seq
    h = jnp.concatenate([meta, x, jnp.zeros((tail, D_MODEL), F32)], axis=0)
    target = jnp.pad(target, ((N_META, tail), (0, 0)))
    w_in_t = jnp.swapaxes(w_in, 1, 2)
    w_pw2_t = jnp.swapaxes(w_pw2, 1, 2)
    w_out_t = jnp.swapaxes(w_out, 1, 2)
    row = lambda a, l: a[l][None, :]

    saved = []
    for l in range(depth):
        pc, qkv, sbg, u = _inproj_fwd(h, row(pre_g, l), w_in[l])
        cout, cv, p, sl = _conv_fwd(pc, conv_w[l], row(conv_b, l), row(ln_g, l), row(ln_b, l), w_pw2[l],
                                    row(b_pw2, l))
        sraw, carries = _attn_fwd(qkv)
        h_new, mixed, mix = _outproj_fwd(h, cout, sraw, sbg, w_out[l], row(post_g, l))
        saved.append((h, pc, qkv, sbg, u, cv, p, sl, sraw, carries, mixed, mix))
        h = h_new

    loss, dh = _loss_and_grad(h, target, seq)

    grads = {k: [None] * depth for k in ("pre_g", "post_g", "w_in", "conv_w", "conv_b", "ln_g", "ln_b", "w_pw2",
                                          "b_pw2", "w_out")}
    for l in reversed(range(depth)):
        h_in, pc, qkv, sbg, u, cv, p, sl, sraw, carries, mixed, mix = saved[l]
        dcout, dsraw, dsbg, dmixed, dg_post = _outproj_bwd(dh, mixed, row(post_g, l), sraw, sbg, w_out_t[l])
        dq, dk, dv = _attn_bwd(qkv, carries, dsraw)
        dcv, dgate, dpb, vecs = _conv_bwd_rows(dcout, pc, cv, p, row(ln_g, l), row(ln_b, l), w_pw2_t[l])
        da, db, dconv_w = _conv_bwd_taps(dcv, pc, conv_w[l])
        dh, dproj, dg_pre = _inproj_bwd(dh, h_in, row(pre_g, l), (da, db, dgate, dq, dk, dv, dsbg), w_in_t[l])
        grads["w_in"][l] = _weight_grad(u, dproj, "w_in_grad")
        grads["w_out"][l] = _weight_grad(mix, dmixed, "w_out_grad")
        grads["w_pw2"][l] = _weight_grad(sl, dpb, "w_pw2_grad")
        grads["pre_g"][l] = dg_pre[0]
        grads["post_g"][l] = dg_post[0]
        grads["b_pw2"][l], grads["ln_g"][l], grads["ln_b"][l], grads["conv_b"][l] = vecs[0], vecs[1], vecs[2], vecs[3]
        grads["conv_w"][l] = dconv_w[:CONV_WIDTH]

    grads = {k: jnp.stack(v) for k, v in grads.items()}
    grads["meta"] = dh[:N_META]
    return loss[0, 0], dh[N_META:N_META + seq], grads


def _shard_major(full, axis):
    shape = full.shape
    split = full.reshape(shape[:axis] + (N_DEV, shape[axis] // N_DEV) + shape[axis + 1:])
    return jnp.moveaxis(split, axis, 0)


def _whole_from_shards(shards, axis):
    moved = jnp.moveaxis(shards, 0, axis)
    shape = moved.shape
    return moved.reshape(shape[:axis] + (shape[axis] * shape[axis + 1],) + shape[axis + 2:])


def kernel(x, meta_tokens, pre_norm_g, post_norm_g, w_in, conv_w, conv_b, conv_ln_g, conv_ln_b, w_pw2, b_pw2, w_out, loss_target, m_meta_tokens, m_pre_norm_g, m_post_norm_g, m_w_in, m_conv_w, m_conv_b, m_conv_ln_g, m_conv_ln_b, m_w_pw2, m_b_pw2, m_w_out, v_meta_tokens, v_pre_norm_g, v_post_norm_g, v_w_in, v_conv_w, v_conv_b, v_conv_ln_g, v_conv_ln_b, v_w_pw2, v_b_pw2, v_w_out):
    me = 4 * lax.axis_index("x") + 2 * lax.axis_index("y") + lax.axis_index("c")

    w_in_s, w_out_s, w_pw2_s, conv_w_s, meta_s = _all_gather(
        [w_in.astype(BF16), w_out.astype(BF16), w_pw2.astype(BF16), conv_w, meta_tokens], "gather_weights")
    w_in_full = _whole_from_shards(w_in_s, 2)
    w_out_full = _whole_from_shards(w_out_s, 1)
    w_pw2_full = _whole_from_shards(w_pw2_s, 1)
    conv_w_full = _whole_from_shards(conv_w_s, 2)
    meta_full = _whole_from_shards(meta_s, 1)

    loss, dx, grads = _local_step(x[0], loss_target[0], meta_full, pre_norm_g, post_norm_g, w_in_full, conv_w_full,
                                  conv_b, conv_ln_g, conv_ln_b, w_pw2_full, b_pw2, w_out_full)
    loss = lax.psum(loss, ("x", "y", "c"))

    landed = _all_to_all([_shard_major(grads["w_in"], 2), _shard_major(grads["w_out"], 1),
                          _shard_major(grads["w_pw2"], 1)], "exchange_matmul_grads")

    def update(parts, w, m, v, name):
        rows = lambda a: a.reshape(-1, a.shape[-1])
        outs = _sum_adamw(parts.reshape(N_DEV, -1, parts.shape[-1]), rows(w), rows(m), rows(v), name)
        return [o.reshape(w.shape) for o in outs]

    g_w_in, d_w_in, nm_w_in, nv_w_in = update(landed[0], w_in, m_w_in, v_w_in, "adamw_w_in")
    g_w_out, d_w_out, nm_w_out, nv_w_out = update(landed[1], w_out, m_w_out, v_w_out, "adamw_w_out")
    g_w_pw2, d_w_pw2, nm_w_pw2, nv_w_pw2 = update(landed[2], w_pw2, m_w_pw2, v_w_pw2, "adamw_w_pw2")

    small_names = ("pre_g", "post_g", "conv_b", "ln_g", "ln_b", "b_pw2", "conv_w", "meta")
    small_full = [grads[k] for k in small_names]
    gathered, = _all_gather([_pack(small_full)], "gather_small_grads")
    summed = _sum_parts(gathered, "sum_small_grads")
    g_small = dict(zip(small_names, _unpack(summed, [a.shape for a in small_full])))
    g_small["conv_w"] = lax.dynamic_slice_in_dim(g_small["conv_w"], me * conv_w.shape[2], conv_w.shape[2], axis=2)
    g_small["meta"] = lax.dynamic_slice_in_dim(g_small["meta"], me * meta_tokens.shape[1], meta_tokens.shape[1], axis=1)
    small_w = dict(zip(small_names, (pre_norm_g, post_norm_g, conv_b, conv_ln_g, conv_ln_b, b_pw2, conv_w, meta_tokens)))
    small_m = (m_pre_norm_g, m_post_norm_g, m_conv_b, m_conv_ln_g, m_conv_ln_b, m_b_pw2, m_conv_w, m_meta_tokens)
    small_v = (v_pre_norm_g, v_post_norm_g, v_conv_b, v_conv_ln_g, v_conv_ln_b, v_b_pw2, v_conv_w, v_meta_tokens)
    small_shapes = [small_w[k].shape for k in small_names]
    outs = _sum_adamw(_pack([g_small[k] for k in small_names])[None], _pack([small_w[k] for k in small_names]),
                      _pack(small_m), _pack(small_v), "adamw_small_weights")
    g_s, d_s, nm_s, nv_s = [dict(zip(small_names, _unpack(o, small_shapes))) for o in outs]

    def ordered(s, w_in_, w_pw2_, w_out_):
        return (s["meta"], s["pre_g"], s["post_g"], w_in_, s["conv_w"], s["conv_b"], s["ln_g"], s["ln_b"], w_pw2_,
                s["b_pw2"], w_out_)

    return (loss, dx[None], *ordered(g_s, g_w_in, g_w_pw2, g_w_out), *ordered(d_s, d_w_in, d_w_pw2, d_w_out),
            *ordered(nm_s, nm_w_in, nm_w_pw2, nm_w_out), *ordered(nv_s, nv_w_in, nv_w_pw2, nv_w_out))
```

```python
import functools

import jax
import jax.numpy as jnp
from jax import lax
from jax.experimental import pallas as pl
from jax.experimental.pallas import tpu as pltpu

F32 = jnp.float32
BF16 = jnp.bfloat16

D_MODEL = 1024
D_CONV = 512
D_SB = 512
HEAD_DIM = 64
HEADS_PER_BLOCK = 4
HEAD_BLK = HEADS_PER_BLOCK * HEAD_DIM
CONV_WIDTH = 31
N_META = 16
D_IN = 3 * D_CONV + 4 * D_SB
RMS_EPS = 1e-6
LN_EPS = 1e-5
SB_SCALE = HEAD_DIM ** -0.5

ADAM_LR = 0.001
ADAM_B1 = 0.9
ADAM_B2 = 0.999
ADAM_EPS = 1e-08
ADAM_WD = 0.01
ADAM_STEP = 10

N_DEV = 8
LANES = 128
ROW_BLK = 256
HALO = 32
VMEM_LIMIT = 56 * 1024 * 1024
MESH = pl.DeviceIdType.MESH


def _cparams(*sem):
    return pltpu.CompilerParams(dimension_semantics=sem, vmem_limit_bytes=VMEM_LIMIT)


def _rows(n_cols, col=0):
    return pl.BlockSpec((ROW_BLK, n_cols), lambda i, col=col: (i, col))


def _whole(shape):
    return pl.BlockSpec(shape, lambda i: (0,) * len(shape))


def _sigmoid(x):
    return jax.nn.sigmoid(x)


def _dsilu(x, s):
    return s * (1.0 + x * (1.0 - s))


def _dot(a, b):
    return jnp.dot(a, b, preferred_element_type=F32)


def _dot_nt(a, b):
    return lax.dot_general(a, b, (((1,), (1,)), ((), ())), preferred_element_type=F32)


def _dot_tn(a, b):
    return lax.dot_general(a, b, (((0,), (0,)), ((), ())), preferred_element_type=F32)


def _inproj_fwd(h, g_pre, w_in):
    t = h.shape[0]

    def body(h_ref, g_ref, w_ref, pc_ref, qkv_ref, sbg_ref, u_ref):
        x = h_ref[...]
        r = lax.rsqrt(jnp.mean(x * x, axis=-1, keepdims=True) + RMS_EPS)
        u = (x * r * g_ref[...]).astype(BF16)
        u_ref[...] = u
        pc_ref[...] = _dot(u, w_ref[:, 0:1536])
        qkv_ref[...] = _dot(u, w_ref[:, 1536:3072]).astype(BF16)
        sbg_ref[...] = _dot(u, w_ref[:, 3072:3584])

    return pl.pallas_call(
        body, name="inproj_fwd", grid=(t // ROW_BLK,),
        in_specs=[_rows(D_MODEL), _whole((1, D_MODEL)), _whole((D_MODEL, D_IN))],
        out_specs=[_rows(1536), _rows(1536), _rows(D_SB), _rows(D_MODEL)],
        out_shape=[jax.ShapeDtypeStruct((t, 1536), F32), jax.ShapeDtypeStruct((t, 1536), BF16),
                   jax.ShapeDtypeStruct((t, D_SB), F32), jax.ShapeDtypeStruct((t, D_MODEL), BF16)],
        compiler_params=_cparams("parallel"),
    )(h, g_pre, w_in)


def _prev_halo(col):
    per = ROW_BLK // HALO
    return pl.BlockSpec((HALO, D_CONV), lambda i, col=col: (jnp.maximum(i * per - 1, 0), col))


def _fill_glu(buf, i, a_ref, b_ref, ha_ref, hb_ref):
    halo = ha_ref[...] * _sigmoid(hb_ref[...])
    buf[0:HALO, :] = jnp.where(i > 0, halo, 0.0)
    buf[HALO:HALO + ROW_BLK, :] = a_ref[...] * _sigmoid(b_ref[...])


def _layer_norm_stats(cv):
    mu = jnp.mean(cv, axis=-1, keepdims=True)
    xc = cv - mu
    rstd = lax.rsqrt(jnp.mean(xc * xc, axis=-1, keepdims=True) + LN_EPS)
    return xc * rstd, rstd


def _conv_fwd(pc, conv_w, conv_b, ln_g, ln_b, w_pw2, b_pw2):
    t = pc.shape[0]

    def body(a_ref, b_ref, gate_ref, ha_ref, hb_ref, cw_ref, cb_ref, lg_ref, lb_ref, wp_ref, bp_ref,
             cout_ref, cv_ref, p_ref, sl_ref, buf):
        i = pl.program_id(0)
        _fill_glu(buf, i, a_ref, b_ref, ha_ref, hb_ref)
        acc = jnp.zeros((ROW_BLK, D_CONV), F32) + cb_ref[...]
        for j in range(CONV_WIDTH):
            acc = acc + cw_ref[j:j + 1, :] * buf[pl.ds(HALO - (CONV_WIDTH - 1) + j, ROW_BLK), :]
        cv_ref[...] = acc
        xh, _ = _layer_norm_stats(acc)
        ln = xh * lg_ref[...] + lb_ref[...]
        sl = (ln * _sigmoid(ln)).astype(BF16)
        sl_ref[...] = sl
        p = _dot(sl, wp_ref[...]) + bp_ref[...]
        p_ref[...] = p
        gate = gate_ref[...]
        cout_ref[...] = (p * (gate * _sigmoid(gate))).astype(BF16)

    vec = _whole((1, D_CONV))
    return pl.pallas_call(
        body, name="conv_fwd", grid=(t // ROW_BLK,),
        in_specs=[_rows(D_CONV, 0), _rows(D_CONV, 1), _rows(D_CONV, 2), _prev_halo(0), _prev_halo(1),
                  _whole((CONV_WIDTH, D_CONV)), vec, vec, vec, _whole((D_CONV, D_CONV)), vec],
        out_specs=[_rows(D_CONV)] * 4,
        out_shape=[jax.ShapeDtypeStruct((t, D_CONV), BF16), jax.ShapeDtypeStruct((t, D_CONV), F32),
                   jax.ShapeDtypeStruct((t, D_CONV), F32), jax.ShapeDtypeStruct((t, D_CONV), BF16)],
        scratch_shapes=[pltpu.VMEM((HALO + ROW_BLK, D_CONV), F32)],
        compiler_params=_cparams("parallel"),
    )(pc, pc, pc, pc, pc, conv_w, conv_b, ln_g, ln_b, w_pw2, b_pw2)


def _lower_triangle():
    row = lax.broadcasted_iota(jnp.int32, (ROW_BLK, ROW_BLK), 0)
    col = lax.broadcasted_iota(jnp.int32, (ROW_BLK, ROW_BLK), 1)
    return row > col


def _lower_triangle_t():
    row = lax.broadcasted_iota(jnp.int32, (ROW_BLK, ROW_BLK), 0)
    col = lax.broadcasted_iota(jnp.int32, (ROW_BLK, ROW_BLK), 1)
    return row < col


def _tri_sum(x, umat):
    return _dot(x.astype(BF16), umat)


def _log_gates(z):
    ls = -(jnp.maximum(z, 0.0) + jnp.log(1.0 + jnp.exp(-jnp.abs(z))))
    return ls, z + ls


def _head_lanes(hh):
    lane = lax.broadcasted_iota(jnp.int32, (ROW_BLK, HEAD_BLK), 1)
    return (lane >= HEAD_DIM * hh) & (lane < HEAD_DIM * (hh + 1))


def _merge_heads(acc_ref):
    out = acc_ref[HEADS_PER_BLOCK - 1]
    for hh in range(HEADS_PER_BLOCK - 1):
        out = jnp.where(_head_lanes(hh), acc_ref[hh], out)
    return out


def _qkv_specs(t):
    n_blk = D_SB // HEAD_BLK
    return [pl.BlockSpec((ROW_BLK, HEAD_BLK), lambda hp, i: (i, hp)),
            pl.BlockSpec((t, HEAD_BLK), lambda hp, i: (0, n_blk + hp)),
            pl.BlockSpec((t, HEAD_BLK), lambda hp, i: (0, 2 * n_blk + hp))]


def _carry_spec():
    return pl.BlockSpec((HEADS_PER_BLOCK, ROW_BLK, LANES), lambda hp, i: (hp, i, 0))


def _attn_fwd(qkv):
    t = qkv.shape[0]
    assert t // ROW_BLK <= LANES

    def body(q_ref, k_ref, v_ref, o_ref, c_ref, acc_ref, run_ref, qm_ref):
        i = pl.program_id(1)
        lower = _lower_triangle()
        umat = jnp.where(lower, 1.0, 0.0).astype(BF16)
        lane = lax.broadcasted_iota(jnp.int32, (ROW_BLK, LANES), 1)
        q = q_ref[...]
        heads = range(HEADS_PER_BLOCK)
        for hh in heads:
            qm_ref[hh] = jnp.where(_head_lanes(hh), q, jnp.zeros_like(q)) * jnp.asarray(SB_SCALE, BF16)
        acc_ref[...] = jnp.zeros_like(acc_ref)
        c_ref[...] = jnp.zeros_like(c_ref)
        run_ref[...] = jnp.zeros_like(run_ref)

        def block(jb, diagonal):
            start = pl.multiple_of(jb * ROW_BLK, ROW_BLK)
            kb = k_ref[pl.ds(start, ROW_BLK), :]
            vb = v_ref[pl.ds(start, ROW_BLK), :]
            zs = [_dot_nt(qm_ref[hh], kb) for hh in heads]
            logits = []
            for hh in heads:
                ls, lb = _log_gates(zs[hh])
                if diagonal:
                    ls = jnp.where(lower, ls, 0.0)
                run = run_ref[hh]
                if not diagonal:
                    c_ref[hh] = jnp.where(lane == jb, run, c_ref[hh])
                logits.append(lb + jnp.concatenate([run, run], axis=1) + _tri_sum(ls, umat))
                run_ref[hh] = run + jnp.sum(ls, axis=1, keepdims=True)
            for hh in heads:
                a = jnp.exp(logits[hh])
                if diagonal:
                    a = jnp.where(lower, a, 0.0)
                acc_ref[hh] += _dot(a.astype(BF16), vb)

        block(i, True)

        @pl.loop(0, i)
        def _(n):
            block(i - 1 - n, False)

        o_ref[...] = _merge_heads(acc_ref)

    per_head = (HEADS_PER_BLOCK, ROW_BLK, HEAD_BLK)
    return pl.pallas_call(
        body, name="attn_fwd", grid=(D_SB // HEAD_BLK, t // ROW_BLK),
        in_specs=_qkv_specs(t),
        out_specs=[pl.BlockSpec((ROW_BLK, HEAD_BLK), lambda hp, i: (i, hp)), _carry_spec()],
        out_shape=[jax.ShapeDtypeStruct((t, D_SB), F32),
                   jax.ShapeDtypeStruct((D_SB // HEAD_DIM, t, LANES), F32)],
        scratch_shapes=[pltpu.VMEM(per_head, F32), pltpu.VMEM((HEADS_PER_BLOCK, ROW_BLK, LANES), F32),
                        pltpu.VMEM(per_head, BF16)],
        compiler_params=_cparams("arbitrary", "arbitrary"),
    )(qkv, qkv, qkv)


def _outproj_fwd(h, cout, sraw, sbg, w_out, g_post):
    t = h.shape[0]

    def body(h_ref, c_ref, s_ref, g_ref, w_ref, gp_ref, hn_ref, mixed_ref, mix_ref):
        gate = g_ref[...]
        mix_ref[:, 0:D_CONV] = c_ref[...]
        mix_ref[:, D_CONV:] = (s_ref[...] * (gate * _sigmoid(gate))).astype(BF16)
        mixed = _dot(mix_ref[...], w_ref[...])
        mixed_ref[...] = mixed
        r = lax.rsqrt(jnp.mean(mixed * mixed, axis=-1, keepdims=True) + RMS_EPS)
        hn_ref[...] = h_ref[...] + mixed * r * gp_ref[...]

    return pl.pallas_call(
        body, name="outproj_fwd", grid=(t // ROW_BLK,),
        in_specs=[_rows(D_MODEL), _rows(D_CONV), _rows(D_SB), _rows(D_SB), _whole((D_MODEL, D_MODEL)),
                  _whole((1, D_MODEL))],
        out_specs=[_rows(D_MODEL)] * 3,
        out_shape=[jax.ShapeDtypeStruct((t, D_MODEL), F32), jax.ShapeDtypeStruct((t, D_MODEL), F32),
                   jax.ShapeDtypeStruct((t, D_MODEL), BF16)],
        compiler_params=_cparams("parallel"),
    )(h, cout, sraw, sbg, w_out, g_post)


def _loss_and_grad(h, target, seq):
    t = h.shape[0]

    def body(h_ref, t_ref, loss_ref, dh_ref):
        i = pl.program_id(0)

        @pl.when(i == 0)
        def _():
            loss_ref[...] = jnp.zeros_like(loss_ref)

        row = i * ROW_BLK + lax.broadcasted_iota(jnp.int32, (ROW_BLK, D_MODEL), 0)
        real = (row >= N_META) & (row < N_META + seq)
        diff = jnp.where(real, h_ref[...] - t_ref[...], 0.0)
        sq = jnp.sum(jnp.sum(diff * diff, axis=1, keepdims=True), axis=0, keepdims=True)
        loss_ref[...] += (0.5 / D_MODEL) * sq
        dh_ref[...] = diff * (1.0 / D_MODEL)

    return pl.pallas_call(
        body, name="loss", grid=(t // ROW_BLK,),
        in_specs=[_rows(D_MODEL), _rows(D_MODEL)],
        out_specs=[_whole((1, 1)), _rows(D_MODEL)],
        out_shape=[jax.ShapeDtypeStruct((1, 1), F32), jax.ShapeDtypeStruct((t, D_MODEL), F32)],
        compiler_params=_cparams("arbitrary"),
    )(h, target)


def _outproj_bwd(dh, mixed, g_post, sraw, sbg, w_out_t):
    t = dh.shape[0]

    def body(dh_ref, mixed_ref, gp_ref, s_ref, g_ref, wt_ref, dc_ref, ds_ref, dg_ref, dmb_ref, dgp_ref):
        @pl.when(pl.program_id(0) == 0)
        def _():
            dgp_ref[...] = jnp.zeros_like(dgp_ref)

        mixed = mixed_ref[...]
        r = lax.rsqrt(jnp.mean(mixed * mixed, axis=-1, keepdims=True) + RMS_EPS)
        nh = mixed * r
        dy = dh_ref[...]
        dgp_ref[...] += jnp.sum(dy * nh, axis=0, keepdims=True)
        dn = dy * gp_ref[...]
        dmixed = (r * (dn - nh * jnp.mean(dn * nh, axis=-1, keepdims=True))).astype(BF16)
        dmb_ref[...] = dmixed
        dmix = _dot(dmixed, wt_ref[...])
        dc_ref[...] = dmix[:, 0:D_CONV]
        dsg = dmix[:, D_CONV:]
        gate = g_ref[...]
        sg = _sigmoid(gate)
        ds_ref[...] = dsg * (gate * sg)
        dg_ref[...] = dsg * s_ref[...] * _dsilu(gate, sg)

    return pl.pallas_call(
        body, name="outproj_bwd", grid=(t // ROW_BLK,),
        in_specs=[_rows(D_MODEL), _rows(D_MODEL), _whole((1, D_MODEL)), _rows(D_SB), _rows(D_SB),
                  _whole((D_MODEL, D_MODEL))],
        out_specs=[_rows(D_CONV), _rows(D_SB), _rows(D_SB), _rows(D_MODEL), _whole((1, D_MODEL))],
        out_shape=[jax.ShapeDtypeStruct((t, D_CONV), F32), jax.ShapeDtypeStruct((t, D_SB), F32),
                   jax.ShapeDtypeStruct((t, D_SB), F32), jax.ShapeDtypeStruct((t, D_MODEL), BF16),
                   jax.ShapeDtypeStruct((1, D_MODEL), F32)],
        compiler_params=_cparams("arbitrary"),
    )(dh, mixed, g_post, sraw, sbg, w_out_t)


def _attn_bwd(qkv, carries, do):
    t = qkv.shape[0]

    def body(q_ref, k_ref, v_ref, c_ref, do_ref, dq_ref, dk_ref, dv_ref, acc_ref, seen_ref, qm_ref, dom_ref):
        i = pl.program_id(1)

        @pl.when(i == 0)
        def _():
            dk_ref[...] = jnp.zeros_like(dk_ref)
            dv_ref[...] = jnp.zeros_like(dv_ref)

        lower = _lower_triangle()
        umat = jnp.where(lower, 1.0, 0.0).astype(BF16)
        umat_t = jnp.where(_lower_triangle_t(), 1.0, 0.0).astype(BF16)
        lane = lax.broadcasted_iota(jnp.int32, (ROW_BLK, LANES), 1)
        q = q_ref[...]
        dof = do_ref[...]
        heads = range(HEADS_PER_BLOCK)
        for hh in heads:
            qm_ref[hh] = jnp.where(_head_lanes(hh), q, jnp.zeros_like(q)) * jnp.asarray(SB_SCALE, BF16)
            dom_ref[hh] = jnp.where(_head_lanes(hh), dof, 0.0).astype(BF16)
        acc_ref[...] = jnp.zeros_like(acc_ref)
        seen_ref[...] = jnp.zeros_like(seen_ref)

        def block(jb, diagonal):
            start = pl.multiple_of(jb * ROW_BLK, ROW_BLK)
            kb = k_ref[pl.ds(start, ROW_BLK), :]
            vb = v_ref[pl.ds(start, ROW_BLK), :]
            zs = [_dot_nt(qm_ref[hh], kb) for hh in heads]
            das = [_dot_nt(dom_ref[hh], vb) for hh in heads]
            lbs, logits = [], []
            for hh in heads:
                ls, lb = _log_gates(zs[hh])
                if diagonal:
                    ls = jnp.where(lower, ls, 0.0)
                    logits.append(lb + _tri_sum(ls, umat))
                else:
                    right = jnp.sum(jnp.where(lane == jb, c_ref[hh], 0.0), axis=1, keepdims=True)
                    logits.append(lb + right + _tri_sum(ls, umat))
                lbs.append(lb)
            abs_, gs, befores = [], [], []
            for hh in heads:
                a = jnp.exp(logits[hh])
                if diagonal:
                    a = jnp.where(lower, a, 0.0)
                g = das[hh] * a
                seen = seen_ref[hh]
                befores.append(jnp.concatenate([seen, seen], axis=1) + _tri_sum(g, umat_t))
                seen_ref[hh] = seen + jnp.sum(g, axis=1, keepdims=True)
                abs_.append(a.astype(BF16))
                gs.append(g)
            dk = dv = None
            for hh in heads:
                dz = gs[hh] - jnp.exp(lbs[hh]) * (gs[hh] + befores[hh])
                if diagonal:
                    dz = jnp.where(lower, dz, 0.0)
                dzb = dz.astype(BF16)
                acc_ref[hh] += _dot(dzb, kb)
                dk_h = _dot_tn(dzb, qm_ref[hh])
                dv_h = _dot_tn(abs_[hh], dom_ref[hh])
                dk = dk_h if dk is None else dk + dk_h
                dv = dv_h if dv is None else dv + dv_h
            dk_ref[pl.ds(start, ROW_BLK), :] += dk
            dv_ref[pl.ds(start, ROW_BLK), :] += dv

        @pl.loop(0, i)
        def _(jb):
            block(jb, False)

        block(i, True)
        dq_ref[...] = _merge_heads(acc_ref) * SB_SCALE

    blk = pl.BlockSpec((ROW_BLK, HEAD_BLK), lambda hp, i: (i, hp))
    full = pl.BlockSpec((t, HEAD_BLK), lambda hp, i: (0, hp))
    per_head = (HEADS_PER_BLOCK, ROW_BLK, HEAD_BLK)
    return pl.pallas_call(
        body, name="attn_bwd", grid=(D_SB // HEAD_BLK, t // ROW_BLK),
        in_specs=_qkv_specs(t) + [_carry_spec(), blk],
        out_specs=[blk, full, full],
        out_shape=[jax.ShapeDtypeStruct((t, D_SB), F32)] * 3,
        scratch_shapes=[pltpu.VMEM(per_head, F32), pltpu.VMEM((HEADS_PER_BLOCK, ROW_BLK, LANES), F32),
                        pltpu.VMEM(per_head, BF16), pltpu.VMEM(per_head, BF16)],
        compiler_params=_cparams("arbitrary", "arbitrary"),
    )(qkv, qkv, qkv, carries, do)


def _conv_bwd_rows(dcout, pc, cv, p, ln_g, ln_b, w_pw2_t):
    t = dcout.shape[0]

    def body(dc_ref, gate_ref, cv_ref, p_ref, lg_ref, lb_ref, wt_ref, dcv_ref, dgate_ref, dpb_ref, vec_ref):
        @pl.when(pl.program_id(0) == 0)
        def _():
            vec_ref[...] = jnp.zeros_like(vec_ref)

        dc = dc_ref[...]
        gate = gate_ref[...]
        sg = _sigmoid(gate)
        dp = dc * (gate * sg)
        dgate_ref[...] = dc * p_ref[...] * _dsilu(gate, sg)
        dpb = dp.astype(BF16)
        dpb_ref[...] = dpb
        xh, rstd = _layer_norm_stats(cv_ref[...])
        ln = xh * lg_ref[...] + lb_ref[...]
        s2 = _sigmoid(ln)
        dln = _dot(dpb, wt_ref[...]) * _dsilu(ln, s2)
        dxh = dln * lg_ref[...]
        dcv = rstd * (dxh - jnp.mean(dxh, axis=-1, keepdims=True)
                      - xh * jnp.mean(dxh * xh, axis=-1, keepdims=True))
        dcv_ref[...] = dcv
        vec_ref[0:1, :] += jnp.sum(dp, axis=0, keepdims=True)
        vec_ref[1:2, :] += jnp.sum(dln * xh, axis=0, keepdims=True)
        vec_ref[2:3, :] += jnp.sum(dln, axis=0, keepdims=True)
        vec_ref[3:4, :] += jnp.sum(dcv, axis=0, keepdims=True)

    vec = _whole((1, D_CONV))
    return pl.pallas_call(
        body, name="conv_bwd_rows", grid=(t // ROW_BLK,),
        in_specs=[_rows(D_CONV), _rows(D_CONV, 2), _rows(D_CONV), _rows(D_CONV), vec, vec,
                  _whole((D_CONV, D_CONV))],
        out_specs=[_rows(D_CONV), _rows(D_CONV), _rows(D_CONV), _whole((8, D_CONV))],
        out_shape=[jax.ShapeDtypeStruct((t, D_CONV), F32), jax.ShapeDtypeStruct((t, D_CONV), F32),
                   jax.ShapeDtypeStruct((t, D_CONV), BF16), jax.ShapeDtypeStruct((8, D_CONV), F32)],
        compiler_params=_cparams("arbitrary"),
    )(dcout, pc, cv, p, ln_g, ln_b, w_pw2_t)


def _conv_bwd_taps(dcv, pc, conv_w):
    t = dcv.shape[0]
    n_halo = t // HALO
    per = ROW_BLK // HALO

    def body(d_ref, dn_ref, a_ref, b_ref, ha_ref, hb_ref, cw_ref, da_ref, db_ref, dw_ref, cbuf, dbuf):
        i = pl.program_id(0)

        @pl.when(i == 0)
        def _():
            dw_ref[...] = jnp.zeros_like(dw_ref)

        _fill_glu(cbuf, i, a_ref, b_ref, ha_ref, hb_ref)
        dcv = d_ref[...]
        dbuf[0:ROW_BLK, :] = dcv
        dbuf[ROW_BLK:ROW_BLK + HALO, :] = jnp.where(i < pl.num_programs(0) - 1, dn_ref[...], 0.0)
        acc = jnp.zeros((ROW_BLK, D_CONV), F32)
        for j in range(CONV_WIDTH):
            acc = acc + cw_ref[j:j + 1, :] * dbuf[pl.ds(CONV_WIDTH - 1 - j, ROW_BLK), :]
            seen = cbuf[pl.ds(HALO - (CONV_WIDTH - 1) + j, ROW_BLK), :]
            dw_ref[j:j + 1, :] += jnp.sum(dcv * seen, axis=0, keepdims=True)
        a = a_ref[...]
        sb = _sigmoid(b_ref[...])
        da_ref[...] = acc * sb
        db_ref[...] = acc * a * sb * (1.0 - sb)

    return pl.pallas_call(
        body, name="conv_bwd_taps", grid=(t // ROW_BLK,),
        in_specs=[_rows(D_CONV),
                  pl.BlockSpec((HALO, D_CONV), lambda i: (jnp.minimum((i + 1) * per, n_halo - 1), 0)),
                  _rows(D_CONV, 0), _rows(D_CONV, 1), _prev_halo(0), _prev_halo(1),
                  _whole((CONV_WIDTH, D_CONV))],
        out_specs=[_rows(D_CONV), _rows(D_CONV), _whole((32, D_CONV))],
        out_shape=[jax.ShapeDtypeStruct((t, D_CONV), F32), jax.ShapeDtypeStruct((t, D_CONV), F32),
                   jax.ShapeDtypeStruct((32, D_CONV), F32)],
        scratch_shapes=[pltpu.VMEM((HALO + ROW_BLK, D_CONV), F32), pltpu.VMEM((ROW_BLK + HALO, D_CONV), F32)],
        compiler_params=_cparams("arbitrary"),
    )(dcv, dcv, pc, pc, pc, pc, conv_w)


def _inproj_bwd(dh_out, h, g_pre, pieces, w_in_t):
    t = h.shape[0]

    def body(dh_ref, h_ref, g_ref, *rest):
        piece_refs, (wt_ref, dhin_ref, dproj_ref, dg_ref) = rest[:7], rest[7:]

        @pl.when(pl.program_id(0) == 0)
        def _():
            dg_ref[...] = jnp.zeros_like(dg_ref)

        for k, ref in enumerate(piece_refs):
            dproj_ref[:, 512 * k:512 * (k + 1)] = ref[...].astype(BF16)
        du = _dot(dproj_ref[...], wt_ref[...])
        x = h_ref[...]
        r = lax.rsqrt(jnp.mean(x * x, axis=-1, keepdims=True) + RMS_EPS)
        xh = x * r
        dg_ref[...] += jnp.sum(du * xh, axis=0, keepdims=True)
        dxh = du * g_ref[...]
        dhin_ref[...] = dh_ref[...] + r * (dxh - xh * jnp.mean(dxh * xh, axis=-1, keepdims=True))

    return pl.pallas_call(
        body, name="inproj_bwd", grid=(t // ROW_BLK,),
        in_specs=[_rows(D_MODEL), _rows(D_MODEL), _whole((1, D_MODEL))] + [_rows(512)] * 7
                 + [_whole((D_IN, D_MODEL))],
        out_specs=[_rows(D_MODEL), _rows(D_IN), _whole((1, D_MODEL))],
        out_shape=[jax.ShapeDtypeStruct((t, D_MODEL), F32), jax.ShapeDtypeStruct((t, D_IN), BF16),
                   jax.ShapeDtypeStruct((1, D_MODEL), F32)],
        compiler_params=_cparams("arbitrary"),
    )(dh_out, h, g_pre, *pieces, w_in_t)


def _weight_grad(xb, dyb, name):
    t, k = xb.shape
    n = dyb.shape[1]
    tn = min(n, 512)

    def body(x_ref, dy_ref, o_ref):
        @pl.when(pl.program_id(1) == 0)
        def _():
            o_ref[...] = jnp.zeros_like(o_ref)

        o_ref[...] += _dot_tn(x_ref[...], dy_ref[...])

    return pl.pallas_call(
        body, name=name, grid=(n // tn, t // ROW_BLK),
        in_specs=[pl.BlockSpec((ROW_BLK, k), lambda j, i: (i, 0)), pl.BlockSpec((ROW_BLK, tn), lambda j, i: (i, j))],
        out_specs=pl.BlockSpec((k, tn), lambda j, i: (0, j)),
        out_shape=jax.ShapeDtypeStruct((k, n), F32),
        compiler_params=_cparams("parallel", "arbitrary"),
    )(xb, dyb)


def _position():
    return lax.axis_index("x"), lax.axis_index("y"), lax.axis_index("c")


def _comm_call(body, name, ins, out_shapes):
    n = len(ins)
    hbm = pl.BlockSpec(memory_space=pltpu.HBM)
    return pl.pallas_call(
        functools.partial(body, n), name=name, in_specs=[hbm] * n, out_specs=[hbm] * n, out_shape=out_shapes,
        scratch_shapes=[pltpu.SemaphoreType.DMA((n, N_DEV - 1)), pltpu.SemaphoreType.DMA((n, N_DEV - 1)),
                        pltpu.SemaphoreType.DMA((n,))],
    )(*ins)


def _all_gather(blocks, name):
    def body(n, *refs):
        x_refs, out_refs, (send_sems, recv_sems, local_sems) = refs[:n], refs[n:2 * n], refs[2 * n:]
        x, y, c = _position()
        me, sibling = (x, y, c), (x, y, 1 - c)
        chips = [(1 - x, y), (x, 1 - y), (1 - x, 1 - y)]

        def slot(a, px, py, pc):
            return out_refs[a].at[4 * px + 2 * py + pc]

        def copy(a, k, origin, to, own=False):
            return pltpu.make_async_remote_copy(
                src_ref=x_refs[a] if own else slot(a, *origin), dst_ref=slot(a, *origin),
                send_sem=send_sems.at[a, k], recv_sem=recv_sems.at[a, k], device_id=to, device_id_type=MESH)

        arrays = range(n)
        mine = [pltpu.make_async_copy(x_refs[a], slot(a, *me), local_sems.at[a]) for a in arrays]
        first = [copy(a, 1 + j, me, (*chip, c), own=True) for j, chip in enumerate(chips) for a in arrays]
        first += [copy(a, 0, me, sibling, own=True) for a in arrays]
        for cp in mine + first:
            cp.start()
        passed = []
        for j, chip in enumerate(chips):
            for a in arrays:
                copy(a, 1 + j, (*chip, c), me).wait_recv()
                passed.append(copy(a, 4 + j, (*chip, c), sibling))
                passed[-1].start()
        for a in arrays:
            copy(a, 0, sibling, me).wait_recv()
            for j, chip in enumerate(chips):
                copy(a, 4 + j, (*chip, 1 - c), me).wait_recv()
        for cp in first + passed:
            cp.wait_send()
        for cp in mine:
            cp.wait()

    return _comm_call(body, name, blocks, [jax.ShapeDtypeStruct((N_DEV,) + b.shape, b.dtype) for b in blocks])


def _all_to_all(slabs, name):
    def body(n, *refs):
        g_refs, land_refs, (send_sems, recv_sems, local_sems) = refs[:n], refs[n:2 * n], refs[2 * n:]
        x, y, c = _position()
        me = 4 * x + 2 * y + c
        arrays = range(n)
        mine = [pltpu.make_async_copy(g_refs[a].at[me], land_refs[a].at[me], local_sems.at[a]) for a in arrays]
        peers = []
        for k in range(1, N_DEV):
            px = 1 - x if k & 4 else x
            py = 1 - y if k & 2 else y
            pc = 1 - c if k & 1 else c
            peers.append((k - 1, (px, py, pc), 4 * px + 2 * py + pc))

        def copy(a, s, dev, idx, landing):
            return pltpu.make_async_remote_copy(
                src_ref=g_refs[a].at[idx], dst_ref=land_refs[a].at[landing], send_sem=send_sems.at[a, s],
                recv_sem=recv_sems.at[a, s], device_id=dev, device_id_type=MESH)

        sends = [copy(a, s, dev, idx, me) for s, dev, idx in peers for a in arrays]
        for cp in mine + sends:
            cp.start()
        for s, dev, idx in peers:
            for a in arrays:
                copy(a, s, dev, idx, idx).wait_recv()
        for cp in sends:
            cp.wait_send()
        for cp in mine:
            cp.wait()

    return _comm_call(body, name, slabs, [jax.ShapeDtypeStruct(g.shape, g.dtype) for g in slabs])


def _block_rows(r, row_bytes, budget=1 << 20):
    cap = max(8, budget // row_bytes)
    return max(d for d in range(8, min(r, cap) + 1, 8) if r % d == 0)


def _sum_adamw(parts, w, m, v, name):
    n_parts, r, c = parts.shape
    br = _block_rows(r, 4 * c)

    def body(p_ref, w_ref, m_ref, v_ref, g_out, d_out, m_out, v_out):
        g = p_ref[0].astype(F32)
        for s in range(1, n_parts):
            g = g + p_ref[s].astype(F32)
        m_new = ADAM_B1 * m_ref[...] + (1.0 - ADAM_B1) * g
        v_new = ADAM_B2 * v_ref[...] + (1.0 - ADAM_B2) * (g * g)
        m_hat = m_new / (1.0 - ADAM_B1 ** ADAM_STEP)
        v_hat = v_new / (1.0 - ADAM_B2 ** ADAM_STEP)
        g_out[...] = g
        d_out[...] = -ADAM_LR * (m_hat / (jnp.sqrt(v_hat) + ADAM_EPS) + ADAM_WD * w_ref[...])
        m_out[...] = m_new
        v_out[...] = v_new

    row = pl.BlockSpec((br, c), lambda i: (i, 0))
    return pl.pallas_call(
        body, name=name, grid=(r // br,),
        in_specs=[pl.BlockSpec((n_parts, br, c), lambda i: (0, i, 0)), row, row, row],
        out_specs=[row] * 4, out_shape=[jax.ShapeDtypeStruct((r, c), F32)] * 4,
        compiler_params=_cparams("parallel"),
    )(parts, w, m, v)


def _sum_parts(parts, name):
    n_parts, r, c = parts.shape

    def body(p_ref, o_ref):
        g = p_ref[0]
        for s in range(1, n_parts):
            g = g + p_ref[s]
        o_ref[...] = g

    return pl.pallas_call(
        body, name=name, in_specs=[pl.BlockSpec(memory_space=pltpu.VMEM)],
        out_specs=pl.BlockSpec(memory_space=pltpu.VMEM), out_shape=jax.ShapeDtypeStruct((r, c), F32),
    )(parts)


def _pack(arrays):
    flat = jnp.concatenate([a.reshape(-1) for a in arrays])
    pad = -flat.shape[0] % (8 * LANES)
    if pad:
        flat = jnp.pad(flat, (0, pad))
    return flat.reshape(-1, LANES)


def _unpack(buf, shapes):
    flat = buf.reshape(-1)
    out, at = [], 0
    for shape in shapes:
        size = 1
        for d in shape:
            size *= d
        out.append(lax.slice_in_dim(flat, at, at + size).reshape(shape))
        at += size
    return out


def _local_step(x, target, meta, pre_g, post_g, w_in, conv_w, conv_b, ln_g, ln_b, w_pw2, b_pw2, w_out):
    depth = w_in.shape[0]
    seq = x.shape[0]
    t = -(-(N_META + seq) // ROW_BLK) * ROW_BLK
    tail = t - N_META - seq
    h = jnp.concatenate([meta, x, jnp.zeros((tail, D_MODEL), F32)], axis=0)
    target = jnp.pad(target, ((N_META, tail), (0, 0)))
    w_in_t = jnp.swapaxes(w_in, 1, 2)
    w_pw2_t = jnp.swapaxes(w_pw2, 1, 2)
    w_out_t = jnp.swapaxes(w_out, 1, 2)
    row = lambda a, l: a[l][None, :]

    saved = []
    for l in range(depth):
        pc, qkv, sbg, u = _inproj_fwd(h, row(pre_g, l), w_in[l])
        cout, cv, p, sl = _conv_fwd(pc, conv_w[l], row(conv_b, l), row(ln_g, l), row(ln_b, l), w_pw2[l],
                                    row(b_pw2, l))
        sraw, carries = _attn_fwd(qkv)
        h_new, mixed, mix = _outproj_fwd(h, cout, sraw, sbg, w_out[l], row(post_g, l))
        saved.append((h, pc, qkv, sbg, u, cv, p, sl, sraw, carries, mixed, mix))
        h = h_new

    loss, dh = _loss_and_grad(h, target, seq)

    grads = {k: [None] * depth for k in ("pre_g", "post_g", "w_in", "conv_w", "conv_b", "ln_g", "ln_b", "w_pw2",
                                          "b_pw2", "w_out")}
    for l in reversed(range(depth)):
        h_in, pc, qkv, sbg, u, cv, p, sl, sraw, carries, mixed, mix = saved[l]
        dcout, dsraw, dsbg, dmixed, dg_post = _outproj_bwd(dh, mixed, row(post_g, l), sraw, sbg, w_out_t[l])
        dq, dk, dv = _attn_bwd(qkv, carries, dsraw)
        dcv, dgate, dpb, vecs = _conv_bwd_rows(dcout, pc, cv, p, row(ln_g, l), row(ln_b, l), w_pw2_t[l])
        da, db, dconv_w = _conv_bwd_taps(dcv, pc, conv_w[l])
        dh, dproj, dg_pre = _inproj_bwd(dh, h_in, row(pre_g, l), (da, db, dgate, dq, dk, dv, dsbg), w_in_t[l])
        grads["w_in"][l] = _weight_grad(u, dproj, "w_in_grad")
        grads["w_out"][l] = _weight_grad(mix, dmixed, "w_out_grad")
        grads["w_pw2"][l] = _weight_grad(sl, dpb, "w_pw2_grad")
        grads["pre_g"][l] = dg_pre[0]
        grads["post_g"][l] = dg_post[0]
        grads["b_pw2"][l], grads["ln_g"][l], grads["ln_b"][l], grads["conv_b"][l] = vecs[0], vecs[1], vecs[2], vecs[3]
        grads["conv_w"][l] = dconv_w[:CONV_WIDTH]

    grads = {k: jnp.stack(v) for k, v in grads.items()}
    grads["meta"] = dh[:N_META]
    return loss[0, 0], dh[N_META:N_META + seq], grads


def _shard_major(full, axis):
    shape = full.shape
    split = full.reshape(shape[:axis] + (N_DEV, shape[axis] // N_DEV) + shape[axis + 1:])
    return jnp.moveaxis(split, axis, 0)


def _whole_from_shards(shards, axis):
    moved = jnp.moveaxis(shards, 0, axis)
    shape = moved.shape
    return moved.reshape(shape[:axis] + (shape[axis] * shape[axis + 1],) + shape[axis + 2:])


def kernel(x, meta_tokens, pre_norm_g, post_norm_g, w_in, conv_w, conv_b, conv_ln_g, conv_ln_b, w_pw2, b_pw2, w_out, loss_target, m_meta_tokens, m_pre_norm_g, m_post_norm_g, m_w_in, m_conv_w, m_conv_b, m_conv_ln_g, m_conv_ln_b, m_w_pw2, m_b_pw2, m_w_out, v_meta_tokens, v_pre_norm_g, v_post_norm_g, v_w_in, v_conv_w, v_conv_b, v_conv_ln_g, v_conv_ln_b, v_w_pw2, v_b_pw2, v_w_out):
    me = 4 * lax.axis_index("x") + 2 * lax.axis_index("y") + lax.axis_index("c")

    w_in_s, w_out_s, w_pw2_s, conv_w_s, meta_s = _all_gather(
        [w_in.astype(BF16), w_out.astype(BF16), w_pw2.astype(BF16), conv_w, meta_tokens], "gather_weights")
    w_in_full = _whole_from_shards(w_in_s, 2)
    w_out_full = _whole_from_shards(w_out_s, 1)
    w_pw2_full = _whole_from_shards(w_pw2_s, 1)
    conv_w_full = _whole_from_shards(conv_w_s, 2)
    meta_full = _whole_from_shards(meta_s, 1)

    loss, dx, grads = _local_step(x[0], loss_target[0], meta_full, pre_norm_g, post_norm_g, w_in_full, conv_w_full,
                                  conv_b, conv_ln_g, conv_ln_b, w_pw2_full, b_pw2, w_out_full)
    loss = lax.psum(loss, ("x", "y", "c"))

    landed = _all_to_all([_shard_major(grads[k], axis).astype(BF16) for k, axis in (("w_in", 2), ("w_out", 1), ("w_pw2", 1))],
                         "exchange_matmul_grads")

    def update(parts, w, m, v, name):
        rows = lambda a: a.reshape(-1, a.shape[-1])
        outs = _sum_adamw(parts.reshape(N_DEV, -1, parts.shape[-1]), rows(w), rows(m), rows(v), name)
        return [o.reshape(w.shape) for o in outs]

    g_w_in, d_w_in, nm_w_in, nv_w_in = update(landed[0], w_in, m_w_in, v_w_in, "adamw_w_in")
    g_w_out, d_w_out, nm_w_out, nv_w_out = update(landed[1], w_out, m_w_out, v_w_out, "adamw_w_out")
    g_w_pw2, d_w_pw2, nm_w_pw2, nv_w_pw2 = update(landed[2], w_pw2, m_w_pw2, v_w_pw2, "adamw_w_pw2")

    small_names = ("pre_g", "post_g", "conv_b", "ln_g", "ln_b", "b_pw2", "conv_w", "meta")
    small_full = [grads[k] for k in small_names]
    gathered, = _all_gather([_pack(small_full)], "gather_small_grads")
    summed = _sum_parts(gathered, "sum_small_grads")
    g_small = dict(zip(small_names, _unpack(summed, [a.shape for a in small_full])))
    g_small["conv_w"] = lax.dynamic_slice_in_dim(g_small["conv_w"], me * conv_w.shape[2], conv_w.shape[2], axis=2)
    g_small["meta"] = lax.dynamic_slice_in_dim(g_small["meta"], me * meta_tokens.shape[1], meta_tokens.shape[1], axis=1)
    small_w = dict(zip(small_names, (pre_norm_g, post_norm_g, conv_b, conv_ln_g, conv_ln_b, b_pw2, conv_w, meta_tokens)))
    small_m = (m_pre_norm_g, m_post_norm_g, m_conv_b, m_conv_ln_g, m_conv_ln_b, m_b_pw2, m_conv_w, m_meta_tokens)
    small_v = (v_pre_norm_g, v_post_norm_g, v_conv_b, v_conv_ln_g, v_conv_ln_b, v_b_pw2, v_conv_w, v_meta_tokens)
    small_shapes = [small_w[k].shape for k in small_names]
    outs = _sum_adamw(_pack([g_small[k] for k in small_names])[None], _pack([small_w[k] for k in small_names]),
                      _pack(small_m), _pack(small_v), "adamw_small_weights")
    g_s, d_s, nm_s, nv_s = [dict(zip(small_names, _unpack(o, small_shapes))) for o in outs]

    def ordered(s, w_in_, w_pw2_, w_out_):
        return (s["meta"], s["pre_g"], s["post_g"], w_in_, s["conv_w"], s["conv_b"], s["ln_g"], s["ln_b"], w_pw2_,
                s["b_pw2"], w_out_)

    return (loss, dx[None], *ordered(g_s, g_w_in, g_w_pw2, g_w_out), *ordered(d_s, d_w_in, d_w_pw2, d_w_out),
            *ordered(nm_s, nm_w_in, nm_w_pw2, nm_w_out), *ordered(nv_s, nv_w_in, nv_w_pw2, nv_w_out))
```

```python
import functools

import jax
import jax.numpy as jnp
from jax import lax
from jax.experimental import pallas as pl
from jax.experimental.pallas import tpu as pltpu

F32 = jnp.float32
BF16 = jnp.bfloat16

D_MODEL = 1024
D_CONV = 512
D_SB = 512
HEAD_DIM = 64
HEADS_PER_BLOCK = 4
HEAD_BLK = HEADS_PER_BLOCK * HEAD_DIM
CONV_WIDTH = 31
N_META = 16
D_IN = 3 * D_CONV + 4 * D_SB
RMS_EPS = 1e-6
LN_EPS = 1e-5
SB_SCALE = HEAD_DIM ** -0.5

ADAM_LR = 0.001
ADAM_B1 = 0.9
ADAM_B2 = 0.999
ADAM_EPS = 1e-08
ADAM_WD = 0.01
ADAM_STEP = 10

N_DEV = 8
LANES = 128
ROW_BLK = 256
HALO = 32
VMEM_LIMIT = 56 * 1024 * 1024
MESH = pl.DeviceIdType.MESH


def _cparams(*sem):
    return pltpu.CompilerParams(dimension_semantics=sem, vmem_limit_bytes=VMEM_LIMIT)


def _rows(n_cols, col=0):
    return pl.BlockSpec((ROW_BLK, n_cols), lambda i, col=col: (i, col))


def _whole(shape):
    return pl.BlockSpec(shape, lambda i: (0,) * len(shape))


def _sigmoid(x):
    return jax.nn.sigmoid(x)


def _dsilu(x, s):
    return s * (1.0 + x * (1.0 - s))


def _dot(a, b):
    return jnp.dot(a, b, preferred_element_type=F32)


def _dot_nt(a, b):
    return lax.dot_general(a, b, (((1,), (1,)), ((), ())), preferred_element_type=F32)


def _dot_tn(a, b):
    return lax.dot_general(a, b, (((0,), (0,)), ((), ())), preferred_element_type=F32)


def _inproj_fwd(h, g_pre, w_in):
    t = h.shape[0]

    def body(h_ref, g_ref, w_ref, pc_ref, qkv_ref, sbg_ref, u_ref):
        x = h_ref[...]
        r = lax.rsqrt(jnp.mean(x * x, axis=-1, keepdims=True) + RMS_EPS)
        u = (x * r * g_ref[...]).astype(BF16)
        u_ref[...] = u
        pc_ref[...] = _dot(u, w_ref[:, 0:1536])
        qkv_ref[...] = _dot(u, w_ref[:, 1536:3072]).astype(BF16)
        sbg_ref[...] = _dot(u, w_ref[:, 3072:3584])

    return pl.pallas_call(
        body, name="inproj_fwd", grid=(t // ROW_BLK,),
        in_specs=[_rows(D_MODEL), _whole((1, D_MODEL)), _whole((D_MODEL, D_IN))],
        out_specs=[_rows(1536), _rows(1536), _rows(D_SB), _rows(D_MODEL)],
        out_shape=[jax.ShapeDtypeStruct((t, 1536), F32), jax.ShapeDtypeStruct((t, 1536), BF16),
                   jax.ShapeDtypeStruct((t, D_SB), F32), jax.ShapeDtypeStruct((t, D_MODEL), BF16)],
        compiler_params=_cparams("parallel"),
    )(h, g_pre, w_in)


def _prev_halo(col):
    per = ROW_BLK // HALO
    return pl.BlockSpec((HALO, D_CONV), lambda i, col=col: (jnp.maximum(i * per - 1, 0), col))


def _fill_glu(buf, i, a_ref, b_ref, ha_ref, hb_ref):
    halo = ha_ref[...] * _sigmoid(hb_ref[...])
    buf[0:HALO, :] = jnp.where(i > 0, halo, 0.0)
    buf[HALO:HALO + ROW_BLK, :] = a_ref[...] * _sigmoid(b_ref[...])


def _layer_norm_stats(cv):
    mu = jnp.mean(cv, axis=-1, keepdims=True)
    xc = cv - mu
    rstd = lax.rsqrt(jnp.mean(xc * xc, axis=-1, keepdims=True) + LN_EPS)
    return xc * rstd, rstd


def _conv_fwd(pc, conv_w, conv_b, ln_g, ln_b, w_pw2, b_pw2):
    t = pc.shape[0]

    def body(a_ref, b_ref, gate_ref, ha_ref, hb_ref, cw_ref, cb_ref, lg_ref, lb_ref, wp_ref, bp_ref,
             cout_ref, cv_ref, p_ref, sl_ref, buf):
        i = pl.program_id(0)
        _fill_glu(buf, i, a_ref, b_ref, ha_ref, hb_ref)
        acc = jnp.zeros((ROW_BLK, D_CONV), F32) + cb_ref[...]
        for j in range(CONV_WIDTH):
            acc = acc + cw_ref[j:j + 1, :] * buf[pl.ds(HALO - (CONV_WIDTH - 1) + j, ROW_BLK), :]
        cv_ref[...] = acc
        xh, _ = _layer_norm_stats(acc)
        ln = xh * lg_ref[...] + lb_ref[...]
        sl = (ln * _sigmoid(ln)).astype(BF16)
        sl_ref[...] = sl
        p = _dot(sl, wp_ref[...]) + bp_ref[...]
        p_ref[...] = p
        gate = gate_ref[...]
        cout_ref[...] = (p * (gate * _sigmoid(gate))).astype(BF16)

    vec = _whole((1, D_CONV))
    return pl.pallas_call(
        body, name="conv_fwd", grid=(t // ROW_BLK,),
        in_specs=[_rows(D_CONV, 0), _rows(D_CONV, 1), _rows(D_CONV, 2), _prev_halo(0), _prev_halo(1),
                  _whole((CONV_WIDTH, D_CONV)), vec, vec, vec, _whole((D_CONV, D_CONV)), vec],
        out_specs=[_rows(D_CONV)] * 4,
        out_shape=[jax.ShapeDtypeStruct((t, D_CONV), BF16), jax.ShapeDtypeStruct((t, D_CONV), F32),
                   jax.ShapeDtypeStruct((t, D_CONV), F32), jax.ShapeDtypeStruct((t, D_CONV), BF16)],
        scratch_shapes=[pltpu.VMEM((HALO + ROW_BLK, D_CONV), F32)],
        compiler_params=_cparams("parallel"),
    )(pc, pc, pc, pc, pc, conv_w, conv_b, ln_g, ln_b, w_pw2, b_pw2)


def _lower_triangle():
    row = lax.broadcasted_iota(jnp.int32, (ROW_BLK, ROW_BLK), 0)
    col = lax.broadcasted_iota(jnp.int32, (ROW_BLK, ROW_BLK), 1)
    return row > col


def _lower_triangle_t():
    row = lax.broadcasted_iota(jnp.int32, (ROW_BLK, ROW_BLK), 0)
    col = lax.broadcasted_iota(jnp.int32, (ROW_BLK, ROW_BLK), 1)
    return row < col


def _tri_sum(x, umat):
    return _dot(x.astype(BF16), umat)


def _log_gates(z):
    ls = -(jnp.maximum(z, 0.0) + jnp.log(1.0 + jnp.exp(-jnp.abs(z))))
    return ls, z + ls


def _head_lanes(hh):
    lane = lax.broadcasted_iota(jnp.int32, (ROW_BLK, HEAD_BLK), 1)
    return (lane >= HEAD_DIM * hh) & (lane < HEAD_DIM * (hh + 1))


def _merge_heads(acc_ref):
    out = acc_ref[HEADS_PER_BLOCK - 1]
    for hh in range(HEADS_PER_BLOCK - 1):
        out = jnp.where(_head_lanes(hh), acc_ref[hh], out)
    return out


def _qkv_specs(t):
    n_blk = D_SB // HEAD_BLK
    return [pl.BlockSpec((ROW_BLK, HEAD_BLK), lambda hp, i: (i, hp)),
            pl.BlockSpec((t, HEAD_BLK), lambda hp, i: (0, n_blk + hp)),
            pl.BlockSpec((t, HEAD_BLK), lambda hp, i: (0, 2 * n_blk + hp))]


def _carry_spec():
    return pl.BlockSpec((HEADS_PER_BLOCK, ROW_BLK, LANES), lambda hp, i: (hp, i, 0))


def _attn_fwd(qkv):
    t = qkv.shape[0]
    assert t // ROW_BLK <= LANES

    def body(q_ref, k_ref, v_ref, o_ref, c_ref, acc_ref, run_ref, qm_ref):
        i = pl.program_id(1)
        lower = _lower_triangle()
        umat = jnp.where(lower, 1.0, 0.0).astype(BF16)
        lane = lax.broadcasted_iota(jnp.int32, (ROW_BLK, LANES), 1)
        q = q_ref[...]
        heads = range(HEADS_PER_BLOCK)
        for hh in heads:
            qm_ref[hh] = jnp.where(_head_lanes(hh), q, jnp.zeros_like(q)) * jnp.asarray(SB_SCALE, BF16)
        acc_ref[...] = jnp.zeros_like(acc_ref)
        c_ref[...] = jnp.zeros_like(c_ref)
        run_ref[...] = jnp.zeros_like(run_ref)

        def block(jb, diagonal):
            start = pl.multiple_of(jb * ROW_BLK, ROW_BLK)
            kb = k_ref[pl.ds(start, ROW_BLK), :]
            vb = v_ref[pl.ds(start, ROW_BLK), :]
            zs = [_dot_nt(qm_ref[hh], kb) for hh in heads]
            logits = []
            for hh in heads:
                ls, lb = _log_gates(zs[hh])
                if diagonal:
                    ls = jnp.where(lower, ls, 0.0)
                run = run_ref[hh]
                if not diagonal:
                    c_ref[hh] = jnp.where(lane == jb, run, c_ref[hh])
                logits.append(lb + jnp.concatenate([run, run], axis=1) + _tri_sum(ls, umat))
                run_ref[hh] = run + jnp.sum(ls, axis=1, keepdims=True)
            for hh in heads:
                a = jnp.exp(logits[hh])
                if diagonal:
                    a = jnp.where(lower, a, 0.0)
                acc_ref[hh] += _dot(a.astype(BF16), vb)

        block(i, True)

        @pl.loop(0, i)
        def _(n):
            block(i - 1 - n, False)

        o_ref[...] = _merge_heads(acc_ref)

    per_head = (HEADS_PER_BLOCK, ROW_BLK, HEAD_BLK)
    return pl.pallas_call(
        body, name="attn_fwd", grid=(D_SB // HEAD_BLK, t // ROW_BLK),
        in_specs=_qkv_specs(t),
        out_specs=[pl.BlockSpec((ROW_BLK, HEAD_BLK), lambda hp, i: (i, hp)), _carry_spec()],
        out_shape=[jax.ShapeDtypeStruct((t, D_SB), F32),
                   jax.ShapeDtypeStruct((D_SB // HEAD_DIM, t, LANES), F32)],
        scratch_shapes=[pltpu.VMEM(per_head, F32), pltpu.VMEM((HEADS_PER_BLOCK, ROW_BLK, LANES), F32),
                        pltpu.VMEM(per_head, BF16)],
        compiler_params=_cparams("arbitrary", "arbitrary"),
    )(qkv, qkv, qkv)


def _outproj_fwd(h, cout, sraw, sbg, w_out, g_post):
    t = h.shape[0]

    def body(h_ref, c_ref, s_ref, g_ref, w_ref, gp_ref, hn_ref, mixed_ref, mix_ref):
        gate = g_ref[...]
        mix_ref[:, 0:D_CONV] = c_ref[...]
        mix_ref[:, D_CONV:] = (s_ref[...] * (gate * _sigmoid(gate))).astype(BF16)
        mixed = _dot(mix_ref[...], w_ref[...])
        mixed_ref[...] = mixed
        r = lax.rsqrt(jnp.mean(mixed * mixed, axis=-1, keepdims=True) + RMS_EPS)
        hn_ref[...] = h_ref[...] + mixed * r * gp_ref[...]

    return pl.pallas_call(
        body, name="outproj_fwd", grid=(t // ROW_BLK,),
        in_specs=[_rows(D_MODEL), _rows(D_CONV), _rows(D_SB), _rows(D_SB), _whole((D_MODEL, D_MODEL)),
                  _whole((1, D_MODEL))],
        out_specs=[_rows(D_MODEL)] * 3,
        out_shape=[jax.ShapeDtypeStruct((t, D_MODEL), F32), jax.ShapeDtypeStruct((t, D_MODEL), F32),
                   jax.ShapeDtypeStruct((t, D_MODEL), BF16)],
        compiler_params=_cparams("parallel"),
    )(h, cout, sraw, sbg, w_out, g_post)


def _loss_and_grad(h, target, seq):
    t = h.shape[0]

    def body(h_ref, t_ref, loss_ref, dh_ref):
        i = pl.program_id(0)

        @pl.when(i == 0)
        def _():
            loss_ref[...] = jnp.zeros_like(loss_ref)

        row = i * ROW_BLK + lax.broadcasted_iota(jnp.int32, (ROW_BLK, D_MODEL), 0)
        real = (row >= N_META) & (row < N_META + seq)
        diff = jnp.where(real, h_ref[...] - t_ref[...], 0.0)
        sq = jnp.sum(jnp.sum(diff * diff, axis=1, keepdims=True), axis=0, keepdims=True)
        loss_ref[...] += (0.5 / D_MODEL) * sq
        dh_ref[...] = diff * (1.0 / D_MODEL)

    return pl.pallas_call(
        body, name="loss", grid=(t // ROW_BLK,),
        in_specs=[_rows(D_MODEL), _rows(D_MODEL)],
        out_specs=[_whole((1, 1)), _rows(D_MODEL)],
        out_shape=[jax.ShapeDtypeStruct((1, 1), F32), jax.ShapeDtypeStruct((t, D_MODEL), F32)],
        compiler_params=_cparams("arbitrary"),
    )(h, target)


def _outproj_bwd(dh, mixed, g_post, sraw, sbg, w_out_t):
    t = dh.shape[0]

    def body(dh_ref, mixed_ref, gp_ref, s_ref, g_ref, wt_ref, dc_ref, ds_ref, dg_ref, dmb_ref, dgp_ref):
        @pl.when(pl.program_id(0) == 0)
        def _():
            dgp_ref[...] = jnp.zeros_like(dgp_ref)

        mixed = mixed_ref[...]
        r = lax.rsqrt(jnp.mean(mixed * mixed, axis=-1, keepdims=True) + RMS_EPS)
        nh = mixed * r
        dy = dh_ref[...]
        dgp_ref[...] += jnp.sum(dy * nh, axis=0, keepdims=True)
        dn = dy * gp_ref[...]
        dmixed = (r * (dn - nh * jnp.mean(dn * nh, axis=-1, keepdims=True))).astype(BF16)
        dmb_ref[...] = dmixed
        dmix = _dot(dmixed, wt_ref[...])
        dc_ref[...] = dmix[:, 0:D_CONV]
        dsg = dmix[:, D_CONV:]
        gate = g_ref[...]
        sg = _sigmoid(gate)
        ds_ref[...] = dsg * (gate * sg)
        dg_ref[...] = dsg * s_ref[...] * _dsilu(gate, sg)

    return pl.pallas_call(
        body, name="outproj_bwd", grid=(t // ROW_BLK,),
        in_specs=[_rows(D_MODEL), _rows(D_MODEL), _whole((1, D_MODEL)), _rows(D_SB), _rows(D_SB),
                  _whole((D_MODEL, D_MODEL))],
        out_specs=[_rows(D_CONV), _rows(D_SB), _rows(D_SB), _rows(D_MODEL), _whole((1, D_MODEL))],
        out_shape=[jax.ShapeDtypeStruct((t, D_CONV), F32), jax.ShapeDtypeStruct((t, D_SB), F32),
                   jax.ShapeDtypeStruct((t, D_SB), F32), jax.ShapeDtypeStruct((t, D_MODEL), BF16),
                   jax.ShapeDtypeStruct((1, D_MODEL), F32)],
        compiler_params=_cparams("arbitrary"),
    )(dh, mixed, g_post, sraw, sbg, w_out_t)


def _attn_bwd(qkv, carries, do):
    t = qkv.shape[0]

    def body(q_ref, k_ref, v_ref, c_ref, do_ref, dq_ref, dk_ref, dv_ref, acc_ref, seen_ref, qm_ref, dom_ref):
        i = pl.program_id(1)

        @pl.when(i == 0)
        def _():
            dk_ref[...] = jnp.zeros_like(dk_ref)
            dv_ref[...] = jnp.zeros_like(dv_ref)

        lower = _lower_triangle()
        umat = jnp.where(lower, 1.0, 0.0).astype(BF16)
        umat_t = jnp.where(_lower_triangle_t(), 1.0, 0.0).astype(BF16)
        lane = lax.broadcasted_iota(jnp.int32, (ROW_BLK, LANES), 1)
        q = q_ref[...]
        dof = do_ref[...]
        heads = range(HEADS_PER_BLOCK)
        for hh in heads:
            qm_ref[hh] = jnp.where(_head_lanes(hh), q, jnp.zeros_like(q)) * jnp.asarray(SB_SCALE, BF16)
            dom_ref[hh] = jnp.where(_head_lanes(hh), dof, 0.0).astype(BF16)
        acc_ref[...] = jnp.zeros_like(acc_ref)
        seen_ref[...] = jnp.zeros_like(seen_ref)

        def block(jb, diagonal):
            start = pl.multiple_of(jb * ROW_BLK, ROW_BLK)
            kb = k_ref[pl.ds(start, ROW_BLK), :]
            vb = v_ref[pl.ds(start, ROW_BLK), :]
            zs = [_dot_nt(qm_ref[hh], kb) for hh in heads]
            das = [_dot_nt(dom_ref[hh], vb) for hh in heads]
            lbs, logits = [], []
            for hh in heads:
                ls, lb = _log_gates(zs[hh])
                if diagonal:
                    ls = jnp.where(lower, ls, 0.0)
                    logits.append(lb + _tri_sum(ls, umat))
                else:
                    right = jnp.sum(jnp.where(lane == jb, c_ref[hh], 0.0), axis=1, keepdims=True)
                    logits.append(lb + right + _tri_sum(ls, umat))
                lbs.append(lb)
            abs_, gs, befores = [], [], []
            for hh in heads:
                a = jnp.exp(logits[hh])
                if diagonal:
                    a = jnp.where(lower, a, 0.0)
                g = das[hh] * a
                seen = seen_ref[hh]
                befores.append(jnp.concatenate([seen, seen], axis=1) + _tri_sum(g, umat_t))
                seen_ref[hh] = seen + jnp.sum(g, axis=1, keepdims=True)
                abs_.append(a.astype(BF16))
                gs.append(g)
            dk = dv = None
            for hh in heads:
                dz = gs[hh] - jnp.exp(lbs[hh]) * (gs[hh] + befores[hh])
                if diagonal:
                    dz = jnp.where(lower, dz, 0.0)
                dzb = dz.astype(BF16)
                acc_ref[hh] += _dot(dzb, kb)
                dk_h = _dot_tn(dzb, qm_ref[hh])
                dv_h = _dot_tn(abs_[hh], dom_ref[hh])
                dk = dk_h if dk is None else dk + dk_h
                dv = dv_h if dv is None else dv + dv_h
            dk_ref[pl.ds(start, ROW_BLK), :] += dk
            dv_ref[pl.ds(start, ROW_BLK), :] += dv

        @pl.loop(0, i)
        def _(jb):
            block(jb, False)

        block(i, True)
        dq_ref[...] = _merge_heads(acc_ref) * SB_SCALE

    blk = pl.BlockSpec((ROW_BLK, HEAD_BLK), lambda hp, i: (i, hp))
    full = pl.BlockSpec((t, HEAD_BLK), lambda hp, i: (0, hp))
    per_head = (HEADS_PER_BLOCK, ROW_BLK, HEAD_BLK)
    return pl.pallas_call(
        body, name="attn_bwd", grid=(D_SB // HEAD_BLK, t // ROW_BLK),
        in_specs=_qkv_specs(t) + [_carry_spec(), blk],
        out_specs=[blk, full, full],
        out_shape=[jax.ShapeDtypeStruct((t, D_SB), F32)] * 3,
        scratch_shapes=[pltpu.VMEM(per_head, F32), pltpu.VMEM((HEADS_PER_BLOCK, ROW_BLK, LANES), F32),
                        pltpu.VMEM(per_head, BF16), pltpu.VMEM(per_head, BF16)],
        compiler_params=_cparams("arbitrary", "arbitrary"),
    )(qkv, qkv, qkv, carries, do)


def _conv_bwd_rows(dcout, pc, cv, p, ln_g, ln_b, w_pw2_t):
    t = dcout.shape[0]

    def body(dc_ref, gate_ref, cv_ref, p_ref, lg_ref, lb_ref, wt_ref, dcv_ref, dgate_ref, dpb_ref, vec_ref):
        @pl.when(pl.program_id(0) == 0)
        def _():
            vec_ref[...] = jnp.zeros_like(vec_ref)

        dc = dc_ref[...]
        gate = gate_ref[...]
        sg = _sigmoid(gate)
        dp = dc * (gate * sg)
        dgate_ref[...] = dc * p_ref[...] * _dsilu(gate, sg)
        dpb = dp.astype(BF16)
        dpb_ref[...] = dpb
        xh, rstd = _layer_norm_stats(cv_ref[...])
        ln = xh * lg_ref[...] + lb_ref[...]
        s2 = _sigmoid(ln)
        dln = _dot(dpb, wt_ref[...]) * _dsilu(ln, s2)
        dxh = dln * lg_ref[...]
        dcv = rstd * (dxh - jnp.mean(dxh, axis=-1, keepdims=True)
                      - xh * jnp.mean(dxh * xh, axis=-1, keepdims=True))
        dcv_ref[...] = dcv
        vec_ref[0:1, :] += jnp.sum(dp, axis=0, keepdims=True)
        vec_ref[1:2, :] += jnp.sum(dln * xh, axis=0, keepdims=True)
        vec_ref[2:3, :] += jnp.sum(dln, axis=0, keepdims=True)
        vec_ref[3:4, :] += jnp.sum(dcv, axis=0, keepdims=True)

    vec = _whole((1, D_CONV))
    return pl.pallas_call(
        body, name="conv_bwd_rows", grid=(t // ROW_BLK,),
        in_specs=[_rows(D_CONV), _rows(D_CONV, 2), _rows(D_CONV), _rows(D_CONV), vec, vec,
                  _whole((D_CONV, D_CONV))],
        out_specs=[_rows(D_CONV), _rows(D_CONV), _rows(D_CONV), _whole((8, D_CONV))],
        out_shape=[jax.ShapeDtypeStruct((t, D_CONV), F32), jax.ShapeDtypeStruct((t, D_CONV), F32),
                   jax.ShapeDtypeStruct((t, D_CONV), BF16), jax.ShapeDtypeStruct((8, D_CONV), F32)],
        compiler_params=_cparams("arbitrary"),
    )(dcout, pc, cv, p, ln_g, ln_b, w_pw2_t)


def _conv_bwd_taps(dcv, pc, conv_w):
    t = dcv.shape[0]
    n_halo = t // HALO
    per = ROW_BLK // HALO

    def body(d_ref, dn_ref, a_ref, b_ref, ha_ref, hb_ref, cw_ref, da_ref, db_ref, dw_ref, cbuf, dbuf):
        i = pl.program_id(0)

        @pl.when(i == 0)
        def _():
            dw_ref[...] = jnp.zeros_like(dw_ref)

        _fill_glu(cbuf, i, a_ref, b_ref, ha_ref, hb_ref)
        dcv = d_ref[...]
        dbuf[0:ROW_BLK, :] = dcv
        dbuf[ROW_BLK:ROW_BLK + HALO, :] = jnp.where(i < pl.num_programs(0) - 1, dn_ref[...], 0.0)
        acc = jnp.zeros((ROW_BLK, D_CONV), F32)
        for j in range(CONV_WIDTH):
            acc = acc + cw_ref[j:j + 1, :] * dbuf[pl.ds(CONV_WIDTH - 1 - j, ROW_BLK), :]
            seen = cbuf[pl.ds(HALO - (CONV_WIDTH - 1) + j, ROW_BLK), :]
            dw_ref[j:j + 1, :] += jnp.sum(dcv * seen, axis=0, keepdims=True)
        a = a_ref[...]
        sb = _sigmoid(b_ref[...])
        da_ref[...] = acc * sb
        db_ref[...] = acc * a * sb * (1.0 - sb)

    return pl.pallas_call(
        body, name="conv_bwd_taps", grid=(t // ROW_BLK,),
        in_specs=[_rows(D_CONV),
                  pl.BlockSpec((HALO, D_CONV), lambda i: (jnp.minimum((i + 1) * per, n_halo - 1), 0)),
                  _rows(D_CONV, 0), _rows(D_CONV, 1), _prev_halo(0), _prev_halo(1),
                  _whole((CONV_WIDTH, D_CONV))],
        out_specs=[_rows(D_CONV), _rows(D_CONV), _whole((32, D_CONV))],
        out_shape=[jax.ShapeDtypeStruct((t, D_CONV), F32), jax.ShapeDtypeStruct((t, D_CONV), F32),
                   jax.ShapeDtypeStruct((32, D_CONV), F32)],
        scratch_shapes=[pltpu.VMEM((HALO + ROW_BLK, D_CONV), F32), pltpu.VMEM((ROW_BLK + HALO, D_CONV), F32)],
        compiler_params=_cparams("arbitrary"),
    )(dcv, dcv, pc, pc, pc, pc, conv_w)


def _inproj_bwd(dh_out, h, g_pre, pieces, w_in_t):
    t = h.shape[0]

    def body(dh_ref, h_ref, g_ref, *rest):
        piece_refs, (wt_ref, dhin_ref, dproj_ref, dg_ref) = rest[:7], rest[7:]

        @pl.when(pl.program_id(0) == 0)
        def _():
            dg_ref[...] = jnp.zeros_like(dg_ref)

        for k, ref in enumerate(piece_refs):
            dproj_ref[:, 512 * k:512 * (k + 1)] = ref[...].astype(BF16)
        du = _dot(dproj_ref[...], wt_ref[...])
        x = h_ref[...]
        r = lax.rsqrt(jnp.mean(x * x, axis=-1, keepdims=True) + RMS_EPS)
        xh = x * r
        dg_ref[...] += jnp.sum(du * xh, axis=0, keepdims=True)
        dxh = du * g_ref[...]
        dhin_ref[...] = dh_ref[...] + r * (dxh - xh * jnp.mean(dxh * xh, axis=-1, keepdims=True))

    return pl.pallas_call(
        body, name="inproj_bwd", grid=(t // ROW_BLK,),
        in_specs=[_rows(D_MODEL), _rows(D_MODEL), _whole((1, D_MODEL))] + [_rows(512)] * 7
                 + [_whole((D_IN, D_MODEL))],
        out_specs=[_rows(D_MODEL), _rows(D_IN), _whole((1, D_MODEL))],
        out_shape=[jax.ShapeDtypeStruct((t, D_MODEL), F32), jax.ShapeDtypeStruct((t, D_IN), BF16),
                   jax.ShapeDtypeStruct((1, D_MODEL), F32)],
        compiler_params=_cparams("arbitrary"),
    )(dh_out, h, g_pre, *pieces, w_in_t)


def _weight_grad(xb, dyb, name):
    t, k = xb.shape
    n = dyb.shape[1]
    tn = min(n, 512)

    def body(x_ref, dy_ref, o_ref):
        @pl.when(pl.program_id(1) == 0)
        def _():
            o_ref[...] = jnp.zeros_like(o_ref)

        o_ref[...] += _dot_tn(x_ref[...], dy_ref[...])

    return pl.pallas_call(
        body, name=name, grid=(n // tn, t // ROW_BLK),
        in_specs=[pl.BlockSpec((ROW_BLK, k), lambda j, i: (i, 0)), pl.BlockSpec((ROW_BLK, tn), lambda j, i: (i, j))],
        out_specs=pl.BlockSpec((k, tn), lambda j, i: (0, j)),
        out_shape=jax.ShapeDtypeStruct((k, n), F32),
        compiler_params=_cparams("parallel", "arbitrary"),
    )(xb, dyb)


def _position():
    return lax.axis_index("x"), lax.axis_index("y"), lax.axis_index("c")


def _comm_call(body, name, ins, out_shapes):
    n = len(ins)
    hbm = pl.BlockSpec(memory_space=pltpu.HBM)
    return pl.pallas_call(
        functools.partial(body, n), name=name, in_specs=[hbm] * n, out_specs=[hbm] * n, out_shape=out_shapes,
        scratch_shapes=[pltpu.SemaphoreType.DMA((n, N_DEV - 1)), pltpu.SemaphoreType.DMA((n, N_DEV - 1)),
                        pltpu.SemaphoreType.DMA((n,))],
    )(*ins)


def _all_gather(blocks, name):
    def body(n, *refs):
        x_refs, out_refs, (send_sems, recv_sems, local_sems) = refs[:n], refs[n:2 * n], refs[2 * n:]
        x, y, c = _position()
        me, sibling = (x, y, c), (x, y, 1 - c)
        chips = [(1 - x, y), (x, 1 - y), (1 - x, 1 - y)]

        def slot(a, px, py, pc):
            return out_refs[a].at[4 * px + 2 * py + pc]

        def copy(a, k, origin, to, own=False):
            return pltpu.make_async_remote_copy(
                src_ref=x_refs[a] if own else slot(a, *origin), dst_ref=slot(a, *origin),
                send_sem=send_sems.at[a, k], recv_sem=recv_sems.at[a, k], device_id=to, device_id_type=MESH)

        arrays = range(n)
        mine = [pltpu.make_async_copy(x_refs[a], slot(a, *me), local_sems.at[a]) for a in arrays]
        first = [copy(a, 1 + j, me, (*chip, c), own=True) for j, chip in enumerate(chips) for a in arrays]
        first += [copy(a, 0, me, sibling, own=True) for a in arrays]
        for cp in mine + first:
            cp.start()
        passed = []
        for j, chip in enumerate(chips):
            for a in arrays:
                copy(a, 1 + j, (*chip, c), me).wait_recv()
                passed.append(copy(a, 4 + j, (*chip, c), sibling))
                passed[-1].start()
        for a in arrays:
            copy(a, 0, sibling, me).wait_recv()
            for j, chip in enumerate(chips):
                copy(a, 4 + j, (*chip, 1 - c), me).wait_recv()
        for cp in first + passed:
            cp.wait_send()
        for cp in mine:
            cp.wait()

    return _comm_call(body, name, blocks, [jax.ShapeDtypeStruct((N_DEV,) + b.shape, b.dtype) for b in blocks])


def _exchange_copies(g_refs, land_refs, sems):
    x, y, c = _position()
    me = 4 * x + 2 * y + c
    out = []
    for g_ref, land_ref, (send_sem, recv_sem, local_sem) in zip(g_refs, land_refs, sems):
        def remote(src, dst, dev):
            return pltpu.make_async_remote_copy(src_ref=src, dst_ref=dst, send_sem=send_sem, recv_sem=recv_sem,
                                                device_id=dev, device_id_type=MESH)

        sends = []
        for k in range(1, N_DEV):
            px = 1 - x if k & 4 else x
            py = 1 - y if k & 2 else y
            pc = 1 - c if k & 1 else c
            sends.append(remote(g_ref.at[4 * px + 2 * py + pc], land_ref.at[me], (px, py, pc)))
        seven = pl.ds(0, N_DEV - 1)
        out.append((pltpu.make_async_copy(g_ref.at[me], land_ref.at[me], local_sem), sends,
                    remote(g_ref.at[seven], land_ref.at[seven], (x, y, c))))
    return out


_HBM = pl.BlockSpec(memory_space=pltpu.HBM)
_SEM = pl.BlockSpec(memory_space=pltpu.SEMAPHORE)
_ORDERED = pltpu.CompilerParams(has_side_effects=pltpu.SideEffectType.DATAFLOW_SIDE_EFFECTING)
SEMS_PER_ARRAY = 3


def _exchange_start(slabs, name):
    n = len(slabs)
    n_sems = SEMS_PER_ARRAY * n

    def body(*refs):
        g_refs, land_refs, sems, token = refs[:n], refs[n:2 * n], refs[2 * n:2 * n + n_sems], refs[-1]
        sems = [sems[SEMS_PER_ARRAY * a:SEMS_PER_ARRAY * (a + 1)] for a in range(n)]
        for local, sends, _ in _exchange_copies(g_refs, land_refs, sems):
            local.start()
            for cp in sends:
                cp.start()
        token[...] = jnp.zeros_like(token)

    buffers = [pltpu.HBM(g.shape, g.dtype) for g in slabs] * 2
    outs = pl.pallas_call(
        body, name=name, in_specs=[_HBM] * (2 * n),
        out_specs=[_SEM] * n_sems + [_HBM] * (2 * n) + [pl.BlockSpec(memory_space=pltpu.VMEM)],
        out_shape=[pltpu.SemaphoreType.DMA(())] * n_sems + buffers + [jax.ShapeDtypeStruct((8, LANES), F32)],
        input_output_aliases={a: n_sems + a for a in range(2 * n)}, compiler_params=_ORDERED,
    )(*[pltpu.with_memory_space_constraint(g, pltpu.HBM) for g in slabs],
      *[pltpu.with_memory_space_constraint(lax.empty(g.shape, g.dtype), pltpu.HBM) for g in slabs])
    return outs[:n_sems], outs[n_sems:n_sems + n], outs[n_sems + n:n_sems + 2 * n], outs[-1]


def _exchange_wait(sems, slabs, landings, after, name):
    n = len(slabs)
    n_sems = SEMS_PER_ARRAY * n

    def body(*refs):
        g_refs, land_refs, sems = refs[:n], refs[n:2 * n], refs[2 * n:2 * n + n_sems]
        sems = [sems[SEMS_PER_ARRAY * a:SEMS_PER_ARRAY * (a + 1)] for a in range(n)]
        for local, _, all_seven in _exchange_copies(g_refs, land_refs, sems):
            all_seven.wait_recv()
            all_seven.wait_send()
            local.wait()

    outs = pl.pallas_call(
        body, name=name, in_specs=[_HBM] * (2 * n) + [_SEM] * n_sems + [pl.BlockSpec(memory_space=pl.ANY)],
        out_specs=[_HBM] * (2 * n), out_shape=[pltpu.HBM(g.shape, g.dtype) for g in slabs] * 2,
        input_output_aliases={a: a for a in range(2 * n)}, compiler_params=_ORDERED,
    )(*slabs, *landings, *sems, after)
    return outs[n:]


def _block_rows(r, row_bytes, budget=1 << 20):
    cap = max(8, budget // row_bytes)
    return max(d for d in range(8, min(r, cap) + 1, 8) if r % d == 0)


def _sum_adamw(parts, w, m, v, name):
    n_parts, r, c = parts.shape
    br = _block_rows(r, 4 * c)

    def body(p_ref, w_ref, m_ref, v_ref, g_out, d_out, m_out, v_out):
        g = p_ref[0].astype(F32)
        for s in range(1, n_parts):
            g = g + p_ref[s].astype(F32)
        m_new = ADAM_B1 * m_ref[...] + (1.0 - ADAM_B1) * g
        v_new = ADAM_B2 * v_ref[...] + (1.0 - ADAM_B2) * (g * g)
        m_hat = m_new / (1.0 - ADAM_B1 ** ADAM_STEP)
        v_hat = v_new / (1.0 - ADAM_B2 ** ADAM_STEP)
        g_out[...] = g
        d_out[...] = -ADAM_LR * (m_hat / (jnp.sqrt(v_hat) + ADAM_EPS) + ADAM_WD * w_ref[...])
        m_out[...] = m_new
        v_out[...] = v_new

    row = pl.BlockSpec((br, c), lambda i: (i, 0))
    return pl.pallas_call(
        body, name=name, grid=(r // br,),
        in_specs=[pl.BlockSpec((n_parts, br, c), lambda i: (0, i, 0)), row, row, row],
        out_specs=[row] * 4, out_shape=[jax.ShapeDtypeStruct((r, c), F32)] * 4,
        compiler_params=_cparams("parallel"),
    )(parts, w, m, v)


def _sum_parts(parts, name):
    n_parts, r, c = parts.shape

    def body(p_ref, o_ref):
        g = p_ref[0]
        for s in range(1, n_parts):
            g = g + p_ref[s]
        o_ref[...] = g

    return pl.pallas_call(
        body, name=name, in_specs=[pl.BlockSpec(memory_space=pltpu.VMEM)],
        out_specs=pl.BlockSpec(memory_space=pltpu.VMEM), out_shape=jax.ShapeDtypeStruct((r, c), F32),
    )(parts)


def _pack(arrays):
    flat = jnp.concatenate([a.reshape(-1) for a in arrays])
    pad = -flat.shape[0] % (8 * LANES)
    if pad:
        flat = jnp.pad(flat, (0, pad))
    return flat.reshape(-1, LANES)


def _unpack(buf, shapes):
    flat = buf.reshape(-1)
    out, at = [], 0
    for shape in shapes:
        size = 1
        for d in shape:
            size *= d
        out.append(lax.slice_in_dim(flat, at, at + size).reshape(shape))
        at += size
    return out


def _local_step(x, target, meta, pre_g, post_g, w_in, conv_w, conv_b, ln_g, ln_b, w_pw2, b_pw2, w_out, ship):
    depth = w_in.shape[0]
    seq = x.shape[0]
    t = -(-(N_META + seq) // ROW_BLK) * ROW_BLK
    tail = t - N_META - seq
    h = jnp.concatenate([meta, x, jnp.zeros((tail, D_MODEL), F32)], axis=0)
    target = jnp.pad(target, ((N_META, tail), (0, 0)))
    w_in_t = jnp.swapaxes(w_in, 1, 2)
    w_pw2_t = jnp.swapaxes(w_pw2, 1, 2)
    w_out_t = jnp.swapaxes(w_out, 1, 2)
    row = lambda a, l: a[l][None, :]

    saved = []
    for l in range(depth):
        pc, qkv, sbg, u = _inproj_fwd(h, row(pre_g, l), w_in[l])
        cout, cv, p, sl = _conv_fwd(pc, conv_w[l], row(conv_b, l), row(ln_g, l), row(ln_b, l), w_pw2[l],
                                    row(b_pw2, l))
        sraw, carries = _attn_fwd(qkv)
        h_new, mixed, mix = _outproj_fwd(h, cout, sraw, sbg, w_out[l], row(post_g, l))
        saved.append((h, pc, qkv, sbg, u, cv, p, sl, sraw, carries, mixed, mix))
        h = h_new

    loss, dh = _loss_and_grad(h, target, seq)

    grads = {k: [None] * depth for k in ("pre_g", "post_g", "conv_w", "conv_b", "ln_g", "ln_b", "b_pw2")}
    token = jnp.zeros((8, LANES), F32)
    for l in reversed(range(depth)):
        h_in, pc, qkv, sbg, u, cv, p, sl, sraw, carries, mixed, mix = saved[l]
        dcout, dsraw, dsbg, dmixed, dg_post = _outproj_bwd(dh, mixed, row(post_g, l) + token[:1, :1], sraw, sbg,
                                                           w_out_t[l])
        dq, dk, dv = _attn_bwd(qkv, carries, dsraw)
        dcv, dgate, dpb, vecs = _conv_bwd_rows(dcout, pc, cv, p, row(ln_g, l), row(ln_b, l), w_pw2_t[l])
        da, db, dconv_w = _conv_bwd_taps(dcv, pc, conv_w[l])
        dh, dproj, dg_pre = _inproj_bwd(dh, h_in, row(pre_g, l), (da, db, dgate, dq, dk, dv, dsbg), w_in_t[l])
        token = ship(l, _weight_grad(u, dproj, "w_in_grad"), _weight_grad(mix, dmixed, "w_out_grad"),
                     _weight_grad(sl, dpb, "w_pw2_grad"))
        grads["pre_g"][l] = dg_pre[0]
        grads["post_g"][l] = dg_post[0]
        grads["b_pw2"][l], grads["ln_g"][l], grads["ln_b"][l], grads["conv_b"][l] = vecs[0], vecs[1], vecs[2], vecs[3]
        grads["conv_w"][l] = dconv_w[:CONV_WIDTH]

    grads = {k: jnp.stack(v) for k, v in grads.items()}
    grads["meta"] = dh[:N_META]
    return loss[0, 0], dh[N_META:N_META + seq], grads


def _shard_major(full, axis):
    shape = full.shape
    split = full.reshape(shape[:axis] + (N_DEV, shape[axis] // N_DEV) + shape[axis + 1:])
    return jnp.moveaxis(split, axis, 0)


def _whole_from_shards(shards, axis):
    moved = jnp.moveaxis(shards, 0, axis)
    shape = moved.shape
    return moved.reshape(shape[:axis] + (shape[axis] * shape[axis + 1],) + shape[axis + 2:])


def kernel(x, meta_tokens, pre_norm_g, post_norm_g, w_in, conv_w, conv_b, conv_ln_g, conv_ln_b, w_pw2, b_pw2, w_out, loss_target, m_meta_tokens, m_pre_norm_g, m_post_norm_g, m_w_in, m_conv_w, m_conv_b, m_conv_ln_g, m_conv_ln_b, m_w_pw2, m_b_pw2, m_w_out, v_meta_tokens, v_pre_norm_g, v_post_norm_g, v_w_in, v_conv_w, v_conv_b, v_conv_ln_g, v_conv_ln_b, v_w_pw2, v_b_pw2, v_w_out):
    me = 4 * lax.axis_index("x") + 2 * lax.axis_index("y") + lax.axis_index("c")

    w_in_s, w_out_s, w_pw2_s, conv_w_s, meta_s = _all_gather(
        [w_in.astype(BF16), w_out.astype(BF16), w_pw2.astype(BF16), conv_w, meta_tokens], "gather_weights")
    w_in_full = _whole_from_shards(w_in_s, 2)
    w_out_full = _whole_from_shards(w_out_s, 1)
    w_pw2_full = _whole_from_shards(w_pw2_s, 1)
    conv_w_full = _whole_from_shards(conv_w_s, 2)
    meta_full = _whole_from_shards(meta_s, 1)

    depth = w_in.shape[0]
    in_flight = [None] * depth

    def ship(l, dw_in, dw_out, dw_pw2):
        slabs = [_shard_major(dw, axis).astype(BF16) for dw, axis in ((dw_in, 1), (dw_out, 0), (dw_pw2, 0))]
        *in_flight[l], token = _exchange_start(slabs, f"exchange_start_{l}")
        return token

    loss, dx, grads = _local_step(x[0], loss_target[0], meta_full, pre_norm_g, post_norm_g, w_in_full, conv_w_full,
                                  conv_b, conv_ln_g, conv_ln_b, w_pw2_full, b_pw2, w_out_full, ship)
    loss = lax.psum(loss, ("x", "y", "c"))
    landed = [_exchange_wait(*in_flight[l], dx, f"exchange_wait_{l}") for l in range(depth)]

    def update(a, w, m, v, name):
        outs = [_sum_adamw(landed[l][a], w[l], m[l], v[l], name) for l in range(depth)]
        return [jnp.stack(o) for o in zip(*outs)]

    g_w_in, d_w_in, nm_w_in, nv_w_in = update(0, w_in, m_w_in, v_w_in, "adamw_w_in")
    g_w_out, d_w_out, nm_w_out, nv_w_out = update(1, w_out, m_w_out, v_w_out, "adamw_w_out")
    g_w_pw2, d_w_pw2, nm_w_pw2, nv_w_pw2 = update(2, w_pw2, m_w_pw2, v_w_pw2, "adamw_w_pw2")

    small_names = ("pre_g", "post_g", "conv_b", "ln_g", "ln_b", "b_pw2", "conv_w", "meta")
    small_full = [grads[k] for k in small_names]
    gathered, = _all_gather([_pack(small_full)], "gather_small_grads")
    summed = _sum_parts(gathered, "sum_small_grads")
    g_small = dict(zip(small_names, _unpack(summed, [a.shape for a in small_full])))
    g_small["conv_w"] = lax.dynamic_slice_in_dim(g_small["conv_w"], me * conv_w.shape[2], conv_w.shape[2], axis=2)
    g_small["meta"] = lax.dynamic_slice_in_dim(g_small["meta"], me * meta_tokens.shape[1], meta_tokens.shape[1], axis=1)
    small_w = dict(zip(small_names, (pre_norm_g, post_norm_g, conv_b, conv_ln_g, conv_ln_b, b_pw2, conv_w, meta_tokens)))
    small_m = (m_pre_norm_g, m_post_norm_g, m_conv_b, m_conv_ln_g, m_conv_ln_b, m_b_pw2, m_conv_w, m_meta_tokens)
    small_v = (v_pre_norm_g, v_post_norm_g, v_conv_b, v_conv_ln_g, v_conv_ln_b, v_b_pw2, v_conv_w, v_meta_tokens)
    small_shapes = [small_w[k].shape for k in small_names]
    outs = _sum_adamw(_pack([g_small[k] for k in small_names])[None], _pack([small_w[k] for k in small_names]),
                      _pack(small_m), _pack(small_v), "adamw_small_weights")
    g_s, d_s, nm_s, nv_s = [dict(zip(small_names, _unpack(o, small_shapes))) for o in outs]

    def ordered(s, w_in_, w_pw2_, w_out_):
        return (s["meta"], s["pre_g"], s["post_g"], w_in_, s["conv_w"], s["conv_b"], s["ln_g"], s["ln_b"], w_pw2_,
                s["b_pw2"], w_out_)

    return (loss, dx[None], *ordered(g_s, g_w_in, g_w_pw2, g_w_out), *ordered(d_s, d_w_in, d_w_pw2, d_w_out),
            *ordered(nm_s, nm_w_in, nm_w_pw2, nm_w_out), *ordered(nv_s, nv_w_in, nv_w_pw2, nv_w_out))
```

```python
import functools

import jax
import jax.numpy as jnp
from jax import lax
from jax.experimental import pallas as pl
from jax.experimental.pallas import tpu as pltpu

F32 = jnp.float32
BF16 = jnp.bfloat16

D_MODEL = 1024
D_CONV = 512
D_SB = 512
HEAD_DIM = 64
HEADS_PER_BLOCK = 4
HEAD_BLK = HEADS_PER_BLOCK * HEAD_DIM
CONV_WIDTH = 31
N_META = 16
D_IN = 3 * D_CONV + 4 * D_SB
RMS_EPS = 1e-6
LN_EPS = 1e-5
SB_SCALE = HEAD_DIM ** -0.5

ADAM_LR = 0.001
ADAM_B1 = 0.9
ADAM_B2 = 0.999
ADAM_EPS = 1e-08
ADAM_WD = 0.01
ADAM_STEP = 10

N_DEV = 8
LANES = 128
ROW_BLK = 256
HALO = 32
VMEM_LIMIT = 56 * 1024 * 1024
MESH = pl.DeviceIdType.MESH


def _cparams(*sem):
    return pltpu.CompilerParams(dimension_semantics=sem, vmem_limit_bytes=VMEM_LIMIT)


def _rows(n_cols, col=0):
    return pl.BlockSpec((ROW_BLK, n_cols), lambda i, col=col: (i, col))


def _whole(shape):
    return pl.BlockSpec(shape, lambda i: (0,) * len(shape))


def _sigmoid(x):
    return jax.nn.sigmoid(x)


def _dsilu(x, s):
    return s * (1.0 + x * (1.0 - s))


def _dot(a, b):
    return jnp.dot(a, b, preferred_element_type=F32)


def _dot_nt(a, b):
    return lax.dot_general(a, b, (((1,), (1,)), ((), ())), preferred_element_type=F32)


def _dot_tn(a, b):
    return lax.dot_general(a, b, (((0,), (0,)), ((), ())), preferred_element_type=F32)


def _inproj_fwd(h, g_pre, w_in):
    t = h.shape[0]

    def body(h_ref, g_ref, w_ref, pc_ref, qkv_ref, sbg_ref, u_ref):
        x = h_ref[...]
        r = lax.rsqrt(jnp.mean(x * x, axis=-1, keepdims=True) + RMS_EPS)
        u = (x * r * g_ref[...]).astype(BF16)
        u_ref[...] = u
        pc_ref[...] = _dot(u, w_ref[:, 0:1536])
        qkv_ref[...] = _dot(u, w_ref[:, 1536:3072]).astype(BF16)
        sbg_ref[...] = _dot(u, w_ref[:, 3072:3584])

    return pl.pallas_call(
        body, name="inproj_fwd", grid=(t // ROW_BLK,),
        in_specs=[_rows(D_MODEL), _whole((1, D_MODEL)), _whole((D_MODEL, D_IN))],
        out_specs=[_rows(1536), _rows(1536), _rows(D_SB), _rows(D_MODEL)],
        out_shape=[jax.ShapeDtypeStruct((t, 1536), F32), jax.ShapeDtypeStruct((t, 1536), BF16),
                   jax.ShapeDtypeStruct((t, D_SB), F32), jax.ShapeDtypeStruct((t, D_MODEL), BF16)],
        compiler_params=_cparams("parallel"),
    )(h, g_pre, w_in)


def _prev_halo(col):
    per = ROW_BLK // HALO
    return pl.BlockSpec((HALO, D_CONV), lambda i, col=col: (jnp.maximum(i * per - 1, 0), col))


def _fill_glu(buf, i, a_ref, b_ref, ha_ref, hb_ref):
    halo = ha_ref[...] * _sigmoid(hb_ref[...])
    buf[0:HALO, :] = jnp.where(i > 0, halo, 0.0)
    buf[HALO:HALO + ROW_BLK, :] = a_ref[...] * _sigmoid(b_ref[...])


def _layer_norm_stats(cv):
    mu = jnp.mean(cv, axis=-1, keepdims=True)
    xc = cv - mu
    rstd = lax.rsqrt(jnp.mean(xc * xc, axis=-1, keepdims=True) + LN_EPS)
    return xc * rstd, rstd


def _conv_fwd(pc, conv_w, conv_b, ln_g, ln_b, w_pw2, b_pw2):
    t = pc.shape[0]

    def body(a_ref, b_ref, gate_ref, ha_ref, hb_ref, cw_ref, cb_ref, lg_ref, lb_ref, wp_ref, bp_ref,
             cout_ref, cv_ref, p_ref, sl_ref, buf):
        i = pl.program_id(0)
        _fill_glu(buf, i, a_ref, b_ref, ha_ref, hb_ref)
        acc = jnp.zeros((ROW_BLK, D_CONV), F32) + cb_ref[...]
        for j in range(CONV_WIDTH):
            acc = acc + cw_ref[j:j + 1, :] * buf[pl.ds(HALO - (CONV_WIDTH - 1) + j, ROW_BLK), :]
        cv_ref[...] = acc
        xh, _ = _layer_norm_stats(acc)
        ln = xh * lg_ref[...] + lb_ref[...]
        sl = (ln * _sigmoid(ln)).astype(BF16)
        sl_ref[...] = sl
        p = _dot(sl, wp_ref[...]) + bp_ref[...]
        p_ref[...] = p
        gate = gate_ref[...]
        cout_ref[...] = (p * (gate * _sigmoid(gate))).astype(BF16)

    vec = _whole((1, D_CONV))
    return pl.pallas_call(
        body, name="conv_fwd", grid=(t // ROW_BLK,),
        in_specs=[_rows(D_CONV, 0), _rows(D_CONV, 1), _rows(D_CONV, 2), _prev_halo(0), _prev_halo(1),
                  _whole((CONV_WIDTH, D_CONV)), vec, vec, vec, _whole((D_CONV, D_CONV)), vec],
        out_specs=[_rows(D_CONV)] * 4,
        out_shape=[jax.ShapeDtypeStruct((t, D_CONV), BF16), jax.ShapeDtypeStruct((t, D_CONV), F32),
                   jax.ShapeDtypeStruct((t, D_CONV), F32), jax.ShapeDtypeStruct((t, D_CONV), BF16)],
        scratch_shapes=[pltpu.VMEM((HALO + ROW_BLK, D_CONV), F32)],
        compiler_params=_cparams("parallel"),
    )(pc, pc, pc, pc, pc, conv_w, conv_b, ln_g, ln_b, w_pw2, b_pw2)


def _lower_triangle():
    row = lax.broadcasted_iota(jnp.int32, (ROW_BLK, ROW_BLK), 0)
    col = lax.broadcasted_iota(jnp.int32, (ROW_BLK, ROW_BLK), 1)
    return row > col


def _lower_triangle_t():
    row = lax.broadcasted_iota(jnp.int32, (ROW_BLK, ROW_BLK), 0)
    col = lax.broadcasted_iota(jnp.int32, (ROW_BLK, ROW_BLK), 1)
    return row < col


def _tri_sum(x, umat):
    return _dot(x.astype(BF16), umat)


def _log_gates(z):
    ls = -(jnp.maximum(z, 0.0) + jnp.log(1.0 + jnp.exp(-jnp.abs(z))))
    return ls, z + ls


def _head_lanes(hh):
    lane = lax.broadcasted_iota(jnp.int32, (ROW_BLK, HEAD_BLK), 1)
    return (lane >= HEAD_DIM * hh) & (lane < HEAD_DIM * (hh + 1))


def _merge_heads(acc_ref):
    out = acc_ref[HEADS_PER_BLOCK - 1]
    for hh in range(HEADS_PER_BLOCK - 1):
        out = jnp.where(_head_lanes(hh), acc_ref[hh], out)
    return out


def _qkv_specs(t):
    n_blk = D_SB // HEAD_BLK
    return [pl.BlockSpec((ROW_BLK, HEAD_BLK), lambda hp, i: (i, hp)),
            pl.BlockSpec((t, HEAD_BLK), lambda hp, i: (0, n_blk + hp)),
            pl.BlockSpec((t, HEAD_BLK), lambda hp, i: (0, 2 * n_blk + hp))]


def _carry_spec():
    return pl.BlockSpec((HEADS_PER_BLOCK, ROW_BLK, LANES), lambda hp, i: (hp, i, 0))


def _attn_fwd(qkv):
    t = qkv.shape[0]
    assert t // ROW_BLK <= LANES

    def body(q_ref, k_ref, v_ref, o_ref, c_ref, acc_ref, run_ref, qm_ref):
        i = pl.program_id(1)
        lower = _lower_triangle()
        umat = jnp.where(lower, 1.0, 0.0).astype(BF16)
        lane = lax.broadcasted_iota(jnp.int32, (ROW_BLK, LANES), 1)
        q = q_ref[...]
        heads = range(HEADS_PER_BLOCK)
        for hh in heads:
            qm_ref[hh] = jnp.where(_head_lanes(hh), q, jnp.zeros_like(q)) * jnp.asarray(SB_SCALE, BF16)
        acc_ref[...] = jnp.zeros_like(acc_ref)
        c_ref[...] = jnp.zeros_like(c_ref)
        run_ref[...] = jnp.zeros_like(run_ref)

        def block(jb, diagonal):
            start = pl.multiple_of(jb * ROW_BLK, ROW_BLK)
            kb = k_ref[pl.ds(start, ROW_BLK), :]
            vb = v_ref[pl.ds(start, ROW_BLK), :]
            zs = [_dot_nt(qm_ref[hh], kb) for hh in heads]
            logits = []
            for hh in heads:
                ls, lb = _log_gates(zs[hh])
                if diagonal:
                    ls = jnp.where(lower, ls, 0.0)
                run = run_ref[hh]
                if not diagonal:
                    c_ref[hh] = jnp.where(lane == jb, run, c_ref[hh])
                logits.append(lb + jnp.concatenate([run, run], axis=1) + _tri_sum(ls, umat))
                run_ref[hh] = run + jnp.sum(ls, axis=1, keepdims=True)
            for hh in heads:
                a = jnp.exp(logits[hh])
                if diagonal:
                    a = jnp.where(lower, a, 0.0)
                acc_ref[hh] += _dot(a.astype(BF16), vb)

        block(i, True)

        @pl.loop(0, i)
        def _(n):
            block(i - 1 - n, False)

        o_ref[...] = _merge_heads(acc_ref)

    per_head = (HEADS_PER_BLOCK, ROW_BLK, HEAD_BLK)
    return pl.pallas_call(
        body, name="attn_fwd", grid=(D_SB // HEAD_BLK, t // ROW_BLK),
        in_specs=_qkv_specs(t),
        out_specs=[pl.BlockSpec((ROW_BLK, HEAD_BLK), lambda hp, i: (i, hp)), _carry_spec()],
        out_shape=[jax.ShapeDtypeStruct((t, D_SB), F32),
                   jax.ShapeDtypeStruct((D_SB // HEAD_DIM, t, LANES), F32)],
        scratch_shapes=[pltpu.VMEM(per_head, F32), pltpu.VMEM((HEADS_PER_BLOCK, ROW_BLK, LANES), F32),
                        pltpu.VMEM(per_head, BF16)],
        compiler_params=_cparams("arbitrary", "arbitrary"),
    )(qkv, qkv, qkv)


def _outproj_fwd(h, cout, sraw, sbg, w_out, g_post):
    t = h.shape[0]

    def body(h_ref, c_ref, s_ref, g_ref, w_ref, gp_ref, hn_ref, mixed_ref, mix_ref):
        gate = g_ref[...]
        mix_ref[:, 0:D_CONV] = c_ref[...]
        mix_ref[:, D_CONV:] = (s_ref[...] * (gate * _sigmoid(gate))).astype(BF16)
        mixed = _dot(mix_ref[...], w_ref[...])
        mixed_ref[...] = mixed
        r = lax.rsqrt(jnp.mean(mixed * mixed, axis=-1, keepdims=True) + RMS_EPS)
        hn_ref[...] = h_ref[...] + mixed * r * gp_ref[...]

    return pl.pallas_call(
        body, name="outproj_fwd", grid=(t // ROW_BLK,),
        in_specs=[_rows(D_MODEL), _rows(D_CONV), _rows(D_SB), _rows(D_SB), _whole((D_MODEL, D_MODEL)),
                  _whole((1, D_MODEL))],
        out_specs=[_rows(D_MODEL)] * 3,
        out_shape=[jax.ShapeDtypeStruct((t, D_MODEL), F32), jax.ShapeDtypeStruct((t, D_MODEL), F32),
                   jax.ShapeDtypeStruct((t, D_MODEL), BF16)],
        compiler_params=_cparams("parallel"),
    )(h, cout, sraw, sbg, w_out, g_post)


def _loss_and_grad(h, target, seq):
    t = h.shape[0]

    def body(h_ref, t_ref, loss_ref, dh_ref):
        i = pl.program_id(0)

        @pl.when(i == 0)
        def _():
            loss_ref[...] = jnp.zeros_like(loss_ref)

        row = i * ROW_BLK + lax.broadcasted_iota(jnp.int32, (ROW_BLK, D_MODEL), 0)
        real = (row >= N_META) & (row < N_META + seq)
        diff = jnp.where(real, h_ref[...] - t_ref[...], 0.0)
        sq = jnp.sum(jnp.sum(diff * diff, axis=1, keepdims=True), axis=0, keepdims=True)
        loss_ref[...] += (0.5 / D_MODEL) * sq
        dh_ref[...] = diff * (1.0 / D_MODEL)

    return pl.pallas_call(
        body, name="loss", grid=(t // ROW_BLK,),
        in_specs=[_rows(D_MODEL), _rows(D_MODEL)],
        out_specs=[_whole((1, 1)), _rows(D_MODEL)],
        out_shape=[jax.ShapeDtypeStruct((1, 1), F32), jax.ShapeDtypeStruct((t, D_MODEL), F32)],
        compiler_params=_cparams("arbitrary"),
    )(h, target)


def _outproj_bwd(dh, mixed, g_post, sraw, sbg, w_out_t):
    t = dh.shape[0]

    def body(dh_ref, mixed_ref, gp_ref, s_ref, g_ref, wt_ref, dc_ref, ds_ref, dg_ref, dmb_ref, dgp_ref):
        @pl.when(pl.program_id(0) == 0)
        def _():
            dgp_ref[...] = jnp.zeros_like(dgp_ref)

        mixed = mixed_ref[...]
        r = lax.rsqrt(jnp.mean(mixed * mixed, axis=-1, keepdims=True) + RMS_EPS)
        nh = mixed * r
        dy = dh_ref[...]
        dgp_ref[...] += jnp.sum(dy * nh, axis=0, keepdims=True)
        dn = dy * gp_ref[...]
        dmixed = (r * (dn - nh * jnp.mean(dn * nh, axis=-1, keepdims=True))).astype(BF16)
        dmb_ref[...] = dmixed
        dmix = _dot(dmixed, wt_ref[...])
        dc_ref[...] = dmix[:, 0:D_CONV]
        dsg = dmix[:, D_CONV:]
        gate = g_ref[...]
        sg = _sigmoid(gate)
        ds_ref[...] = dsg * (gate * sg)
        dg_ref[...] = dsg * s_ref[...] * _dsilu(gate, sg)

    return pl.pallas_call(
        body, name="outproj_bwd", grid=(t // ROW_BLK,),
        in_specs=[_rows(D_MODEL), _rows(D_MODEL), _whole((1, D_MODEL)), _rows(D_SB), _rows(D_SB),
                  _whole((D_MODEL, D_MODEL))],
        out_specs=[_rows(D_CONV), _rows(D_SB), _rows(D_SB), _rows(D_MODEL), _whole((1, D_MODEL))],
        out_shape=[jax.ShapeDtypeStruct((t, D_CONV), F32), jax.ShapeDtypeStruct((t, D_SB), F32),
                   jax.ShapeDtypeStruct((t, D_SB), F32), jax.ShapeDtypeStruct((t, D_MODEL), BF16),
                   jax.ShapeDtypeStruct((1, D_MODEL), F32)],
        compiler_params=_cparams("arbitrary"),
    )(dh, mixed, g_post, sraw, sbg, w_out_t)


def _attn_bwd(qkv, carries, do):
    t = qkv.shape[0]

    def body(q_ref, k_ref, v_ref, c_ref, do_ref, dq_ref, dk_ref, dv_ref, acc_ref, seen_ref, qm_ref, dom_ref):
        i = pl.program_id(1)

        @pl.when(i == 0)
        def _():
            dk_ref[...] = jnp.zeros_like(dk_ref)
            dv_ref[...] = jnp.zeros_like(dv_ref)

        lower = _lower_triangle()
        umat = jnp.where(lower, 1.0, 0.0).astype(BF16)
        umat_t = jnp.where(_lower_triangle_t(), 1.0, 0.0).astype(BF16)
        lane = lax.broadcasted_iota(jnp.int32, (ROW_BLK, LANES), 1)
        q = q_ref[...]
        dof = do_ref[...]
        heads = range(HEADS_PER_BLOCK)
        for hh in heads:
            qm_ref[hh] = jnp.where(_head_lanes(hh), q, jnp.zeros_like(q)) * jnp.asarray(SB_SCALE, BF16)
            dom_ref[hh] = jnp.where(_head_lanes(hh), dof, 0.0).astype(BF16)
        acc_ref[...] = jnp.zeros_like(acc_ref)
        seen_ref[...] = jnp.zeros_like(seen_ref)

        def block(jb, diagonal):
            start = pl.multiple_of(jb * ROW_BLK, ROW_BLK)
            kb = k_ref[pl.ds(start, ROW_BLK), :]
            vb = v_ref[pl.ds(start, ROW_BLK), :]
            zs = [_dot_nt(qm_ref[hh], kb) for hh in heads]
            das = [_dot_nt(dom_ref[hh], vb) for hh in heads]
            lbs, logits = [], []
            for hh in heads:
                ls, lb = _log_gates(zs[hh])
                if diagonal:
                    ls = jnp.where(lower, ls, 0.0)
                    logits.append(lb + _tri_sum(ls, umat))
                else:
                    right = jnp.sum(jnp.where(lane == jb, c_ref[hh], 0.0), axis=1, keepdims=True)
                    logits.append(lb + right + _tri_sum(ls, umat))
                lbs.append(lb)
            abs_, gs, befores = [], [], []
            for hh in heads:
                a = jnp.exp(logits[hh])
                if diagonal:
                    a = jnp.where(lower, a, 0.0)
                g = das[hh] * a
                seen = seen_ref[hh]
                befores.append(jnp.concatenate([seen, seen], axis=1) + _tri_sum(g, umat_t))
                seen_ref[hh] = seen + jnp.sum(g, axis=1, keepdims=True)
                abs_.append(a.astype(BF16))
                gs.append(g)
            dk = dv = None
            for hh in heads:
                dz = gs[hh] - jnp.exp(lbs[hh]) * (gs[hh] + befores[hh])
                if diagonal:
                    dz = jnp.where(lower, dz, 0.0)
                dzb = dz.astype(BF16)
                acc_ref[hh] += _dot(dzb, kb)
                dk_h = _dot_tn(dzb, qm_ref[hh])
                dv_h = _dot_tn(abs_[hh], dom_ref[hh])
                dk = dk_h if dk is None else dk + dk_h
                dv = dv_h if dv is None else dv + dv_h
            dk_ref[pl.ds(start, ROW_BLK), :] += dk
            dv_ref[pl.ds(start, ROW_BLK), :] += dv

        @pl.loop(0, i)
        def _(jb):
            block(jb, False)

        block(i, True)
        dq_ref[...] = _merge_heads(acc_ref) * SB_SCALE

    blk = pl.BlockSpec((ROW_BLK, HEAD_BLK), lambda hp, i: (i, hp))
    full = pl.BlockSpec((t, HEAD_BLK), lambda hp, i: (0, hp))
    per_head = (HEADS_PER_BLOCK, ROW_BLK, HEAD_BLK)
    return pl.pallas_call(
        body, name="attn_bwd", grid=(D_SB // HEAD_BLK, t // ROW_BLK),
        in_specs=_qkv_specs(t) + [_carry_spec(), blk],
        out_specs=[blk, full, full],
        out_shape=[jax.ShapeDtypeStruct((t, D_SB), F32)] * 3,
        scratch_shapes=[pltpu.VMEM(per_head, F32), pltpu.VMEM((HEADS_PER_BLOCK, ROW_BLK, LANES), F32),
                        pltpu.VMEM(per_head, BF16), pltpu.VMEM(per_head, BF16)],
        compiler_params=_cparams("arbitrary", "arbitrary"),
    )(qkv, qkv, qkv, carries, do)


def _conv_bwd_rows(dcout, pc, cv, p, ln_g, ln_b, w_pw2_t):
    t = dcout.shape[0]

    def body(dc_ref, gate_ref, cv_ref, p_ref, lg_ref, lb_ref, wt_ref, dcv_ref, dgate_ref, dpb_ref, vec_ref):
        @pl.when(pl.program_id(0) == 0)
        def _():
            vec_ref[...] = jnp.zeros_like(vec_ref)

        dc = dc_ref[...]
        gate = gate_ref[...]
        sg = _sigmoid(gate)
        dp = dc * (gate * sg)
        dgate_ref[...] = dc * p_ref[...] * _dsilu(gate, sg)
        dpb = dp.astype(BF16)
        dpb_ref[...] = dpb
        xh, rstd = _layer_norm_stats(cv_ref[...])
        ln = xh * lg_ref[...] + lb_ref[...]
        s2 = _sigmoid(ln)
        dln = _dot(dpb, wt_ref[...]) * _dsilu(ln, s2)
        dxh = dln * lg_ref[...]
        dcv = rstd * (dxh - jnp.mean(dxh, axis=-1, keepdims=True)
                      - xh * jnp.mean(dxh * xh, axis=-1, keepdims=True))
        dcv_ref[...] = dcv
        vec_ref[0:1, :] += jnp.sum(dp, axis=0, keepdims=True)
        vec_ref[1:2, :] += jnp.sum(dln * xh, axis=0, keepdims=True)
        vec_ref[2:3, :] += jnp.sum(dln, axis=0, keepdims=True)
        vec_ref[3:4, :] += jnp.sum(dcv, axis=0, keepdims=True)

    vec = _whole((1, D_CONV))
    return pl.pallas_call(
        body, name="conv_bwd_rows", grid=(t // ROW_BLK,),
        in_specs=[_rows(D_CONV), _rows(D_CONV, 2), _rows(D_CONV), _rows(D_CONV), vec, vec,
                  _whole((D_CONV, D_CONV))],
        out_specs=[_rows(D_CONV), _rows(D_CONV), _rows(D_CONV), _whole((8, D_CONV))],
        out_shape=[jax.ShapeDtypeStruct((t, D_CONV), F32), jax.ShapeDtypeStruct((t, D_CONV), F32),
                   jax.ShapeDtypeStruct((t, D_CONV), BF16), jax.ShapeDtypeStruct((8, D_CONV), F32)],
        compiler_params=_cparams("arbitrary"),
    )(dcout, pc, cv, p, ln_g, ln_b, w_pw2_t)


def _conv_bwd_taps(dcv, pc, conv_w):
    t = dcv.shape[0]
    n_halo = t // HALO
    per = ROW_BLK // HALO

    def body(d_ref, dn_ref, a_ref, b_ref, ha_ref, hb_ref, cw_ref, da_ref, db_ref, dw_ref, cbuf, dbuf):
        i = pl.program_id(0)

        @pl.when(i == 0)
        def _():
            dw_ref[...] = jnp.zeros_like(dw_ref)

        _fill_glu(cbuf, i, a_ref, b_ref, ha_ref, hb_ref)
        dcv = d_ref[...]
        dbuf[0:ROW_BLK, :] = dcv
        dbuf[ROW_BLK:ROW_BLK + HALO, :] = jnp.where(i < pl.num_programs(0) - 1, dn_ref[...], 0.0)
        acc = jnp.zeros((ROW_BLK, D_CONV), F32)
        for j in range(CONV_WIDTH):
            acc = acc + cw_ref[j:j + 1, :] * dbuf[pl.ds(CONV_WIDTH - 1 - j, ROW_BLK), :]
            seen = cbuf[pl.ds(HALO - (CONV_WIDTH - 1) + j, ROW_BLK), :]
            dw_ref[j:j + 1, :] += jnp.sum(dcv * seen, axis=0, keepdims=True)
        a = a_ref[...]
        sb = _sigmoid(b_ref[...])
        da_ref[...] = acc * sb
        db_ref[...] = acc * a * sb * (1.0 - sb)

    return pl.pallas_call(
        body, name="conv_bwd_taps", grid=(t // ROW_BLK,),
        in_specs=[_rows(D_CONV),
                  pl.BlockSpec((HALO, D_CONV), lambda i: (jnp.minimum((i + 1) * per, n_halo - 1), 0)),
                  _rows(D_CONV, 0), _rows(D_CONV, 1), _prev_halo(0), _prev_halo(1),
                  _whole((CONV_WIDTH, D_CONV))],
        out_specs=[_rows(D_CONV), _rows(D_CONV), _whole((32, D_CONV))],
        out_shape=[jax.ShapeDtypeStruct((t, D_CONV), F32), jax.ShapeDtypeStruct((t, D_CONV), F32),
                   jax.ShapeDtypeStruct((32, D_CONV), F32)],
        scratch_shapes=[pltpu.VMEM((HALO + ROW_BLK, D_CONV), F32), pltpu.VMEM((ROW_BLK + HALO, D_CONV), F32)],
        compiler_params=_cparams("arbitrary"),
    )(dcv, dcv, pc, pc, pc, pc, conv_w)


def _inproj_bwd(dh_out, h, g_pre, pieces, w_in_t):
    t = h.shape[0]

    def body(dh_ref, h_ref, g_ref, *rest):
        piece_refs, (wt_ref, dhin_ref, dproj_ref, dg_ref) = rest[:7], rest[7:]

        @pl.when(pl.program_id(0) == 0)
        def _():
            dg_ref[...] = jnp.zeros_like(dg_ref)

        for k, ref in enumerate(piece_refs):
            dproj_ref[:, 512 * k:512 * (k + 1)] = ref[...].astype(BF16)
        du = _dot(dproj_ref[...], wt_ref[...])
        x = h_ref[...]
        r = lax.rsqrt(jnp.mean(x * x, axis=-1, keepdims=True) + RMS_EPS)
        xh = x * r
        dg_ref[...] += jnp.sum(du * xh, axis=0, keepdims=True)
        dxh = du * g_ref[...]
        dhin_ref[...] = dh_ref[...] + r * (dxh - xh * jnp.mean(dxh * xh, axis=-1, keepdims=True))

    return pl.pallas_call(
        body, name="inproj_bwd", grid=(t // ROW_BLK,),
        in_specs=[_rows(D_MODEL), _rows(D_MODEL), _whole((1, D_MODEL))] + [_rows(512)] * 7
                 + [_whole((D_IN, D_MODEL))],
        out_specs=[_rows(D_MODEL), _rows(D_IN), _whole((1, D_MODEL))],
        out_shape=[jax.ShapeDtypeStruct((t, D_MODEL), F32), jax.ShapeDtypeStruct((t, D_IN), BF16),
                   jax.ShapeDtypeStruct((1, D_MODEL), F32)],
        compiler_params=_cparams("arbitrary"),
    )(dh_out, h, g_pre, *pieces, w_in_t)


def _weight_grad(xb, dyb, name):
    t, k = xb.shape
    n = dyb.shape[1]
    tn = min(n, 512)

    def body(x_ref, dy_ref, o_ref):
        @pl.when(pl.program_id(1) == 0)
        def _():
            o_ref[...] = jnp.zeros_like(o_ref)

        o_ref[...] += _dot_tn(x_ref[...], dy_ref[...])

    return pl.pallas_call(
        body, name=name, grid=(n // tn, t // ROW_BLK),
        in_specs=[pl.BlockSpec((ROW_BLK, k), lambda j, i: (i, 0)), pl.BlockSpec((ROW_BLK, tn), lambda j, i: (i, j))],
        out_specs=pl.BlockSpec((k, tn), lambda j, i: (0, j)),
        out_shape=jax.ShapeDtypeStruct((k, n), F32),
        compiler_params=_cparams("parallel", "arbitrary"),
    )(xb, dyb)


def _position():
    return lax.axis_index("x"), lax.axis_index("y"), lax.axis_index("c")


def _comm_call(body, name, ins, out_shapes):
    n = len(ins)
    hbm = pl.BlockSpec(memory_space=pltpu.HBM)
    return pl.pallas_call(
        functools.partial(body, n), name=name, in_specs=[hbm] * n, out_specs=[hbm] * n, out_shape=out_shapes,
        scratch_shapes=[pltpu.SemaphoreType.DMA((n, N_DEV - 1)), pltpu.SemaphoreType.DMA((n, N_DEV - 1)),
                        pltpu.SemaphoreType.DMA((n,))],
    )(*ins)


def _all_gather(blocks, name):
    def body(n, *refs):
        x_refs, out_refs, (send_sems, recv_sems, local_sems) = refs[:n], refs[n:2 * n], refs[2 * n:]
        x, y, c = _position()
        me, sibling = (x, y, c), (x, y, 1 - c)
        chips = [(1 - x, y), (x, 1 - y), (1 - x, 1 - y)]

        def slot(a, px, py, pc):
            return out_refs[a].at[4 * px + 2 * py + pc]

        def copy(a, k, origin, to, own=False):
            return pltpu.make_async_remote_copy(
                src_ref=x_refs[a] if own else slot(a, *origin), dst_ref=slot(a, *origin),
                send_sem=send_sems.at[a, k], recv_sem=recv_sems.at[a, k], device_id=to, device_id_type=MESH)

        arrays = range(n)
        mine = [pltpu.make_async_copy(x_refs[a], slot(a, *me), local_sems.at[a]) for a in arrays]
        first = [copy(a, 1 + j, me, (*chip, c), own=True) for j, chip in enumerate(chips) for a in arrays]
        first += [copy(a, 0, me, sibling, own=True) for a in arrays]
        for cp in mine + first:
            cp.start()
        passed = []
        for j, chip in enumerate(chips):
            for a in arrays:
                copy(a, 1 + j, (*chip, c), me).wait_recv()
                passed.append(copy(a, 4 + j, (*chip, c), sibling))
                passed[-1].start()
        for a in arrays:
            copy(a, 0, sibling, me).wait_recv()
            for j, chip in enumerate(chips):
                copy(a, 4 + j, (*chip, 1 - c), me).wait_recv()
        for cp in first + passed:
            cp.wait_send()
        for cp in mine:
            cp.wait()

    return _comm_call(body, name, blocks, [jax.ShapeDtypeStruct((N_DEV,) + b.shape, b.dtype) for b in blocks])


def _exchange_copies(g_refs, land_refs, sems, gather):
    x, y, c = _position()
    me = 4 * x + 2 * y + c
    out = []
    for g_ref, land_ref, (send_sem, recv_sem, local_sem) in zip(g_refs, land_refs, sems):
        def mine(slot, g_ref=g_ref):
            return g_ref if gather else g_ref.at[slot]

        def remote(src, dst, dev):
            return pltpu.make_async_remote_copy(src_ref=src, dst_ref=dst, send_sem=send_sem, recv_sem=recv_sem,
                                                device_id=dev, device_id_type=MESH)

        sends = []
        for k in range(1, N_DEV):
            px = 1 - x if k & 4 else x
            py = 1 - y if k & 2 else y
            pc = 1 - c if k & 1 else c
            sends.append(remote(mine(4 * px + 2 * py + pc), land_ref.at[me], (px, py, pc)))
        seven = land_ref.at[pl.ds(0, N_DEV - 1)]
        out.append((pltpu.make_async_copy(mine(me), land_ref.at[me], local_sem), sends, remote(seven, seven, (x, y, c))))
    return out


_HBM = pl.BlockSpec(memory_space=pltpu.HBM)
_SEM = pl.BlockSpec(memory_space=pltpu.SEMAPHORE)
_ORDERED = pltpu.CompilerParams(has_side_effects=pltpu.SideEffectType.DATAFLOW_SIDE_EFFECTING)
SEMS_PER_ARRAY = 3


def _exchange_start(arrays, after, name, gather):
    n = len(arrays)
    n_sems = SEMS_PER_ARRAY * n

    def body(*refs):
        g_refs, land_refs, sems, token = refs[:n], refs[n:2 * n], refs[2 * n + 1:2 * n + 1 + n_sems], refs[-1]
        sems = [sems[SEMS_PER_ARRAY * a:SEMS_PER_ARRAY * (a + 1)] for a in range(n)]
        for local, sends, _ in _exchange_copies(g_refs, land_refs, sems, gather):
            local.start()
            for cp in sends:
                cp.start()
        token[...] = jnp.zeros_like(token)

    buffers = list(arrays) + [lax.empty((N_DEV,) + g.shape if gather else g.shape, g.dtype) for g in arrays]
    outs = pl.pallas_call(
        body, name=name, in_specs=[_HBM] * (2 * n) + [pl.BlockSpec(memory_space=pl.ANY)],
        out_specs=[_SEM] * n_sems + [_HBM] * (2 * n) + [pl.BlockSpec(memory_space=pltpu.VMEM)],
        out_shape=[pltpu.SemaphoreType.DMA(())] * n_sems + [pltpu.HBM(b.shape, b.dtype) for b in buffers]
                  + [jax.ShapeDtypeStruct((8, LANES), F32)],
        input_output_aliases={a: n_sems + a for a in range(2 * n)}, compiler_params=_ORDERED,
    )(*[pltpu.with_memory_space_constraint(b, pltpu.HBM) for b in buffers], after)
    return outs[:n_sems], outs[n_sems:n_sems + n], outs[n_sems + n:n_sems + 2 * n], outs[-1]


def _exchange_wait(sems, arrays, landings, after, name, gather):
    n = len(arrays)
    n_sems = SEMS_PER_ARRAY * n

    def body(*refs):
        g_refs, land_refs, sems = refs[:n], refs[n:2 * n], refs[2 * n:2 * n + n_sems]
        sems = [sems[SEMS_PER_ARRAY * a:SEMS_PER_ARRAY * (a + 1)] for a in range(n)]
        for local, _, all_seven in _exchange_copies(g_refs, land_refs, sems, gather):
            all_seven.wait_recv()
            all_seven.wait_send()
            local.wait()

    buffers = list(arrays) + list(landings)
    outs = pl.pallas_call(
        body, name=name, in_specs=[_HBM] * (2 * n) + [_SEM] * n_sems + [pl.BlockSpec(memory_space=pl.ANY)],
        out_specs=[_HBM] * (2 * n), out_shape=[pltpu.HBM(b.shape, b.dtype) for b in buffers],
        input_output_aliases={a: a for a in range(2 * n)}, compiler_params=_ORDERED,
    )(*buffers, *sems, after)
    return outs[n:]


def _block_rows(r, row_bytes, budget=1 << 20):
    cap = max(8, budget // row_bytes)
    return max(d for d in range(8, min(r, cap) + 1, 8) if r % d == 0)


def _sum_adamw(parts, w, m, v, name):
    n_parts, r, c = parts.shape
    br = _block_rows(r, 4 * c)

    def body(p_ref, w_ref, m_ref, v_ref, g_out, d_out, m_out, v_out):
        g = p_ref[0].astype(F32)
        for s in range(1, n_parts):
            g = g + p_ref[s].astype(F32)
        m_new = ADAM_B1 * m_ref[...] + (1.0 - ADAM_B1) * g
        v_new = ADAM_B2 * v_ref[...] + (1.0 - ADAM_B2) * (g * g)
        m_hat = m_new / (1.0 - ADAM_B1 ** ADAM_STEP)
        v_hat = v_new / (1.0 - ADAM_B2 ** ADAM_STEP)
        g_out[...] = g
        d_out[...] = -ADAM_LR * (m_hat / (jnp.sqrt(v_hat) + ADAM_EPS) + ADAM_WD * w_ref[...])
        m_out[...] = m_new
        v_out[...] = v_new

    row = pl.BlockSpec((br, c), lambda i: (i, 0))
    return pl.pallas_call(
        body, name=name, grid=(r // br,),
        in_specs=[pl.BlockSpec((n_parts, br, c), lambda i: (0, i, 0)), row, row, row],
        out_specs=[row] * 4, out_shape=[jax.ShapeDtypeStruct((r, c), F32)] * 4,
        compiler_params=_cparams("parallel"),
    )(parts, w, m, v)


def _sum_parts(parts, name):
    n_parts, r, c = parts.shape

    def body(p_ref, o_ref):
        g = p_ref[0]
        for s in range(1, n_parts):
            g = g + p_ref[s]
        o_ref[...] = g

    return pl.pallas_call(
        body, name=name, in_specs=[pl.BlockSpec(memory_space=pltpu.VMEM)],
        out_specs=pl.BlockSpec(memory_space=pltpu.VMEM), out_shape=jax.ShapeDtypeStruct((r, c), F32),
    )(parts)


def _pack(arrays):
    flat = jnp.concatenate([a.reshape(-1) for a in arrays])
    pad = -flat.shape[0] % (8 * LANES)
    if pad:
        flat = jnp.pad(flat, (0, pad))
    return flat.reshape(-1, LANES)


def _unpack(buf, shapes):
    flat = buf.reshape(-1)
    out, at = [], 0
    for shape in shapes:
        size = 1
        for d in shape:
            size *= d
        out.append(lax.slice_in_dim(flat, at, at + size).reshape(shape))
        at += size
    return out


def _local_step(x, target, meta, pre_g, post_g, conv_w, conv_b, ln_g, ln_b, b_pw2, weights, ship):
    depth = pre_g.shape[0]
    seq = x.shape[0]
    t = -(-(N_META + seq) // ROW_BLK) * ROW_BLK
    tail = t - N_META - seq
    h = jnp.concatenate([meta, x, jnp.zeros((tail, D_MODEL), F32)], axis=0)
    target = jnp.pad(target, ((N_META, tail), (0, 0)))
    row = lambda a, l: a[l][None, :]

    saved = []
    for l in range(depth):
        w_in, w_pw2, w_out = weights(l, h)
        pc, qkv, sbg, u = _inproj_fwd(h, row(pre_g, l), w_in)
        cout, cv, p, sl = _conv_fwd(pc, conv_w[l], row(conv_b, l), row(ln_g, l), row(ln_b, l), w_pw2, row(b_pw2, l))
        sraw, carries = _attn_fwd(qkv)
        h_new, mixed, mix = _outproj_fwd(h, cout, sraw, sbg, w_out, row(post_g, l))
        saved.append((h, pc, qkv, sbg, u, cv, p, sl, sraw, carries, mixed, mix, w_in.T, w_pw2.T, w_out.T))
        h = h_new

    loss, dh = _loss_and_grad(h, target, seq)

    grads = {k: [None] * depth for k in ("pre_g", "post_g", "conv_w", "conv_b", "ln_g", "ln_b", "b_pw2")}
    token = jnp.zeros((8, LANES), F32)
    for l in reversed(range(depth)):
        h_in, pc, qkv, sbg, u, cv, p, sl, sraw, carries, mixed, mix, w_in_t, w_pw2_t, w_out_t = saved[l]
        dcout, dsraw, dsbg, dmixed, dg_post = _outproj_bwd(dh, mixed, row(post_g, l) + token[:1, :1], sraw, sbg, w_out_t)
        dq, dk, dv = _attn_bwd(qkv, carries, dsraw)
        dcv, dgate, dpb, vecs = _conv_bwd_rows(dcout, pc, cv, p, row(ln_g, l), row(ln_b, l), w_pw2_t)
        da, db, dconv_w = _conv_bwd_taps(dcv, pc, conv_w[l])
        dh, dproj, dg_pre = _inproj_bwd(dh, h_in, row(pre_g, l), (da, db, dgate, dq, dk, dv, dsbg), w_in_t)
        token = ship(l, dh, _weight_grad(u, dproj, "w_in_grad"), _weight_grad(sl, dpb, "w_pw2_grad"),
                     _weight_grad(mix, dmixed, "w_out_grad"))
        grads["pre_g"][l] = dg_pre[0]
        grads["post_g"][l] = dg_post[0]
        grads["b_pw2"][l], grads["ln_g"][l], grads["ln_b"][l], grads["conv_b"][l] = vecs[0], vecs[1], vecs[2], vecs[3]
        grads["conv_w"][l] = dconv_w[:CONV_WIDTH]

    grads = {k: jnp.stack(v) for k, v in grads.items()}
    grads["meta"] = dh[:N_META]
    return loss[0, 0], dh[N_META:N_META + seq], grads


def _shard_major(full, axis):
    shape = full.shape
    split = full.reshape(shape[:axis] + (N_DEV, shape[axis] // N_DEV) + shape[axis + 1:])
    return jnp.moveaxis(split, axis, 0)


def _whole_from_shards(shards, axis):
    moved = jnp.moveaxis(shards, 0, axis)
    shape = moved.shape
    return moved.reshape(shape[:axis] + (shape[axis] * shape[axis + 1],) + shape[axis + 2:])


def kernel(x, meta_tokens, pre_norm_g, post_norm_g, w_in, conv_w, conv_b, conv_ln_g, conv_ln_b, w_pw2, b_pw2, w_out, loss_target, m_meta_tokens, m_pre_norm_g, m_post_norm_g, m_w_in, m_conv_w, m_conv_b, m_conv_ln_g, m_conv_ln_b, m_w_pw2, m_b_pw2, m_w_out, v_meta_tokens, v_pre_norm_g, v_post_norm_g, v_w_in, v_conv_w, v_conv_b, v_conv_ln_g, v_conv_ln_b, v_w_pw2, v_b_pw2, v_w_out):
    me = 4 * lax.axis_index("x") + 2 * lax.axis_index("y") + lax.axis_index("c")

    depth = w_in.shape[0]
    big = [w.astype(BF16) for w in (w_in, w_pw2, w_out)]
    *first, conv_w_s, meta_s = _all_gather([w[0] for w in big] + [conv_w, meta_tokens], "gather_first_layer")
    *gathering, token = _exchange_start([w[1:] for w in big], meta_s, "gather_start", gather=True)
    conv_w_full = _whole_from_shards(conv_w_s, 2)
    meta_full = _whole_from_shards(meta_s, 1)
    shard_axis = (1, 0, 0)
    later = []

    def weights(l, h):
        if l == 0:
            return [_whole_from_shards(s, axis) for s, axis in zip(first, shard_axis)]
        if not later:
            later.extend(_exchange_wait(*gathering, h, "gather_wait", gather=True))
        return [_whole_from_shards(s[:, l - 1], axis) for s, axis in zip(later, shard_axis)]

    in_flight = [None] * depth

    def ship(l, dh, dw_in, dw_pw2, dw_out):
        slabs = [_shard_major(dw, axis).astype(BF16) for dw, axis in zip((dw_in, dw_pw2, dw_out), shard_axis)]
        *in_flight[l], token = _exchange_start(slabs, dh, f"exchange_start_{l}", gather=False)
        return token

    loss, dx, grads = _local_step(x[0], loss_target[0], meta_full, pre_norm_g + token[:1, :1], post_norm_g, conv_w_full,
                                  conv_b, conv_ln_g, conv_ln_b, b_pw2, weights, ship)
    loss = lax.psum(loss, ("x", "y", "c"))
    landed = [_exchange_wait(*in_flight[l], dx, f"exchange_wait_{l}", gather=False) for l in range(depth)]

    def update(a, w, m, v, name):
        outs = [_sum_adamw(landed[l][a], w[l], m[l], v[l], name) for l in range(depth)]
        return [jnp.stack(o) for o in zip(*outs)]

    g_w_in, d_w_in, nm_w_in, nv_w_in = update(0, w_in, m_w_in, v_w_in, "adamw_w_in")
    g_w_pw2, d_w_pw2, nm_w_pw2, nv_w_pw2 = update(1, w_pw2, m_w_pw2, v_w_pw2, "adamw_w_pw2")
    g_w_out, d_w_out, nm_w_out, nv_w_out = update(2, w_out, m_w_out, v_w_out, "adamw_w_out")

    small_names = ("pre_g", "post_g", "conv_b", "ln_g", "ln_b", "b_pw2", "conv_w", "meta")
    small_full = [grads[k] for k in small_names]
    gathered, = _all_gather([_pack(small_full)], "gather_small_grads")
    summed = _sum_parts(gathered, "sum_small_grads")
    g_small = dict(zip(small_names, _unpack(summed, [a.shape for a in small_full])))
    g_small["conv_w"] = lax.dynamic_slice_in_dim(g_small["conv_w"], me * conv_w.shape[2], conv_w.shape[2], axis=2)
    g_small["meta"] = lax.dynamic_slice_in_dim(g_small["meta"], me * meta_tokens.shape[1], meta_tokens.shape[1], axis=1)
    small_w = dict(zip(small_names, (pre_norm_g, post_norm_g, conv_b, conv_ln_g, conv_ln_b, b_pw2, conv_w, meta_tokens)))
    small_m = (m_pre_norm_g, m_post_norm_g, m_conv_b, m_conv_ln_g, m_conv_ln_b, m_b_pw2, m_conv_w, m_meta_tokens)
    small_v = (v_pre_norm_g, v_post_norm_g, v_conv_b, v_conv_ln_g, v_conv_ln_b, v_b_pw2, v_conv_w, v_meta_tokens)
    small_shapes = [small_w[k].shape for k in small_names]
    outs = _sum_adamw(_pack([g_small[k] for k in small_names])[None], _pack([small_w[k] for k in small_names]),
                      _pack(small_m), _pack(small_v), "adamw_small_weights")
    g_s, d_s, nm_s, nv_s = [dict(zip(small_names, _unpack(o, small_shapes))) for o in outs]

    def ordered(s, w_in_, w_pw2_, w_out_):
        return (s["meta"], s["pre_g"], s["post_g"], w_in_, s["conv_w"], s["conv_b"], s["ln_g"], s["ln_b"], w_pw2_,
                s["b_pw2"], w_out_)

    return (loss, dx[None], *ordered(g_s, g_w_in, g_w_pw2, g_w_out), *ordered(d_s, d_w_in, d_w_pw2, d_w_out),
            *ordered(nm_s, nm_w_in, nm_w_pw2, nm_w_out), *ordered(nv_s, nv_w_in, nv_w_pw2, nv_w_out))
```

```python
import functools

import jax
import jax.numpy as jnp
from jax import lax
from jax.experimental import pallas as pl
from jax.experimental.pallas import tpu as pltpu

F32 = jnp.float32
BF16 = jnp.bfloat16

D_MODEL = 1024
D_CONV = 512
D_SB = 512
HEAD_DIM = 64
HEADS_PER_BLOCK = 4
HEAD_BLK = HEADS_PER_BLOCK * HEAD_DIM
CONV_WIDTH = 31
N_META = 16
D_IN = 3 * D_CONV + 4 * D_SB
RMS_EPS = 1e-6
LN_EPS = 1e-5
SB_SCALE = HEAD_DIM ** -0.5

ADAM_LR = 0.001
ADAM_B1 = 0.9
ADAM_B2 = 0.999
ADAM_EPS = 1e-08
ADAM_WD = 0.01
ADAM_STEP = 10

N_DEV = 8
LANES = 128
ROW_BLK = 256
HALO = 32
VMEM_LIMIT = 56 * 1024 * 1024
MESH = pl.DeviceIdType.MESH


def _cparams(*sem):
    return pltpu.CompilerParams(dimension_semantics=sem, vmem_limit_bytes=VMEM_LIMIT)


def _rows(n_cols, col=0):
    return pl.BlockSpec((ROW_BLK, n_cols), lambda i, col=col: (i, col))


def _whole(shape):
    return pl.BlockSpec(shape, lambda i: (0,) * len(shape))


def _sigmoid(x):
    return jax.nn.sigmoid(x)


def _dsilu(x, s):
    return s * (1.0 + x * (1.0 - s))


def _dot(a, b):
    return jnp.dot(a, b, preferred_element_type=F32)


def _dot_nt(a, b):
    return lax.dot_general(a, b, (((1,), (1,)), ((), ())), preferred_element_type=F32)


def _dot_tn(a, b):
    return lax.dot_general(a, b, (((0,), (0,)), ((), ())), preferred_element_type=F32)


def _inproj_fwd(h, g_pre, w_in):
    t = h.shape[0]

    def body(h_ref, g_ref, w_ref, pc_ref, qkv_ref, sbg_ref, u_ref):
        x = h_ref[...]
        r = lax.rsqrt(jnp.mean(x * x, axis=-1, keepdims=True) + RMS_EPS)
        u = (x * r * g_ref[...]).astype(BF16)
        u_ref[...] = u
        pc_ref[...] = _dot(u, w_ref[:, 0:1536])
        qkv_ref[...] = _dot(u, w_ref[:, 1536:3072]).astype(BF16)
        sbg_ref[...] = _dot(u, w_ref[:, 3072:3584])

    return pl.pallas_call(
        body, name="inproj_fwd", grid=(t // ROW_BLK,),
        in_specs=[_rows(D_MODEL), _whole((1, D_MODEL)), _whole((D_MODEL, D_IN))],
        out_specs=[_rows(1536), _rows(1536), _rows(D_SB), _rows(D_MODEL)],
        out_shape=[jax.ShapeDtypeStruct((t, 1536), F32), jax.ShapeDtypeStruct((t, 1536), BF16),
                   jax.ShapeDtypeStruct((t, D_SB), F32), jax.ShapeDtypeStruct((t, D_MODEL), BF16)],
        compiler_params=_cparams("parallel"),
    )(h, g_pre, w_in)


def _prev_halo(col):
    per = ROW_BLK // HALO
    return pl.BlockSpec((HALO, D_CONV), lambda i, col=col: (jnp.maximum(i * per - 1, 0), col))


def _fill_glu(buf, i, a_ref, b_ref, ha_ref, hb_ref):
    halo = ha_ref[...] * _sigmoid(hb_ref[...])
    buf[0:HALO, :] = jnp.where(i > 0, halo, 0.0)
    buf[HALO:HALO + ROW_BLK, :] = a_ref[...] * _sigmoid(b_ref[...])


def _layer_norm_stats(cv):
    mu = jnp.mean(cv, axis=-1, keepdims=True)
    xc = cv - mu
    rstd = lax.rsqrt(jnp.mean(xc * xc, axis=-1, keepdims=True) + LN_EPS)
    return xc * rstd, rstd


def _conv_fwd(pc, conv_w, conv_b, ln_g, ln_b, w_pw2, b_pw2):
    t = pc.shape[0]

    def body(a_ref, b_ref, gate_ref, ha_ref, hb_ref, cw_ref, cb_ref, lg_ref, lb_ref, wp_ref, bp_ref,
             cout_ref, cv_ref, p_ref, sl_ref, buf):
        i = pl.program_id(0)
        _fill_glu(buf, i, a_ref, b_ref, ha_ref, hb_ref)
        acc = jnp.zeros((ROW_BLK, D_CONV), F32) + cb_ref[...]
        for j in range(CONV_WIDTH):
            acc = acc + cw_ref[j:j + 1, :] * buf[pl.ds(HALO - (CONV_WIDTH - 1) + j, ROW_BLK), :]
        cv_ref[...] = acc
        xh, _ = _layer_norm_stats(acc)
        ln = xh * lg_ref[...] + lb_ref[...]
        sl = (ln * _sigmoid(ln)).astype(BF16)
        sl_ref[...] = sl
        p = _dot(sl, wp_ref[...]) + bp_ref[...]
        p_ref[...] = p
        gate = gate_ref[...]
        cout_ref[...] = (p * (gate * _sigmoid(gate))).astype(BF16)

    vec = _whole((1, D_CONV))
    return pl.pallas_call(
        body, name="conv_fwd", grid=(t // ROW_BLK,),
        in_specs=[_rows(D_CONV, 0), _rows(D_CONV, 1), _rows(D_CONV, 2), _prev_halo(0), _prev_halo(1),
                  _whole((CONV_WIDTH, D_CONV)), vec, vec, vec, _whole((D_CONV, D_CONV)), vec],
        out_specs=[_rows(D_CONV)] * 4,
        out_shape=[jax.ShapeDtypeStruct((t, D_CONV), BF16), jax.ShapeDtypeStruct((t, D_CONV), F32),
                   jax.ShapeDtypeStruct((t, D_CONV), F32), jax.ShapeDtypeStruct((t, D_CONV), BF16)],
        scratch_shapes=[pltpu.VMEM((HALO + ROW_BLK, D_CONV), F32)],
        compiler_params=_cparams("parallel"),
    )(pc, pc, pc, pc, pc, conv_w, conv_b, ln_g, ln_b, w_pw2, b_pw2)


def _lower_triangle():
    row = lax.broadcasted_iota(jnp.int32, (ROW_BLK, ROW_BLK), 0)
    col = lax.broadcasted_iota(jnp.int32, (ROW_BLK, ROW_BLK), 1)
    return row > col


def _lower_triangle_t():
    row = lax.broadcasted_iota(jnp.int32, (ROW_BLK, ROW_BLK), 0)
    col = lax.broadcasted_iota(jnp.int32, (ROW_BLK, ROW_BLK), 1)
    return row < col


def _tri_sum(x, umat):
    return _dot(x.astype(BF16), umat)


def _log_gates(z):
    ls = -(jnp.maximum(z, 0.0) + jnp.log(1.0 + jnp.exp(-jnp.abs(z))))
    return ls, z + ls


def _head_lanes(hh):
    lane = lax.broadcasted_iota(jnp.int32, (ROW_BLK, HEAD_BLK), 1)
    return (lane >= HEAD_DIM * hh) & (lane < HEAD_DIM * (hh + 1))


def _merge_heads(acc_ref):
    out = acc_ref[HEADS_PER_BLOCK - 1]
    for hh in range(HEADS_PER_BLOCK - 1):
        out = jnp.where(_head_lanes(hh), acc_ref[hh], out)
    return out


def _qkv_specs(t):
    n_blk = D_SB // HEAD_BLK
    return [pl.BlockSpec((ROW_BLK, HEAD_BLK), lambda hp, i: (i, hp)),
            pl.BlockSpec((t, HEAD_BLK), lambda hp, i: (0, n_blk + hp)),
            pl.BlockSpec((t, HEAD_BLK), lambda hp, i: (0, 2 * n_blk + hp))]


def _carry_spec():
    return pl.BlockSpec((HEADS_PER_BLOCK, ROW_BLK, LANES), lambda hp, i: (hp, i, 0))


def _attn_fwd(qkv):
    t = qkv.shape[0]
    assert t // ROW_BLK <= LANES

    def body(q_ref, k_ref, v_ref, o_ref, c_ref, acc_ref, run_ref, qm_ref, z_ref):
        i = pl.program_id(1)
        lower = _lower_triangle()
        umat = jnp.where(lower, 1.0, 0.0).astype(BF16)
        lane = lax.broadcasted_iota(jnp.int32, (ROW_BLK, LANES), 1)
        q = q_ref[...]
        heads = range(HEADS_PER_BLOCK)
        for hh in heads:
            qm_ref[hh] = jnp.where(_head_lanes(hh), q, jnp.zeros_like(q)) * jnp.asarray(SB_SCALE, BF16)
        acc_ref[...] = jnp.zeros_like(acc_ref)
        c_ref[...] = jnp.zeros_like(c_ref)
        run_ref[...] = jnp.zeros_like(run_ref)

        def scores(jb):
            start = pl.multiple_of(jb * ROW_BLK, ROW_BLK)
            kb = k_ref[pl.ds(start, ROW_BLK), :]
            for hh in heads:
                z_ref[hh] = _dot_nt(qm_ref[hh], kb)

        def block(jb, diagonal):
            start = pl.multiple_of(jb * ROW_BLK, ROW_BLK)
            vb = v_ref[pl.ds(start, ROW_BLK), :]
            logits = []
            for hh in heads:
                ls, lb = _log_gates(z_ref[hh])
                if diagonal:
                    ls = jnp.where(lower, ls, 0.0)
                run = run_ref[hh]
                if not diagonal:
                    c_ref[hh] = jnp.where(lane == jb, run, c_ref[hh])
                logits.append(lb + jnp.concatenate([run, run], axis=1) + _tri_sum(ls, umat))
                run_ref[hh] = run + jnp.sum(ls, axis=1, keepdims=True)
            scores(jnp.maximum(jb - 1, 0))
            for hh in heads:
                a = jnp.exp(logits[hh])
                if diagonal:
                    a = jnp.where(lower, a, 0.0)
                acc_ref[hh] += _dot(a.astype(BF16), vb)

        scores(i)
        block(i, True)

        @pl.loop(0, i)
        def _(n):
            block(i - 1 - n, False)

        o_ref[...] = _merge_heads(acc_ref)

    per_head = (HEADS_PER_BLOCK, ROW_BLK, HEAD_BLK)
    return pl.pallas_call(
        body, name="attn_fwd", grid=(D_SB // HEAD_BLK, t // ROW_BLK),
        in_specs=_qkv_specs(t),
        out_specs=[pl.BlockSpec((ROW_BLK, HEAD_BLK), lambda hp, i: (i, hp)), _carry_spec()],
        out_shape=[jax.ShapeDtypeStruct((t, D_SB), F32),
                   jax.ShapeDtypeStruct((D_SB // HEAD_DIM, t, LANES), F32)],
        scratch_shapes=[pltpu.VMEM(per_head, F32), pltpu.VMEM((HEADS_PER_BLOCK, ROW_BLK, LANES), F32),
                        pltpu.VMEM(per_head, BF16), pltpu.VMEM((HEADS_PER_BLOCK, ROW_BLK, ROW_BLK), F32)],
        compiler_params=_cparams("arbitrary", "arbitrary"),
    )(qkv, qkv, qkv)


def _outproj_fwd(h, cout, sraw, sbg, w_out, g_post):
    t = h.shape[0]

    def body(h_ref, c_ref, s_ref, g_ref, w_ref, gp_ref, hn_ref, mixed_ref, mix_ref):
        gate = g_ref[...]
        mix_ref[:, 0:D_CONV] = c_ref[...]
        mix_ref[:, D_CONV:] = (s_ref[...] * (gate * _sigmoid(gate))).astype(BF16)
        mixed = _dot(mix_ref[...], w_ref[...])
        mixed_ref[...] = mixed
        r = lax.rsqrt(jnp.mean(mixed * mixed, axis=-1, keepdims=True) + RMS_EPS)
        hn_ref[...] = h_ref[...] + mixed * r * gp_ref[...]

    return pl.pallas_call(
        body, name="outproj_fwd", grid=(t // ROW_BLK,),
        in_specs=[_rows(D_MODEL), _rows(D_CONV), _rows(D_SB), _rows(D_SB), _whole((D_MODEL, D_MODEL)),
                  _whole((1, D_MODEL))],
        out_specs=[_rows(D_MODEL)] * 3,
        out_shape=[jax.ShapeDtypeStruct((t, D_MODEL), F32), jax.ShapeDtypeStruct((t, D_MODEL), F32),
                   jax.ShapeDtypeStruct((t, D_MODEL), BF16)],
        compiler_params=_cparams("parallel"),
    )(h, cout, sraw, sbg, w_out, g_post)


def _loss_and_grad(h, target, seq):
    t = h.shape[0]

    def body(h_ref, t_ref, loss_ref, dh_ref):
        i = pl.program_id(0)

        @pl.when(i == 0)
        def _():
            loss_ref[...] = jnp.zeros_like(loss_ref)

        row = i * ROW_BLK + lax.broadcasted_iota(jnp.int32, (ROW_BLK, D_MODEL), 0)
        real = (row >= N_META) & (row < N_META + seq)
        diff = jnp.where(real, h_ref[...] - t_ref[...], 0.0)
        sq = jnp.sum(jnp.sum(diff * diff, axis=1, keepdims=True), axis=0, keepdims=True)
        loss_ref[...] += (0.5 / D_MODEL) * sq
        dh_ref[...] = diff * (1.0 / D_MODEL)

    return pl.pallas_call(
        body, name="loss", grid=(t // ROW_BLK,),
        in_specs=[_rows(D_MODEL), _rows(D_MODEL)],
        out_specs=[_whole((1, 1)), _rows(D_MODEL)],
        out_shape=[jax.ShapeDtypeStruct((1, 1), F32), jax.ShapeDtypeStruct((t, D_MODEL), F32)],
        compiler_params=_cparams("arbitrary"),
    )(h, target)


def _outproj_bwd(dh, mixed, g_post, sraw, sbg, w_out_t):
    t = dh.shape[0]

    def body(dh_ref, mixed_ref, gp_ref, s_ref, g_ref, wt_ref, dc_ref, ds_ref, dg_ref, dmb_ref, dgp_ref):
        @pl.when(pl.program_id(0) == 0)
        def _():
            dgp_ref[...] = jnp.zeros_like(dgp_ref)

        mixed = mixed_ref[...]
        r = lax.rsqrt(jnp.mean(mixed * mixed, axis=-1, keepdims=True) + RMS_EPS)
        nh = mixed * r
        dy = dh_ref[...]
        dgp_ref[...] += jnp.sum(dy * nh, axis=0, keepdims=True)
        dn = dy * gp_ref[...]
        dmixed = (r * (dn - nh * jnp.mean(dn * nh, axis=-1, keepdims=True))).astype(BF16)
        dmb_ref[...] = dmixed
        dmix = _dot(dmixed, wt_ref[...])
        dc_ref[...] = dmix[:, 0:D_CONV]
        dsg = dmix[:, D_CONV:]
        gate = g_ref[...]
        sg = _sigmoid(gate)
        ds_ref[...] = dsg * (gate * sg)
        dg_ref[...] = dsg * s_ref[...] * _dsilu(gate, sg)

    return pl.pallas_call(
        body, name="outproj_bwd", grid=(t // ROW_BLK,),
        in_specs=[_rows(D_MODEL), _rows(D_MODEL), _whole((1, D_MODEL)), _rows(D_SB), _rows(D_SB),
                  _whole((D_MODEL, D_MODEL))],
        out_specs=[_rows(D_CONV), _rows(D_SB), _rows(D_SB), _rows(D_MODEL), _whole((1, D_MODEL))],
        out_shape=[jax.ShapeDtypeStruct((t, D_CONV), F32), jax.ShapeDtypeStruct((t, D_SB), F32),
                   jax.ShapeDtypeStruct((t, D_SB), F32), jax.ShapeDtypeStruct((t, D_MODEL), BF16),
                   jax.ShapeDtypeStruct((1, D_MODEL), F32)],
        compiler_params=_cparams("arbitrary"),
    )(dh, mixed, g_post, sraw, sbg, w_out_t)


def _attn_bwd(qkv, carries, do):
    t = qkv.shape[0]

    def body(q_ref, k_ref, v_ref, c_ref, do_ref, dq_ref, dk_ref, dv_ref, acc_ref, seen_ref, qm_ref, dom_ref, z_ref,
             da_ref, dz_ref, a_ref):
        i = pl.program_id(1)

        @pl.when(i == 0)
        def _():
            dk_ref[...] = jnp.zeros_like(dk_ref)
            dv_ref[...] = jnp.zeros_like(dv_ref)

        lower = _lower_triangle()
        umat = jnp.where(lower, 1.0, 0.0).astype(BF16)
        umat_t = jnp.where(_lower_triangle_t(), 1.0, 0.0).astype(BF16)
        lane = lax.broadcasted_iota(jnp.int32, (ROW_BLK, LANES), 1)
        q = q_ref[...]
        dof = do_ref[...]
        heads = range(HEADS_PER_BLOCK)
        for hh in heads:
            qm_ref[hh] = jnp.where(_head_lanes(hh), q, jnp.zeros_like(q)) * jnp.asarray(SB_SCALE, BF16)
            dom_ref[hh] = jnp.where(_head_lanes(hh), dof, 0.0).astype(BF16)
        acc_ref[...] = jnp.zeros_like(acc_ref)
        seen_ref[...] = jnp.zeros_like(seen_ref)

        def scores(jb):
            start = pl.multiple_of(jb * ROW_BLK, ROW_BLK)
            kb = k_ref[pl.ds(start, ROW_BLK), :]
            for hh in heads:
                z_ref[hh] = _dot_nt(qm_ref[hh], kb)

        def value_grads(jb):
            start = pl.multiple_of(jb * ROW_BLK, ROW_BLK)
            vb = v_ref[pl.ds(start, ROW_BLK), :]
            for hh in heads:
                da_ref[hh] = _dot_nt(dom_ref[hh], vb)

        def products(jb, hh):
            start = pl.multiple_of(jb * ROW_BLK, ROW_BLK)
            dzb = dz_ref[hh]
            acc_ref[hh] += _dot(dzb, k_ref[pl.ds(start, ROW_BLK), :])
            dk_ref[pl.ds(start, ROW_BLK), :] += _dot_tn(dzb, qm_ref[hh])
            dv_ref[pl.ds(start, ROW_BLK), :] += _dot_tn(a_ref[hh], dom_ref[hh])

        def block(jb, diagonal):
            before = jnp.maximum(jb - 1, 0)
            lbs, logits = [], []
            for hh in heads:
                products(before, hh)
                ls, lb = _log_gates(z_ref[hh])
                if diagonal:
                    ls = jnp.where(lower, ls, 0.0)
                    logits.append(lb + _tri_sum(ls, umat))
                else:
                    right = jnp.sum(jnp.where(lane == jb, c_ref[hh], 0.0), axis=1, keepdims=True)
                    logits.append(lb + right + _tri_sum(ls, umat))
                lbs.append(lb)
            if not diagonal:
                scores(jb + 1)
            gs, befores = [], []
            for hh in heads:
                a = jnp.exp(logits[hh])
                if diagonal:
                    a = jnp.where(lower, a, 0.0)
                g = da_ref[hh] * a
                seen = seen_ref[hh]
                befores.append(jnp.concatenate([seen, seen], axis=1) + _tri_sum(g, umat_t))
                seen_ref[hh] = seen + jnp.sum(g, axis=1, keepdims=True)
                a_ref[hh] = a.astype(BF16)
                gs.append(g)
            if not diagonal:
                value_grads(jb + 1)
            for hh in heads:
                dz = gs[hh] - jnp.exp(lbs[hh]) * (gs[hh] + befores[hh])
                if diagonal:
                    dz = jnp.where(lower, dz, 0.0)
                dz_ref[hh] = dz.astype(BF16)

        dz_ref[...] = jnp.zeros_like(dz_ref)
        a_ref[...] = jnp.zeros_like(a_ref)
        scores(0)
        value_grads(0)

        @pl.loop(0, i)
        def _(jb):
            block(jb, False)

        block(i, True)
        for hh in heads:
            products(i, hh)
        dq_ref[...] = _merge_heads(acc_ref) * SB_SCALE

    blk = pl.BlockSpec((ROW_BLK, HEAD_BLK), lambda hp, i: (i, hp))
    full = pl.BlockSpec((t, HEAD_BLK), lambda hp, i: (0, hp))
    per_head = (HEADS_PER_BLOCK, ROW_BLK, HEAD_BLK)
    return pl.pallas_call(
        body, name="attn_bwd", grid=(D_SB // HEAD_BLK, t // ROW_BLK),
        in_specs=_qkv_specs(t) + [_carry_spec(), blk],
        out_specs=[blk, full, full],
        out_shape=[jax.ShapeDtypeStruct((t, D_SB), F32)] * 3,
        scratch_shapes=[pltpu.VMEM(per_head, F32), pltpu.VMEM((HEADS_PER_BLOCK, ROW_BLK, LANES), F32),
                        pltpu.VMEM(per_head, BF16), pltpu.VMEM(per_head, BF16)]
                       + [pltpu.VMEM((HEADS_PER_BLOCK, ROW_BLK, ROW_BLK), dtype) for dtype in (F32, F32, BF16, BF16)],
        compiler_params=_cparams("arbitrary", "arbitrary"),
    )(qkv, qkv, qkv, carries, do)


def _conv_bwd_rows(dcout, pc, cv, p, ln_g, ln_b, w_pw2_t):
    t = dcout.shape[0]

    def body(dc_ref, gate_ref, cv_ref, p_ref, lg_ref, lb_ref, wt_ref, dcv_ref, dgate_ref, dpb_ref, vec_ref):
        @pl.when(pl.program_id(0) == 0)
        def _():
            vec_ref[...] = jnp.zeros_like(vec_ref)

        dc = dc_ref[...]
        gate = gate_ref[...]
        sg = _sigmoid(gate)
        dp = dc * (gate * sg)
        dgate_ref[...] = dc * p_ref[...] * _dsilu(gate, sg)
        dpb = dp.astype(BF16)
        dpb_ref[...] = dpb
        xh, rstd = _layer_norm_stats(cv_ref[...])
        ln = xh * lg_ref[...] + lb_ref[...]
        s2 = _sigmoid(ln)
        dln = _dot(dpb, wt_ref[...]) * _dsilu(ln, s2)
        dxh = dln * lg_ref[...]
        dcv = rstd * (dxh - jnp.mean(dxh, axis=-1, keepdims=True)
                      - xh * jnp.mean(dxh * xh, axis=-1, keepdims=True))
        dcv_ref[...] = dcv
        vec_ref[0:1, :] += jnp.sum(dp, axis=0, keepdims=True)
        vec_ref[1:2, :] += jnp.sum(dln * xh, axis=0, keepdims=True)
        vec_ref[2:3, :] += jnp.sum(dln, axis=0, keepdims=True)
        vec_ref[3:4, :] += jnp.sum(dcv, axis=0, keepdims=True)

    vec = _whole((1, D_CONV))
    return pl.pallas_call(
        body, name="conv_bwd_rows", grid=(t // ROW_BLK,),
        in_specs=[_rows(D_CONV), _rows(D_CONV, 2), _rows(D_CONV), _rows(D_CONV), vec, vec,
                  _whole((D_CONV, D_CONV))],
        out_specs=[_rows(D_CONV), _rows(D_CONV), _rows(D_CONV), _whole((8, D_CONV))],
        out_shape=[jax.ShapeDtypeStruct((t, D_CONV), F32), jax.ShapeDtypeStruct((t, D_CONV), F32),
                   jax.ShapeDtypeStruct((t, D_CONV), BF16), jax.ShapeDtypeStruct((8, D_CONV), F32)],
        compiler_params=_cparams("arbitrary"),
    )(dcout, pc, cv, p, ln_g, ln_b, w_pw2_t)


def _conv_bwd_taps(dcv, pc, conv_w):
    t = dcv.shape[0]
    n_halo = t // HALO
    per = ROW_BLK // HALO

    def body(d_ref, dn_ref, a_ref, b_ref, ha_ref, hb_ref, cw_ref, da_ref, db_ref, dw_ref, cbuf, dbuf):
        i = pl.program_id(0)

        @pl.when(i == 0)
        def _():
            dw_ref[...] = jnp.zeros_like(dw_ref)

        _fill_glu(cbuf, i, a_ref, b_ref, ha_ref, hb_ref)
        dcv = d_ref[...]
        dbuf[0:ROW_BLK, :] = dcv
        dbuf[ROW_BLK:ROW_BLK + HALO, :] = jnp.where(i < pl.num_programs(0) - 1, dn_ref[...], 0.0)
        acc = jnp.zeros((ROW_BLK, D_CONV), F32)
        for j in range(CONV_WIDTH):
            acc = acc + cw_ref[j:j + 1, :] * dbuf[pl.ds(CONV_WIDTH - 1 - j, ROW_BLK), :]
            seen = cbuf[pl.ds(HALO - (CONV_WIDTH - 1) + j, ROW_BLK), :]
            dw_ref[j:j + 1, :] += jnp.sum(dcv * seen, axis=0, keepdims=True)
        a = a_ref[...]
        sb = _sigmoid(b_ref[...])
        da_ref[...] = acc * sb
        db_ref[...] = acc * a * sb * (1.0 - sb)

    return pl.pallas_call(
        body, name="conv_bwd_taps", grid=(t // ROW_BLK,),
        in_specs=[_rows(D_CONV),
                  pl.BlockSpec((HALO, D_CONV), lambda i: (jnp.minimum((i + 1) * per, n_halo - 1), 0)),
                  _rows(D_CONV, 0), _rows(D_CONV, 1), _prev_halo(0), _prev_halo(1),
                  _whole((CONV_WIDTH, D_CONV))],
        out_specs=[_rows(D_CONV), _rows(D_CONV), _whole((32, D_CONV))],
        out_shape=[jax.ShapeDtypeStruct((t, D_CONV), F32), jax.ShapeDtypeStruct((t, D_CONV), F32),
                   jax.ShapeDtypeStruct((32, D_CONV), F32)],
        scratch_shapes=[pltpu.VMEM((HALO + ROW_BLK, D_CONV), F32), pltpu.VMEM((ROW_BLK + HALO, D_CONV), F32)],
        compiler_params=_cparams("arbitrary"),
    )(dcv, dcv, pc, pc, pc, pc, conv_w)


def _inproj_bwd(dh_out, h, g_pre, pieces, w_in_t):
    t = h.shape[0]

    def body(dh_ref, h_ref, g_ref, *rest):
        piece_refs, (wt_ref, dhin_ref, dproj_ref, dg_ref) = rest[:7], rest[7:]

        @pl.when(pl.program_id(0) == 0)
        def _():
            dg_ref[...] = jnp.zeros_like(dg_ref)

        for k, ref in enumerate(piece_refs):
            dproj_ref[:, 512 * k:512 * (k + 1)] = ref[...].astype(BF16)
        du = _dot(dproj_ref[...], wt_ref[...])
        x = h_ref[...]
        r = lax.rsqrt(jnp.mean(x * x, axis=-1, keepdims=True) + RMS_EPS)
        xh = x * r
        dg_ref[...] += jnp.sum(du * xh, axis=0, keepdims=True)
        dxh = du * g_ref[...]
        dhin_ref[...] = dh_ref[...] + r * (dxh - xh * jnp.mean(dxh * xh, axis=-1, keepdims=True))

    return pl.pallas_call(
        body, name="inproj_bwd", grid=(t // ROW_BLK,),
        in_specs=[_rows(D_MODEL), _rows(D_MODEL), _whole((1, D_MODEL))] + [_rows(512)] * 7
                 + [_whole((D_IN, D_MODEL))],
        out_specs=[_rows(D_MODEL), _rows(D_IN), _whole((1, D_MODEL))],
        out_shape=[jax.ShapeDtypeStruct((t, D_MODEL), F32), jax.ShapeDtypeStruct((t, D_IN), BF16),
                   jax.ShapeDtypeStruct((1, D_MODEL), F32)],
        compiler_params=_cparams("arbitrary"),
    )(dh_out, h, g_pre, *pieces, w_in_t)


def _weight_grad(xb, dyb, name):
    t, k = xb.shape
    n = dyb.shape[1]
    tn = n

    def body(x_ref, dy_ref, o_ref):
        @pl.when(pl.program_id(1) == 0)
        def _():
            o_ref[...] = jnp.zeros_like(o_ref)

        o_ref[...] += _dot_tn(x_ref[...], dy_ref[...])

    return pl.pallas_call(
        body, name=name, grid=(n // tn, t // ROW_BLK),
        in_specs=[pl.BlockSpec((ROW_BLK, k), lambda j, i: (i, 0)), pl.BlockSpec((ROW_BLK, tn), lambda j, i: (i, j))],
        out_specs=pl.BlockSpec((k, tn), lambda j, i: (0, j)),
        out_shape=jax.ShapeDtypeStruct((k, n), F32),
        compiler_params=_cparams("parallel", "arbitrary"),
    )(xb, dyb)


def _position():
    return lax.axis_index("x"), lax.axis_index("y"), lax.axis_index("c")


def _comm_call(body, name, ins, out_shapes):
    n = len(ins)
    hbm = pl.BlockSpec(memory_space=pltpu.HBM)
    return pl.pallas_call(
        functools.partial(body, n), name=name, in_specs=[hbm] * n, out_specs=[hbm] * n, out_shape=out_shapes,
        scratch_shapes=[pltpu.SemaphoreType.DMA((n, N_DEV - 1)), pltpu.SemaphoreType.DMA((n, N_DEV - 1)),
                        pltpu.SemaphoreType.DMA((n,))],
    )(*ins)


def _all_gather(blocks, name):
    def body(n, *refs):
        x_refs, out_refs, (send_sems, recv_sems, local_sems) = refs[:n], refs[n:2 * n], refs[2 * n:]
        x, y, c = _position()
        me, sibling = (x, y, c), (x, y, 1 - c)
        chips = [(1 - x, y), (x, 1 - y), (1 - x, 1 - y)]

        def slot(a, px, py, pc):
            return out_refs[a].at[4 * px + 2 * py + pc]

        def copy(a, k, origin, to, own=False):
            return pltpu.make_async_remote_copy(
                src_ref=x_refs[a] if own else slot(a, *origin), dst_ref=slot(a, *origin),
                send_sem=send_sems.at[a, k], recv_sem=recv_sems.at[a, k], device_id=to, device_id_type=MESH)

        arrays = range(n)
        mine = [pltpu.make_async_copy(x_refs[a], slot(a, *me), local_sems.at[a]) for a in arrays]
        first = [copy(a, 1 + j, me, (*chip, c), own=True) for j, chip in enumerate(chips) for a in arrays]
        first += [copy(a, 0, me, sibling, own=True) for a in arrays]
        for cp in mine + first:
            cp.start()
        passed = []
        for j, chip in enumerate(chips):
            for a in arrays:
                copy(a, 1 + j, (*chip, c), me).wait_recv()
                passed.append(copy(a, 4 + j, (*chip, c), sibling))
                passed[-1].start()
        for a in arrays:
            copy(a, 0, sibling, me).wait_recv()
            for j, chip in enumerate(chips):
                copy(a, 4 + j, (*chip, 1 - c), me).wait_recv()
        for cp in first + passed:
            cp.wait_send()
        for cp in mine:
            cp.wait()

    return _comm_call(body, name, blocks, [jax.ShapeDtypeStruct((N_DEV,) + b.shape, b.dtype) for b in blocks])


def _exchange_copies(g_refs, land_refs, sems, gather):
    x, y, c = _position()
    me = 4 * x + 2 * y + c
    out = []
    for g_ref, land_ref, (send_sem, recv_sem, local_sem) in zip(g_refs, land_refs, sems):
        def mine(slot, g_ref=g_ref):
            return g_ref if gather else g_ref.at[slot]

        def remote(src, dst, dev):
            return pltpu.make_async_remote_copy(src_ref=src, dst_ref=dst, send_sem=send_sem, recv_sem=recv_sem,
                                                device_id=dev, device_id_type=MESH)

        sends = []
        for k in range(1, N_DEV):
            px = 1 - x if k & 4 else x
            py = 1 - y if k & 2 else y
            pc = 1 - c if k & 1 else c
            sends.append(remote(mine(4 * px + 2 * py + pc), land_ref.at[me], (px, py, pc)))
        seven = land_ref.at[pl.ds(0, N_DEV - 1)]
        out.append((pltpu.make_async_copy(mine(me), land_ref.at[me], local_sem), sends, remote(seven, seven, (x, y, c))))
    return out


_HBM = pl.BlockSpec(memory_space=pltpu.HBM)
_SEM = pl.BlockSpec(memory_space=pltpu.SEMAPHORE)
_ORDERED = pltpu.CompilerParams(has_side_effects=pltpu.SideEffectType.DATAFLOW_SIDE_EFFECTING)
SEMS_PER_ARRAY = 3


def _exchange_start(arrays, after, name, gather):
    n = len(arrays)
    n_sems = SEMS_PER_ARRAY * n

    def body(*refs):
        g_refs, land_refs, sems, token = refs[:n], refs[n:2 * n], refs[2 * n + 1:2 * n + 1 + n_sems], refs[-1]
        sems = [sems[SEMS_PER_ARRAY * a:SEMS_PER_ARRAY * (a + 1)] for a in range(n)]
        for local, sends, _ in _exchange_copies(g_refs, land_refs, sems, gather):
            local.start()
            for cp in sends:
                cp.start()
        token[...] = jnp.zeros_like(token)

    buffers = list(arrays) + [lax.empty((N_DEV,) + g.shape if gather else g.shape, g.dtype) for g in arrays]
    outs = pl.pallas_call(
        body, name=name, in_specs=[_HBM] * (2 * n) + [pl.BlockSpec(memory_space=pl.ANY)],
        out_specs=[_SEM] * n_sems + [_HBM] * (2 * n) + [pl.BlockSpec(memory_space=pltpu.VMEM)],
        out_shape=[pltpu.SemaphoreType.DMA(())] * n_sems + [pltpu.HBM(b.shape, b.dtype) for b in buffers]
                  + [jax.ShapeDtypeStruct((8, LANES), F32)],
        input_output_aliases={a: n_sems + a for a in range(2 * n)}, compiler_params=_ORDERED,
    )(*[pltpu.with_memory_space_constraint(b, pltpu.HBM) for b in buffers], after)
    return outs[:n_sems], outs[n_sems:n_sems + n], outs[n_sems + n:n_sems + 2 * n], outs[-1]


def _exchange_wait(sems, arrays, landings, after, name, gather):
    n = len(arrays)
    n_sems = SEMS_PER_ARRAY * n

    def body(*refs):
        g_refs, land_refs, sems = refs[:n], refs[n:2 * n], refs[2 * n:2 * n + n_sems]
        sems = [sems[SEMS_PER_ARRAY * a:SEMS_PER_ARRAY * (a + 1)] for a in range(n)]
        for local, _, all_seven in _exchange_copies(g_refs, land_refs, sems, gather):
            all_seven.wait_recv()
            all_seven.wait_send()
            local.wait()

    buffers = list(arrays) + list(landings)
    outs = pl.pallas_call(
        body, name=name, in_specs=[_HBM] * (2 * n) + [_SEM] * n_sems + [pl.BlockSpec(memory_space=pl.ANY)],
        out_specs=[_HBM] * (2 * n), out_shape=[pltpu.HBM(b.shape, b.dtype) for b in buffers],
        input_output_aliases={a: a for a in range(2 * n)}, compiler_params=_ORDERED,
    )(*buffers, *sems, after)
    return outs[n:]


def _block_rows(r, row_bytes, budget=1 << 20):
    cap = max(8, budget // row_bytes)
    return max(d for d in range(8, min(r, cap) + 1, 8) if r % d == 0)


def _sum_adamw(parts, w, m, v, name):
    n_parts, r, c = parts.shape
    br = _block_rows(r, 4 * c)

    def body(p_ref, w_ref, m_ref, v_ref, g_out, d_out, m_out, v_out):
        g = p_ref[0].astype(F32)
        for s in range(1, n_parts):
            g = g + p_ref[s].astype(F32)
        m_new = ADAM_B1 * m_ref[...] + (1.0 - ADAM_B1) * g
        v_new = ADAM_B2 * v_ref[...] + (1.0 - ADAM_B2) * (g * g)
        m_hat = m_new / (1.0 - ADAM_B1 ** ADAM_STEP)
        v_hat = v_new / (1.0 - ADAM_B2 ** ADAM_STEP)
        g_out[...] = g
        d_out[...] = -ADAM_LR * (m_hat / (jnp.sqrt(v_hat) + ADAM_EPS) + ADAM_WD * w_ref[...])
        m_out[...] = m_new
        v_out[...] = v_new

    row = pl.BlockSpec((br, c), lambda i: (i, 0))
    return pl.pallas_call(
        body, name=name, grid=(r // br,),
        in_specs=[pl.BlockSpec((n_parts, br, c), lambda i: (0, i, 0)), row, row, row],
        out_specs=[row] * 4, out_shape=[jax.ShapeDtypeStruct((r, c), F32)] * 4,
        compiler_params=_cparams("parallel"),
    )(parts, w, m, v)


def _sum_parts(parts, name):
    n_parts, r, c = parts.shape

    def body(p_ref, o_ref):
        g = p_ref[0]
        for s in range(1, n_parts):
            g = g + p_ref[s]
        o_ref[...] = g

    return pl.pallas_call(
        body, name=name, in_specs=[pl.BlockSpec(memory_space=pltpu.VMEM)],
        out_specs=pl.BlockSpec(memory_space=pltpu.VMEM), out_shape=jax.ShapeDtypeStruct((r, c), F32),
    )(parts)


def _pack(arrays):
    flat = jnp.concatenate([a.reshape(-1) for a in arrays])
    pad = -flat.shape[0] % (8 * LANES)
    if pad:
        flat = jnp.pad(flat, (0, pad))
    return flat.reshape(-1, LANES)


def _unpack(buf, shapes):
    flat = buf.reshape(-1)
    out, at = [], 0
    for shape in shapes:
        size = 1
        for d in shape:
            size *= d
        out.append(lax.slice_in_dim(flat, at, at + size).reshape(shape))
        at += size
    return out


def _local_step(x, target, meta, pre_g, post_g, conv_w, conv_b, ln_g, ln_b, b_pw2, weights, ship):
    depth = pre_g.shape[0]
    seq = x.shape[0]
    t = -(-(N_META + seq) // ROW_BLK) * ROW_BLK
    tail = t - N_META - seq
    h = jnp.concatenate([meta, x, jnp.zeros((tail, D_MODEL), F32)], axis=0)
    target = jnp.pad(target, ((N_META, tail), (0, 0)))
    row = lambda a, l: a[l][None, :]

    saved = []
    for l in range(depth):
        w_in, w_pw2, w_out = weights(l, h)
        pc, qkv, sbg, u = _inproj_fwd(h, row(pre_g, l), w_in)
        cout, cv, p, sl = _conv_fwd(pc, conv_w[l], row(conv_b, l), row(ln_g, l), row(ln_b, l), w_pw2, row(b_pw2, l))
        sraw, carries = _attn_fwd(qkv)
        h_new, mixed, mix = _outproj_fwd(h, cout, sraw, sbg, w_out, row(post_g, l))
        saved.append((h, pc, qkv, sbg, u, cv, p, sl, sraw, carries, mixed, mix, w_in.T, w_pw2.T, w_out.T))
        h = h_new

    loss, dh = _loss_and_grad(h, target, seq)

    grads = {k: [None] * depth for k in ("pre_g", "post_g", "conv_w", "conv_b", "ln_g", "ln_b", "b_pw2")}
    token = jnp.zeros((8, LANES), F32)
    for l in reversed(range(depth)):
        h_in, pc, qkv, sbg, u, cv, p, sl, sraw, carries, mixed, mix, w_in_t, w_pw2_t, w_out_t = saved[l]
        dcout, dsraw, dsbg, dmixed, dg_post = _outproj_bwd(dh, mixed, row(post_g, l) + token[:1, :1], sraw, sbg, w_out_t)
        dq, dk, dv = _attn_bwd(qkv, carries, dsraw)
        dcv, dgate, dpb, vecs = _conv_bwd_rows(dcout, pc, cv, p, row(ln_g, l), row(ln_b, l), w_pw2_t)
        da, db, dconv_w = _conv_bwd_taps(dcv, pc, conv_w[l])
        dh, dproj, dg_pre = _inproj_bwd(dh, h_in, row(pre_g, l), (da, db, dgate, dq, dk, dv, dsbg), w_in_t)
        token = ship(l, dh, _weight_grad(u, dproj, "w_in_grad"), _weight_grad(sl, dpb, "w_pw2_grad"),
                     _weight_grad(mix, dmixed, "w_out_grad"))
        grads["pre_g"][l] = dg_pre[0]
        grads["post_g"][l] = dg_post[0]
        grads["b_pw2"][l], grads["ln_g"][l], grads["ln_b"][l], grads["conv_b"][l] = vecs[0], vecs[1], vecs[2], vecs[3]
        grads["conv_w"][l] = dconv_w[:CONV_WIDTH]

    grads = {k: jnp.stack(v) for k, v in grads.items()}
    grads["meta"] = dh[:N_META]
    return loss[0, 0], dh[N_META:N_META + seq], grads


def _shard_major(full, axis):
    shape = full.shape
    split = full.reshape(shape[:axis] + (N_DEV, shape[axis] // N_DEV) + shape[axis + 1:])
    return jnp.moveaxis(split, axis, 0)


def _whole_from_shards(shards, axis):
    moved = jnp.moveaxis(shards, 0, axis)
    shape = moved.shape
    return moved.reshape(shape[:axis] + (shape[axis] * shape[axis + 1],) + shape[axis + 2:])


def kernel(x, meta_tokens, pre_norm_g, post_norm_g, w_in, conv_w, conv_b, conv_ln_g, conv_ln_b, w_pw2, b_pw2, w_out, loss_target, m_meta_tokens, m_pre_norm_g, m_post_norm_g, m_w_in, m_conv_w, m_conv_b, m_conv_ln_g, m_conv_ln_b, m_w_pw2, m_b_pw2, m_w_out, v_meta_tokens, v_pre_norm_g, v_post_norm_g, v_w_in, v_conv_w, v_conv_b, v_conv_ln_g, v_conv_ln_b, v_w_pw2, v_b_pw2, v_w_out):
    me = 4 * lax.axis_index("x") + 2 * lax.axis_index("y") + lax.axis_index("c")

    depth = w_in.shape[0]
    big = [w.astype(BF16) for w in (w_in, w_pw2, w_out)]
    *first, conv_w_s, meta_s = _all_gather([w[0] for w in big] + [conv_w, meta_tokens], "gather_first_layer")
    *gathering, token = _exchange_start([w[1:] for w in big], meta_s, "gather_start", gather=True)
    conv_w_full = _whole_from_shards(conv_w_s, 2)
    meta_full = _whole_from_shards(meta_s, 1)
    shard_axis = (1, 0, 0)
    later = []

    def weights(l, h):
        if l == 0:
            return [_whole_from_shards(s, axis) for s, axis in zip(first, shard_axis)]
        if not later:
            later.extend(_exchange_wait(*gathering, h, "gather_wait", gather=True))
        return [_whole_from_shards(s[:, l - 1], axis) for s, axis in zip(later, shard_axis)]

    in_flight = [None] * depth

    def ship(l, dh, dw_in, dw_pw2, dw_out):
        slabs = [_shard_major(dw, axis).astype(BF16) for dw, axis in zip((dw_in, dw_pw2, dw_out), shard_axis)]
        *in_flight[l], token = _exchange_start(slabs, dh, f"exchange_start_{l}", gather=False)
        return token

    loss, dx, grads = _local_step(x[0], loss_target[0], meta_full, pre_norm_g + token[:1, :1], post_norm_g, conv_w_full,
                                  conv_b, conv_ln_g, conv_ln_b, b_pw2, weights, ship)
    loss = lax.psum(loss, ("x", "y", "c"))
    landed = [_exchange_wait(*in_flight[l], dx, f"exchange_wait_{l}", gather=False) for l in range(depth)]

    def update(a, w, m, v, name):
        outs = [_sum_adamw(landed[l][a], w[l], m[l], v[l], name) for l in range(depth)]
        return [jnp.stack(o) for o in zip(*outs)]

    g_w_in, d_w_in, nm_w_in, nv_w_in = update(0, w_in, m_w_in, v_w_in, "adamw_w_in")
    g_w_pw2, d_w_pw2, nm_w_pw2, nv_w_pw2 = update(1, w_pw2, m_w_pw2, v_w_pw2, "adamw_w_pw2")
    g_w_out, d_w_out, nm_w_out, nv_w_out = update(2, w_out, m_w_out, v_w_out, "adamw_w_out")

    small_names = ("pre_g", "post_g", "conv_b", "ln_g", "ln_b", "b_pw2", "conv_w", "meta")
    small_full = [grads[k] for k in small_names]
    gathered, = _all_gather([_pack(small_full)], "gather_small_grads")
    summed = _sum_parts(gathered, "sum_small_grads")
    g_small = dict(zip(small_names, _unpack(summed, [a.shape for a in small_full])))
    g_small["conv_w"] = lax.dynamic_slice_in_dim(g_small["conv_w"], me * conv_w.shape[2], conv_w.shape[2], axis=2)
    g_small["meta"] = lax.dynamic_slice_in_dim(g_small["meta"], me * meta_tokens.shape[1], meta_tokens.shape[1], axis=1)
    small_w = dict(zip(small_names, (pre_norm_g, post_norm_g, conv_b, conv_ln_g, conv_ln_b, b_pw2, conv_w, meta_tokens)))
    small_m = (m_pre_norm_g, m_post_norm_g, m_conv_b, m_conv_ln_g, m_conv_ln_b, m_b_pw2, m_conv_w, m_meta_tokens)
    small_v = (v_pre_norm_g, v_post_norm_g, v_conv_b, v_conv_ln_g, v_conv_ln_b, v_b_pw2, v_conv_w, v_meta_tokens)
    small_shapes = [small_w[k].shape for k in small_names]
    outs = _sum_adamw(_pack([g_small[k] for k in small_names])[None], _pack([small_w[k] for k in small_names]),
                      _pack(small_m), _pack(small_v), "adamw_small_weights")
    g_s, d_s, nm_s, nv_s = [dict(zip(small_names, _unpack(o, small_shapes))) for o in outs]

    def ordered(s, w_in_, w_pw2_, w_out_):
        return (s["meta"], s["pre_g"], s["post_g"], w_in_, s["conv_w"], s["conv_b"], s["ln_g"], s["ln_b"], w_pw2_,
                s["b_pw2"], w_out_)

    return (loss, dx[None], *ordered(g_s, g_w_in, g_w_pw2, g_w_out), *ordered(d_s, d_w_in, d_w_pw2, d_w_out),
            *ordered(nm_s, nm_w_in, nm_w_pw2, nm_w_out), *ordered(nv_s, nv_w_in, nv_w_pw2, nv_w_out))
```

```python
import functools

import jax
import jax.numpy as jnp
from jax import lax
from jax.experimental import pallas as pl
from jax.experimental.pallas import tpu as pltpu

F32 = jnp.float32
BF16 = jnp.bfloat16

D_MODEL = 1024
D_CONV = 512
D_SB = 512
HEAD_DIM = 64
HEADS_PER_BLOCK = 4
HEAD_BLK = HEADS_PER_BLOCK * HEAD_DIM
CONV_WIDTH = 31
N_META = 16
D_IN = 3 * D_CONV + 4 * D_SB
RMS_EPS = 1e-6
LN_EPS = 1e-5
SB_SCALE = HEAD_DIM ** -0.5

ADAM_LR = 0.001
ADAM_B1 = 0.9
ADAM_B2 = 0.999
ADAM_EPS = 1e-08
ADAM_WD = 0.01
ADAM_STEP = 10

N_DEV = 8
LANES = 128
ROW_BLK = 256
HALO = 32
VMEM_LIMIT = 56 * 1024 * 1024
MESH = pl.DeviceIdType.MESH


def _cparams(*sem):
    return pltpu.CompilerParams(dimension_semantics=sem, vmem_limit_bytes=VMEM_LIMIT)


def _rows(n_cols, col=0):
    return pl.BlockSpec((ROW_BLK, n_cols), lambda i, col=col: (i, col))


def _whole(shape):
    return pl.BlockSpec(shape, lambda i: (0,) * len(shape))


def _sigmoid(x):
    return jax.nn.sigmoid(x)


def _dsilu(x, s):
    return s * (1.0 + x * (1.0 - s))


def _dot(a, b):
    return jnp.dot(a, b, preferred_element_type=F32)


def _dot_nt(a, b):
    return lax.dot_general(a, b, (((1,), (1,)), ((), ())), preferred_element_type=F32)


def _dot_tn(a, b):
    return lax.dot_general(a, b, (((0,), (0,)), ((), ())), preferred_element_type=F32)


def _inproj_fwd(h, g_pre, w_in):
    t = h.shape[0]

    def body(h_ref, g_ref, w_ref, pc_ref, qkv_ref, sbg_ref, u_ref):
        x = h_ref[...]
        r = lax.rsqrt(jnp.mean(x * x, axis=-1, keepdims=True) + RMS_EPS)
        u = (x * r * g_ref[...]).astype(BF16)
        u_ref[...] = u
        pc_ref[...] = _dot(u, w_ref[:, 0:1536])
        qkv_ref[...] = _dot(u, w_ref[:, 1536:3072]).astype(BF16)
        sbg_ref[...] = _dot(u, w_ref[:, 3072:3584])

    return pl.pallas_call(
        body, name="inproj_fwd", grid=(t // ROW_BLK,),
        in_specs=[_rows(D_MODEL), _whole((1, D_MODEL)), _whole((D_MODEL, D_IN))],
        out_specs=[_rows(1536), _rows(1536), _rows(D_SB), _rows(D_MODEL)],
        out_shape=[jax.ShapeDtypeStruct((t, 1536), F32), jax.ShapeDtypeStruct((t, 1536), BF16),
                   jax.ShapeDtypeStruct((t, D_SB), F32), jax.ShapeDtypeStruct((t, D_MODEL), BF16)],
        compiler_params=_cparams("parallel"),
    )(h, g_pre, w_in)


def _prev_halo(col):
    per = ROW_BLK // HALO
    return pl.BlockSpec((HALO, D_CONV), lambda i, col=col: (jnp.maximum(i * per - 1, 0), col))


def _fill_glu(buf, i, a_ref, b_ref, ha_ref, hb_ref):
    halo = ha_ref[...] * _sigmoid(hb_ref[...])
    buf[0:HALO, :] = jnp.where(i > 0, halo, 0.0)
    buf[HALO:HALO + ROW_BLK, :] = a_ref[...] * _sigmoid(b_ref[...])


SUBLANES = 8
TAP_ROWS = 64
SHIFT_ROWS = HALO + ROW_BLK - SUBLANES


def _fill_shifts(shifts, buf):
    for b in range(1, SUBLANES):
        shifts[b - 1] = buf[pl.ds(b, SHIFT_ROWS), :]


def _window(buf, shifts, first, rows, lanes):
    whole, part = divmod(first, SUBLANES)
    src = buf if part == 0 else shifts.at[part - 1]
    return src[pl.ds(rows.start + SUBLANES * whole, rows.size), lanes]


TAP_ROW_CHUNKS = [pl.ds(r, TAP_ROWS) for r in range(0, ROW_BLK, TAP_ROWS)]
TAP_LANE_TILES = [pl.ds(c, LANES) for c in range(0, D_CONV, LANES)]


def _layer_norm_stats(cv):
    mu = jnp.mean(cv, axis=-1, keepdims=True)
    xc = cv - mu
    rstd = lax.rsqrt(jnp.mean(xc * xc, axis=-1, keepdims=True) + LN_EPS)
    return xc * rstd, rstd


def _conv_fwd(pc, conv_w, conv_b, ln_g, ln_b, w_pw2, b_pw2):
    t = pc.shape[0]

    def body(a_ref, b_ref, gate_ref, ha_ref, hb_ref, cw_ref, cb_ref, lg_ref, lb_ref, wp_ref, bp_ref,
             cout_ref, cv_ref, p_ref, sl_ref, buf, shifts):
        i = pl.program_id(0)
        _fill_glu(buf, i, a_ref, b_ref, ha_ref, hb_ref)
        _fill_shifts(shifts, buf)
        for lanes in TAP_LANE_TILES:
            for rows in TAP_ROW_CHUNKS:
                acc = jnp.zeros((TAP_ROWS, LANES), F32) + cb_ref[:, lanes]
                for j in range(CONV_WIDTH):
                    acc = acc + cw_ref[j:j + 1, lanes] * _window(buf, shifts, HALO - (CONV_WIDTH - 1) + j, rows, lanes)
                cv_ref[rows, lanes] = acc
        xh, _ = _layer_norm_stats(cv_ref[...])
        ln = xh * lg_ref[...] + lb_ref[...]
        sl = (ln * _sigmoid(ln)).astype(BF16)
        sl_ref[...] = sl
        p = _dot(sl, wp_ref[...]) + bp_ref[...]
        p_ref[...] = p
        gate = gate_ref[...]
        cout_ref[...] = (p * (gate * _sigmoid(gate))).astype(BF16)

    vec = _whole((1, D_CONV))
    return pl.pallas_call(
        body, name="conv_fwd", grid=(t // ROW_BLK,),
        in_specs=[_rows(D_CONV, 0), _rows(D_CONV, 1), _rows(D_CONV, 2), _prev_halo(0), _prev_halo(1),
                  _whole((CONV_WIDTH, D_CONV)), vec, vec, vec, _whole((D_CONV, D_CONV)), vec],
        out_specs=[_rows(D_CONV)] * 4,
        out_shape=[jax.ShapeDtypeStruct((t, D_CONV), BF16), jax.ShapeDtypeStruct((t, D_CONV), F32),
                   jax.ShapeDtypeStruct((t, D_CONV), F32), jax.ShapeDtypeStruct((t, D_CONV), BF16)],
        scratch_shapes=[pltpu.VMEM((HALO + ROW_BLK, D_CONV), F32), pltpu.VMEM((SUBLANES - 1, SHIFT_ROWS, D_CONV), F32)],
        compiler_params=_cparams("parallel"),
    )(pc, pc, pc, pc, pc, conv_w, conv_b, ln_g, ln_b, w_pw2, b_pw2)


def _lower_triangle():
    row = lax.broadcasted_iota(jnp.int32, (ROW_BLK, ROW_BLK), 0)
    col = lax.broadcasted_iota(jnp.int32, (ROW_BLK, ROW_BLK), 1)
    return row > col


def _lower_triangle_t():
    row = lax.broadcasted_iota(jnp.int32, (ROW_BLK, ROW_BLK), 0)
    col = lax.broadcasted_iota(jnp.int32, (ROW_BLK, ROW_BLK), 1)
    return row < col


def _tri_sum(x, umat):
    return _dot(x.astype(BF16), umat)


def _log_gates(z):
    ls = -(jnp.maximum(z, 0.0) + jnp.log(1.0 + jnp.exp(-jnp.abs(z))))
    return ls, z + ls


def _head_lanes(hh):
    lane = lax.broadcasted_iota(jnp.int32, (ROW_BLK, HEAD_BLK), 1)
    return (lane >= HEAD_DIM * hh) & (lane < HEAD_DIM * (hh + 1))


def _merge_heads(acc_ref):
    out = acc_ref[HEADS_PER_BLOCK - 1]
    for hh in range(HEADS_PER_BLOCK - 1):
        out = jnp.where(_head_lanes(hh), acc_ref[hh], out)
    return out


def _qkv_specs(t):
    n_blk = D_SB // HEAD_BLK
    return [pl.BlockSpec((ROW_BLK, HEAD_BLK), lambda hp, i: (i, hp)),
            pl.BlockSpec((t, HEAD_BLK), lambda hp, i: (0, n_blk + hp)),
            pl.BlockSpec((t, HEAD_BLK), lambda hp, i: (0, 2 * n_blk + hp))]


def _carry_spec():
    return pl.BlockSpec((HEADS_PER_BLOCK, ROW_BLK, LANES), lambda hp, i: (hp, i, 0))


def _attn_fwd(qkv):
    t = qkv.shape[0]
    assert t // ROW_BLK <= LANES

    def body(q_ref, k_ref, v_ref, o_ref, c_ref, acc_ref, run_ref, qm_ref, z_ref):
        i = pl.program_id(1)
        lower = _lower_triangle()
        umat = jnp.where(lower, 1.0, 0.0).astype(BF16)
        lane = lax.broadcasted_iota(jnp.int32, (ROW_BLK, LANES), 1)
        q = q_ref[...]
        heads = range(HEADS_PER_BLOCK)
        for hh in heads:
            qm_ref[hh] = jnp.where(_head_lanes(hh), q, jnp.zeros_like(q)) * jnp.asarray(SB_SCALE, BF16)
        acc_ref[...] = jnp.zeros_like(acc_ref)
        c_ref[...] = jnp.zeros_like(c_ref)
        run_ref[...] = jnp.zeros_like(run_ref)

        def scores(jb):
            start = pl.multiple_of(jb * ROW_BLK, ROW_BLK)
            kb = k_ref[pl.ds(start, ROW_BLK), :]
            for hh in heads:
                z_ref[hh] = _dot_nt(qm_ref[hh], kb)

        def block(jb, diagonal):
            start = pl.multiple_of(jb * ROW_BLK, ROW_BLK)
            vb = v_ref[pl.ds(start, ROW_BLK), :]
            logits = []
            for hh in heads:
                ls, lb = _log_gates(z_ref[hh])
                if diagonal:
                    ls = jnp.where(lower, ls, 0.0)
                run = run_ref[hh]
                if not diagonal:
                    c_ref[hh] = jnp.where(lane == jb, run, c_ref[hh])
                logits.append(lb + jnp.concatenate([run, run], axis=1) + _tri_sum(ls, umat))
                run_ref[hh] = run + jnp.sum(ls, axis=1, keepdims=True)
            scores(jnp.maximum(jb - 1, 0))
            for hh in heads:
                a = jnp.exp(logits[hh])
                if diagonal:
                    a = jnp.where(lower, a, 0.0)
                acc_ref[hh] += _dot(a.astype(BF16), vb)

        scores(i)
        block(i, True)

        @pl.loop(0, i)
        def _(n):
            block(i - 1 - n, False)

        o_ref[...] = _merge_heads(acc_ref)

    per_head = (HEADS_PER_BLOCK, ROW_BLK, HEAD_BLK)
    return pl.pallas_call(
        body, name="attn_fwd", grid=(D_SB // HEAD_BLK, t // ROW_BLK),
        in_specs=_qkv_specs(t),
        out_specs=[pl.BlockSpec((ROW_BLK, HEAD_BLK), lambda hp, i: (i, hp)), _carry_spec()],
        out_shape=[jax.ShapeDtypeStruct((t, D_SB), F32),
                   jax.ShapeDtypeStruct((D_SB // HEAD_DIM, t, LANES), F32)],
        scratch_shapes=[pltpu.VMEM(per_head, F32), pltpu.VMEM((HEADS_PER_BLOCK, ROW_BLK, LANES), F32),
                        pltpu.VMEM(per_head, BF16), pltpu.VMEM((HEADS_PER_BLOCK, ROW_BLK, ROW_BLK), F32)],
        compiler_params=_cparams("arbitrary", "arbitrary"),
    )(qkv, qkv, qkv)


def _outproj_fwd(h, cout, sraw, sbg, w_out, g_post):
    t = h.shape[0]

    def body(h_ref, c_ref, s_ref, g_ref, w_ref, gp_ref, hn_ref, mixed_ref, mix_ref):
        gate = g_ref[...]
        mix_ref[:, 0:D_CONV] = c_ref[...]
        mix_ref[:, D_CONV:] = (s_ref[...] * (gate * _sigmoid(gate))).astype(BF16)
        mixed = _dot(mix_ref[...], w_ref[...])
        mixed_ref[...] = mixed
        r = lax.rsqrt(jnp.mean(mixed * mixed, axis=-1, keepdims=True) + RMS_EPS)
        hn_ref[...] = h_ref[...] + mixed * r * gp_ref[...]

    return pl.pallas_call(
        body, name="outproj_fwd", grid=(t // ROW_BLK,),
        in_specs=[_rows(D_MODEL), _rows(D_CONV), _rows(D_SB), _rows(D_SB), _whole((D_MODEL, D_MODEL)),
                  _whole((1, D_MODEL))],
        out_specs=[_rows(D_MODEL)] * 3,
        out_shape=[jax.ShapeDtypeStruct((t, D_MODEL), F32), jax.ShapeDtypeStruct((t, D_MODEL), F32),
                   jax.ShapeDtypeStruct((t, D_MODEL), BF16)],
        compiler_params=_cparams("parallel"),
    )(h, cout, sraw, sbg, w_out, g_post)


def _loss_and_grad(h, target, seq):
    t = h.shape[0]

    def body(h_ref, t_ref, loss_ref, dh_ref):
        i = pl.program_id(0)

        @pl.when(i == 0)
        def _():
            loss_ref[...] = jnp.zeros_like(loss_ref)

        row = i * ROW_BLK + lax.broadcasted_iota(jnp.int32, (ROW_BLK, D_MODEL), 0)
        real = (row >= N_META) & (row < N_META + seq)
        diff = jnp.where(real, h_ref[...] - t_ref[...], 0.0)
        sq = jnp.sum(jnp.sum(diff * diff, axis=1, keepdims=True), axis=0, keepdims=True)
        loss_ref[...] += (0.5 / D_MODEL) * sq
        dh_ref[...] = diff * (1.0 / D_MODEL)

    return pl.pallas_call(
        body, name="loss", grid=(t // ROW_BLK,),
        in_specs=[_rows(D_MODEL), _rows(D_MODEL)],
        out_specs=[_whole((1, 1)), _rows(D_MODEL)],
        out_shape=[jax.ShapeDtypeStruct((1, 1), F32), jax.ShapeDtypeStruct((t, D_MODEL), F32)],
        compiler_params=_cparams("arbitrary"),
    )(h, target)


def _outproj_bwd(dh, mixed, g_post, sraw, sbg, w_out_t):
    t = dh.shape[0]

    def body(dh_ref, mixed_ref, gp_ref, s_ref, g_ref, wt_ref, dc_ref, ds_ref, dg_ref, dmb_ref, dgp_ref):
        @pl.when(pl.program_id(0) == 0)
        def _():
            dgp_ref[...] = jnp.zeros_like(dgp_ref)

        mixed = mixed_ref[...]
        r = lax.rsqrt(jnp.mean(mixed * mixed, axis=-1, keepdims=True) + RMS_EPS)
        nh = mixed * r
        dy = dh_ref[...]
        dgp_ref[...] += jnp.sum(dy * nh, axis=0, keepdims=True)
        dn = dy * gp_ref[...]
        dmixed = (r * (dn - nh * jnp.mean(dn * nh, axis=-1, keepdims=True))).astype(BF16)
        dmb_ref[...] = dmixed
        dmix = _dot(dmixed, wt_ref[...])
        dc_ref[...] = dmix[:, 0:D_CONV]
        dsg = dmix[:, D_CONV:]
        gate = g_ref[...]
        sg = _sigmoid(gate)
        ds_ref[...] = dsg * (gate * sg)
        dg_ref[...] = dsg * s_ref[...] * _dsilu(gate, sg)

    return pl.pallas_call(
        body, name="outproj_bwd", grid=(t // ROW_BLK,),
        in_specs=[_rows(D_MODEL), _rows(D_MODEL), _whole((1, D_MODEL)), _rows(D_SB), _rows(D_SB),
                  _whole((D_MODEL, D_MODEL))],
        out_specs=[_rows(D_CONV), _rows(D_SB), _rows(D_SB), _rows(D_MODEL), _whole((1, D_MODEL))],
        out_shape=[jax.ShapeDtypeStruct((t, D_CONV), F32), jax.ShapeDtypeStruct((t, D_SB), F32),
                   jax.ShapeDtypeStruct((t, D_SB), F32), jax.ShapeDtypeStruct((t, D_MODEL), BF16),
                   jax.ShapeDtypeStruct((1, D_MODEL), F32)],
        compiler_params=_cparams("arbitrary"),
    )(dh, mixed, g_post, sraw, sbg, w_out_t)


def _attn_bwd(qkv, carries, do):
    t = qkv.shape[0]

    def body(q_ref, k_ref, v_ref, c_ref, do_ref, dq_ref, dk_ref, dv_ref, acc_ref, seen_ref, qm_ref, dom_ref, z_ref,
             da_ref, dz_ref, a_ref):
        i = pl.program_id(1)

        @pl.when(i == 0)
        def _():
            dk_ref[...] = jnp.zeros_like(dk_ref)
            dv_ref[...] = jnp.zeros_like(dv_ref)

        lower = _lower_triangle()
        umat = jnp.where(lower, 1.0, 0.0).astype(BF16)
        umat_t = jnp.where(_lower_triangle_t(), 1.0, 0.0).astype(BF16)
        lane = lax.broadcasted_iota(jnp.int32, (ROW_BLK, LANES), 1)
        q = q_ref[...]
        dof = do_ref[...]
        heads = range(HEADS_PER_BLOCK)
        for hh in heads:
            qm_ref[hh] = jnp.where(_head_lanes(hh), q, jnp.zeros_like(q)) * jnp.asarray(SB_SCALE, BF16)
            dom_ref[hh] = jnp.where(_head_lanes(hh), dof, 0.0).astype(BF16)
        acc_ref[...] = jnp.zeros_like(acc_ref)
        seen_ref[...] = jnp.zeros_like(seen_ref)

        def scores(jb):
            start = pl.multiple_of(jb * ROW_BLK, ROW_BLK)
            kb = k_ref[pl.ds(start, ROW_BLK), :]
            for hh in heads:
                z_ref[hh] = _dot_nt(qm_ref[hh], kb)

        def value_grads(jb):
            start = pl.multiple_of(jb * ROW_BLK, ROW_BLK)
            vb = v_ref[pl.ds(start, ROW_BLK), :]
            for hh in heads:
                da_ref[hh] = _dot_nt(dom_ref[hh], vb)

        def products(jb, hh):
            start = pl.multiple_of(jb * ROW_BLK, ROW_BLK)
            dzb = dz_ref[hh]
            acc_ref[hh] += _dot(dzb, k_ref[pl.ds(start, ROW_BLK), :])
            dk_ref[pl.ds(start, ROW_BLK), :] += _dot_tn(dzb, qm_ref[hh])
            dv_ref[pl.ds(start, ROW_BLK), :] += _dot_tn(a_ref[hh], dom_ref[hh])

        def block(jb, diagonal):
            before = jnp.maximum(jb - 1, 0)
            lbs, logits = [], []
            for hh in heads:
                products(before, hh)
                ls, lb = _log_gates(z_ref[hh])
                if diagonal:
                    ls = jnp.where(lower, ls, 0.0)
                    logits.append(lb + _tri_sum(ls, umat))
                else:
                    right = jnp.sum(jnp.where(lane == jb, c_ref[hh], 0.0), axis=1, keepdims=True)
                    logits.append(lb + right + _tri_sum(ls, umat))
                lbs.append(lb)
            if not diagonal:
                scores(jb + 1)
            gs, befores = [], []
            for hh in heads:
                a = jnp.exp(logits[hh])
                if diagonal:
                    a = jnp.where(lower, a, 0.0)
                g = da_ref[hh] * a
                seen = seen_ref[hh]
                befores.append(jnp.concatenate([seen, seen], axis=1) + _tri_sum(g, umat_t))
                seen_ref[hh] = seen + jnp.sum(g, axis=1, keepdims=True)
                a_ref[hh] = a.astype(BF16)
                gs.append(g)
            if not diagonal:
                value_grads(jb + 1)
            for hh in heads:
                dz = gs[hh] - jnp.exp(lbs[hh]) * (gs[hh] + befores[hh])
                if diagonal:
                    dz = jnp.where(lower, dz, 0.0)
                dz_ref[hh] = dz.astype(BF16)

        dz_ref[...] = jnp.zeros_like(dz_ref)
        a_ref[...] = jnp.zeros_like(a_ref)
        scores(0)
        value_grads(0)

        @pl.loop(0, i)
        def _(jb):
            block(jb, False)

        block(i, True)
        for hh in heads:
            products(i, hh)
        dq_ref[...] = _merge_heads(acc_ref) * SB_SCALE

    blk = pl.BlockSpec((ROW_BLK, HEAD_BLK), lambda hp, i: (i, hp))
    full = pl.BlockSpec((t, HEAD_BLK), lambda hp, i: (0, hp))
    per_head = (HEADS_PER_BLOCK, ROW_BLK, HEAD_BLK)
    return pl.pallas_call(
        body, name="attn_bwd", grid=(D_SB // HEAD_BLK, t // ROW_BLK),
        in_specs=_qkv_specs(t) + [_carry_spec(), blk],
        out_specs=[blk, full, full],
        out_shape=[jax.ShapeDtypeStruct((t, D_SB), F32)] * 3,
        scratch_shapes=[pltpu.VMEM(per_head, F32), pltpu.VMEM((HEADS_PER_BLOCK, ROW_BLK, LANES), F32),
                        pltpu.VMEM(per_head, BF16), pltpu.VMEM(per_head, BF16)]
                       + [pltpu.VMEM((HEADS_PER_BLOCK, ROW_BLK, ROW_BLK), dtype) for dtype in (F32, F32, BF16, BF16)],
        compiler_params=_cparams("arbitrary", "arbitrary"),
    )(qkv, qkv, qkv, carries, do)


def _conv_bwd_rows(dcout, pc, cv, p, ln_g, ln_b, w_pw2_t):
    t = dcout.shape[0]

    def body(dc_ref, gate_ref, cv_ref, p_ref, lg_ref, lb_ref, wt_ref, dcv_ref, dgate_ref, dpb_ref, vec_ref):
        @pl.when(pl.program_id(0) == 0)
        def _():
            vec_ref[...] = jnp.zeros_like(vec_ref)

        dc = dc_ref[...]
        gate = gate_ref[...]
        sg = _sigmoid(gate)
        dp = dc * (gate * sg)
        dgate_ref[...] = dc * p_ref[...] * _dsilu(gate, sg)
        dpb = dp.astype(BF16)
        dpb_ref[...] = dpb
        xh, rstd = _layer_norm_stats(cv_ref[...])
        ln = xh * lg_ref[...] + lb_ref[...]
        s2 = _sigmoid(ln)
        dln = _dot(dpb, wt_ref[...]) * _dsilu(ln, s2)
        dxh = dln * lg_ref[...]
        dcv = rstd * (dxh - jnp.mean(dxh, axis=-1, keepdims=True)
                      - xh * jnp.mean(dxh * xh, axis=-1, keepdims=True))
        dcv_ref[...] = dcv
        vec_ref[0:1, :] += jnp.sum(dp, axis=0, keepdims=True)
        vec_ref[1:2, :] += jnp.sum(dln * xh, axis=0, keepdims=True)
        vec_ref[2:3, :] += jnp.sum(dln, axis=0, keepdims=True)
        vec_ref[3:4, :] += jnp.sum(dcv, axis=0, keepdims=True)

    vec = _whole((1, D_CONV))
    return pl.pallas_call(
        body, name="conv_bwd_rows", grid=(t // ROW_BLK,),
        in_specs=[_rows(D_CONV), _rows(D_CONV, 2), _rows(D_CONV), _rows(D_CONV), vec, vec,
                  _whole((D_CONV, D_CONV))],
        out_specs=[_rows(D_CONV), _rows(D_CONV), _rows(D_CONV), _whole((8, D_CONV))],
        out_shape=[jax.ShapeDtypeStruct((t, D_CONV), F32), jax.ShapeDtypeStruct((t, D_CONV), F32),
                   jax.ShapeDtypeStruct((t, D_CONV), BF16), jax.ShapeDtypeStruct((8, D_CONV), F32)],
        compiler_params=_cparams("arbitrary"),
    )(dcout, pc, cv, p, ln_g, ln_b, w_pw2_t)


def _conv_bwd_taps(dcv, pc, conv_w):
    t = dcv.shape[0]
    n_halo = t // HALO
    per = ROW_BLK // HALO

    def body(d_ref, dn_ref, a_ref, b_ref, ha_ref, hb_ref, cw_ref, da_ref, db_ref, dw_ref, cbuf, dbuf, cshifts, dshifts):
        i = pl.program_id(0)

        @pl.when(i == 0)
        def _():
            dw_ref[...] = jnp.zeros_like(dw_ref)

        _fill_glu(cbuf, i, a_ref, b_ref, ha_ref, hb_ref)
        dbuf[0:ROW_BLK, :] = d_ref[...]
        dbuf[ROW_BLK:ROW_BLK + HALO, :] = jnp.where(i < pl.num_programs(0) - 1, dn_ref[...], 0.0)
        _fill_shifts(cshifts, cbuf)
        _fill_shifts(dshifts, dbuf)
        for lanes in TAP_LANE_TILES:
            for rows in TAP_ROW_CHUNKS:
                acc = jnp.zeros((TAP_ROWS, LANES), F32)
                for j in range(CONV_WIDTH):
                    acc = acc + cw_ref[j:j + 1, lanes] * _window(dbuf, dshifts, CONV_WIDTH - 1 - j, rows, lanes)
                sb = _sigmoid(b_ref[rows, lanes])
                da_ref[rows, lanes] = acc * sb
                db_ref[rows, lanes] = acc * a_ref[rows, lanes] * sb * (1.0 - sb)
            for j in range(CONV_WIDTH):
                acc = jnp.zeros((TAP_ROWS, LANES), F32)
                for rows in TAP_ROW_CHUNKS:
                    acc = acc + d_ref[rows, lanes] * _window(cbuf, cshifts, HALO - (CONV_WIDTH - 1) + j, rows, lanes)
                dw_ref[j:j + 1, lanes] += jnp.sum(acc, axis=0, keepdims=True)

    return pl.pallas_call(
        body, name="conv_bwd_taps", grid=(t // ROW_BLK,),
        in_specs=[_rows(D_CONV),
                  pl.BlockSpec((HALO, D_CONV), lambda i: (jnp.minimum((i + 1) * per, n_halo - 1), 0)),
                  _rows(D_CONV, 0), _rows(D_CONV, 1), _prev_halo(0), _prev_halo(1),
                  _whole((CONV_WIDTH, D_CONV))],
        out_specs=[_rows(D_CONV), _rows(D_CONV), _whole((32, D_CONV))],
        out_shape=[jax.ShapeDtypeStruct((t, D_CONV), F32), jax.ShapeDtypeStruct((t, D_CONV), F32),
                   jax.ShapeDtypeStruct((32, D_CONV), F32)],
        scratch_shapes=[pltpu.VMEM((HALO + ROW_BLK, D_CONV), F32), pltpu.VMEM((ROW_BLK + HALO, D_CONV), F32),
                        pltpu.VMEM((SUBLANES - 1, SHIFT_ROWS, D_CONV), F32),
                        pltpu.VMEM((SUBLANES - 1, SHIFT_ROWS, D_CONV), F32)],
        compiler_params=_cparams("arbitrary"),
    )(dcv, dcv, pc, pc, pc, pc, conv_w)


def _inproj_bwd(dh_out, h, g_pre, pieces, w_in_t):
    t = h.shape[0]

    def body(dh_ref, h_ref, g_ref, *rest):
        piece_refs, (wt_ref, dhin_ref, dproj_ref, dg_ref) = rest[:7], rest[7:]

        @pl.when(pl.program_id(0) == 0)
        def _():
            dg_ref[...] = jnp.zeros_like(dg_ref)

        for k, ref in enumerate(piece_refs):
            dproj_ref[:, 512 * k:512 * (k + 1)] = ref[...].astype(BF16)
        du = _dot(dproj_ref[...], wt_ref[...])
        x = h_ref[...]
        r = lax.rsqrt(jnp.mean(x * x, axis=-1, keepdims=True) + RMS_EPS)
        xh = x * r
        dg_ref[...] += jnp.sum(du * xh, axis=0, keepdims=True)
        dxh = du * g_ref[...]
        dhin_ref[...] = dh_ref[...] + r * (dxh - xh * jnp.mean(dxh * xh, axis=-1, keepdims=True))

    return pl.pallas_call(
        body, name="inproj_bwd", grid=(t // ROW_BLK,),
        in_specs=[_rows(D_MODEL), _rows(D_MODEL), _whole((1, D_MODEL))] + [_rows(512)] * 7
                 + [_whole((D_IN, D_MODEL))],
        out_specs=[_rows(D_MODEL), _rows(D_IN), _whole((1, D_MODEL))],
        out_shape=[jax.ShapeDtypeStruct((t, D_MODEL), F32), jax.ShapeDtypeStruct((t, D_IN), BF16),
                   jax.ShapeDtypeStruct((1, D_MODEL), F32)],
        compiler_params=_cparams("arbitrary"),
    )(dh_out, h, g_pre, *pieces, w_in_t)


def _weight_grad(xb, dyb, name):
    t, k = xb.shape
    n = dyb.shape[1]
    tn = n

    def body(x_ref, dy_ref, o_ref):
        @pl.when(pl.program_id(1) == 0)
        def _():
            o_ref[...] = jnp.zeros_like(o_ref)

        o_ref[...] += _dot_tn(x_ref[...], dy_ref[...])

    return pl.pallas_call(
        body, name=name, grid=(n // tn, t // ROW_BLK),
        in_specs=[pl.BlockSpec((ROW_BLK, k), lambda j, i: (i, 0)), pl.BlockSpec((ROW_BLK, tn), lambda j, i: (i, j))],
        out_specs=pl.BlockSpec((k, tn), lambda j, i: (0, j)),
        out_shape=jax.ShapeDtypeStruct((k, n), F32),
        compiler_params=_cparams("parallel", "arbitrary"),
    )(xb, dyb)


def _position():
    return lax.axis_index("x"), lax.axis_index("y"), lax.axis_index("c")


def _comm_call(body, name, ins, out_shapes):
    n = len(ins)
    hbm = pl.BlockSpec(memory_space=pltpu.HBM)
    return pl.pallas_call(
        functools.partial(body, n), name=name, in_specs=[hbm] * n, out_specs=[hbm] * n, out_shape=out_shapes,
        scratch_shapes=[pltpu.SemaphoreType.DMA((n, N_DEV - 1)), pltpu.SemaphoreType.DMA((n, N_DEV - 1)),
                        pltpu.SemaphoreType.DMA((n,))],
    )(*ins)


def _all_gather(blocks, name):
    def body(n, *refs):
        x_refs, out_refs, (send_sems, recv_sems, local_sems) = refs[:n], refs[n:2 * n], refs[2 * n:]
        x, y, c = _position()
        me, sibling = (x, y, c), (x, y, 1 - c)
        chips = [(1 - x, y), (x, 1 - y), (1 - x, 1 - y)]

        def slot(a, px, py, pc):
            return out_refs[a].at[4 * px + 2 * py + pc]

        def copy(a, k, origin, to, own=False):
            return pltpu.make_async_remote_copy(
                src_ref=x_refs[a] if own else slot(a, *origin), dst_ref=slot(a, *origin),
                send_sem=send_sems.at[a, k], recv_sem=recv_sems.at[a, k], device_id=to, device_id_type=MESH)

        arrays = range(n)
        mine = [pltpu.make_async_copy(x_refs[a], slot(a, *me), local_sems.at[a]) for a in arrays]
        first = [copy(a, 1 + j, me, (*chip, c), own=True) for j, chip in enumerate(chips) for a in arrays]
        first += [copy(a, 0, me, sibling, own=True) for a in arrays]
        for cp in mine + first:
            cp.start()
        passed = []
        for j, chip in enumerate(chips):
            for a in arrays:
                copy(a, 1 + j, (*chip, c), me).wait_recv()
                passed.append(copy(a, 4 + j, (*chip, c), sibling))
                passed[-1].start()
        for a in arrays:
            copy(a, 0, sibling, me).wait_recv()
            for j, chip in enumerate(chips):
                copy(a, 4 + j, (*chip, 1 - c), me).wait_recv()
        for cp in first + passed:
            cp.wait_send()
        for cp in mine:
            cp.wait()

    return _comm_call(body, name, blocks, [jax.ShapeDtypeStruct((N_DEV,) + b.shape, b.dtype) for b in blocks])


def _exchange_copies(g_refs, land_refs, sems, gather):
    x, y, c = _position()
    me = 4 * x + 2 * y + c
    out = []
    for g_ref, land_ref, (send_sem, recv_sem, local_sem) in zip(g_refs, land_refs, sems):
        def mine(slot, g_ref=g_ref):
            return g_ref if gather else g_ref.at[slot]

        def remote(src, dst, dev):
            return pltpu.make_async_remote_copy(src_ref=src, dst_ref=dst, send_sem=send_sem, recv_sem=recv_sem,
                                                device_id=dev, device_id_type=MESH)

        sends = []
        for k in range(1, N_DEV):
            px = 1 - x if k & 4 else x
            py = 1 - y if k & 2 else y
            pc = 1 - c if k & 1 else c
            sends.append(remote(mine(4 * px + 2 * py + pc), land_ref.at[me], (px, py, pc)))
        seven = land_ref.at[pl.ds(0, N_DEV - 1)]
        out.append((pltpu.make_async_copy(mine(me), land_ref.at[me], local_sem), sends, remote(seven, seven, (x, y, c))))
    return out


_HBM = pl.BlockSpec(memory_space=pltpu.HBM)
_SEM = pl.BlockSpec(memory_space=pltpu.SEMAPHORE)
_ORDERED = pltpu.CompilerParams(has_side_effects=pltpu.SideEffectType.DATAFLOW_SIDE_EFFECTING)
SEMS_PER_ARRAY = 3


def _exchange_start(arrays, after, name, gather):
    n = len(arrays)
    n_sems = SEMS_PER_ARRAY * n

    def body(*refs):
        g_refs, land_refs, sems, token = refs[:n], refs[n:2 * n], refs[2 * n + 1:2 * n + 1 + n_sems], refs[-1]
        sems = [sems[SEMS_PER_ARRAY * a:SEMS_PER_ARRAY * (a + 1)] for a in range(n)]
        for local, sends, _ in _exchange_copies(g_refs, land_refs, sems, gather):
            local.start()
            for cp in sends:
                cp.start()
        token[...] = jnp.zeros_like(token)

    buffers = list(arrays) + [lax.empty((N_DEV,) + g.shape if gather else g.shape, g.dtype) for g in arrays]
    outs = pl.pallas_call(
        body, name=name, in_specs=[_HBM] * (2 * n) + [pl.BlockSpec(memory_space=pl.ANY)],
        out_specs=[_SEM] * n_sems + [_HBM] * (2 * n) + [pl.BlockSpec(memory_space=pltpu.VMEM)],
        out_shape=[pltpu.SemaphoreType.DMA(())] * n_sems + [pltpu.HBM(b.shape, b.dtype) for b in buffers]
                  + [jax.ShapeDtypeStruct((8, LANES), F32)],
        input_output_aliases={a: n_sems + a for a in range(2 * n)}, compiler_params=_ORDERED,
    )(*[pltpu.with_memory_space_constraint(b, pltpu.HBM) for b in buffers], after)
    return outs[:n_sems], outs[n_sems:n_sems + n], outs[n_sems + n:n_sems + 2 * n], outs[-1]


def _exchange_wait(sems, arrays, landings, after, name, gather):
    n = len(arrays)
    n_sems = SEMS_PER_ARRAY * n

    def body(*refs):
        g_refs, land_refs, sems = refs[:n], refs[n:2 * n], refs[2 * n:2 * n + n_sems]
        sems = [sems[SEMS_PER_ARRAY * a:SEMS_PER_ARRAY * (a + 1)] for a in range(n)]
        for local, _, all_seven in _exchange_copies(g_refs, land_refs, sems, gather):
            all_seven.wait_recv()
            all_seven.wait_send()
            local.wait()

    buffers = list(arrays) + list(landings)
    outs = pl.pallas_call(
        body, name=name, in_specs=[_HBM] * (2 * n) + [_SEM] * n_sems + [pl.BlockSpec(memory_space=pl.ANY)],
        out_specs=[_HBM] * (2 * n), out_shape=[pltpu.HBM(b.shape, b.dtype) for b in buffers],
        input_output_aliases={a: a for a in range(2 * n)}, compiler_params=_ORDERED,
    )(*buffers, *sems, after)
    return outs[n:]


def _block_rows(r, row_bytes, budget=1 << 20):
    cap = max(8, budget // row_bytes)
    return max(d for d in range(8, min(r, cap) + 1, 8) if r % d == 0)


def _sum_adamw(parts, w, m, v, name):
    n_parts, r, c = parts.shape
    br = _block_rows(r, 4 * c)

    def body(p_ref, w_ref, m_ref, v_ref, g_out, d_out, m_out, v_out):
        g = p_ref[0].astype(F32)
        for s in range(1, n_parts):
            g = g + p_ref[s].astype(F32)
        m_new = ADAM_B1 * m_ref[...] + (1.0 - ADAM_B1) * g
        v_new = ADAM_B2 * v_ref[...] + (1.0 - ADAM_B2) * (g * g)
        m_hat = m_new / (1.0 - ADAM_B1 ** ADAM_STEP)
        v_hat = v_new / (1.0 - ADAM_B2 ** ADAM_STEP)
        g_out[...] = g
        d_out[...] = -ADAM_LR * (m_hat / (jnp.sqrt(v_hat) + ADAM_EPS) + ADAM_WD * w_ref[...])
        m_out[...] = m_new
        v_out[...] = v_new

    row = pl.BlockSpec((br, c), lambda i: (i, 0))
    return pl.pallas_call(
        body, name=name, grid=(r // br,),
        in_specs=[pl.BlockSpec((n_parts, br, c), lambda i: (0, i, 0)), row, row, row],
        out_specs=[row] * 4, out_shape=[jax.ShapeDtypeStruct((r, c), F32)] * 4,
        compiler_params=_cparams("parallel"),
    )(parts, w, m, v)


def _sum_parts(parts, name):
    n_parts, r, c = parts.shape

    def body(p_ref, o_ref):
        g = p_ref[0]
        for s in range(1, n_parts):
            g = g + p_ref[s]
        o_ref[...] = g

    return pl.pallas_call(
        body, name=name, in_specs=[pl.BlockSpec(memory_space=pltpu.VMEM)],
        out_specs=pl.BlockSpec(memory_space=pltpu.VMEM), out_shape=jax.ShapeDtypeStruct((r, c), F32),
    )(parts)


def _pack(arrays):
    flat = jnp.concatenate([a.reshape(-1) for a in arrays])
    pad = -flat.shape[0] % (8 * LANES)
    if pad:
        flat = jnp.pad(flat, (0, pad))
    return flat.reshape(-1, LANES)


def _unpack(buf, shapes):
    flat = buf.reshape(-1)
    out, at = [], 0
    for shape in shapes:
        size = 1
        for d in shape:
            size *= d
        out.append(lax.slice_in_dim(flat, at, at + size).reshape(shape))
        at += size
    return out


def _local_step(x, target, meta, pre_g, post_g, conv_w, conv_b, ln_g, ln_b, b_pw2, weights, ship):
    depth = pre_g.shape[0]
    seq = x.shape[0]
    t = -(-(N_META + seq) // ROW_BLK) * ROW_BLK
    tail = t - N_META - seq
    h = jnp.concatenate([meta, x, jnp.zeros((tail, D_MODEL), F32)], axis=0)
    target = jnp.pad(target, ((N_META, tail), (0, 0)))
    row = lambda a, l: a[l][None, :]

    saved = []
    for l in range(depth):
        w_in, w_pw2, w_out = weights(l, h)
        pc, qkv, sbg, u = _inproj_fwd(h, row(pre_g, l), w_in)
        cout, cv, p, sl = _conv_fwd(pc, conv_w[l], row(conv_b, l), row(ln_g, l), row(ln_b, l), w_pw2, row(b_pw2, l))
        sraw, carries = _attn_fwd(qkv)
        h_new, mixed, mix = _outproj_fwd(h, cout, sraw, sbg, w_out, row(post_g, l))
        saved.append((h, pc, qkv, sbg, u, cv, p, sl, sraw, carries, mixed, mix, w_in.T, w_pw2.T, w_out.T))
        h = h_new

    loss, dh = _loss_and_grad(h, target, seq)

    grads = {k: [None] * depth for k in ("pre_g", "post_g", "conv_w", "conv_b", "ln_g", "ln_b", "b_pw2")}
    token = jnp.zeros((8, LANES), F32)
    for l in reversed(range(depth)):
        h_in, pc, qkv, sbg, u, cv, p, sl, sraw, carries, mixed, mix, w_in_t, w_pw2_t, w_out_t = saved[l]
        dcout, dsraw, dsbg, dmixed, dg_post = _outproj_bwd(dh, mixed, row(post_g, l) + token[:1, :1], sraw, sbg, w_out_t)
        dq, dk, dv = _attn_bwd(qkv, carries, dsraw)
        dcv, dgate, dpb, vecs = _conv_bwd_rows(dcout, pc, cv, p, row(ln_g, l), row(ln_b, l), w_pw2_t)
        da, db, dconv_w = _conv_bwd_taps(dcv, pc, conv_w[l])
        dh, dproj, dg_pre = _inproj_bwd(dh, h_in, row(pre_g, l), (da, db, dgate, dq, dk, dv, dsbg), w_in_t)
        token = ship(l, dh, _weight_grad(u, dproj, "w_in_grad"), _weight_grad(sl, dpb, "w_pw2_grad"),
                     _weight_grad(mix, dmixed, "w_out_grad"))
        grads["pre_g"][l] = dg_pre[0]
        grads["post_g"][l] = dg_post[0]
        grads["b_pw2"][l], grads["ln_g"][l], grads["ln_b"][l], grads["conv_b"][l] = vecs[0], vecs[1], vecs[2], vecs[3]
        grads["conv_w"][l] = dconv_w[:CONV_WIDTH]

    grads = {k: jnp.stack(v) for k, v in grads.items()}
    grads["meta"] = dh[:N_META]
    return loss[0, 0], dh[N_META:N_META + seq], grads


def _shard_major(full, axis):
    shape = full.shape
    split = full.reshape(shape[:axis] + (N_DEV, shape[axis] // N_DEV) + shape[axis + 1:])
    return jnp.moveaxis(split, axis, 0)


def _whole_from_shards(shards, axis):
    moved = jnp.moveaxis(shards, 0, axis)
    shape = moved.shape
    return moved.reshape(shape[:axis] + (shape[axis] * shape[axis + 1],) + shape[axis + 2:])


def kernel(x, meta_tokens, pre_norm_g, post_norm_g, w_in, conv_w, conv_b, conv_ln_g, conv_ln_b, w_pw2, b_pw2, w_out, loss_target, m_meta_tokens, m_pre_norm_g, m_post_norm_g, m_w_in, m_conv_w, m_conv_b, m_conv_ln_g, m_conv_ln_b, m_w_pw2, m_b_pw2, m_w_out, v_meta_tokens, v_pre_norm_g, v_post_norm_g, v_w_in, v_conv_w, v_conv_b, v_conv_ln_g, v_conv_ln_b, v_w_pw2, v_b_pw2, v_w_out):
    me = 4 * lax.axis_index("x") + 2 * lax.axis_index("y") + lax.axis_index("c")

    depth = w_in.shape[0]
    big = [w.astype(BF16) for w in (w_in, w_pw2, w_out)]
    *first, conv_w_s, meta_s = _all_gather([w[0] for w in big] + [conv_w, meta_tokens], "gather_first_layer")
    *gathering, token = _exchange_start([w[1:] for w in big], meta_s, "gather_start", gather=True)
    conv_w_full = _whole_from_shards(conv_w_s, 2)
    meta_full = _whole_from_shards(meta_s, 1)
    shard_axis = (1, 0, 0)
    later = []

    def weights(l, h):
        if l == 0:
            return [_whole_from_shards(s, axis) for s, axis in zip(first, shard_axis)]
        if not later:
            later.extend(_exchange_wait(*gathering, h, "gather_wait", gather=True))
        return [_whole_from_shards(s[:, l - 1], axis) for s, axis in zip(later, shard_axis)]

    in_flight = [None] * depth

    def ship(l, dh, dw_in, dw_pw2, dw_out):
        slabs = [_shard_major(dw, axis).astype(BF16) for dw, axis in zip((dw_in, dw_pw2, dw_out), shard_axis)]
        *in_flight[l], token = _exchange_start(slabs, dh, f"exchange_start_{l}", gather=False)
        return token

    loss, dx, grads = _local_step(x[0], loss_target[0], meta_full, pre_norm_g + token[:1, :1], post_norm_g, conv_w_full,
                                  conv_b, conv_ln_g, conv_ln_b, b_pw2, weights, ship)
    loss = lax.psum(loss, ("x", "y", "c"))
    landed = [_exchange_wait(*in_flight[l], dx, f"exchange_wait_{l}", gather=False) for l in range(depth)]

    def update(a, w, m, v, name):
        outs = [_sum_adamw(landed[l][a], w[l], m[l], v[l], name) for l in range(depth)]
        return [jnp.stack(o) for o in zip(*outs)]

    g_w_in, d_w_in, nm_w_in, nv_w_in = update(0, w_in, m_w_in, v_w_in, "adamw_w_in")
    g_w_pw2, d_w_pw2, nm_w_pw2, nv_w_pw2 = update(1, w_pw2, m_w_pw2, v_w_pw2, "adamw_w_pw2")
    g_w_out, d_w_out, nm_w_out, nv_w_out = update(2, w_out, m_w_out, v_w_out, "adamw_w_out")

    small_names = ("pre_g", "post_g", "conv_b", "ln_g", "ln_b", "b_pw2", "conv_w", "meta")
    small_full = [grads[k] for k in small_names]
    gathered, = _all_gather([_pack(small_full)], "gather_small_grads")
    summed = _sum_parts(gathered, "sum_small_grads")
    g_small = dict(zip(small_names, _unpack(summed, [a.shape for a in small_full])))
    g_small["conv_w"] = lax.dynamic_slice_in_dim(g_small["conv_w"], me * conv_w.shape[2], conv_w.shape[2], axis=2)
    g_small["meta"] = lax.dynamic_slice_in_dim(g_small["meta"], me * meta_tokens.shape[1], meta_tokens.shape[1], axis=1)
    small_w = dict(zip(small_names, (pre_norm_g, post_norm_g, conv_b, conv_ln_g, conv_ln_b, b_pw2, conv_w, meta_tokens)))
    small_m = (m_pre_norm_g, m_post_norm_g, m_conv_b, m_conv_ln_g, m_conv_ln_b, m_b_pw2, m_conv_w, m_meta_tokens)
    small_v = (v_pre_norm_g, v_post_norm_g, v_conv_b, v_conv_ln_g, v_conv_ln_b, v_b_pw2, v_conv_w, v_meta_tokens)
    small_shapes = [small_w[k].shape for k in small_names]
    outs = _sum_adamw(_pack([g_small[k] for k in small_names])[None], _pack([small_w[k] for k in small_names]),
                      _pack(small_m), _pack(small_v), "adamw_small_weights")
    g_s, d_s, nm_s, nv_s = [dict(zip(small_names, _unpack(o, small_shapes))) for o in outs]

    def ordered(s, w_in_, w_pw2_, w_out_):
        return (s["meta"], s["pre_g"], s["post_g"], w_in_, s["conv_w"], s["conv_b"], s["ln_g"], s["ln_b"], w_pw2_,
                s["b_pw2"], w_out_)

    return (loss, dx[None], *ordered(g_s, g_w_in, g_w_pw2, g_w_out), *ordered(d_s, d_w_in, d_w_pw2, d_w_out),
            *ordered(nm_s, nm_w_in, nm_w_pw2, nm_w_out), *ordered(nv_s, nv_w_in, nv_w_pw2, nv_w_out))
```

```python
import functools

import jax
import jax.numpy as jnp
from jax import lax
from jax.experimental import pallas as pl
from jax.experimental.pallas import tpu as pltpu

F32 = jnp.float32
BF16 = jnp.bfloat16

D_MODEL = 1024
D_CONV = 512
D_SB = 512
HEAD_DIM = 64
HEADS_PER_BLOCK = 4
HEAD_BLK = HEADS_PER_BLOCK * HEAD_DIM
CONV_WIDTH = 31
N_META = 16
D_IN = 3 * D_CONV + 4 * D_SB
RMS_EPS = 1e-6
LN_EPS = 1e-5
SB_SCALE = HEAD_DIM ** -0.5

ADAM_LR = 0.001
ADAM_B1 = 0.9
ADAM_B2 = 0.999
ADAM_EPS = 1e-08
ADAM_WD = 0.01
ADAM_STEP = 10

N_DEV = 8
LANES = 128
ROW_BLK = 256
HALO = 32
VMEM_LIMIT = 56 * 1024 * 1024
MESH = pl.DeviceIdType.MESH


def _cparams(*sem):
    return pltpu.CompilerParams(dimension_semantics=sem, vmem_limit_bytes=VMEM_LIMIT)


def _rows(n_cols, col=0):
    return pl.BlockSpec((ROW_BLK, n_cols), lambda i, col=col: (i, col))


def _whole(shape):
    return pl.BlockSpec(shape, lambda i: (0,) * len(shape))


def _sigmoid(x):
    return jax.nn.sigmoid(x)


def _dsilu(x, s):
    return s * (1.0 + x * (1.0 - s))


def _dot(a, b):
    return jnp.dot(a, b, preferred_element_type=F32)


def _dot_nt(a, b):
    return lax.dot_general(a, b, (((1,), (1,)), ((), ())), preferred_element_type=F32)


def _dot_tn(a, b):
    return lax.dot_general(a, b, (((0,), (0,)), ((), ())), preferred_element_type=F32)


def _inproj_fwd(h, g_pre, w_in):
    t = h.shape[0]

    def body(h_ref, g_ref, w_ref, pc_ref, qkv_ref, sbg_ref, u_ref):
        x = h_ref[...]
        r = lax.rsqrt(jnp.mean(x * x, axis=-1, keepdims=True) + RMS_EPS)
        u = (x * r * g_ref[...]).astype(BF16)
        u_ref[...] = u
        pc_ref[...] = _dot(u, w_ref[:, 0:1536])
        qkv_ref[...] = _dot(u, w_ref[:, 1536:3072]).astype(BF16)
        sbg_ref[...] = _dot(u, w_ref[:, 3072:3584])

    return pl.pallas_call(
        body, name="inproj_fwd", grid=(t // ROW_BLK,),
        in_specs=[_rows(D_MODEL), _whole((1, D_MODEL)), _whole((D_MODEL, D_IN))],
        out_specs=[_rows(1536), _rows(1536), _rows(D_SB), _rows(D_MODEL)],
        out_shape=[jax.ShapeDtypeStruct((t, 1536), F32), jax.ShapeDtypeStruct((t, 1536), BF16),
                   jax.ShapeDtypeStruct((t, D_SB), F32), jax.ShapeDtypeStruct((t, D_MODEL), BF16)],
        compiler_params=_cparams("parallel"),
    )(h, g_pre, w_in)


def _prev_halo(col):
    per = ROW_BLK // HALO
    return pl.BlockSpec((HALO, D_CONV), lambda i, col=col: (jnp.maximum(i * per - 1, 0), col))


def _fill_glu(buf, i, a_ref, b_ref, ha_ref, hb_ref):
    halo = ha_ref[...] * _sigmoid(hb_ref[...])
    buf[0:HALO, :] = jnp.where(i > 0, halo, 0.0)
    buf[HALO:HALO + ROW_BLK, :] = a_ref[...] * _sigmoid(b_ref[...])


SUBLANES = 8
TAP_ROWS = 64
SHIFT_ROWS = HALO + ROW_BLK - SUBLANES


def _fill_shifts(shifts, buf):
    for b in range(1, SUBLANES):
        shifts[b - 1] = buf[pl.ds(b, SHIFT_ROWS), :]


def _window(buf, shifts, first, rows, lanes):
    whole, part = divmod(first, SUBLANES)
    src = buf if part == 0 else shifts.at[part - 1]
    return src[pl.ds(rows.start + SUBLANES * whole, rows.size), lanes]


TAP_ROW_CHUNKS = [pl.ds(r, TAP_ROWS) for r in range(0, ROW_BLK, TAP_ROWS)]
TAP_LANE_TILES = [pl.ds(c, LANES) for c in range(0, D_CONV, LANES)]


def _layer_norm_stats(cv):
    mu = jnp.mean(cv, axis=-1, keepdims=True)
    xc = cv - mu
    rstd = lax.rsqrt(jnp.mean(xc * xc, axis=-1, keepdims=True) + LN_EPS)
    return xc * rstd, rstd


def _conv_fwd(pc, conv_w, conv_b, ln_g, ln_b, w_pw2, b_pw2):
    t = pc.shape[0]

    def body(a_ref, b_ref, gate_ref, ha_ref, hb_ref, cw_ref, cb_ref, lg_ref, lb_ref, wp_ref, bp_ref,
             cout_ref, cv_ref, p_ref, sl_ref, buf, shifts):
        i = pl.program_id(0)
        _fill_glu(buf, i, a_ref, b_ref, ha_ref, hb_ref)
        _fill_shifts(shifts, buf)
        for lanes in TAP_LANE_TILES:
            for rows in TAP_ROW_CHUNKS:
                acc = jnp.zeros((TAP_ROWS, LANES), F32) + cb_ref[:, lanes]
                for j in range(CONV_WIDTH):
                    acc = acc + cw_ref[j:j + 1, lanes] * _window(buf, shifts, HALO - (CONV_WIDTH - 1) + j, rows, lanes)
                cv_ref[rows, lanes] = acc
        xh, _ = _layer_norm_stats(cv_ref[...])
        ln = xh * lg_ref[...] + lb_ref[...]
        sl = (ln * _sigmoid(ln)).astype(BF16)
        sl_ref[...] = sl
        p = _dot(sl, wp_ref[...]) + bp_ref[...]
        p_ref[...] = p
        gate = gate_ref[...]
        cout_ref[...] = (p * (gate * _sigmoid(gate))).astype(BF16)

    vec = _whole((1, D_CONV))
    return pl.pallas_call(
        body, name="conv_fwd", grid=(t // ROW_BLK,),
        in_specs=[_rows(D_CONV, 0), _rows(D_CONV, 1), _rows(D_CONV, 2), _prev_halo(0), _prev_halo(1),
                  _whole((CONV_WIDTH, D_CONV)), vec, vec, vec, _whole((D_CONV, D_CONV)), vec],
        out_specs=[_rows(D_CONV)] * 4,
        out_shape=[jax.ShapeDtypeStruct((t, D_CONV), BF16), jax.ShapeDtypeStruct((t, D_CONV), F32),
                   jax.ShapeDtypeStruct((t, D_CONV), F32), jax.ShapeDtypeStruct((t, D_CONV), BF16)],
        scratch_shapes=[pltpu.VMEM((HALO + ROW_BLK, D_CONV), F32), pltpu.VMEM((SUBLANES - 1, SHIFT_ROWS, D_CONV), F32)],
        compiler_params=_cparams("parallel"),
    )(pc, pc, pc, pc, pc, conv_w, conv_b, ln_g, ln_b, w_pw2, b_pw2)


def _lower_triangle():
    row = lax.broadcasted_iota(jnp.int32, (ROW_BLK, ROW_BLK), 0)
    col = lax.broadcasted_iota(jnp.int32, (ROW_BLK, ROW_BLK), 1)
    return row > col


def _lower_triangle_t():
    row = lax.broadcasted_iota(jnp.int32, (ROW_BLK, ROW_BLK), 0)
    col = lax.broadcasted_iota(jnp.int32, (ROW_BLK, ROW_BLK), 1)
    return row < col


def _tri_sum(x, umat):
    return _dot(x.astype(BF16), umat)


def _log_gates(z):
    ls = -(jnp.maximum(z, 0.0) + jnp.log(1.0 + jnp.exp(-jnp.abs(z))))
    return ls, z + ls


def _head_lanes(hh):
    lane = lax.broadcasted_iota(jnp.int32, (ROW_BLK, HEAD_BLK), 1)
    return (lane >= HEAD_DIM * hh) & (lane < HEAD_DIM * (hh + 1))


def _merge_heads(acc_ref):
    out = acc_ref[HEADS_PER_BLOCK - 1]
    for hh in range(HEADS_PER_BLOCK - 1):
        out = jnp.where(_head_lanes(hh), acc_ref[hh], out)
    return out


def _qkv_specs(t):
    n_blk = D_SB // HEAD_BLK
    return [pl.BlockSpec((ROW_BLK, HEAD_BLK), lambda hp, i: (i, hp)),
            pl.BlockSpec((t, HEAD_BLK), lambda hp, i: (0, n_blk + hp)),
            pl.BlockSpec((t, HEAD_BLK), lambda hp, i: (0, 2 * n_blk + hp))]


def _carry_spec():
    return pl.BlockSpec((HEADS_PER_BLOCK, ROW_BLK, LANES), lambda hp, i: (hp, i, 0))


def _last_block_rows(t, n_tokens):
    packed_rows = 16
    return -(-(n_tokens - (t - ROW_BLK)) // packed_rows) * packed_rows


def _by_block_rows(i, last_rows, sweep):
    if last_rows == ROW_BLK:
        sweep(ROW_BLK)
        return
    last = pl.num_programs(1) - 1
    pl.when(i < last)(lambda: sweep(ROW_BLK))
    pl.when(i == last)(lambda: sweep(last_rows))


def _attn_fwd(qkv, n_tokens):
    t = qkv.shape[0]
    assert t // ROW_BLK <= LANES

    def body(q_ref, k_ref, v_ref, o_ref, c_ref, acc_ref, run_ref, qm_ref, z_ref):
        i = pl.program_id(1)
        q = q_ref[...]
        heads = range(HEADS_PER_BLOCK)
        for hh in heads:
            qm_ref[hh] = jnp.where(_head_lanes(hh), q, jnp.zeros_like(q)) * jnp.asarray(SB_SCALE, BF16)
        acc_ref[...] = jnp.zeros_like(acc_ref)
        c_ref[...] = jnp.zeros_like(c_ref)
        run_ref[...] = jnp.zeros_like(run_ref)

        def sweep(n_rows):
            rows = pl.ds(0, n_rows)
            lower = _lower_triangle()[:n_rows]
            umat = jnp.where(_lower_triangle(), 1.0, 0.0).astype(BF16)
            lane = lax.broadcasted_iota(jnp.int32, (n_rows, LANES), 1)

            def scores(jb):
                start = pl.multiple_of(jb * ROW_BLK, ROW_BLK)
                kb = k_ref[pl.ds(start, ROW_BLK), :]
                for hh in heads:
                    z_ref[hh, rows] = _dot_nt(qm_ref[hh, rows], kb)

            def block(jb, diagonal):
                start = pl.multiple_of(jb * ROW_BLK, ROW_BLK)
                vb = v_ref[pl.ds(start, ROW_BLK), :]
                logits = []
                for hh in heads:
                    ls, lb = _log_gates(z_ref[hh, rows])
                    if diagonal:
                        ls = jnp.where(lower, ls, 0.0)
                    run = run_ref[hh, rows]
                    if not diagonal:
                        c_ref[hh, rows] = jnp.where(lane == jb, run, c_ref[hh, rows])
                    logits.append(lb + jnp.concatenate([run, run], axis=1) + _tri_sum(ls, umat))
                    run_ref[hh, rows] = run + jnp.sum(ls, axis=1, keepdims=True)
                scores(jnp.maximum(jb - 1, 0))
                for hh in heads:
                    a = jnp.exp(logits[hh])
                    if diagonal:
                        a = jnp.where(lower, a, 0.0)
                    acc_ref[hh, rows] += _dot(a.astype(BF16), vb)

            scores(i)
            block(i, True)

            @pl.loop(0, i)
            def _(n):
                block(i - 1 - n, False)

        _by_block_rows(i, _last_block_rows(t, n_tokens), sweep)
        o_ref[...] = _merge_heads(acc_ref)

    per_head = (HEADS_PER_BLOCK, ROW_BLK, HEAD_BLK)
    return pl.pallas_call(
        body, name="attn_fwd", grid=(D_SB // HEAD_BLK, t // ROW_BLK),
        in_specs=_qkv_specs(t),
        out_specs=[pl.BlockSpec((ROW_BLK, HEAD_BLK), lambda hp, i: (i, hp)), _carry_spec()],
        out_shape=[jax.ShapeDtypeStruct((t, D_SB), F32),
                   jax.ShapeDtypeStruct((D_SB // HEAD_DIM, t, LANES), F32)],
        scratch_shapes=[pltpu.VMEM(per_head, F32), pltpu.VMEM((HEADS_PER_BLOCK, ROW_BLK, LANES), F32),
                        pltpu.VMEM(per_head, BF16), pltpu.VMEM((HEADS_PER_BLOCK, ROW_BLK, ROW_BLK), F32)],
        compiler_params=_cparams("arbitrary", "arbitrary"),
    )(qkv, qkv, qkv)


def _outproj_fwd(h, cout, sraw, sbg, w_out, g_post):
    t = h.shape[0]

    def body(h_ref, c_ref, s_ref, g_ref, w_ref, gp_ref, hn_ref, mixed_ref, mix_ref):
        gate = g_ref[...]
        mix_ref[:, 0:D_CONV] = c_ref[...]
        mix_ref[:, D_CONV:] = (s_ref[...] * (gate * _sigmoid(gate))).astype(BF16)
        mixed = _dot(mix_ref[...], w_ref[...])
        mixed_ref[...] = mixed
        r = lax.rsqrt(jnp.mean(mixed * mixed, axis=-1, keepdims=True) + RMS_EPS)
        hn_ref[...] = h_ref[...] + mixed * r * gp_ref[...]

    return pl.pallas_call(
        body, name="outproj_fwd", grid=(t // ROW_BLK,),
        in_specs=[_rows(D_MODEL), _rows(D_CONV), _rows(D_SB), _rows(D_SB), _whole((D_MODEL, D_MODEL)),
                  _whole((1, D_MODEL))],
        out_specs=[_rows(D_MODEL)] * 3,
        out_shape=[jax.ShapeDtypeStruct((t, D_MODEL), F32), jax.ShapeDtypeStruct((t, D_MODEL), F32),
                   jax.ShapeDtypeStruct((t, D_MODEL), BF16)],
        compiler_params=_cparams("parallel"),
    )(h, cout, sraw, sbg, w_out, g_post)


def _loss_and_grad(h, target, seq):
    t = h.shape[0]

    def body(h_ref, t_ref, loss_ref, dh_ref):
        i = pl.program_id(0)

        @pl.when(i == 0)
        def _():
            loss_ref[...] = jnp.zeros_like(loss_ref)

        row = i * ROW_BLK + lax.broadcasted_iota(jnp.int32, (ROW_BLK, D_MODEL), 0)
        real = (row >= N_META) & (row < N_META + seq)
        diff = jnp.where(real, h_ref[...] - t_ref[...], 0.0)
        sq = jnp.sum(jnp.sum(diff * diff, axis=1, keepdims=True), axis=0, keepdims=True)
        loss_ref[...] += (0.5 / D_MODEL) * sq
        dh_ref[...] = diff * (1.0 / D_MODEL)

    return pl.pallas_call(
        body, name="loss", grid=(t // ROW_BLK,),
        in_specs=[_rows(D_MODEL), _rows(D_MODEL)],
        out_specs=[_whole((1, 1)), _rows(D_MODEL)],
        out_shape=[jax.ShapeDtypeStruct((1, 1), F32), jax.ShapeDtypeStruct((t, D_MODEL), F32)],
        compiler_params=_cparams("arbitrary"),
    )(h, target)


def _outproj_bwd(dh, mixed, g_post, sraw, sbg, w_out_t):
    t = dh.shape[0]

    def body(dh_ref, mixed_ref, gp_ref, s_ref, g_ref, wt_ref, dc_ref, ds_ref, dg_ref, dmb_ref, dgp_ref):
        @pl.when(pl.program_id(0) == 0)
        def _():
            dgp_ref[...] = jnp.zeros_like(dgp_ref)

        mixed = mixed_ref[...]
        r = lax.rsqrt(jnp.mean(mixed * mixed, axis=-1, keepdims=True) + RMS_EPS)
        nh = mixed * r
        dy = dh_ref[...]
        dgp_ref[...] += jnp.sum(dy * nh, axis=0, keepdims=True)
        dn = dy * gp_ref[...]
        dmixed = (r * (dn - nh * jnp.mean(dn * nh, axis=-1, keepdims=True))).astype(BF16)
        dmb_ref[...] = dmixed
        dmix = _dot(dmixed, wt_ref[...])
        dc_ref[...] = dmix[:, 0:D_CONV]
        dsg = dmix[:, D_CONV:]
        gate = g_ref[...]
        sg = _sigmoid(gate)
        ds_ref[...] = dsg * (gate * sg)
        dg_ref[...] = dsg * s_ref[...] * _dsilu(gate, sg)

    return pl.pallas_call(
        body, name="outproj_bwd", grid=(t // ROW_BLK,),
        in_specs=[_rows(D_MODEL), _rows(D_MODEL), _whole((1, D_MODEL)), _rows(D_SB), _rows(D_SB),
                  _whole((D_MODEL, D_MODEL))],
        out_specs=[_rows(D_CONV), _rows(D_SB), _rows(D_SB), _rows(D_MODEL), _whole((1, D_MODEL))],
        out_shape=[jax.ShapeDtypeStruct((t, D_CONV), F32), jax.ShapeDtypeStruct((t, D_SB), F32),
                   jax.ShapeDtypeStruct((t, D_SB), F32), jax.ShapeDtypeStruct((t, D_MODEL), BF16),
                   jax.ShapeDtypeStruct((1, D_MODEL), F32)],
        compiler_params=_cparams("arbitrary"),
    )(dh, mixed, g_post, sraw, sbg, w_out_t)


def _attn_bwd(qkv, carries, do, n_tokens):
    t = qkv.shape[0]

    def body(q_ref, k_ref, v_ref, c_ref, do_ref, dq_ref, dk_ref, dv_ref, acc_ref, seen_ref, qm_ref, dom_ref, z_ref,
             da_ref, dz_ref, a_ref):
        i = pl.program_id(1)

        @pl.when(i == 0)
        def _():
            dk_ref[...] = jnp.zeros_like(dk_ref)
            dv_ref[...] = jnp.zeros_like(dv_ref)

        q = q_ref[...]
        dof = do_ref[...]
        heads = range(HEADS_PER_BLOCK)
        for hh in heads:
            qm_ref[hh] = jnp.where(_head_lanes(hh), q, jnp.zeros_like(q)) * jnp.asarray(SB_SCALE, BF16)
            dom_ref[hh] = jnp.where(_head_lanes(hh), dof, 0.0).astype(BF16)
        acc_ref[...] = jnp.zeros_like(acc_ref)
        seen_ref[...] = jnp.zeros_like(seen_ref)

        def sweep(n_rows):
            rows = pl.ds(0, n_rows)
            lower = _lower_triangle()[:n_rows]
            umat = jnp.where(_lower_triangle(), 1.0, 0.0).astype(BF16)
            umat_t = jnp.where(_lower_triangle_t(), 1.0, 0.0).astype(BF16)
            lane = lax.broadcasted_iota(jnp.int32, (n_rows, LANES), 1)

            def scores(jb):
                start = pl.multiple_of(jb * ROW_BLK, ROW_BLK)
                kb = k_ref[pl.ds(start, ROW_BLK), :]
                for hh in heads:
                    z_ref[hh, rows] = _dot_nt(qm_ref[hh, rows], kb)

            def value_grads(jb):
                start = pl.multiple_of(jb * ROW_BLK, ROW_BLK)
                vb = v_ref[pl.ds(start, ROW_BLK), :]
                for hh in heads:
                    da_ref[hh, rows] = _dot_nt(dom_ref[hh, rows], vb)

            def products(jb, hh):
                start = pl.multiple_of(jb * ROW_BLK, ROW_BLK)
                dzb = dz_ref[hh, rows]
                acc_ref[hh, rows] += _dot(dzb, k_ref[pl.ds(start, ROW_BLK), :])
                dk_ref[pl.ds(start, ROW_BLK), :] += _dot_tn(dzb, qm_ref[hh, rows])
                dv_ref[pl.ds(start, ROW_BLK), :] += _dot_tn(a_ref[hh, rows], dom_ref[hh, rows])

            def block(jb, diagonal):
                before = jnp.maximum(jb - 1, 0)
                lbs, logits = [], []
                for hh in heads:
                    products(before, hh)
                    ls, lb = _log_gates(z_ref[hh, rows])
                    if diagonal:
                        ls = jnp.where(lower, ls, 0.0)
                        logits.append(lb + _tri_sum(ls, umat))
                    else:
                        right = jnp.sum(jnp.where(lane == jb, c_ref[hh, rows], 0.0), axis=1, keepdims=True)
                        logits.append(lb + right + _tri_sum(ls, umat))
                    lbs.append(lb)
                if not diagonal:
                    scores(jb + 1)
                gs, befores = [], []
                for hh in heads:
                    a = jnp.exp(logits[hh])
                    if diagonal:
                        a = jnp.where(lower, a, 0.0)
                    g = da_ref[hh, rows] * a
                    seen = seen_ref[hh, rows]
                    befores.append(jnp.concatenate([seen, seen], axis=1) + _tri_sum(g, umat_t))
                    seen_ref[hh, rows] = seen + jnp.sum(g, axis=1, keepdims=True)
                    a_ref[hh, rows] = a.astype(BF16)
                    gs.append(g)
                if not diagonal:
                    value_grads(jb + 1)
                for hh in heads:
                    dz = gs[hh] - jnp.exp(lbs[hh]) * (gs[hh] + befores[hh])
                    if diagonal:
                        dz = jnp.where(lower, dz, 0.0)
                    dz_ref[hh, rows] = dz.astype(BF16)

            dz_ref[...] = jnp.zeros_like(dz_ref)
            a_ref[...] = jnp.zeros_like(a_ref)
            scores(0)
            value_grads(0)

            @pl.loop(0, i)
            def _(jb):
                block(jb, False)

            block(i, True)
            for hh in heads:
                products(i, hh)

        _by_block_rows(i, _last_block_rows(t, n_tokens), sweep)
        dq_ref[...] = _merge_heads(acc_ref) * SB_SCALE

    blk = pl.BlockSpec((ROW_BLK, HEAD_BLK), lambda hp, i: (i, hp))
    full = pl.BlockSpec((t, HEAD_BLK), lambda hp, i: (0, hp))
    per_head = (HEADS_PER_BLOCK, ROW_BLK, HEAD_BLK)
    return pl.pallas_call(
        body, name="attn_bwd", grid=(D_SB // HEAD_BLK, t // ROW_BLK),
        in_specs=_qkv_specs(t) + [_carry_spec(), blk],
        out_specs=[blk, full, full],
        out_shape=[jax.ShapeDtypeStruct((t, D_SB), F32)] * 3,
        scratch_shapes=[pltpu.VMEM(per_head, F32), pltpu.VMEM((HEADS_PER_BLOCK, ROW_BLK, LANES), F32),
                        pltpu.VMEM(per_head, BF16), pltpu.VMEM(per_head, BF16)]
                       + [pltpu.VMEM((HEADS_PER_BLOCK, ROW_BLK, ROW_BLK), dtype) for dtype in (F32, F32, BF16, BF16)],
        compiler_params=_cparams("arbitrary", "arbitrary"),
    )(qkv, qkv, qkv, carries, do)


def _conv_bwd_rows(dcout, pc, cv, p, ln_g, ln_b, w_pw2_t):
    t = dcout.shape[0]

    def body(dc_ref, gate_ref, cv_ref, p_ref, lg_ref, lb_ref, wt_ref, dcv_ref, dgate_ref, dpb_ref, vec_ref):
        @pl.when(pl.program_id(0) == 0)
        def _():
            vec_ref[...] = jnp.zeros_like(vec_ref)

        dc = dc_ref[...]
        gate = gate_ref[...]
        sg = _sigmoid(gate)
        dp = dc * (gate * sg)
        dgate_ref[...] = dc * p_ref[...] * _dsilu(gate, sg)
        dpb = dp.astype(BF16)
        dpb_ref[...] = dpb
        xh, rstd = _layer_norm_stats(cv_ref[...])
        ln = xh * lg_ref[...] + lb_ref[...]
        s2 = _sigmoid(ln)
        dln = _dot(dpb, wt_ref[...]) * _dsilu(ln, s2)
        dxh = dln * lg_ref[...]
        dcv = rstd * (dxh - jnp.mean(dxh, axis=-1, keepdims=True)
                      - xh * jnp.mean(dxh * xh, axis=-1, keepdims=True))
        dcv_ref[...] = dcv
        vec_ref[0:1, :] += jnp.sum(dp, axis=0, keepdims=True)
        vec_ref[1:2, :] += jnp.sum(dln * xh, axis=0, keepdims=True)
        vec_ref[2:3, :] += jnp.sum(dln, axis=0, keepdims=True)
        vec_ref[3:4, :] += jnp.sum(dcv, axis=0, keepdims=True)

    vec = _whole((1, D_CONV))
    return pl.pallas_call(
        body, name="conv_bwd_rows", grid=(t // ROW_BLK,),
        in_specs=[_rows(D_CONV), _rows(D_CONV, 2), _rows(D_CONV), _rows(D_CONV), vec, vec,
                  _whole((D_CONV, D_CONV))],
        out_specs=[_rows(D_CONV), _rows(D_CONV), _rows(D_CONV), _whole((8, D_CONV))],
        out_shape=[jax.ShapeDtypeStruct((t, D_CONV), F32), jax.ShapeDtypeStruct((t, D_CONV), F32),
                   jax.ShapeDtypeStruct((t, D_CONV), BF16), jax.ShapeDtypeStruct((8, D_CONV), F32)],
        compiler_params=_cparams("arbitrary"),
    )(dcout, pc, cv, p, ln_g, ln_b, w_pw2_t)


def _conv_bwd_taps(dcv, pc, conv_w):
    t = dcv.shape[0]
    n_halo = t // HALO
    per = ROW_BLK // HALO

    def body(d_ref, dn_ref, a_ref, b_ref, ha_ref, hb_ref, cw_ref, da_ref, db_ref, dw_ref, cbuf, dbuf, cshifts, dshifts):
        i = pl.program_id(0)

        @pl.when(i == 0)
        def _():
            dw_ref[...] = jnp.zeros_like(dw_ref)

        _fill_glu(cbuf, i, a_ref, b_ref, ha_ref, hb_ref)
        dbuf[0:ROW_BLK, :] = d_ref[...]
        dbuf[ROW_BLK:ROW_BLK + HALO, :] = jnp.where(i < pl.num_programs(0) - 1, dn_ref[...], 0.0)
        _fill_shifts(cshifts, cbuf)
        _fill_shifts(dshifts, dbuf)
        for lanes in TAP_LANE_TILES:
            for rows in TAP_ROW_CHUNKS:
                acc = jnp.zeros((TAP_ROWS, LANES), F32)
                for j in range(CONV_WIDTH):
                    acc = acc + cw_ref[j:j + 1, lanes] * _window(dbuf, dshifts, CONV_WIDTH - 1 - j, rows, lanes)
                sb = _sigmoid(b_ref[rows, lanes])
                da_ref[rows, lanes] = acc * sb
                db_ref[rows, lanes] = acc * a_ref[rows, lanes] * sb * (1.0 - sb)
            for j in range(CONV_WIDTH):
                acc = jnp.zeros((TAP_ROWS, LANES), F32)
                for rows in TAP_ROW_CHUNKS:
                    acc = acc + d_ref[rows, lanes] * _window(cbuf, cshifts, HALO - (CONV_WIDTH - 1) + j, rows, lanes)
                dw_ref[j:j + 1, lanes] += jnp.sum(acc, axis=0, keepdims=True)

    return pl.pallas_call(
        body, name="conv_bwd_taps", grid=(t // ROW_BLK,),
        in_specs=[_rows(D_CONV),
                  pl.BlockSpec((HALO, D_CONV), lambda i: (jnp.minimum((i + 1) * per, n_halo - 1), 0)),
                  _rows(D_CONV, 0), _rows(D_CONV, 1), _prev_halo(0), _prev_halo(1),
                  _whole((CONV_WIDTH, D_CONV))],
        out_specs=[_rows(D_CONV), _rows(D_CONV), _whole((32, D_CONV))],
        out_shape=[jax.ShapeDtypeStruct((t, D_CONV), F32), jax.ShapeDtypeStruct((t, D_CONV), F32),
                   jax.ShapeDtypeStruct((32, D_CONV), F32)],
        scratch_shapes=[pltpu.VMEM((HALO + ROW_BLK, D_CONV), F32), pltpu.VMEM((ROW_BLK + HALO, D_CONV), F32),
                        pltpu.VMEM((SUBLANES - 1, SHIFT_ROWS, D_CONV), F32),
                        pltpu.VMEM((SUBLANES - 1, SHIFT_ROWS, D_CONV), F32)],
        compiler_params=_cparams("arbitrary"),
    )(dcv, dcv, pc, pc, pc, pc, conv_w)


def _inproj_bwd(dh_out, h, g_pre, pieces, w_in_t):
    t = h.shape[0]

    def body(dh_ref, h_ref, g_ref, *rest):
        piece_refs, (wt_ref, dhin_ref, dproj_ref, dg_ref) = rest[:7], rest[7:]

        @pl.when(pl.program_id(0) == 0)
        def _():
            dg_ref[...] = jnp.zeros_like(dg_ref)

        for k, ref in enumerate(piece_refs):
            dproj_ref[:, 512 * k:512 * (k + 1)] = ref[...].astype(BF16)
        du = _dot(dproj_ref[...], wt_ref[...])
        x = h_ref[...]
        r = lax.rsqrt(jnp.mean(x * x, axis=-1, keepdims=True) + RMS_EPS)
        xh = x * r
        dg_ref[...] += jnp.sum(du * xh, axis=0, keepdims=True)
        dxh = du * g_ref[...]
        dhin_ref[...] = dh_ref[...] + r * (dxh - xh * jnp.mean(dxh * xh, axis=-1, keepdims=True))

    return pl.pallas_call(
        body, name="inproj_bwd", grid=(t // ROW_BLK,),
        in_specs=[_rows(D_MODEL), _rows(D_MODEL), _whole((1, D_MODEL))] + [_rows(512)] * 7
                 + [_whole((D_IN, D_MODEL))],
        out_specs=[_rows(D_MODEL), _rows(D_IN), _whole((1, D_MODEL))],
        out_shape=[jax.ShapeDtypeStruct((t, D_MODEL), F32), jax.ShapeDtypeStruct((t, D_IN), BF16),
                   jax.ShapeDtypeStruct((1, D_MODEL), F32)],
        compiler_params=_cparams("arbitrary"),
    )(dh_out, h, g_pre, *pieces, w_in_t)


def _weight_grad(xb, dyb, name):
    t, k = xb.shape
    n = dyb.shape[1]
    tn = n

    def body(x_ref, dy_ref, o_ref):
        @pl.when(pl.program_id(1) == 0)
        def _():
            o_ref[...] = jnp.zeros_like(o_ref)

        o_ref[...] += _dot_tn(x_ref[...], dy_ref[...])

    return pl.pallas_call(
        body, name=name, grid=(n // tn, t // ROW_BLK),
        in_specs=[pl.BlockSpec((ROW_BLK, k), lambda j, i: (i, 0)), pl.BlockSpec((ROW_BLK, tn), lambda j, i: (i, j))],
        out_specs=pl.BlockSpec((k, tn), lambda j, i: (0, j)),
        out_shape=jax.ShapeDtypeStruct((k, n), F32),
        compiler_params=_cparams("parallel", "arbitrary"),
    )(xb, dyb)


def _position():
    return lax.axis_index("x"), lax.axis_index("y"), lax.axis_index("c")


def _comm_call(body, name, ins, out_shapes):
    n = len(ins)
    hbm = pl.BlockSpec(memory_space=pltpu.HBM)
    return pl.pallas_call(
        functools.partial(body, n), name=name, in_specs=[hbm] * n, out_specs=[hbm] * n, out_shape=out_shapes,
        scratch_shapes=[pltpu.SemaphoreType.DMA((n, N_DEV - 1)), pltpu.SemaphoreType.DMA((n, N_DEV - 1)),
                        pltpu.SemaphoreType.DMA((n,))],
    )(*ins)


def _all_gather(blocks, name):
    def body(n, *refs):
        x_refs, out_refs, (send_sems, recv_sems, local_sems) = refs[:n], refs[n:2 * n], refs[2 * n:]
        x, y, c = _position()
        me, sibling = (x, y, c), (x, y, 1 - c)
        chips = [(1 - x, y), (x, 1 - y), (1 - x, 1 - y)]

        def slot(a, px, py, pc):
            return out_refs[a].at[4 * px + 2 * py + pc]

        def copy(a, k, origin, to, own=False):
            return pltpu.make_async_remote_copy(
                src_ref=x_refs[a] if own else slot(a, *origin), dst_ref=slot(a, *origin),
                send_sem=send_sems.at[a, k], recv_sem=recv_sems.at[a, k], device_id=to, device_id_type=MESH)

        arrays = range(n)
        mine = [pltpu.make_async_copy(x_refs[a], slot(a, *me), local_sems.at[a]) for a in arrays]
        first = [copy(a, 1 + j, me, (*chip, c), own=True) for j, chip in enumerate(chips) for a in arrays]
        first += [copy(a, 0, me, sibling, own=True) for a in arrays]
        for cp in mine + first:
            cp.start()
        passed = []
        for j, chip in enumerate(chips):
            for a in arrays:
                copy(a, 1 + j, (*chip, c), me).wait_recv()
                passed.append(copy(a, 4 + j, (*chip, c), sibling))
                passed[-1].start()
        for a in arrays:
            copy(a, 0, sibling, me).wait_recv()
            for j, chip in enumerate(chips):
                copy(a, 4 + j, (*chip, 1 - c), me).wait_recv()
        for cp in first + passed:
            cp.wait_send()
        for cp in mine:
            cp.wait()

    return _comm_call(body, name, blocks, [jax.ShapeDtypeStruct((N_DEV,) + b.shape, b.dtype) for b in blocks])


def _exchange_copies(g_refs, land_refs, sems, gather):
    x, y, c = _position()
    me = 4 * x + 2 * y + c
    out = []
    for g_ref, land_ref, (send_sem, recv_sem, local_sem) in zip(g_refs, land_refs, sems):
        def mine(slot, g_ref=g_ref):
            return g_ref if gather else g_ref.at[slot]

        def remote(src, dst, dev):
            return pltpu.make_async_remote_copy(src_ref=src, dst_ref=dst, send_sem=send_sem, recv_sem=recv_sem,
                                                device_id=dev, device_id_type=MESH)

        sends = []
        for k in range(1, N_DEV):
            px = 1 - x if k & 4 else x
            py = 1 - y if k & 2 else y
            pc = 1 - c if k & 1 else c
            sends.append(remote(mine(4 * px + 2 * py + pc), land_ref.at[me], (px, py, pc)))
        seven = land_ref.at[pl.ds(0, N_DEV - 1)]
        out.append((pltpu.make_async_copy(mine(me), land_ref.at[me], local_sem), sends, remote(seven, seven, (x, y, c))))
    return out


_HBM = pl.BlockSpec(memory_space=pltpu.HBM)
_SEM = pl.BlockSpec(memory_space=pltpu.SEMAPHORE)
_ORDERED = pltpu.CompilerParams(has_side_effects=pltpu.SideEffectType.DATAFLOW_SIDE_EFFECTING)
SEMS_PER_ARRAY = 3


def _exchange_start(arrays, after, name, gather):
    n = len(arrays)
    n_sems = SEMS_PER_ARRAY * n

    def body(*refs):
        g_refs, land_refs, sems, token = refs[:n], refs[n:2 * n], refs[2 * n + 1:2 * n + 1 + n_sems], refs[-1]
        sems = [sems[SEMS_PER_ARRAY * a:SEMS_PER_ARRAY * (a + 1)] for a in range(n)]
        for local, sends, _ in _exchange_copies(g_refs, land_refs, sems, gather):
            local.start()
            for cp in sends:
                cp.start()
        token[...] = jnp.zeros_like(token)

    buffers = list(arrays) + [lax.empty((N_DEV,) + g.shape if gather else g.shape, g.dtype) for g in arrays]
    outs = pl.pallas_call(
        body, name=name, in_specs=[_HBM] * (2 * n) + [pl.BlockSpec(memory_space=pl.ANY)],
        out_specs=[_SEM] * n_sems + [_HBM] * (2 * n) + [pl.BlockSpec(memory_space=pltpu.VMEM)],
        out_shape=[pltpu.SemaphoreType.DMA(())] * n_sems + [pltpu.HBM(b.shape, b.dtype) for b in buffers]
                  + [jax.ShapeDtypeStruct((8, LANES), F32)],
        input_output_aliases={a: n_sems + a for a in range(2 * n)}, compiler_params=_ORDERED,
    )(*[pltpu.with_memory_space_constraint(b, pltpu.HBM) for b in buffers], after)
    return outs[:n_sems], outs[n_sems:n_sems + n], outs[n_sems + n:n_sems + 2 * n], outs[-1]


def _exchange_wait(sems, arrays, landings, after, name, gather):
    n = len(arrays)
    n_sems = SEMS_PER_ARRAY * n

    def body(*refs):
        g_refs, land_refs, sems = refs[:n], refs[n:2 * n], refs[2 * n:2 * n + n_sems]
        sems = [sems[SEMS_PER_ARRAY * a:SEMS_PER_ARRAY * (a + 1)] for a in range(n)]
        for local, _, all_seven in _exchange_copies(g_refs, land_refs, sems, gather):
            all_seven.wait_recv()
            all_seven.wait_send()
            local.wait()

    buffers = list(arrays) + list(landings)
    outs = pl.pallas_call(
        body, name=name, in_specs=[_HBM] * (2 * n) + [_SEM] * n_sems + [pl.BlockSpec(memory_space=pl.ANY)],
        out_specs=[_HBM] * (2 * n), out_shape=[pltpu.HBM(b.shape, b.dtype) for b in buffers],
        input_output_aliases={a: a for a in range(2 * n)}, compiler_params=_ORDERED,
    )(*buffers, *sems, after)
    return outs[n:]


def _block_rows(r, row_bytes, budget=1 << 20):
    cap = max(8, budget // row_bytes)
    return max(d for d in range(8, min(r, cap) + 1, 8) if r % d == 0)


def _sum_adamw(parts, w, m, v, name):
    n_parts, r, c = parts.shape
    br = _block_rows(r, 4 * c)

    def body(p_ref, w_ref, m_ref, v_ref, g_out, d_out, m_out, v_out):
        g = p_ref[0].astype(F32)
        for s in range(1, n_parts):
            g = g + p_ref[s].astype(F32)
        m_new = ADAM_B1 * m_ref[...] + (1.0 - ADAM_B1) * g
        v_new = ADAM_B2 * v_ref[...] + (1.0 - ADAM_B2) * (g * g)
        m_hat = m_new / (1.0 - ADAM_B1 ** ADAM_STEP)
        v_hat = v_new / (1.0 - ADAM_B2 ** ADAM_STEP)
        g_out[...] = g
        d_out[...] = -ADAM_LR * (m_hat / (jnp.sqrt(v_hat) + ADAM_EPS) + ADAM_WD * w_ref[...])
        m_out[...] = m_new
        v_out[...] = v_new

    row = pl.BlockSpec((br, c), lambda i: (i, 0))
    return pl.pallas_call(
        body, name=name, grid=(r // br,),
        in_specs=[pl.BlockSpec((n_parts, br, c), lambda i: (0, i, 0)), row, row, row],
        out_specs=[row] * 4, out_shape=[jax.ShapeDtypeStruct((r, c), F32)] * 4,
        compiler_params=_cparams("parallel"),
    )(parts, w, m, v)


def _sum_parts(parts, name):
    n_parts, r, c = parts.shape

    def body(p_ref, o_ref):
        g = p_ref[0]
        for s in range(1, n_parts):
            g = g + p_ref[s]
        o_ref[...] = g

    return pl.pallas_call(
        body, name=name, in_specs=[pl.BlockSpec(memory_space=pltpu.VMEM)],
        out_specs=pl.BlockSpec(memory_space=pltpu.VMEM), out_shape=jax.ShapeDtypeStruct((r, c), F32),
    )(parts)


def _pack(arrays):
    flat = jnp.concatenate([a.reshape(-1) for a in arrays])
    pad = -flat.shape[0] % (8 * LANES)
    if pad:
        flat = jnp.pad(flat, (0, pad))
    return flat.reshape(-1, LANES)


def _unpack(buf, shapes):
    flat = buf.reshape(-1)
    out, at = [], 0
    for shape in shapes:
        size = 1
        for d in shape:
            size *= d
        out.append(lax.slice_in_dim(flat, at, at + size).reshape(shape))
        at += size
    return out


def _local_step(x, target, meta, pre_g, post_g, conv_w, conv_b, ln_g, ln_b, b_pw2, weights, ship):
    depth = pre_g.shape[0]
    seq = x.shape[0]
    t = -(-(N_META + seq) // ROW_BLK) * ROW_BLK
    tail = t - N_META - seq
    h = jnp.concatenate([meta, x, jnp.zeros((tail, D_MODEL), F32)], axis=0)
    target = jnp.pad(target, ((N_META, tail), (0, 0)))
    row = lambda a, l: a[l][None, :]

    saved = []
    for l in range(depth):
        w_in, w_pw2, w_out = weights(l, h)
        pc, qkv, sbg, u = _inproj_fwd(h, row(pre_g, l), w_in)
        cout, cv, p, sl = _conv_fwd(pc, conv_w[l], row(conv_b, l), row(ln_g, l), row(ln_b, l), w_pw2, row(b_pw2, l))
        sraw, carries = _attn_fwd(qkv, N_META + seq)
        h_new, mixed, mix = _outproj_fwd(h, cout, sraw, sbg, w_out, row(post_g, l))
        saved.append((h, pc, qkv, sbg, u, cv, p, sl, sraw, carries, mixed, mix, w_in.T, w_pw2.T, w_out.T))
        h = h_new

    loss, dh = _loss_and_grad(h, target, seq)

    grads = {k: [None] * depth for k in ("pre_g", "post_g", "conv_w", "conv_b", "ln_g", "ln_b", "b_pw2")}
    token = jnp.zeros((8, LANES), F32)
    for l in reversed(range(depth)):
        h_in, pc, qkv, sbg, u, cv, p, sl, sraw, carries, mixed, mix, w_in_t, w_pw2_t, w_out_t = saved[l]
        dcout, dsraw, dsbg, dmixed, dg_post = _outproj_bwd(dh, mixed, row(post_g, l) + token[:1, :1], sraw, sbg, w_out_t)
        dq, dk, dv = _attn_bwd(qkv, carries, dsraw, N_META + seq)
        dcv, dgate, dpb, vecs = _conv_bwd_rows(dcout, pc, cv, p, row(ln_g, l), row(ln_b, l), w_pw2_t)
        da, db, dconv_w = _conv_bwd_taps(dcv, pc, conv_w[l])
        dh, dproj, dg_pre = _inproj_bwd(dh, h_in, row(pre_g, l), (da, db, dgate, dq, dk, dv, dsbg), w_in_t)
        token = ship(l, dh, _weight_grad(u, dproj, "w_in_grad"), _weight_grad(sl, dpb, "w_pw2_grad"),
                     _weight_grad(mix, dmixed, "w_out_grad"))
        grads["pre_g"][l] = dg_pre[0]
        grads["post_g"][l] = dg_post[0]
        grads["b_pw2"][l], grads["ln_g"][l], grads["ln_b"][l], grads["conv_b"][l] = vecs[0], vecs[1], vecs[2], vecs[3]
        grads["conv_w"][l] = dconv_w[:CONV_WIDTH]

    grads = {k: jnp.stack(v) for k, v in grads.items()}
    grads["meta"] = dh[:N_META]
    return loss[0, 0], dh[N_META:N_META + seq], grads


def _shard_major(full, axis):
    shape = full.shape
    split = full.reshape(shape[:axis] + (N_DEV, shape[axis] // N_DEV) + shape[axis + 1:])
    return jnp.moveaxis(split, axis, 0)


def _whole_from_shards(shards, axis):
    moved = jnp.moveaxis(shards, 0, axis)
    shape = moved.shape
    return moved.reshape(shape[:axis] + (shape[axis] * shape[axis + 1],) + shape[axis + 2:])


def kernel(x, meta_tokens, pre_norm_g, post_norm_g, w_in, conv_w, conv_b, conv_ln_g, conv_ln_b, w_pw2, b_pw2, w_out, loss_target, m_meta_tokens, m_pre_norm_g, m_post_norm_g, m_w_in, m_conv_w, m_conv_b, m_conv_ln_g, m_conv_ln_b, m_w_pw2, m_b_pw2, m_w_out, v_meta_tokens, v_pre_norm_g, v_post_norm_g, v_w_in, v_conv_w, v_conv_b, v_conv_ln_g, v_conv_ln_b, v_w_pw2, v_b_pw2, v_w_out):
    me = 4 * lax.axis_index("x") + 2 * lax.axis_index("y") + lax.axis_index("c")

    depth = w_in.shape[0]
    big = [w.astype(BF16) for w in (w_in, w_pw2, w_out)]
    *first, conv_w_s, meta_s = _all_gather([w[0] for w in big] + [conv_w, meta_tokens], "gather_first_layer")
    *gathering, token = _exchange_start([w[1:] for w in big], meta_s, "gather_start", gather=True)
    conv_w_full = _whole_from_shards(conv_w_s, 2)
    meta_full = _whole_from_shards(meta_s, 1)
    shard_axis = (1, 0, 0)
    later = []

    def weights(l, h):
        if l == 0:
            return [_whole_from_shards(s, axis) for s, axis in zip(first, shard_axis)]
        if not later:
            later.extend(_exchange_wait(*gathering, h, "gather_wait", gather=True))
        return [_whole_from_shards(s[:, l - 1], axis) for s, axis in zip(later, shard_axis)]

    in_flight = [None] * depth

    def ship(l, dh, dw_in, dw_pw2, dw_out):
        slabs = [_shard_major(dw, axis).astype(BF16) for dw, axis in zip((dw_in, dw_pw2, dw_out), shard_axis)]
        *in_flight[l], token = _exchange_start(slabs, dh, f"exchange_start_{l}", gather=False)
        return token

    loss, dx, grads = _local_step(x[0], loss_target[0], meta_full, pre_norm_g + token[:1, :1], post_norm_g, conv_w_full,
                                  conv_b, conv_ln_g, conv_ln_b, b_pw2, weights, ship)
    loss = lax.psum(loss, ("x", "y", "c"))
    landed = [_exchange_wait(*in_flight[l], dx, f"exchange_wait_{l}", gather=False) for l in range(depth)]

    def update(a, w, m, v, name):
        outs = [_sum_adamw(landed[l][a], w[l], m[l], v[l], name) for l in range(depth)]
        return [jnp.stack(o) for o in zip(*outs)]

    g_w_in, d_w_in, nm_w_in, nv_w_in = update(0, w_in, m_w_in, v_w_in, "adamw_w_in")
    g_w_pw2, d_w_pw2, nm_w_pw2, nv_w_pw2 = update(1, w_pw2, m_w_pw2, v_w_pw2, "adamw_w_pw2")
    g_w_out, d_w_out, nm_w_out, nv_w_out = update(2, w_out, m_w_out, v_w_out, "adamw_w_out")

    small_names = ("pre_g", "post_g", "conv_b", "ln_g", "ln_b", "b_pw2", "conv_w", "meta")
    small_full = [grads[k] for k in small_names]
    gathered, = _all_gather([_pack(small_full)], "gather_small_grads")
    summed = _sum_parts(gathered, "sum_small_grads")
    g_small = dict(zip(small_names, _unpack(summed, [a.shape for a in small_full])))
    g_small["conv_w"] = lax.dynamic_slice_in_dim(g_small["conv_w"], me * conv_w.shape[2], conv_w.shape[2], axis=2)
    g_small["meta"] = lax.dynamic_slice_in_dim(g_small["meta"], me * meta_tokens.shape[1], meta_tokens.shape[1], axis=1)
    small_w = dict(zip(small_names, (pre_norm_g, post_norm_g, conv_b, conv_ln_g, conv_ln_b, b_pw2, conv_w, meta_tokens)))
    small_m = (m_pre_norm_g, m_post_norm_g, m_conv_b, m_conv_ln_g, m_conv_ln_b, m_b_pw2, m_conv_w, m_meta_tokens)
    small_v = (v_pre_norm_g, v_post_norm_g, v_conv_b, v_conv_ln_g, v_conv_ln_b, v_b_pw2, v_conv_w, v_meta_tokens)
    small_shapes = [small_w[k].shape for k in small_names]
    outs = _sum_adamw(_pack([g_small[k] for k in small_names])[None], _pack([small_w[k] for k in small_names]),
                      _pack(small_m), _pack(small_v), "adamw_small_weights")
    g_s, d_s, nm_s, nv_s = [dict(zip(small_names, _unpack(o, small_shapes))) for o in outs]

    def ordered(s, w_in_, w_pw2_, w_out_):
        return (s["meta"], s["pre_g"], s["post_g"], w_in_, s["conv_w"], s["conv_b"], s["ln_g"], s["ln_b"], w_pw2_,
                s["b_pw2"], w_out_)

    return (loss, dx[None], *ordered(g_s, g_w_in, g_w_pw2, g_w_out), *ordered(d_s, d_w_in, d_w_pw2, d_w_out),
            *ordered(nm_s, nm_w_in, nm_w_pw2, nm_w_out), *ordered(nv_s, nv_w_in, nv_w_pw2, nv_w_out))
```

```python
import functools

import jax
import jax.numpy as jnp
from jax import lax
from jax.experimental import pallas as pl
from jax.experimental.pallas import tpu as pltpu

F32 = jnp.float32
BF16 = jnp.bfloat16

D_MODEL = 1024
D_CONV = 512
D_SB = 512
HEAD_DIM = 64
HEADS_PER_BLOCK = 4
HEAD_BLK = HEADS_PER_BLOCK * HEAD_DIM
CONV_WIDTH = 31
N_META = 16
D_IN = 3 * D_CONV + 4 * D_SB
RMS_EPS = 1e-6
LN_EPS = 1e-5
SB_SCALE = HEAD_DIM ** -0.5

ADAM_LR = 0.001
ADAM_B1 = 0.9
ADAM_B2 = 0.999
ADAM_EPS = 1e-08
ADAM_WD = 0.01
ADAM_STEP = 10

N_DEV = 8
LANES = 128
ROW_BLK = 256
HALO = 32
VMEM_LIMIT = 56 * 1024 * 1024
MESH = pl.DeviceIdType.MESH


def _cparams(*sem):
    return pltpu.CompilerParams(dimension_semantics=sem, vmem_limit_bytes=VMEM_LIMIT)


def _rows(n_cols, col=0):
    return pl.BlockSpec((ROW_BLK, n_cols), lambda i, col=col: (i, col))


def _whole(shape):
    return pl.BlockSpec(shape, lambda i: (0,) * len(shape))


def _sigmoid(x):
    return jax.nn.sigmoid(x)


def _dsilu(x, s):
    return s * (1.0 + x * (1.0 - s))


def _dot(a, b):
    return jnp.dot(a, b, preferred_element_type=F32)


def _dot_nt(a, b):
    return lax.dot_general(a, b, (((1,), (1,)), ((), ())), preferred_element_type=F32)


def _dot_tn(a, b):
    return lax.dot_general(a, b, (((0,), (0,)), ((), ())), preferred_element_type=F32)


def _inproj_fwd(h, g_pre, w_in):
    t = h.shape[0]

    def body(h_ref, g_ref, w_ref, pc_ref, qkv_ref, sbg_ref, u_ref):
        x = h_ref[...]
        r = lax.rsqrt(jnp.mean(x * x, axis=-1, keepdims=True) + RMS_EPS)
        u = (x * r * g_ref[...]).astype(BF16)
        u_ref[...] = u
        pc_ref[...] = _dot(u, w_ref[:, 0:1536])
        qkv_ref[...] = _dot(u, w_ref[:, 1536:3072]).astype(BF16)
        sbg_ref[...] = _dot(u, w_ref[:, 3072:3584])

    return pl.pallas_call(
        body, name="inproj_fwd", grid=(t // ROW_BLK,),
        in_specs=[_rows(D_MODEL), _whole((1, D_MODEL)), _whole((D_MODEL, D_IN))],
        out_specs=[_rows(1536), _rows(1536), _rows(D_SB), _rows(D_MODEL)],
        out_shape=[jax.ShapeDtypeStruct((t, 1536), F32), jax.ShapeDtypeStruct((t, 1536), BF16),
                   jax.ShapeDtypeStruct((t, D_SB), F32), jax.ShapeDtypeStruct((t, D_MODEL), BF16)],
        compiler_params=_cparams("parallel"),
    )(h, g_pre, w_in)


def _prev_halo(col):
    per = ROW_BLK // HALO
    return pl.BlockSpec((HALO, D_CONV), lambda i, col=col: (jnp.maximum(i * per - 1, 0), col))


def _fill_glu(buf, i, a_ref, b_ref, ha_ref, hb_ref):
    halo = ha_ref[...] * _sigmoid(hb_ref[...])
    buf[0:HALO, :] = jnp.where(i > 0, halo, 0.0)
    buf[HALO:HALO + ROW_BLK, :] = a_ref[...] * _sigmoid(b_ref[...])


SUBLANES = 8
TAP_ROWS = 64
SHIFT_ROWS = HALO + ROW_BLK - SUBLANES


def _fill_shifts(shifts, buf):
    for b in range(1, SUBLANES):
        shifts[b - 1] = buf[pl.ds(b, SHIFT_ROWS), :]


def _window(buf, shifts, first, rows, lanes):
    whole, part = divmod(first, SUBLANES)
    src = buf if part == 0 else shifts.at[part - 1]
    return src[pl.ds(rows.start + SUBLANES * whole, rows.size), lanes]


TAP_ROW_CHUNKS = [pl.ds(r, TAP_ROWS) for r in range(0, ROW_BLK, TAP_ROWS)]
TAP_LANE_TILES = [pl.ds(c, LANES) for c in range(0, D_CONV, LANES)]


def _layer_norm_stats(cv):
    mu = jnp.mean(cv, axis=-1, keepdims=True)
    xc = cv - mu
    rstd = lax.rsqrt(jnp.mean(xc * xc, axis=-1, keepdims=True) + LN_EPS)
    return xc * rstd, rstd


def _conv_fwd(pc, conv_w, conv_b, ln_g, ln_b, w_pw2, b_pw2):
    t = pc.shape[0]

    def body(a_ref, b_ref, gate_ref, ha_ref, hb_ref, cw_ref, cb_ref, lg_ref, lb_ref, wp_ref, bp_ref,
             cout_ref, cv_ref, p_ref, sl_ref, buf, shifts):
        i = pl.program_id(0)
        _fill_glu(buf, i, a_ref, b_ref, ha_ref, hb_ref)
        _fill_shifts(shifts, buf)
        for lanes in TAP_LANE_TILES:
            for rows in TAP_ROW_CHUNKS:
                acc = jnp.zeros((TAP_ROWS, LANES), F32) + cb_ref[:, lanes]
                for j in range(CONV_WIDTH):
                    acc = acc + cw_ref[j:j + 1, lanes] * _window(buf, shifts, HALO - (CONV_WIDTH - 1) + j, rows, lanes)
                cv_ref[rows, lanes] = acc
        xh, _ = _layer_norm_stats(cv_ref[...])
        ln = xh * lg_ref[...] + lb_ref[...]
        sl = (ln * _sigmoid(ln)).astype(BF16)
        sl_ref[...] = sl
        p = _dot(sl, wp_ref[...]) + bp_ref[...]
        p_ref[...] = p
        gate = gate_ref[...]
        cout_ref[...] = (p * (gate * _sigmoid(gate))).astype(BF16)

    vec = _whole((1, D_CONV))
    return pl.pallas_call(
        body, name="conv_fwd", grid=(t // ROW_BLK,),
        in_specs=[_rows(D_CONV, 0), _rows(D_CONV, 1), _rows(D_CONV, 2), _prev_halo(0), _prev_halo(1),
                  _whole((CONV_WIDTH, D_CONV)), vec, vec, vec, _whole((D_CONV, D_CONV)), vec],
        out_specs=[_rows(D_CONV)] * 4,
        out_shape=[jax.ShapeDtypeStruct((t, D_CONV), BF16), jax.ShapeDtypeStruct((t, D_CONV), F32),
                   jax.ShapeDtypeStruct((t, D_CONV), F32), jax.ShapeDtypeStruct((t, D_CONV), BF16)],
        scratch_shapes=[pltpu.VMEM((HALO + ROW_BLK, D_CONV), F32), pltpu.VMEM((SUBLANES - 1, SHIFT_ROWS, D_CONV), F32)],
        compiler_params=_cparams("parallel"),
    )(pc, pc, pc, pc, pc, conv_w, conv_b, ln_g, ln_b, w_pw2, b_pw2)


def _lower_triangle():
    row = lax.broadcasted_iota(jnp.int32, (ROW_BLK, ROW_BLK), 0)
    col = lax.broadcasted_iota(jnp.int32, (ROW_BLK, ROW_BLK), 1)
    return row > col


def _lower_triangle_t():
    row = lax.broadcasted_iota(jnp.int32, (ROW_BLK, ROW_BLK), 0)
    col = lax.broadcasted_iota(jnp.int32, (ROW_BLK, ROW_BLK), 1)
    return row < col


def _tri_sum(x, umat):
    return _dot(x.astype(BF16), umat)


def _log_gates(z):
    ls = -(jnp.maximum(z, 0.0) + jnp.log(1.0 + jnp.exp(-jnp.abs(z))))
    return ls, z + ls


def _head_lanes(hh):
    lane = lax.broadcasted_iota(jnp.int32, (ROW_BLK, HEAD_BLK), 1)
    return (lane >= HEAD_DIM * hh) & (lane < HEAD_DIM * (hh + 1))


def _merge_heads(acc_ref):
    out = acc_ref[HEADS_PER_BLOCK - 1]
    for hh in range(HEADS_PER_BLOCK - 1):
        out = jnp.where(_head_lanes(hh), acc_ref[hh], out)
    return out


def _qkv_specs(t):
    n_blk = D_SB // HEAD_BLK
    return [pl.BlockSpec((ROW_BLK, HEAD_BLK), lambda hp, i: (i, hp)),
            pl.BlockSpec((t, HEAD_BLK), lambda hp, i: (0, n_blk + hp)),
            pl.BlockSpec((t, HEAD_BLK), lambda hp, i: (0, 2 * n_blk + hp))]


def _carry_spec():
    return pl.BlockSpec((HEADS_PER_BLOCK, ROW_BLK, LANES), lambda hp, i: (hp, i, 0))


def _last_block_rows(t, n_tokens):
    packed_rows = 16
    return -(-(n_tokens - (t - ROW_BLK)) // packed_rows) * packed_rows


def _by_block_rows(i, last_rows, sweep):
    if last_rows == ROW_BLK:
        sweep(ROW_BLK)
        return
    last = pl.num_programs(1) - 1
    pl.when(i < last)(lambda: sweep(ROW_BLK))
    pl.when(i == last)(lambda: sweep(last_rows))


def _attn_fwd(qkv, n_tokens):
    t = qkv.shape[0]
    assert t // ROW_BLK <= LANES

    def body(q_ref, k_ref, v_ref, o_ref, c_ref, acc_ref, run_ref, qm_ref, z_ref):
        i = pl.program_id(1)
        q = q_ref[...]
        heads = range(HEADS_PER_BLOCK)
        for hh in heads:
            qm_ref[hh] = jnp.where(_head_lanes(hh), q, jnp.zeros_like(q)) * jnp.asarray(SB_SCALE, BF16)
        acc_ref[...] = jnp.zeros_like(acc_ref)
        c_ref[...] = jnp.zeros_like(c_ref)
        run_ref[...] = jnp.zeros_like(run_ref)

        def sweep(n_rows):
            rows = pl.ds(0, n_rows)
            lower = _lower_triangle()[:n_rows]
            umat = jnp.where(_lower_triangle(), 1.0, 0.0).astype(BF16)
            lane = lax.broadcasted_iota(jnp.int32, (n_rows, LANES), 1)

            def scores(jb):
                start = pl.multiple_of(jb * ROW_BLK, ROW_BLK)
                kb = k_ref[pl.ds(start, ROW_BLK), :]
                for hh in heads:
                    z_ref[hh, rows] = _dot_nt(qm_ref[hh, rows], kb)

            def block(jb, diagonal):
                start = pl.multiple_of(jb * ROW_BLK, ROW_BLK)
                vb = v_ref[pl.ds(start, ROW_BLK), :]
                logits = []
                for hh in heads:
                    ls, lb = _log_gates(z_ref[hh, rows])
                    if diagonal:
                        ls = jnp.where(lower, ls, 0.0)
                    run = run_ref[hh, rows]
                    if not diagonal:
                        c_ref[hh, rows] = jnp.where(lane == jb, run, c_ref[hh, rows])
                    logits.append(lb + jnp.concatenate([run, run], axis=1) + _tri_sum(ls, umat))
                    run_ref[hh, rows] = run + jnp.sum(ls, axis=1, keepdims=True)
                scores(jnp.maximum(jb - 1, 0))
                for hh in heads:
                    a = jnp.exp(logits[hh])
                    if diagonal:
                        a = jnp.where(lower, a, 0.0)
                    acc_ref[hh, rows] += _dot(a.astype(BF16), vb)

            scores(i)
            block(i, True)

            @pl.loop(0, i)
            def _(n):
                block(i - 1 - n, False)

        _by_block_rows(i, _last_block_rows(t, n_tokens), sweep)
        o_ref[...] = _merge_heads(acc_ref)

    per_head = (HEADS_PER_BLOCK, ROW_BLK, HEAD_BLK)
    return pl.pallas_call(
        body, name="attn_fwd", grid=(D_SB // HEAD_BLK, t // ROW_BLK),
        in_specs=_qkv_specs(t),
        out_specs=[pl.BlockSpec((ROW_BLK, HEAD_BLK), lambda hp, i: (i, hp)), _carry_spec()],
        out_shape=[jax.ShapeDtypeStruct((t, D_SB), F32),
                   jax.ShapeDtypeStruct((D_SB // HEAD_DIM, t, LANES), F32)],
        scratch_shapes=[pltpu.VMEM(per_head, F32), pltpu.VMEM((HEADS_PER_BLOCK, ROW_BLK, LANES), F32),
                        pltpu.VMEM(per_head, BF16), pltpu.VMEM((HEADS_PER_BLOCK, ROW_BLK, ROW_BLK), F32)],
        compiler_params=_cparams("arbitrary", "arbitrary"),
    )(qkv, qkv, qkv)


def _outproj_fwd(h, cout, sraw, sbg, w_out, g_post):
    t = h.shape[0]

    def body(h_ref, c_ref, s_ref, g_ref, w_ref, gp_ref, hn_ref, mixed_ref, mix_ref):
        gate = g_ref[...]
        mix_ref[:, 0:D_CONV] = c_ref[...]
        mix_ref[:, D_CONV:] = (s_ref[...] * (gate * _sigmoid(gate))).astype(BF16)
        mixed = _dot(mix_ref[...], w_ref[...])
        mixed_ref[...] = mixed
        r = lax.rsqrt(jnp.mean(mixed * mixed, axis=-1, keepdims=True) + RMS_EPS)
        hn_ref[...] = h_ref[...] + mixed * r * gp_ref[...]

    return pl.pallas_call(
        body, name="outproj_fwd", grid=(t // ROW_BLK,),
        in_specs=[_rows(D_MODEL), _rows(D_CONV), _rows(D_SB), _rows(D_SB), _whole((D_MODEL, D_MODEL)),
                  _whole((1, D_MODEL))],
        out_specs=[_rows(D_MODEL)] * 3,
        out_shape=[jax.ShapeDtypeStruct((t, D_MODEL), F32), jax.ShapeDtypeStruct((t, D_MODEL), F32),
                   jax.ShapeDtypeStruct((t, D_MODEL), BF16)],
        compiler_params=_cparams("parallel"),
    )(h, cout, sraw, sbg, w_out, g_post)


def _loss_and_grad(h, target, seq):
    t = h.shape[0]

    def body(h_ref, t_ref, loss_ref, dh_ref):
        i = pl.program_id(0)

        @pl.when(i == 0)
        def _():
            loss_ref[...] = jnp.zeros_like(loss_ref)

        row = i * ROW_BLK + lax.broadcasted_iota(jnp.int32, (ROW_BLK, D_MODEL), 0)
        real = (row >= N_META) & (row < N_META + seq)
        diff = jnp.where(real, h_ref[...] - t_ref[...], 0.0)
        sq = jnp.sum(jnp.sum(diff * diff, axis=1, keepdims=True), axis=0, keepdims=True)
        loss_ref[...] += (0.5 / D_MODEL) * sq
        dh_ref[...] = diff * (1.0 / D_MODEL)

    return pl.pallas_call(
        body, name="loss", grid=(t // ROW_BLK,),
        in_specs=[_rows(D_MODEL), _rows(D_MODEL)],
        out_specs=[_whole((1, 1)), _rows(D_MODEL)],
        out_shape=[jax.ShapeDtypeStruct((1, 1), F32), jax.ShapeDtypeStruct((t, D_MODEL), F32)],
        compiler_params=_cparams("arbitrary"),
    )(h, target)


def _outproj_bwd(dh, mixed, g_post, sraw, sbg, w_out_t):
    t = dh.shape[0]

    def body(dh_ref, mixed_ref, gp_ref, s_ref, g_ref, wt_ref, dc_ref, ds_ref, dg_ref, dmb_ref, dgp_ref):
        @pl.when(pl.program_id(0) == 0)
        def _():
            dgp_ref[...] = jnp.zeros_like(dgp_ref)

        mixed = mixed_ref[...]
        r = lax.rsqrt(jnp.mean(mixed * mixed, axis=-1, keepdims=True) + RMS_EPS)
        nh = mixed * r
        dy = dh_ref[...]
        dgp_ref[...] += jnp.sum(dy * nh, axis=0, keepdims=True)
        dn = dy * gp_ref[...]
        dmixed = (r * (dn - nh * jnp.mean(dn * nh, axis=-1, keepdims=True))).astype(BF16)
        dmb_ref[...] = dmixed
        dmix = _dot_nt(dmixed, wt_ref[...])
        dc_ref[...] = dmix[:, 0:D_CONV]
        dsg = dmix[:, D_CONV:]
        gate = g_ref[...]
        sg = _sigmoid(gate)
        ds_ref[...] = dsg * (gate * sg)
        dg_ref[...] = dsg * s_ref[...] * _dsilu(gate, sg)

    return pl.pallas_call(
        body, name="outproj_bwd", grid=(t // ROW_BLK,),
        in_specs=[_rows(D_MODEL), _rows(D_MODEL), _whole((1, D_MODEL)), _rows(D_SB), _rows(D_SB),
                  _whole((D_MODEL, D_MODEL))],
        out_specs=[_rows(D_CONV), _rows(D_SB), _rows(D_SB), _rows(D_MODEL), _whole((1, D_MODEL))],
        out_shape=[jax.ShapeDtypeStruct((t, D_CONV), F32), jax.ShapeDtypeStruct((t, D_SB), F32),
                   jax.ShapeDtypeStruct((t, D_SB), F32), jax.ShapeDtypeStruct((t, D_MODEL), BF16),
                   jax.ShapeDtypeStruct((1, D_MODEL), F32)],
        compiler_params=_cparams("arbitrary"),
    )(dh, mixed, g_post, sraw, sbg, w_out_t)


def _attn_bwd(qkv, carries, do, n_tokens):
    t = qkv.shape[0]

    def body(q_ref, k_ref, v_ref, c_ref, do_ref, dq_ref, dk_ref, dv_ref, acc_ref, seen_ref, qm_ref, dom_ref, z_ref,
             da_ref, dz_ref, a_ref):
        i = pl.program_id(1)

        @pl.when(i == 0)
        def _():
            dk_ref[...] = jnp.zeros_like(dk_ref)
            dv_ref[...] = jnp.zeros_like(dv_ref)

        q = q_ref[...]
        dof = do_ref[...]
        heads = range(HEADS_PER_BLOCK)
        for hh in heads:
            qm_ref[hh] = jnp.where(_head_lanes(hh), q, jnp.zeros_like(q)) * jnp.asarray(SB_SCALE, BF16)
            dom_ref[hh] = jnp.where(_head_lanes(hh), dof, 0.0).astype(BF16)
        acc_ref[...] = jnp.zeros_like(acc_ref)
        seen_ref[...] = jnp.zeros_like(seen_ref)

        def sweep(n_rows):
            rows = pl.ds(0, n_rows)
            lower = _lower_triangle()[:n_rows]
            umat = jnp.where(_lower_triangle(), 1.0, 0.0).astype(BF16)
            umat_t = jnp.where(_lower_triangle_t(), 1.0, 0.0).astype(BF16)
            lane = lax.broadcasted_iota(jnp.int32, (n_rows, LANES), 1)

            def scores(jb):
                start = pl.multiple_of(jb * ROW_BLK, ROW_BLK)
                kb = k_ref[pl.ds(start, ROW_BLK), :]
                for hh in heads:
                    z_ref[hh, rows] = _dot_nt(qm_ref[hh, rows], kb)

            def value_grads(jb):
                start = pl.multiple_of(jb * ROW_BLK, ROW_BLK)
                vb = v_ref[pl.ds(start, ROW_BLK), :]
                for hh in heads:
                    da_ref[hh, rows] = _dot_nt(dom_ref[hh, rows], vb)

            def products(jb, hh):
                start = pl.multiple_of(jb * ROW_BLK, ROW_BLK)
                dzb = dz_ref[hh, rows]
                acc_ref[hh, rows] += _dot(dzb, k_ref[pl.ds(start, ROW_BLK), :])
                dk_ref[pl.ds(start, ROW_BLK), :] += _dot_tn(dzb, qm_ref[hh, rows])
                dv_ref[pl.ds(start, ROW_BLK), :] += _dot_tn(a_ref[hh, rows], dom_ref[hh, rows])

            def block(jb, diagonal):
                before = jnp.maximum(jb - 1, 0)
                lbs, logits = [], []
                for hh in heads:
                    products(before, hh)
                    ls, lb = _log_gates(z_ref[hh, rows])
                    if diagonal:
                        ls = jnp.where(lower, ls, 0.0)
                        logits.append(lb + _tri_sum(ls, umat))
                    else:
                        right = jnp.sum(jnp.where(lane == jb, c_ref[hh, rows], 0.0), axis=1, keepdims=True)
                        logits.append(lb + right + _tri_sum(ls, umat))
                    lbs.append(lb)
                if not diagonal:
                    scores(jb + 1)
                gs, befores = [], []
                for hh in heads:
                    a = jnp.exp(logits[hh])
                    if diagonal:
                        a = jnp.where(lower, a, 0.0)
                    g = da_ref[hh, rows] * a
                    seen = seen_ref[hh, rows]
                    befores.append(jnp.concatenate([seen, seen], axis=1) + _tri_sum(g, umat_t))
                    seen_ref[hh, rows] = seen + jnp.sum(g, axis=1, keepdims=True)
                    a_ref[hh, rows] = a.astype(BF16)
                    gs.append(g)
                if not diagonal:
                    value_grads(jb + 1)
                for hh in heads:
                    dz = gs[hh] - jnp.exp(lbs[hh]) * (gs[hh] + befores[hh])
                    if diagonal:
                        dz = jnp.where(lower, dz, 0.0)
                    dz_ref[hh, rows] = dz.astype(BF16)

            dz_ref[...] = jnp.zeros_like(dz_ref)
            a_ref[...] = jnp.zeros_like(a_ref)
            scores(0)
            value_grads(0)

            @pl.loop(0, i)
            def _(jb):
                block(jb, False)

            block(i, True)
            for hh in heads:
                products(i, hh)

        _by_block_rows(i, _last_block_rows(t, n_tokens), sweep)
        dq_ref[...] = _merge_heads(acc_ref) * SB_SCALE

    blk = pl.BlockSpec((ROW_BLK, HEAD_BLK), lambda hp, i: (i, hp))
    full = pl.BlockSpec((t, HEAD_BLK), lambda hp, i: (0, hp))
    per_head = (HEADS_PER_BLOCK, ROW_BLK, HEAD_BLK)
    return pl.pallas_call(
        body, name="attn_bwd", grid=(D_SB // HEAD_BLK, t // ROW_BLK),
        in_specs=_qkv_specs(t) + [_carry_spec(), blk],
        out_specs=[blk, full, full],
        out_shape=[jax.ShapeDtypeStruct((t, D_SB), F32)] * 3,
        scratch_shapes=[pltpu.VMEM(per_head, F32), pltpu.VMEM((HEADS_PER_BLOCK, ROW_BLK, LANES), F32),
                        pltpu.VMEM(per_head, BF16), pltpu.VMEM(per_head, BF16)]
                       + [pltpu.VMEM((HEADS_PER_BLOCK, ROW_BLK, ROW_BLK), dtype) for dtype in (F32, F32, BF16, BF16)],
        compiler_params=_cparams("arbitrary", "arbitrary"),
    )(qkv, qkv, qkv, carries, do)


def _conv_bwd_rows(dcout, pc, cv, p, ln_g, ln_b, w_pw2_t):
    t = dcout.shape[0]

    def body(dc_ref, gate_ref, cv_ref, p_ref, lg_ref, lb_ref, wt_ref, dcv_ref, dgate_ref, dpb_ref, vec_ref):
        @pl.when(pl.program_id(0) == 0)
        def _():
            vec_ref[...] = jnp.zeros_like(vec_ref)

        dc = dc_ref[...]
        gate = gate_ref[...]
        sg = _sigmoid(gate)
        dp = dc * (gate * sg)
        dgate_ref[...] = dc * p_ref[...] * _dsilu(gate, sg)
        dpb = dp.astype(BF16)
        dpb_ref[...] = dpb
        xh, rstd = _layer_norm_stats(cv_ref[...])
        ln = xh * lg_ref[...] + lb_ref[...]
        s2 = _sigmoid(ln)
        dln = _dot_nt(dpb, wt_ref[...]) * _dsilu(ln, s2)
        dxh = dln * lg_ref[...]
        dcv = rstd * (dxh - jnp.mean(dxh, axis=-1, keepdims=True)
                      - xh * jnp.mean(dxh * xh, axis=-1, keepdims=True))
        dcv_ref[...] = dcv
        vec_ref[0:1, :] += jnp.sum(dp, axis=0, keepdims=True)
        vec_ref[1:2, :] += jnp.sum(dln * xh, axis=0, keepdims=True)
        vec_ref[2:3, :] += jnp.sum(dln, axis=0, keepdims=True)
        vec_ref[3:4, :] += jnp.sum(dcv, axis=0, keepdims=True)

    vec = _whole((1, D_CONV))
    return pl.pallas_call(
        body, name="conv_bwd_rows", grid=(t // ROW_BLK,),
        in_specs=[_rows(D_CONV), _rows(D_CONV, 2), _rows(D_CONV), _rows(D_CONV), vec, vec,
                  _whole((D_CONV, D_CONV))],
        out_specs=[_rows(D_CONV), _rows(D_CONV), _rows(D_CONV), _whole((8, D_CONV))],
        out_shape=[jax.ShapeDtypeStruct((t, D_CONV), F32), jax.ShapeDtypeStruct((t, D_CONV), F32),
                   jax.ShapeDtypeStruct((t, D_CONV), BF16), jax.ShapeDtypeStruct((8, D_CONV), F32)],
        compiler_params=_cparams("arbitrary"),
    )(dcout, pc, cv, p, ln_g, ln_b, w_pw2_t)


def _conv_bwd_taps(dcv, pc, conv_w):
    t = dcv.shape[0]
    n_halo = t // HALO
    per = ROW_BLK // HALO

    def body(d_ref, dn_ref, a_ref, b_ref, ha_ref, hb_ref, cw_ref, da_ref, db_ref, dw_ref, cbuf, dbuf, cshifts, dshifts):
        i = pl.program_id(0)

        @pl.when(i == 0)
        def _():
            dw_ref[...] = jnp.zeros_like(dw_ref)

        _fill_glu(cbuf, i, a_ref, b_ref, ha_ref, hb_ref)
        dbuf[0:ROW_BLK, :] = d_ref[...]
        dbuf[ROW_BLK:ROW_BLK + HALO, :] = jnp.where(i < pl.num_programs(0) - 1, dn_ref[...], 0.0)
        _fill_shifts(cshifts, cbuf)
        _fill_shifts(dshifts, dbuf)
        for lanes in TAP_LANE_TILES:
            for rows in TAP_ROW_CHUNKS:
                acc = jnp.zeros((TAP_ROWS, LANES), F32)
                for j in range(CONV_WIDTH):
                    acc = acc + cw_ref[j:j + 1, lanes] * _window(dbuf, dshifts, CONV_WIDTH - 1 - j, rows, lanes)
                sb = _sigmoid(b_ref[rows, lanes])
                da_ref[rows, lanes] = acc * sb
                db_ref[rows, lanes] = acc * a_ref[rows, lanes] * sb * (1.0 - sb)
            for j in range(CONV_WIDTH):
                acc = jnp.zeros((TAP_ROWS, LANES), F32)
                for rows in TAP_ROW_CHUNKS:
                    acc = acc + d_ref[rows, lanes] * _window(cbuf, cshifts, HALO - (CONV_WIDTH - 1) + j, rows, lanes)
                dw_ref[j:j + 1, lanes] += jnp.sum(acc, axis=0, keepdims=True)

    return pl.pallas_call(
        body, name="conv_bwd_taps", grid=(t // ROW_BLK,),
        in_specs=[_rows(D_CONV),
                  pl.BlockSpec((HALO, D_CONV), lambda i: (jnp.minimum((i + 1) * per, n_halo - 1), 0)),
                  _rows(D_CONV, 0), _rows(D_CONV, 1), _prev_halo(0), _prev_halo(1),
                  _whole((CONV_WIDTH, D_CONV))],
        out_specs=[_rows(D_CONV), _rows(D_CONV), _whole((32, D_CONV))],
        out_shape=[jax.ShapeDtypeStruct((t, D_CONV), F32), jax.ShapeDtypeStruct((t, D_CONV), F32),
                   jax.ShapeDtypeStruct((32, D_CONV), F32)],
        scratch_shapes=[pltpu.VMEM((HALO + ROW_BLK, D_CONV), F32), pltpu.VMEM((ROW_BLK + HALO, D_CONV), F32),
                        pltpu.VMEM((SUBLANES - 1, SHIFT_ROWS, D_CONV), F32),
                        pltpu.VMEM((SUBLANES - 1, SHIFT_ROWS, D_CONV), F32)],
        compiler_params=_cparams("arbitrary"),
    )(dcv, dcv, pc, pc, pc, pc, conv_w)


def _inproj_bwd(dh_out, h, g_pre, pieces, w_in_t):
    t = h.shape[0]

    def body(dh_ref, h_ref, g_ref, *rest):
        piece_refs, (wt_ref, dhin_ref, dproj_ref, dg_ref) = rest[:7], rest[7:]

        @pl.when(pl.program_id(0) == 0)
        def _():
            dg_ref[...] = jnp.zeros_like(dg_ref)

        for k, ref in enumerate(piece_refs):
            dproj_ref[:, 512 * k:512 * (k + 1)] = ref[...].astype(BF16)
        du = _dot_nt(dproj_ref[...], wt_ref[...])
        x = h_ref[...]
        r = lax.rsqrt(jnp.mean(x * x, axis=-1, keepdims=True) + RMS_EPS)
        xh = x * r
        dg_ref[...] += jnp.sum(du * xh, axis=0, keepdims=True)
        dxh = du * g_ref[...]
        dhin_ref[...] = dh_ref[...] + r * (dxh - xh * jnp.mean(dxh * xh, axis=-1, keepdims=True))

    return pl.pallas_call(
        body, name="inproj_bwd", grid=(t // ROW_BLK,),
        in_specs=[_rows(D_MODEL), _rows(D_MODEL), _whole((1, D_MODEL))] + [_rows(512)] * 7
                 + [_whole((D_MODEL, D_IN))],
        out_specs=[_rows(D_MODEL), _rows(D_IN), _whole((1, D_MODEL))],
        out_shape=[jax.ShapeDtypeStruct((t, D_MODEL), F32), jax.ShapeDtypeStruct((t, D_IN), BF16),
                   jax.ShapeDtypeStruct((1, D_MODEL), F32)],
        compiler_params=_cparams("arbitrary"),
    )(dh_out, h, g_pre, *pieces, w_in_t)


def _weight_grad(xb, dyb, name):
    t, k = xb.shape
    n = dyb.shape[1]
    tn = n

    def body(x_ref, dy_ref, o_ref):
        @pl.when(pl.program_id(1) == 0)
        def _():
            o_ref[...] = jnp.zeros_like(o_ref)

        o_ref[...] += _dot_tn(x_ref[...], dy_ref[...])

    return pl.pallas_call(
        body, name=name, grid=(n // tn, t // ROW_BLK),
        in_specs=[pl.BlockSpec((ROW_BLK, k), lambda j, i: (i, 0)), pl.BlockSpec((ROW_BLK, tn), lambda j, i: (i, j))],
        out_specs=pl.BlockSpec((k, tn), lambda j, i: (0, j)),
        out_shape=jax.ShapeDtypeStruct((k, n), F32),
        compiler_params=_cparams("parallel", "arbitrary"),
    )(xb, dyb)


def _position():
    return lax.axis_index("x"), lax.axis_index("y"), lax.axis_index("c")


def _comm_call(body, name, ins, out_shapes):
    n = len(ins)
    hbm = pl.BlockSpec(memory_space=pltpu.HBM)
    return pl.pallas_call(
        functools.partial(body, n), name=name, in_specs=[hbm] * n, out_specs=[hbm] * n, out_shape=out_shapes,
        scratch_shapes=[pltpu.SemaphoreType.DMA((n, N_DEV - 1)), pltpu.SemaphoreType.DMA((n, N_DEV - 1)),
                        pltpu.SemaphoreType.DMA((n,))],
    )(*ins)


def _all_gather(blocks, name):
    def body(n, *refs):
        x_refs, out_refs, (send_sems, recv_sems, local_sems) = refs[:n], refs[n:2 * n], refs[2 * n:]
        x, y, c = _position()
        me, sibling = (x, y, c), (x, y, 1 - c)
        chips = [(1 - x, y), (x, 1 - y), (1 - x, 1 - y)]

        def slot(a, px, py, pc):
            return out_refs[a].at[4 * px + 2 * py + pc]

        def copy(a, k, origin, to, own=False):
            return pltpu.make_async_remote_copy(
                src_ref=x_refs[a] if own else slot(a, *origin), dst_ref=slot(a, *origin),
                send_sem=send_sems.at[a, k], recv_sem=recv_sems.at[a, k], device_id=to, device_id_type=MESH)

        arrays = range(n)
        mine = [pltpu.make_async_copy(x_refs[a], slot(a, *me), local_sems.at[a]) for a in arrays]
        first = [copy(a, 1 + j, me, (*chip, c), own=True) for j, chip in enumerate(chips) for a in arrays]
        first += [copy(a, 0, me, sibling, own=True) for a in arrays]
        for cp in mine + first:
            cp.start()
        passed = []
        for j, chip in enumerate(chips):
            for a in arrays:
                copy(a, 1 + j, (*chip, c), me).wait_recv()
                passed.append(copy(a, 4 + j, (*chip, c), sibling))
                passed[-1].start()
        for a in arrays:
            copy(a, 0, sibling, me).wait_recv()
            for j, chip in enumerate(chips):
                copy(a, 4 + j, (*chip, 1 - c), me).wait_recv()
        for cp in first + passed:
            cp.wait_send()
        for cp in mine:
            cp.wait()

    return _comm_call(body, name, blocks, [jax.ShapeDtypeStruct((N_DEV,) + b.shape, b.dtype) for b in blocks])


def _exchange_copies(g_refs, land_refs, sems, gather):
    x, y, c = _position()
    me = 4 * x + 2 * y + c
    out = []
    for g_ref, land_ref, (send_sem, recv_sem, local_sem) in zip(g_refs, land_refs, sems):
        def mine(slot, g_ref=g_ref):
            return g_ref if gather else g_ref.at[slot]

        def remote(src, dst, dev):
            return pltpu.make_async_remote_copy(src_ref=src, dst_ref=dst, send_sem=send_sem, recv_sem=recv_sem,
                                                device_id=dev, device_id_type=MESH)

        sends = []
        for k in range(1, N_DEV):
            px = 1 - x if k & 4 else x
            py = 1 - y if k & 2 else y
            pc = 1 - c if k & 1 else c
            sends.append(remote(mine(4 * px + 2 * py + pc), land_ref.at[me], (px, py, pc)))
        seven = land_ref.at[pl.ds(0, N_DEV - 1)]
        out.append((pltpu.make_async_copy(mine(me), land_ref.at[me], local_sem), sends, remote(seven, seven, (x, y, c))))
    return out


_HBM = pl.BlockSpec(memory_space=pltpu.HBM)
_SEM = pl.BlockSpec(memory_space=pltpu.SEMAPHORE)
_ORDERED = pltpu.CompilerParams(has_side_effects=pltpu.SideEffectType.DATAFLOW_SIDE_EFFECTING)
SEMS_PER_ARRAY = 3


def _exchange_start(arrays, after, name, gather):
    n = len(arrays)
    n_sems = SEMS_PER_ARRAY * n

    def body(*refs):
        g_refs, land_refs, sems, token = refs[:n], refs[n:2 * n], refs[2 * n + 1:2 * n + 1 + n_sems], refs[-1]
        sems = [sems[SEMS_PER_ARRAY * a:SEMS_PER_ARRAY * (a + 1)] for a in range(n)]
        for local, sends, _ in _exchange_copies(g_refs, land_refs, sems, gather):
            local.start()
            for cp in sends:
                cp.start()
        token[...] = jnp.zeros_like(token)

    buffers = list(arrays) + [lax.empty((N_DEV,) + g.shape if gather else g.shape, g.dtype) for g in arrays]
    outs = pl.pallas_call(
        body, name=name, in_specs=[_HBM] * (2 * n) + [pl.BlockSpec(memory_space=pl.ANY)],
        out_specs=[_SEM] * n_sems + [_HBM] * (2 * n) + [pl.BlockSpec(memory_space=pltpu.VMEM)],
        out_shape=[pltpu.SemaphoreType.DMA(())] * n_sems + [pltpu.HBM(b.shape, b.dtype) for b in buffers]
                  + [jax.ShapeDtypeStruct((8, LANES), F32)],
        input_output_aliases={a: n_sems + a for a in range(2 * n)}, compiler_params=_ORDERED,
    )(*[pltpu.with_memory_space_constraint(b, pltpu.HBM) for b in buffers], after)
    return outs[:n_sems], outs[n_sems:n_sems + n], outs[n_sems + n:n_sems + 2 * n], outs[-1]


def _exchange_wait(sems, arrays, landings, after, name, gather):
    n = len(arrays)
    n_sems = SEMS_PER_ARRAY * n

    def body(*refs):
        g_refs, land_refs, sems = refs[:n], refs[n:2 * n], refs[2 * n:2 * n + n_sems]
        sems = [sems[SEMS_PER_ARRAY * a:SEMS_PER_ARRAY * (a + 1)] for a in range(n)]
        for local, _, all_seven in _exchange_copies(g_refs, land_refs, sems, gather):
            all_seven.wait_recv()
            all_seven.wait_send()
            local.wait()

    buffers = list(arrays) + list(landings)
    outs = pl.pallas_call(
        body, name=name, in_specs=[_HBM] * (2 * n) + [_SEM] * n_sems + [pl.BlockSpec(memory_space=pl.ANY)],
        out_specs=[_HBM] * (2 * n), out_shape=[pltpu.HBM(b.shape, b.dtype) for b in buffers],
        input_output_aliases={a: a for a in range(2 * n)}, compiler_params=_ORDERED,
    )(*buffers, *sems, after)
    return outs[n:]


def _block_rows(r, row_bytes, budget=1 << 20):
    cap = max(8, budget // row_bytes)
    return max(d for d in range(8, min(r, cap) + 1, 8) if r % d == 0)


def _sum_adamw(parts, w, m, v, name):
    n_parts, r, c = parts.shape
    br = _block_rows(r, 4 * c)

    def body(p_ref, w_ref, m_ref, v_ref, g_out, d_out, m_out, v_out):
        g = p_ref[0].astype(F32)
        for s in range(1, n_parts):
            g = g + p_ref[s].astype(F32)
        m_new = ADAM_B1 * m_ref[...] + (1.0 - ADAM_B1) * g
        v_new = ADAM_B2 * v_ref[...] + (1.0 - ADAM_B2) * (g * g)
        m_hat = m_new / (1.0 - ADAM_B1 ** ADAM_STEP)
        v_hat = v_new / (1.0 - ADAM_B2 ** ADAM_STEP)
        g_out[...] = g
        d_out[...] = -ADAM_LR * (m_hat / (jnp.sqrt(v_hat) + ADAM_EPS) + ADAM_WD * w_ref[...])
        m_out[...] = m_new
        v_out[...] = v_new

    row = pl.BlockSpec((br, c), lambda i: (i, 0))
    return pl.pallas_call(
        body, name=name, grid=(r // br,),
        in_specs=[pl.BlockSpec((n_parts, br, c), lambda i: (0, i, 0)), row, row, row],
        out_specs=[row] * 4, out_shape=[jax.ShapeDtypeStruct((r, c), F32)] * 4,
        compiler_params=_cparams("parallel"),
    )(parts, w, m, v)


def _sum_parts(parts, name):
    n_parts, r, c = parts.shape

    def body(p_ref, o_ref):
        g = p_ref[0]
        for s in range(1, n_parts):
            g = g + p_ref[s]
        o_ref[...] = g

    return pl.pallas_call(
        body, name=name, in_specs=[pl.BlockSpec(memory_space=pltpu.VMEM)],
        out_specs=pl.BlockSpec(memory_space=pltpu.VMEM), out_shape=jax.ShapeDtypeStruct((r, c), F32),
    )(parts)


def _pack(arrays):
    flat = jnp.concatenate([a.reshape(-1) for a in arrays])
    pad = -flat.shape[0] % (8 * LANES)
    if pad:
        flat = jnp.pad(flat, (0, pad))
    return flat.reshape(-1, LANES)


def _unpack(buf, shapes):
    flat = buf.reshape(-1)
    out, at = [], 0
    for shape in shapes:
        size = 1
        for d in shape:
            size *= d
        out.append(lax.slice_in_dim(flat, at, at + size).reshape(shape))
        at += size
    return out


def _local_step(x, target, meta, pre_g, post_g, conv_w, conv_b, ln_g, ln_b, b_pw2, weights, ship):
    depth = pre_g.shape[0]
    seq = x.shape[0]
    t = -(-(N_META + seq) // ROW_BLK) * ROW_BLK
    tail = t - N_META - seq
    h = jnp.concatenate([meta, x, jnp.zeros((tail, D_MODEL), F32)], axis=0)
    target = jnp.pad(target, ((N_META, tail), (0, 0)))
    row = lambda a, l: a[l][None, :]

    saved = []
    for l in range(depth):
        w_in, w_pw2, w_out = weights(l, h)
        pc, qkv, sbg, u = _inproj_fwd(h, row(pre_g, l), w_in)
        cout, cv, p, sl = _conv_fwd(pc, conv_w[l], row(conv_b, l), row(ln_g, l), row(ln_b, l), w_pw2, row(b_pw2, l))
        sraw, carries = _attn_fwd(qkv, N_META + seq)
        h_new, mixed, mix = _outproj_fwd(h, cout, sraw, sbg, w_out, row(post_g, l))
        saved.append((h, pc, qkv, sbg, u, cv, p, sl, sraw, carries, mixed, mix, w_in, w_pw2, w_out))
        h = h_new

    loss, dh = _loss_and_grad(h, target, seq)

    grads = {k: [None] * depth for k in ("pre_g", "post_g", "conv_w", "conv_b", "ln_g", "ln_b", "b_pw2")}
    token = jnp.zeros((8, LANES), F32)
    for l in reversed(range(depth)):
        h_in, pc, qkv, sbg, u, cv, p, sl, sraw, carries, mixed, mix, w_in_t, w_pw2_t, w_out_t = saved[l]
        dcout, dsraw, dsbg, dmixed, dg_post = _outproj_bwd(dh, mixed, row(post_g, l) + token[:1, :1], sraw, sbg, w_out_t)
        dq, dk, dv = _attn_bwd(qkv, carries, dsraw, N_META + seq)
        dcv, dgate, dpb, vecs = _conv_bwd_rows(dcout, pc, cv, p, row(ln_g, l), row(ln_b, l), w_pw2_t)
        da, db, dconv_w = _conv_bwd_taps(dcv, pc, conv_w[l])
        dh, dproj, dg_pre = _inproj_bwd(dh, h_in, row(pre_g, l), (da, db, dgate, dq, dk, dv, dsbg), w_in_t)
        token = ship(l, dh, _weight_grad(u, dproj, "w_in_grad"), _weight_grad(sl, dpb, "w_pw2_grad"),
                     _weight_grad(mix, dmixed, "w_out_grad"))
        grads["pre_g"][l] = dg_pre[0]
        grads["post_g"][l] = dg_post[0]
        grads["b_pw2"][l], grads["ln_g"][l], grads["ln_b"][l], grads["conv_b"][l] = vecs[0], vecs[1], vecs[2], vecs[3]
        grads["conv_w"][l] = dconv_w[:CONV_WIDTH]

    grads = {k: jnp.stack(v) for k, v in grads.items()}
    grads["meta"] = dh[:N_META]
    return loss[0, 0], dh[N_META:N_META + seq], grads


def _shard_major(full, axis):
    shape = full.shape
    split = full.reshape(shape[:axis] + (N_DEV, shape[axis] // N_DEV) + shape[axis + 1:])
    return jnp.moveaxis(split, axis, 0)


def _whole_from_shards(shards, axis):
    moved = jnp.moveaxis(shards, 0, axis)
    shape = moved.shape
    return moved.reshape(shape[:axis] + (shape[axis] * shape[axis + 1],) + shape[axis + 2:])


def kernel(x, meta_tokens, pre_norm_g, post_norm_g, w_in, conv_w, conv_b, conv_ln_g, conv_ln_b, w_pw2, b_pw2, w_out, loss_target, m_meta_tokens, m_pre_norm_g, m_post_norm_g, m_w_in, m_conv_w, m_conv_b, m_conv_ln_g, m_conv_ln_b, m_w_pw2, m_b_pw2, m_w_out, v_meta_tokens, v_pre_norm_g, v_post_norm_g, v_w_in, v_conv_w, v_conv_b, v_conv_ln_g, v_conv_ln_b, v_w_pw2, v_b_pw2, v_w_out):
    me = 4 * lax.axis_index("x") + 2 * lax.axis_index("y") + lax.axis_index("c")

    depth = w_in.shape[0]
    big = [w.astype(BF16) for w in (w_in, w_pw2, w_out)]
    *first, conv_w_s, meta_s = _all_gather([w[0] for w in big] + [conv_w, meta_tokens], "gather_first_layer")
    *gathering, token = _exchange_start([w[1:] for w in big], meta_s, "gather_start", gather=True)
    conv_w_full = _whole_from_shards(conv_w_s, 2)
    meta_full = _whole_from_shards(meta_s, 1)
    shard_axis = (1, 0, 0)
    later = []

    def weights(l, h):
        if l == 0:
            return [_whole_from_shards(s, axis) for s, axis in zip(first, shard_axis)]
        if not later:
            later.extend(_exchange_wait(*gathering, h, "gather_wait", gather=True))
        return [_whole_from_shards(s[:, l - 1], axis) for s, axis in zip(later, shard_axis)]

    in_flight = [None] * depth

    def ship(l, dh, dw_in, dw_pw2, dw_out):
        slabs = [_shard_major(dw, axis).astype(BF16) for dw, axis in zip((dw_in, dw_pw2, dw_out), shard_axis)]
        *in_flight[l], token = _exchange_start(slabs, dh, f"exchange_start_{l}", gather=False)
        return token

    loss, dx, grads = _local_step(x[0], loss_target[0], meta_full, pre_norm_g + token[:1, :1], post_norm_g, conv_w_full,
                                  conv_b, conv_ln_g, conv_ln_b, b_pw2, weights, ship)
    small_names = ("pre_g", "post_g", "conv_b", "ln_g", "ln_b", "b_pw2", "conv_w", "meta")
    small_full = [grads[k] for k in small_names] + [loss.reshape(1)]
    *small_in_flight, _ = _exchange_start([_pack(small_full)], dx, "small_grads_start", gather=True)

    updated = [None] * depth
    done = dx

    def update_layer(l):
        landed = _exchange_wait(*in_flight[l], done, f"exchange_wait_{l}", gather=False)
        return [_sum_adamw(parts, w[l], m[l], v[l], name) for parts, w, m, v, name in zip(
            landed, (w_in, w_pw2, w_out), (m_w_in, m_w_pw2, m_w_out), (v_w_in, v_w_pw2, v_w_out),
            ("adamw_w_in", "adamw_w_pw2", "adamw_w_out"))]

    for l in reversed(range(1, depth)):
        updated[l] = update_layer(l)
        done = updated[l][0][1]

    gathered, = _exchange_wait(*small_in_flight, done, "small_grads_wait", gather=True)
    summed = _unpack(_sum_parts(gathered, "sum_small_grads"), [a.shape for a in small_full])
    loss = summed[-1][0]
    g_small = dict(zip(small_names, summed))
    g_small["conv_w"] = lax.dynamic_slice_in_dim(g_small["conv_w"], me * conv_w.shape[2], conv_w.shape[2], axis=2)
    g_small["meta"] = lax.dynamic_slice_in_dim(g_small["meta"], me * meta_tokens.shape[1], meta_tokens.shape[1], axis=1)
    small_w = dict(zip(small_names, (pre_norm_g, post_norm_g, conv_b, conv_ln_g, conv_ln_b, b_pw2, conv_w, meta_tokens)))
    small_m = (m_pre_norm_g, m_post_norm_g, m_conv_b, m_conv_ln_g, m_conv_ln_b, m_b_pw2, m_conv_w, m_meta_tokens)
    small_v = (v_pre_norm_g, v_post_norm_g, v_conv_b, v_conv_ln_g, v_conv_ln_b, v_b_pw2, v_conv_w, v_meta_tokens)
    small_shapes = [small_w[k].shape for k in small_names]
    outs = _sum_adamw(_pack([g_small[k] for k in small_names])[None], _pack([small_w[k] for k in small_names]),
                      _pack(small_m), _pack(small_v), "adamw_small_weights")
    g_s, d_s, nm_s, nv_s = [dict(zip(small_names, _unpack(o, small_shapes))) for o in outs]

    done = outs[1]
    updated[0] = update_layer(0)
    (g_w_in, d_w_in, nm_w_in, nv_w_in), (g_w_pw2, d_w_pw2, nm_w_pw2, nv_w_pw2), (g_w_out, d_w_out, nm_w_out, nv_w_out) = [
        [jnp.stack([updated[l][a][k] for l in range(depth)]) for k in range(4)] for a in range(3)]

    def ordered(s, w_in_, w_pw2_, w_out_):
        return (s["meta"], s["pre_g"], s["post_g"], w_in_, s["conv_w"], s["conv_b"], s["ln_g"], s["ln_b"], w_pw2_,
                s["b_pw2"], w_out_)

    return (loss, dx[None], *ordered(g_s, g_w_in, g_w_pw2, g_w_out), *ordered(d_s, d_w_in, d_w_pw2, d_w_out),
            *ordered(nm_s, nm_w_in, nm_w_pw2, nm_w_out), *ordered(nv_s, nv_w_in, nv_w_pw2, nv_w_out))
```

```python
import functools

import jax
import jax.numpy as jnp
from jax import lax
from jax.experimental import pallas as pl
from jax.experimental.pallas import tpu as pltpu

F32 = jnp.float32
BF16 = jnp.bfloat16

D_MODEL = 1024
D_CONV = 512
D_SB = 512
HEAD_DIM = 64
HEADS_PER_BLOCK = 4
HEAD_BLK = HEADS_PER_BLOCK * HEAD_DIM
CONV_WIDTH = 31
N_META = 16
D_IN = 3 * D_CONV + 4 * D_SB
RMS_EPS = 1e-6
LN_EPS = 1e-5
SB_SCALE = HEAD_DIM ** -0.5

ADAM_LR = 0.001
ADAM_B1 = 0.9
ADAM_B2 = 0.999
ADAM_EPS = 1e-08
ADAM_WD = 0.01
ADAM_STEP = 10

N_DEV = 8
LANES = 128
ROW_BLK = 256
HALO = 32
VMEM_LIMIT = 56 * 1024 * 1024
MESH = pl.DeviceIdType.MESH


def _cparams(*sem):
    return pltpu.CompilerParams(dimension_semantics=sem, vmem_limit_bytes=VMEM_LIMIT)


def _rows(n_cols, col=0):
    return pl.BlockSpec((ROW_BLK, n_cols), lambda i, col=col: (i, col))


def _whole(shape):
    return pl.BlockSpec(shape, lambda i: (0,) * len(shape))


def _sigmoid(x):
    return jax.nn.sigmoid(x)


def _dsilu(x, s):
    return s * (1.0 + x * (1.0 - s))


def _dot(a, b):
    return jnp.dot(a, b, preferred_element_type=F32)


def _dot_nt(a, b):
    return lax.dot_general(a, b, (((1,), (1,)), ((), ())), preferred_element_type=F32)


def _dot_tn(a, b):
    return lax.dot_general(a, b, (((0,), (0,)), ((), ())), preferred_element_type=F32)


def _inproj_fwd(h, g_pre, w_in):
    t = h.shape[0]

    def body(h_ref, g_ref, w_ref, pc_ref, qkv_ref, sbg_ref, u_ref):
        x = h_ref[...]
        r = lax.rsqrt(jnp.mean(x * x, axis=-1, keepdims=True) + RMS_EPS)
        u = (x * r * g_ref[...]).astype(BF16)
        u_ref[...] = u
        pc_ref[...] = _dot(u, w_ref[:, 0:1536])
        qkv_ref[...] = _dot(u, w_ref[:, 1536:3072]).astype(BF16)
        sbg_ref[...] = _dot(u, w_ref[:, 3072:3584])

    return pl.pallas_call(
        body, name="inproj_fwd", grid=(t // ROW_BLK,),
        in_specs=[_rows(D_MODEL), _whole((1, D_MODEL)), _whole((D_MODEL, D_IN))],
        out_specs=[_rows(1536), _rows(1536), _rows(D_SB), _rows(D_MODEL)],
        out_shape=[jax.ShapeDtypeStruct((t, 1536), F32), jax.ShapeDtypeStruct((t, 1536), BF16),
                   jax.ShapeDtypeStruct((t, D_SB), F32), jax.ShapeDtypeStruct((t, D_MODEL), BF16)],
        compiler_params=_cparams("parallel"),
    )(h, g_pre, w_in)


def _prev_halo(col):
    per = ROW_BLK // HALO
    return pl.BlockSpec((HALO, D_CONV), lambda i, col=col: (jnp.maximum(i * per - 1, 0), col))


def _fill_glu(buf, i, a_ref, b_ref, ha_ref, hb_ref):
    halo = ha_ref[...] * _sigmoid(hb_ref[...])
    buf[0:HALO, :] = jnp.where(i > 0, halo, 0.0)
    buf[HALO:HALO + ROW_BLK, :] = a_ref[...] * _sigmoid(b_ref[...])


SUBLANES = 8
TAP_ROWS = 64
SHIFT_ROWS = HALO + ROW_BLK - SUBLANES


def _fill_shifts(shifts, buf):
    for b in range(1, SUBLANES):
        shifts[b - 1] = buf[pl.ds(b, SHIFT_ROWS), :]


def _window(buf, shifts, first, rows, lanes):
    whole, part = divmod(first, SUBLANES)
    src = buf if part == 0 else shifts.at[part - 1]
    return src[pl.ds(rows.start + SUBLANES * whole, rows.size), lanes]


TAP_ROW_CHUNKS = [pl.ds(r, TAP_ROWS) for r in range(0, ROW_BLK, TAP_ROWS)]
TAP_LANE_TILES = [pl.ds(c, LANES) for c in range(0, D_CONV, LANES)]


def _layer_norm_stats(cv):
    mu = jnp.mean(cv, axis=-1, keepdims=True)
    xc = cv - mu
    rstd = lax.rsqrt(jnp.mean(xc * xc, axis=-1, keepdims=True) + LN_EPS)
    return xc * rstd, rstd


def _conv_fwd(pc, conv_w, conv_b, ln_g, ln_b, w_pw2, b_pw2):
    t = pc.shape[0]

    def body(a_ref, b_ref, gate_ref, ha_ref, hb_ref, cw_ref, cb_ref, lg_ref, lb_ref, wp_ref, bp_ref,
             cout_ref, cv_ref, p_ref, sl_ref, buf, shifts):
        i = pl.program_id(0)
        _fill_glu(buf, i, a_ref, b_ref, ha_ref, hb_ref)
        _fill_shifts(shifts, buf)
        for lanes in TAP_LANE_TILES:
            for rows in TAP_ROW_CHUNKS:
                acc = jnp.zeros((TAP_ROWS, LANES), F32) + cb_ref[:, lanes]
                for j in range(CONV_WIDTH):
                    acc = acc + cw_ref[j:j + 1, lanes] * _window(buf, shifts, HALO - (CONV_WIDTH - 1) + j, rows, lanes)
                cv_ref[rows, lanes] = acc
        xh, _ = _layer_norm_stats(cv_ref[...])
        ln = xh * lg_ref[...] + lb_ref[...]
        sl = (ln * _sigmoid(ln)).astype(BF16)
        sl_ref[...] = sl
        p = _dot(sl, wp_ref[...]) + bp_ref[...]
        p_ref[...] = p
        gate = gate_ref[...]
        cout_ref[...] = (p * (gate * _sigmoid(gate))).astype(BF16)

    vec = _whole((1, D_CONV))
    return pl.pallas_call(
        body, name="conv_fwd", grid=(t // ROW_BLK,),
        in_specs=[_rows(D_CONV, 0), _rows(D_CONV, 1), _rows(D_CONV, 2), _prev_halo(0), _prev_halo(1),
                  _whole((CONV_WIDTH, D_CONV)), vec, vec, vec, _whole((D_CONV, D_CONV)), vec],
        out_specs=[_rows(D_CONV)] * 4,
        out_shape=[jax.ShapeDtypeStruct((t, D_CONV), BF16), jax.ShapeDtypeStruct((t, D_CONV), F32),
                   jax.ShapeDtypeStruct((t, D_CONV), F32), jax.ShapeDtypeStruct((t, D_CONV), BF16)],
        scratch_shapes=[pltpu.VMEM((HALO + ROW_BLK, D_CONV), F32), pltpu.VMEM((SUBLANES - 1, SHIFT_ROWS, D_CONV), F32)],
        compiler_params=_cparams("parallel"),
    )(pc, pc, pc, pc, pc, conv_w, conv_b, ln_g, ln_b, w_pw2, b_pw2)


def _lower_triangle():
    row = lax.broadcasted_iota(jnp.int32, (ROW_BLK, ROW_BLK), 0)
    col = lax.broadcasted_iota(jnp.int32, (ROW_BLK, ROW_BLK), 1)
    return row > col


def _lower_triangle_t():
    row = lax.broadcasted_iota(jnp.int32, (ROW_BLK, ROW_BLK), 0)
    col = lax.broadcasted_iota(jnp.int32, (ROW_BLK, ROW_BLK), 1)
    return row < col


def _tri_sum(x, umat):
    return _dot(x.astype(BF16), umat)


def _log_gates(z):
    ls = -(jnp.maximum(z, 0.0) + jnp.log(1.0 + jnp.exp(-jnp.abs(z))))
    return ls, z + ls


def _head_lanes(hh):
    lane = lax.broadcasted_iota(jnp.int32, (ROW_BLK, HEAD_BLK), 1)
    return (lane >= HEAD_DIM * hh) & (lane < HEAD_DIM * (hh + 1))


def _merge_heads(acc_ref):
    out = acc_ref[HEADS_PER_BLOCK - 1]
    for hh in range(HEADS_PER_BLOCK - 1):
        out = jnp.where(_head_lanes(hh), acc_ref[hh], out)
    return out


def _qkv_specs(t):
    n_blk = D_SB // HEAD_BLK
    return [pl.BlockSpec((ROW_BLK, HEAD_BLK), lambda hp, i: (i, hp)),
            pl.BlockSpec((t, HEAD_BLK), lambda hp, i: (0, n_blk + hp)),
            pl.BlockSpec((t, HEAD_BLK), lambda hp, i: (0, 2 * n_blk + hp))]


def _carry_spec():
    return pl.BlockSpec((HEADS_PER_BLOCK, ROW_BLK, LANES), lambda hp, i: (hp, i, 0))


def _last_block_rows(t, n_tokens):
    packed_rows = 16
    return -(-(n_tokens - (t - ROW_BLK)) // packed_rows) * packed_rows


def _by_block_rows(i, last_rows, sweep):
    if last_rows == ROW_BLK:
        sweep(ROW_BLK)
        return
    last = pl.num_programs(1) - 1
    pl.when(i < last)(lambda: sweep(ROW_BLK))
    pl.when(i == last)(lambda: sweep(last_rows))


def _attn_fwd(qkv, n_tokens):
    t = qkv.shape[0]
    assert t // ROW_BLK <= LANES

    def body(q_ref, k_ref, v_ref, o_ref, c_ref, acc_ref, run_ref, qm_ref, z_ref):
        i = pl.program_id(1)
        q = q_ref[...]
        heads = range(HEADS_PER_BLOCK)
        for hh in heads:
            qm_ref[hh] = jnp.where(_head_lanes(hh), q, jnp.zeros_like(q)) * jnp.asarray(SB_SCALE, BF16)
        acc_ref[...] = jnp.zeros_like(acc_ref)
        c_ref[...] = jnp.zeros_like(c_ref)
        run_ref[...] = jnp.zeros_like(run_ref)

        def sweep(n_rows):
            rows = pl.ds(0, n_rows)
            lower = _lower_triangle()[:n_rows]
            umat = jnp.where(_lower_triangle(), 1.0, 0.0).astype(BF16)
            lane = lax.broadcasted_iota(jnp.int32, (n_rows, LANES), 1)

            def scores(jb):
                start = pl.multiple_of(jb * ROW_BLK, ROW_BLK)
                kb = k_ref[pl.ds(start, ROW_BLK), :]
                for hh in heads:
                    z_ref[hh, rows] = _dot_nt(qm_ref[hh, rows], kb)

            def block(jb, diagonal):
                start = pl.multiple_of(jb * ROW_BLK, ROW_BLK)
                vb = v_ref[pl.ds(start, ROW_BLK), :]
                logits = []
                for hh in heads:
                    ls, lb = _log_gates(z_ref[hh, rows])
                    if diagonal:
                        ls = jnp.where(lower, ls, 0.0)
                    run = run_ref[hh, rows]
                    if not diagonal:
                        c_ref[hh, rows] = jnp.where(lane == jb, run, c_ref[hh, rows])
                    logits.append(lb + jnp.concatenate([run, run], axis=1) + _tri_sum(ls, umat))
                    run_ref[hh, rows] = run + jnp.sum(ls, axis=1, keepdims=True)
                scores(jnp.maximum(jb - 1, 0))
                for hh in heads:
                    a = jnp.exp(logits[hh])
                    if diagonal:
                        a = jnp.where(lower, a, 0.0)
                    acc_ref[hh, rows] += _dot(a.astype(BF16), vb)

            scores(i)
            block(i, True)

            @pl.loop(0, i)
            def _(n):
                block(i - 1 - n, False)

        _by_block_rows(i, _last_block_rows(t, n_tokens), sweep)
        o_ref[...] = _merge_heads(acc_ref)

    per_head = (HEADS_PER_BLOCK, ROW_BLK, HEAD_BLK)
    return pl.pallas_call(
        body, name="attn_fwd", grid=(D_SB // HEAD_BLK, t // ROW_BLK),
        in_specs=_qkv_specs(t),
        out_specs=[pl.BlockSpec((ROW_BLK, HEAD_BLK), lambda hp, i: (i, hp)), _carry_spec()],
        out_shape=[jax.ShapeDtypeStruct((t, D_SB), F32),
                   jax.ShapeDtypeStruct((D_SB // HEAD_DIM, t, LANES), F32)],
        scratch_shapes=[pltpu.VMEM(per_head, F32), pltpu.VMEM((HEADS_PER_BLOCK, ROW_BLK, LANES), F32),
                        pltpu.VMEM(per_head, BF16), pltpu.VMEM((HEADS_PER_BLOCK, ROW_BLK, ROW_BLK), F32)],
        compiler_params=_cparams("arbitrary", "arbitrary"),
    )(qkv, qkv, qkv)


def _outproj_fwd(h, cout, sraw, sbg, w_out, g_post):
    t = h.shape[0]

    def body(h_ref, c_ref, s_ref, g_ref, w_ref, gp_ref, hn_ref, mixed_ref, mix_ref):
        gate = g_ref[...]
        mix_ref[:, 0:D_CONV] = c_ref[...]
        mix_ref[:, D_CONV:] = (s_ref[...] * (gate * _sigmoid(gate))).astype(BF16)
        mixed = _dot(mix_ref[...], w_ref[...])
        mixed_ref[...] = mixed
        r = lax.rsqrt(jnp.mean(mixed * mixed, axis=-1, keepdims=True) + RMS_EPS)
        hn_ref[...] = h_ref[...] + mixed * r * gp_ref[...]

    return pl.pallas_call(
        body, name="outproj_fwd", grid=(t // ROW_BLK,),
        in_specs=[_rows(D_MODEL), _rows(D_CONV), _rows(D_SB), _rows(D_SB), _whole((D_MODEL, D_MODEL)),
                  _whole((1, D_MODEL))],
        out_specs=[_rows(D_MODEL)] * 3,
        out_shape=[jax.ShapeDtypeStruct((t, D_MODEL), F32), jax.ShapeDtypeStruct((t, D_MODEL), F32),
                   jax.ShapeDtypeStruct((t, D_MODEL), BF16)],
        compiler_params=_cparams("parallel"),
    )(h, cout, sraw, sbg, w_out, g_post)


def _loss_and_grad(h, target, seq):
    t = h.shape[0]

    def body(h_ref, t_ref, loss_ref, dh_ref):
        i = pl.program_id(0)

        @pl.when(i == 0)
        def _():
            loss_ref[...] = jnp.zeros_like(loss_ref)

        row = i * ROW_BLK + lax.broadcasted_iota(jnp.int32, (ROW_BLK, D_MODEL), 0)
        real = (row >= N_META) & (row < N_META + seq)
        diff = jnp.where(real, h_ref[...] - t_ref[...], 0.0)
        sq = jnp.sum(jnp.sum(diff * diff, axis=1, keepdims=True), axis=0, keepdims=True)
        loss_ref[...] += (0.5 / D_MODEL) * sq
        dh_ref[...] = diff * (1.0 / D_MODEL)

    return pl.pallas_call(
        body, name="loss", grid=(t // ROW_BLK,),
        in_specs=[_rows(D_MODEL), _rows(D_MODEL)],
        out_specs=[_whole((1, 1)), _rows(D_MODEL)],
        out_shape=[jax.ShapeDtypeStruct((1, 1), F32), jax.ShapeDtypeStruct((t, D_MODEL), F32)],
        compiler_params=_cparams("arbitrary"),
    )(h, target)


def _outproj_bwd(dh, mixed, g_post, sraw, sbg, w_out_t):
    t = dh.shape[0]

    def body(dh_ref, mixed_ref, gp_ref, s_ref, g_ref, wt_ref, dc_ref, ds_ref, dg_ref, dmb_ref, dgp_ref):
        @pl.when(pl.program_id(0) == 0)
        def _():
            dgp_ref[...] = jnp.zeros_like(dgp_ref)

        mixed = mixed_ref[...]
        r = lax.rsqrt(jnp.mean(mixed * mixed, axis=-1, keepdims=True) + RMS_EPS)
        nh = mixed * r
        dy = dh_ref[...]
        dgp_ref[...] += jnp.sum(dy * nh, axis=0, keepdims=True)
        dn = dy * gp_ref[...]
        dmixed = (r * (dn - nh * jnp.mean(dn * nh, axis=-1, keepdims=True))).astype(BF16)
        dmb_ref[...] = dmixed
        dmix = _dot_nt(dmixed, wt_ref[...])
        dc_ref[...] = dmix[:, 0:D_CONV]
        dsg = dmix[:, D_CONV:]
        gate = g_ref[...]
        sg = _sigmoid(gate)
        ds_ref[...] = dsg * (gate * sg)
        dg_ref[...] = dsg * s_ref[...] * _dsilu(gate, sg)

    return pl.pallas_call(
        body, name="outproj_bwd", grid=(t // ROW_BLK,),
        in_specs=[_rows(D_MODEL), _rows(D_MODEL), _whole((1, D_MODEL)), _rows(D_SB), _rows(D_SB),
                  _whole((D_MODEL, D_MODEL))],
        out_specs=[_rows(D_CONV), _rows(D_SB), _rows(D_SB), _rows(D_MODEL), _whole((1, D_MODEL))],
        out_shape=[jax.ShapeDtypeStruct((t, D_CONV), F32), jax.ShapeDtypeStruct((t, D_SB), F32),
                   jax.ShapeDtypeStruct((t, D_SB), F32), jax.ShapeDtypeStruct((t, D_MODEL), BF16),
                   jax.ShapeDtypeStruct((1, D_MODEL), F32)],
        compiler_params=_cparams("arbitrary"),
    )(dh, mixed, g_post, sraw, sbg, w_out_t)


def _attn_bwd(qkv, carries, do, n_tokens):
    t = qkv.shape[0]

    def body(q_ref, k_ref, v_ref, c_ref, do_ref, dq_ref, dk_ref, dv_ref, acc_ref, seen_ref, qm_ref, dom_ref, z_ref,
             da_ref, dz_ref, a_ref):
        i = pl.program_id(1)

        @pl.when(i == 0)
        def _():
            dk_ref[...] = jnp.zeros_like(dk_ref)
            dv_ref[...] = jnp.zeros_like(dv_ref)

        q = q_ref[...]
        dof = do_ref[...]
        heads = range(HEADS_PER_BLOCK)
        for hh in heads:
            qm_ref[hh] = jnp.where(_head_lanes(hh), q, jnp.zeros_like(q)) * jnp.asarray(SB_SCALE, BF16)
            dom_ref[hh] = jnp.where(_head_lanes(hh), dof, 0.0).astype(BF16)
        acc_ref[...] = jnp.zeros_like(acc_ref)
        seen_ref[...] = jnp.zeros_like(seen_ref)

        def sweep(n_rows):
            rows = pl.ds(0, n_rows)
            lower = _lower_triangle()[:n_rows]
            umat = jnp.where(_lower_triangle(), 1.0, 0.0).astype(BF16)
            umat_t = jnp.where(_lower_triangle_t(), 1.0, 0.0).astype(BF16)
            lane = lax.broadcasted_iota(jnp.int32, (n_rows, LANES), 1)

            def scores(jb):
                start = pl.multiple_of(jb * ROW_BLK, ROW_BLK)
                kb = k_ref[pl.ds(start, ROW_BLK), :]
                for hh in heads:
                    z_ref[hh, rows] = _dot_nt(qm_ref[hh, rows], kb)

            def value_grads(jb):
                start = pl.multiple_of(jb * ROW_BLK, ROW_BLK)
                vb = v_ref[pl.ds(start, ROW_BLK), :]
                for hh in heads:
                    da_ref[hh, rows] = _dot_nt(dom_ref[hh, rows], vb)

            def products(jb, hh):
                start = pl.multiple_of(jb * ROW_BLK, ROW_BLK)
                dzb = dz_ref[hh, rows]
                acc_ref[hh, rows] += _dot(dzb, k_ref[pl.ds(start, ROW_BLK), :])
                dk_ref[pl.ds(start, ROW_BLK), :] += _dot_tn(dzb, qm_ref[hh, rows])
                dv_ref[pl.ds(start, ROW_BLK), :] += _dot_tn(a_ref[hh, rows], dom_ref[hh, rows])

            def block(jb, diagonal):
                before = jnp.maximum(jb - 1, 0)
                lbs, logits = [], []
                for hh in heads:
                    products(before, hh)
                    ls, lb = _log_gates(z_ref[hh, rows])
                    if diagonal:
                        ls = jnp.where(lower, ls, 0.0)
                        logits.append(lb + _tri_sum(ls, umat))
                    else:
                        right = jnp.sum(jnp.where(lane == jb, c_ref[hh, rows], 0.0), axis=1, keepdims=True)
                        logits.append(lb + right + _tri_sum(ls, umat))
                    lbs.append(lb)
                if not diagonal:
                    scores(jb + 1)
                gs, befores = [], []
                for hh in heads:
                    a = jnp.exp(logits[hh])
                    if diagonal:
                        a = jnp.where(lower, a, 0.0)
                    g = da_ref[hh, rows] * a
                    seen = seen_ref[hh, rows]
                    befores.append(jnp.concatenate([seen, seen], axis=1) + _tri_sum(g, umat_t))
                    seen_ref[hh, rows] = seen + jnp.sum(g, axis=1, keepdims=True)
                    a_ref[hh, rows] = a.astype(BF16)
                    gs.append(g)
                if not diagonal:
                    value_grads(jb + 1)
                for hh in heads:
                    dz = gs[hh] - jnp.exp(lbs[hh]) * (gs[hh] + befores[hh])
                    if diagonal:
                        dz = jnp.where(lower, dz, 0.0)
                    dz_ref[hh, rows] = dz.astype(BF16)

            dz_ref[...] = jnp.zeros_like(dz_ref)
            a_ref[...] = jnp.zeros_like(a_ref)
            scores(0)
            value_grads(0)

            @pl.loop(0, i)
            def _(jb):
                block(jb, False)

            block(i, True)
            for hh in heads:
                products(i, hh)

        _by_block_rows(i, _last_block_rows(t, n_tokens), sweep)
        dq_ref[...] = _merge_heads(acc_ref) * SB_SCALE

    blk = pl.BlockSpec((ROW_BLK, HEAD_BLK), lambda hp, i: (i, hp))
    full = pl.BlockSpec((t, HEAD_BLK), lambda hp, i: (0, hp))
    per_head = (HEADS_PER_BLOCK, ROW_BLK, HEAD_BLK)
    return pl.pallas_call(
        body, name="attn_bwd", grid=(D_SB // HEAD_BLK, t // ROW_BLK),
        in_specs=_qkv_specs(t) + [_carry_spec(), blk],
        out_specs=[blk, full, full],
        out_shape=[jax.ShapeDtypeStruct((t, D_SB), F32)] * 3,
        scratch_shapes=[pltpu.VMEM(per_head, F32), pltpu.VMEM((HEADS_PER_BLOCK, ROW_BLK, LANES), F32),
                        pltpu.VMEM(per_head, BF16), pltpu.VMEM(per_head, BF16)]
                       + [pltpu.VMEM((HEADS_PER_BLOCK, ROW_BLK, ROW_BLK), dtype) for dtype in (F32, F32, BF16, BF16)],
        compiler_params=_cparams("arbitrary", "arbitrary"),
    )(qkv, qkv, qkv, carries, do)


def _conv_bwd_rows(dcout, pc, cv, p, ln_g, ln_b, w_pw2_t):
    t = dcout.shape[0]

    def body(dc_ref, gate_ref, cv_ref, p_ref, lg_ref, lb_ref, wt_ref, dcv_ref, dgate_ref, dpb_ref, vec_ref):
        @pl.when(pl.program_id(0) == 0)
        def _():
            vec_ref[...] = jnp.zeros_like(vec_ref)

        dc = dc_ref[...]
        gate = gate_ref[...]
        sg = _sigmoid(gate)
        dp = dc * (gate * sg)
        dgate_ref[...] = dc * p_ref[...] * _dsilu(gate, sg)
        dpb = dp.astype(BF16)
        dpb_ref[...] = dpb
        xh, rstd = _layer_norm_stats(cv_ref[...])
        ln = xh * lg_ref[...] + lb_ref[...]
        s2 = _sigmoid(ln)
        dln = _dot_nt(dpb, wt_ref[...]) * _dsilu(ln, s2)
        dxh = dln * lg_ref[...]
        dcv = rstd * (dxh - jnp.mean(dxh, axis=-1, keepdims=True)
                      - xh * jnp.mean(dxh * xh, axis=-1, keepdims=True))
        dcv_ref[...] = dcv
        vec_ref[0:1, :] += jnp.sum(dp, axis=0, keepdims=True)
        vec_ref[1:2, :] += jnp.sum(dln * xh, axis=0, keepdims=True)
        vec_ref[2:3, :] += jnp.sum(dln, axis=0, keepdims=True)
        vec_ref[3:4, :] += jnp.sum(dcv, axis=0, keepdims=True)

    vec = _whole((1, D_CONV))
    return pl.pallas_call(
        body, name="conv_bwd_rows", grid=(t // ROW_BLK,),
        in_specs=[_rows(D_CONV), _rows(D_CONV, 2), _rows(D_CONV), _rows(D_CONV), vec, vec,
                  _whole((D_CONV, D_CONV))],
        out_specs=[_rows(D_CONV), _rows(D_CONV), _rows(D_CONV), _whole((8, D_CONV))],
        out_shape=[jax.ShapeDtypeStruct((t, D_CONV), F32), jax.ShapeDtypeStruct((t, D_CONV), F32),
                   jax.ShapeDtypeStruct((t, D_CONV), BF16), jax.ShapeDtypeStruct((8, D_CONV), F32)],
        compiler_params=_cparams("arbitrary"),
    )(dcout, pc, cv, p, ln_g, ln_b, w_pw2_t)


def _conv_bwd_taps(dcv, pc, conv_w):
    t = dcv.shape[0]
    n_halo = t // HALO
    per = ROW_BLK // HALO

    def body(d_ref, dn_ref, a_ref, b_ref, ha_ref, hb_ref, cw_ref, da_ref, db_ref, dw_ref, cbuf, dbuf, cshifts, dshifts):
        i = pl.program_id(0)

        @pl.when(i == 0)
        def _():
            dw_ref[...] = jnp.zeros_like(dw_ref)

        _fill_glu(cbuf, i, a_ref, b_ref, ha_ref, hb_ref)
        dbuf[0:ROW_BLK, :] = d_ref[...]
        dbuf[ROW_BLK:ROW_BLK + HALO, :] = jnp.where(i < pl.num_programs(0) - 1, dn_ref[...], 0.0)
        _fill_shifts(cshifts, cbuf)
        _fill_shifts(dshifts, dbuf)
        for lanes in TAP_LANE_TILES:
            for rows in TAP_ROW_CHUNKS:
                acc = jnp.zeros((TAP_ROWS, LANES), F32)
                for j in range(CONV_WIDTH):
                    acc = acc + cw_ref[j:j + 1, lanes] * _window(dbuf, dshifts, CONV_WIDTH - 1 - j, rows, lanes)
                sb = _sigmoid(b_ref[rows, lanes])
                da_ref[rows, lanes] = acc * sb
                db_ref[rows, lanes] = acc * a_ref[rows, lanes] * sb * (1.0 - sb)
            for j in range(CONV_WIDTH):
                acc = jnp.zeros((TAP_ROWS, LANES), F32)
                for rows in TAP_ROW_CHUNKS:
                    acc = acc + d_ref[rows, lanes] * _window(cbuf, cshifts, HALO - (CONV_WIDTH - 1) + j, rows, lanes)
                dw_ref[j:j + 1, lanes] += jnp.sum(acc, axis=0, keepdims=True)

    return pl.pallas_call(
        body, name="conv_bwd_taps", grid=(t // ROW_BLK,),
        in_specs=[_rows(D_CONV),
                  pl.BlockSpec((HALO, D_CONV), lambda i: (jnp.minimum((i + 1) * per, n_halo - 1), 0)),
                  _rows(D_CONV, 0), _rows(D_CONV, 1), _prev_halo(0), _prev_halo(1),
                  _whole((CONV_WIDTH, D_CONV))],
        out_specs=[_rows(D_CONV), _rows(D_CONV), _whole((32, D_CONV))],
        out_shape=[jax.ShapeDtypeStruct((t, D_CONV), F32), jax.ShapeDtypeStruct((t, D_CONV), F32),
                   jax.ShapeDtypeStruct((32, D_CONV), F32)],
        scratch_shapes=[pltpu.VMEM((HALO + ROW_BLK, D_CONV), F32), pltpu.VMEM((ROW_BLK + HALO, D_CONV), F32),
                        pltpu.VMEM((SUBLANES - 1, SHIFT_ROWS, D_CONV), F32),
                        pltpu.VMEM((SUBLANES - 1, SHIFT_ROWS, D_CONV), F32)],
        compiler_params=_cparams("arbitrary"),
    )(dcv, dcv, pc, pc, pc, pc, conv_w)


def _inproj_bwd(dh_out, h, g_pre, pieces, w_in_t):
    t = h.shape[0]

    def body(dh_ref, h_ref, g_ref, *rest):
        piece_refs, (wt_ref, dhin_ref, dproj_ref, dg_ref) = rest[:7], rest[7:]

        @pl.when(pl.program_id(0) == 0)
        def _():
            dg_ref[...] = jnp.zeros_like(dg_ref)

        for k, ref in enumerate(piece_refs):
            dproj_ref[:, 512 * k:512 * (k + 1)] = ref[...].astype(BF16)
        du = _dot_nt(dproj_ref[...], wt_ref[...])
        x = h_ref[...]
        r = lax.rsqrt(jnp.mean(x * x, axis=-1, keepdims=True) + RMS_EPS)
        xh = x * r
        dg_ref[...] += jnp.sum(du * xh, axis=0, keepdims=True)
        dxh = du * g_ref[...]
        dhin_ref[...] = dh_ref[...] + r * (dxh - xh * jnp.mean(dxh * xh, axis=-1, keepdims=True))

    return pl.pallas_call(
        body, name="inproj_bwd", grid=(t // ROW_BLK,),
        in_specs=[_rows(D_MODEL), _rows(D_MODEL), _whole((1, D_MODEL))] + [_rows(512)] * 7
                 + [_whole((D_MODEL, D_IN))],
        out_specs=[_rows(D_MODEL), _rows(D_IN), _whole((1, D_MODEL))],
        out_shape=[jax.ShapeDtypeStruct((t, D_MODEL), F32), jax.ShapeDtypeStruct((t, D_IN), BF16),
                   jax.ShapeDtypeStruct((1, D_MODEL), F32)],
        compiler_params=_cparams("arbitrary"),
    )(dh_out, h, g_pre, *pieces, w_in_t)


def _weight_grad(xb, dyb, name):
    t, k = xb.shape
    n = dyb.shape[1]
    tn = n

    def body(x_ref, dy_ref, o_ref):
        @pl.when(pl.program_id(1) == 0)
        def _():
            o_ref[...] = jnp.zeros_like(o_ref)

        o_ref[...] += _dot_tn(x_ref[...], dy_ref[...])

    return pl.pallas_call(
        body, name=name, grid=(n // tn, t // ROW_BLK),
        in_specs=[pl.BlockSpec((ROW_BLK, k), lambda j, i: (i, 0)), pl.BlockSpec((ROW_BLK, tn), lambda j, i: (i, j))],
        out_specs=pl.BlockSpec((k, tn), lambda j, i: (0, j)),
        out_shape=jax.ShapeDtypeStruct((k, n), F32),
        compiler_params=_cparams("parallel", "arbitrary"),
    )(xb, dyb)


def _position():
    return lax.axis_index("x"), lax.axis_index("y"), lax.axis_index("c")


def _comm_call(body, name, ins, out_shapes):
    n = len(ins)
    hbm = pl.BlockSpec(memory_space=pltpu.HBM)
    return pl.pallas_call(
        functools.partial(body, n), name=name, in_specs=[hbm] * n, out_specs=[hbm] * n, out_shape=out_shapes,
        scratch_shapes=[pltpu.SemaphoreType.DMA((n, N_DEV - 1)), pltpu.SemaphoreType.DMA((n, N_DEV - 1)),
                        pltpu.SemaphoreType.DMA((n,))],
    )(*ins)


def _all_gather(blocks, name):
    def body(n, *refs):
        x_refs, out_refs, (send_sems, recv_sems, local_sems) = refs[:n], refs[n:2 * n], refs[2 * n:]
        x, y, c = _position()
        me, sibling = (x, y, c), (x, y, 1 - c)
        chips = [(1 - x, y), (x, 1 - y), (1 - x, 1 - y)]

        def slot(a, px, py, pc):
            return out_refs[a].at[4 * px + 2 * py + pc]

        def copy(a, k, origin, to, own=False):
            return pltpu.make_async_remote_copy(
                src_ref=x_refs[a] if own else slot(a, *origin), dst_ref=slot(a, *origin),
                send_sem=send_sems.at[a, k], recv_sem=recv_sems.at[a, k], device_id=to, device_id_type=MESH)

        arrays = range(n)
        mine = [pltpu.make_async_copy(x_refs[a], slot(a, *me), local_sems.at[a]) for a in arrays]
        first = [copy(a, 1 + j, me, (*chip, c), own=True) for j, chip in enumerate(chips) for a in arrays]
        first += [copy(a, 0, me, sibling, own=True) for a in arrays]
        for cp in mine + first:
            cp.start()
        passed = []
        for j, chip in enumerate(chips):
            for a in arrays:
                copy(a, 1 + j, (*chip, c), me).wait_recv()
                passed.append(copy(a, 4 + j, (*chip, c), sibling))
                passed[-1].start()
        for a in arrays:
            copy(a, 0, sibling, me).wait_recv()
            for j, chip in enumerate(chips):
                copy(a, 4 + j, (*chip, 1 - c), me).wait_recv()
        for cp in first + passed:
            cp.wait_send()
        for cp in mine:
            cp.wait()

    return _comm_call(body, name, blocks, [jax.ShapeDtypeStruct((N_DEV,) + b.shape, b.dtype) for b in blocks])


def _exchange_copies(g_refs, land_refs, sems, gather):
    x, y, c = _position()
    me = 4 * x + 2 * y + c
    out = []
    for g_ref, land_ref, (send_sem, recv_sem, local_sem) in zip(g_refs, land_refs, sems):
        def mine(slot, g_ref=g_ref):
            return g_ref if gather else g_ref.at[slot]

        def remote(src, dst, dev):
            return pltpu.make_async_remote_copy(src_ref=src, dst_ref=dst, send_sem=send_sem, recv_sem=recv_sem,
                                                device_id=dev, device_id_type=MESH)

        sends = []
        for k in range(1, N_DEV):
            px = 1 - x if k & 4 else x
            py = 1 - y if k & 2 else y
            pc = 1 - c if k & 1 else c
            sends.append(remote(mine(4 * px + 2 * py + pc), land_ref.at[me], (px, py, pc)))
        seven = land_ref.at[pl.ds(0, N_DEV - 1)]
        out.append((pltpu.make_async_copy(mine(me), land_ref.at[me], local_sem), sends, remote(seven, seven, (x, y, c))))
    return out


_HBM = pl.BlockSpec(memory_space=pltpu.HBM)
_SEM = pl.BlockSpec(memory_space=pltpu.SEMAPHORE)
_ORDERED = pltpu.CompilerParams(has_side_effects=pltpu.SideEffectType.DATAFLOW_SIDE_EFFECTING)
SEMS_PER_ARRAY = 3


def _exchange_start(arrays, after, name, gather):
    n = len(arrays)
    n_sems = SEMS_PER_ARRAY * n

    def body(*refs):
        g_refs, land_refs, sems, token = refs[:n], refs[n:2 * n], refs[2 * n + 1:2 * n + 1 + n_sems], refs[-1]
        sems = [sems[SEMS_PER_ARRAY * a:SEMS_PER_ARRAY * (a + 1)] for a in range(n)]
        for local, sends, _ in _exchange_copies(g_refs, land_refs, sems, gather):
            local.start()
            for cp in sends:
                cp.start()
        token[...] = jnp.zeros_like(token)

    buffers = list(arrays) + [lax.empty((N_DEV,) + g.shape if gather else g.shape, g.dtype) for g in arrays]
    outs = pl.pallas_call(
        body, name=name, in_specs=[_HBM] * (2 * n) + [pl.BlockSpec(memory_space=pl.ANY)],
        out_specs=[_SEM] * n_sems + [_HBM] * (2 * n) + [pl.BlockSpec(memory_space=pltpu.VMEM)],
        out_shape=[pltpu.SemaphoreType.DMA(())] * n_sems + [pltpu.HBM(b.shape, b.dtype) for b in buffers]
                  + [jax.ShapeDtypeStruct((8, LANES), F32)],
        input_output_aliases={a: n_sems + a for a in range(2 * n)}, compiler_params=_ORDERED,
    )(*[pltpu.with_memory_space_constraint(b, pltpu.HBM) for b in buffers], after)
    return outs[:n_sems], outs[n_sems:n_sems + n], outs[n_sems + n:n_sems + 2 * n], outs[-1]


def _exchange_wait(sems, arrays, landings, after, name, gather):
    n = len(arrays)
    n_sems = SEMS_PER_ARRAY * n

    def body(*refs):
        g_refs, land_refs, sems = refs[:n], refs[n:2 * n], refs[2 * n:2 * n + n_sems]
        sems = [sems[SEMS_PER_ARRAY * a:SEMS_PER_ARRAY * (a + 1)] for a in range(n)]
        for local, _, all_seven in _exchange_copies(g_refs, land_refs, sems, gather):
            all_seven.wait_recv()
            all_seven.wait_send()
            local.wait()

    buffers = list(arrays) + list(landings)
    outs = pl.pallas_call(
        body, name=name, in_specs=[_HBM] * (2 * n) + [_SEM] * n_sems + [pl.BlockSpec(memory_space=pl.ANY)],
        out_specs=[_HBM] * (2 * n), out_shape=[pltpu.HBM(b.shape, b.dtype) for b in buffers],
        input_output_aliases={a: a for a in range(2 * n)}, compiler_params=_ORDERED,
    )(*buffers, *sems, after)
    return outs[n:]


def _block_rows(r, row_bytes, budget=1 << 20):
    cap = max(8, budget // row_bytes)
    return max(d for d in range(8, min(r, cap) + 1, 8) if r % d == 0)


def _sum_adamw(parts, w, m, v, name):
    n_parts, r, c = parts.shape
    br = _block_rows(r, 4 * c)

    def body(p_ref, w_ref, m_ref, v_ref, g_out, d_out, m_out, v_out):
        g = p_ref[0].astype(F32)
        for s in range(1, n_parts):
            g = g + p_ref[s].astype(F32)
        m_new = ADAM_B1 * m_ref[...] + (1.0 - ADAM_B1) * g
        v_new = ADAM_B2 * v_ref[...] + (1.0 - ADAM_B2) * (g * g)
        m_hat = m_new / (1.0 - ADAM_B1 ** ADAM_STEP)
        v_hat = v_new / (1.0 - ADAM_B2 ** ADAM_STEP)
        g_out[...] = g
        d_out[...] = -ADAM_LR * (m_hat / (jnp.sqrt(v_hat) + ADAM_EPS) + ADAM_WD * w_ref[...])
        m_out[...] = m_new
        v_out[...] = v_new

    row = pl.BlockSpec((br, c), lambda i: (i, 0))
    return pl.pallas_call(
        body, name=name, grid=(r // br,),
        in_specs=[pl.BlockSpec((n_parts, br, c), lambda i: (0, i, 0)), row, row, row],
        out_specs=[row] * 4, out_shape=[jax.ShapeDtypeStruct((r, c), F32)] * 4,
        compiler_params=_cparams("parallel"),
    )(parts, w, m, v)


def _sum_parts(parts, name):
    n_parts, r, c = parts.shape

    def body(p_ref, o_ref):
        g = p_ref[0]
        for s in range(1, n_parts):
            g = g + p_ref[s]
        o_ref[...] = g

    return pl.pallas_call(
        body, name=name, in_specs=[pl.BlockSpec(memory_space=pltpu.VMEM)],
        out_specs=pl.BlockSpec(memory_space=pltpu.VMEM), out_shape=jax.ShapeDtypeStruct((r, c), F32),
    )(parts)


def _pack(arrays):
    flat = jnp.concatenate([a.reshape(-1) for a in arrays])
    pad = -flat.shape[0] % (8 * LANES)
    if pad:
        flat = jnp.pad(flat, (0, pad))
    return flat.reshape(-1, LANES)


def _unpack(buf, shapes):
    flat = buf.reshape(-1)
    out, at = [], 0
    for shape in shapes:
        size = 1
        for d in shape:
            size *= d
        out.append(lax.slice_in_dim(flat, at, at + size).reshape(shape))
        at += size
    return out


def _local_step(x, target, meta, pre_g, post_g, conv_w, conv_b, ln_g, ln_b, b_pw2, weights, ship):
    depth = pre_g.shape[0]
    seq = x.shape[0]
    t = -(-(N_META + seq) // ROW_BLK) * ROW_BLK
    tail = t - N_META - seq
    h = jnp.concatenate([meta, x, jnp.zeros((tail, D_MODEL), F32)], axis=0)
    target = jnp.pad(target, ((N_META, tail), (0, 0)))
    row = lambda a, l: a[l][None, :]

    saved = []
    for l in range(depth):
        w_in, w_pw2, w_out = weights(l, h)
        pc, qkv, sbg, u = _inproj_fwd(h, row(pre_g, l), w_in)
        cout, cv, p, sl = _conv_fwd(pc, conv_w[l], row(conv_b, l), row(ln_g, l), row(ln_b, l), w_pw2, row(b_pw2, l))
        sraw, carries = _attn_fwd(qkv, N_META + seq)
        h_new, mixed, mix = _outproj_fwd(h, cout, sraw, sbg, w_out, row(post_g, l))
        saved.append((h, pc, qkv, sbg, u, cv, p, sl, sraw, carries, mixed, mix, w_in, w_pw2, w_out))
        h = h_new

    loss, dh = _loss_and_grad(h, target, seq)

    grads = {k: [None] * depth for k in ("pre_g", "post_g", "conv_w", "conv_b", "ln_g", "ln_b", "b_pw2")}
    token = jnp.zeros((8, LANES), F32)
    for l in reversed(range(depth)):
        h_in, pc, qkv, sbg, u, cv, p, sl, sraw, carries, mixed, mix, w_in_t, w_pw2_t, w_out_t = saved[l]
        dcout, dsraw, dsbg, dmixed, dg_post = _outproj_bwd(dh, mixed, row(post_g, l) + token[:1, :1], sraw, sbg, w_out_t)
        dq, dk, dv = _attn_bwd(qkv, carries, dsraw, N_META + seq)
        dcv, dgate, dpb, vecs = _conv_bwd_rows(dcout, pc, cv, p, row(ln_g, l), row(ln_b, l), w_pw2_t)
        da, db, dconv_w = _conv_bwd_taps(dcv, pc, conv_w[l])
        dh, dproj, dg_pre = _inproj_bwd(dh, h_in, row(pre_g, l), (da, db, dgate, dq, dk, dv, dsbg), w_in_t)
        token = ship(l, dh, _weight_grad(u, dproj, "w_in_grad"), _weight_grad(sl, dpb, "w_pw2_grad"),
                     _weight_grad(mix, dmixed, "w_out_grad"))
        grads["pre_g"][l] = dg_pre[0]
        grads["post_g"][l] = dg_post[0]
        grads["b_pw2"][l], grads["ln_g"][l], grads["ln_b"][l], grads["conv_b"][l] = vecs[0], vecs[1], vecs[2], vecs[3]
        grads["conv_w"][l] = dconv_w[:CONV_WIDTH]

    grads = {k: jnp.stack(v) for k, v in grads.items()}
    grads["meta"] = dh[:N_META]
    return loss[0, 0], dh[N_META:N_META + seq], grads, token


def _shard_major(full, axis):
    shape = full.shape
    split = full.reshape(shape[:axis] + (N_DEV, shape[axis] // N_DEV) + shape[axis + 1:])
    return jnp.moveaxis(split, axis, 0)


def _whole_from_shards(shards, axis):
    moved = jnp.moveaxis(shards, 0, axis)
    shape = moved.shape
    return moved.reshape(shape[:axis] + (shape[axis] * shape[axis + 1],) + shape[axis + 2:])


def kernel(x, meta_tokens, pre_norm_g, post_norm_g, w_in, conv_w, conv_b, conv_ln_g, conv_ln_b, w_pw2, b_pw2, w_out, loss_target, m_meta_tokens, m_pre_norm_g, m_post_norm_g, m_w_in, m_conv_w, m_conv_b, m_conv_ln_g, m_conv_ln_b, m_w_pw2, m_b_pw2, m_w_out, v_meta_tokens, v_pre_norm_g, v_post_norm_g, v_w_in, v_conv_w, v_conv_b, v_conv_ln_g, v_conv_ln_b, v_w_pw2, v_b_pw2, v_w_out):
    me = 4 * lax.axis_index("x") + 2 * lax.axis_index("y") + lax.axis_index("c")

    depth = w_in.shape[0]
    big = [w.astype(BF16) for w in (w_in, w_pw2, w_out)]
    *first, conv_w_s, meta_s = _all_gather([w[0] for w in big] + [conv_w, meta_tokens], "gather_first_layer")
    *gathering, token = _exchange_start([w[1:] for w in big], meta_s, "gather_start", gather=True)
    conv_w_full = _whole_from_shards(conv_w_s, 2)
    meta_full = _whole_from_shards(meta_s, 1)
    shard_axis = (1, 0, 0)
    later = []

    def weights(l, h):
        if l == 0:
            return [_whole_from_shards(s, axis) for s, axis in zip(first, shard_axis)]
        if not later:
            later.extend(_exchange_wait(*gathering, h, "gather_wait", gather=True))
        return [_whole_from_shards(s[:, l - 1], axis) for s, axis in zip(later, shard_axis)]

    in_flight = [None] * depth

    def ship(l, dh, dw_in, dw_pw2, dw_out):
        slabs = [_shard_major(dw, axis).astype(BF16) for dw, axis in zip((dw_in, dw_pw2, dw_out), shard_axis)]
        *in_flight[l], token = _exchange_start(slabs, dh, f"exchange_start_{l}", gather=False)
        return token

    loss, dx, grads, shipped = _local_step(x[0], loss_target[0], meta_full, pre_norm_g + token[:1, :1], post_norm_g, conv_w_full,
                                  conv_b, conv_ln_g, conv_ln_b, b_pw2, weights, ship)
    small_names = ("pre_g", "post_g", "conv_b", "ln_g", "ln_b", "b_pw2", "conv_w", "meta")
    small_full = [grads[k] for k in small_names] + [loss.reshape(1)]
    *small_in_flight, _ = _exchange_start([_pack(small_full)], shipped, "small_grads_start", gather=True)

    updated = [None] * depth
    done = shipped

    def update_layer(l):
        landed = _exchange_wait(*in_flight[l], done, f"exchange_wait_{l}", gather=False)
        return [_sum_adamw(parts, w[l], m[l], v[l], name) for parts, w, m, v, name in zip(
            landed, (w_in, w_pw2, w_out), (m_w_in, m_w_pw2, m_w_out), (v_w_in, v_w_pw2, v_w_out),
            ("adamw_w_in", "adamw_w_pw2", "adamw_w_out"))]

    for l in reversed(range(1, depth)):
        updated[l] = update_layer(l)
        done = updated[l][0][1]

    gathered, = _exchange_wait(*small_in_flight, done, "small_grads_wait", gather=True)
    summed = _unpack(_sum_parts(gathered, "sum_small_grads"), [a.shape for a in small_full])
    loss = summed[-1][0]
    g_small = dict(zip(small_names, summed))
    g_small["conv_w"] = lax.dynamic_slice_in_dim(g_small["conv_w"], me * conv_w.shape[2], conv_w.shape[2], axis=2)
    g_small["meta"] = lax.dynamic_slice_in_dim(g_small["meta"], me * meta_tokens.shape[1], meta_tokens.shape[1], axis=1)
    small_w = dict(zip(small_names, (pre_norm_g, post_norm_g, conv_b, conv_ln_g, conv_ln_b, b_pw2, conv_w, meta_tokens)))
    small_m = (m_pre_norm_g, m_post_norm_g, m_conv_b, m_conv_ln_g, m_conv_ln_b, m_b_pw2, m_conv_w, m_meta_tokens)
    small_v = (v_pre_norm_g, v_post_norm_g, v_conv_b, v_conv_ln_g, v_conv_ln_b, v_b_pw2, v_conv_w, v_meta_tokens)
    small_shapes = [small_w[k].shape for k in small_names]
    outs = _sum_adamw(_pack([g_small[k] for k in small_names])[None], _pack([small_w[k] for k in small_names]),
                      _pack(small_m), _pack(small_v), "adamw_small_weights")
    g_s, d_s, nm_s, nv_s = [dict(zip(small_names, _unpack(o, small_shapes))) for o in outs]

    done = outs[1]
    updated[0] = update_layer(0)
    (g_w_in, d_w_in, nm_w_in, nv_w_in), (g_w_pw2, d_w_pw2, nm_w_pw2, nv_w_pw2), (g_w_out, d_w_out, nm_w_out, nv_w_out) = [
        [jnp.stack([updated[l][a][k] for l in range(depth)]) for k in range(4)] for a in range(3)]

    def ordered(s, w_in_, w_pw2_, w_out_):
        return (s["meta"], s["pre_g"], s["post_g"], w_in_, s["conv_w"], s["conv_b"], s["ln_g"], s["ln_b"], w_pw2_,
                s["b_pw2"], w_out_)

    return (loss, dx[None], *ordered(g_s, g_w_in, g_w_pw2, g_w_out), *ordered(d_s, d_w_in, d_w_pw2, d_w_out),
            *ordered(nm_s, nm_w_in, nm_w_pw2, nm_w_out), *ordered(nv_s, nv_w_in, nv_w_pw2, nv_w_out))
```

```python
import functools

import jax
import jax.numpy as jnp
from jax import lax
from jax.experimental import pallas as pl
from jax.experimental.pallas import tpu as pltpu

F32 = jnp.float32
BF16 = jnp.bfloat16

D_MODEL = 1024
D_CONV = 512
D_SB = 512
HEAD_DIM = 64
HEADS_PER_BLOCK = 4
HEAD_BLK = HEADS_PER_BLOCK * HEAD_DIM
CONV_WIDTH = 31
N_META = 16
D_IN = 3 * D_CONV + 4 * D_SB
RMS_EPS = 1e-6
LN_EPS = 1e-5
SB_SCALE = HEAD_DIM ** -0.5

ADAM_LR = 0.001
ADAM_B1 = 0.9
ADAM_B2 = 0.999
ADAM_EPS = 1e-08
ADAM_WD = 0.01
ADAM_STEP = 10

N_DEV = 8
LANES = 128
ROW_BLK = 256
HALO = 32
VMEM_LIMIT = 56 * 1024 * 1024
MESH = pl.DeviceIdType.MESH


def _cparams(*sem):
    return pltpu.CompilerParams(dimension_semantics=sem, vmem_limit_bytes=VMEM_LIMIT)


def _rows(n_cols, col=0):
    return pl.BlockSpec((ROW_BLK, n_cols), lambda i, col=col: (i, col))


def _whole(shape):
    return pl.BlockSpec(shape, lambda i: (0,) * len(shape))


def _sigmoid(x):
    return jax.nn.sigmoid(x)


def _dsilu(x, s):
    return s * (1.0 + x * (1.0 - s))


def _dot(a, b):
    return jnp.dot(a, b, preferred_element_type=F32)


def _dot_nt(a, b):
    return lax.dot_general(a, b, (((1,), (1,)), ((), ())), preferred_element_type=F32)


def _dot_tn(a, b):
    return lax.dot_general(a, b, (((0,), (0,)), ((), ())), preferred_element_type=F32)


def _inproj_fwd(h, g_pre, w_in):
    t = h.shape[0]

    def body(h_ref, g_ref, w_ref, pc_ref, qkv_ref, sbg_ref, u_ref):
        x = h_ref[...]
        r = lax.rsqrt(jnp.mean(x * x, axis=-1, keepdims=True) + RMS_EPS)
        u = (x * r * g_ref[...]).astype(BF16)
        u_ref[...] = u
        pc_ref[...] = _dot(u, w_ref[:, 0:1536])
        qkv_ref[...] = _dot(u, w_ref[:, 1536:3072]).astype(BF16)
        sbg_ref[...] = _dot(u, w_ref[:, 3072:3584])

    return pl.pallas_call(
        body, name="inproj_fwd", grid=(t // ROW_BLK,),
        in_specs=[_rows(D_MODEL), _whole((1, D_MODEL)), _whole((D_MODEL, D_IN))],
        out_specs=[_rows(1536), _rows(1536), _rows(D_SB), _rows(D_MODEL)],
        out_shape=[jax.ShapeDtypeStruct((t, 1536), F32), jax.ShapeDtypeStruct((t, 1536), BF16),
                   jax.ShapeDtypeStruct((t, D_SB), F32), jax.ShapeDtypeStruct((t, D_MODEL), BF16)],
        compiler_params=_cparams("parallel"),
    )(h, g_pre, w_in)


def _prev_halo(col):
    per = ROW_BLK // HALO
    return pl.BlockSpec((HALO, D_CONV), lambda i, col=col: (jnp.maximum(i * per - 1, 0), col))


def _fill_glu(buf, i, a_ref, b_ref, ha_ref, hb_ref):
    halo = ha_ref[...] * _sigmoid(hb_ref[...])
    buf[0:HALO, :] = jnp.where(i > 0, halo, 0.0)
    buf[HALO:HALO + ROW_BLK, :] = a_ref[...] * _sigmoid(b_ref[...])


SUBLANES = 8
TAP_ROWS = 64
SHIFT_ROWS = HALO + ROW_BLK - SUBLANES


def _fill_shifts(shifts, buf):
    for b in range(1, SUBLANES):
        shifts[b - 1] = buf[pl.ds(b, SHIFT_ROWS), :]


def _window(buf, shifts, first, rows, lanes):
    whole, part = divmod(first, SUBLANES)
    src = buf if part == 0 else shifts.at[part - 1]
    return src[pl.ds(rows.start + SUBLANES * whole, rows.size), lanes]


TAP_ROW_CHUNKS = [pl.ds(r, TAP_ROWS) for r in range(0, ROW_BLK, TAP_ROWS)]
TAP_LANE_TILES = [pl.ds(c, LANES) for c in range(0, D_CONV, LANES)]


def _layer_norm_stats(cv):
    mu = jnp.mean(cv, axis=-1, keepdims=True)
    xc = cv - mu
    rstd = lax.rsqrt(jnp.mean(xc * xc, axis=-1, keepdims=True) + LN_EPS)
    return xc * rstd, rstd


def _conv_fwd(pc, conv_w, conv_b, ln_g, ln_b, w_pw2, b_pw2):
    t = pc.shape[0]

    def body(a_ref, b_ref, gate_ref, ha_ref, hb_ref, cw_ref, cb_ref, lg_ref, lb_ref, wp_ref, bp_ref,
             cout_ref, cv_ref, p_ref, sl_ref, buf, shifts):
        i = pl.program_id(0)
        _fill_glu(buf, i, a_ref, b_ref, ha_ref, hb_ref)
        _fill_shifts(shifts, buf)
        for lanes in TAP_LANE_TILES:
            for rows in TAP_ROW_CHUNKS:
                acc = jnp.zeros((TAP_ROWS, LANES), F32) + cb_ref[:, lanes]
                for j in range(CONV_WIDTH):
                    acc = acc + cw_ref[j:j + 1, lanes] * _window(buf, shifts, HALO - (CONV_WIDTH - 1) + j, rows, lanes)
                cv_ref[rows, lanes] = acc
        xh, _ = _layer_norm_stats(cv_ref[...])
        ln = xh * lg_ref[...] + lb_ref[...]
        sl = (ln * _sigmoid(ln)).astype(BF16)
        sl_ref[...] = sl
        p = _dot(sl, wp_ref[...]) + bp_ref[...]
        p_ref[...] = p
        gate = gate_ref[...]
        cout_ref[...] = (p * (gate * _sigmoid(gate))).astype(BF16)

    vec = _whole((1, D_CONV))
    return pl.pallas_call(
        body, name="conv_fwd", grid=(t // ROW_BLK,),
        in_specs=[_rows(D_CONV, 0), _rows(D_CONV, 1), _rows(D_CONV, 2), _prev_halo(0), _prev_halo(1),
                  _whole((CONV_WIDTH, D_CONV)), vec, vec, vec, _whole((D_CONV, D_CONV)), vec],
        out_specs=[_rows(D_CONV)] * 4,
        out_shape=[jax.ShapeDtypeStruct((t, D_CONV), BF16), jax.ShapeDtypeStruct((t, D_CONV), F32),
                   jax.ShapeDtypeStruct((t, D_CONV), F32), jax.ShapeDtypeStruct((t, D_CONV), BF16)],
        scratch_shapes=[pltpu.VMEM((HALO + ROW_BLK, D_CONV), F32), pltpu.VMEM((SUBLANES - 1, SHIFT_ROWS, D_CONV), F32)],
        compiler_params=_cparams("parallel"),
    )(pc, pc, pc, pc, pc, conv_w, conv_b, ln_g, ln_b, w_pw2, b_pw2)


def _lower_triangle():
    row = lax.broadcasted_iota(jnp.int32, (ROW_BLK, ROW_BLK), 0)
    col = lax.broadcasted_iota(jnp.int32, (ROW_BLK, ROW_BLK), 1)
    return row > col


def _lower_triangle_t():
    row = lax.broadcasted_iota(jnp.int32, (ROW_BLK, ROW_BLK), 0)
    col = lax.broadcasted_iota(jnp.int32, (ROW_BLK, ROW_BLK), 1)
    return row < col


def _tri_sum(x, umat):
    return _dot(x.astype(BF16), umat)


def _log_gates(z):
    ls = -(jnp.maximum(z, 0.0) + jnp.log(1.0 + jnp.exp(-jnp.abs(z))))
    return ls, z + ls


def _head_lanes(hh):
    lane = lax.broadcasted_iota(jnp.int32, (ROW_BLK, HEAD_BLK), 1)
    return (lane >= HEAD_DIM * hh) & (lane < HEAD_DIM * (hh + 1))


def _merge_heads(acc_ref):
    out = acc_ref[HEADS_PER_BLOCK - 1]
    for hh in range(HEADS_PER_BLOCK - 1):
        out = jnp.where(_head_lanes(hh), acc_ref[hh], out)
    return out


def _qkv_specs(t):
    n_blk = D_SB // HEAD_BLK
    return [pl.BlockSpec((ROW_BLK, HEAD_BLK), lambda hp, i: (i, hp)),
            pl.BlockSpec((t, HEAD_BLK), lambda hp, i: (0, n_blk + hp)),
            pl.BlockSpec((t, HEAD_BLK), lambda hp, i: (0, 2 * n_blk + hp))]


def _carry_spec():
    return pl.BlockSpec((HEADS_PER_BLOCK, ROW_BLK, LANES), lambda hp, i: (hp, i, 0))


def _last_block_rows(t, n_tokens):
    packed_rows = 16
    return -(-(n_tokens - (t - ROW_BLK)) // packed_rows) * packed_rows


def _by_block_rows(i, last_rows, sweep):
    if last_rows == ROW_BLK:
        sweep(ROW_BLK)
        return
    last = pl.num_programs(1) - 1
    pl.when(i < last)(lambda: sweep(ROW_BLK))
    pl.when(i == last)(lambda: sweep(last_rows))


def _attn_fwd(qkv, n_tokens):
    t = qkv.shape[0]
    assert t // ROW_BLK <= LANES

    def body(q_ref, k_ref, v_ref, o_ref, c_ref, acc_ref, run_ref, qm_ref, z_ref):
        i = pl.program_id(1)
        q = q_ref[...]
        heads = range(HEADS_PER_BLOCK)
        for hh in heads:
            qm_ref[hh] = jnp.where(_head_lanes(hh), q, jnp.zeros_like(q)) * jnp.asarray(SB_SCALE, BF16)
        acc_ref[...] = jnp.zeros_like(acc_ref)
        c_ref[...] = jnp.zeros_like(c_ref)
        run_ref[...] = jnp.zeros_like(run_ref)

        def sweep(n_rows):
            rows = pl.ds(0, n_rows)
            lower = _lower_triangle()[:n_rows]
            umat = jnp.where(_lower_triangle(), 1.0, 0.0).astype(BF16)
            lane = lax.broadcasted_iota(jnp.int32, (n_rows, LANES), 1)

            def scores(jb):
                start = pl.multiple_of(jb * ROW_BLK, ROW_BLK)
                kb = k_ref[pl.ds(start, ROW_BLK), :]
                for hh in heads:
                    z_ref[hh, rows] = _dot_nt(qm_ref[hh, rows], kb)

            def block(jb, diagonal):
                start = pl.multiple_of(jb * ROW_BLK, ROW_BLK)
                vb = v_ref[pl.ds(start, ROW_BLK), :]
                logits = []
                for hh in heads:
                    ls, lb = _log_gates(z_ref[hh, rows])
                    if diagonal:
                        ls = jnp.where(lower, ls, 0.0)
                    run = run_ref[hh, rows]
                    if not diagonal:
                        c_ref[hh, rows] = jnp.where(lane == jb, run, c_ref[hh, rows])
                    logits.append(lb + jnp.concatenate([run, run], axis=1) + _tri_sum(ls, umat))
                    run_ref[hh, rows] = run + jnp.sum(ls, axis=1, keepdims=True)
                scores(jnp.maximum(jb - 1, 0))
                for hh in heads:
                    a = jnp.exp(logits[hh])
                    if diagonal:
                        a = jnp.where(lower, a, 0.0)
                    acc_ref[hh, rows] += _dot(a.astype(BF16), vb)

            scores(i)
            block(i, True)

            @pl.loop(0, i)
            def _(n):
                block(i - 1 - n, False)

        _by_block_rows(i, _last_block_rows(t, n_tokens), sweep)
        o_ref[...] = _merge_heads(acc_ref)

    per_head = (HEADS_PER_BLOCK, ROW_BLK, HEAD_BLK)
    return pl.pallas_call(
        body, name="attn_fwd", grid=(D_SB // HEAD_BLK, t // ROW_BLK),
        in_specs=_qkv_specs(t),
        out_specs=[pl.BlockSpec((ROW_BLK, HEAD_BLK), lambda hp, i: (i, hp)), _carry_spec()],
        out_shape=[jax.ShapeDtypeStruct((t, D_SB), F32),
                   jax.ShapeDtypeStruct((D_SB // HEAD_DIM, t, LANES), F32)],
        scratch_shapes=[pltpu.VMEM(per_head, F32), pltpu.VMEM((HEADS_PER_BLOCK, ROW_BLK, LANES), F32),
                        pltpu.VMEM(per_head, BF16), pltpu.VMEM((HEADS_PER_BLOCK, ROW_BLK, ROW_BLK), F32)],
        compiler_params=_cparams("arbitrary", "arbitrary"),
    )(qkv, qkv, qkv)


def _outproj_fwd(h, cout, sraw, sbg, w_out, g_post):
    t = h.shape[0]

    def body(h_ref, c_ref, s_ref, g_ref, w_ref, gp_ref, hn_ref, mixed_ref, mix_ref):
        gate = g_ref[...]
        mix_ref[:, 0:D_CONV] = c_ref[...]
        mix_ref[:, D_CONV:] = (s_ref[...] * (gate * _sigmoid(gate))).astype(BF16)
        mixed = _dot(mix_ref[...], w_ref[...])
        mixed_ref[...] = mixed
        r = lax.rsqrt(jnp.mean(mixed * mixed, axis=-1, keepdims=True) + RMS_EPS)
        hn_ref[...] = h_ref[...] + mixed * r * gp_ref[...]

    return pl.pallas_call(
        body, name="outproj_fwd", grid=(t // ROW_BLK,),
        in_specs=[_rows(D_MODEL), _rows(D_CONV), _rows(D_SB), _rows(D_SB), _whole((D_MODEL, D_MODEL)),
                  _whole((1, D_MODEL))],
        out_specs=[_rows(D_MODEL)] * 3,
        out_shape=[jax.ShapeDtypeStruct((t, D_MODEL), F32), jax.ShapeDtypeStruct((t, D_MODEL), F32),
                   jax.ShapeDtypeStruct((t, D_MODEL), BF16)],
        compiler_params=_cparams("parallel"),
    )(h, cout, sraw, sbg, w_out, g_post)


def _loss_and_grad(h, target, seq):
    t = h.shape[0]

    def body(h_ref, t_ref, loss_ref, dh_ref):
        i = pl.program_id(0)

        @pl.when(i == 0)
        def _():
            loss_ref[...] = jnp.zeros_like(loss_ref)

        row = i * ROW_BLK + lax.broadcasted_iota(jnp.int32, (ROW_BLK, D_MODEL), 0)
        real = (row >= N_META) & (row < N_META + seq)
        diff = jnp.where(real, h_ref[...] - t_ref[...], 0.0)
        sq = jnp.sum(jnp.sum(diff * diff, axis=1, keepdims=True), axis=0, keepdims=True)
        loss_ref[...] += (0.5 / D_MODEL) * sq
        dh_ref[...] = diff * (1.0 / D_MODEL)

    return pl.pallas_call(
        body, name="loss", grid=(t // ROW_BLK,),
        in_specs=[_rows(D_MODEL), _rows(D_MODEL)],
        out_specs=[_whole((1, 1)), _rows(D_MODEL)],
        out_shape=[jax.ShapeDtypeStruct((1, 1), F32), jax.ShapeDtypeStruct((t, D_MODEL), F32)],
        compiler_params=_cparams("arbitrary"),
    )(h, target)


def _outproj_bwd(dh, mixed, g_post, sraw, sbg, w_out_t):
    t = dh.shape[0]

    def body(dh_ref, mixed_ref, gp_ref, s_ref, g_ref, wt_ref, dc_ref, ds_ref, dg_ref, dmb_ref, dgp_ref):
        @pl.when(pl.program_id(0) == 0)
        def _():
            dgp_ref[...] = jnp.zeros_like(dgp_ref)

        mixed = mixed_ref[...]
        r = lax.rsqrt(jnp.mean(mixed * mixed, axis=-1, keepdims=True) + RMS_EPS)
        nh = mixed * r
        dy = dh_ref[...]
        dgp_ref[...] += jnp.sum(dy * nh, axis=0, keepdims=True)
        dn = dy * gp_ref[...]
        dmixed = (r * (dn - nh * jnp.mean(dn * nh, axis=-1, keepdims=True))).astype(BF16)
        dmb_ref[...] = dmixed
        dmix = _dot_nt(dmixed, wt_ref[...])
        dc_ref[...] = dmix[:, 0:D_CONV]
        dsg = dmix[:, D_CONV:]
        gate = g_ref[...]
        sg = _sigmoid(gate)
        ds_ref[...] = dsg * (gate * sg)
        dg_ref[...] = dsg * s_ref[...] * _dsilu(gate, sg)

    return pl.pallas_call(
        body, name="outproj_bwd", grid=(t // ROW_BLK,),
        in_specs=[_rows(D_MODEL), _rows(D_MODEL), _whole((1, D_MODEL)), _rows(D_SB), _rows(D_SB),
                  _whole((D_MODEL, D_MODEL))],
        out_specs=[_rows(D_CONV), _rows(D_SB), _rows(D_SB), _rows(D_MODEL), _whole((1, D_MODEL))],
        out_shape=[jax.ShapeDtypeStruct((t, D_CONV), F32), jax.ShapeDtypeStruct((t, D_SB), F32),
                   jax.ShapeDtypeStruct((t, D_SB), F32), jax.ShapeDtypeStruct((t, D_MODEL), BF16),
                   jax.ShapeDtypeStruct((1, D_MODEL), F32)],
        compiler_params=_cparams("arbitrary"),
    )(dh, mixed, g_post, sraw, sbg, w_out_t)


def _attn_bwd(qkv, carries, do, n_tokens):
    t = qkv.shape[0]

    def body(q_ref, k_ref, v_ref, c_ref, do_ref, dq_ref, dk_ref, dv_ref, acc_ref, seen_ref, qm_ref, dom_ref, z_ref,
             da_ref, dz_ref, a_ref):
        i = pl.program_id(1)

        @pl.when(i == 0)
        def _():
            dk_ref[...] = jnp.zeros_like(dk_ref)
            dv_ref[...] = jnp.zeros_like(dv_ref)

        q = q_ref[...]
        dof = do_ref[...]
        heads = range(HEADS_PER_BLOCK)
        for hh in heads:
            qm_ref[hh] = jnp.where(_head_lanes(hh), q, jnp.zeros_like(q)) * jnp.asarray(SB_SCALE, BF16)
            dom_ref[hh] = jnp.where(_head_lanes(hh), dof, 0.0).astype(BF16)
        acc_ref[...] = jnp.zeros_like(acc_ref)
        seen_ref[...] = jnp.zeros_like(seen_ref)

        def sweep(n_rows):
            rows = pl.ds(0, n_rows)
            lower = _lower_triangle()[:n_rows]
            umat = jnp.where(_lower_triangle(), 1.0, 0.0).astype(BF16)
            umat_t = jnp.where(_lower_triangle_t(), 1.0, 0.0).astype(BF16)
            lane = lax.broadcasted_iota(jnp.int32, (n_rows, LANES), 1)

            def scores(jb):
                start = pl.multiple_of(jb * ROW_BLK, ROW_BLK)
                kb = k_ref[pl.ds(start, ROW_BLK), :]
                for hh in heads:
                    z_ref[hh, rows] = _dot_nt(qm_ref[hh, rows], kb)

            def value_grads(jb):
                start = pl.multiple_of(jb * ROW_BLK, ROW_BLK)
                vb = v_ref[pl.ds(start, ROW_BLK), :]
                for hh in heads:
                    da_ref[hh, rows] = _dot_nt(dom_ref[hh, rows], vb)

            def products(jb, hh):
                start = pl.multiple_of(jb * ROW_BLK, ROW_BLK)
                dzb = dz_ref[hh, rows]
                acc_ref[hh, rows] += _dot(dzb, k_ref[pl.ds(start, ROW_BLK), :])
                dk_ref[pl.ds(start, ROW_BLK), :] += _dot_tn(dzb, qm_ref[hh, rows])
                dv_ref[pl.ds(start, ROW_BLK), :] += _dot_tn(a_ref[hh, rows], dom_ref[hh, rows])

            def block(jb, diagonal):
                before = jnp.maximum(jb - 1, 0)
                lbs, logits = [], []
                for hh in heads:
                    products(before, hh)
                    ls, lb = _log_gates(z_ref[hh, rows])
                    if diagonal:
                        ls = jnp.where(lower, ls, 0.0)
                        logits.append(lb + _tri_sum(ls, umat))
                    else:
                        right = jnp.sum(jnp.where(lane == jb, c_ref[hh, rows], 0.0), axis=1, keepdims=True)
                        logits.append(lb + right + _tri_sum(ls, umat))
                    lbs.append(lb)
                if not diagonal:
                    scores(jb + 1)
                gs, befores = [], []
                for hh in heads:
                    a = jnp.exp(logits[hh])
                    if diagonal:
                        a = jnp.where(lower, a, 0.0)
                    g = da_ref[hh, rows] * a
                    seen = seen_ref[hh, rows]
                    befores.append(jnp.concatenate([seen, seen], axis=1) + _tri_sum(g, umat_t))
                    seen_ref[hh, rows] = seen + jnp.sum(g, axis=1, keepdims=True)
                    a_ref[hh, rows] = a.astype(BF16)
                    gs.append(g)
                if not diagonal:
                    value_grads(jb + 1)
                for hh in heads:
                    dz = gs[hh] - jnp.exp(lbs[hh]) * (gs[hh] + befores[hh])
                    if diagonal:
                        dz = jnp.where(lower, dz, 0.0)
                    dz_ref[hh, rows] = dz.astype(BF16)

            dz_ref[...] = jnp.zeros_like(dz_ref)
            a_ref[...] = jnp.zeros_like(a_ref)
            scores(0)
            value_grads(0)

            @pl.loop(0, i)
            def _(jb):
                block(jb, False)

            block(i, True)
            for hh in heads:
                products(i, hh)

        _by_block_rows(i, _last_block_rows(t, n_tokens), sweep)
        dq_ref[...] = _merge_heads(acc_ref) * SB_SCALE

    blk = pl.BlockSpec((ROW_BLK, HEAD_BLK), lambda hp, i: (i, hp))
    full = pl.BlockSpec((t, HEAD_BLK), lambda hp, i: (0, hp))
    per_head = (HEADS_PER_BLOCK, ROW_BLK, HEAD_BLK)
    return pl.pallas_call(
        body, name="attn_bwd", grid=(D_SB // HEAD_BLK, t // ROW_BLK),
        in_specs=_qkv_specs(t) + [_carry_spec(), blk],
        out_specs=[blk, full, full],
        out_shape=[jax.ShapeDtypeStruct((t, D_SB), F32)] * 3,
        scratch_shapes=[pltpu.VMEM(per_head, F32), pltpu.VMEM((HEADS_PER_BLOCK, ROW_BLK, LANES), F32),
                        pltpu.VMEM(per_head, BF16), pltpu.VMEM(per_head, BF16)]
                       + [pltpu.VMEM((HEADS_PER_BLOCK, ROW_BLK, ROW_BLK), dtype) for dtype in (F32, F32, BF16, BF16)],
        compiler_params=_cparams("arbitrary", "arbitrary"),
    )(qkv, qkv, qkv, carries, do)


def _conv_bwd_rows(dcout, pc, cv, p, ln_g, ln_b, w_pw2_t):
    t = dcout.shape[0]

    def body(dc_ref, gate_ref, cv_ref, p_ref, lg_ref, lb_ref, wt_ref, dcv_ref, dgate_ref, dpb_ref, vec_ref):
        @pl.when(pl.program_id(0) == 0)
        def _():
            vec_ref[...] = jnp.zeros_like(vec_ref)

        dc = dc_ref[...]
        gate = gate_ref[...]
        sg = _sigmoid(gate)
        dp = dc * (gate * sg)
        dgate_ref[...] = dc * p_ref[...] * _dsilu(gate, sg)
        dpb = dp.astype(BF16)
        dpb_ref[...] = dpb
        xh, rstd = _layer_norm_stats(cv_ref[...])
        ln = xh * lg_ref[...] + lb_ref[...]
        s2 = _sigmoid(ln)
        dln = _dot_nt(dpb, wt_ref[...]) * _dsilu(ln, s2)
        dxh = dln * lg_ref[...]
        dcv = rstd * (dxh - jnp.mean(dxh, axis=-1, keepdims=True)
                      - xh * jnp.mean(dxh * xh, axis=-1, keepdims=True))
        dcv_ref[...] = dcv
        vec_ref[0:1, :] += jnp.sum(dp, axis=0, keepdims=True)
        vec_ref[1:2, :] += jnp.sum(dln * xh, axis=0, keepdims=True)
        vec_ref[2:3, :] += jnp.sum(dln, axis=0, keepdims=True)
        vec_ref[3:4, :] += jnp.sum(dcv, axis=0, keepdims=True)

    vec = _whole((1, D_CONV))
    return pl.pallas_call(
        body, name="conv_bwd_rows", grid=(t // ROW_BLK,),
        in_specs=[_rows(D_CONV), _rows(D_CONV, 2), _rows(D_CONV), _rows(D_CONV), vec, vec,
                  _whole((D_CONV, D_CONV))],
        out_specs=[_rows(D_CONV), _rows(D_CONV), _rows(D_CONV), _whole((8, D_CONV))],
        out_shape=[jax.ShapeDtypeStruct((t, D_CONV), F32), jax.ShapeDtypeStruct((t, D_CONV), F32),
                   jax.ShapeDtypeStruct((t, D_CONV), BF16), jax.ShapeDtypeStruct((8, D_CONV), F32)],
        compiler_params=_cparams("arbitrary"),
    )(dcout, pc, cv, p, ln_g, ln_b, w_pw2_t)


def _conv_bwd_taps(dcv, pc, conv_w):
    t = dcv.shape[0]
    n_halo = t // HALO
    per = ROW_BLK // HALO

    def body(d_ref, dn_ref, a_ref, b_ref, ha_ref, hb_ref, cw_ref, da_ref, db_ref, dw_ref, cbuf, dbuf, cshifts, dshifts):
        i = pl.program_id(0)

        @pl.when(i == 0)
        def _():
            dw_ref[...] = jnp.zeros_like(dw_ref)

        _fill_glu(cbuf, i, a_ref, b_ref, ha_ref, hb_ref)
        dbuf[0:ROW_BLK, :] = d_ref[...]
        dbuf[ROW_BLK:ROW_BLK + HALO, :] = jnp.where(i < pl.num_programs(0) - 1, dn_ref[...], 0.0)
        _fill_shifts(cshifts, cbuf)
        _fill_shifts(dshifts, dbuf)
        for lanes in TAP_LANE_TILES:
            for rows in TAP_ROW_CHUNKS:
                acc = jnp.zeros((TAP_ROWS, LANES), F32)
                for j in range(CONV_WIDTH):
                    acc = acc + cw_ref[j:j + 1, lanes] * _window(dbuf, dshifts, CONV_WIDTH - 1 - j, rows, lanes)
                sb = _sigmoid(b_ref[rows, lanes])
                da_ref[rows, lanes] = acc * sb
                db_ref[rows, lanes] = acc * a_ref[rows, lanes] * sb * (1.0 - sb)
            for j in range(CONV_WIDTH):
                acc = jnp.zeros((TAP_ROWS, LANES), F32)
                for rows in TAP_ROW_CHUNKS:
                    acc = acc + d_ref[rows, lanes] * _window(cbuf, cshifts, HALO - (CONV_WIDTH - 1) + j, rows, lanes)
                dw_ref[j:j + 1, lanes] += jnp.sum(acc, axis=0, keepdims=True)

    return pl.pallas_call(
        body, name="conv_bwd_taps", grid=(t // ROW_BLK,),
        in_specs=[_rows(D_CONV),
                  pl.BlockSpec((HALO, D_CONV), lambda i: (jnp.minimum((i + 1) * per, n_halo - 1), 0)),
                  _rows(D_CONV, 0), _rows(D_CONV, 1), _prev_halo(0), _prev_halo(1),
                  _whole((CONV_WIDTH, D_CONV))],
        out_specs=[_rows(D_CONV), _rows(D_CONV), _whole((32, D_CONV))],
        out_shape=[jax.ShapeDtypeStruct((t, D_CONV), F32), jax.ShapeDtypeStruct((t, D_CONV), F32),
                   jax.ShapeDtypeStruct((32, D_CONV), F32)],
        scratch_shapes=[pltpu.VMEM((HALO + ROW_BLK, D_CONV), F32), pltpu.VMEM((ROW_BLK + HALO, D_CONV), F32),
                        pltpu.VMEM((SUBLANES - 1, SHIFT_ROWS, D_CONV), F32),
                        pltpu.VMEM((SUBLANES - 1, SHIFT_ROWS, D_CONV), F32)],
        compiler_params=_cparams("arbitrary"),
    )(dcv, dcv, pc, pc, pc, pc, conv_w)


def _inproj_bwd(dh_out, h, g_pre, pieces, w_in_t):
    t = h.shape[0]

    def body(dh_ref, h_ref, g_ref, *rest):
        piece_refs, (wt_ref, dhin_ref, dproj_ref, dg_ref) = rest[:7], rest[7:]

        @pl.when(pl.program_id(0) == 0)
        def _():
            dg_ref[...] = jnp.zeros_like(dg_ref)

        for k, ref in enumerate(piece_refs):
            dproj_ref[:, 512 * k:512 * (k + 1)] = ref[...].astype(BF16)
        du = _dot_nt(dproj_ref[...], wt_ref[...])
        x = h_ref[...]
        r = lax.rsqrt(jnp.mean(x * x, axis=-1, keepdims=True) + RMS_EPS)
        xh = x * r
        dg_ref[...] += jnp.sum(du * xh, axis=0, keepdims=True)
        dxh = du * g_ref[...]
        dhin_ref[...] = dh_ref[...] + r * (dxh - xh * jnp.mean(dxh * xh, axis=-1, keepdims=True))

    return pl.pallas_call(
        body, name="inproj_bwd", grid=(t // ROW_BLK,),
        in_specs=[_rows(D_MODEL), _rows(D_MODEL), _whole((1, D_MODEL))] + [_rows(512)] * 7
                 + [_whole((D_MODEL, D_IN))],
        out_specs=[_rows(D_MODEL), _rows(D_IN), _whole((1, D_MODEL))],
        out_shape=[jax.ShapeDtypeStruct((t, D_MODEL), F32), jax.ShapeDtypeStruct((t, D_IN), BF16),
                   jax.ShapeDtypeStruct((1, D_MODEL), F32)],
        compiler_params=_cparams("arbitrary"),
    )(dh_out, h, g_pre, *pieces, w_in_t)


def _weight_grad(xb, dyb, name):
    t, k = xb.shape
    n = dyb.shape[1]
    tn = n

    def body(x_ref, dy_ref, o_ref):
        @pl.when(pl.program_id(1) == 0)
        def _():
            o_ref[...] = jnp.zeros_like(o_ref)

        o_ref[...] += _dot_tn(x_ref[...], dy_ref[...])

    return pl.pallas_call(
        body, name=name, grid=(n // tn, t // ROW_BLK),
        in_specs=[pl.BlockSpec((ROW_BLK, k), lambda j, i: (i, 0)), pl.BlockSpec((ROW_BLK, tn), lambda j, i: (i, j))],
        out_specs=pl.BlockSpec((k, tn), lambda j, i: (0, j)),
        out_shape=jax.ShapeDtypeStruct((k, n), F32),
        compiler_params=_cparams("parallel", "arbitrary"),
    )(xb, dyb)


def _position():
    return lax.axis_index("x"), lax.axis_index("y"), lax.axis_index("c")


def _comm_call(body, name, ins, out_shapes):
    n = len(ins)
    hbm = pl.BlockSpec(memory_space=pltpu.HBM)
    return pl.pallas_call(
        functools.partial(body, n), name=name, in_specs=[hbm] * n, out_specs=[hbm] * n, out_shape=out_shapes,
        scratch_shapes=[pltpu.SemaphoreType.DMA((n, N_DEV - 1)), pltpu.SemaphoreType.DMA((n, N_DEV - 1)),
                        pltpu.SemaphoreType.DMA((n,))],
    )(*ins)


def _all_gather(blocks, name):
    def body(n, *refs):
        x_refs, out_refs, (send_sems, recv_sems, local_sems) = refs[:n], refs[n:2 * n], refs[2 * n:]
        x, y, c = _position()
        me, sibling = (x, y, c), (x, y, 1 - c)
        chips = [(1 - x, y), (x, 1 - y), (1 - x, 1 - y)]

        def slot(a, px, py, pc):
            return out_refs[a].at[4 * px + 2 * py + pc]

        def copy(a, k, origin, to, own=False):
            return pltpu.make_async_remote_copy(
                src_ref=x_refs[a] if own else slot(a, *origin), dst_ref=slot(a, *origin),
                send_sem=send_sems.at[a, k], recv_sem=recv_sems.at[a, k], device_id=to, device_id_type=MESH)

        arrays = range(n)
        mine = [pltpu.make_async_copy(x_refs[a], slot(a, *me), local_sems.at[a]) for a in arrays]
        first = [copy(a, 1 + j, me, (*chip, c), own=True) for j, chip in enumerate(chips) for a in arrays]
        first += [copy(a, 0, me, sibling, own=True) for a in arrays]
        for cp in mine + first:
            cp.start()
        passed = []
        for j, chip in enumerate(chips):
            for a in arrays:
                copy(a, 1 + j, (*chip, c), me).wait_recv()
                passed.append(copy(a, 4 + j, (*chip, c), sibling))
                passed[-1].start()
        for a in arrays:
            copy(a, 0, sibling, me).wait_recv()
            for j, chip in enumerate(chips):
                copy(a, 4 + j, (*chip, 1 - c), me).wait_recv()
        for cp in first + passed:
            cp.wait_send()
        for cp in mine:
            cp.wait()

    return _comm_call(body, name, blocks, [jax.ShapeDtypeStruct((N_DEV,) + b.shape, b.dtype) for b in blocks])


def _exchange_copies(g_refs, land_refs, sems, gather):
    x, y, c = _position()
    me = 4 * x + 2 * y + c
    out = []
    for g_ref, land_ref, (send_sem, recv_sem, local_sem) in zip(g_refs, land_refs, sems):
        def mine(slot, g_ref=g_ref):
            return g_ref if gather else g_ref.at[slot]

        def remote(src, dst, dev):
            return pltpu.make_async_remote_copy(src_ref=src, dst_ref=dst, send_sem=send_sem, recv_sem=recv_sem,
                                                device_id=dev, device_id_type=MESH)

        sends = []
        for k in range(1, N_DEV):
            px = 1 - x if k & 4 else x
            py = 1 - y if k & 2 else y
            pc = 1 - c if k & 1 else c
            sends.append(remote(mine(4 * px + 2 * py + pc), land_ref.at[me], (px, py, pc)))
        seven = land_ref.at[pl.ds(0, N_DEV - 1)]
        out.append((pltpu.make_async_copy(mine(me), land_ref.at[me], local_sem), sends, remote(seven, seven, (x, y, c))))
    return out


_HBM = pl.BlockSpec(memory_space=pltpu.HBM)
_SEM = pl.BlockSpec(memory_space=pltpu.SEMAPHORE)
_ORDERED = pltpu.CompilerParams(has_side_effects=pltpu.SideEffectType.DATAFLOW_SIDE_EFFECTING)
SEMS_PER_ARRAY = 3


def _exchange_start(arrays, after, name, gather):
    n = len(arrays)
    n_sems = SEMS_PER_ARRAY * n

    def body(*refs):
        g_refs, land_refs, sems, token = refs[:n], refs[n:2 * n], refs[2 * n + 1:2 * n + 1 + n_sems], refs[-1]
        sems = [sems[SEMS_PER_ARRAY * a:SEMS_PER_ARRAY * (a + 1)] for a in range(n)]
        for local, sends, _ in _exchange_copies(g_refs, land_refs, sems, gather):
            local.start()
            for cp in sends:
                cp.start()
        token[...] = jnp.zeros_like(token)

    buffers = list(arrays) + [lax.empty((N_DEV,) + g.shape if gather else g.shape, g.dtype) for g in arrays]
    outs = pl.pallas_call(
        body, name=name, in_specs=[_HBM] * (2 * n) + [pl.BlockSpec(memory_space=pl.ANY)],
        out_specs=[_SEM] * n_sems + [_HBM] * (2 * n) + [pl.BlockSpec(memory_space=pltpu.VMEM)],
        out_shape=[pltpu.SemaphoreType.DMA(())] * n_sems + [pltpu.HBM(b.shape, b.dtype) for b in buffers]
                  + [jax.ShapeDtypeStruct((8, LANES), F32)],
        input_output_aliases={a: n_sems + a for a in range(2 * n)}, compiler_params=_ORDERED,
    )(*[pltpu.with_memory_space_constraint(b, pltpu.HBM) for b in buffers], after)
    return outs[:n_sems], outs[n_sems:n_sems + n], outs[n_sems + n:n_sems + 2 * n], outs[-1]


def _exchange_wait(sems, arrays, landings, after, name, gather):
    n = len(arrays)
    n_sems = SEMS_PER_ARRAY * n

    def body(*refs):
        g_refs, land_refs, sems = refs[:n], refs[n:2 * n], refs[2 * n:2 * n + n_sems]
        sems = [sems[SEMS_PER_ARRAY * a:SEMS_PER_ARRAY * (a + 1)] for a in range(n)]
        for local, _, all_seven in _exchange_copies(g_refs, land_refs, sems, gather):
            all_seven.wait_recv()
            all_seven.wait_send()
            local.wait()

    buffers = list(arrays) + list(landings)
    outs = pl.pallas_call(
        body, name=name, in_specs=[_HBM] * (2 * n) + [_SEM] * n_sems + [pl.BlockSpec(memory_space=pl.ANY)],
        out_specs=[_HBM] * (2 * n), out_shape=[pltpu.HBM(b.shape, b.dtype) for b in buffers],
        input_output_aliases={a: a for a in range(2 * n)}, compiler_params=_ORDERED,
    )(*buffers, *sems, after)
    return outs[n:]


def _block_rows(r, row_bytes, budget=1 << 20):
    cap = max(8, budget // row_bytes)
    return max(d for d in range(8, min(r, cap) + 1, 8) if r % d == 0)


def _sum_adamw(parts, w, m, v, name):
    n_parts, r, c = parts.shape
    br = _block_rows(r, 4 * c)

    def body(p_ref, w_ref, m_ref, v_ref, g_out, d_out, m_out, v_out):
        g = p_ref[0].astype(F32)
        for s in range(1, n_parts):
            g = g + p_ref[s].astype(F32)
        m_new = ADAM_B1 * m_ref[...] + (1.0 - ADAM_B1) * g
        v_new = ADAM_B2 * v_ref[...] + (1.0 - ADAM_B2) * (g * g)
        m_hat = m_new / (1.0 - ADAM_B1 ** ADAM_STEP)
        v_hat = v_new / (1.0 - ADAM_B2 ** ADAM_STEP)
        g_out[...] = g
        d_out[...] = -ADAM_LR * (m_hat / (jnp.sqrt(v_hat) + ADAM_EPS) + ADAM_WD * w_ref[...])
        m_out[...] = m_new
        v_out[...] = v_new

    row = pl.BlockSpec((br, c), lambda i: (i, 0))
    return pl.pallas_call(
        body, name=name, grid=(r // br,),
        in_specs=[pl.BlockSpec((n_parts, br, c), lambda i: (0, i, 0)), row, row, row],
        out_specs=[row] * 4, out_shape=[jax.ShapeDtypeStruct((r, c), F32)] * 4,
        compiler_params=_cparams("parallel"),
    )(parts, w, m, v)


def _sum_parts(parts, name):
    n_parts, r, c = parts.shape

    def body(p_ref, o_ref):
        g = p_ref[0]
        for s in range(1, n_parts):
            g = g + p_ref[s]
        o_ref[...] = g

    return pl.pallas_call(
        body, name=name, in_specs=[pl.BlockSpec(memory_space=pltpu.VMEM)],
        out_specs=pl.BlockSpec(memory_space=pltpu.VMEM), out_shape=jax.ShapeDtypeStruct((r, c), F32),
    )(parts)


def _pack(arrays):
    flat = jnp.concatenate([a.reshape(-1) for a in arrays])
    pad = -flat.shape[0] % (8 * LANES)
    if pad:
        flat = jnp.pad(flat, (0, pad))
    return flat.reshape(-1, LANES)


def _unpack(buf, shapes):
    flat = buf.reshape(-1)
    out, at = [], 0
    for shape in shapes:
        size = 1
        for d in shape:
            size *= d
        out.append(lax.slice_in_dim(flat, at, at + size).reshape(shape))
        at += size
    return out


def _local_step(x, target, meta, pre_g, post_g, conv_w, conv_b, ln_g, ln_b, b_pw2, weights, ship, ship_small):
    depth = pre_g.shape[0]
    seq = x.shape[0]
    t = -(-(N_META + seq) // ROW_BLK) * ROW_BLK
    tail = t - N_META - seq
    h = jnp.concatenate([meta, x, jnp.zeros((tail, D_MODEL), F32)], axis=0)
    target = jnp.pad(target, ((N_META, tail), (0, 0)))
    row = lambda a, l: a[l][None, :]

    saved = []
    for l in range(depth):
        w_in, w_pw2, w_out = weights(l, h)
        pc, qkv, sbg, u = _inproj_fwd(h, row(pre_g, l), w_in)
        cout, cv, p, sl = _conv_fwd(pc, conv_w[l], row(conv_b, l), row(ln_g, l), row(ln_b, l), w_pw2, row(b_pw2, l))
        sraw, carries = _attn_fwd(qkv, N_META + seq)
        h_new, mixed, mix = _outproj_fwd(h, cout, sraw, sbg, w_out, row(post_g, l))
        saved.append((h, pc, qkv, sbg, u, cv, p, sl, sraw, carries, mixed, mix, w_in, w_pw2, w_out))
        h = h_new

    loss, dh = _loss_and_grad(h, target, seq)

    grads = {k: [None] * depth for k in ("pre_g", "post_g", "conv_w", "conv_b", "ln_g", "ln_b", "b_pw2")}
    token = jnp.zeros((8, LANES), F32)
    for l in reversed(range(depth)):
        h_in, pc, qkv, sbg, u, cv, p, sl, sraw, carries, mixed, mix, w_in_t, w_pw2_t, w_out_t = saved[l]
        dcout, dsraw, dsbg, dmixed, dg_post = _outproj_bwd(dh, mixed, row(post_g, l) + token[:1, :1], sraw, sbg, w_out_t)
        dq, dk, dv = _attn_bwd(qkv, carries, dsraw, N_META + seq)
        dcv, dgate, dpb, vecs = _conv_bwd_rows(dcout, pc, cv, p, row(ln_g, l), row(ln_b, l), w_pw2_t)
        da, db, dconv_w = _conv_bwd_taps(dcv, pc, conv_w[l])
        dh, dproj, dg_pre = _inproj_bwd(dh, h_in, row(pre_g, l), (da, db, dgate, dq, dk, dv, dsbg), w_in_t)
        grads["pre_g"][l] = dg_pre[0]
        grads["post_g"][l] = dg_post[0]
        grads["b_pw2"][l], grads["ln_g"][l], grads["ln_b"][l], grads["conv_b"][l] = vecs[0], vecs[1], vecs[2], vecs[3]
        grads["conv_w"][l] = dconv_w[:CONV_WIDTH]
        after = dh
        if l == 0:
            grads = {k: jnp.stack(v) for k, v in grads.items()}
            grads["meta"] = dh[:N_META]
            after = ship_small(grads, loss[0, 0])
        token = ship(l, after, _weight_grad(u, dproj, "w_in_grad"), _weight_grad(sl, dpb, "w_pw2_grad"),
                     _weight_grad(mix, dmixed, "w_out_grad"))

    return dh[N_META:N_META + seq], token


def _shard_major(full, axis):
    shape = full.shape
    split = full.reshape(shape[:axis] + (N_DEV, shape[axis] // N_DEV) + shape[axis + 1:])
    return jnp.moveaxis(split, axis, 0)


def _whole_from_shards(shards, axis):
    moved = jnp.moveaxis(shards, 0, axis)
    shape = moved.shape
    return moved.reshape(shape[:axis] + (shape[axis] * shape[axis + 1],) + shape[axis + 2:])


def kernel(x, meta_tokens, pre_norm_g, post_norm_g, w_in, conv_w, conv_b, conv_ln_g, conv_ln_b, w_pw2, b_pw2, w_out, loss_target, m_meta_tokens, m_pre_norm_g, m_post_norm_g, m_w_in, m_conv_w, m_conv_b, m_conv_ln_g, m_conv_ln_b, m_w_pw2, m_b_pw2, m_w_out, v_meta_tokens, v_pre_norm_g, v_post_norm_g, v_w_in, v_conv_w, v_conv_b, v_conv_ln_g, v_conv_ln_b, v_w_pw2, v_b_pw2, v_w_out):
    me = 4 * lax.axis_index("x") + 2 * lax.axis_index("y") + lax.axis_index("c")

    depth = w_in.shape[0]
    big = [w.astype(BF16) for w in (w_in, w_pw2, w_out)]
    *first, conv_w_s, meta_s = _all_gather([w[0] for w in big] + [conv_w, meta_tokens], "gather_first_layer")
    *gathering, token = _exchange_start([w[1:] for w in big], meta_s, "gather_start", gather=True)
    conv_w_full = _whole_from_shards(conv_w_s, 2)
    meta_full = _whole_from_shards(meta_s, 1)
    shard_axis = (1, 0, 0)
    later = []

    def weights(l, h):
        if l == 0:
            return [_whole_from_shards(s, axis) for s, axis in zip(first, shard_axis)]
        if not later:
            later.extend(_exchange_wait(*gathering, h, "gather_wait", gather=True))
        return [_whole_from_shards(s[:, l - 1], axis) for s, axis in zip(later, shard_axis)]

    in_flight = [None] * depth

    def ship(l, dh, dw_in, dw_pw2, dw_out):
        slabs = [_shard_major(dw, axis).astype(BF16) for dw, axis in zip((dw_in, dw_pw2, dw_out), shard_axis)]
        *in_flight[l], token = _exchange_start(slabs, dh, f"exchange_start_{l}", gather=False)
        return token

    small_names = ("pre_g", "post_g", "conv_b", "ln_g", "ln_b", "b_pw2", "conv_w", "meta")
    small_in_flight, small_shapes_full = [], []

    def ship_small(grads, loss):
        small_full = [grads[k] for k in small_names] + [loss.reshape(1)]
        small_shapes_full.extend(a.shape for a in small_full)
        *in_flight_now, token = _exchange_start([_pack(small_full)], grads["meta"], "small_grads_start", gather=True)
        small_in_flight.extend(in_flight_now)
        return token

    dx, shipped = _local_step(x[0], loss_target[0], meta_full, pre_norm_g + token[:1, :1], post_norm_g, conv_w_full,
                              conv_b, conv_ln_g, conv_ln_b, b_pw2, weights, ship, ship_small)

    updated = [None] * depth
    done = shipped

    def update_layer(l):
        landed = _exchange_wait(*in_flight[l], done, f"exchange_wait_{l}", gather=False)
        return [_sum_adamw(parts, w[l], m[l], v[l], name) for parts, w, m, v, name in zip(
            landed, (w_in, w_pw2, w_out), (m_w_in, m_w_pw2, m_w_out), (v_w_in, v_w_pw2, v_w_out),
            ("adamw_w_in", "adamw_w_pw2", "adamw_w_out"))]

    for l in reversed(range(1, depth)):
        updated[l] = update_layer(l)
        done = updated[l][0][1]

    gathered, = _exchange_wait(*small_in_flight, done, "small_grads_wait", gather=True)
    summed = _unpack(_sum_parts(gathered, "sum_small_grads"), small_shapes_full)
    loss = summed[-1][0]
    g_small = dict(zip(small_names, summed))
    g_small["conv_w"] = lax.dynamic_slice_in_dim(g_small["conv_w"], me * conv_w.shape[2], conv_w.shape[2], axis=2)
    g_small["meta"] = lax.dynamic_slice_in_dim(g_small["meta"], me * meta_tokens.shape[1], meta_tokens.shape[1], axis=1)
    small_w = dict(zip(small_names, (pre_norm_g, post_norm_g, conv_b, conv_ln_g, conv_ln_b, b_pw2, conv_w, meta_tokens)))
    small_m = (m_pre_norm_g, m_post_norm_g, m_conv_b, m_conv_ln_g, m_conv_ln_b, m_b_pw2, m_conv_w, m_meta_tokens)
    small_v = (v_pre_norm_g, v_post_norm_g, v_conv_b, v_conv_ln_g, v_conv_ln_b, v_b_pw2, v_conv_w, v_meta_tokens)
    small_shapes = [small_w[k].shape for k in small_names]
    outs = _sum_adamw(_pack([g_small[k] for k in small_names])[None], _pack([small_w[k] for k in small_names]),
                      _pack(small_m), _pack(small_v), "adamw_small_weights")
    g_s, d_s, nm_s, nv_s = [dict(zip(small_names, _unpack(o, small_shapes))) for o in outs]

    done = outs[1]
    updated[0] = update_layer(0)
    (g_w_in, d_w_in, nm_w_in, nv_w_in), (g_w_pw2, d_w_pw2, nm_w_pw2, nv_w_pw2), (g_w_out, d_w_out, nm_w_out, nv_w_out) = [
        [jnp.stack([updated[l][a][k] for l in range(depth)]) for k in range(4)] for a in range(3)]

    def ordered(s, w_in_, w_pw2_, w_out_):
        return (s["meta"], s["pre_g"], s["post_g"], w_in_, s["conv_w"], s["conv_b"], s["ln_g"], s["ln_b"], w_pw2_,
                s["b_pw2"], w_out_)

    return (loss, dx[None], *ordered(g_s, g_w_in, g_w_pw2, g_w_out), *ordered(d_s, d_w_in, d_w_pw2, d_w_out),
            *ordered(nm_s, nm_w_in, nm_w_pw2, nm_w_out), *ordered(nv_s, nv_w_in, nv_w_pw2, nv_w_out))
```

```python
import functools

import jax
import jax.numpy as jnp
from jax import lax
from jax.experimental import pallas as pl
from jax.experimental.pallas import tpu as pltpu

F32 = jnp.float32
BF16 = jnp.bfloat16

D_MODEL = 1024
D_CONV = 512
D_SB = 512
HEAD_DIM = 64
HEADS_PER_BLOCK = 4
HEAD_BLK = HEADS_PER_BLOCK * HEAD_DIM
CONV_WIDTH = 31
N_META = 16
D_IN = 3 * D_CONV + 4 * D_SB
RMS_EPS = 1e-6
LN_EPS = 1e-5
SB_SCALE = HEAD_DIM ** -0.5

ADAM_LR = 0.001
ADAM_B1 = 0.9
ADAM_B2 = 0.999
ADAM_EPS = 1e-08
ADAM_WD = 0.01
ADAM_STEP = 10

N_DEV = 8
LANES = 128
ROW_BLK = 256
HALO = 32
VMEM_LIMIT = 56 * 1024 * 1024
MESH = pl.DeviceIdType.MESH


def _cparams(*sem):
    return pltpu.CompilerParams(dimension_semantics=sem, vmem_limit_bytes=VMEM_LIMIT)


def _rows(n_cols, col=0):
    return pl.BlockSpec((ROW_BLK, n_cols), lambda i, col=col: (i, col))


def _whole(shape):
    return pl.BlockSpec(shape, lambda i: (0,) * len(shape))


def _sigmoid(x):
    return jax.nn.sigmoid(x)


def _dsilu(x, s):
    return s * (1.0 + x * (1.0 - s))


def _dot(a, b):
    return jnp.dot(a, b, preferred_element_type=F32)


def _dot_nt(a, b):
    return lax.dot_general(a, b, (((1,), (1,)), ((), ())), preferred_element_type=F32)


def _dot_tn(a, b):
    return lax.dot_general(a, b, (((0,), (0,)), ((), ())), preferred_element_type=F32)


def _inproj_fwd(h, g_pre, w_in):
    t = h.shape[0]

    def body(h_ref, g_ref, w_ref, pc_ref, qkv_ref, sbg_ref, u_ref):
        x = h_ref[...]
        r = lax.rsqrt(jnp.mean(x * x, axis=-1, keepdims=True) + RMS_EPS)
        u = (x * r * g_ref[...]).astype(BF16)
        u_ref[...] = u
        pc_ref[...] = _dot(u, w_ref[:, 0:1536])
        qkv_ref[...] = _dot(u, w_ref[:, 1536:3072]).astype(BF16)
        sbg_ref[...] = _dot(u, w_ref[:, 3072:3584])

    return pl.pallas_call(
        body, name="inproj_fwd", grid=(t // ROW_BLK,),
        in_specs=[_rows(D_MODEL), _whole((1, D_MODEL)), _whole((D_MODEL, D_IN))],
        out_specs=[_rows(1536), _rows(1536), _rows(D_SB), _rows(D_MODEL)],
        out_shape=[jax.ShapeDtypeStruct((t, 1536), F32), jax.ShapeDtypeStruct((t, 1536), BF16),
                   jax.ShapeDtypeStruct((t, D_SB), F32), jax.ShapeDtypeStruct((t, D_MODEL), BF16)],
        compiler_params=_cparams("parallel"),
    )(h, g_pre, w_in)


def _prev_halo(col):
    per = ROW_BLK // HALO
    return pl.BlockSpec((HALO, D_CONV), lambda i, col=col: (jnp.maximum(i * per - 1, 0), col))


def _fill_glu(buf, i, a_ref, b_ref, ha_ref, hb_ref):
    halo = ha_ref[...] * _sigmoid(hb_ref[...])
    buf[0:HALO, :] = jnp.where(i > 0, halo, 0.0)
    buf[HALO:HALO + ROW_BLK, :] = a_ref[...] * _sigmoid(b_ref[...])


SUBLANES = 8
TAP_ROWS = 64
SHIFT_ROWS = HALO + ROW_BLK - SUBLANES


def _fill_shifts(shifts, buf):
    for b in range(1, SUBLANES):
        shifts[b - 1] = buf[pl.ds(b, SHIFT_ROWS), :]


def _window(buf, shifts, first, rows, lanes):
    whole, part = divmod(first, SUBLANES)
    src = buf if part == 0 else shifts.at[part - 1]
    return src[pl.ds(rows.start + SUBLANES * whole, rows.size), lanes]


TAP_ROW_CHUNKS = [pl.ds(r, TAP_ROWS) for r in range(0, ROW_BLK, TAP_ROWS)]
TAP_LANE_TILES = [pl.ds(c, LANES) for c in range(0, D_CONV, LANES)]


def _layer_norm_stats(cv):
    mu = jnp.mean(cv, axis=-1, keepdims=True)
    xc = cv - mu
    rstd = lax.rsqrt(jnp.mean(xc * xc, axis=-1, keepdims=True) + LN_EPS)
    return xc * rstd, rstd


def _conv_fwd(pc, conv_w, conv_b, ln_g, ln_b, w_pw2, b_pw2):
    t = pc.shape[0]

    def body(a_ref, b_ref, gate_ref, ha_ref, hb_ref, cw_ref, cb_ref, lg_ref, lb_ref, wp_ref, bp_ref,
             cout_ref, cv_ref, p_ref, sl_ref, buf, shifts):
        i = pl.program_id(0)
        _fill_glu(buf, i, a_ref, b_ref, ha_ref, hb_ref)
        _fill_shifts(shifts, buf)
        for lanes in TAP_LANE_TILES:
            for rows in TAP_ROW_CHUNKS:
                acc = jnp.zeros((TAP_ROWS, LANES), F32) + cb_ref[:, lanes]
                for j in range(CONV_WIDTH):
                    acc = acc + cw_ref[j:j + 1, lanes] * _window(buf, shifts, HALO - (CONV_WIDTH - 1) + j, rows, lanes)
                cv_ref[rows, lanes] = acc
        xh, _ = _layer_norm_stats(cv_ref[...])
        ln = xh * lg_ref[...] + lb_ref[...]
        sl = (ln * _sigmoid(ln)).astype(BF16)
        sl_ref[...] = sl
        p = _dot(sl, wp_ref[...]) + bp_ref[...]
        p_ref[...] = p
        gate = gate_ref[...]
        cout_ref[...] = (p * (gate * _sigmoid(gate))).astype(BF16)

    vec = _whole((1, D_CONV))
    return pl.pallas_call(
        body, name="conv_fwd", grid=(t // ROW_BLK,),
        in_specs=[_rows(D_CONV, 0), _rows(D_CONV, 1), _rows(D_CONV, 2), _prev_halo(0), _prev_halo(1),
                  _whole((CONV_WIDTH, D_CONV)), vec, vec, vec, _whole((D_CONV, D_CONV)), vec],
        out_specs=[_rows(D_CONV)] * 4,
        out_shape=[jax.ShapeDtypeStruct((t, D_CONV), BF16), jax.ShapeDtypeStruct((t, D_CONV), F32),
                   jax.ShapeDtypeStruct((t, D_CONV), F32), jax.ShapeDtypeStruct((t, D_CONV), BF16)],
        scratch_shapes=[pltpu.VMEM((HALO + ROW_BLK, D_CONV), F32), pltpu.VMEM((SUBLANES - 1, SHIFT_ROWS, D_CONV), F32)],
        compiler_params=_cparams("parallel"),
    )(pc, pc, pc, pc, pc, conv_w, conv_b, ln_g, ln_b, w_pw2, b_pw2)


def _lower_triangle():
    row = lax.broadcasted_iota(jnp.int32, (ROW_BLK, ROW_BLK), 0)
    col = lax.broadcasted_iota(jnp.int32, (ROW_BLK, ROW_BLK), 1)
    return row > col


def _lower_triangle_t():
    row = lax.broadcasted_iota(jnp.int32, (ROW_BLK, ROW_BLK), 0)
    col = lax.broadcasted_iota(jnp.int32, (ROW_BLK, ROW_BLK), 1)
    return row < col


def _tri_sum(x, umat):
    return _dot(x.astype(BF16), umat)


def _log_gates(z):
    ls = -(jnp.maximum(z, 0.0) + jnp.log(1.0 + jnp.exp(-jnp.abs(z))))
    return ls, z + ls


def _head_lanes(hh):
    lane = lax.broadcasted_iota(jnp.int32, (ROW_BLK, HEAD_BLK), 1)
    return (lane >= HEAD_DIM * hh) & (lane < HEAD_DIM * (hh + 1))


def _merge_heads(acc_ref):
    out = acc_ref[HEADS_PER_BLOCK - 1]
    for hh in range(HEADS_PER_BLOCK - 1):
        out = jnp.where(_head_lanes(hh), acc_ref[hh], out)
    return out


def _qkv_specs(t):
    n_blk = D_SB // HEAD_BLK
    return [pl.BlockSpec((ROW_BLK, HEAD_BLK), lambda hp, i: (i, hp)),
            pl.BlockSpec((t, HEAD_BLK), lambda hp, i: (0, n_blk + hp)),
            pl.BlockSpec((t, HEAD_BLK), lambda hp, i: (0, 2 * n_blk + hp))]


def _carry_spec():
    return pl.BlockSpec((HEADS_PER_BLOCK, ROW_BLK, LANES), lambda hp, i: (hp, i, 0))


def _last_block_rows(t, n_tokens):
    packed_rows = 16
    return -(-(n_tokens - (t - ROW_BLK)) // packed_rows) * packed_rows


def _by_block_rows(i, last_rows, sweep):
    if last_rows == ROW_BLK:
        sweep(ROW_BLK)
        return
    last = pl.num_programs(1) - 1
    pl.when(i < last)(lambda: sweep(ROW_BLK))
    pl.when(i == last)(lambda: sweep(last_rows))


def _attn_fwd(qkv, n_tokens):
    t = qkv.shape[0]
    assert t // ROW_BLK <= LANES

    def body(q_ref, k_ref, v_ref, o_ref, c_ref, acc_ref, run_ref, qm_ref, z_ref):
        i = pl.program_id(1)
        q = q_ref[...]
        heads = range(HEADS_PER_BLOCK)
        for hh in heads:
            qm_ref[hh] = jnp.where(_head_lanes(hh), q, jnp.zeros_like(q)) * jnp.asarray(SB_SCALE, BF16)
        acc_ref[...] = jnp.zeros_like(acc_ref)
        c_ref[...] = jnp.zeros_like(c_ref)
        run_ref[...] = jnp.zeros_like(run_ref)

        def sweep(n_rows):
            rows = pl.ds(0, n_rows)
            lower = _lower_triangle()[:n_rows]
            umat = jnp.where(_lower_triangle(), 1.0, 0.0).astype(BF16)
            lane = lax.broadcasted_iota(jnp.int32, (n_rows, LANES), 1)

            def scores(jb):
                start = pl.multiple_of(jb * ROW_BLK, ROW_BLK)
                kb = k_ref[pl.ds(start, ROW_BLK), :]
                for hh in heads:
                    z_ref[hh, rows] = _dot_nt(qm_ref[hh, rows], kb)

            def block(jb, diagonal):
                start = pl.multiple_of(jb * ROW_BLK, ROW_BLK)
                vb = v_ref[pl.ds(start, ROW_BLK), :]
                logits = []
                for hh in heads:
                    ls, lb = _log_gates(z_ref[hh, rows])
                    if diagonal:
                        ls = jnp.where(lower, ls, 0.0)
                    run = run_ref[hh, rows]
                    if not diagonal:
                        c_ref[hh, rows] = jnp.where(lane == jb, run, c_ref[hh, rows])
                    logits.append(lb + jnp.concatenate([run, run], axis=1) + _tri_sum(ls, umat))
                    run_ref[hh, rows] = run + jnp.sum(ls, axis=1, keepdims=True)
                scores(jnp.maximum(jb - 1, 0))
                for hh in heads:
                    a = jnp.exp(logits[hh])
                    if diagonal:
                        a = jnp.where(lower, a, 0.0)
                    acc_ref[hh, rows] += _dot(a.astype(BF16), vb)

            scores(i)
            block(i, True)

            @pl.loop(0, i)
            def _(n):
                block(i - 1 - n, False)

        _by_block_rows(i, _last_block_rows(t, n_tokens), sweep)
        o_ref[...] = _merge_heads(acc_ref)

    per_head = (HEADS_PER_BLOCK, ROW_BLK, HEAD_BLK)
    return pl.pallas_call(
        body, name="attn_fwd", grid=(D_SB // HEAD_BLK, t // ROW_BLK),
        in_specs=_qkv_specs(t),
        out_specs=[pl.BlockSpec((ROW_BLK, HEAD_BLK), lambda hp, i: (i, hp)), _carry_spec()],
        out_shape=[jax.ShapeDtypeStruct((t, D_SB), F32),
                   jax.ShapeDtypeStruct((D_SB // HEAD_DIM, t, LANES), F32)],
        scratch_shapes=[pltpu.VMEM(per_head, F32), pltpu.VMEM((HEADS_PER_BLOCK, ROW_BLK, LANES), F32),
                        pltpu.VMEM(per_head, BF16), pltpu.VMEM((HEADS_PER_BLOCK, ROW_BLK, ROW_BLK), F32)],
        compiler_params=_cparams("arbitrary", "arbitrary"),
    )(qkv, qkv, qkv)


def _outproj_fwd(h, cout, sraw, sbg, w_out, g_post):
    t = h.shape[0]

    def body(h_ref, c_ref, s_ref, g_ref, w_ref, gp_ref, hn_ref, mixed_ref, mix_ref):
        gate = g_ref[...]
        mix_ref[:, 0:D_CONV] = c_ref[...]
        mix_ref[:, D_CONV:] = (s_ref[...] * (gate * _sigmoid(gate))).astype(BF16)
        mixed = _dot(mix_ref[...], w_ref[...])
        mixed_ref[...] = mixed
        r = lax.rsqrt(jnp.mean(mixed * mixed, axis=-1, keepdims=True) + RMS_EPS)
        hn_ref[...] = h_ref[...] + mixed * r * gp_ref[...]

    return pl.pallas_call(
        body, name="outproj_fwd", grid=(t // ROW_BLK,),
        in_specs=[_rows(D_MODEL), _rows(D_CONV), _rows(D_SB), _rows(D_SB), _whole((D_MODEL, D_MODEL)),
                  _whole((1, D_MODEL))],
        out_specs=[_rows(D_MODEL)] * 3,
        out_shape=[jax.ShapeDtypeStruct((t, D_MODEL), F32), jax.ShapeDtypeStruct((t, D_MODEL), F32),
                   jax.ShapeDtypeStruct((t, D_MODEL), BF16)],
        compiler_params=_cparams("parallel"),
    )(h, cout, sraw, sbg, w_out, g_post)


def _loss_and_grad(h, target, seq):
    t = h.shape[0]

    def body(h_ref, t_ref, loss_ref, dh_ref):
        i = pl.program_id(0)

        @pl.when(i == 0)
        def _():
            loss_ref[...] = jnp.zeros_like(loss_ref)

        row = i * ROW_BLK + lax.broadcasted_iota(jnp.int32, (ROW_BLK, D_MODEL), 0)
        real = (row >= N_META) & (row < N_META + seq)
        diff = jnp.where(real, h_ref[...] - t_ref[...], 0.0)
        sq = jnp.sum(jnp.sum(diff * diff, axis=1, keepdims=True), axis=0, keepdims=True)
        loss_ref[...] += (0.5 / D_MODEL) * sq
        dh_ref[...] = diff * (1.0 / D_MODEL)

    return pl.pallas_call(
        body, name="loss", grid=(t // ROW_BLK,),
        in_specs=[_rows(D_MODEL), _rows(D_MODEL)],
        out_specs=[_whole((1, 1)), _rows(D_MODEL)],
        out_shape=[jax.ShapeDtypeStruct((1, 1), F32), jax.ShapeDtypeStruct((t, D_MODEL), F32)],
        compiler_params=_cparams("arbitrary"),
    )(h, target)


def _outproj_bwd(dh, mixed, g_post, sraw, sbg, w_out_t):
    t = dh.shape[0]

    def body(dh_ref, mixed_ref, gp_ref, s_ref, g_ref, wt_ref, dc_ref, ds_ref, dg_ref, dmb_ref, dgp_ref):
        @pl.when(pl.program_id(0) == 0)
        def _():
            dgp_ref[...] = jnp.zeros_like(dgp_ref)

        mixed = mixed_ref[...]
        r = lax.rsqrt(jnp.mean(mixed * mixed, axis=-1, keepdims=True) + RMS_EPS)
        nh = mixed * r
        dy = dh_ref[...]
        dgp_ref[...] += jnp.sum(dy * nh, axis=0, keepdims=True)
        dn = dy * gp_ref[...]
        dmixed = (r * (dn - nh * jnp.mean(dn * nh, axis=-1, keepdims=True))).astype(BF16)
        dmb_ref[...] = dmixed
        dmix = _dot_nt(dmixed, wt_ref[...])
        dc_ref[...] = dmix[:, 0:D_CONV]
        dsg = dmix[:, D_CONV:]
        gate = g_ref[...]
        sg = _sigmoid(gate)
        ds_ref[...] = dsg * (gate * sg)
        dg_ref[...] = (dsg * s_ref[...] * _dsilu(gate, sg)).astype(BF16)

    return pl.pallas_call(
        body, name="outproj_bwd", grid=(t // ROW_BLK,),
        in_specs=[_rows(D_MODEL), _rows(D_MODEL), _whole((1, D_MODEL)), _rows(D_SB), _rows(D_SB),
                  _whole((D_MODEL, D_MODEL))],
        out_specs=[_rows(D_CONV), _rows(D_SB), _rows(D_SB), _rows(D_MODEL), _whole((1, D_MODEL))],
        out_shape=[jax.ShapeDtypeStruct((t, D_CONV), F32), jax.ShapeDtypeStruct((t, D_SB), F32),
                   jax.ShapeDtypeStruct((t, D_SB), BF16), jax.ShapeDtypeStruct((t, D_MODEL), BF16),
                   jax.ShapeDtypeStruct((1, D_MODEL), F32)],
        compiler_params=_cparams("arbitrary"),
    )(dh, mixed, g_post, sraw, sbg, w_out_t)


def _attn_bwd(qkv, carries, do, n_tokens):
    t = qkv.shape[0]

    def body(q_ref, k_ref, v_ref, c_ref, do_ref, dq_ref, dk_ref, dv_ref, acc_ref, seen_ref, qm_ref, dom_ref, z_ref,
             da_ref, dz_ref, a_ref):
        i = pl.program_id(1)

        @pl.when(i == 0)
        def _():
            dk_ref[...] = jnp.zeros_like(dk_ref)
            dv_ref[...] = jnp.zeros_like(dv_ref)

        q = q_ref[...]
        dof = do_ref[...]
        heads = range(HEADS_PER_BLOCK)
        for hh in heads:
            qm_ref[hh] = jnp.where(_head_lanes(hh), q, jnp.zeros_like(q)) * jnp.asarray(SB_SCALE, BF16)
            dom_ref[hh] = jnp.where(_head_lanes(hh), dof, 0.0).astype(BF16)
        acc_ref[...] = jnp.zeros_like(acc_ref)
        seen_ref[...] = jnp.zeros_like(seen_ref)

        def sweep(n_rows):
            rows = pl.ds(0, n_rows)
            lower = _lower_triangle()[:n_rows]
            umat = jnp.where(_lower_triangle(), 1.0, 0.0).astype(BF16)
            umat_t = jnp.where(_lower_triangle_t(), 1.0, 0.0).astype(BF16)
            lane = lax.broadcasted_iota(jnp.int32, (n_rows, LANES), 1)

            def scores(jb):
                start = pl.multiple_of(jb * ROW_BLK, ROW_BLK)
                kb = k_ref[pl.ds(start, ROW_BLK), :]
                for hh in heads:
                    z_ref[hh, rows] = _dot_nt(qm_ref[hh, rows], kb)

            def value_grads(jb):
                start = pl.multiple_of(jb * ROW_BLK, ROW_BLK)
                vb = v_ref[pl.ds(start, ROW_BLK), :]
                for hh in heads:
                    da_ref[hh, rows] = _dot_nt(dom_ref[hh, rows], vb)

            def products(jb, hh):
                start = pl.multiple_of(jb * ROW_BLK, ROW_BLK)
                dzb = dz_ref[hh, rows]
                acc_ref[hh, rows] += _dot(dzb, k_ref[pl.ds(start, ROW_BLK), :])
                dk_ref[pl.ds(start, ROW_BLK), :] += _dot_tn(dzb, qm_ref[hh, rows])
                dv_ref[pl.ds(start, ROW_BLK), :] += _dot_tn(a_ref[hh, rows], dom_ref[hh, rows])

            def block(jb, diagonal):
                before = jnp.maximum(jb - 1, 0)
                lbs, logits = [], []
                for hh in heads:
                    products(before, hh)
                    ls, lb = _log_gates(z_ref[hh, rows])
                    if diagonal:
                        ls = jnp.where(lower, ls, 0.0)
                        logits.append(lb + _tri_sum(ls, umat))
                    else:
                        right = jnp.sum(jnp.where(lane == jb, c_ref[hh, rows], 0.0), axis=1, keepdims=True)
                        logits.append(lb + right + _tri_sum(ls, umat))
                    lbs.append(lb)
                if not diagonal:
                    scores(jb + 1)
                gs, befores = [], []
                for hh in heads:
                    a = jnp.exp(logits[hh])
                    if diagonal:
                        a = jnp.where(lower, a, 0.0)
                    g = da_ref[hh, rows] * a
                    seen = seen_ref[hh, rows]
                    befores.append(jnp.concatenate([seen, seen], axis=1) + _tri_sum(g, umat_t))
                    seen_ref[hh, rows] = seen + jnp.sum(g, axis=1, keepdims=True)
                    a_ref[hh, rows] = a.astype(BF16)
                    gs.append(g)
                if not diagonal:
                    value_grads(jb + 1)
                for hh in heads:
                    dz = gs[hh] - jnp.exp(lbs[hh]) * (gs[hh] + befores[hh])
                    if diagonal:
                        dz = jnp.where(lower, dz, 0.0)
                    dz_ref[hh, rows] = dz.astype(BF16)

            dz_ref[...] = jnp.zeros_like(dz_ref)
            a_ref[...] = jnp.zeros_like(a_ref)
            scores(0)
            value_grads(0)

            @pl.loop(0, i)
            def _(jb):
                block(jb, False)

            block(i, True)
            for hh in heads:
                products(i, hh)

        _by_block_rows(i, _last_block_rows(t, n_tokens), sweep)
        dq_ref[...] = (_merge_heads(acc_ref) * SB_SCALE).astype(BF16)

    blk = pl.BlockSpec((ROW_BLK, HEAD_BLK), lambda hp, i: (i, hp))
    full = pl.BlockSpec((t, HEAD_BLK), lambda hp, i: (0, hp))
    per_head = (HEADS_PER_BLOCK, ROW_BLK, HEAD_BLK)
    return pl.pallas_call(
        body, name="attn_bwd", grid=(D_SB // HEAD_BLK, t // ROW_BLK),
        in_specs=_qkv_specs(t) + [_carry_spec(), blk],
        out_specs=[blk, full, full],
        out_shape=[jax.ShapeDtypeStruct((t, D_SB), BF16)] + [jax.ShapeDtypeStruct((t, D_SB), F32)] * 2,
        scratch_shapes=[pltpu.VMEM(per_head, F32), pltpu.VMEM((HEADS_PER_BLOCK, ROW_BLK, LANES), F32),
                        pltpu.VMEM(per_head, BF16), pltpu.VMEM(per_head, BF16)]
                       + [pltpu.VMEM((HEADS_PER_BLOCK, ROW_BLK, ROW_BLK), dtype) for dtype in (F32, F32, BF16, BF16)],
        compiler_params=_cparams("arbitrary", "arbitrary"),
    )(qkv, qkv, qkv, carries, do)


def _conv_bwd_rows(dcout, pc, cv, p, ln_g, ln_b, w_pw2_t):
    t = dcout.shape[0]

    def body(dc_ref, gate_ref, cv_ref, p_ref, lg_ref, lb_ref, wt_ref, dcv_ref, dgate_ref, dpb_ref, vec_ref):
        @pl.when(pl.program_id(0) == 0)
        def _():
            vec_ref[...] = jnp.zeros_like(vec_ref)

        dc = dc_ref[...]
        gate = gate_ref[...]
        sg = _sigmoid(gate)
        dp = dc * (gate * sg)
        dgate_ref[...] = (dc * p_ref[...] * _dsilu(gate, sg)).astype(BF16)
        dpb = dp.astype(BF16)
        dpb_ref[...] = dpb
        xh, rstd = _layer_norm_stats(cv_ref[...])
        ln = xh * lg_ref[...] + lb_ref[...]
        s2 = _sigmoid(ln)
        dln = _dot_nt(dpb, wt_ref[...]) * _dsilu(ln, s2)
        dxh = dln * lg_ref[...]
        dcv = rstd * (dxh - jnp.mean(dxh, axis=-1, keepdims=True)
                      - xh * jnp.mean(dxh * xh, axis=-1, keepdims=True))
        dcv_ref[...] = dcv
        vec_ref[0:1, :] += jnp.sum(dp, axis=0, keepdims=True)
        vec_ref[1:2, :] += jnp.sum(dln * xh, axis=0, keepdims=True)
        vec_ref[2:3, :] += jnp.sum(dln, axis=0, keepdims=True)
        vec_ref[3:4, :] += jnp.sum(dcv, axis=0, keepdims=True)

    vec = _whole((1, D_CONV))
    return pl.pallas_call(
        body, name="conv_bwd_rows", grid=(t // ROW_BLK,),
        in_specs=[_rows(D_CONV), _rows(D_CONV, 2), _rows(D_CONV), _rows(D_CONV), vec, vec,
                  _whole((D_CONV, D_CONV))],
        out_specs=[_rows(D_CONV), _rows(D_CONV), _rows(D_CONV), _whole((8, D_CONV))],
        out_shape=[jax.ShapeDtypeStruct((t, D_CONV), F32), jax.ShapeDtypeStruct((t, D_CONV), BF16),
                   jax.ShapeDtypeStruct((t, D_CONV), BF16), jax.ShapeDtypeStruct((8, D_CONV), F32)],
        compiler_params=_cparams("arbitrary"),
    )(dcout, pc, cv, p, ln_g, ln_b, w_pw2_t)


def _conv_bwd_taps(dcv, pc, conv_w):
    t = dcv.shape[0]
    n_halo = t // HALO
    per = ROW_BLK // HALO

    def body(d_ref, dn_ref, a_ref, b_ref, ha_ref, hb_ref, cw_ref, da_ref, db_ref, dw_ref, cbuf, dbuf, cshifts, dshifts):
        i = pl.program_id(0)

        @pl.when(i == 0)
        def _():
            dw_ref[...] = jnp.zeros_like(dw_ref)

        _fill_glu(cbuf, i, a_ref, b_ref, ha_ref, hb_ref)
        dbuf[0:ROW_BLK, :] = d_ref[...]
        dbuf[ROW_BLK:ROW_BLK + HALO, :] = jnp.where(i < pl.num_programs(0) - 1, dn_ref[...], 0.0)
        _fill_shifts(cshifts, cbuf)
        _fill_shifts(dshifts, dbuf)
        for lanes in TAP_LANE_TILES:
            for rows in TAP_ROW_CHUNKS:
                acc = jnp.zeros((TAP_ROWS, LANES), F32)
                for j in range(CONV_WIDTH):
                    acc = acc + cw_ref[j:j + 1, lanes] * _window(dbuf, dshifts, CONV_WIDTH - 1 - j, rows, lanes)
                sb = _sigmoid(b_ref[rows, lanes])
                da_ref[rows, lanes] = (acc * sb).astype(BF16)
                db_ref[rows, lanes] = (acc * a_ref[rows, lanes] * sb * (1.0 - sb)).astype(BF16)
            for j in range(CONV_WIDTH):
                acc = jnp.zeros((TAP_ROWS, LANES), F32)
                for rows in TAP_ROW_CHUNKS:
                    acc = acc + d_ref[rows, lanes] * _window(cbuf, cshifts, HALO - (CONV_WIDTH - 1) + j, rows, lanes)
                dw_ref[j:j + 1, lanes] += jnp.sum(acc, axis=0, keepdims=True)

    return pl.pallas_call(
        body, name="conv_bwd_taps", grid=(t // ROW_BLK,),
        in_specs=[_rows(D_CONV),
                  pl.BlockSpec((HALO, D_CONV), lambda i: (jnp.minimum((i + 1) * per, n_halo - 1), 0)),
                  _rows(D_CONV, 0), _rows(D_CONV, 1), _prev_halo(0), _prev_halo(1),
                  _whole((CONV_WIDTH, D_CONV))],
        out_specs=[_rows(D_CONV), _rows(D_CONV), _whole((32, D_CONV))],
        out_shape=[jax.ShapeDtypeStruct((t, D_CONV), BF16), jax.ShapeDtypeStruct((t, D_CONV), BF16),
                   jax.ShapeDtypeStruct((32, D_CONV), F32)],
        scratch_shapes=[pltpu.VMEM((HALO + ROW_BLK, D_CONV), F32), pltpu.VMEM((ROW_BLK + HALO, D_CONV), F32),
                        pltpu.VMEM((SUBLANES - 1, SHIFT_ROWS, D_CONV), F32),
                        pltpu.VMEM((SUBLANES - 1, SHIFT_ROWS, D_CONV), F32)],
        compiler_params=_cparams("arbitrary"),
    )(dcv, dcv, pc, pc, pc, pc, conv_w)


def _inproj_bwd(dh_out, h, g_pre, pieces, w_in_t):
    t = h.shape[0]

    def body(dh_ref, h_ref, g_ref, *rest):
        piece_refs, (wt_ref, dhin_ref, dproj_ref, dg_ref) = rest[:7], rest[7:]

        @pl.when(pl.program_id(0) == 0)
        def _():
            dg_ref[...] = jnp.zeros_like(dg_ref)

        for k, ref in enumerate(piece_refs):
            dproj_ref[:, 512 * k:512 * (k + 1)] = ref[...].astype(BF16)
        du = _dot_nt(dproj_ref[...], wt_ref[...])
        x = h_ref[...]
        r = lax.rsqrt(jnp.mean(x * x, axis=-1, keepdims=True) + RMS_EPS)
        xh = x * r
        dg_ref[...] += jnp.sum(du * xh, axis=0, keepdims=True)
        dxh = du * g_ref[...]
        dhin_ref[...] = dh_ref[...] + r * (dxh - xh * jnp.mean(dxh * xh, axis=-1, keepdims=True))

    return pl.pallas_call(
        body, name="inproj_bwd", grid=(t // ROW_BLK,),
        in_specs=[_rows(D_MODEL), _rows(D_MODEL), _whole((1, D_MODEL))] + [_rows(512)] * 7
                 + [_whole((D_MODEL, D_IN))],
        out_specs=[_rows(D_MODEL), _rows(D_IN), _whole((1, D_MODEL))],
        out_shape=[jax.ShapeDtypeStruct((t, D_MODEL), F32), jax.ShapeDtypeStruct((t, D_IN), BF16),
                   jax.ShapeDtypeStruct((1, D_MODEL), F32)],
        compiler_params=_cparams("arbitrary"),
    )(dh_out, h, g_pre, *pieces, w_in_t)


def _weight_grad(xb, dyb, name):
    t, k = xb.shape
    n = dyb.shape[1]

    def body(x_ref, dy_ref, o_ref, acc_ref):
        i = pl.program_id(0)

        @pl.when(i == 0)
        def _():
            acc_ref[...] = jnp.zeros_like(acc_ref)

        acc_ref[...] += _dot_tn(x_ref[...], dy_ref[...])

        @pl.when(i == pl.num_programs(0) - 1)
        def _():
            o_ref[...] = acc_ref[...].astype(BF16)

    return pl.pallas_call(
        body, name=name, grid=(t // ROW_BLK,),
        in_specs=[_rows(k), _rows(n)], out_specs=_whole((k, n)), out_shape=jax.ShapeDtypeStruct((k, n), BF16),
        scratch_shapes=[pltpu.VMEM((k, n), F32)],
        compiler_params=_cparams("arbitrary"),
    )(xb, dyb)


def _position():
    return lax.axis_index("x"), lax.axis_index("y"), lax.axis_index("c")


def _comm_call(body, name, ins, out_shapes):
    n = len(ins)
    hbm = pl.BlockSpec(memory_space=pltpu.HBM)
    return pl.pallas_call(
        functools.partial(body, n), name=name, in_specs=[hbm] * n, out_specs=[hbm] * n, out_shape=out_shapes,
        scratch_shapes=[pltpu.SemaphoreType.DMA((n, N_DEV - 1)), pltpu.SemaphoreType.DMA((n, N_DEV - 1)),
                        pltpu.SemaphoreType.DMA((n,))],
    )(*ins)


def _all_gather(blocks, name):
    def body(n, *refs):
        x_refs, out_refs, (send_sems, recv_sems, local_sems) = refs[:n], refs[n:2 * n], refs[2 * n:]
        x, y, c = _position()
        me, sibling = (x, y, c), (x, y, 1 - c)
        chips = [(1 - x, y), (x, 1 - y), (1 - x, 1 - y)]

        def slot(a, px, py, pc):
            return out_refs[a].at[4 * px + 2 * py + pc]

        def copy(a, k, origin, to, own=False):
            return pltpu.make_async_remote_copy(
                src_ref=x_refs[a] if own else slot(a, *origin), dst_ref=slot(a, *origin),
                send_sem=send_sems.at[a, k], recv_sem=recv_sems.at[a, k], device_id=to, device_id_type=MESH)

        arrays = range(n)
        mine = [pltpu.make_async_copy(x_refs[a], slot(a, *me), local_sems.at[a]) for a in arrays]
        first = [copy(a, 1 + j, me, (*chip, c), own=True) for j, chip in enumerate(chips) for a in arrays]
        first += [copy(a, 0, me, sibling, own=True) for a in arrays]
        for cp in mine + first:
            cp.start()
        passed = []
        for j, chip in enumerate(chips):
            for a in arrays:
                copy(a, 1 + j, (*chip, c), me).wait_recv()
                passed.append(copy(a, 4 + j, (*chip, c), sibling))
                passed[-1].start()
        for a in arrays:
            copy(a, 0, sibling, me).wait_recv()
            for j, chip in enumerate(chips):
                copy(a, 4 + j, (*chip, 1 - c), me).wait_recv()
        for cp in first + passed:
            cp.wait_send()
        for cp in mine:
            cp.wait()

    return _comm_call(body, name, blocks, [jax.ShapeDtypeStruct((N_DEV,) + b.shape, b.dtype) for b in blocks])


def _exchange_copies(g_refs, land_refs, sems, gather):
    x, y, c = _position()
    me = 4 * x + 2 * y + c
    out = []
    for g_ref, land_ref, (send_sem, recv_sem, local_sem) in zip(g_refs, land_refs, sems):
        def mine(slot, g_ref=g_ref):
            return g_ref if gather else g_ref.at[slot]

        def remote(src, dst, dev):
            return pltpu.make_async_remote_copy(src_ref=src, dst_ref=dst, send_sem=send_sem, recv_sem=recv_sem,
                                                device_id=dev, device_id_type=MESH)

        sends = []
        for k in range(1, N_DEV):
            px = 1 - x if k & 4 else x
            py = 1 - y if k & 2 else y
            pc = 1 - c if k & 1 else c
            sends.append(remote(mine(4 * px + 2 * py + pc), land_ref.at[me], (px, py, pc)))
        seven = land_ref.at[pl.ds(0, N_DEV - 1)]
        out.append((pltpu.make_async_copy(mine(me), land_ref.at[me], local_sem), sends, remote(seven, seven, (x, y, c))))
    return out


_HBM = pl.BlockSpec(memory_space=pltpu.HBM)
_SEM = pl.BlockSpec(memory_space=pltpu.SEMAPHORE)
_ORDERED = pltpu.CompilerParams(has_side_effects=pltpu.SideEffectType.DATAFLOW_SIDE_EFFECTING)
SEMS_PER_ARRAY = 3


def _exchange_start(arrays, after, name, gather):
    n = len(arrays)
    n_sems = SEMS_PER_ARRAY * n

    def body(*refs):
        g_refs, land_refs, sems, token = refs[:n], refs[n:2 * n], refs[2 * n + 1:2 * n + 1 + n_sems], refs[-1]
        sems = [sems[SEMS_PER_ARRAY * a:SEMS_PER_ARRAY * (a + 1)] for a in range(n)]
        for local, sends, _ in _exchange_copies(g_refs, land_refs, sems, gather):
            local.start()
            for cp in sends:
                cp.start()
        token[...] = jnp.zeros_like(token)

    buffers = list(arrays) + [lax.empty((N_DEV,) + g.shape if gather else g.shape, g.dtype) for g in arrays]
    outs = pl.pallas_call(
        body, name=name, in_specs=[_HBM] * (2 * n) + [pl.BlockSpec(memory_space=pl.ANY)],
        out_specs=[_SEM] * n_sems + [_HBM] * (2 * n) + [pl.BlockSpec(memory_space=pltpu.VMEM)],
        out_shape=[pltpu.SemaphoreType.DMA(())] * n_sems + [pltpu.HBM(b.shape, b.dtype) for b in buffers]
                  + [jax.ShapeDtypeStruct((8, LANES), F32)],
        input_output_aliases={a: n_sems + a for a in range(2 * n)}, compiler_params=_ORDERED,
    )(*[pltpu.with_memory_space_constraint(b, pltpu.HBM) for b in buffers], after)
    return outs[:n_sems], outs[n_sems:n_sems + n], outs[n_sems + n:n_sems + 2 * n], outs[-1]


def _exchange_wait(sems, arrays, landings, after, name, gather):
    n = len(arrays)
    n_sems = SEMS_PER_ARRAY * n

    def body(*refs):
        g_refs, land_refs, sems = refs[:n], refs[n:2 * n], refs[2 * n:2 * n + n_sems]
        sems = [sems[SEMS_PER_ARRAY * a:SEMS_PER_ARRAY * (a + 1)] for a in range(n)]
        for local, _, all_seven in _exchange_copies(g_refs, land_refs, sems, gather):
            all_seven.wait_recv()
            all_seven.wait_send()
            local.wait()

    buffers = list(arrays) + list(landings)
    outs = pl.pallas_call(
        body, name=name, in_specs=[_HBM] * (2 * n) + [_SEM] * n_sems + [pl.BlockSpec(memory_space=pl.ANY)],
        out_specs=[_HBM] * (2 * n), out_shape=[pltpu.HBM(b.shape, b.dtype) for b in buffers],
        input_output_aliases={a: a for a in range(2 * n)}, compiler_params=_ORDERED,
    )(*buffers, *sems, after)
    return outs[n:]


def _block_rows(r, row_bytes, budget=1 << 20):
    cap = max(8, budget // row_bytes)
    return max(d for d in range(8, min(r, cap) + 1, 8) if r % d == 0)


def _sum_adamw(parts, w, m, v, name):
    n_parts, r, c = parts.shape
    br = _block_rows(r, 4 * c)

    def body(p_ref, w_ref, m_ref, v_ref, g_out, d_out, m_out, v_out):
        g = p_ref[0].astype(F32)
        for s in range(1, n_parts):
            g = g + p_ref[s].astype(F32)
        m_new = ADAM_B1 * m_ref[...] + (1.0 - ADAM_B1) * g
        v_new = ADAM_B2 * v_ref[...] + (1.0 - ADAM_B2) * (g * g)
        m_hat = m_new / (1.0 - ADAM_B1 ** ADAM_STEP)
        v_hat = v_new / (1.0 - ADAM_B2 ** ADAM_STEP)
        g_out[...] = g
        d_out[...] = -ADAM_LR * (m_hat / (jnp.sqrt(v_hat) + ADAM_EPS) + ADAM_WD * w_ref[...])
        m_out[...] = m_new
        v_out[...] = v_new

    row = pl.BlockSpec((br, c), lambda i: (i, 0))
    return pl.pallas_call(
        body, name=name, grid=(r // br,),
        in_specs=[pl.BlockSpec((n_parts, br, c), lambda i: (0, i, 0)), row, row, row],
        out_specs=[row] * 4, out_shape=[jax.ShapeDtypeStruct((r, c), F32)] * 4,
        compiler_params=_cparams("parallel"),
    )(parts, w, m, v)


def _sum_parts(parts, name):
    n_parts, r, c = parts.shape

    def body(p_ref, o_ref):
        g = p_ref[0]
        for s in range(1, n_parts):
            g = g + p_ref[s]
        o_ref[...] = g

    return pl.pallas_call(
        body, name=name, in_specs=[pl.BlockSpec(memory_space=pltpu.VMEM)],
        out_specs=pl.BlockSpec(memory_space=pltpu.VMEM), out_shape=jax.ShapeDtypeStruct((r, c), F32),
    )(parts)


def _pack(arrays):
    flat = jnp.concatenate([a.reshape(-1) for a in arrays])
    pad = -flat.shape[0] % (8 * LANES)
    if pad:
        flat = jnp.pad(flat, (0, pad))
    return flat.reshape(-1, LANES)


def _unpack(buf, shapes):
    flat = buf.reshape(-1)
    out, at = [], 0
    for shape in shapes:
        size = 1
        for d in shape:
            size *= d
        out.append(lax.slice_in_dim(flat, at, at + size).reshape(shape))
        at += size
    return out


def _local_step(x, target, meta, pre_g, post_g, conv_w, conv_b, ln_g, ln_b, b_pw2, weights, ship, ship_small):
    depth = pre_g.shape[0]
    seq = x.shape[0]
    t = -(-(N_META + seq) // ROW_BLK) * ROW_BLK
    tail = t - N_META - seq
    h = jnp.concatenate([meta, x, jnp.zeros((tail, D_MODEL), F32)], axis=0)
    target = jnp.pad(target, ((N_META, tail), (0, 0)))
    row = lambda a, l: a[l][None, :]

    saved = []
    for l in range(depth):
        w_in, w_pw2, w_out = weights(l, h)
        pc, qkv, sbg, u = _inproj_fwd(h, row(pre_g, l), w_in)
        cout, cv, p, sl = _conv_fwd(pc, conv_w[l], row(conv_b, l), row(ln_g, l), row(ln_b, l), w_pw2, row(b_pw2, l))
        sraw, carries = _attn_fwd(qkv, N_META + seq)
        h_new, mixed, mix = _outproj_fwd(h, cout, sraw, sbg, w_out, row(post_g, l))
        saved.append((h, pc, qkv, sbg, u, cv, p, sl, sraw, carries, mixed, mix, w_in, w_pw2, w_out))
        h = h_new

    loss, dh = _loss_and_grad(h, target, seq)

    grads = {k: [None] * depth for k in ("pre_g", "post_g", "conv_w", "conv_b", "ln_g", "ln_b", "b_pw2")}
    token = jnp.zeros((8, LANES), F32)
    for l in reversed(range(depth)):
        h_in, pc, qkv, sbg, u, cv, p, sl, sraw, carries, mixed, mix, w_in_t, w_pw2_t, w_out_t = saved[l]
        dcout, dsraw, dsbg, dmixed, dg_post = _outproj_bwd(dh, mixed, row(post_g, l) + token[:1, :1], sraw, sbg, w_out_t)
        dq, dk, dv = _attn_bwd(qkv, carries, dsraw, N_META + seq)
        dcv, dgate, dpb, vecs = _conv_bwd_rows(dcout, pc, cv, p, row(ln_g, l), row(ln_b, l), w_pw2_t)
        da, db, dconv_w = _conv_bwd_taps(dcv, pc, conv_w[l])
        dh, dproj, dg_pre = _inproj_bwd(dh, h_in, row(pre_g, l), (da, db, dgate, dq, dk, dv, dsbg), w_in_t)
        grads["pre_g"][l] = dg_pre[0]
        grads["post_g"][l] = dg_post[0]
        grads["b_pw2"][l], grads["ln_g"][l], grads["ln_b"][l], grads["conv_b"][l] = vecs[0], vecs[1], vecs[2], vecs[3]
        grads["conv_w"][l] = dconv_w[:CONV_WIDTH]
        after = dh
        if l == 0:
            grads = {k: jnp.stack(v) for k, v in grads.items()}
            grads["meta"] = dh[:N_META]
            after = ship_small(grads, loss[0, 0])
        token = ship(l, after, _weight_grad(u, dproj, "w_in_grad"), _weight_grad(sl, dpb, "w_pw2_grad"),
                     _weight_grad(mix, dmixed, "w_out_grad"))

    return dh[N_META:N_META + seq], token


def _shard_major(full, axis):
    shape = full.shape
    split = full.reshape(shape[:axis] + (N_DEV, shape[axis] // N_DEV) + shape[axis + 1:])
    return jnp.moveaxis(split, axis, 0)


def _whole_from_shards(shards, axis):
    moved = jnp.moveaxis(shards, 0, axis)
    shape = moved.shape
    return moved.reshape(shape[:axis] + (shape[axis] * shape[axis + 1],) + shape[axis + 2:])


def kernel(x, meta_tokens, pre_norm_g, post_norm_g, w_in, conv_w, conv_b, conv_ln_g, conv_ln_b, w_pw2, b_pw2, w_out, loss_target, m_meta_tokens, m_pre_norm_g, m_post_norm_g, m_w_in, m_conv_w, m_conv_b, m_conv_ln_g, m_conv_ln_b, m_w_pw2, m_b_pw2, m_w_out, v_meta_tokens, v_pre_norm_g, v_post_norm_g, v_w_in, v_conv_w, v_conv_b, v_conv_ln_g, v_conv_ln_b, v_w_pw2, v_b_pw2, v_w_out):
    me = 4 * lax.axis_index("x") + 2 * lax.axis_index("y") + lax.axis_index("c")

    depth = w_in.shape[0]
    big = [w.astype(BF16) for w in (w_in, w_pw2, w_out)]
    *first, conv_w_s, meta_s = _all_gather([w[0] for w in big] + [conv_w, meta_tokens], "gather_first_layer")
    *gathering, token = _exchange_start([w[1:] for w in big], meta_s, "gather_start", gather=True)
    conv_w_full = _whole_from_shards(conv_w_s, 2)
    meta_full = _whole_from_shards(meta_s, 1)
    shard_axis = (1, 0, 0)
    later = []

    def weights(l, h):
        if l == 0:
            return [_whole_from_shards(s, axis) for s, axis in zip(first, shard_axis)]
        if not later:
            later.extend(_exchange_wait(*gathering, h, "gather_wait", gather=True))
        return [_whole_from_shards(s[:, l - 1], axis) for s, axis in zip(later, shard_axis)]

    in_flight = [None] * depth

    def ship(l, dh, dw_in, dw_pw2, dw_out):
        slabs = [_shard_major(dw, axis) for dw, axis in zip((dw_in, dw_pw2, dw_out), shard_axis)]
        *in_flight[l], token = _exchange_start(slabs, dh, f"exchange_start_{l}", gather=False)
        return token

    small_names = ("pre_g", "post_g", "conv_b", "ln_g", "ln_b", "b_pw2", "conv_w", "meta")
    small_in_flight, small_shapes_full = [], []

    def ship_small(grads, loss):
        small_full = [grads[k] for k in small_names] + [loss.reshape(1)]
        small_shapes_full.extend(a.shape for a in small_full)
        *in_flight_now, token = _exchange_start([_pack(small_full)], grads["meta"], "small_grads_start", gather=True)
        small_in_flight.extend(in_flight_now)
        return token

    dx, shipped = _local_step(x[0], loss_target[0], meta_full, pre_norm_g + token[:1, :1], post_norm_g, conv_w_full,
                              conv_b, conv_ln_g, conv_ln_b, b_pw2, weights, ship, ship_small)

    updated = [None] * depth
    done = shipped

    def update_layer(l):
        landed = _exchange_wait(*in_flight[l], done, f"exchange_wait_{l}", gather=False)
        return [_sum_adamw(parts, w[l], m[l], v[l], name) for parts, w, m, v, name in zip(
            landed, (w_in, w_pw2, w_out), (m_w_in, m_w_pw2, m_w_out), (v_w_in, v_w_pw2, v_w_out),
            ("adamw_w_in", "adamw_w_pw2", "adamw_w_out"))]

    for l in reversed(range(1, depth)):
        updated[l] = update_layer(l)
        done = updated[l][0][1]

    gathered, = _exchange_wait(*small_in_flight, done, "small_grads_wait", gather=True)
    summed = _unpack(_sum_parts(gathered, "sum_small_grads"), small_shapes_full)
    loss = summed[-1][0]
    g_small = dict(zip(small_names, summed))
    g_small["conv_w"] = lax.dynamic_slice_in_dim(g_small["conv_w"], me * conv_w.shape[2], conv_w.shape[2], axis=2)
    g_small["meta"] = lax.dynamic_slice_in_dim(g_small["meta"], me * meta_tokens.shape[1], meta_tokens.shape[1], axis=1)
    small_w = dict(zip(small_names, (pre_norm_g, post_norm_g, conv_b, conv_ln_g, conv_ln_b, b_pw2, conv_w, meta_tokens)))
    small_m = (m_pre_norm_g, m_post_norm_g, m_conv_b, m_conv_ln_g, m_conv_ln_b, m_b_pw2, m_conv_w, m_meta_tokens)
    small_v = (v_pre_norm_g, v_post_norm_g, v_conv_b, v_conv_ln_g, v_conv_ln_b, v_b_pw2, v_conv_w, v_meta_tokens)
    small_shapes = [small_w[k].shape for k in small_names]
    outs = _sum_adamw(_pack([g_small[k] for k in small_names])[None], _pack([small_w[k] for k in small_names]),
                      _pack(small_m), _pack(small_v), "adamw_small_weights")
    g_s, d_s, nm_s, nv_s = [dict(zip(small_names, _unpack(o, small_shapes))) for o in outs]

    done = outs[1]
    updated[0] = update_layer(0)
    (g_w_in, d_w_in, nm_w_in, nv_w_in), (g_w_pw2, d_w_pw2, nm_w_pw2, nv_w_pw2), (g_w_out, d_w_out, nm_w_out, nv_w_out) = [
        [jnp.stack([updated[l][a][k] for l in range(depth)]) for k in range(4)] for a in range(3)]

    def ordered(s, w_in_, w_pw2_, w_out_):
        return (s["meta"], s["pre_g"], s["post_g"], w_in_, s["conv_w"], s["conv_b"], s["ln_g"], s["ln_b"], w_pw2_,
                s["b_pw2"], w_out_)

    return (loss, dx[None], *ordered(g_s, g_w_in, g_w_pw2, g_w_out), *ordered(d_s, d_w_in, d_w_pw2, d_w_out),
            *ordered(nm_s, nm_w_in, nm_w_pw2, nm_w_out), *ordered(nv_s, nv_w_in, nv_w_pw2, nv_w_out))
```

```python
import functools

import jax
import jax.numpy as jnp
from jax import lax
from jax.experimental import pallas as pl
from jax.experimental.pallas import tpu as pltpu

F32 = jnp.float32
BF16 = jnp.bfloat16

D_MODEL = 1024
D_CONV = 512
D_SB = 512
HEAD_DIM = 64
HEADS_PER_BLOCK = 4
HEAD_BLK = HEADS_PER_BLOCK * HEAD_DIM
CONV_WIDTH = 31
N_META = 16
D_IN = 3 * D_CONV + 4 * D_SB
RMS_EPS = 1e-6
LN_EPS = 1e-5
SB_SCALE = HEAD_DIM ** -0.5

ADAM_LR = 0.001
ADAM_B1 = 0.9
ADAM_B2 = 0.999
ADAM_EPS = 1e-08
ADAM_WD = 0.01
ADAM_STEP = 10

N_DEV = 8
LANES = 128
ROW_BLK = 256
HALO = 32
VMEM_LIMIT = 56 * 1024 * 1024
MESH = pl.DeviceIdType.MESH


def _cparams(*sem):
    return pltpu.CompilerParams(dimension_semantics=sem, vmem_limit_bytes=VMEM_LIMIT)


def _rows(n_cols, col=0):
    return pl.BlockSpec((ROW_BLK, n_cols), lambda i, col=col: (i, col))


def _whole(shape):
    return pl.BlockSpec(shape, lambda i: (0,) * len(shape))


def _sigmoid(x):
    return jax.nn.sigmoid(x)


def _dsilu(x, s):
    return s * (1.0 + x * (1.0 - s))


def _dot(a, b):
    return jnp.dot(a, b, preferred_element_type=F32)


def _dot_nt(a, b):
    return lax.dot_general(a, b, (((1,), (1,)), ((), ())), preferred_element_type=F32)


def _dot_tn(a, b):
    return lax.dot_general(a, b, (((0,), (0,)), ((), ())), preferred_element_type=F32)


def _inproj_fwd(h, g_pre, w_in):
    t = h.shape[0]

    def body(h_ref, g_ref, w_ref, pc_ref, qkv_ref, sbg_ref, u_ref):
        x = h_ref[...]
        r = lax.rsqrt(jnp.mean(x * x, axis=-1, keepdims=True) + RMS_EPS)
        u = (x * r * g_ref[...]).astype(BF16)
        u_ref[...] = u
        pc_ref[...] = _dot(u, w_ref[:, 0:1536])
        qkv_ref[...] = _dot(u, w_ref[:, 1536:3072]).astype(BF16)
        sbg_ref[...] = _dot(u, w_ref[:, 3072:3584])

    return pl.pallas_call(
        body, name="inproj_fwd", grid=(t // ROW_BLK,),
        in_specs=[_rows(D_MODEL), _whole((1, D_MODEL)), _whole((D_MODEL, D_IN))],
        out_specs=[_rows(1536), _rows(1536), _rows(D_SB), _rows(D_MODEL)],
        out_shape=[jax.ShapeDtypeStruct((t, 1536), F32), jax.ShapeDtypeStruct((t, 1536), BF16),
                   jax.ShapeDtypeStruct((t, D_SB), F32), jax.ShapeDtypeStruct((t, D_MODEL), BF16)],
        compiler_params=_cparams("parallel"),
    )(h, g_pre, w_in)


def _prev_halo(col):
    per = ROW_BLK // HALO
    return pl.BlockSpec((HALO, D_CONV), lambda i, col=col: (jnp.maximum(i * per - 1, 0), col))


def _fill_glu(buf, i, a_ref, b_ref, ha_ref, hb_ref):
    halo = ha_ref[...] * _sigmoid(hb_ref[...])
    buf[0:HALO, :] = jnp.where(i > 0, halo, 0.0)
    buf[HALO:HALO + ROW_BLK, :] = a_ref[...] * _sigmoid(b_ref[...])


SUBLANES = 8
TAP_ROWS = 64
SHIFT_ROWS = HALO + ROW_BLK - SUBLANES


def _fill_shifts(shifts, buf):
    for b in range(1, SUBLANES):
        shifts[b - 1] = buf[pl.ds(b, SHIFT_ROWS), :]


def _window(buf, shifts, first, rows, lanes):
    whole, part = divmod(first, SUBLANES)
    src = buf if part == 0 else shifts.at[part - 1]
    return src[pl.ds(rows.start + SUBLANES * whole, rows.size), lanes]


TAP_ROW_CHUNKS = [pl.ds(r, TAP_ROWS) for r in range(0, ROW_BLK, TAP_ROWS)]
TAP_LANE_TILES = [pl.ds(c, LANES) for c in range(0, D_CONV, LANES)]


def _layer_norm_stats(cv):
    mu = jnp.mean(cv, axis=-1, keepdims=True)
    xc = cv - mu
    rstd = lax.rsqrt(jnp.mean(xc * xc, axis=-1, keepdims=True) + LN_EPS)
    return xc * rstd, rstd


def _conv_fwd(pc, conv_w, conv_b, ln_g, ln_b, w_pw2, b_pw2):
    t = pc.shape[0]

    def body(a_ref, b_ref, gate_ref, ha_ref, hb_ref, cw_ref, cb_ref, lg_ref, lb_ref, wp_ref, bp_ref,
             cout_ref, cv_ref, p_ref, sl_ref, buf, shifts):
        i = pl.program_id(0)
        _fill_glu(buf, i, a_ref, b_ref, ha_ref, hb_ref)
        _fill_shifts(shifts, buf)
        for lanes in TAP_LANE_TILES:
            for rows in TAP_ROW_CHUNKS:
                acc = jnp.zeros((TAP_ROWS, LANES), F32) + cb_ref[:, lanes]
                for j in range(CONV_WIDTH):
                    acc = acc + cw_ref[j:j + 1, lanes] * _window(buf, shifts, HALO - (CONV_WIDTH - 1) + j, rows, lanes)
                cv_ref[rows, lanes] = acc
        xh, _ = _layer_norm_stats(cv_ref[...])
        ln = xh * lg_ref[...] + lb_ref[...]
        sl = (ln * _sigmoid(ln)).astype(BF16)
        sl_ref[...] = sl
        p = _dot(sl, wp_ref[...]) + bp_ref[...]
        p_ref[...] = p
        gate = gate_ref[...]
        cout_ref[...] = (p * (gate * _sigmoid(gate))).astype(BF16)

    vec = _whole((1, D_CONV))
    return pl.pallas_call(
        body, name="conv_fwd", grid=(t // ROW_BLK,),
        in_specs=[_rows(D_CONV, 0), _rows(D_CONV, 1), _rows(D_CONV, 2), _prev_halo(0), _prev_halo(1),
                  _whole((CONV_WIDTH, D_CONV)), vec, vec, vec, _whole((D_CONV, D_CONV)), vec],
        out_specs=[_rows(D_CONV)] * 4,
        out_shape=[jax.ShapeDtypeStruct((t, D_CONV), BF16), jax.ShapeDtypeStruct((t, D_CONV), F32),
                   jax.ShapeDtypeStruct((t, D_CONV), F32), jax.ShapeDtypeStruct((t, D_CONV), BF16)],
        scratch_shapes=[pltpu.VMEM((HALO + ROW_BLK, D_CONV), F32), pltpu.VMEM((SUBLANES - 1, SHIFT_ROWS, D_CONV), F32)],
        compiler_params=_cparams("parallel"),
    )(pc, pc, pc, pc, pc, conv_w, conv_b, ln_g, ln_b, w_pw2, b_pw2)


def _lower_triangle():
    row = lax.broadcasted_iota(jnp.int32, (ROW_BLK, ROW_BLK), 0)
    col = lax.broadcasted_iota(jnp.int32, (ROW_BLK, ROW_BLK), 1)
    return row > col


def _lower_triangle_t():
    row = lax.broadcasted_iota(jnp.int32, (ROW_BLK, ROW_BLK), 0)
    col = lax.broadcasted_iota(jnp.int32, (ROW_BLK, ROW_BLK), 1)
    return row < col


def _tri_sum(x, umat):
    return _dot(x.astype(BF16), umat)


def _log_gates(z):
    ls = -(jnp.maximum(z, 0.0) + jnp.log(1.0 + jnp.exp(-jnp.abs(z))))
    return ls, z + ls


def _head_lanes(hh):
    lane = lax.broadcasted_iota(jnp.int32, (ROW_BLK, HEAD_BLK), 1)
    return (lane >= HEAD_DIM * hh) & (lane < HEAD_DIM * (hh + 1))


def _merge_heads(acc_ref):
    out = acc_ref[HEADS_PER_BLOCK - 1]
    for hh in range(HEADS_PER_BLOCK - 1):
        out = jnp.where(_head_lanes(hh), acc_ref[hh], out)
    return out


def _qkv_specs(t):
    n_blk = D_SB // HEAD_BLK
    return [pl.BlockSpec((ROW_BLK, HEAD_BLK), lambda hp, i: (i, hp)),
            pl.BlockSpec((t, HEAD_BLK), lambda hp, i: (0, n_blk + hp)),
            pl.BlockSpec((t, HEAD_BLK), lambda hp, i: (0, 2 * n_blk + hp))]


def _carry_spec():
    return pl.BlockSpec((HEADS_PER_BLOCK, ROW_BLK, LANES), lambda hp, i: (hp, i, 0))


def _last_block_rows(t, n_tokens):
    packed_rows = 16
    return -(-(n_tokens - (t - ROW_BLK)) // packed_rows) * packed_rows


def _by_block_rows(i, last_rows, sweep):
    if last_rows == ROW_BLK:
        sweep(ROW_BLK)
        return
    last = pl.num_programs(1) - 1
    pl.when(i < last)(lambda: sweep(ROW_BLK))
    pl.when(i == last)(lambda: sweep(last_rows))


def _attn_fwd(qkv, n_tokens):
    t = qkv.shape[0]
    assert t // ROW_BLK <= LANES

    def body(q_ref, k_ref, v_ref, o_ref, c_ref, acc_ref, run_ref, qm_ref, z_ref):
        i = pl.program_id(1)
        q = q_ref[...]
        heads = range(HEADS_PER_BLOCK)
        for hh in heads:
            qm_ref[hh] = jnp.where(_head_lanes(hh), q, jnp.zeros_like(q)) * jnp.asarray(SB_SCALE, BF16)
        acc_ref[...] = jnp.zeros_like(acc_ref)
        c_ref[...] = jnp.zeros_like(c_ref)
        run_ref[...] = jnp.zeros_like(run_ref)

        def sweep(n_rows):
            rows = pl.ds(0, n_rows)
            lower = _lower_triangle()[:n_rows]
            umat = jnp.where(_lower_triangle(), 1.0, 0.0).astype(BF16)
            lane = lax.broadcasted_iota(jnp.int32, (n_rows, LANES), 1)

            def scores(jb):
                start = pl.multiple_of(jb * ROW_BLK, ROW_BLK)
                kb = k_ref[pl.ds(start, ROW_BLK), :]
                for hh in heads:
                    z_ref[hh, rows] = _dot_nt(qm_ref[hh, rows], kb)

            def block(jb, diagonal):
                start = pl.multiple_of(jb * ROW_BLK, ROW_BLK)
                vb = v_ref[pl.ds(start, ROW_BLK), :]
                logits = []
                for hh in heads:
                    ls, lb = _log_gates(z_ref[hh, rows])
                    if diagonal:
                        ls = jnp.where(lower, ls, 0.0)
                    run = run_ref[hh, rows]
                    if not diagonal:
                        c_ref[hh, rows] = jnp.where(lane == jb, run, c_ref[hh, rows])
                    logits.append(lb + jnp.concatenate([run, run], axis=1) + _tri_sum(ls, umat))
                    run_ref[hh, rows] = run + jnp.sum(ls, axis=1, keepdims=True)
                scores(jnp.maximum(jb - 1, 0))
                for hh in heads:
                    a = jnp.exp(logits[hh])
                    if diagonal:
                        a = jnp.where(lower, a, 0.0)
                    acc_ref[hh, rows] += _dot(a.astype(BF16), vb)

            scores(i)
            block(i, True)

            @pl.loop(0, i)
            def _(n):
                block(i - 1 - n, False)

        _by_block_rows(i, _last_block_rows(t, n_tokens), sweep)
        o_ref[...] = _merge_heads(acc_ref)

    per_head = (HEADS_PER_BLOCK, ROW_BLK, HEAD_BLK)
    return pl.pallas_call(
        body, name="attn_fwd", grid=(D_SB // HEAD_BLK, t // ROW_BLK),
        in_specs=_qkv_specs(t),
        out_specs=[pl.BlockSpec((ROW_BLK, HEAD_BLK), lambda hp, i: (i, hp)), _carry_spec()],
        out_shape=[jax.ShapeDtypeStruct((t, D_SB), F32),
                   jax.ShapeDtypeStruct((D_SB // HEAD_DIM, t, LANES), F32)],
        scratch_shapes=[pltpu.VMEM(per_head, F32), pltpu.VMEM((HEADS_PER_BLOCK, ROW_BLK, LANES), F32),
                        pltpu.VMEM(per_head, BF16), pltpu.VMEM((HEADS_PER_BLOCK, ROW_BLK, ROW_BLK), F32)],
        compiler_params=_cparams("arbitrary", "arbitrary"),
    )(qkv, qkv, qkv)


def _outproj_fwd(h, cout, sraw, sbg, w_out, g_post):
    t = h.shape[0]

    def body(h_ref, c_ref, s_ref, g_ref, w_ref, gp_ref, hn_ref, mixed_ref, mix_ref):
        gate = g_ref[...]
        mix_ref[:, 0:D_CONV] = c_ref[...]
        mix_ref[:, D_CONV:] = (s_ref[...] * (gate * _sigmoid(gate))).astype(BF16)
        mixed = _dot(mix_ref[...], w_ref[...])
        mixed_ref[...] = mixed
        r = lax.rsqrt(jnp.mean(mixed * mixed, axis=-1, keepdims=True) + RMS_EPS)
        hn_ref[...] = h_ref[...] + mixed * r * gp_ref[...]

    return pl.pallas_call(
        body, name="outproj_fwd", grid=(t // ROW_BLK,),
        in_specs=[_rows(D_MODEL), _rows(D_CONV), _rows(D_SB), _rows(D_SB), _whole((D_MODEL, D_MODEL)),
                  _whole((1, D_MODEL))],
        out_specs=[_rows(D_MODEL)] * 3,
        out_shape=[jax.ShapeDtypeStruct((t, D_MODEL), F32), jax.ShapeDtypeStruct((t, D_MODEL), F32),
                   jax.ShapeDtypeStruct((t, D_MODEL), BF16)],
        compiler_params=_cparams("parallel"),
    )(h, cout, sraw, sbg, w_out, g_post)


def _loss_and_grad(h, target, seq):
    t = h.shape[0]

    def body(h_ref, t_ref, loss_ref, dh_ref):
        i = pl.program_id(0)

        @pl.when(i == 0)
        def _():
            loss_ref[...] = jnp.zeros_like(loss_ref)

        row = i * ROW_BLK + lax.broadcasted_iota(jnp.int32, (ROW_BLK, D_MODEL), 0)
        real = (row >= N_META) & (row < N_META + seq)
        diff = jnp.where(real, h_ref[...] - t_ref[...], 0.0)
        sq = jnp.sum(jnp.sum(diff * diff, axis=1, keepdims=True), axis=0, keepdims=True)
        loss_ref[...] += (0.5 / D_MODEL) * sq
        dh_ref[...] = diff * (1.0 / D_MODEL)

    return pl.pallas_call(
        body, name="loss", grid=(t // ROW_BLK,),
        in_specs=[_rows(D_MODEL), _rows(D_MODEL)],
        out_specs=[_whole((1, 1)), _rows(D_MODEL)],
        out_shape=[jax.ShapeDtypeStruct((1, 1), F32), jax.ShapeDtypeStruct((t, D_MODEL), F32)],
        compiler_params=_cparams("arbitrary"),
    )(h, target)


def _outproj_bwd(dh, mixed, g_post, sraw, sbg, w_out_t):
    t = dh.shape[0]

    def body(dh_ref, mixed_ref, gp_ref, s_ref, g_ref, wt_ref, dc_ref, ds_ref, dg_ref, dmb_ref, dgp_ref):
        @pl.when(pl.program_id(0) == 0)
        def _():
            dgp_ref[...] = jnp.zeros_like(dgp_ref)

        mixed = mixed_ref[...]
        r = lax.rsqrt(jnp.mean(mixed * mixed, axis=-1, keepdims=True) + RMS_EPS)
        nh = mixed * r
        dy = dh_ref[...]
        dgp_ref[...] += jnp.sum(dy * nh, axis=0, keepdims=True)
        dn = dy * gp_ref[...]
        dmixed = (r * (dn - nh * jnp.mean(dn * nh, axis=-1, keepdims=True))).astype(BF16)
        dmb_ref[...] = dmixed
        dmix = _dot_nt(dmixed, wt_ref[...])
        dc_ref[...] = dmix[:, 0:D_CONV]
        dsg = dmix[:, D_CONV:]
        gate = g_ref[...]
        sg = _sigmoid(gate)
        ds_ref[...] = dsg * (gate * sg)
        dg_ref[...] = (dsg * s_ref[...] * _dsilu(gate, sg)).astype(BF16)

    return pl.pallas_call(
        body, name="outproj_bwd", grid=(t // ROW_BLK,),
        in_specs=[_rows(D_MODEL), _rows(D_MODEL), _whole((1, D_MODEL)), _rows(D_SB), _rows(D_SB),
                  _whole((D_MODEL, D_MODEL))],
        out_specs=[_rows(D_CONV), _rows(D_SB), _rows(D_SB), _rows(D_MODEL), _whole((1, D_MODEL))],
        out_shape=[jax.ShapeDtypeStruct((t, D_CONV), F32), jax.ShapeDtypeStruct((t, D_SB), F32),
                   jax.ShapeDtypeStruct((t, D_SB), BF16), jax.ShapeDtypeStruct((t, D_MODEL), BF16),
                   jax.ShapeDtypeStruct((1, D_MODEL), F32)],
        compiler_params=_cparams("arbitrary"),
    )(dh, mixed, g_post, sraw, sbg, w_out_t)


def _attn_bwd(qkv, carries, do, n_tokens):
    t = qkv.shape[0]

    def body(q_ref, k_ref, v_ref, c_ref, do_ref, dq_ref, dk_ref, dv_ref, acc_ref, seen_ref, qm_ref, dom_ref, z_ref,
             da_ref, dz_ref, a_ref):
        i = pl.program_id(1)

        @pl.when(i == 0)
        def _():
            dk_ref[...] = jnp.zeros_like(dk_ref)
            dv_ref[...] = jnp.zeros_like(dv_ref)

        q = q_ref[...]
        dof = do_ref[...]
        heads = range(HEADS_PER_BLOCK)
        for hh in heads:
            qm_ref[hh] = jnp.where(_head_lanes(hh), q, jnp.zeros_like(q)) * jnp.asarray(SB_SCALE, BF16)
            dom_ref[hh] = jnp.where(_head_lanes(hh), dof, 0.0).astype(BF16)
        acc_ref[...] = jnp.zeros_like(acc_ref)
        seen_ref[...] = jnp.zeros_like(seen_ref)

        def sweep(n_rows):
            rows = pl.ds(0, n_rows)
            lower = _lower_triangle()[:n_rows]
            umat = jnp.where(_lower_triangle(), 1.0, 0.0).astype(BF16)
            umat_t = jnp.where(_lower_triangle_t(), 1.0, 0.0).astype(BF16)
            lane = lax.broadcasted_iota(jnp.int32, (n_rows, LANES), 1)

            def scores(jb):
                start = pl.multiple_of(jb * ROW_BLK, ROW_BLK)
                kb = k_ref[pl.ds(start, ROW_BLK), :]
                for hh in heads:
                    z_ref[hh, rows] = _dot_nt(qm_ref[hh, rows], kb)

            def value_grads(jb):
                start = pl.multiple_of(jb * ROW_BLK, ROW_BLK)
                vb = v_ref[pl.ds(start, ROW_BLK), :]
                for hh in heads:
                    da_ref[hh, rows] = _dot_nt(dom_ref[hh, rows], vb)

            def products(jb, hh):
                start = pl.multiple_of(jb * ROW_BLK, ROW_BLK)
                dzb = dz_ref[hh, rows]
                acc_ref[hh, rows] += _dot(dzb, k_ref[pl.ds(start, ROW_BLK), :])
                dk_ref[pl.ds(start, ROW_BLK), :] += _dot_tn(dzb, qm_ref[hh, rows])
                dv_ref[pl.ds(start, ROW_BLK), :] += _dot_tn(a_ref[hh, rows], dom_ref[hh, rows])

            def block(jb, diagonal):
                before = jnp.maximum(jb - 1, 0)
                lbs, logits = [], []
                for hh in heads:
                    products(before, hh)
                    ls, lb = _log_gates(z_ref[hh, rows])
                    if diagonal:
                        ls = jnp.where(lower, ls, 0.0)
                        logits.append(lb + _tri_sum(ls, umat))
                    else:
                        right = jnp.sum(jnp.where(lane == jb, c_ref[hh, rows], 0.0), axis=1, keepdims=True)
                        logits.append(lb + right + _tri_sum(ls, umat))
                    lbs.append(lb)
                if not diagonal:
                    scores(jb + 1)
                gs, befores = [], []
                for hh in heads:
                    a = jnp.exp(logits[hh])
                    if diagonal:
                        a = jnp.where(lower, a, 0.0)
                    g = da_ref[hh, rows] * a
                    seen = seen_ref[hh, rows]
                    befores.append(jnp.concatenate([seen, seen], axis=1) + _tri_sum(g, umat_t))
                    seen_ref[hh, rows] = seen + jnp.sum(g, axis=1, keepdims=True)
                    a_ref[hh, rows] = a.astype(BF16)
                    gs.append(g)
                if not diagonal:
                    value_grads(jb + 1)
                for hh in heads:
                    dz = gs[hh] - jnp.exp(lbs[hh]) * (gs[hh] + befores[hh])
                    if diagonal:
                        dz = jnp.where(lower, dz, 0.0)
                    dz_ref[hh, rows] = dz.astype(BF16)

            dz_ref[...] = jnp.zeros_like(dz_ref)
            a_ref[...] = jnp.zeros_like(a_ref)
            scores(0)
            value_grads(0)

            @pl.loop(0, i)
            def _(jb):
                block(jb, False)

            block(i, True)
            for hh in heads:
                products(i, hh)

        _by_block_rows(i, _last_block_rows(t, n_tokens), sweep)
        dq_ref[...] = (_merge_heads(acc_ref) * SB_SCALE).astype(BF16)

    blk = pl.BlockSpec((ROW_BLK, HEAD_BLK), lambda hp, i: (i, hp))
    full = pl.BlockSpec((t, HEAD_BLK), lambda hp, i: (0, hp))
    per_head = (HEADS_PER_BLOCK, ROW_BLK, HEAD_BLK)
    return pl.pallas_call(
        body, name="attn_bwd", grid=(D_SB // HEAD_BLK, t // ROW_BLK),
        in_specs=_qkv_specs(t) + [_carry_spec(), blk],
        out_specs=[blk, full, full],
        out_shape=[jax.ShapeDtypeStruct((t, D_SB), BF16)] + [jax.ShapeDtypeStruct((t, D_SB), F32)] * 2,
        scratch_shapes=[pltpu.VMEM(per_head, F32), pltpu.VMEM((HEADS_PER_BLOCK, ROW_BLK, LANES), F32),
                        pltpu.VMEM(per_head, BF16), pltpu.VMEM(per_head, BF16)]
                       + [pltpu.VMEM((HEADS_PER_BLOCK, ROW_BLK, ROW_BLK), dtype) for dtype in (F32, F32, BF16, BF16)],
        compiler_params=_cparams("arbitrary", "arbitrary"),
    )(qkv, qkv, qkv, carries, do)


def _conv_bwd_rows(dcout, pc, cv, p, ln_g, ln_b, w_pw2_t):
    t = dcout.shape[0]

    def body(dc_ref, gate_ref, cv_ref, p_ref, lg_ref, lb_ref, wt_ref, dcv_ref, dgate_ref, dpb_ref, vec_ref):
        @pl.when(pl.program_id(0) == 0)
        def _():
            vec_ref[...] = jnp.zeros_like(vec_ref)

        dc = dc_ref[...]
        gate = gate_ref[...]
        sg = _sigmoid(gate)
        dp = dc * (gate * sg)
        dgate_ref[...] = (dc * p_ref[...] * _dsilu(gate, sg)).astype(BF16)
        dpb = dp.astype(BF16)
        dpb_ref[...] = dpb
        xh, rstd = _layer_norm_stats(cv_ref[...])
        ln = xh * lg_ref[...] + lb_ref[...]
        s2 = _sigmoid(ln)
        dln = _dot_nt(dpb, wt_ref[...]) * _dsilu(ln, s2)
        dxh = dln * lg_ref[...]
        dcv = rstd * (dxh - jnp.mean(dxh, axis=-1, keepdims=True)
                      - xh * jnp.mean(dxh * xh, axis=-1, keepdims=True))
        dcv_ref[...] = dcv
        vec_ref[0:1, :] += jnp.sum(dp, axis=0, keepdims=True)
        vec_ref[1:2, :] += jnp.sum(dln * xh, axis=0, keepdims=True)
        vec_ref[2:3, :] += jnp.sum(dln, axis=0, keepdims=True)
        vec_ref[3:4, :] += jnp.sum(dcv, axis=0, keepdims=True)

    vec = _whole((1, D_CONV))
    return pl.pallas_call(
        body, name="conv_bwd_rows", grid=(t // ROW_BLK,),
        in_specs=[_rows(D_CONV), _rows(D_CONV, 2), _rows(D_CONV), _rows(D_CONV), vec, vec,
                  _whole((D_CONV, D_CONV))],
        out_specs=[_rows(D_CONV), _rows(D_CONV), _rows(D_CONV), _whole((8, D_CONV))],
        out_shape=[jax.ShapeDtypeStruct((t, D_CONV), F32), jax.ShapeDtypeStruct((t, D_CONV), BF16),
                   jax.ShapeDtypeStruct((t, D_CONV), BF16), jax.ShapeDtypeStruct((8, D_CONV), F32)],
        compiler_params=_cparams("arbitrary"),
    )(dcout, pc, cv, p, ln_g, ln_b, w_pw2_t)


def _conv_bwd_taps(dcv, pc, conv_w):
    t = dcv.shape[0]
    n_halo = t // HALO
    per = ROW_BLK // HALO

    def body(d_ref, dn_ref, a_ref, b_ref, ha_ref, hb_ref, cw_ref, da_ref, db_ref, dw_ref, cbuf, dbuf, cshifts, dshifts):
        i = pl.program_id(0)

        @pl.when(i == 0)
        def _():
            dw_ref[...] = jnp.zeros_like(dw_ref)

        _fill_glu(cbuf, i, a_ref, b_ref, ha_ref, hb_ref)
        dbuf[0:ROW_BLK, :] = d_ref[...]
        dbuf[ROW_BLK:ROW_BLK + HALO, :] = jnp.where(i < pl.num_programs(0) - 1, dn_ref[...], 0.0)
        _fill_shifts(cshifts, cbuf)
        _fill_shifts(dshifts, dbuf)
        for lanes in TAP_LANE_TILES:
            for rows in TAP_ROW_CHUNKS:
                acc = jnp.zeros((TAP_ROWS, LANES), F32)
                for j in range(CONV_WIDTH):
                    acc = acc + cw_ref[j:j + 1, lanes] * _window(dbuf, dshifts, CONV_WIDTH - 1 - j, rows, lanes)
                sb = _sigmoid(b_ref[rows, lanes])
                da_ref[rows, lanes] = (acc * sb).astype(BF16)
                db_ref[rows, lanes] = (acc * a_ref[rows, lanes] * sb * (1.0 - sb)).astype(BF16)
            for j in range(CONV_WIDTH):
                acc = jnp.zeros((TAP_ROWS, LANES), F32)
                for rows in TAP_ROW_CHUNKS:
                    acc = acc + d_ref[rows, lanes] * _window(cbuf, cshifts, HALO - (CONV_WIDTH - 1) + j, rows, lanes)
                dw_ref[j:j + 1, lanes] += jnp.sum(acc, axis=0, keepdims=True)

    return pl.pallas_call(
        body, name="conv_bwd_taps", grid=(t // ROW_BLK,),
        in_specs=[_rows(D_CONV),
                  pl.BlockSpec((HALO, D_CONV), lambda i: (jnp.minimum((i + 1) * per, n_halo - 1), 0)),
                  _rows(D_CONV, 0), _rows(D_CONV, 1), _prev_halo(0), _prev_halo(1),
                  _whole((CONV_WIDTH, D_CONV))],
        out_specs=[_rows(D_CONV), _rows(D_CONV), _whole((32, D_CONV))],
        out_shape=[jax.ShapeDtypeStruct((t, D_CONV), BF16), jax.ShapeDtypeStruct((t, D_CONV), BF16),
                   jax.ShapeDtypeStruct((32, D_CONV), F32)],
        scratch_shapes=[pltpu.VMEM((HALO + ROW_BLK, D_CONV), F32), pltpu.VMEM((ROW_BLK + HALO, D_CONV), F32),
                        pltpu.VMEM((SUBLANES - 1, SHIFT_ROWS, D_CONV), F32),
                        pltpu.VMEM((SUBLANES - 1, SHIFT_ROWS, D_CONV), F32)],
        compiler_params=_cparams("arbitrary"),
    )(dcv, dcv, pc, pc, pc, pc, conv_w)


def _inproj_bwd(dh_out, h, g_pre, pieces, w_in_t):
    t = h.shape[0]

    def body(dh_ref, h_ref, g_ref, *rest):
        piece_refs, (wt_ref, dhin_ref, dproj_ref, dg_ref) = rest[:7], rest[7:]

        @pl.when(pl.program_id(0) == 0)
        def _():
            dg_ref[...] = jnp.zeros_like(dg_ref)

        for k, ref in enumerate(piece_refs):
            dproj_ref[:, 512 * k:512 * (k + 1)] = ref[...].astype(BF16)
        du = _dot_nt(dproj_ref[...], wt_ref[...])
        x = h_ref[...]
        r = lax.rsqrt(jnp.mean(x * x, axis=-1, keepdims=True) + RMS_EPS)
        xh = x * r
        dg_ref[...] += jnp.sum(du * xh, axis=0, keepdims=True)
        dxh = du * g_ref[...]
        dhin_ref[...] = dh_ref[...] + r * (dxh - xh * jnp.mean(dxh * xh, axis=-1, keepdims=True))

    return pl.pallas_call(
        body, name="inproj_bwd", grid=(t // ROW_BLK,),
        in_specs=[_rows(D_MODEL), _rows(D_MODEL), _whole((1, D_MODEL))] + [_rows(512)] * 7
                 + [_whole((D_MODEL, D_IN))],
        out_specs=[_rows(D_MODEL), _rows(D_IN), _whole((1, D_MODEL))],
        out_shape=[jax.ShapeDtypeStruct((t, D_MODEL), F32), jax.ShapeDtypeStruct((t, D_IN), BF16),
                   jax.ShapeDtypeStruct((1, D_MODEL), F32)],
        compiler_params=_cparams("arbitrary"),
    )(dh_out, h, g_pre, *pieces, w_in_t)


def _weight_grad(xb, dyb, name):
    t, k = xb.shape
    n = dyb.shape[1]

    def body(x_ref, dy_ref, o_ref, acc_ref):
        i = pl.program_id(0)

        @pl.when(i == 0)
        def _():
            acc_ref[...] = jnp.zeros_like(acc_ref)

        acc_ref[...] += _dot_tn(x_ref[...], dy_ref[...])

        @pl.when(i == pl.num_programs(0) - 1)
        def _():
            o_ref[...] = acc_ref[...].astype(BF16)

    return pl.pallas_call(
        body, name=name, grid=(t // ROW_BLK,),
        in_specs=[_rows(k), _rows(n)], out_specs=_whole((k, n)), out_shape=jax.ShapeDtypeStruct((k, n), BF16),
        scratch_shapes=[pltpu.VMEM((k, n), F32)],
        compiler_params=_cparams("arbitrary"),
    )(xb, dyb)


def _position():
    return lax.axis_index("x"), lax.axis_index("y"), lax.axis_index("c")


def _comm_call(body, name, ins, out_shapes):
    n = len(ins)
    hbm = pl.BlockSpec(memory_space=pltpu.HBM)
    return pl.pallas_call(
        functools.partial(body, n), name=name, in_specs=[hbm] * n, out_specs=[hbm] * n, out_shape=out_shapes,
        scratch_shapes=[pltpu.SemaphoreType.DMA((n, N_DEV - 1)), pltpu.SemaphoreType.DMA((n, N_DEV - 1)),
                        pltpu.SemaphoreType.DMA((n,))],
    )(*ins)


def _all_gather(blocks, name):
    def body(n, *refs):
        x_refs, out_refs, (send_sems, recv_sems, local_sems) = refs[:n], refs[n:2 * n], refs[2 * n:]
        x, y, c = _position()
        me, sibling = (x, y, c), (x, y, 1 - c)
        chips = [(1 - x, y), (x, 1 - y), (1 - x, 1 - y)]

        def slot(a, px, py, pc):
            return out_refs[a].at[4 * px + 2 * py + pc]

        def copy(a, k, origin, to, own=False):
            return pltpu.make_async_remote_copy(
                src_ref=x_refs[a] if own else slot(a, *origin), dst_ref=slot(a, *origin),
                send_sem=send_sems.at[a, k], recv_sem=recv_sems.at[a, k], device_id=to, device_id_type=MESH)

        arrays = range(n)
        mine = [pltpu.make_async_copy(x_refs[a], slot(a, *me), local_sems.at[a]) for a in arrays]
        first = [copy(a, 1 + j, me, (*chip, c), own=True) for j, chip in enumerate(chips) for a in arrays]
        first += [copy(a, 0, me, sibling, own=True) for a in arrays]
        for cp in mine + first:
            cp.start()
        passed = []
        for j, chip in enumerate(chips):
            for a in arrays:
                copy(a, 1 + j, (*chip, c), me).wait_recv()
                passed.append(copy(a, 4 + j, (*chip, c), sibling))
                passed[-1].start()
        for a in arrays:
            copy(a, 0, sibling, me).wait_recv()
            for j, chip in enumerate(chips):
                copy(a, 4 + j, (*chip, 1 - c), me).wait_recv()
        for cp in first + passed:
            cp.wait_send()
        for cp in mine:
            cp.wait()

    return _comm_call(body, name, blocks, [jax.ShapeDtypeStruct((N_DEV,) + b.shape, b.dtype) for b in blocks])


def _exchange_copies(g_refs, land_refs, sems, gather):
    x, y, c = _position()
    me = 4 * x + 2 * y + c
    out = []
    for g_ref, land_ref, (send_sem, recv_sem, local_sem) in zip(g_refs, land_refs, sems):
        def mine(slot, g_ref=g_ref):
            return g_ref if gather else g_ref.at[slot]

        def remote(src, dst, dev):
            return pltpu.make_async_remote_copy(src_ref=src, dst_ref=dst, send_sem=send_sem, recv_sem=recv_sem,
                                                device_id=dev, device_id_type=MESH)

        sends = []
        for k in range(1, N_DEV):
            px = 1 - x if k & 4 else x
            py = 1 - y if k & 2 else y
            pc = 1 - c if k & 1 else c
            sends.append(remote(mine(4 * px + 2 * py + pc), land_ref.at[me], (px, py, pc)))
        seven = land_ref.at[pl.ds(0, N_DEV - 1)]
        out.append((pltpu.make_async_copy(mine(me), land_ref.at[me], local_sem), sends, remote(seven, seven, (x, y, c))))
    return out


_HBM = pl.BlockSpec(memory_space=pltpu.HBM)
_SEM = pl.BlockSpec(memory_space=pltpu.SEMAPHORE)
_ORDERED = pltpu.CompilerParams(has_side_effects=pltpu.SideEffectType.DATAFLOW_SIDE_EFFECTING)
SEMS_PER_ARRAY = 3


def _exchange_start(arrays, after, name, gather):
    n = len(arrays)
    n_sems = SEMS_PER_ARRAY * n

    def body(*refs):
        g_refs, land_refs, sems, token = refs[:n], refs[n:2 * n], refs[2 * n + 1:2 * n + 1 + n_sems], refs[-1]
        sems = [sems[SEMS_PER_ARRAY * a:SEMS_PER_ARRAY * (a + 1)] for a in range(n)]
        for local, sends, _ in _exchange_copies(g_refs, land_refs, sems, gather):
            local.start()
            for cp in sends:
                cp.start()
        token[...] = jnp.zeros_like(token)

    buffers = list(arrays) + [lax.empty((N_DEV,) + g.shape if gather else g.shape, g.dtype) for g in arrays]
    outs = pl.pallas_call(
        body, name=name, in_specs=[_HBM] * (2 * n) + [pl.BlockSpec(memory_space=pl.ANY)],
        out_specs=[_SEM] * n_sems + [_HBM] * (2 * n) + [pl.BlockSpec(memory_space=pltpu.VMEM)],
        out_shape=[pltpu.SemaphoreType.DMA(())] * n_sems + [pltpu.HBM(b.shape, b.dtype) for b in buffers]
                  + [jax.ShapeDtypeStruct((8, LANES), F32)],
        input_output_aliases={a: n_sems + a for a in range(2 * n)}, compiler_params=_ORDERED,
    )(*[pltpu.with_memory_space_constraint(b, pltpu.HBM) for b in buffers], after)
    return outs[:n_sems], outs[n_sems:n_sems + n], outs[n_sems + n:n_sems + 2 * n], outs[-1]


def _exchange_wait(sems, arrays, landings, after, name, gather):
    n = len(arrays)
    n_sems = SEMS_PER_ARRAY * n

    def body(*refs):
        g_refs, land_refs, sems = refs[:n], refs[n:2 * n], refs[2 * n:2 * n + n_sems]
        sems = [sems[SEMS_PER_ARRAY * a:SEMS_PER_ARRAY * (a + 1)] for a in range(n)]
        for local, _, all_seven in _exchange_copies(g_refs, land_refs, sems, gather):
            all_seven.wait_recv()
            all_seven.wait_send()
            local.wait()

    buffers = list(arrays) + list(landings)
    outs = pl.pallas_call(
        body, name=name, in_specs=[_HBM] * (2 * n) + [_SEM] * n_sems + [pl.BlockSpec(memory_space=pl.ANY)],
        out_specs=[_HBM] * (2 * n), out_shape=[pltpu.HBM(b.shape, b.dtype) for b in buffers],
        input_output_aliases={a: a for a in range(2 * n)}, compiler_params=_ORDERED,
    )(*buffers, *sems, after)
    return outs[n:]


def _block_rows(r, row_bytes, budget=1 << 20):
    cap = max(8, budget // row_bytes)
    return max(d for d in range(8, min(r, cap) + 1, 8) if r % d == 0)


def _sum_adamw(parts, w, m, v, name):
    n_parts, r, c = parts.shape
    br = _block_rows(r, 4 * c)

    def body(p_ref, w_ref, m_ref, v_ref, g_out, d_out, m_out, v_out):
        g = p_ref[0].astype(F32)
        for s in range(1, n_parts):
            g = g + p_ref[s].astype(F32)
        m_new = ADAM_B1 * m_ref[...] + (1.0 - ADAM_B1) * g
        v_new = ADAM_B2 * v_ref[...] + (1.0 - ADAM_B2) * (g * g)
        m_hat = m_new / (1.0 - ADAM_B1 ** ADAM_STEP)
        v_hat = v_new / (1.0 - ADAM_B2 ** ADAM_STEP)
        g_out[...] = g
        d_out[...] = -ADAM_LR * (m_hat / (jnp.sqrt(v_hat) + ADAM_EPS) + ADAM_WD * w_ref[...])
        m_out[...] = m_new
        v_out[...] = v_new

    row = pl.BlockSpec((br, c), lambda i: (i, 0))
    return pl.pallas_call(
        body, name=name, grid=(r // br,),
        in_specs=[pl.BlockSpec((n_parts, br, c), lambda i: (0, i, 0)), row, row, row],
        out_specs=[row] * 4, out_shape=[jax.ShapeDtypeStruct((r, c), F32)] * 4,
        compiler_params=_cparams("parallel"),
    )(parts, w, m, v)


def _sum_parts(parts, name):
    n_parts, r, c = parts.shape

    def body(p_ref, o_ref):
        g = p_ref[0]
        for s in range(1, n_parts):
            g = g + p_ref[s]
        o_ref[...] = g

    return pl.pallas_call(
        body, name=name, in_specs=[pl.BlockSpec(memory_space=pltpu.VMEM)],
        out_specs=pl.BlockSpec(memory_space=pltpu.VMEM), out_shape=jax.ShapeDtypeStruct((r, c), F32),
    )(parts)


def _pack(arrays):
    flat = jnp.concatenate([a.reshape(-1) for a in arrays])
    pad = -flat.shape[0] % (8 * LANES)
    if pad:
        flat = jnp.pad(flat, (0, pad))
    return flat.reshape(-1, LANES)


def _unpack(buf, shapes):
    flat = buf.reshape(-1)
    out, at = [], 0
    for shape in shapes:
        size = 1
        for d in shape:
            size *= d
        out.append(lax.slice_in_dim(flat, at, at + size).reshape(shape))
        at += size
    return out


def _local_step(x, target, meta, pre_g, post_g, conv_w, conv_b, ln_g, ln_b, b_pw2, weights, ship, ship_small):
    depth = pre_g.shape[0]
    seq = x.shape[0]
    t = -(-(N_META + seq) // ROW_BLK) * ROW_BLK
    tail = t - N_META - seq
    h = jnp.concatenate([meta, x, jnp.zeros((tail, D_MODEL), F32)], axis=0)
    target = jnp.pad(target, ((N_META, tail), (0, 0)))
    row = lambda a, l: a[l][None, :]

    saved = []
    for l in range(depth):
        w_in, w_pw2, w_out = weights(l, h)
        pc, qkv, sbg, u = _inproj_fwd(h, row(pre_g, l), w_in)
        cout, cv, p, sl = _conv_fwd(pc, conv_w[l], row(conv_b, l), row(ln_g, l), row(ln_b, l), w_pw2, row(b_pw2, l))
        sraw, carries = _attn_fwd(qkv, N_META + seq)
        h_new, mixed, mix = _outproj_fwd(h, cout, sraw, sbg, w_out, row(post_g, l))
        saved.append((h, pc, qkv, sbg, u, cv, p, sl, sraw, carries, mixed, mix, w_in, w_pw2, w_out))
        h = h_new

    loss, dh = _loss_and_grad(h, target, seq)

    grads = {k: [None] * depth for k in ("pre_g", "post_g", "conv_w", "conv_b", "ln_g", "ln_b", "b_pw2")}
    token = jnp.zeros((8, LANES), F32)
    for l in reversed(range(depth)):
        h_in, pc, qkv, sbg, u, cv, p, sl, sraw, carries, mixed, mix, w_in_t, w_pw2_t, w_out_t = saved[l]
        dcout, dsraw, dsbg, dmixed, dg_post = _outproj_bwd(dh, mixed, row(post_g, l) + token[:1, :1], sraw, sbg, w_out_t)
        dq, dk, dv = _attn_bwd(qkv, carries, dsraw, N_META + seq)
        dcv, dgate, dpb, vecs = _conv_bwd_rows(dcout, pc, cv, p, row(ln_g, l), row(ln_b, l), w_pw2_t)
        da, db, dconv_w = _conv_bwd_taps(dcv, pc, conv_w[l])
        dh, dproj, dg_pre = _inproj_bwd(dh, h_in, row(pre_g, l), (da, db, dgate, dq, dk, dv, dsbg), w_in_t)
        grads["pre_g"][l] = dg_pre[0]
        grads["post_g"][l] = dg_post[0]
        grads["b_pw2"][l], grads["ln_g"][l], grads["ln_b"][l], grads["conv_b"][l] = vecs[0], vecs[1], vecs[2], vecs[3]
        grads["conv_w"][l] = dconv_w[:CONV_WIDTH]
        after = dh
        if l == 0:
            grads = {k: jnp.stack(v) for k, v in grads.items()}
            grads["meta"] = dh[:N_META]
            after = ship_small(grads, loss[0, 0])
        token = ship(l, after, _weight_grad(u, dproj, "w_in_grad"), _weight_grad(sl, dpb, "w_pw2_grad"),
                     _weight_grad(mix, dmixed, "w_out_grad"))

    return dh[N_META:N_META + seq], token


def _shard_major(full, axis):
    shape = full.shape
    split = full.reshape(shape[:axis] + (N_DEV, shape[axis] // N_DEV) + shape[axis + 1:])
    return jnp.moveaxis(split, axis, 0)


def _whole_from_shards(shards, axis):
    moved = jnp.moveaxis(shards, 0, axis)
    shape = moved.shape
    return moved.reshape(shape[:axis] + (shape[axis] * shape[axis + 1],) + shape[axis + 2:])


def kernel(x, meta_tokens, pre_norm_g, post_norm_g, w_in, conv_w, conv_b, conv_ln_g, conv_ln_b, w_pw2, b_pw2, w_out, loss_target, m_meta_tokens, m_pre_norm_g, m_post_norm_g, m_w_in, m_conv_w, m_conv_b, m_conv_ln_g, m_conv_ln_b, m_w_pw2, m_b_pw2, m_w_out, v_meta_tokens, v_pre_norm_g, v_post_norm_g, v_w_in, v_conv_w, v_conv_b, v_conv_ln_g, v_conv_ln_b, v_w_pw2, v_b_pw2, v_w_out):
    me = 4 * lax.axis_index("x") + 2 * lax.axis_index("y") + lax.axis_index("c")

    depth = w_in.shape[0]
    big = [w.astype(BF16) for w in (w_in, w_pw2, w_out)]
    *first, conv_w_s, meta_s = _all_gather([w[0] for w in big] + [conv_w, meta_tokens], "gather_first_layer")
    *gathering, token = _exchange_start([w[l] for l in range(1, depth) for w in big], meta_s, "gather_start", gather=True)
    conv_w_full = _whole_from_shards(conv_w_s, 2)
    meta_full = _whole_from_shards(meta_s, 1)
    shard_axis = (1, 0, 0)
    later = []

    def weights(l, h):
        if l == 0:
            return [_whole_from_shards(s, axis) for s, axis in zip(first, shard_axis)]
        if not later:
            later.extend(_exchange_wait(*gathering, h, "gather_wait", gather=True))
        return [_whole_from_shards(s, axis) for s, axis in zip(later[len(big) * (l - 1):len(big) * l], shard_axis)]

    in_flight = [None] * depth

    def ship(l, dh, dw_in, dw_pw2, dw_out):
        slabs = [_shard_major(dw, axis) for dw, axis in zip((dw_in, dw_pw2, dw_out), shard_axis)]
        *in_flight[l], token = _exchange_start(slabs, dh, f"exchange_start_{l}", gather=False)
        return token

    small_names = ("pre_g", "post_g", "conv_b", "ln_g", "ln_b", "b_pw2", "conv_w", "meta")
    small_in_flight, small_shapes_full = [], []

    def ship_small(grads, loss):
        small_full = [grads[k] for k in small_names] + [loss.reshape(1)]
        small_shapes_full.extend(a.shape for a in small_full)
        *in_flight_now, token = _exchange_start([_pack(small_full)], grads["meta"], "small_grads_start", gather=True)
        small_in_flight.extend(in_flight_now)
        return token

    dx, shipped = _local_step(x[0], loss_target[0], meta_full, pre_norm_g + token[:1, :1], post_norm_g, conv_w_full,
                              conv_b, conv_ln_g, conv_ln_b, b_pw2, weights, ship, ship_small)

    updated = [None] * depth
    done = shipped

    def update_layer(l):
        landed = _exchange_wait(*in_flight[l], done, f"exchange_wait_{l}", gather=False)
        return [_sum_adamw(parts, w[l], m[l], v[l], name) for parts, w, m, v, name in zip(
            landed, (w_in, w_pw2, w_out), (m_w_in, m_w_pw2, m_w_out), (v_w_in, v_w_pw2, v_w_out),
            ("adamw_w_in", "adamw_w_pw2", "adamw_w_out"))]

    for l in reversed(range(1, depth)):
        updated[l] = update_layer(l)
        done = updated[l][0][1]

    gathered, = _exchange_wait(*small_in_flight, done, "small_grads_wait", gather=True)
    summed = _unpack(_sum_parts(gathered, "sum_small_grads"), small_shapes_full)
    loss = summed[-1][0]
    g_small = dict(zip(small_names, summed))
    g_small["conv_w"] = lax.dynamic_slice_in_dim(g_small["conv_w"], me * conv_w.shape[2], conv_w.shape[2], axis=2)
    g_small["meta"] = lax.dynamic_slice_in_dim(g_small["meta"], me * meta_tokens.shape[1], meta_tokens.shape[1], axis=1)
    small_w = dict(zip(small_names, (pre_norm_g, post_norm_g, conv_b, conv_ln_g, conv_ln_b, b_pw2, conv_w, meta_tokens)))
    small_m = (m_pre_norm_g, m_post_norm_g, m_conv_b, m_conv_ln_g, m_conv_ln_b, m_b_pw2, m_conv_w, m_meta_tokens)
    small_v = (v_pre_norm_g, v_post_norm_g, v_conv_b, v_conv_ln_g, v_conv_ln_b, v_b_pw2, v_conv_w, v_meta_tokens)
    small_shapes = [small_w[k].shape for k in small_names]
    outs = _sum_adamw(_pack([g_small[k] for k in small_names])[None], _pack([small_w[k] for k in small_names]),
                      _pack(small_m), _pack(small_v), "adamw_small_weights")
    g_s, d_s, nm_s, nv_s = [dict(zip(small_names, _unpack(o, small_shapes))) for o in outs]

    done = outs[1]
    updated[0] = update_layer(0)
    (g_w_in, d_w_in, nm_w_in, nv_w_in), (g_w_pw2, d_w_pw2, nm_w_pw2, nv_w_pw2), (g_w_out, d_w_out, nm_w_out, nv_w_out) = [
        [jnp.stack([updated[l][a][k] for l in range(depth)]) for k in range(4)] for a in range(3)]

    def ordered(s, w_in_, w_pw2_, w_out_):
        return (s["meta"], s["pre_g"], s["post_g"], w_in_, s["conv_w"], s["conv_b"], s["ln_g"], s["ln_b"], w_pw2_,
                s["b_pw2"], w_out_)

    return (loss, dx[None], *ordered(g_s, g_w_in, g_w_pw2, g_w_out), *ordered(d_s, d_w_in, d_w_pw2, d_w_out),
            *ordered(nm_s, nm_w_in, nm_w_pw2, nm_w_out), *ordered(nv_s, nv_w_in, nv_w_pw2, nv_w_out))
```

```python
import functools

import jax
import jax.numpy as jnp
from jax import lax
from jax.experimental import pallas as pl
from jax.experimental.pallas import tpu as pltpu

F32 = jnp.float32
BF16 = jnp.bfloat16

D_MODEL = 1024
D_CONV = 512
D_SB = 512
HEAD_DIM = 64
HEADS_PER_BLOCK = 4
HEAD_BLK = HEADS_PER_BLOCK * HEAD_DIM
CONV_WIDTH = 31
N_META = 16
D_IN = 3 * D_CONV + 4 * D_SB
RMS_EPS = 1e-6
LN_EPS = 1e-5
SB_SCALE = HEAD_DIM ** -0.5

ADAM_LR = 0.001
ADAM_B1 = 0.9
ADAM_B2 = 0.999
ADAM_EPS = 1e-08
ADAM_WD = 0.01
ADAM_STEP = 10

N_DEV = 8
LANES = 128
ROW_BLK = 256
DENSE_ROWS_MAX = 544
HALO = 32
VMEM_LIMIT = 56 * 1024 * 1024
MESH = pl.DeviceIdType.MESH


def _cparams(*sem):
    return pltpu.CompilerParams(dimension_semantics=sem, vmem_limit_bytes=VMEM_LIMIT)


def _rows(n_cols, col=0, rows=ROW_BLK):
    return pl.BlockSpec((rows, n_cols), lambda i, col=col: (i, col))


def _dense_rows(t):
    packed_rows = 16
    return max(d for d in range(packed_rows, min(t, DENSE_ROWS_MAX) + 1, packed_rows) if t % d == 0)


def _whole(shape):
    return pl.BlockSpec(shape, lambda i: (0,) * len(shape))


def _sigmoid(x):
    return jax.nn.sigmoid(x)


def _dsilu(x, s):
    return s * (1.0 + x * (1.0 - s))


def _dot(a, b):
    return jnp.dot(a, b, preferred_element_type=F32)


def _dot_nt(a, b):
    return lax.dot_general(a, b, (((1,), (1,)), ((), ())), preferred_element_type=F32)


def _dot_tn(a, b):
    return lax.dot_general(a, b, (((0,), (0,)), ((), ())), preferred_element_type=F32)


def _inproj_fwd(h, g_pre, w_in):
    t = h.shape[0]
    blk = _dense_rows(t)

    def body(h_ref, g_ref, w_ref, pc_ref, qkv_ref, sbg_ref, u_ref):
        x = h_ref[...]
        r = lax.rsqrt(jnp.mean(x * x, axis=-1, keepdims=True) + RMS_EPS)
        u = (x * r * g_ref[...]).astype(BF16)
        u_ref[...] = u
        pc_ref[...] = _dot(u, w_ref[:, 0:1536])
        qkv_ref[...] = _dot(u, w_ref[:, 1536:3072]).astype(BF16)
        sbg_ref[...] = _dot(u, w_ref[:, 3072:3584])

    return pl.pallas_call(
        body, name="inproj_fwd", grid=(t // blk,),
        in_specs=[_rows(D_MODEL, rows=blk), _whole((1, D_MODEL)), _whole((D_MODEL, D_IN))],
        out_specs=[_rows(1536, rows=blk), _rows(1536, rows=blk), _rows(D_SB, rows=blk), _rows(D_MODEL, rows=blk)],
        out_shape=[jax.ShapeDtypeStruct((t, 1536), F32), jax.ShapeDtypeStruct((t, 1536), BF16),
                   jax.ShapeDtypeStruct((t, D_SB), F32), jax.ShapeDtypeStruct((t, D_MODEL), BF16)],
        compiler_params=_cparams("parallel"),
    )(h, g_pre, w_in)


def _prev_halo(col):
    per = ROW_BLK // HALO
    return pl.BlockSpec((HALO, D_CONV), lambda i, col=col: (jnp.maximum(i * per - 1, 0), col))


def _fill_glu(buf, i, a_ref, b_ref, ha_ref, hb_ref):
    halo = ha_ref[...] * _sigmoid(hb_ref[...])
    buf[0:HALO, :] = jnp.where(i > 0, halo, 0.0)
    buf[HALO:HALO + ROW_BLK, :] = a_ref[...] * _sigmoid(b_ref[...])


SUBLANES = 8
TAP_ROWS = 64
SHIFT_ROWS = HALO + ROW_BLK - SUBLANES


def _fill_shifts(shifts, buf):
    for b in range(1, SUBLANES):
        shifts[b - 1] = buf[pl.ds(b, SHIFT_ROWS), :]


def _window(buf, shifts, first, rows, lanes):
    whole, part = divmod(first, SUBLANES)
    src = buf if part == 0 else shifts.at[part - 1]
    return src[pl.ds(rows.start + SUBLANES * whole, rows.size), lanes]


TAP_ROW_CHUNKS = [pl.ds(r, TAP_ROWS) for r in range(0, ROW_BLK, TAP_ROWS)]
TAP_LANE_TILES = [pl.ds(c, LANES) for c in range(0, D_CONV, LANES)]


def _layer_norm_stats(cv):
    mu = jnp.mean(cv, axis=-1, keepdims=True)
    xc = cv - mu
    rstd = lax.rsqrt(jnp.mean(xc * xc, axis=-1, keepdims=True) + LN_EPS)
    return xc * rstd, rstd


def _conv_fwd(pc, conv_w, conv_b, ln_g, ln_b, w_pw2, b_pw2):
    t = pc.shape[0]

    def body(a_ref, b_ref, gate_ref, ha_ref, hb_ref, cw_ref, cb_ref, lg_ref, lb_ref, wp_ref, bp_ref,
             cout_ref, cv_ref, p_ref, sl_ref, buf, shifts):
        i = pl.program_id(0)
        _fill_glu(buf, i, a_ref, b_ref, ha_ref, hb_ref)
        _fill_shifts(shifts, buf)
        for lanes in TAP_LANE_TILES:
            for rows in TAP_ROW_CHUNKS:
                acc = jnp.zeros((TAP_ROWS, LANES), F32) + cb_ref[:, lanes]
                for j in range(CONV_WIDTH):
                    acc = acc + cw_ref[j:j + 1, lanes] * _window(buf, shifts, HALO - (CONV_WIDTH - 1) + j, rows, lanes)
                cv_ref[rows, lanes] = acc
        xh, _ = _layer_norm_stats(cv_ref[...])
        ln = xh * lg_ref[...] + lb_ref[...]
        sl = (ln * _sigmoid(ln)).astype(BF16)
        sl_ref[...] = sl
        p = _dot(sl, wp_ref[...]) + bp_ref[...]
        p_ref[...] = p
        gate = gate_ref[...]
        cout_ref[...] = (p * (gate * _sigmoid(gate))).astype(BF16)

    vec = _whole((1, D_CONV))
    return pl.pallas_call(
        body, name="conv_fwd", grid=(t // ROW_BLK,),
        in_specs=[_rows(D_CONV, 0), _rows(D_CONV, 1), _rows(D_CONV, 2), _prev_halo(0), _prev_halo(1),
                  _whole((CONV_WIDTH, D_CONV)), vec, vec, vec, _whole((D_CONV, D_CONV)), vec],
        out_specs=[_rows(D_CONV)] * 4,
        out_shape=[jax.ShapeDtypeStruct((t, D_CONV), BF16), jax.ShapeDtypeStruct((t, D_CONV), F32),
                   jax.ShapeDtypeStruct((t, D_CONV), F32), jax.ShapeDtypeStruct((t, D_CONV), BF16)],
        scratch_shapes=[pltpu.VMEM((HALO + ROW_BLK, D_CONV), F32), pltpu.VMEM((SUBLANES - 1, SHIFT_ROWS, D_CONV), F32)],
        compiler_params=_cparams("parallel"),
    )(pc, pc, pc, pc, pc, conv_w, conv_b, ln_g, ln_b, w_pw2, b_pw2)


def _lower_triangle():
    row = lax.broadcasted_iota(jnp.int32, (ROW_BLK, ROW_BLK), 0)
    col = lax.broadcasted_iota(jnp.int32, (ROW_BLK, ROW_BLK), 1)
    return row > col


def _lower_triangle_t():
    row = lax.broadcasted_iota(jnp.int32, (ROW_BLK, ROW_BLK), 0)
    col = lax.broadcasted_iota(jnp.int32, (ROW_BLK, ROW_BLK), 1)
    return row < col


def _tri_sum(x, umat):
    return _dot(x.astype(BF16), umat)


def _log_gates(z):
    ls = -(jnp.maximum(z, 0.0) + jnp.log(1.0 + jnp.exp(-jnp.abs(z))))
    return ls, z + ls


def _head_lanes(hh):
    lane = lax.broadcasted_iota(jnp.int32, (ROW_BLK, HEAD_BLK), 1)
    return (lane >= HEAD_DIM * hh) & (lane < HEAD_DIM * (hh + 1))


def _merge_heads(acc_ref):
    out = acc_ref[HEADS_PER_BLOCK - 1]
    for hh in range(HEADS_PER_BLOCK - 1):
        out = jnp.where(_head_lanes(hh), acc_ref[hh], out)
    return out


def _qkv_specs(t):
    n_blk = D_SB // HEAD_BLK
    return [pl.BlockSpec((ROW_BLK, HEAD_BLK), lambda hp, i: (i, hp)),
            pl.BlockSpec((t, HEAD_BLK), lambda hp, i: (0, n_blk + hp)),
            pl.BlockSpec((t, HEAD_BLK), lambda hp, i: (0, 2 * n_blk + hp))]


def _carry_spec():
    return pl.BlockSpec((HEADS_PER_BLOCK, ROW_BLK, LANES), lambda hp, i: (hp, i, 0))


def _last_block_rows(t, n_tokens):
    packed_rows = 16
    return -(-(n_tokens - (t - ROW_BLK)) // packed_rows) * packed_rows


def _by_block_rows(i, last_rows, sweep):
    if last_rows == ROW_BLK:
        sweep(ROW_BLK)
        return
    last = pl.num_programs(1) - 1
    pl.when(i < last)(lambda: sweep(ROW_BLK))
    pl.when(i == last)(lambda: sweep(last_rows))


def _attn_fwd(qkv, n_tokens):
    t = qkv.shape[0]
    assert t // ROW_BLK <= LANES

    def body(q_ref, k_ref, v_ref, o_ref, c_ref, acc_ref, run_ref, qm_ref, z_ref):
        i = pl.program_id(1)
        q = q_ref[...]
        heads = range(HEADS_PER_BLOCK)
        for hh in heads:
            qm_ref[hh] = jnp.where(_head_lanes(hh), q, jnp.zeros_like(q)) * jnp.asarray(SB_SCALE, BF16)
        acc_ref[...] = jnp.zeros_like(acc_ref)
        c_ref[...] = jnp.zeros_like(c_ref)
        run_ref[...] = jnp.zeros_like(run_ref)

        def sweep(n_rows):
            rows = pl.ds(0, n_rows)
            lower = _lower_triangle()[:n_rows]
            umat = jnp.where(_lower_triangle(), 1.0, 0.0).astype(BF16)
            lane = lax.broadcasted_iota(jnp.int32, (n_rows, LANES), 1)

            def scores(jb):
                start = pl.multiple_of(jb * ROW_BLK, ROW_BLK)
                kb = k_ref[pl.ds(start, ROW_BLK), :]
                for hh in heads:
                    z_ref[hh, rows] = _dot_nt(qm_ref[hh, rows], kb)

            def block(jb, diagonal):
                start = pl.multiple_of(jb * ROW_BLK, ROW_BLK)
                vb = v_ref[pl.ds(start, ROW_BLK), :]
                logits = []
                for hh in heads:
                    ls, lb = _log_gates(z_ref[hh, rows])
                    if diagonal:
                        ls = jnp.where(lower, ls, 0.0)
                    run = run_ref[hh, rows]
                    if not diagonal:
                        c_ref[hh, rows] = jnp.where(lane == jb, run, c_ref[hh, rows])
                    logits.append(lb + jnp.concatenate([run, run], axis=1) + _tri_sum(ls, umat))
                    run_ref[hh, rows] = run + jnp.sum(ls, axis=1, keepdims=True)
                scores(jnp.maximum(jb - 1, 0))
                for hh in heads:
                    a = jnp.exp(logits[hh])
                    if diagonal:
                        a = jnp.where(lower, a, 0.0)
                    acc_ref[hh, rows] += _dot(a.astype(BF16), vb)

            scores(i)
            block(i, True)

            @pl.loop(0, i)
            def _(n):
                block(i - 1 - n, False)

        _by_block_rows(i, _last_block_rows(t, n_tokens), sweep)
        o_ref[...] = _merge_heads(acc_ref)

    per_head = (HEADS_PER_BLOCK, ROW_BLK, HEAD_BLK)
    return pl.pallas_call(
        body, name="attn_fwd", grid=(D_SB // HEAD_BLK, t // ROW_BLK),
        in_specs=_qkv_specs(t),
        out_specs=[pl.BlockSpec((ROW_BLK, HEAD_BLK), lambda hp, i: (i, hp)), _carry_spec()],
        out_shape=[jax.ShapeDtypeStruct((t, D_SB), F32),
                   jax.ShapeDtypeStruct((D_SB // HEAD_DIM, t, LANES), F32)],
        scratch_shapes=[pltpu.VMEM(per_head, F32), pltpu.VMEM((HEADS_PER_BLOCK, ROW_BLK, LANES), F32),
                        pltpu.VMEM(per_head, BF16), pltpu.VMEM((HEADS_PER_BLOCK, ROW_BLK, ROW_BLK), F32)],
        compiler_params=_cparams("arbitrary", "arbitrary"),
    )(qkv, qkv, qkv)


def _outproj_fwd(h, cout, sraw, sbg, w_out, g_post):
    t = h.shape[0]
    blk = _dense_rows(t)

    def body(h_ref, c_ref, s_ref, g_ref, w_ref, gp_ref, hn_ref, mixed_ref, mix_ref):
        gate = g_ref[...]
        mix_ref[:, 0:D_CONV] = c_ref[...]
        mix_ref[:, D_CONV:] = (s_ref[...] * (gate * _sigmoid(gate))).astype(BF16)
        mixed = _dot(mix_ref[...], w_ref[...])
        mixed_ref[...] = mixed
        r = lax.rsqrt(jnp.mean(mixed * mixed, axis=-1, keepdims=True) + RMS_EPS)
        hn_ref[...] = h_ref[...] + mixed * r * gp_ref[...]

    return pl.pallas_call(
        body, name="outproj_fwd", grid=(t // blk,),
        in_specs=[_rows(D_MODEL, rows=blk), _rows(D_CONV, rows=blk), _rows(D_SB, rows=blk), _rows(D_SB, rows=blk),
                  _whole((D_MODEL, D_MODEL)), _whole((1, D_MODEL))],
        out_specs=[_rows(D_MODEL, rows=blk)] * 3,
        out_shape=[jax.ShapeDtypeStruct((t, D_MODEL), F32), jax.ShapeDtypeStruct((t, D_MODEL), F32),
                   jax.ShapeDtypeStruct((t, D_MODEL), BF16)],
        compiler_params=_cparams("parallel"),
    )(h, cout, sraw, sbg, w_out, g_post)


def _loss_and_grad(h, target, seq):
    t = h.shape[0]
    blk = _dense_rows(t)

    def body(h_ref, t_ref, loss_ref, dh_ref):
        i = pl.program_id(0)

        @pl.when(i == 0)
        def _():
            loss_ref[...] = jnp.zeros_like(loss_ref)

        row = i * blk + lax.broadcasted_iota(jnp.int32, (blk, D_MODEL), 0)
        real = (row >= N_META) & (row < N_META + seq)
        diff = jnp.where(real, h_ref[...] - t_ref[...], 0.0)
        sq = jnp.sum(jnp.sum(diff * diff, axis=1, keepdims=True), axis=0, keepdims=True)
        loss_ref[...] += (0.5 / D_MODEL) * sq
        dh_ref[...] = diff * (1.0 / D_MODEL)

    return pl.pallas_call(
        body, name="loss", grid=(t // blk,),
        in_specs=[_rows(D_MODEL, rows=blk), _rows(D_MODEL, rows=blk)],
        out_specs=[_whole((1, 1)), _rows(D_MODEL, rows=blk)],
        out_shape=[jax.ShapeDtypeStruct((1, 1), F32), jax.ShapeDtypeStruct((t, D_MODEL), F32)],
        compiler_params=_cparams("arbitrary"),
    )(h, target)


def _outproj_bwd(dh, mixed, g_post, sraw, sbg, w_out_t):
    t = dh.shape[0]
    blk = _dense_rows(t)

    def body(dh_ref, mixed_ref, gp_ref, s_ref, g_ref, wt_ref, dc_ref, ds_ref, dg_ref, dmb_ref, dgp_ref):
        @pl.when(pl.program_id(0) == 0)
        def _():
            dgp_ref[...] = jnp.zeros_like(dgp_ref)

        mixed = mixed_ref[...]
        r = lax.rsqrt(jnp.mean(mixed * mixed, axis=-1, keepdims=True) + RMS_EPS)
        nh = mixed * r
        dy = dh_ref[...]
        dgp_ref[...] += jnp.sum(dy * nh, axis=0, keepdims=True)
        dn = dy * gp_ref[...]
        dmixed = (r * (dn - nh * jnp.mean(dn * nh, axis=-1, keepdims=True))).astype(BF16)
        dmb_ref[...] = dmixed
        dmix = _dot_nt(dmixed, wt_ref[...])
        dc_ref[...] = dmix[:, 0:D_CONV]
        dsg = dmix[:, D_CONV:]
        gate = g_ref[...]
        sg = _sigmoid(gate)
        ds_ref[...] = dsg * (gate * sg)
        dg_ref[...] = (dsg * s_ref[...] * _dsilu(gate, sg)).astype(BF16)

    return pl.pallas_call(
        body, name="outproj_bwd", grid=(t // blk,),
        in_specs=[_rows(D_MODEL, rows=blk), _rows(D_MODEL, rows=blk), _whole((1, D_MODEL)), _rows(D_SB, rows=blk),
                  _rows(D_SB, rows=blk), _whole((D_MODEL, D_MODEL))],
        out_specs=[_rows(D_CONV, rows=blk), _rows(D_SB, rows=blk), _rows(D_SB, rows=blk), _rows(D_MODEL, rows=blk),
                   _whole((1, D_MODEL))],
        out_shape=[jax.ShapeDtypeStruct((t, D_CONV), F32), jax.ShapeDtypeStruct((t, D_SB), F32),
                   jax.ShapeDtypeStruct((t, D_SB), BF16), jax.ShapeDtypeStruct((t, D_MODEL), BF16),
                   jax.ShapeDtypeStruct((1, D_MODEL), F32)],
        compiler_params=_cparams("arbitrary"),
    )(dh, mixed, g_post, sraw, sbg, w_out_t)


def _attn_bwd(qkv, carries, do, n_tokens):
    t = qkv.shape[0]

    def body(q_ref, k_ref, v_ref, c_ref, do_ref, dq_ref, dk_ref, dv_ref, acc_ref, seen_ref, qm_ref, dom_ref, z_ref,
             da_ref, dz_ref, a_ref):
        i = pl.program_id(1)

        @pl.when(i == 0)
        def _():
            dk_ref[...] = jnp.zeros_like(dk_ref)
            dv_ref[...] = jnp.zeros_like(dv_ref)

        q = q_ref[...]
        dof = do_ref[...]
        heads = range(HEADS_PER_BLOCK)
        for hh in heads:
            qm_ref[hh] = jnp.where(_head_lanes(hh), q, jnp.zeros_like(q)) * jnp.asarray(SB_SCALE, BF16)
            dom_ref[hh] = jnp.where(_head_lanes(hh), dof, 0.0).astype(BF16)
        acc_ref[...] = jnp.zeros_like(acc_ref)
        seen_ref[...] = jnp.zeros_like(seen_ref)

        def sweep(n_rows):
            rows = pl.ds(0, n_rows)
            lower = _lower_triangle()[:n_rows]
            umat = jnp.where(_lower_triangle(), 1.0, 0.0).astype(BF16)
            umat_t = jnp.where(_lower_triangle_t(), 1.0, 0.0).astype(BF16)
            lane = lax.broadcasted_iota(jnp.int32, (n_rows, LANES), 1)

            def scores(jb):
                start = pl.multiple_of(jb * ROW_BLK, ROW_BLK)
                kb = k_ref[pl.ds(start, ROW_BLK), :]
                for hh in heads:
                    z_ref[hh, rows] = _dot_nt(qm_ref[hh, rows], kb)

            def value_grads(jb):
                start = pl.multiple_of(jb * ROW_BLK, ROW_BLK)
                vb = v_ref[pl.ds(start, ROW_BLK), :]
                for hh in heads:
                    da_ref[hh, rows] = _dot_nt(dom_ref[hh, rows], vb)

            def products(jb, hh):
                start = pl.multiple_of(jb * ROW_BLK, ROW_BLK)
                dzb = dz_ref[hh, rows]
                acc_ref[hh, rows] += _dot(dzb, k_ref[pl.ds(start, ROW_BLK), :])
                dk_ref[pl.ds(start, ROW_BLK), :] += _dot_tn(dzb, qm_ref[hh, rows])
                dv_ref[pl.ds(start, ROW_BLK), :] += _dot_tn(a_ref[hh, rows], dom_ref[hh, rows])

            def block(jb, diagonal):
                before = jnp.maximum(jb - 1, 0)
                lbs, logits = [], []
                for hh in heads:
                    products(before, hh)
                    ls, lb = _log_gates(z_ref[hh, rows])
                    if diagonal:
                        ls = jnp.where(lower, ls, 0.0)
                        logits.append(lb + _tri_sum(ls, umat))
                    else:
                        right = jnp.sum(jnp.where(lane == jb, c_ref[hh, rows], 0.0), axis=1, keepdims=True)
                        logits.append(lb + right + _tri_sum(ls, umat))
                    lbs.append(lb)
                if not diagonal:
                    scores(jb + 1)
                gs, befores = [], []
                for hh in heads:
                    a = jnp.exp(logits[hh])
                    if diagonal:
                        a = jnp.where(lower, a, 0.0)
                    g = da_ref[hh, rows] * a
                    seen = seen_ref[hh, rows]
                    befores.append(jnp.concatenate([seen, seen], axis=1) + _tri_sum(g, umat_t))
                    seen_ref[hh, rows] = seen + jnp.sum(g, axis=1, keepdims=True)
                    a_ref[hh, rows] = a.astype(BF16)
                    gs.append(g)
                if not diagonal:
                    value_grads(jb + 1)
                for hh in heads:
                    dz = gs[hh] - jnp.exp(lbs[hh]) * (gs[hh] + befores[hh])
                    if diagonal:
                        dz = jnp.where(lower, dz, 0.0)
                    dz_ref[hh, rows] = dz.astype(BF16)

            dz_ref[...] = jnp.zeros_like(dz_ref)
            a_ref[...] = jnp.zeros_like(a_ref)
            scores(0)
            value_grads(0)

            @pl.loop(0, i)
            def _(jb):
                block(jb, False)

            block(i, True)
            for hh in heads:
                products(i, hh)

        _by_block_rows(i, _last_block_rows(t, n_tokens), sweep)
        dq_ref[...] = (_merge_heads(acc_ref) * SB_SCALE).astype(BF16)

    blk = pl.BlockSpec((ROW_BLK, HEAD_BLK), lambda hp, i: (i, hp))
    full = pl.BlockSpec((t, HEAD_BLK), lambda hp, i: (0, hp))
    per_head = (HEADS_PER_BLOCK, ROW_BLK, HEAD_BLK)
    return pl.pallas_call(
        body, name="attn_bwd", grid=(D_SB // HEAD_BLK, t // ROW_BLK),
        in_specs=_qkv_specs(t) + [_carry_spec(), blk],
        out_specs=[blk, full, full],
        out_shape=[jax.ShapeDtypeStruct((t, D_SB), BF16)] + [jax.ShapeDtypeStruct((t, D_SB), F32)] * 2,
        scratch_shapes=[pltpu.VMEM(per_head, F32), pltpu.VMEM((HEADS_PER_BLOCK, ROW_BLK, LANES), F32),
                        pltpu.VMEM(per_head, BF16), pltpu.VMEM(per_head, BF16)]
                       + [pltpu.VMEM((HEADS_PER_BLOCK, ROW_BLK, ROW_BLK), dtype) for dtype in (F32, F32, BF16, BF16)],
        compiler_params=_cparams("arbitrary", "arbitrary"),
    )(qkv, qkv, qkv, carries, do)


def _conv_bwd_rows(dcout, pc, cv, p, ln_g, ln_b, w_pw2_t):
    t = dcout.shape[0]
    blk = _dense_rows(t)

    def body(dc_ref, gate_ref, cv_ref, p_ref, lg_ref, lb_ref, wt_ref, dcv_ref, dgate_ref, dpb_ref, vec_ref):
        @pl.when(pl.program_id(0) == 0)
        def _():
            vec_ref[...] = jnp.zeros_like(vec_ref)

        dc = dc_ref[...]
        gate = gate_ref[...]
        sg = _sigmoid(gate)
        dp = dc * (gate * sg)
        dgate_ref[...] = (dc * p_ref[...] * _dsilu(gate, sg)).astype(BF16)
        dpb = dp.astype(BF16)
        dpb_ref[...] = dpb
        xh, rstd = _layer_norm_stats(cv_ref[...])
        ln = xh * lg_ref[...] + lb_ref[...]
        s2 = _sigmoid(ln)
        dln = _dot_nt(dpb, wt_ref[...]) * _dsilu(ln, s2)
        dxh = dln * lg_ref[...]
        dcv = rstd * (dxh - jnp.mean(dxh, axis=-1, keepdims=True)
                      - xh * jnp.mean(dxh * xh, axis=-1, keepdims=True))
        dcv_ref[...] = dcv
        vec_ref[0:1, :] += jnp.sum(dp, axis=0, keepdims=True)
        vec_ref[1:2, :] += jnp.sum(dln * xh, axis=0, keepdims=True)
        vec_ref[2:3, :] += jnp.sum(dln, axis=0, keepdims=True)
        vec_ref[3:4, :] += jnp.sum(dcv, axis=0, keepdims=True)

    vec = _whole((1, D_CONV))
    return pl.pallas_call(
        body, name="conv_bwd_rows", grid=(t // blk,),
        in_specs=[_rows(D_CONV, rows=blk), _rows(D_CONV, 2, rows=blk), _rows(D_CONV, rows=blk), _rows(D_CONV, rows=blk),
                  vec, vec, _whole((D_CONV, D_CONV))],
        out_specs=[_rows(D_CONV, rows=blk), _rows(D_CONV, rows=blk), _rows(D_CONV, rows=blk), _whole((8, D_CONV))],
        out_shape=[jax.ShapeDtypeStruct((t, D_CONV), F32), jax.ShapeDtypeStruct((t, D_CONV), BF16),
                   jax.ShapeDtypeStruct((t, D_CONV), BF16), jax.ShapeDtypeStruct((8, D_CONV), F32)],
        compiler_params=_cparams("arbitrary"),
    )(dcout, pc, cv, p, ln_g, ln_b, w_pw2_t)


def _conv_bwd_taps(dcv, pc, conv_w):
    t = dcv.shape[0]
    n_halo = t // HALO
    per = ROW_BLK // HALO

    def body(d_ref, dn_ref, a_ref, b_ref, ha_ref, hb_ref, cw_ref, da_ref, db_ref, dw_ref, cbuf, dbuf, cshifts, dshifts):
        i = pl.program_id(0)

        @pl.when(i == 0)
        def _():
            dw_ref[...] = jnp.zeros_like(dw_ref)

        _fill_glu(cbuf, i, a_ref, b_ref, ha_ref, hb_ref)
        dbuf[0:ROW_BLK, :] = d_ref[...]
        dbuf[ROW_BLK:ROW_BLK + HALO, :] = jnp.where(i < pl.num_programs(0) - 1, dn_ref[...], 0.0)
        _fill_shifts(cshifts, cbuf)
        _fill_shifts(dshifts, dbuf)
        for lanes in TAP_LANE_TILES:
            for rows in TAP_ROW_CHUNKS:
                acc = jnp.zeros((TAP_ROWS, LANES), F32)
                for j in range(CONV_WIDTH):
                    acc = acc + cw_ref[j:j + 1, lanes] * _window(dbuf, dshifts, CONV_WIDTH - 1 - j, rows, lanes)
                sb = _sigmoid(b_ref[rows, lanes])
                da_ref[rows, lanes] = (acc * sb).astype(BF16)
                db_ref[rows, lanes] = (acc * a_ref[rows, lanes] * sb * (1.0 - sb)).astype(BF16)
            for j in range(CONV_WIDTH):
                acc = jnp.zeros((TAP_ROWS, LANES), F32)
                for rows in TAP_ROW_CHUNKS:
                    acc = acc + d_ref[rows, lanes] * _window(cbuf, cshifts, HALO - (CONV_WIDTH - 1) + j, rows, lanes)
                dw_ref[j:j + 1, lanes] += jnp.sum(acc, axis=0, keepdims=True)

    return pl.pallas_call(
        body, name="conv_bwd_taps", grid=(t // ROW_BLK,),
        in_specs=[_rows(D_CONV),
                  pl.BlockSpec((HALO, D_CONV), lambda i: (jnp.minimum((i + 1) * per, n_halo - 1), 0)),
                  _rows(D_CONV, 0), _rows(D_CONV, 1), _prev_halo(0), _prev_halo(1),
                  _whole((CONV_WIDTH, D_CONV))],
        out_specs=[_rows(D_CONV), _rows(D_CONV), _whole((32, D_CONV))],
        out_shape=[jax.ShapeDtypeStruct((t, D_CONV), BF16), jax.ShapeDtypeStruct((t, D_CONV), BF16),
                   jax.ShapeDtypeStruct((32, D_CONV), F32)],
        scratch_shapes=[pltpu.VMEM((HALO + ROW_BLK, D_CONV), F32), pltpu.VMEM((ROW_BLK + HALO, D_CONV), F32),
                        pltpu.VMEM((SUBLANES - 1, SHIFT_ROWS, D_CONV), F32),
                        pltpu.VMEM((SUBLANES - 1, SHIFT_ROWS, D_CONV), F32)],
        compiler_params=_cparams("arbitrary"),
    )(dcv, dcv, pc, pc, pc, pc, conv_w)


def _inproj_bwd(dh_out, h, g_pre, pieces, w_in_t):
    t = h.shape[0]
    blk = _dense_rows(t)

    def body(dh_ref, h_ref, g_ref, *rest):
        piece_refs, (wt_ref, dhin_ref, dproj_ref, dg_ref) = rest[:7], rest[7:]

        @pl.when(pl.program_id(0) == 0)
        def _():
            dg_ref[...] = jnp.zeros_like(dg_ref)

        for k, ref in enumerate(piece_refs):
            dproj_ref[:, 512 * k:512 * (k + 1)] = ref[...].astype(BF16)
        du = _dot_nt(dproj_ref[...], wt_ref[...])
        x = h_ref[...]
        r = lax.rsqrt(jnp.mean(x * x, axis=-1, keepdims=True) + RMS_EPS)
        xh = x * r
        dg_ref[...] += jnp.sum(du * xh, axis=0, keepdims=True)
        dxh = du * g_ref[...]
        dhin_ref[...] = dh_ref[...] + r * (dxh - xh * jnp.mean(dxh * xh, axis=-1, keepdims=True))

    return pl.pallas_call(
        body, name="inproj_bwd", grid=(t // blk,),
        in_specs=[_rows(D_MODEL, rows=blk), _rows(D_MODEL, rows=blk), _whole((1, D_MODEL))] + [_rows(512, rows=blk)] * 7
                 + [_whole((D_MODEL, D_IN))],
        out_specs=[_rows(D_MODEL, rows=blk), _rows(D_IN, rows=blk), _whole((1, D_MODEL))],
        out_shape=[jax.ShapeDtypeStruct((t, D_MODEL), F32), jax.ShapeDtypeStruct((t, D_IN), BF16),
                   jax.ShapeDtypeStruct((1, D_MODEL), F32)],
        compiler_params=_cparams("arbitrary"),
    )(dh_out, h, g_pre, *pieces, w_in_t)


def _weight_grad(xb, dyb, name):
    t, k = xb.shape
    blk = _dense_rows(t)
    n = dyb.shape[1]

    def body(x_ref, dy_ref, o_ref, acc_ref):
        i = pl.program_id(0)

        @pl.when(i == 0)
        def _():
            acc_ref[...] = jnp.zeros_like(acc_ref)

        acc_ref[...] += _dot_tn(x_ref[...], dy_ref[...])

        @pl.when(i == pl.num_programs(0) - 1)
        def _():
            o_ref[...] = acc_ref[...].astype(BF16)

    return pl.pallas_call(
        body, name=name, grid=(t // blk,),
        in_specs=[_rows(k, rows=blk), _rows(n, rows=blk)], out_specs=_whole((k, n)),
        out_shape=jax.ShapeDtypeStruct((k, n), BF16),
        scratch_shapes=[pltpu.VMEM((k, n), F32)],
        compiler_params=_cparams("arbitrary"),
    )(xb, dyb)


def _position():
    return lax.axis_index("x"), lax.axis_index("y"), lax.axis_index("c")


def _comm_call(body, name, ins, out_shapes):
    n = len(ins)
    hbm = pl.BlockSpec(memory_space=pltpu.HBM)
    return pl.pallas_call(
        functools.partial(body, n), name=name, in_specs=[hbm] * n, out_specs=[hbm] * n, out_shape=out_shapes,
        scratch_shapes=[pltpu.SemaphoreType.DMA((n, N_DEV - 1)), pltpu.SemaphoreType.DMA((n, N_DEV - 1)),
                        pltpu.SemaphoreType.DMA((n,))],
    )(*ins)


def _all_gather(blocks, name):
    def body(n, *refs):
        x_refs, out_refs, (send_sems, recv_sems, local_sems) = refs[:n], refs[n:2 * n], refs[2 * n:]
        x, y, c = _position()
        me, sibling = (x, y, c), (x, y, 1 - c)
        chips = [(1 - x, y), (x, 1 - y), (1 - x, 1 - y)]

        def slot(a, px, py, pc):
            return out_refs[a].at[4 * px + 2 * py + pc]

        def copy(a, k, origin, to, own=False):
            return pltpu.make_async_remote_copy(
                src_ref=x_refs[a] if own else slot(a, *origin), dst_ref=slot(a, *origin),
                send_sem=send_sems.at[a, k], recv_sem=recv_sems.at[a, k], device_id=to, device_id_type=MESH)

        arrays = range(n)
        mine = [pltpu.make_async_copy(x_refs[a], slot(a, *me), local_sems.at[a]) for a in arrays]
        first = [copy(a, 1 + j, me, (*chip, c), own=True) for j, chip in enumerate(chips) for a in arrays]
        first += [copy(a, 0, me, sibling, own=True) for a in arrays]
        for cp in mine + first:
            cp.start()
        passed = []
        for j, chip in enumerate(chips):
            for a in arrays:
                copy(a, 1 + j, (*chip, c), me).wait_recv()
                passed.append(copy(a, 4 + j, (*chip, c), sibling))
                passed[-1].start()
        for a in arrays:
            copy(a, 0, sibling, me).wait_recv()
            for j, chip in enumerate(chips):
                copy(a, 4 + j, (*chip, 1 - c), me).wait_recv()
        for cp in first + passed:
            cp.wait_send()
        for cp in mine:
            cp.wait()

    return _comm_call(body, name, blocks, [jax.ShapeDtypeStruct((N_DEV,) + b.shape, b.dtype) for b in blocks])


def _exchange_copies(g_refs, land_refs, sems, gather):
    x, y, c = _position()
    me = 4 * x + 2 * y + c
    out = []
    for g_ref, land_ref, (send_sem, recv_sem, local_sem) in zip(g_refs, land_refs, sems):
        def mine(slot, g_ref=g_ref):
            return g_ref if gather else g_ref.at[slot]

        def remote(src, dst, dev):
            return pltpu.make_async_remote_copy(src_ref=src, dst_ref=dst, send_sem=send_sem, recv_sem=recv_sem,
                                                device_id=dev, device_id_type=MESH)

        sends = []
        for k in range(1, N_DEV):
            px = 1 - x if k & 4 else x
            py = 1 - y if k & 2 else y
            pc = 1 - c if k & 1 else c
            sends.append(remote(mine(4 * px + 2 * py + pc), land_ref.at[me], (px, py, pc)))
        seven = land_ref.at[pl.ds(0, N_DEV - 1)]
        out.append((pltpu.make_async_copy(mine(me), land_ref.at[me], local_sem), sends, remote(seven, seven, (x, y, c))))
    return out


_HBM = pl.BlockSpec(memory_space=pltpu.HBM)
_SEM = pl.BlockSpec(memory_space=pltpu.SEMAPHORE)
_ORDERED = pltpu.CompilerParams(has_side_effects=pltpu.SideEffectType.DATAFLOW_SIDE_EFFECTING)
SEMS_PER_ARRAY = 3


def _exchange_start(arrays, after, name, gather):
    n = len(arrays)
    n_sems = SEMS_PER_ARRAY * n

    def body(*refs):
        g_refs, land_refs, sems, token = refs[:n], refs[n:2 * n], refs[2 * n + 1:2 * n + 1 + n_sems], refs[-1]
        sems = [sems[SEMS_PER_ARRAY * a:SEMS_PER_ARRAY * (a + 1)] for a in range(n)]
        for local, sends, _ in _exchange_copies(g_refs, land_refs, sems, gather):
            local.start()
            for cp in sends:
                cp.start()
        token[...] = jnp.zeros_like(token)

    buffers = list(arrays) + [lax.empty((N_DEV,) + g.shape if gather else g.shape, g.dtype) for g in arrays]
    outs = pl.pallas_call(
        body, name=name, in_specs=[_HBM] * (2 * n) + [pl.BlockSpec(memory_space=pl.ANY)],
        out_specs=[_SEM] * n_sems + [_HBM] * (2 * n) + [pl.BlockSpec(memory_space=pltpu.VMEM)],
        out_shape=[pltpu.SemaphoreType.DMA(())] * n_sems + [pltpu.HBM(b.shape, b.dtype) for b in buffers]
                  + [jax.ShapeDtypeStruct((8, LANES), F32)],
        input_output_aliases={a: n_sems + a for a in range(2 * n)}, compiler_params=_ORDERED,
    )(*[pltpu.with_memory_space_constraint(b, pltpu.HBM) for b in buffers], after)
    return outs[:n_sems], outs[n_sems:n_sems + n], outs[n_sems + n:n_sems + 2 * n], outs[-1]


def _exchange_wait(sems, arrays, landings, after, name, gather):
    n = len(arrays)
    n_sems = SEMS_PER_ARRAY * n

    def body(*refs):
        g_refs, land_refs, sems = refs[:n], refs[n:2 * n], refs[2 * n:2 * n + n_sems]
        sems = [sems[SEMS_PER_ARRAY * a:SEMS_PER_ARRAY * (a + 1)] for a in range(n)]
        for local, _, all_seven in _exchange_copies(g_refs, land_refs, sems, gather):
            all_seven.wait_recv()
            all_seven.wait_send()
            local.wait()

    buffers = list(arrays) + list(landings)
    outs = pl.pallas_call(
        body, name=name, in_specs=[_HBM] * (2 * n) + [_SEM] * n_sems + [pl.BlockSpec(memory_space=pl.ANY)],
        out_specs=[_HBM] * (2 * n), out_shape=[pltpu.HBM(b.shape, b.dtype) for b in buffers],
        input_output_aliases={a: a for a in range(2 * n)}, compiler_params=_ORDERED,
    )(*buffers, *sems, after)
    return outs[n:]


def _block_rows(r, row_bytes, budget=1 << 20):
    cap = max(8, budget // row_bytes)
    return max(d for d in range(8, min(r, cap) + 1, 8) if r % d == 0)


def _sum_adamw(parts, w, m, v, name):
    n_parts, r, c = parts.shape
    br = _block_rows(r, 4 * c)

    def body(p_ref, w_ref, m_ref, v_ref, g_out, d_out, m_out, v_out):
        g = p_ref[0].astype(F32)
        for s in range(1, n_parts):
            g = g + p_ref[s].astype(F32)
        m_new = ADAM_B1 * m_ref[...] + (1.0 - ADAM_B1) * g
        v_new = ADAM_B2 * v_ref[...] + (1.0 - ADAM_B2) * (g * g)
        m_hat = m_new / (1.0 - ADAM_B1 ** ADAM_STEP)
        v_hat = v_new / (1.0 - ADAM_B2 ** ADAM_STEP)
        g_out[...] = g
        d_out[...] = -ADAM_LR * (m_hat / (jnp.sqrt(v_hat) + ADAM_EPS) + ADAM_WD * w_ref[...])
        m_out[...] = m_new
        v_out[...] = v_new

    row = pl.BlockSpec((br, c), lambda i: (i, 0))
    return pl.pallas_call(
        body, name=name, grid=(r // br,),
        in_specs=[pl.BlockSpec((n_parts, br, c), lambda i: (0, i, 0)), row, row, row],
        out_specs=[row] * 4, out_shape=[jax.ShapeDtypeStruct((r, c), F32)] * 4,
        compiler_params=_cparams("parallel"),
    )(parts, w, m, v)


def _sum_parts(parts, name):
    n_parts, r, c = parts.shape

    def body(p_ref, o_ref):
        g = p_ref[0]
        for s in range(1, n_parts):
            g = g + p_ref[s]
        o_ref[...] = g

    return pl.pallas_call(
        body, name=name, in_specs=[pl.BlockSpec(memory_space=pltpu.VMEM)],
        out_specs=pl.BlockSpec(memory_space=pltpu.VMEM), out_shape=jax.ShapeDtypeStruct((r, c), F32),
    )(parts)


def _pack(arrays):
    flat = jnp.concatenate([a.reshape(-1) for a in arrays])
    pad = -flat.shape[0] % (8 * LANES)
    if pad:
        flat = jnp.pad(flat, (0, pad))
    return flat.reshape(-1, LANES)


def _unpack(buf, shapes):
    flat = buf.reshape(-1)
    out, at = [], 0
    for shape in shapes:
        size = 1
        for d in shape:
            size *= d
        out.append(lax.slice_in_dim(flat, at, at + size).reshape(shape))
        at += size
    return out


def _local_step(x, target, meta, pre_g, post_g, conv_w, conv_b, ln_g, ln_b, b_pw2, weights, ship, ship_small):
    depth = pre_g.shape[0]
    seq = x.shape[0]
    t = -(-(N_META + seq) // ROW_BLK) * ROW_BLK
    tail = t - N_META - seq
    h = jnp.concatenate([meta, x, jnp.zeros((tail, D_MODEL), F32)], axis=0)
    target = jnp.pad(target, ((N_META, tail), (0, 0)))
    row = lambda a, l: a[l][None, :]

    saved = []
    for l in range(depth):
        w_in, w_pw2, w_out = weights(l, h)
        pc, qkv, sbg, u = _inproj_fwd(h, row(pre_g, l), w_in)
        cout, cv, p, sl = _conv_fwd(pc, conv_w[l], row(conv_b, l), row(ln_g, l), row(ln_b, l), w_pw2, row(b_pw2, l))
        sraw, carries = _attn_fwd(qkv, N_META + seq)
        h_new, mixed, mix = _outproj_fwd(h, cout, sraw, sbg, w_out, row(post_g, l))
        saved.append((h, pc, qkv, sbg, u, cv, p, sl, sraw, carries, mixed, mix, w_in, w_pw2, w_out))
        h = h_new

    loss, dh = _loss_and_grad(h, target, seq)

    grads = {k: [None] * depth for k in ("pre_g", "post_g", "conv_w", "conv_b", "ln_g", "ln_b", "b_pw2")}
    token = jnp.zeros((8, LANES), F32)
    for l in reversed(range(depth)):
        h_in, pc, qkv, sbg, u, cv, p, sl, sraw, carries, mixed, mix, w_in_t, w_pw2_t, w_out_t = saved[l]
        dcout, dsraw, dsbg, dmixed, dg_post = _outproj_bwd(dh, mixed, row(post_g, l) + token[:1, :1], sraw, sbg, w_out_t)
        dq, dk, dv = _attn_bwd(qkv, carries, dsraw, N_META + seq)
        dcv, dgate, dpb, vecs = _conv_bwd_rows(dcout, pc, cv, p, row(ln_g, l), row(ln_b, l), w_pw2_t)
        da, db, dconv_w = _conv_bwd_taps(dcv, pc, conv_w[l])
        dh, dproj, dg_pre = _inproj_bwd(dh, h_in, row(pre_g, l), (da, db, dgate, dq, dk, dv, dsbg), w_in_t)
        grads["pre_g"][l] = dg_pre[0]
        grads["post_g"][l] = dg_post[0]
        grads["b_pw2"][l], grads["ln_g"][l], grads["ln_b"][l], grads["conv_b"][l] = vecs[0], vecs[1], vecs[2], vecs[3]
        grads["conv_w"][l] = dconv_w[:CONV_WIDTH]
        after = dh
        if l == 0:
            grads = {k: jnp.stack(v) for k, v in grads.items()}
            grads["meta"] = dh[:N_META]
            after = ship_small(grads, loss[0, 0])
        token = ship(l, after, _weight_grad(u, dproj, "w_in_grad"), _weight_grad(sl, dpb, "w_pw2_grad"),
                     _weight_grad(mix, dmixed, "w_out_grad"))

    return dh[N_META:N_META + seq], token


def _shard_major(full, axis):
    shape = full.shape
    split = full.reshape(shape[:axis] + (N_DEV, shape[axis] // N_DEV) + shape[axis + 1:])
    return jnp.moveaxis(split, axis, 0)


def _whole_from_shards(shards, axis):
    moved = jnp.moveaxis(shards, 0, axis)
    shape = moved.shape
    return moved.reshape(shape[:axis] + (shape[axis] * shape[axis + 1],) + shape[axis + 2:])


def kernel(x, meta_tokens, pre_norm_g, post_norm_g, w_in, conv_w, conv_b, conv_ln_g, conv_ln_b, w_pw2, b_pw2, w_out, loss_target, m_meta_tokens, m_pre_norm_g, m_post_norm_g, m_w_in, m_conv_w, m_conv_b, m_conv_ln_g, m_conv_ln_b, m_w_pw2, m_b_pw2, m_w_out, v_meta_tokens, v_pre_norm_g, v_post_norm_g, v_w_in, v_conv_w, v_conv_b, v_conv_ln_g, v_conv_ln_b, v_w_pw2, v_b_pw2, v_w_out):
    me = 4 * lax.axis_index("x") + 2 * lax.axis_index("y") + lax.axis_index("c")

    depth = w_in.shape[0]
    big = [w.astype(BF16) for w in (w_in, w_pw2, w_out)]
    *first, conv_w_s, meta_s = _all_gather([w[0] for w in big] + [conv_w, meta_tokens], "gather_first_layer")
    *gathering, token = _exchange_start([w[l] for l in range(1, depth) for w in big], meta_s, "gather_start", gather=True)
    conv_w_full = _whole_from_shards(conv_w_s, 2)
    meta_full = _whole_from_shards(meta_s, 1)
    shard_axis = (1, 0, 0)
    later = []

    def weights(l, h):
        if l == 0:
            return [_whole_from_shards(s, axis) for s, axis in zip(first, shard_axis)]
        if not later:
            later.extend(_exchange_wait(*gathering, h, "gather_wait", gather=True))
        return [_whole_from_shards(s, axis) for s, axis in zip(later[len(big) * (l - 1):len(big) * l], shard_axis)]

    in_flight = [None] * depth

    def ship(l, dh, dw_in, dw_pw2, dw_out):
        slabs = [_shard_major(dw, axis) for dw, axis in zip((dw_in, dw_pw2, dw_out), shard_axis)]
        *in_flight[l], token = _exchange_start(slabs, dh, f"exchange_start_{l}", gather=False)
        return token

    small_names = ("pre_g", "post_g", "conv_b", "ln_g", "ln_b", "b_pw2", "conv_w", "meta")
    small_in_flight, small_shapes_full = [], []

    def ship_small(grads, loss):
        small_full = [grads[k] for k in small_names] + [loss.reshape(1)]
        small_shapes_full.extend(a.shape for a in small_full)
        *in_flight_now, token = _exchange_start([_pack(small_full)], grads["meta"], "small_grads_start", gather=True)
        small_in_flight.extend(in_flight_now)
        return token

    dx, shipped = _local_step(x[0], loss_target[0], meta_full, pre_norm_g + token[:1, :1], post_norm_g, conv_w_full,
                              conv_b, conv_ln_g, conv_ln_b, b_pw2, weights, ship, ship_small)

    updated = [None] * depth
    done = shipped

    def update_layer(l):
        landed = _exchange_wait(*in_flight[l], done, f"exchange_wait_{l}", gather=False)
        return [_sum_adamw(parts, w[l], m[l], v[l], name) for parts, w, m, v, name in zip(
            landed, (w_in, w_pw2, w_out), (m_w_in, m_w_pw2, m_w_out), (v_w_in, v_w_pw2, v_w_out),
            ("adamw_w_in", "adamw_w_pw2", "adamw_w_out"))]

    for l in reversed(range(1, depth)):
        updated[l] = update_layer(l)
        done = updated[l][0][1]

    gathered, = _exchange_wait(*small_in_flight, done, "small_grads_wait", gather=True)
    summed = _unpack(_sum_parts(gathered, "sum_small_grads"), small_shapes_full)
    loss = summed[-1][0]
    g_small = dict(zip(small_names, summed))
    g_small["conv_w"] = lax.dynamic_slice_in_dim(g_small["conv_w"], me * conv_w.shape[2], conv_w.shape[2], axis=2)
    g_small["meta"] = lax.dynamic_slice_in_dim(g_small["meta"], me * meta_tokens.shape[1], meta_tokens.shape[1], axis=1)
    small_w = dict(zip(small_names, (pre_norm_g, post_norm_g, conv_b, conv_ln_g, conv_ln_b, b_pw2, conv_w, meta_tokens)))
    small_m = (m_pre_norm_g, m_post_norm_g, m_conv_b, m_conv_ln_g, m_conv_ln_b, m_b_pw2, m_conv_w, m_meta_tokens)
    small_v = (v_pre_norm_g, v_post_norm_g, v_conv_b, v_conv_ln_g, v_conv_ln_b, v_b_pw2, v_conv_w, v_meta_tokens)
    small_shapes = [small_w[k].shape for k in small_names]
    outs = _sum_adamw(_pack([g_small[k] for k in small_names])[None], _pack([small_w[k] for k in small_names]),
                      _pack(small_m), _pack(small_v), "adamw_small_weights")
    g_s, d_s, nm_s, nv_s = [dict(zip(small_names, _unpack(o, small_shapes))) for o in outs]

    done = outs[1]
    updated[0] = update_layer(0)
    (g_w_in, d_w_in, nm_w_in, nv_w_in), (g_w_pw2, d_w_pw2, nm_w_pw2, nv_w_pw2), (g_w_out, d_w_out, nm_w_out, nv_w_out) = [
        [jnp.stack([updated[l][a][k] for l in range(depth)]) for k in range(4)] for a in range(3)]

    def ordered(s, w_in_, w_pw2_, w_out_):
        return (s["meta"], s["pre_g"], s["post_g"], w_in_, s["conv_w"], s["conv_b"], s["ln_g"], s["ln_b"], w_pw2_,
                s["b_pw2"], w_out_)

    return (loss, dx[None], *ordered(g_s, g_w_in, g_w_pw2, g_w_out), *ordered(d_s, d_w_in, d_w_pw2, d_w_out),
            *ordered(nm_s, nm_w_in, nm_w_pw2, nm_w_out), *ordered(nv_s, nv_w_in, nv_w_pw2, nv_w_out))
```

```python
import functools

import jax
import jax.numpy as jnp
from jax import lax
from jax.experimental import pallas as pl
from jax.experimental.pallas import tpu as pltpu

F32 = jnp.float32
BF16 = jnp.bfloat16

D_MODEL = 1024
D_CONV = 512
D_SB = 512
HEAD_DIM = 64
HEADS_PER_BLOCK = 4
HEAD_BLK = HEADS_PER_BLOCK * HEAD_DIM
CONV_WIDTH = 31
N_META = 16
D_IN = 3 * D_CONV + 4 * D_SB
RMS_EPS = 1e-6
LN_EPS = 1e-5
SB_SCALE = HEAD_DIM ** -0.5

ADAM_LR = 0.001
ADAM_B1 = 0.9
ADAM_B2 = 0.999
ADAM_EPS = 1e-08
ADAM_WD = 0.01
ADAM_STEP = 10

N_DEV = 8
LANES = 128
ROW_BLK = 256
DENSE_ROWS_MAX = 544
HALO = 32
VMEM_LIMIT = 56 * 1024 * 1024
MESH = pl.DeviceIdType.MESH


def _cparams(*sem):
    return pltpu.CompilerParams(dimension_semantics=sem, vmem_limit_bytes=VMEM_LIMIT)


def _rows(n_cols, col=0, rows=ROW_BLK):
    return pl.BlockSpec((rows, n_cols), lambda i, col=col: (i, col))


def _dense_rows(t):
    packed_rows = 16
    return max(d for d in range(packed_rows, min(t, DENSE_ROWS_MAX) + 1, packed_rows) if t % d == 0)


def _whole(shape):
    return pl.BlockSpec(shape, lambda i: (0,) * len(shape))


def _sigmoid(x):
    return jax.nn.sigmoid(x)


def _dsilu(x, s):
    return s * (1.0 + x * (1.0 - s))


def _dot(a, b):
    return jnp.dot(a, b, preferred_element_type=F32)


def _dot_nt(a, b):
    return lax.dot_general(a, b, (((1,), (1,)), ((), ())), preferred_element_type=F32)


def _dot_tn(a, b):
    return lax.dot_general(a, b, (((0,), (0,)), ((), ())), preferred_element_type=F32)


def _inproj_fwd(h, g_pre, w_in):
    t = h.shape[0]
    blk = _dense_rows(t)

    def body(h_ref, g_ref, w_ref, pc_ref, qkv_ref, sbg_ref, u_ref):
        x = h_ref[...]
        r = lax.rsqrt(jnp.mean(x * x, axis=-1, keepdims=True) + RMS_EPS)
        u = (x * r * g_ref[...]).astype(BF16)
        u_ref[...] = u
        pc_ref[...] = _dot(u, w_ref[:, 0:1536])
        qkv_ref[...] = _dot(u, w_ref[:, 1536:3072]).astype(BF16)
        sbg_ref[...] = _dot(u, w_ref[:, 3072:3584])

    return pl.pallas_call(
        body, name="inproj_fwd", grid=(t // blk,),
        in_specs=[_rows(D_MODEL, rows=blk), _whole((1, D_MODEL)), _whole((D_MODEL, D_IN))],
        out_specs=[_rows(1536, rows=blk), _rows(1536, rows=blk), _rows(D_SB, rows=blk), _rows(D_MODEL, rows=blk)],
        out_shape=[jax.ShapeDtypeStruct((t, 1536), F32), jax.ShapeDtypeStruct((t, 1536), BF16),
                   jax.ShapeDtypeStruct((t, D_SB), F32), jax.ShapeDtypeStruct((t, D_MODEL), BF16)],
        compiler_params=_cparams("parallel"),
    )(h, g_pre, w_in)


def _prev_halo(col):
    per = ROW_BLK // HALO
    return pl.BlockSpec((HALO, D_CONV), lambda i, col=col: (jnp.maximum(i * per - 1, 0), col))


def _fill_glu(buf, i, a_ref, b_ref, ha_ref, hb_ref):
    halo = ha_ref[...] * _sigmoid(hb_ref[...])
    buf[0:HALO, :] = jnp.where(i > 0, halo, 0.0)
    buf[HALO:HALO + ROW_BLK, :] = a_ref[...] * _sigmoid(b_ref[...])


SUBLANES = 8
TAP_ROWS = 64
SHIFT_ROWS = HALO + ROW_BLK - SUBLANES


def _fill_shifts(shifts, buf):
    for b in range(1, SUBLANES):
        shifts[b - 1] = buf[pl.ds(b, SHIFT_ROWS), :]


def _window(buf, shifts, first, rows, lanes):
    whole, part = divmod(first, SUBLANES)
    src = buf if part == 0 else shifts.at[part - 1]
    return src[pl.ds(rows.start + SUBLANES * whole, rows.size), lanes]


TAP_ROW_CHUNKS = [pl.ds(r, TAP_ROWS) for r in range(0, ROW_BLK, TAP_ROWS)]
TAP_LANE_TILES = [pl.ds(c, LANES) for c in range(0, D_CONV, LANES)]


def _layer_norm_stats(cv):
    mu = jnp.mean(cv, axis=-1, keepdims=True)
    xc = cv - mu
    rstd = lax.rsqrt(jnp.mean(xc * xc, axis=-1, keepdims=True) + LN_EPS)
    return xc * rstd, rstd


def _conv_fwd(pc, conv_w, conv_b, ln_g, ln_b, w_pw2, b_pw2):
    t = pc.shape[0]

    def body(a_ref, b_ref, gate_ref, ha_ref, hb_ref, cw_ref, cb_ref, lg_ref, lb_ref, wp_ref, bp_ref,
             cout_ref, cv_ref, p_ref, sl_ref, buf, shifts):
        i = pl.program_id(0)
        _fill_glu(buf, i, a_ref, b_ref, ha_ref, hb_ref)
        _fill_shifts(shifts, buf)
        for lanes in TAP_LANE_TILES:
            for rows in TAP_ROW_CHUNKS:
                acc = jnp.zeros((TAP_ROWS, LANES), F32) + cb_ref[:, lanes]
                for j in range(CONV_WIDTH):
                    acc = acc + cw_ref[j:j + 1, lanes] * _window(buf, shifts, HALO - (CONV_WIDTH - 1) + j, rows, lanes)
                cv_ref[rows, lanes] = acc
        xh, _ = _layer_norm_stats(cv_ref[...])
        ln = xh * lg_ref[...] + lb_ref[...]
        sl = (ln * _sigmoid(ln)).astype(BF16)
        sl_ref[...] = sl
        p = _dot(sl, wp_ref[...]) + bp_ref[...]
        p_ref[...] = p
        gate = gate_ref[...]
        cout_ref[...] = (p * (gate * _sigmoid(gate))).astype(BF16)

    vec = _whole((1, D_CONV))
    return pl.pallas_call(
        body, name="conv_fwd", grid=(t // ROW_BLK,),
        in_specs=[_rows(D_CONV, 0), _rows(D_CONV, 1), _rows(D_CONV, 2), _prev_halo(0), _prev_halo(1),
                  _whole((CONV_WIDTH, D_CONV)), vec, vec, vec, _whole((D_CONV, D_CONV)), vec],
        out_specs=[_rows(D_CONV)] * 4,
        out_shape=[jax.ShapeDtypeStruct((t, D_CONV), BF16), jax.ShapeDtypeStruct((t, D_CONV), F32),
                   jax.ShapeDtypeStruct((t, D_CONV), F32), jax.ShapeDtypeStruct((t, D_CONV), BF16)],
        scratch_shapes=[pltpu.VMEM((HALO + ROW_BLK, D_CONV), F32), pltpu.VMEM((SUBLANES - 1, SHIFT_ROWS, D_CONV), F32)],
        compiler_params=_cparams("parallel"),
    )(pc, pc, pc, pc, pc, conv_w, conv_b, ln_g, ln_b, w_pw2, b_pw2)


def _lower_triangle():
    row = lax.broadcasted_iota(jnp.int32, (ROW_BLK, ROW_BLK), 0)
    col = lax.broadcasted_iota(jnp.int32, (ROW_BLK, ROW_BLK), 1)
    return row > col


def _lower_triangle_t():
    row = lax.broadcasted_iota(jnp.int32, (ROW_BLK, ROW_BLK), 0)
    col = lax.broadcasted_iota(jnp.int32, (ROW_BLK, ROW_BLK), 1)
    return row < col


def _tri_sum(x, umat):
    return _dot(x.astype(BF16), umat)


def _log_gates(z):
    ls = -(jnp.maximum(z, 0.0) + jnp.log(1.0 + jnp.exp(-jnp.abs(z))))
    return ls, z + ls


def _head_lanes(hh):
    lane = lax.broadcasted_iota(jnp.int32, (ROW_BLK, HEAD_BLK), 1)
    return (lane >= HEAD_DIM * hh) & (lane < HEAD_DIM * (hh + 1))


def _merge_heads(acc_ref):
    out = acc_ref[HEADS_PER_BLOCK - 1]
    for hh in range(HEADS_PER_BLOCK - 1):
        out = jnp.where(_head_lanes(hh), acc_ref[hh], out)
    return out


def _qkv_specs(t):
    n_blk = D_SB // HEAD_BLK
    return [pl.BlockSpec((ROW_BLK, HEAD_BLK), lambda hp, i: (i, hp)),
            pl.BlockSpec((t, HEAD_BLK), lambda hp, i: (0, n_blk + hp)),
            pl.BlockSpec((t, HEAD_BLK), lambda hp, i: (0, 2 * n_blk + hp))]


def _carry_spec():
    return pl.BlockSpec((HEADS_PER_BLOCK, ROW_BLK, LANES), lambda hp, i: (hp, i, 0))


def _last_block_rows(t, n_tokens):
    packed_rows = 16
    return -(-(n_tokens - (t - ROW_BLK)) // packed_rows) * packed_rows


def _by_block_rows(i, last_rows, sweep):
    if last_rows == ROW_BLK:
        sweep(ROW_BLK)
        return
    last = pl.num_programs(1) - 1
    pl.when(i < last)(lambda: sweep(ROW_BLK))
    pl.when(i == last)(lambda: sweep(last_rows))


def _attn_fwd(qkv, n_tokens):
    t = qkv.shape[0]
    assert t // ROW_BLK <= LANES

    def body(q_ref, k_ref, v_ref, o_ref, c_ref, acc_ref, run_ref, qm_ref, z_ref):
        i = pl.program_id(1)
        q = q_ref[...]
        heads = range(HEADS_PER_BLOCK)
        for hh in heads:
            qm_ref[hh] = jnp.where(_head_lanes(hh), q, jnp.zeros_like(q)) * jnp.asarray(SB_SCALE, BF16)
        acc_ref[...] = jnp.zeros_like(acc_ref)
        c_ref[...] = jnp.zeros_like(c_ref)
        run_ref[...] = jnp.zeros_like(run_ref)

        def sweep(n_rows):
            rows = pl.ds(0, n_rows)
            lower = _lower_triangle()[:n_rows]
            umat = jnp.where(_lower_triangle(), 1.0, 0.0).astype(BF16)
            lane = lax.broadcasted_iota(jnp.int32, (n_rows, LANES), 1)

            def scores(jb):
                start = pl.multiple_of(jb * ROW_BLK, ROW_BLK)
                kb = k_ref[pl.ds(start, ROW_BLK), :]
                for hh in heads:
                    z_ref[hh, rows] = _dot_nt(qm_ref[hh, rows], kb)

            def block(jb, diagonal):
                start = pl.multiple_of(jb * ROW_BLK, ROW_BLK)
                vb = v_ref[pl.ds(start, ROW_BLK), :]
                logits = []
                for hh in heads:
                    ls, lb = _log_gates(z_ref[hh, rows])
                    if diagonal:
                        ls = jnp.where(lower, ls, 0.0)
                    run = run_ref[hh, rows]
                    if not diagonal:
                        c_ref[hh, rows] = jnp.where(lane == jb, run, c_ref[hh, rows])
                    logits.append(lb + jnp.concatenate([run, run], axis=1) + _tri_sum(ls, umat))
                    run_ref[hh, rows] = run + jnp.sum(ls, axis=1, keepdims=True)
                scores(jnp.maximum(jb - 1, 0))
                for hh in heads:
                    a = jnp.exp(logits[hh])
                    if diagonal:
                        a = jnp.where(lower, a, 0.0)
                    acc_ref[hh, rows] += _dot(a.astype(BF16), vb)

            scores(i)
            block(i, True)

            @pl.loop(0, i)
            def _(n):
                block(i - 1 - n, False)

        _by_block_rows(i, _last_block_rows(t, n_tokens), sweep)
        o_ref[...] = _merge_heads(acc_ref)

    per_head = (HEADS_PER_BLOCK, ROW_BLK, HEAD_BLK)
    return pl.pallas_call(
        body, name="attn_fwd", grid=(D_SB // HEAD_BLK, t // ROW_BLK),
        in_specs=_qkv_specs(t),
        out_specs=[pl.BlockSpec((ROW_BLK, HEAD_BLK), lambda hp, i: (i, hp)), _carry_spec()],
        out_shape=[jax.ShapeDtypeStruct((t, D_SB), F32),
                   jax.ShapeDtypeStruct((D_SB // HEAD_DIM, t, LANES), F32)],
        scratch_shapes=[pltpu.VMEM(per_head, F32), pltpu.VMEM((HEADS_PER_BLOCK, ROW_BLK, LANES), F32),
                        pltpu.VMEM(per_head, BF16), pltpu.VMEM((HEADS_PER_BLOCK, ROW_BLK, ROW_BLK), F32)],
        compiler_params=_cparams("arbitrary", "arbitrary"),
    )(qkv, qkv, qkv)


def _outproj_fwd(h, cout, sraw, sbg, w_out, g_post):
    t = h.shape[0]
    blk = _dense_rows(t)

    def body(h_ref, c_ref, s_ref, g_ref, w_ref, gp_ref, hn_ref, mixed_ref, mix_ref):
        gate = g_ref[...]
        mix_ref[:, 0:D_CONV] = c_ref[...]
        mix_ref[:, D_CONV:] = (s_ref[...] * (gate * _sigmoid(gate))).astype(BF16)
        mixed = _dot(mix_ref[...], w_ref[...])
        mixed_ref[...] = mixed
        r = lax.rsqrt(jnp.mean(mixed * mixed, axis=-1, keepdims=True) + RMS_EPS)
        hn_ref[...] = h_ref[...] + mixed * r * gp_ref[...]

    return pl.pallas_call(
        body, name="outproj_fwd", grid=(t // blk,),
        in_specs=[_rows(D_MODEL, rows=blk), _rows(D_CONV, rows=blk), _rows(D_SB, rows=blk), _rows(D_SB, rows=blk),
                  _whole((D_MODEL, D_MODEL)), _whole((1, D_MODEL))],
        out_specs=[_rows(D_MODEL, rows=blk)] * 3,
        out_shape=[jax.ShapeDtypeStruct((t, D_MODEL), F32), jax.ShapeDtypeStruct((t, D_MODEL), F32),
                   jax.ShapeDtypeStruct((t, D_MODEL), BF16)],
        compiler_params=_cparams("parallel"),
    )(h, cout, sraw, sbg, w_out, g_post)


def _loss_and_grad(h, target, seq):
    t = h.shape[0]
    blk = _dense_rows(t)

    def body(h_ref, t_ref, loss_ref, dh_ref):
        i = pl.program_id(0)

        @pl.when(i == 0)
        def _():
            loss_ref[...] = jnp.zeros_like(loss_ref)

        row = i * blk + lax.broadcasted_iota(jnp.int32, (blk, D_MODEL), 0)
        real = (row >= N_META) & (row < N_META + seq)
        diff = jnp.where(real, h_ref[...] - t_ref[...], 0.0)
        sq = jnp.sum(jnp.sum(diff * diff, axis=1, keepdims=True), axis=0, keepdims=True)
        loss_ref[...] += (0.5 / D_MODEL) * sq
        dh_ref[...] = diff * (1.0 / D_MODEL)

    return pl.pallas_call(
        body, name="loss", grid=(t // blk,),
        in_specs=[_rows(D_MODEL, rows=blk), _rows(D_MODEL, rows=blk)],
        out_specs=[_whole((1, 1)), _rows(D_MODEL, rows=blk)],
        out_shape=[jax.ShapeDtypeStruct((1, 1), F32), jax.ShapeDtypeStruct((t, D_MODEL), F32)],
        compiler_params=_cparams("arbitrary"),
    )(h, target)


def _outproj_bwd(dh, mixed, g_post, sraw, sbg, w_out_t):
    t = dh.shape[0]
    blk = _dense_rows(t)

    def body(dh_ref, mixed_ref, gp_ref, s_ref, g_ref, wt_ref, dc_ref, ds_ref, dg_ref, dmb_ref, dgp_ref):
        @pl.when(pl.program_id(0) == 0)
        def _():
            dgp_ref[...] = jnp.zeros_like(dgp_ref)

        mixed = mixed_ref[...]
        r = lax.rsqrt(jnp.mean(mixed * mixed, axis=-1, keepdims=True) + RMS_EPS)
        nh = mixed * r
        dy = dh_ref[...]
        dgp_ref[...] += jnp.sum(dy * nh, axis=0, keepdims=True)
        dn = dy * gp_ref[...]
        dmixed = (r * (dn - nh * jnp.mean(dn * nh, axis=-1, keepdims=True))).astype(BF16)
        dmb_ref[...] = dmixed
        dmix = _dot_nt(dmixed, wt_ref[...])
        dc_ref[...] = dmix[:, 0:D_CONV]
        dsg = dmix[:, D_CONV:]
        gate = g_ref[...]
        sg = _sigmoid(gate)
        ds_ref[...] = dsg * (gate * sg)
        dg_ref[...] = (dsg * s_ref[...] * _dsilu(gate, sg)).astype(BF16)

    return pl.pallas_call(
        body, name="outproj_bwd", grid=(t // blk,),
        in_specs=[_rows(D_MODEL, rows=blk), _rows(D_MODEL, rows=blk), _whole((1, D_MODEL)), _rows(D_SB, rows=blk),
                  _rows(D_SB, rows=blk), _whole((D_MODEL, D_MODEL))],
        out_specs=[_rows(D_CONV, rows=blk), _rows(D_SB, rows=blk), _rows(D_SB, rows=blk), _rows(D_MODEL, rows=blk),
                   _whole((1, D_MODEL))],
        out_shape=[jax.ShapeDtypeStruct((t, D_CONV), F32), jax.ShapeDtypeStruct((t, D_SB), F32),
                   jax.ShapeDtypeStruct((t, D_SB), BF16), jax.ShapeDtypeStruct((t, D_MODEL), BF16),
                   jax.ShapeDtypeStruct((1, D_MODEL), F32)],
        compiler_params=_cparams("arbitrary"),
    )(dh, mixed, g_post, sraw, sbg, w_out_t)


def _attn_bwd(qkv, carries, do, n_tokens):
    t = qkv.shape[0]

    def body(q_ref, k_ref, v_ref, c_ref, do_ref, dq_ref, dk_ref, dv_ref, acc_ref, seen_ref, qm_ref, dom_ref, z_ref,
             da_ref, dz_ref, a_ref):
        i = pl.program_id(1)

        @pl.when(i == 0)
        def _():
            dk_ref[...] = jnp.zeros_like(dk_ref)
            dv_ref[...] = jnp.zeros_like(dv_ref)

        q = q_ref[...]
        dof = do_ref[...]
        heads = range(HEADS_PER_BLOCK)
        for hh in heads:
            qm_ref[hh] = jnp.where(_head_lanes(hh), q, jnp.zeros_like(q)) * jnp.asarray(SB_SCALE, BF16)
            dom_ref[hh] = jnp.where(_head_lanes(hh), dof, 0.0).astype(BF16)
        acc_ref[...] = jnp.zeros_like(acc_ref)
        seen_ref[...] = jnp.zeros_like(seen_ref)

        def sweep(n_rows):
            rows = pl.ds(0, n_rows)
            lower = _lower_triangle()[:n_rows]
            umat = jnp.where(_lower_triangle(), 1.0, 0.0).astype(BF16)
            umat_t = jnp.where(_lower_triangle_t(), 1.0, 0.0).astype(BF16)
            lane = lax.broadcasted_iota(jnp.int32, (n_rows, LANES), 1)

            def scores(jb):
                start = pl.multiple_of(jb * ROW_BLK, ROW_BLK)
                kb = k_ref[pl.ds(start, ROW_BLK), :]
                for hh in heads:
                    z_ref[hh, rows] = _dot_nt(qm_ref[hh, rows], kb)

            def value_grads(jb):
                start = pl.multiple_of(jb * ROW_BLK, ROW_BLK)
                vb = v_ref[pl.ds(start, ROW_BLK), :]
                for hh in heads:
                    da_ref[hh, rows] = _dot_nt(dom_ref[hh, rows], vb)

            def products(jb, hh):
                start = pl.multiple_of(jb * ROW_BLK, ROW_BLK)
                dzb = dz_ref[hh, rows]
                acc_ref[hh, rows] += _dot(dzb, k_ref[pl.ds(start, ROW_BLK), :])
                dk_ref[pl.ds(start, ROW_BLK), :] += _dot_tn(dzb, qm_ref[hh, rows])
                dv_ref[pl.ds(start, ROW_BLK), :] += _dot_tn(a_ref[hh, rows], dom_ref[hh, rows])

            def block(jb, diagonal):
                before = jnp.maximum(jb - 1, 0)
                lbs, logits = [], []
                for hh in heads:
                    products(before, hh)
                    ls, lb = _log_gates(z_ref[hh, rows])
                    if diagonal:
                        ls = jnp.where(lower, ls, 0.0)
                        logits.append(lb + _tri_sum(ls, umat))
                    else:
                        right = jnp.sum(jnp.where(lane == jb, c_ref[hh, rows], 0.0), axis=1, keepdims=True)
                        logits.append(lb + right + _tri_sum(ls, umat))
                    lbs.append(lb)
                if not diagonal:
                    scores(jb + 1)
                gs, befores = [], []
                for hh in heads:
                    a = jnp.exp(logits[hh])
                    if diagonal:
                        a = jnp.where(lower, a, 0.0)
                    g = da_ref[hh, rows] * a
                    seen = seen_ref[hh, rows]
                    befores.append(jnp.concatenate([seen, seen], axis=1) + _tri_sum(g, umat_t))
                    seen_ref[hh, rows] = seen + jnp.sum(g, axis=1, keepdims=True)
                    a_ref[hh, rows] = a.astype(BF16)
                    gs.append(g)
                if not diagonal:
                    value_grads(jb + 1)
                for hh in heads:
                    dz = gs[hh] - jnp.exp(lbs[hh]) * (gs[hh] + befores[hh])
                    if diagonal:
                        dz = jnp.where(lower, dz, 0.0)
                    dz_ref[hh, rows] = dz.astype(BF16)

            dz_ref[...] = jnp.zeros_like(dz_ref)
            a_ref[...] = jnp.zeros_like(a_ref)
            scores(0)
            value_grads(0)

            @pl.loop(0, i)
            def _(jb):
                block(jb, False)

            block(i, True)
            for hh in heads:
                products(i, hh)

        _by_block_rows(i, _last_block_rows(t, n_tokens), sweep)
        dq_ref[...] = (_merge_heads(acc_ref) * SB_SCALE).astype(BF16)

    blk = pl.BlockSpec((ROW_BLK, HEAD_BLK), lambda hp, i: (i, hp))
    full = pl.BlockSpec((t, HEAD_BLK), lambda hp, i: (0, hp))
    per_head = (HEADS_PER_BLOCK, ROW_BLK, HEAD_BLK)
    return pl.pallas_call(
        body, name="attn_bwd", grid=(D_SB // HEAD_BLK, t // ROW_BLK),
        in_specs=_qkv_specs(t) + [_carry_spec(), blk],
        out_specs=[blk, full, full],
        out_shape=[jax.ShapeDtypeStruct((t, D_SB), BF16)] + [jax.ShapeDtypeStruct((t, D_SB), F32)] * 2,
        scratch_shapes=[pltpu.VMEM(per_head, F32), pltpu.VMEM((HEADS_PER_BLOCK, ROW_BLK, LANES), F32),
                        pltpu.VMEM(per_head, BF16), pltpu.VMEM(per_head, BF16)]
                       + [pltpu.VMEM((HEADS_PER_BLOCK, ROW_BLK, ROW_BLK), dtype) for dtype in (F32, F32, BF16, BF16)],
        compiler_params=_cparams("arbitrary", "arbitrary"),
    )(qkv, qkv, qkv, carries, do)


def _conv_bwd_rows(dcout, pc, cv, p, ln_g, ln_b, w_pw2_t):
    t = dcout.shape[0]
    blk = _dense_rows(t)

    def body(dc_ref, gate_ref, cv_ref, p_ref, lg_ref, lb_ref, wt_ref, dcv_ref, dgate_ref, dpb_ref, vec_ref):
        @pl.when(pl.program_id(0) == 0)
        def _():
            vec_ref[...] = jnp.zeros_like(vec_ref)

        dc = dc_ref[...]
        gate = gate_ref[...]
        sg = _sigmoid(gate)
        dp = dc * (gate * sg)
        dgate_ref[...] = (dc * p_ref[...] * _dsilu(gate, sg)).astype(BF16)
        dpb = dp.astype(BF16)
        dpb_ref[...] = dpb
        xh, rstd = _layer_norm_stats(cv_ref[...])
        ln = xh * lg_ref[...] + lb_ref[...]
        s2 = _sigmoid(ln)
        dln = _dot_nt(dpb, wt_ref[...]) * _dsilu(ln, s2)
        dxh = dln * lg_ref[...]
        dcv = rstd * (dxh - jnp.mean(dxh, axis=-1, keepdims=True)
                      - xh * jnp.mean(dxh * xh, axis=-1, keepdims=True))
        dcv_ref[...] = dcv
        vec_ref[0:1, :] += jnp.sum(dp, axis=0, keepdims=True)
        vec_ref[1:2, :] += jnp.sum(dln * xh, axis=0, keepdims=True)
        vec_ref[2:3, :] += jnp.sum(dln, axis=0, keepdims=True)
        vec_ref[3:4, :] += jnp.sum(dcv, axis=0, keepdims=True)

    vec = _whole((1, D_CONV))
    return pl.pallas_call(
        body, name="conv_bwd_rows", grid=(t // blk,),
        in_specs=[_rows(D_CONV, rows=blk), _rows(D_CONV, 2, rows=blk), _rows(D_CONV, rows=blk), _rows(D_CONV, rows=blk),
                  vec, vec, _whole((D_CONV, D_CONV))],
        out_specs=[_rows(D_CONV, rows=blk), _rows(D_CONV, rows=blk), _rows(D_CONV, rows=blk), _whole((8, D_CONV))],
        out_shape=[jax.ShapeDtypeStruct((t, D_CONV), F32), jax.ShapeDtypeStruct((t, D_CONV), BF16),
                   jax.ShapeDtypeStruct((t, D_CONV), BF16), jax.ShapeDtypeStruct((8, D_CONV), F32)],
        compiler_params=_cparams("arbitrary"),
    )(dcout, pc, cv, p, ln_g, ln_b, w_pw2_t)


def _conv_bwd_taps(dcv, pc, conv_w):
    t = dcv.shape[0]
    n_halo = t // HALO
    per = ROW_BLK // HALO

    def body(d_ref, dn_ref, a_ref, b_ref, ha_ref, hb_ref, cw_ref, da_ref, db_ref, dw_ref, cbuf, dbuf, cshifts, dshifts):
        i = pl.program_id(0)

        @pl.when(i == 0)
        def _():
            dw_ref[...] = jnp.zeros_like(dw_ref)

        _fill_glu(cbuf, i, a_ref, b_ref, ha_ref, hb_ref)
        dbuf[0:ROW_BLK, :] = d_ref[...]
        dbuf[ROW_BLK:ROW_BLK + HALO, :] = jnp.where(i < pl.num_programs(0) - 1, dn_ref[...], 0.0)
        _fill_shifts(cshifts, cbuf)
        _fill_shifts(dshifts, dbuf)
        for lanes in TAP_LANE_TILES:
            for rows in TAP_ROW_CHUNKS:
                acc = jnp.zeros((TAP_ROWS, LANES), F32)
                for j in range(CONV_WIDTH):
                    acc = acc + cw_ref[j:j + 1, lanes] * _window(dbuf, dshifts, CONV_WIDTH - 1 - j, rows, lanes)
                sb = _sigmoid(b_ref[rows, lanes])
                da_ref[rows, lanes] = (acc * sb).astype(BF16)
                db_ref[rows, lanes] = (acc * a_ref[rows, lanes] * sb * (1.0 - sb)).astype(BF16)
            for j in range(CONV_WIDTH):
                acc = jnp.zeros((TAP_ROWS, LANES), F32)
                for rows in TAP_ROW_CHUNKS:
                    acc = acc + d_ref[rows, lanes] * _window(cbuf, cshifts, HALO - (CONV_WIDTH - 1) + j, rows, lanes)
                dw_ref[j:j + 1, lanes] += jnp.sum(acc, axis=0, keepdims=True)

    return pl.pallas_call(
        body, name="conv_bwd_taps", grid=(t // ROW_BLK,),
        in_specs=[_rows(D_CONV),
                  pl.BlockSpec((HALO, D_CONV), lambda i: (jnp.minimum((i + 1) * per, n_halo - 1), 0)),
                  _rows(D_CONV, 0), _rows(D_CONV, 1), _prev_halo(0), _prev_halo(1),
                  _whole((CONV_WIDTH, D_CONV))],
        out_specs=[_rows(D_CONV), _rows(D_CONV), _whole((32, D_CONV))],
        out_shape=[jax.ShapeDtypeStruct((t, D_CONV), BF16), jax.ShapeDtypeStruct((t, D_CONV), BF16),
                   jax.ShapeDtypeStruct((32, D_CONV), F32)],
        scratch_shapes=[pltpu.VMEM((HALO + ROW_BLK, D_CONV), F32), pltpu.VMEM((ROW_BLK + HALO, D_CONV), F32),
                        pltpu.VMEM((SUBLANES - 1, SHIFT_ROWS, D_CONV), F32),
                        pltpu.VMEM((SUBLANES - 1, SHIFT_ROWS, D_CONV), F32)],
        compiler_params=_cparams("arbitrary"),
    )(dcv, dcv, pc, pc, pc, pc, conv_w)


def _inproj_bwd(dh_out, h, g_pre, pieces, w_in_t):
    t = h.shape[0]
    blk = _dense_rows(t)

    def body(dh_ref, h_ref, g_ref, *rest):
        piece_refs, (wt_ref, dhin_ref, dproj_ref, dg_ref) = rest[:7], rest[7:]

        @pl.when(pl.program_id(0) == 0)
        def _():
            dg_ref[...] = jnp.zeros_like(dg_ref)

        for k, ref in enumerate(piece_refs):
            dproj_ref[:, 512 * k:512 * (k + 1)] = ref[...].astype(BF16)
        du = _dot_nt(dproj_ref[...], wt_ref[...])
        x = h_ref[...]
        r = lax.rsqrt(jnp.mean(x * x, axis=-1, keepdims=True) + RMS_EPS)
        xh = x * r
        dg_ref[...] += jnp.sum(du * xh, axis=0, keepdims=True)
        dxh = du * g_ref[...]
        dhin_ref[...] = dh_ref[...] + r * (dxh - xh * jnp.mean(dxh * xh, axis=-1, keepdims=True))

    return pl.pallas_call(
        body, name="inproj_bwd", grid=(t // blk,),
        in_specs=[_rows(D_MODEL, rows=blk), _rows(D_MODEL, rows=blk), _whole((1, D_MODEL))] + [_rows(512, rows=blk)] * 7
                 + [_whole((D_MODEL, D_IN))],
        out_specs=[_rows(D_MODEL, rows=blk), _rows(D_IN, rows=blk), _whole((1, D_MODEL))],
        out_shape=[jax.ShapeDtypeStruct((t, D_MODEL), F32), jax.ShapeDtypeStruct((t, D_IN), BF16),
                   jax.ShapeDtypeStruct((1, D_MODEL), F32)],
        compiler_params=_cparams("arbitrary"),
    )(dh_out, h, g_pre, *pieces, w_in_t)


def _weight_grad(xb, dyb, name):
    t, k = xb.shape
    n = dyb.shape[1]

    def body(x_ref, dy_ref, o_ref, acc_ref):
        i = pl.program_id(0)

        @pl.when(i == 0)
        def _():
            acc_ref[...] = jnp.zeros_like(acc_ref)

        acc_ref[...] += _dot_tn(x_ref[...], dy_ref[...])

        @pl.when(i == pl.num_programs(0) - 1)
        def _():
            o_ref[...] = acc_ref[...].astype(BF16)

    return pl.pallas_call(
        body, name=name, grid=(t // ROW_BLK,),
        in_specs=[_rows(k), _rows(n)], out_specs=_whole((k, n)), out_shape=jax.ShapeDtypeStruct((k, n), BF16),
        scratch_shapes=[pltpu.VMEM((k, n), F32)],
        compiler_params=_cparams("arbitrary"),
    )(xb, dyb)


def _position():
    return lax.axis_index("x"), lax.axis_index("y"), lax.axis_index("c")


def _comm_call(body, name, ins, out_shapes):
    n = len(ins)
    hbm = pl.BlockSpec(memory_space=pltpu.HBM)
    return pl.pallas_call(
        functools.partial(body, n), name=name, in_specs=[hbm] * n, out_specs=[hbm] * n, out_shape=out_shapes,
        scratch_shapes=[pltpu.SemaphoreType.DMA((n, N_DEV - 1)), pltpu.SemaphoreType.DMA((n, N_DEV - 1)),
                        pltpu.SemaphoreType.DMA((n,))],
    )(*ins)


def _all_gather(blocks, name):
    def body(n, *refs):
        x_refs, out_refs, (send_sems, recv_sems, local_sems) = refs[:n], refs[n:2 * n], refs[2 * n:]
        x, y, c = _position()
        me, sibling = (x, y, c), (x, y, 1 - c)
        chips = [(1 - x, y), (x, 1 - y), (1 - x, 1 - y)]

        def slot(a, px, py, pc):
            return out_refs[a].at[4 * px + 2 * py + pc]

        def copy(a, k, origin, to, own=False):
            return pltpu.make_async_remote_copy(
                src_ref=x_refs[a] if own else slot(a, *origin), dst_ref=slot(a, *origin),
                send_sem=send_sems.at[a, k], recv_sem=recv_sems.at[a, k], device_id=to, device_id_type=MESH)

        arrays = range(n)
        mine = [pltpu.make_async_copy(x_refs[a], slot(a, *me), local_sems.at[a]) for a in arrays]
        first = [copy(a, 1 + j, me, (*chip, c), own=True) for j, chip in enumerate(chips) for a in arrays]
        first += [copy(a, 0, me, sibling, own=True) for a in arrays]
        for cp in mine + first:
            cp.start()
        passed = []
        for j, chip in enumerate(chips):
            for a in arrays:
                copy(a, 1 + j, (*chip, c), me).wait_recv()
                passed.append(copy(a, 4 + j, (*chip, c), sibling))
                passed[-1].start()
        for a in arrays:
            copy(a, 0, sibling, me).wait_recv()
            for j, chip in enumerate(chips):
                copy(a, 4 + j, (*chip, 1 - c), me).wait_recv()
        for cp in first + passed:
            cp.wait_send()
        for cp in mine:
            cp.wait()

    return _comm_call(body, name, blocks, [jax.ShapeDtypeStruct((N_DEV,) + b.shape, b.dtype) for b in blocks])


def _exchange_copies(g_refs, land_refs, sems, gather):
    x, y, c = _position()
    me = 4 * x + 2 * y + c
    out = []
    for g_ref, land_ref, (send_sem, recv_sem, local_sem) in zip(g_refs, land_refs, sems):
        def mine(slot, g_ref=g_ref):
            return g_ref if gather else g_ref.at[slot]

        def remote(src, dst, dev):
            return pltpu.make_async_remote_copy(src_ref=src, dst_ref=dst, send_sem=send_sem, recv_sem=recv_sem,
                                                device_id=dev, device_id_type=MESH)

        sends = []
        for k in range(1, N_DEV):
            px = 1 - x if k & 4 else x
            py = 1 - y if k & 2 else y
            pc = 1 - c if k & 1 else c
            sends.append(remote(mine(4 * px + 2 * py + pc), land_ref.at[me], (px, py, pc)))
        seven = land_ref.at[pl.ds(0, N_DEV - 1)]
        out.append((pltpu.make_async_copy(mine(me), land_ref.at[me], local_sem), sends, remote(seven, seven, (x, y, c))))
    return out


_HBM = pl.BlockSpec(memory_space=pltpu.HBM)
_SEM = pl.BlockSpec(memory_space=pltpu.SEMAPHORE)
_ORDERED = pltpu.CompilerParams(has_side_effects=pltpu.SideEffectType.DATAFLOW_SIDE_EFFECTING)
SEMS_PER_ARRAY = 3


def _exchange_start(arrays, after, name, gather):
    n = len(arrays)
    n_sems = SEMS_PER_ARRAY * n

    def body(*refs):
        g_refs, land_refs, sems, token = refs[:n], refs[n:2 * n], refs[2 * n + 1:2 * n + 1 + n_sems], refs[-1]
        sems = [sems[SEMS_PER_ARRAY * a:SEMS_PER_ARRAY * (a + 1)] for a in range(n)]
        for local, sends, _ in _exchange_copies(g_refs, land_refs, sems, gather):
            local.start()
            for cp in sends:
                cp.start()
        token[...] = jnp.zeros_like(token)

    buffers = list(arrays) + [lax.empty((N_DEV,) + g.shape if gather else g.shape, g.dtype) for g in arrays]
    outs = pl.pallas_call(
        body, name=name, in_specs=[_HBM] * (2 * n) + [pl.BlockSpec(memory_space=pl.ANY)],
        out_specs=[_SEM] * n_sems + [_HBM] * (2 * n) + [pl.BlockSpec(memory_space=pltpu.VMEM)],
        out_shape=[pltpu.SemaphoreType.DMA(())] * n_sems + [pltpu.HBM(b.shape, b.dtype) for b in buffers]
                  + [jax.ShapeDtypeStruct((8, LANES), F32)],
        input_output_aliases={a: n_sems + a for a in range(2 * n)}, compiler_params=_ORDERED,
    )(*[pltpu.with_memory_space_constraint(b, pltpu.HBM) for b in buffers], after)
    return outs[:n_sems], outs[n_sems:n_sems + n], outs[n_sems + n:n_sems + 2 * n], outs[-1]


def _exchange_wait(sems, arrays, landings, after, name, gather):
    n = len(arrays)
    n_sems = SEMS_PER_ARRAY * n

    def body(*refs):
        g_refs, land_refs, sems = refs[:n], refs[n:2 * n], refs[2 * n:2 * n + n_sems]
        sems = [sems[SEMS_PER_ARRAY * a:SEMS_PER_ARRAY * (a + 1)] for a in range(n)]
        for local, _, all_seven in _exchange_copies(g_refs, land_refs, sems, gather):
            all_seven.wait_recv()
            all_seven.wait_send()
            local.wait()

    buffers = list(arrays) + list(landings)
    outs = pl.pallas_call(
        body, name=name, in_specs=[_HBM] * (2 * n) + [_SEM] * n_sems + [pl.BlockSpec(memory_space=pl.ANY)],
        out_specs=[_HBM] * (2 * n), out_shape=[pltpu.HBM(b.shape, b.dtype) for b in buffers],
        input_output_aliases={a: a for a in range(2 * n)}, compiler_params=_ORDERED,
    )(*buffers, *sems, after)
    return outs[n:]


def _block_rows(r, row_bytes, budget=1 << 20):
    cap = max(8, budget // row_bytes)
    return max(d for d in range(8, min(r, cap) + 1, 8) if r % d == 0)


def _sum_adamw(parts, w, m, v, name):
    n_parts, r, c = parts.shape
    br = _block_rows(r, 4 * c)

    def body(p_ref, w_ref, m_ref, v_ref, g_out, d_out, m_out, v_out):
        g = p_ref[0].astype(F32)
        for s in range(1, n_parts):
            g = g + p_ref[s].astype(F32)
        m_new = ADAM_B1 * m_ref[...] + (1.0 - ADAM_B1) * g
        v_new = ADAM_B2 * v_ref[...] + (1.0 - ADAM_B2) * (g * g)
        m_hat = m_new / (1.0 - ADAM_B1 ** ADAM_STEP)
        v_hat = v_new / (1.0 - ADAM_B2 ** ADAM_STEP)
        g_out[...] = g
        d_out[...] = -ADAM_LR * (m_hat / (jnp.sqrt(v_hat) + ADAM_EPS) + ADAM_WD * w_ref[...])
        m_out[...] = m_new
        v_out[...] = v_new

    row = pl.BlockSpec((br, c), lambda i: (i, 0))
    return pl.pallas_call(
        body, name=name, grid=(r // br,),
        in_specs=[pl.BlockSpec((n_parts, br, c), lambda i: (0, i, 0)), row, row, row],
        out_specs=[row] * 4, out_shape=[jax.ShapeDtypeStruct((r, c), F32)] * 4,
        compiler_params=_cparams("parallel"),
    )(parts, w, m, v)


def _sum_parts(parts, name):
    n_parts, r, c = parts.shape

    def body(p_ref, o_ref):
        g = p_ref[0]
        for s in range(1, n_parts):
            g = g + p_ref[s]
        o_ref[...] = g

    return pl.pallas_call(
        body, name=name, in_specs=[pl.BlockSpec(memory_space=pltpu.VMEM)],
        out_specs=pl.BlockSpec(memory_space=pltpu.VMEM), out_shape=jax.ShapeDtypeStruct((r, c), F32),
    )(parts)


def _pack(arrays):
    flat = jnp.concatenate([a.reshape(-1) for a in arrays])
    pad = -flat.shape[0] % (8 * LANES)
    if pad:
        flat = jnp.pad(flat, (0, pad))
    return flat.reshape(-1, LANES)


def _unpack(buf, shapes):
    flat = buf.reshape(-1)
    out, at = [], 0
    for shape in shapes:
        size = 1
        for d in shape:
            size *= d
        out.append(lax.slice_in_dim(flat, at, at + size).reshape(shape))
        at += size
    return out


def _local_step(x, target, meta, pre_g, post_g, conv_w, conv_b, ln_g, ln_b, b_pw2, weights, ship, ship_small):
    depth = pre_g.shape[0]
    seq = x.shape[0]
    t = -(-(N_META + seq) // ROW_BLK) * ROW_BLK
    tail = t - N_META - seq
    h = jnp.concatenate([meta, x, jnp.zeros((tail, D_MODEL), F32)], axis=0)
    target = jnp.pad(target, ((N_META, tail), (0, 0)))
    row = lambda a, l: a[l][None, :]

    saved = []
    for l in range(depth):
        w_in, w_pw2, w_out = weights(l, h)
        pc, qkv, sbg, u = _inproj_fwd(h, row(pre_g, l), w_in)
        cout, cv, p, sl = _conv_fwd(pc, conv_w[l], row(conv_b, l), row(ln_g, l), row(ln_b, l), w_pw2, row(b_pw2, l))
        sraw, carries = _attn_fwd(qkv, N_META + seq)
        h_new, mixed, mix = _outproj_fwd(h, cout, sraw, sbg, w_out, row(post_g, l))
        saved.append((h, pc, qkv, sbg, u, cv, p, sl, sraw, carries, mixed, mix, w_in, w_pw2, w_out))
        h = h_new

    loss, dh = _loss_and_grad(h, target, seq)

    grads = {k: [None] * depth for k in ("pre_g", "post_g", "conv_w", "conv_b", "ln_g", "ln_b", "b_pw2")}
    token = jnp.zeros((8, LANES), F32)
    for l in reversed(range(depth)):
        h_in, pc, qkv, sbg, u, cv, p, sl, sraw, carries, mixed, mix, w_in_t, w_pw2_t, w_out_t = saved[l]
        dcout, dsraw, dsbg, dmixed, dg_post = _outproj_bwd(dh, mixed, row(post_g, l) + token[:1, :1], sraw, sbg, w_out_t)
        dq, dk, dv = _attn_bwd(qkv, carries, dsraw, N_META + seq)
        dcv, dgate, dpb, vecs = _conv_bwd_rows(dcout, pc, cv, p, row(ln_g, l), row(ln_b, l), w_pw2_t)
        da, db, dconv_w = _conv_bwd_taps(dcv, pc, conv_w[l])
        dh, dproj, dg_pre = _inproj_bwd(dh, h_in, row(pre_g, l), (da, db, dgate, dq, dk, dv, dsbg), w_in_t)
        grads["pre_g"][l] = dg_pre[0]
        grads["post_g"][l] = dg_post[0]
        grads["b_pw2"][l], grads["ln_g"][l], grads["ln_b"][l], grads["conv_b"][l] = vecs[0], vecs[1], vecs[2], vecs[3]
        grads["conv_w"][l] = dconv_w[:CONV_WIDTH]
        after = dh
        if l == 0:
            grads = {k: jnp.stack(v) for k, v in grads.items()}
            grads["meta"] = dh[:N_META]
            after = ship_small(grads, loss[0, 0])
        token = ship(l, after, _weight_grad(u, dproj, "w_in_grad"), _weight_grad(sl, dpb, "w_pw2_grad"),
                     _weight_grad(mix, dmixed, "w_out_grad"))

    return dh[N_META:N_META + seq], token


def _shard_major(full, axis):
    shape = full.shape
    split = full.reshape(shape[:axis] + (N_DEV, shape[axis] // N_DEV) + shape[axis + 1:])
    return jnp.moveaxis(split, axis, 0)


def _whole_from_shards(shards, axis):
    moved = jnp.moveaxis(shards, 0, axis)
    shape = moved.shape
    return moved.reshape(shape[:axis] + (shape[axis] * shape[axis + 1],) + shape[axis + 2:])


def kernel(x, meta_tokens, pre_norm_g, post_norm_g, w_in, conv_w, conv_b, conv_ln_g, conv_ln_b, w_pw2, b_pw2, w_out, loss_target, m_meta_tokens, m_pre_norm_g, m_post_norm_g, m_w_in, m_conv_w, m_conv_b, m_conv_ln_g, m_conv_ln_b, m_w_pw2, m_b_pw2, m_w_out, v_meta_tokens, v_pre_norm_g, v_post_norm_g, v_w_in, v_conv_w, v_conv_b, v_conv_ln_g, v_conv_ln_b, v_w_pw2, v_b_pw2, v_w_out):
    me = 4 * lax.axis_index("x") + 2 * lax.axis_index("y") + lax.axis_index("c")

    depth = w_in.shape[0]
    big = [w.astype(BF16) for w in (w_in, w_pw2, w_out)]
    *first, conv_w_s, meta_s = _all_gather([w[0] for w in big] + [conv_w, meta_tokens], "gather_first_layer")
    *gathering, token = _exchange_start([w[l] for l in range(1, depth) for w in big], meta_s, "gather_start", gather=True)
    conv_w_full = _whole_from_shards(conv_w_s, 2)
    meta_full = _whole_from_shards(meta_s, 1)
    shard_axis = (1, 0, 0)
    later = []

    def weights(l, h):
        if l == 0:
            return [_whole_from_shards(s, axis) for s, axis in zip(first, shard_axis)]
        if not later:
            later.extend(_exchange_wait(*gathering, h, "gather_wait", gather=True))
        return [_whole_from_shards(s, axis) for s, axis in zip(later[len(big) * (l - 1):len(big) * l], shard_axis)]

    in_flight = [None] * depth

    def ship(l, dh, dw_in, dw_pw2, dw_out):
        slabs = [_shard_major(dw, axis) for dw, axis in zip((dw_in, dw_pw2, dw_out), shard_axis)]
        *in_flight[l], token = _exchange_start(slabs, dh, f"exchange_start_{l}", gather=False)
        return token

    small_names = ("pre_g", "post_g", "conv_b", "ln_g", "ln_b", "b_pw2", "conv_w", "meta")
    small_in_flight, small_shapes_full = [], []

    def ship_small(grads, loss):
        small_full = [grads[k] for k in small_names] + [loss.reshape(1)]
        small_shapes_full.extend(a.shape for a in small_full)
        *in_flight_now, token = _exchange_start([_pack(small_full)], grads["meta"], "small_grads_start", gather=True)
        small_in_flight.extend(in_flight_now)
        return token

    dx, shipped = _local_step(x[0], loss_target[0], meta_full, pre_norm_g + token[:1, :1], post_norm_g, conv_w_full,
                              conv_b, conv_ln_g, conv_ln_b, b_pw2, weights, ship, ship_small)

    updated = [None] * depth
    done = shipped

    def update_layer(l):
        landed = _exchange_wait(*in_flight[l], done, f"exchange_wait_{l}", gather=False)
        return [_sum_adamw(parts, w[l], m[l], v[l], name) for parts, w, m, v, name in zip(
            landed, (w_in, w_pw2, w_out), (m_w_in, m_w_pw2, m_w_out), (v_w_in, v_w_pw2, v_w_out),
            ("adamw_w_in", "adamw_w_pw2", "adamw_w_out"))]

    for l in reversed(range(1, depth)):
        updated[l] = update_layer(l)
        done = updated[l][0][1]

    gathered, = _exchange_wait(*small_in_flight, done, "small_grads_wait", gather=True)
    summed = _unpack(_sum_parts(gathered, "sum_small_grads"), small_shapes_full)
    loss = summed[-1][0]
    g_small = dict(zip(small_names, summed))
    g_small["conv_w"] = lax.dynamic_slice_in_dim(g_small["conv_w"], me * conv_w.shape[2], conv_w.shape[2], axis=2)
    g_small["meta"] = lax.dynamic_slice_in_dim(g_small["meta"], me * meta_tokens.shape[1], meta_tokens.shape[1], axis=1)
    small_w = dict(zip(small_names, (pre_norm_g, post_norm_g, conv_b, conv_ln_g, conv_ln_b, b_pw2, conv_w, meta_tokens)))
    small_m = (m_pre_norm_g, m_post_norm_g, m_conv_b, m_conv_ln_g, m_conv_ln_b, m_b_pw2, m_conv_w, m_meta_tokens)
    small_v = (v_pre_norm_g, v_post_norm_g, v_conv_b, v_conv_ln_g, v_conv_ln_b, v_b_pw2, v_conv_w, v_meta_tokens)
    small_shapes = [small_w[k].shape for k in small_names]
    outs = _sum_adamw(_pack([g_small[k] for k in small_names])[None], _pack([small_w[k] for k in small_names]),
                      _pack(small_m), _pack(small_v), "adamw_small_weights")
    g_s, d_s, nm_s, nv_s = [dict(zip(small_names, _unpack(o, small_shapes))) for o in outs]

    done = outs[1]
    updated[0] = update_layer(0)
    (g_w_in, d_w_in, nm_w_in, nv_w_in), (g_w_pw2, d_w_pw2, nm_w_pw2, nv_w_pw2), (g_w_out, d_w_out, nm_w_out, nv_w_out) = [
        [jnp.stack([updated[l][a][k] for l in range(depth)]) for k in range(4)] for a in range(3)]

    def ordered(s, w_in_, w_pw2_, w_out_):
        return (s["meta"], s["pre_g"], s["post_g"], w_in_, s["conv_w"], s["conv_b"], s["ln_g"], s["ln_b"], w_pw2_,
                s["b_pw2"], w_out_)

    return (loss, dx[None], *ordered(g_s, g_w_in, g_w_pw2, g_w_out), *ordered(d_s, d_w_in, d_w_pw2, d_w_out),
            *ordered(nm_s, nm_w_in, nm_w_pw2, nm_w_out), *ordered(nv_s, nv_w_in, nv_w_pw2, nv_w_out))
```

```python
import functools

import jax
import jax.numpy as jnp
from jax import lax
from jax.experimental import pallas as pl
from jax.experimental.pallas import tpu as pltpu

F32 = jnp.float32
BF16 = jnp.bfloat16

D_MODEL = 1024
D_CONV = 512
D_SB = 512
HEAD_DIM = 64
HEADS_PER_BLOCK = 4
HEAD_BLK = HEADS_PER_BLOCK * HEAD_DIM
CONV_WIDTH = 31
N_META = 16
D_IN = 3 * D_CONV + 4 * D_SB
RMS_EPS = 1e-6
LN_EPS = 1e-5
SB_SCALE = HEAD_DIM ** -0.5

ADAM_LR = 0.001
ADAM_B1 = 0.9
ADAM_B2 = 0.999
ADAM_EPS = 1e-08
ADAM_WD = 0.01
ADAM_STEP = 10

N_DEV = 8
LANES = 128
ROW_BLK = 256
DENSE_ROWS_MAX = 544
LIGHT_ROWS_MAX = 1088
HALO = 32
VMEM_LIMIT = 56 * 1024 * 1024
MESH = pl.DeviceIdType.MESH


def _cparams(*sem):
    return pltpu.CompilerParams(dimension_semantics=sem, vmem_limit_bytes=VMEM_LIMIT)


def _rows(n_cols, col=0, rows=ROW_BLK):
    return pl.BlockSpec((rows, n_cols), lambda i, col=col: (i, col))


def _dense_rows(t, most=DENSE_ROWS_MAX):
    packed_rows = 16
    return max(d for d in range(packed_rows, min(t, most) + 1, packed_rows) if t % d == 0)


def _whole(shape):
    return pl.BlockSpec(shape, lambda i: (0,) * len(shape))


def _sigmoid(x):
    return jax.nn.sigmoid(x)


def _dsilu(x, s):
    return s * (1.0 + x * (1.0 - s))


def _dot(a, b):
    return jnp.dot(a, b, preferred_element_type=F32)


def _dot_nt(a, b):
    return lax.dot_general(a, b, (((1,), (1,)), ((), ())), preferred_element_type=F32)


def _dot_tn(a, b):
    return lax.dot_general(a, b, (((0,), (0,)), ((), ())), preferred_element_type=F32)


def _inproj_fwd(h, g_pre, w_in):
    t = h.shape[0]
    blk = _dense_rows(t)

    def body(h_ref, g_ref, w_ref, pc_ref, qkv_ref, sbg_ref, u_ref):
        x = h_ref[...]
        r = lax.rsqrt(jnp.mean(x * x, axis=-1, keepdims=True) + RMS_EPS)
        u = (x * r * g_ref[...]).astype(BF16)
        u_ref[...] = u
        pc_ref[...] = _dot(u, w_ref[:, 0:1536])
        qkv_ref[...] = _dot(u, w_ref[:, 1536:3072]).astype(BF16)
        sbg_ref[...] = _dot(u, w_ref[:, 3072:3584])

    return pl.pallas_call(
        body, name="inproj_fwd", grid=(t // blk,),
        in_specs=[_rows(D_MODEL, rows=blk), _whole((1, D_MODEL)), _whole((D_MODEL, D_IN))],
        out_specs=[_rows(1536, rows=blk), _rows(1536, rows=blk), _rows(D_SB, rows=blk), _rows(D_MODEL, rows=blk)],
        out_shape=[jax.ShapeDtypeStruct((t, 1536), F32), jax.ShapeDtypeStruct((t, 1536), BF16),
                   jax.ShapeDtypeStruct((t, D_SB), F32), jax.ShapeDtypeStruct((t, D_MODEL), BF16)],
        compiler_params=_cparams("parallel"),
    )(h, g_pre, w_in)


def _prev_halo(col):
    per = ROW_BLK // HALO
    return pl.BlockSpec((HALO, D_CONV), lambda i, col=col: (jnp.maximum(i * per - 1, 0), col))


def _fill_glu(buf, i, a_ref, b_ref, ha_ref, hb_ref):
    halo = ha_ref[...] * _sigmoid(hb_ref[...])
    buf[0:HALO, :] = jnp.where(i > 0, halo, 0.0)
    buf[HALO:HALO + ROW_BLK, :] = a_ref[...] * _sigmoid(b_ref[...])


SUBLANES = 8
TAP_ROWS = 64
SHIFT_ROWS = HALO + ROW_BLK - SUBLANES


def _fill_shifts(shifts, buf):
    for b in range(1, SUBLANES):
        shifts[b - 1] = buf[pl.ds(b, SHIFT_ROWS), :]


def _window(buf, shifts, first, rows, lanes):
    whole, part = divmod(first, SUBLANES)
    src = buf if part == 0 else shifts.at[part - 1]
    return src[pl.ds(rows.start + SUBLANES * whole, rows.size), lanes]


TAP_ROW_CHUNKS = [pl.ds(r, TAP_ROWS) for r in range(0, ROW_BLK, TAP_ROWS)]
TAP_LANE_TILES = [pl.ds(c, LANES) for c in range(0, D_CONV, LANES)]


def _layer_norm_stats(cv):
    mu = jnp.mean(cv, axis=-1, keepdims=True)
    xc = cv - mu
    rstd = lax.rsqrt(jnp.mean(xc * xc, axis=-1, keepdims=True) + LN_EPS)
    return xc * rstd, rstd


def _conv_fwd(pc, conv_w, conv_b, ln_g, ln_b, w_pw2, b_pw2):
    t = pc.shape[0]

    def body(a_ref, b_ref, gate_ref, ha_ref, hb_ref, cw_ref, cb_ref, lg_ref, lb_ref, wp_ref, bp_ref,
             cout_ref, cv_ref, p_ref, sl_ref, buf, shifts):
        i = pl.program_id(0)
        _fill_glu(buf, i, a_ref, b_ref, ha_ref, hb_ref)
        _fill_shifts(shifts, buf)
        for lanes in TAP_LANE_TILES:
            for rows in TAP_ROW_CHUNKS:
                acc = jnp.zeros((TAP_ROWS, LANES), F32) + cb_ref[:, lanes]
                for j in range(CONV_WIDTH):
                    acc = acc + cw_ref[j:j + 1, lanes] * _window(buf, shifts, HALO - (CONV_WIDTH - 1) + j, rows, lanes)
                cv_ref[rows, lanes] = acc
        xh, _ = _layer_norm_stats(cv_ref[...])
        ln = xh * lg_ref[...] + lb_ref[...]
        sl = (ln * _sigmoid(ln)).astype(BF16)
        sl_ref[...] = sl
        p = _dot(sl, wp_ref[...]) + bp_ref[...]
        p_ref[...] = p
        gate = gate_ref[...]
        cout_ref[...] = (p * (gate * _sigmoid(gate))).astype(BF16)

    vec = _whole((1, D_CONV))
    return pl.pallas_call(
        body, name="conv_fwd", grid=(t // ROW_BLK,),
        in_specs=[_rows(D_CONV, 0), _rows(D_CONV, 1), _rows(D_CONV, 2), _prev_halo(0), _prev_halo(1),
                  _whole((CONV_WIDTH, D_CONV)), vec, vec, vec, _whole((D_CONV, D_CONV)), vec],
        out_specs=[_rows(D_CONV)] * 4,
        out_shape=[jax.ShapeDtypeStruct((t, D_CONV), BF16), jax.ShapeDtypeStruct((t, D_CONV), F32),
                   jax.ShapeDtypeStruct((t, D_CONV), F32), jax.ShapeDtypeStruct((t, D_CONV), BF16)],
        scratch_shapes=[pltpu.VMEM((HALO + ROW_BLK, D_CONV), F32), pltpu.VMEM((SUBLANES - 1, SHIFT_ROWS, D_CONV), F32)],
        compiler_params=_cparams("parallel"),
    )(pc, pc, pc, pc, pc, conv_w, conv_b, ln_g, ln_b, w_pw2, b_pw2)


def _lower_triangle():
    row = lax.broadcasted_iota(jnp.int32, (ROW_BLK, ROW_BLK), 0)
    col = lax.broadcasted_iota(jnp.int32, (ROW_BLK, ROW_BLK), 1)
    return row > col


def _lower_triangle_t():
    row = lax.broadcasted_iota(jnp.int32, (ROW_BLK, ROW_BLK), 0)
    col = lax.broadcasted_iota(jnp.int32, (ROW_BLK, ROW_BLK), 1)
    return row < col


def _tri_sum(x, umat):
    return _dot(x.astype(BF16), umat)


def _log_gates(z):
    ls = -(jnp.maximum(z, 0.0) + jnp.log(1.0 + jnp.exp(-jnp.abs(z))))
    return ls, z + ls


def _head_lanes(hh):
    lane = lax.broadcasted_iota(jnp.int32, (ROW_BLK, HEAD_BLK), 1)
    return (lane >= HEAD_DIM * hh) & (lane < HEAD_DIM * (hh + 1))


def _merge_heads(acc_ref):
    out = acc_ref[HEADS_PER_BLOCK - 1]
    for hh in range(HEADS_PER_BLOCK - 1):
        out = jnp.where(_head_lanes(hh), acc_ref[hh], out)
    return out


def _qkv_specs(t):
    n_blk = D_SB // HEAD_BLK
    return [pl.BlockSpec((ROW_BLK, HEAD_BLK), lambda hp, i: (i, hp)),
            pl.BlockSpec((t, HEAD_BLK), lambda hp, i: (0, n_blk + hp)),
            pl.BlockSpec((t, HEAD_BLK), lambda hp, i: (0, 2 * n_blk + hp))]


def _carry_spec():
    return pl.BlockSpec((HEADS_PER_BLOCK, ROW_BLK, LANES), lambda hp, i: (hp, i, 0))


def _last_block_rows(t, n_tokens):
    packed_rows = 16
    return -(-(n_tokens - (t - ROW_BLK)) // packed_rows) * packed_rows


def _by_block_rows(i, last_rows, sweep):
    if last_rows == ROW_BLK:
        sweep(ROW_BLK)
        return
    last = pl.num_programs(1) - 1
    pl.when(i < last)(lambda: sweep(ROW_BLK))
    pl.when(i == last)(lambda: sweep(last_rows))


def _attn_fwd(qkv, n_tokens):
    t = qkv.shape[0]
    assert t // ROW_BLK <= LANES

    def body(q_ref, k_ref, v_ref, o_ref, c_ref, acc_ref, run_ref, qm_ref, z_ref):
        i = pl.program_id(1)
        q = q_ref[...]
        heads = range(HEADS_PER_BLOCK)
        for hh in heads:
            qm_ref[hh] = jnp.where(_head_lanes(hh), q, jnp.zeros_like(q)) * jnp.asarray(SB_SCALE, BF16)
        acc_ref[...] = jnp.zeros_like(acc_ref)
        c_ref[...] = jnp.zeros_like(c_ref)
        run_ref[...] = jnp.zeros_like(run_ref)

        def sweep(n_rows):
            rows = pl.ds(0, n_rows)
            lower = _lower_triangle()[:n_rows]
            umat = jnp.where(_lower_triangle(), 1.0, 0.0).astype(BF16)
            lane = lax.broadcasted_iota(jnp.int32, (n_rows, LANES), 1)

            def scores(jb):
                start = pl.multiple_of(jb * ROW_BLK, ROW_BLK)
                kb = k_ref[pl.ds(start, ROW_BLK), :]
                for hh in heads:
                    z_ref[hh, rows] = _dot_nt(qm_ref[hh, rows], kb)

            def block(jb, diagonal):
                start = pl.multiple_of(jb * ROW_BLK, ROW_BLK)
                vb = v_ref[pl.ds(start, ROW_BLK), :]
                logits = []
                for hh in heads:
                    ls, lb = _log_gates(z_ref[hh, rows])
                    if diagonal:
                        ls = jnp.where(lower, ls, 0.0)
                    run = run_ref[hh, rows]
                    if not diagonal:
                        c_ref[hh, rows] = jnp.where(lane == jb, run, c_ref[hh, rows])
                    logits.append(lb + jnp.concatenate([run, run], axis=1) + _tri_sum(ls, umat))
                    run_ref[hh, rows] = run + jnp.sum(ls, axis=1, keepdims=True)
                scores(jnp.maximum(jb - 1, 0))
                for hh in heads:
                    a = jnp.exp(logits[hh])
                    if diagonal:
                        a = jnp.where(lower, a, 0.0)
                    acc_ref[hh, rows] += _dot(a.astype(BF16), vb)

            scores(i)
            block(i, True)

            @pl.loop(0, i)
            def _(n):
                block(i - 1 - n, False)

        _by_block_rows(i, _last_block_rows(t, n_tokens), sweep)
        o_ref[...] = _merge_heads(acc_ref)

    per_head = (HEADS_PER_BLOCK, ROW_BLK, HEAD_BLK)
    return pl.pallas_call(
        body, name="attn_fwd", grid=(D_SB // HEAD_BLK, t // ROW_BLK),
        in_specs=_qkv_specs(t),
        out_specs=[pl.BlockSpec((ROW_BLK, HEAD_BLK), lambda hp, i: (i, hp)), _carry_spec()],
        out_shape=[jax.ShapeDtypeStruct((t, D_SB), F32),
                   jax.ShapeDtypeStruct((D_SB // HEAD_DIM, t, LANES), F32)],
        scratch_shapes=[pltpu.VMEM(per_head, F32), pltpu.VMEM((HEADS_PER_BLOCK, ROW_BLK, LANES), F32),
                        pltpu.VMEM(per_head, BF16), pltpu.VMEM((HEADS_PER_BLOCK, ROW_BLK, ROW_BLK), F32)],
        compiler_params=_cparams("arbitrary", "arbitrary"),
    )(qkv, qkv, qkv)


def _outproj_fwd(h, cout, sraw, sbg, w_out, g_post):
    t = h.shape[0]
    blk = _dense_rows(t, LIGHT_ROWS_MAX)

    def body(h_ref, c_ref, s_ref, g_ref, w_ref, gp_ref, hn_ref, mixed_ref, mix_ref):
        gate = g_ref[...]
        mix_ref[:, 0:D_CONV] = c_ref[...]
        mix_ref[:, D_CONV:] = (s_ref[...] * (gate * _sigmoid(gate))).astype(BF16)
        mixed = _dot(mix_ref[...], w_ref[...])
        mixed_ref[...] = mixed
        r = lax.rsqrt(jnp.mean(mixed * mixed, axis=-1, keepdims=True) + RMS_EPS)
        hn_ref[...] = h_ref[...] + mixed * r * gp_ref[...]

    return pl.pallas_call(
        body, name="outproj_fwd", grid=(t // blk,),
        in_specs=[_rows(D_MODEL, rows=blk), _rows(D_CONV, rows=blk), _rows(D_SB, rows=blk), _rows(D_SB, rows=blk),
                  _whole((D_MODEL, D_MODEL)), _whole((1, D_MODEL))],
        out_specs=[_rows(D_MODEL, rows=blk)] * 3,
        out_shape=[jax.ShapeDtypeStruct((t, D_MODEL), F32), jax.ShapeDtypeStruct((t, D_MODEL), F32),
                   jax.ShapeDtypeStruct((t, D_MODEL), BF16)],
        compiler_params=_cparams("parallel"),
    )(h, cout, sraw, sbg, w_out, g_post)


def _loss_and_grad(h, target, seq):
    t = h.shape[0]
    blk = _dense_rows(t, LIGHT_ROWS_MAX)

    def body(h_ref, t_ref, loss_ref, dh_ref):
        i = pl.program_id(0)

        @pl.when(i == 0)
        def _():
            loss_ref[...] = jnp.zeros_like(loss_ref)

        row = i * blk + lax.broadcasted_iota(jnp.int32, (blk, D_MODEL), 0)
        real = (row >= N_META) & (row < N_META + seq)
        diff = jnp.where(real, h_ref[...] - t_ref[...], 0.0)
        sq = jnp.sum(jnp.sum(diff * diff, axis=1, keepdims=True), axis=0, keepdims=True)
        loss_ref[...] += (0.5 / D_MODEL) * sq
        dh_ref[...] = diff * (1.0 / D_MODEL)

    return pl.pallas_call(
        body, name="loss", grid=(t // blk,),
        in_specs=[_rows(D_MODEL, rows=blk), _rows(D_MODEL, rows=blk)],
        out_specs=[_whole((1, 1)), _rows(D_MODEL, rows=blk)],
        out_shape=[jax.ShapeDtypeStruct((1, 1), F32), jax.ShapeDtypeStruct((t, D_MODEL), F32)],
        compiler_params=_cparams("arbitrary"),
    )(h, target)


def _outproj_bwd(dh, mixed, g_post, sraw, sbg, w_out_t):
    t = dh.shape[0]
    blk = _dense_rows(t, LIGHT_ROWS_MAX)

    def body(dh_ref, mixed_ref, gp_ref, s_ref, g_ref, wt_ref, dc_ref, ds_ref, dg_ref, dmb_ref, dgp_ref):
        @pl.when(pl.program_id(0) == 0)
        def _():
            dgp_ref[...] = jnp.zeros_like(dgp_ref)

        mixed = mixed_ref[...]
        r = lax.rsqrt(jnp.mean(mixed * mixed, axis=-1, keepdims=True) + RMS_EPS)
        nh = mixed * r
        dy = dh_ref[...]
        dgp_ref[...] += jnp.sum(dy * nh, axis=0, keepdims=True)
        dn = dy * gp_ref[...]
        dmixed = (r * (dn - nh * jnp.mean(dn * nh, axis=-1, keepdims=True))).astype(BF16)
        dmb_ref[...] = dmixed
        dmix = _dot_nt(dmixed, wt_ref[...])
        dc_ref[...] = dmix[:, 0:D_CONV]
        dsg = dmix[:, D_CONV:]
        gate = g_ref[...]
        sg = _sigmoid(gate)
        ds_ref[...] = dsg * (gate * sg)
        dg_ref[...] = (dsg * s_ref[...] * _dsilu(gate, sg)).astype(BF16)

    return pl.pallas_call(
        body, name="outproj_bwd", grid=(t // blk,),
        in_specs=[_rows(D_MODEL, rows=blk), _rows(D_MODEL, rows=blk), _whole((1, D_MODEL)), _rows(D_SB, rows=blk),
                  _rows(D_SB, rows=blk), _whole((D_MODEL, D_MODEL))],
        out_specs=[_rows(D_CONV, rows=blk), _rows(D_SB, rows=blk), _rows(D_SB, rows=blk), _rows(D_MODEL, rows=blk),
                   _whole((1, D_MODEL))],
        out_shape=[jax.ShapeDtypeStruct((t, D_CONV), F32), jax.ShapeDtypeStruct((t, D_SB), F32),
                   jax.ShapeDtypeStruct((t, D_SB), BF16), jax.ShapeDtypeStruct((t, D_MODEL), BF16),
                   jax.ShapeDtypeStruct((1, D_MODEL), F32)],
        compiler_params=_cparams("arbitrary"),
    )(dh, mixed, g_post, sraw, sbg, w_out_t)


def _attn_bwd(qkv, carries, do, n_tokens):
    t = qkv.shape[0]

    def body(q_ref, k_ref, v_ref, c_ref, do_ref, dq_ref, dk_ref, dv_ref, acc_ref, seen_ref, qm_ref, dom_ref, z_ref,
             da_ref, dz_ref, a_ref):
        i = pl.program_id(1)

        @pl.when(i == 0)
        def _():
            dk_ref[...] = jnp.zeros_like(dk_ref)
            dv_ref[...] = jnp.zeros_like(dv_ref)

        q = q_ref[...]
        dof = do_ref[...]
        heads = range(HEADS_PER_BLOCK)
        for hh in heads:
            qm_ref[hh] = jnp.where(_head_lanes(hh), q, jnp.zeros_like(q)) * jnp.asarray(SB_SCALE, BF16)
            dom_ref[hh] = jnp.where(_head_lanes(hh), dof, 0.0).astype(BF16)
        acc_ref[...] = jnp.zeros_like(acc_ref)
        seen_ref[...] = jnp.zeros_like(seen_ref)

        def sweep(n_rows):
            rows = pl.ds(0, n_rows)
            lower = _lower_triangle()[:n_rows]
            umat = jnp.where(_lower_triangle(), 1.0, 0.0).astype(BF16)
            umat_t = jnp.where(_lower_triangle_t(), 1.0, 0.0).astype(BF16)
            lane = lax.broadcasted_iota(jnp.int32, (n_rows, LANES), 1)

            def scores(jb):
                start = pl.multiple_of(jb * ROW_BLK, ROW_BLK)
                kb = k_ref[pl.ds(start, ROW_BLK), :]
                for hh in heads:
                    z_ref[hh, rows] = _dot_nt(qm_ref[hh, rows], kb)

            def value_grads(jb):
                start = pl.multiple_of(jb * ROW_BLK, ROW_BLK)
                vb = v_ref[pl.ds(start, ROW_BLK), :]
                for hh in heads:
                    da_ref[hh, rows] = _dot_nt(dom_ref[hh, rows], vb)

            def products(jb, hh):
                start = pl.multiple_of(jb * ROW_BLK, ROW_BLK)
                dzb = dz_ref[hh, rows]
                acc_ref[hh, rows] += _dot(dzb, k_ref[pl.ds(start, ROW_BLK), :])
                dk_ref[pl.ds(start, ROW_BLK), :] += _dot_tn(dzb, qm_ref[hh, rows])
                dv_ref[pl.ds(start, ROW_BLK), :] += _dot_tn(a_ref[hh, rows], dom_ref[hh, rows])

            def block(jb, diagonal):
                before = jnp.maximum(jb - 1, 0)
                lbs, logits = [], []
                for hh in heads:
                    products(before, hh)
                    ls, lb = _log_gates(z_ref[hh, rows])
                    if diagonal:
                        ls = jnp.where(lower, ls, 0.0)
                        logits.append(lb + _tri_sum(ls, umat))
                    else:
                        right = jnp.sum(jnp.where(lane == jb, c_ref[hh, rows], 0.0), axis=1, keepdims=True)
                        logits.append(lb + right + _tri_sum(ls, umat))
                    lbs.append(lb)
                if not diagonal:
                    scores(jb + 1)
                gs, befores = [], []
                for hh in heads:
                    a = jnp.exp(logits[hh])
                    if diagonal:
                        a = jnp.where(lower, a, 0.0)
                    g = da_ref[hh, rows] * a
                    seen = seen_ref[hh, rows]
                    befores.append(jnp.concatenate([seen, seen], axis=1) + _tri_sum(g, umat_t))
                    seen_ref[hh, rows] = seen + jnp.sum(g, axis=1, keepdims=True)
                    a_ref[hh, rows] = a.astype(BF16)
                    gs.append(g)
                if not diagonal:
                    value_grads(jb + 1)
                for hh in heads:
                    dz = gs[hh] - jnp.exp(lbs[hh]) * (gs[hh] + befores[hh])
                    if diagonal:
                        dz = jnp.where(lower, dz, 0.0)
                    dz_ref[hh, rows] = dz.astype(BF16)

            dz_ref[...] = jnp.zeros_like(dz_ref)
            a_ref[...] = jnp.zeros_like(a_ref)
            scores(0)
            value_grads(0)

            @pl.loop(0, i)
            def _(jb):
                block(jb, False)

            block(i, True)
            for hh in heads:
                products(i, hh)

        _by_block_rows(i, _last_block_rows(t, n_tokens), sweep)
        dq_ref[...] = (_merge_heads(acc_ref) * SB_SCALE).astype(BF16)

    blk = pl.BlockSpec((ROW_BLK, HEAD_BLK), lambda hp, i: (i, hp))
    full = pl.BlockSpec((t, HEAD_BLK), lambda hp, i: (0, hp))
    per_head = (HEADS_PER_BLOCK, ROW_BLK, HEAD_BLK)
    return pl.pallas_call(
        body, name="attn_bwd", grid=(D_SB // HEAD_BLK, t // ROW_BLK),
        in_specs=_qkv_specs(t) + [_carry_spec(), blk],
        out_specs=[blk, full, full],
        out_shape=[jax.ShapeDtypeStruct((t, D_SB), BF16)] + [jax.ShapeDtypeStruct((t, D_SB), F32)] * 2,
        scratch_shapes=[pltpu.VMEM(per_head, F32), pltpu.VMEM((HEADS_PER_BLOCK, ROW_BLK, LANES), F32),
                        pltpu.VMEM(per_head, BF16), pltpu.VMEM(per_head, BF16)]
                       + [pltpu.VMEM((HEADS_PER_BLOCK, ROW_BLK, ROW_BLK), dtype) for dtype in (F32, F32, BF16, BF16)],
        compiler_params=_cparams("arbitrary", "arbitrary"),
    )(qkv, qkv, qkv, carries, do)


def _conv_bwd_rows(dcout, pc, cv, p, ln_g, ln_b, w_pw2_t):
    t = dcout.shape[0]
    blk = _dense_rows(t, LIGHT_ROWS_MAX)

    def body(dc_ref, gate_ref, cv_ref, p_ref, lg_ref, lb_ref, wt_ref, dcv_ref, dgate_ref, dpb_ref, vec_ref):
        @pl.when(pl.program_id(0) == 0)
        def _():
            vec_ref[...] = jnp.zeros_like(vec_ref)

        dc = dc_ref[...]
        gate = gate_ref[...]
        sg = _sigmoid(gate)
        dp = dc * (gate * sg)
        dgate_ref[...] = (dc * p_ref[...] * _dsilu(gate, sg)).astype(BF16)
        dpb = dp.astype(BF16)
        dpb_ref[...] = dpb
        xh, rstd = _layer_norm_stats(cv_ref[...])
        ln = xh * lg_ref[...] + lb_ref[...]
        s2 = _sigmoid(ln)
        dln = _dot_nt(dpb, wt_ref[...]) * _dsilu(ln, s2)
        dxh = dln * lg_ref[...]
        dcv = rstd * (dxh - jnp.mean(dxh, axis=-1, keepdims=True)
                      - xh * jnp.mean(dxh * xh, axis=-1, keepdims=True))
        dcv_ref[...] = dcv
        vec_ref[0:1, :] += jnp.sum(dp, axis=0, keepdims=True)
        vec_ref[1:2, :] += jnp.sum(dln * xh, axis=0, keepdims=True)
        vec_ref[2:3, :] += jnp.sum(dln, axis=0, keepdims=True)
        vec_ref[3:4, :] += jnp.sum(dcv, axis=0, keepdims=True)

    vec = _whole((1, D_CONV))
    return pl.pallas_call(
        body, name="conv_bwd_rows", grid=(t // blk,),
        in_specs=[_rows(D_CONV, rows=blk), _rows(D_CONV, 2, rows=blk), _rows(D_CONV, rows=blk), _rows(D_CONV, rows=blk),
                  vec, vec, _whole((D_CONV, D_CONV))],
        out_specs=[_rows(D_CONV, rows=blk), _rows(D_CONV, rows=blk), _rows(D_CONV, rows=blk), _whole((8, D_CONV))],
        out_shape=[jax.ShapeDtypeStruct((t, D_CONV), F32), jax.ShapeDtypeStruct((t, D_CONV), BF16),
                   jax.ShapeDtypeStruct((t, D_CONV), BF16), jax.ShapeDtypeStruct((8, D_CONV), F32)],
        compiler_params=_cparams("arbitrary"),
    )(dcout, pc, cv, p, ln_g, ln_b, w_pw2_t)


def _conv_bwd_taps(dcv, pc, conv_w):
    t = dcv.shape[0]
    n_halo = t // HALO
    per = ROW_BLK // HALO

    def body(d_ref, dn_ref, a_ref, b_ref, ha_ref, hb_ref, cw_ref, da_ref, db_ref, dw_ref, cbuf, dbuf, cshifts, dshifts):
        i = pl.program_id(0)

        @pl.when(i == 0)
        def _():
            dw_ref[...] = jnp.zeros_like(dw_ref)

        _fill_glu(cbuf, i, a_ref, b_ref, ha_ref, hb_ref)
        dbuf[0:ROW_BLK, :] = d_ref[...]
        dbuf[ROW_BLK:ROW_BLK + HALO, :] = jnp.where(i < pl.num_programs(0) - 1, dn_ref[...], 0.0)
        _fill_shifts(cshifts, cbuf)
        _fill_shifts(dshifts, dbuf)
        for lanes in TAP_LANE_TILES:
            for rows in TAP_ROW_CHUNKS:
                acc = jnp.zeros((TAP_ROWS, LANES), F32)
                for j in range(CONV_WIDTH):
                    acc = acc + cw_ref[j:j + 1, lanes] * _window(dbuf, dshifts, CONV_WIDTH - 1 - j, rows, lanes)
                sb = _sigmoid(b_ref[rows, lanes])
                da_ref[rows, lanes] = (acc * sb).astype(BF16)
                db_ref[rows, lanes] = (acc * a_ref[rows, lanes] * sb * (1.0 - sb)).astype(BF16)
            for j in range(CONV_WIDTH):
                acc = jnp.zeros((TAP_ROWS, LANES), F32)
                for rows in TAP_ROW_CHUNKS:
                    acc = acc + d_ref[rows, lanes] * _window(cbuf, cshifts, HALO - (CONV_WIDTH - 1) + j, rows, lanes)
                dw_ref[j:j + 1, lanes] += jnp.sum(acc, axis=0, keepdims=True)

    return pl.pallas_call(
        body, name="conv_bwd_taps", grid=(t // ROW_BLK,),
        in_specs=[_rows(D_CONV),
                  pl.BlockSpec((HALO, D_CONV), lambda i: (jnp.minimum((i + 1) * per, n_halo - 1), 0)),
                  _rows(D_CONV, 0), _rows(D_CONV, 1), _prev_halo(0), _prev_halo(1),
                  _whole((CONV_WIDTH, D_CONV))],
        out_specs=[_rows(D_CONV), _rows(D_CONV), _whole((32, D_CONV))],
        out_shape=[jax.ShapeDtypeStruct((t, D_CONV), BF16), jax.ShapeDtypeStruct((t, D_CONV), BF16),
                   jax.ShapeDtypeStruct((32, D_CONV), F32)],
        scratch_shapes=[pltpu.VMEM((HALO + ROW_BLK, D_CONV), F32), pltpu.VMEM((ROW_BLK + HALO, D_CONV), F32),
                        pltpu.VMEM((SUBLANES - 1, SHIFT_ROWS, D_CONV), F32),
                        pltpu.VMEM((SUBLANES - 1, SHIFT_ROWS, D_CONV), F32)],
        compiler_params=_cparams("arbitrary"),
    )(dcv, dcv, pc, pc, pc, pc, conv_w)


def _inproj_bwd(dh_out, h, g_pre, pieces, w_in_t):
    t = h.shape[0]
    blk = _dense_rows(t)

    def body(dh_ref, h_ref, g_ref, *rest):
        piece_refs, (wt_ref, dhin_ref, dproj_ref, dg_ref) = rest[:7], rest[7:]

        @pl.when(pl.program_id(0) == 0)
        def _():
            dg_ref[...] = jnp.zeros_like(dg_ref)

        for k, ref in enumerate(piece_refs):
            dproj_ref[:, 512 * k:512 * (k + 1)] = ref[...].astype(BF16)
        du = _dot_nt(dproj_ref[...], wt_ref[...])
        x = h_ref[...]
        r = lax.rsqrt(jnp.mean(x * x, axis=-1, keepdims=True) + RMS_EPS)
        xh = x * r
        dg_ref[...] += jnp.sum(du * xh, axis=0, keepdims=True)
        dxh = du * g_ref[...]
        dhin_ref[...] = dh_ref[...] + r * (dxh - xh * jnp.mean(dxh * xh, axis=-1, keepdims=True))

    return pl.pallas_call(
        body, name="inproj_bwd", grid=(t // blk,),
        in_specs=[_rows(D_MODEL, rows=blk), _rows(D_MODEL, rows=blk), _whole((1, D_MODEL))] + [_rows(512, rows=blk)] * 7
                 + [_whole((D_MODEL, D_IN))],
        out_specs=[_rows(D_MODEL, rows=blk), _rows(D_IN, rows=blk), _whole((1, D_MODEL))],
        out_shape=[jax.ShapeDtypeStruct((t, D_MODEL), F32), jax.ShapeDtypeStruct((t, D_IN), BF16),
                   jax.ShapeDtypeStruct((1, D_MODEL), F32)],
        compiler_params=_cparams("arbitrary"),
    )(dh_out, h, g_pre, *pieces, w_in_t)


def _weight_grad(xb, dyb, name):
    t, k = xb.shape
    n = dyb.shape[1]

    def body(x_ref, dy_ref, o_ref, acc_ref):
        i = pl.program_id(0)

        @pl.when(i == 0)
        def _():
            acc_ref[...] = jnp.zeros_like(acc_ref)

        acc_ref[...] += _dot_tn(x_ref[...], dy_ref[...])

        @pl.when(i == pl.num_programs(0) - 1)
        def _():
            o_ref[...] = acc_ref[...].astype(BF16)

    return pl.pallas_call(
        body, name=name, grid=(t // ROW_BLK,),
        in_specs=[_rows(k), _rows(n)], out_specs=_whole((k, n)), out_shape=jax.ShapeDtypeStruct((k, n), BF16),
        scratch_shapes=[pltpu.VMEM((k, n), F32)],
        compiler_params=_cparams("arbitrary"),
    )(xb, dyb)


def _position():
    return lax.axis_index("x"), lax.axis_index("y"), lax.axis_index("c")


def _comm_call(body, name, ins, out_shapes):
    n = len(ins)
    hbm = pl.BlockSpec(memory_space=pltpu.HBM)
    return pl.pallas_call(
        functools.partial(body, n), name=name, in_specs=[hbm] * n, out_specs=[hbm] * n, out_shape=out_shapes,
        scratch_shapes=[pltpu.SemaphoreType.DMA((n, N_DEV - 1)), pltpu.SemaphoreType.DMA((n, N_DEV - 1)),
                        pltpu.SemaphoreType.DMA((n,))],
    )(*ins)


def _all_gather(blocks, name):
    def body(n, *refs):
        x_refs, out_refs, (send_sems, recv_sems, local_sems) = refs[:n], refs[n:2 * n], refs[2 * n:]
        x, y, c = _position()
        me, sibling = (x, y, c), (x, y, 1 - c)
        chips = [(1 - x, y), (x, 1 - y), (1 - x, 1 - y)]

        def slot(a, px, py, pc):
            return out_refs[a].at[4 * px + 2 * py + pc]

        def copy(a, k, origin, to, own=False):
            return pltpu.make_async_remote_copy(
                src_ref=x_refs[a] if own else slot(a, *origin), dst_ref=slot(a, *origin),
                send_sem=send_sems.at[a, k], recv_sem=recv_sems.at[a, k], device_id=to, device_id_type=MESH)

        arrays = range(n)
        mine = [pltpu.make_async_copy(x_refs[a], slot(a, *me), local_sems.at[a]) for a in arrays]
        first = [copy(a, 1 + j, me, (*chip, c), own=True) for j, chip in enumerate(chips) for a in arrays]
        first += [copy(a, 0, me, sibling, own=True) for a in arrays]
        for cp in mine + first:
            cp.start()
        passed = []
        for j, chip in enumerate(chips):
            for a in arrays:
                copy(a, 1 + j, (*chip, c), me).wait_recv()
                passed.append(copy(a, 4 + j, (*chip, c), sibling))
                passed[-1].start()
        for a in arrays:
            copy(a, 0, sibling, me).wait_recv()
            for j, chip in enumerate(chips):
                copy(a, 4 + j, (*chip, 1 - c), me).wait_recv()
        for cp in first + passed:
            cp.wait_send()
        for cp in mine:
            cp.wait()

    return _comm_call(body, name, blocks, [jax.ShapeDtypeStruct((N_DEV,) + b.shape, b.dtype) for b in blocks])


def _exchange_copies(g_refs, land_refs, sems, gather):
    x, y, c = _position()
    me = 4 * x + 2 * y + c
    out = []
    for g_ref, land_ref, (send_sem, recv_sem, local_sem) in zip(g_refs, land_refs, sems):
        def mine(slot, g_ref=g_ref):
            return g_ref if gather else g_ref.at[slot]

        def remote(src, dst, dev):
            return pltpu.make_async_remote_copy(src_ref=src, dst_ref=dst, send_sem=send_sem, recv_sem=recv_sem,
                                                device_id=dev, device_id_type=MESH)

        sends = []
        for k in range(1, N_DEV):
            px = 1 - x if k & 4 else x
            py = 1 - y if k & 2 else y
            pc = 1 - c if k & 1 else c
            sends.append(remote(mine(4 * px + 2 * py + pc), land_ref.at[me], (px, py, pc)))
        seven = land_ref.at[pl.ds(0, N_DEV - 1)]
        out.append((pltpu.make_async_copy(mine(me), land_ref.at[me], local_sem), sends, remote(seven, seven, (x, y, c))))
    return out


_HBM = pl.BlockSpec(memory_space=pltpu.HBM)
_SEM = pl.BlockSpec(memory_space=pltpu.SEMAPHORE)
_ORDERED = pltpu.CompilerParams(has_side_effects=pltpu.SideEffectType.DATAFLOW_SIDE_EFFECTING)
SEMS_PER_ARRAY = 3


def _exchange_start(arrays, after, name, gather):
    n = len(arrays)
    n_sems = SEMS_PER_ARRAY * n

    def body(*refs):
        g_refs, land_refs, sems, token = refs[:n], refs[n:2 * n], refs[2 * n + 1:2 * n + 1 + n_sems], refs[-1]
        sems = [sems[SEMS_PER_ARRAY * a:SEMS_PER_ARRAY * (a + 1)] for a in range(n)]
        for local, sends, _ in _exchange_copies(g_refs, land_refs, sems, gather):
            local.start()
            for cp in sends:
                cp.start()
        token[...] = jnp.zeros_like(token)

    buffers = list(arrays) + [lax.empty((N_DEV,) + g.shape if gather else g.shape, g.dtype) for g in arrays]
    outs = pl.pallas_call(
        body, name=name, in_specs=[_HBM] * (2 * n) + [pl.BlockSpec(memory_space=pl.ANY)],
        out_specs=[_SEM] * n_sems + [_HBM] * (2 * n) + [pl.BlockSpec(memory_space=pltpu.VMEM)],
        out_shape=[pltpu.SemaphoreType.DMA(())] * n_sems + [pltpu.HBM(b.shape, b.dtype) for b in buffers]
                  + [jax.ShapeDtypeStruct((8, LANES), F32)],
        input_output_aliases={a: n_sems + a for a in range(2 * n)}, compiler_params=_ORDERED,
    )(*[pltpu.with_memory_space_constraint(b, pltpu.HBM) for b in buffers], after)
    return outs[:n_sems], outs[n_sems:n_sems + n], outs[n_sems + n:n_sems + 2 * n], outs[-1]


def _exchange_wait(sems, arrays, landings, after, name, gather):
    n = len(arrays)
    n_sems = SEMS_PER_ARRAY * n

    def body(*refs):
        g_refs, land_refs, sems = refs[:n], refs[n:2 * n], refs[2 * n:2 * n + n_sems]
        sems = [sems[SEMS_PER_ARRAY * a:SEMS_PER_ARRAY * (a + 1)] for a in range(n)]
        for local, _, all_seven in _exchange_copies(g_refs, land_refs, sems, gather):
            all_seven.wait_recv()
            all_seven.wait_send()
            local.wait()

    buffers = list(arrays) + list(landings)
    outs = pl.pallas_call(
        body, name=name, in_specs=[_HBM] * (2 * n) + [_SEM] * n_sems + [pl.BlockSpec(memory_space=pl.ANY)],
        out_specs=[_HBM] * (2 * n), out_shape=[pltpu.HBM(b.shape, b.dtype) for b in buffers],
        input_output_aliases={a: a for a in range(2 * n)}, compiler_params=_ORDERED,
    )(*buffers, *sems, after)
    return outs[n:]


def _block_rows(r, row_bytes, budget=1 << 20):
    cap = max(8, budget // row_bytes)
    return max(d for d in range(8, min(r, cap) + 1, 8) if r % d == 0)


def _sum_adamw(parts, w, m, v, name):
    n_parts, r, c = parts.shape
    br = _block_rows(r, 4 * c)

    def body(p_ref, w_ref, m_ref, v_ref, g_out, d_out, m_out, v_out):
        g = p_ref[0].astype(F32)
        for s in range(1, n_parts):
            g = g + p_ref[s].astype(F32)
        m_new = ADAM_B1 * m_ref[...] + (1.0 - ADAM_B1) * g
        v_new = ADAM_B2 * v_ref[...] + (1.0 - ADAM_B2) * (g * g)
        m_hat = m_new / (1.0 - ADAM_B1 ** ADAM_STEP)
        v_hat = v_new / (1.0 - ADAM_B2 ** ADAM_STEP)
        g_out[...] = g
        d_out[...] = -ADAM_LR * (m_hat / (jnp.sqrt(v_hat) + ADAM_EPS) + ADAM_WD * w_ref[...])
        m_out[...] = m_new
        v_out[...] = v_new

    row = pl.BlockSpec((br, c), lambda i: (i, 0))
    return pl.pallas_call(
        body, name=name, grid=(r // br,),
        in_specs=[pl.BlockSpec((n_parts, br, c), lambda i: (0, i, 0)), row, row, row],
        out_specs=[row] * 4, out_shape=[jax.ShapeDtypeStruct((r, c), F32)] * 4,
        compiler_params=_cparams("parallel"),
    )(parts, w, m, v)


def _sum_parts(parts, name):
    n_parts, r, c = parts.shape

    def body(p_ref, o_ref):
        g = p_ref[0]
        for s in range(1, n_parts):
            g = g + p_ref[s]
        o_ref[...] = g

    return pl.pallas_call(
        body, name=name, in_specs=[pl.BlockSpec(memory_space=pltpu.VMEM)],
        out_specs=pl.BlockSpec(memory_space=pltpu.VMEM), out_shape=jax.ShapeDtypeStruct((r, c), F32),
    )(parts)


def _pack(arrays):
    flat = jnp.concatenate([a.reshape(-1) for a in arrays])
    pad = -flat.shape[0] % (8 * LANES)
    if pad:
        flat = jnp.pad(flat, (0, pad))
    return flat.reshape(-1, LANES)


def _unpack(buf, shapes):
    flat = buf.reshape(-1)
    out, at = [], 0
    for shape in shapes:
        size = 1
        for d in shape:
            size *= d
        out.append(lax.slice_in_dim(flat, at, at + size).reshape(shape))
        at += size
    return out


def _local_step(x, target, meta, pre_g, post_g, conv_w, conv_b, ln_g, ln_b, b_pw2, weights, ship, ship_small):
    depth = pre_g.shape[0]
    seq = x.shape[0]
    t = -(-(N_META + seq) // ROW_BLK) * ROW_BLK
    tail = t - N_META - seq
    h = jnp.concatenate([meta, x, jnp.zeros((tail, D_MODEL), F32)], axis=0)
    target = jnp.pad(target, ((N_META, tail), (0, 0)))
    row = lambda a, l: a[l][None, :]

    saved = []
    for l in range(depth):
        w_in, w_pw2, w_out = weights(l, h)
        pc, qkv, sbg, u = _inproj_fwd(h, row(pre_g, l), w_in)
        cout, cv, p, sl = _conv_fwd(pc, conv_w[l], row(conv_b, l), row(ln_g, l), row(ln_b, l), w_pw2, row(b_pw2, l))
        sraw, carries = _attn_fwd(qkv, N_META + seq)
        h_new, mixed, mix = _outproj_fwd(h, cout, sraw, sbg, w_out, row(post_g, l))
        saved.append((h, pc, qkv, sbg, u, cv, p, sl, sraw, carries, mixed, mix, w_in, w_pw2, w_out))
        h = h_new

    loss, dh = _loss_and_grad(h, target, seq)

    grads = {k: [None] * depth for k in ("pre_g", "post_g", "conv_w", "conv_b", "ln_g", "ln_b", "b_pw2")}
    token = jnp.zeros((8, LANES), F32)
    for l in reversed(range(depth)):
        h_in, pc, qkv, sbg, u, cv, p, sl, sraw, carries, mixed, mix, w_in_t, w_pw2_t, w_out_t = saved[l]
        dcout, dsraw, dsbg, dmixed, dg_post = _outproj_bwd(dh, mixed, row(post_g, l) + token[:1, :1], sraw, sbg, w_out_t)
        dq, dk, dv = _attn_bwd(qkv, carries, dsraw, N_META + seq)
        dcv, dgate, dpb, vecs = _conv_bwd_rows(dcout, pc, cv, p, row(ln_g, l), row(ln_b, l), w_pw2_t)
        da, db, dconv_w = _conv_bwd_taps(dcv, pc, conv_w[l])
        dh, dproj, dg_pre = _inproj_bwd(dh, h_in, row(pre_g, l), (da, db, dgate, dq, dk, dv, dsbg), w_in_t)
        grads["pre_g"][l] = dg_pre[0]
        grads["post_g"][l] = dg_post[0]
        grads["b_pw2"][l], grads["ln_g"][l], grads["ln_b"][l], grads["conv_b"][l] = vecs[0], vecs[1], vecs[2], vecs[3]
        grads["conv_w"][l] = dconv_w[:CONV_WIDTH]
        after = dh
        if l == 0:
            grads = {k: jnp.stack(v) for k, v in grads.items()}
            grads["meta"] = dh[:N_META]
            after = ship_small(grads, loss[0, 0])
        token = ship(l, after, _weight_grad(u, dproj, "w_in_grad"), _weight_grad(sl, dpb, "w_pw2_grad"),
                     _weight_grad(mix, dmixed, "w_out_grad"))

    return dh[N_META:N_META + seq], token


def _shard_major(full, axis):
    shape = full.shape
    split = full.reshape(shape[:axis] + (N_DEV, shape[axis] // N_DEV) + shape[axis + 1:])
    return jnp.moveaxis(split, axis, 0)


def _whole_from_shards(shards, axis):
    moved = jnp.moveaxis(shards, 0, axis)
    shape = moved.shape
    return moved.reshape(shape[:axis] + (shape[axis] * shape[axis + 1],) + shape[axis + 2:])


def kernel(x, meta_tokens, pre_norm_g, post_norm_g, w_in, conv_w, conv_b, conv_ln_g, conv_ln_b, w_pw2, b_pw2, w_out, loss_target, m_meta_tokens, m_pre_norm_g, m_post_norm_g, m_w_in, m_conv_w, m_conv_b, m_conv_ln_g, m_conv_ln_b, m_w_pw2, m_b_pw2, m_w_out, v_meta_tokens, v_pre_norm_g, v_post_norm_g, v_w_in, v_conv_w, v_conv_b, v_conv_ln_g, v_conv_ln_b, v_w_pw2, v_b_pw2, v_w_out):
    me = 4 * lax.axis_index("x") + 2 * lax.axis_index("y") + lax.axis_index("c")

    depth = w_in.shape[0]
    big = [w.astype(BF16) for w in (w_in, w_pw2, w_out)]
    *first, conv_w_s, meta_s = _all_gather([w[0] for w in big] + [conv_w, meta_tokens], "gather_first_layer")
    *gathering, token = _exchange_start([w[l] for l in range(1, depth) for w in big], meta_s, "gather_start", gather=True)
    conv_w_full = _whole_from_shards(conv_w_s, 2)
    meta_full = _whole_from_shards(meta_s, 1)
    shard_axis = (1, 0, 0)
    later = []

    def weights(l, h):
        if l == 0:
            return [_whole_from_shards(s, axis) for s, axis in zip(first, shard_axis)]
        if not later:
            later.extend(_exchange_wait(*gathering, h, "gather_wait", gather=True))
        return [_whole_from_shards(s, axis) for s, axis in zip(later[len(big) * (l - 1):len(big) * l], shard_axis)]

    in_flight = [None] * depth

    def ship(l, dh, dw_in, dw_pw2, dw_out):
        slabs = [_shard_major(dw, axis) for dw, axis in zip((dw_in, dw_pw2, dw_out), shard_axis)]
        *in_flight[l], token = _exchange_start(slabs, dh, f"exchange_start_{l}", gather=False)
        return token

    small_names = ("pre_g", "post_g", "conv_b", "ln_g", "ln_b", "b_pw2", "conv_w", "meta")
    small_in_flight, small_shapes_full = [], []

    def ship_small(grads, loss):
        small_full = [grads[k] for k in small_names] + [loss.reshape(1)]
        small_shapes_full.extend(a.shape for a in small_full)
        *in_flight_now, token = _exchange_start([_pack(small_full)], grads["meta"], "small_grads_start", gather=True)
        small_in_flight.extend(in_flight_now)
        return token

    dx, shipped = _local_step(x[0], loss_target[0], meta_full, pre_norm_g + token[:1, :1], post_norm_g, conv_w_full,
                              conv_b, conv_ln_g, conv_ln_b, b_pw2, weights, ship, ship_small)

    updated = [None] * depth
    done = shipped

    def update_layer(l):
        landed = _exchange_wait(*in_flight[l], done, f"exchange_wait_{l}", gather=False)
        return [_sum_adamw(parts, w[l], m[l], v[l], name) for parts, w, m, v, name in zip(
            landed, (w_in, w_pw2, w_out), (m_w_in, m_w_pw2, m_w_out), (v_w_in, v_w_pw2, v_w_out),
            ("adamw_w_in", "adamw_w_pw2", "adamw_w_out"))]

    for l in reversed(range(1, depth)):
        updated[l] = update_layer(l)
        done = updated[l][0][1]

    gathered, = _exchange_wait(*small_in_flight, done, "small_grads_wait", gather=True)
    summed = _unpack(_sum_parts(gathered, "sum_small_grads"), small_shapes_full)
    loss = summed[-1][0]
    g_small = dict(zip(small_names, summed))
    g_small["conv_w"] = lax.dynamic_slice_in_dim(g_small["conv_w"], me * conv_w.shape[2], conv_w.shape[2], axis=2)
    g_small["meta"] = lax.dynamic_slice_in_dim(g_small["meta"], me * meta_tokens.shape[1], meta_tokens.shape[1], axis=1)
    small_w = dict(zip(small_names, (pre_norm_g, post_norm_g, conv_b, conv_ln_g, conv_ln_b, b_pw2, conv_w, meta_tokens)))
    small_m = (m_pre_norm_g, m_post_norm_g, m_conv_b, m_conv_ln_g, m_conv_ln_b, m_b_pw2, m_conv_w, m_meta_tokens)
    small_v = (v_pre_norm_g, v_post_norm_g, v_conv_b, v_conv_ln_g, v_conv_ln_b, v_b_pw2, v_conv_w, v_meta_tokens)
    small_shapes = [small_w[k].shape for k in small_names]
    outs = _sum_adamw(_pack([g_small[k] for k in small_names])[None], _pack([small_w[k] for k in small_names]),
                      _pack(small_m), _pack(small_v), "adamw_small_weights")
    g_s, d_s, nm_s, nv_s = [dict(zip(small_names, _unpack(o, small_shapes))) for o in outs]

    done = outs[1]
    updated[0] = update_layer(0)
    (g_w_in, d_w_in, nm_w_in, nv_w_in), (g_w_pw2, d_w_pw2, nm_w_pw2, nv_w_pw2), (g_w_out, d_w_out, nm_w_out, nv_w_out) = [
        [jnp.stack([updated[l][a][k] for l in range(depth)]) for k in range(4)] for a in range(3)]

    def ordered(s, w_in_, w_pw2_, w_out_):
        return (s["meta"], s["pre_g"], s["post_g"], w_in_, s["conv_w"], s["conv_b"], s["ln_g"], s["ln_b"], w_pw2_,
                s["b_pw2"], w_out_)

    return (loss, dx[None], *ordered(g_s, g_w_in, g_w_pw2, g_w_out), *ordered(d_s, d_w_in, d_w_pw2, d_w_out),
            *ordered(nm_s, nm_w_in, nm_w_pw2, nm_w_out), *ordered(nv_s, nv_w_in, nv_w_pw2, nv_w_out))
```

```python
import functools

import jax
import jax.numpy as jnp
from jax import lax
from jax.experimental import pallas as pl
from jax.experimental.pallas import tpu as pltpu

F32 = jnp.float32
BF16 = jnp.bfloat16

D_MODEL = 1024
D_CONV = 512
D_SB = 512
HEAD_DIM = 64
HEADS_PER_BLOCK = 4
HEAD_BLK = HEADS_PER_BLOCK * HEAD_DIM
CONV_WIDTH = 31
N_META = 16
D_IN = 3 * D_CONV + 4 * D_SB
RMS_EPS = 1e-6
LN_EPS = 1e-5
SB_SCALE = HEAD_DIM ** -0.5

ADAM_LR = 0.001
ADAM_B1 = 0.9
ADAM_B2 = 0.999
ADAM_EPS = 1e-08
ADAM_WD = 0.01
ADAM_STEP = 10

N_DEV = 8
LANES = 128
ROW_BLK = 256
DENSE_ROWS_MAX = 544
HALO = 32
VMEM_LIMIT = 56 * 1024 * 1024
MESH = pl.DeviceIdType.MESH


def _cparams(*sem):
    return pltpu.CompilerParams(dimension_semantics=sem, vmem_limit_bytes=VMEM_LIMIT)


def _rows(n_cols, col=0, rows=ROW_BLK):
    return pl.BlockSpec((rows, n_cols), lambda i, col=col: (i, col))


def _dense_rows(t):
    packed_rows = 16
    return max(d for d in range(packed_rows, min(t, DENSE_ROWS_MAX) + 1, packed_rows) if t % d == 0)


def _whole(shape):
    return pl.BlockSpec(shape, lambda i: (0,) * len(shape))


def _sigmoid(x):
    return jax.nn.sigmoid(x)


def _dsilu(x, s):
    return s * (1.0 + x * (1.0 - s))


def _dot(a, b):
    return jnp.dot(a, b, preferred_element_type=F32)


def _dot_nt(a, b):
    return lax.dot_general(a, b, (((1,), (1,)), ((), ())), preferred_element_type=F32)


def _dot_tn(a, b):
    return lax.dot_general(a, b, (((0,), (0,)), ((), ())), preferred_element_type=F32)


def _inproj_fwd(h, g_pre, w_in):
    t = h.shape[0]
    blk = _dense_rows(t)

    def body(h_ref, g_ref, w_ref, pc_ref, qkv_ref, sbg_ref, u_ref):
        x = h_ref[...]
        r = lax.rsqrt(jnp.mean(x * x, axis=-1, keepdims=True) + RMS_EPS)
        u = (x * r * g_ref[...]).astype(BF16)
        u_ref[...] = u
        pc_ref[...] = _dot(u, w_ref[:, 0:1536])
        qkv_ref[...] = _dot(u, w_ref[:, 1536:3072]).astype(BF16)
        sbg_ref[...] = _dot(u, w_ref[:, 3072:3584])

    return pl.pallas_call(
        body, name="inproj_fwd", grid=(t // blk,),
        in_specs=[_rows(D_MODEL, rows=blk), _whole((1, D_MODEL)), _whole((D_MODEL, D_IN))],
        out_specs=[_rows(1536, rows=blk), _rows(1536, rows=blk), _rows(D_SB, rows=blk), _rows(D_MODEL, rows=blk)],
        out_shape=[jax.ShapeDtypeStruct((t, 1536), F32), jax.ShapeDtypeStruct((t, 1536), BF16),
                   jax.ShapeDtypeStruct((t, D_SB), F32), jax.ShapeDtypeStruct((t, D_MODEL), BF16)],
        compiler_params=_cparams("parallel"),
    )(h, g_pre, w_in)


def _prev_halo(col):
    per = ROW_BLK // HALO
    return pl.BlockSpec((HALO, D_CONV), lambda i, col=col: (jnp.maximum(i * per - 1, 0), col))


def _fill_glu(buf, i, a_ref, b_ref, ha_ref, hb_ref):
    halo = ha_ref[...] * _sigmoid(hb_ref[...])
    buf[0:HALO, :] = jnp.where(i > 0, halo, 0.0)
    buf[HALO:HALO + ROW_BLK, :] = a_ref[...] * _sigmoid(b_ref[...])


SUBLANES = 8
TAP_ROWS = 64
SHIFT_ROWS = HALO + ROW_BLK - SUBLANES


def _fill_shifts(shifts, buf):
    for b in range(1, SUBLANES):
        shifts[b - 1] = buf[pl.ds(b, SHIFT_ROWS), :]


def _window(buf, shifts, first, rows, lanes):
    whole, part = divmod(first, SUBLANES)
    src = buf if part == 0 else shifts.at[part - 1]
    return src[pl.ds(rows.start + SUBLANES * whole, rows.size), lanes]


TAP_ROW_CHUNKS = [pl.ds(r, TAP_ROWS) for r in range(0, ROW_BLK, TAP_ROWS)]
TAP_LANE_TILES = [pl.ds(c, LANES) for c in range(0, D_CONV, LANES)]


def _layer_norm_stats(cv):
    mu = jnp.mean(cv, axis=-1, keepdims=True)
    xc = cv - mu
    rstd = lax.rsqrt(jnp.mean(xc * xc, axis=-1, keepdims=True) + LN_EPS)
    return xc * rstd, rstd


def _conv_fwd(pc, conv_w, conv_b, ln_g, ln_b, w_pw2, b_pw2):
    t = pc.shape[0]

    def body(a_ref, b_ref, gate_ref, ha_ref, hb_ref, cw_ref, cb_ref, lg_ref, lb_ref, wp_ref, bp_ref,
             cout_ref, cv_ref, p_ref, sl_ref, buf, shifts):
        i = pl.program_id(0)
        _fill_glu(buf, i, a_ref, b_ref, ha_ref, hb_ref)
        _fill_shifts(shifts, buf)
        for lanes in TAP_LANE_TILES:
            for rows in TAP_ROW_CHUNKS:
                acc = jnp.zeros((TAP_ROWS, LANES), F32) + cb_ref[:, lanes]
                for j in range(CONV_WIDTH):
                    acc = acc + cw_ref[j:j + 1, lanes] * _window(buf, shifts, HALO - (CONV_WIDTH - 1) + j, rows, lanes)
                cv_ref[rows, lanes] = acc
        xh, _ = _layer_norm_stats(cv_ref[...])
        ln = xh * lg_ref[...] + lb_ref[...]
        sl = (ln * _sigmoid(ln)).astype(BF16)
        sl_ref[...] = sl
        p = _dot(sl, wp_ref[...]) + bp_ref[...]
        p_ref[...] = p
        gate = gate_ref[...]
        cout_ref[...] = (p * (gate * _sigmoid(gate))).astype(BF16)

    vec = _whole((1, D_CONV))
    return pl.pallas_call(
        body, name="conv_fwd", grid=(t // ROW_BLK,),
        in_specs=[_rows(D_CONV, 0), _rows(D_CONV, 1), _rows(D_CONV, 2), _prev_halo(0), _prev_halo(1),
                  _whole((CONV_WIDTH, D_CONV)), vec, vec, vec, _whole((D_CONV, D_CONV)), vec],
        out_specs=[_rows(D_CONV)] * 4,
        out_shape=[jax.ShapeDtypeStruct((t, D_CONV), BF16), jax.ShapeDtypeStruct((t, D_CONV), F32),
                   jax.ShapeDtypeStruct((t, D_CONV), F32), jax.ShapeDtypeStruct((t, D_CONV), BF16)],
        scratch_shapes=[pltpu.VMEM((HALO + ROW_BLK, D_CONV), F32), pltpu.VMEM((SUBLANES - 1, SHIFT_ROWS, D_CONV), F32)],
        compiler_params=_cparams("parallel"),
    )(pc, pc, pc, pc, pc, conv_w, conv_b, ln_g, ln_b, w_pw2, b_pw2)


def _lower_triangle():
    row = lax.broadcasted_iota(jnp.int32, (ROW_BLK, ROW_BLK), 0)
    col = lax.broadcasted_iota(jnp.int32, (ROW_BLK, ROW_BLK), 1)
    return row > col


def _lower_triangle_t():
    row = lax.broadcasted_iota(jnp.int32, (ROW_BLK, ROW_BLK), 0)
    col = lax.broadcasted_iota(jnp.int32, (ROW_BLK, ROW_BLK), 1)
    return row < col


def _tri_sum(x, umat):
    return _dot(x.astype(BF16), umat)


def _log_gates(z):
    ls = -(jnp.maximum(z, 0.0) + jnp.log(1.0 + jnp.exp(-jnp.abs(z))))
    return ls, z + ls


def _head_lanes(hh):
    lane = lax.broadcasted_iota(jnp.int32, (ROW_BLK, HEAD_BLK), 1)
    return (lane >= HEAD_DIM * hh) & (lane < HEAD_DIM * (hh + 1))


def _merge_heads(acc_ref):
    out = acc_ref[HEADS_PER_BLOCK - 1]
    for hh in range(HEADS_PER_BLOCK - 1):
        out = jnp.where(_head_lanes(hh), acc_ref[hh], out)
    return out


def _qkv_specs(t):
    n_blk = D_SB // HEAD_BLK
    return [pl.BlockSpec((ROW_BLK, HEAD_BLK), lambda hp, i: (i, hp)),
            pl.BlockSpec((t, HEAD_BLK), lambda hp, i: (0, n_blk + hp)),
            pl.BlockSpec((t, HEAD_BLK), lambda hp, i: (0, 2 * n_blk + hp))]


def _carry_spec():
    return pl.BlockSpec((HEADS_PER_BLOCK, ROW_BLK, LANES), lambda hp, i: (hp, i, 0))


def _last_block_rows(t, n_tokens):
    packed_rows = 16
    return -(-(n_tokens - (t - ROW_BLK)) // packed_rows) * packed_rows


def _by_block_rows(i, last_rows, sweep):
    if last_rows == ROW_BLK:
        sweep(ROW_BLK)
        return
    last = pl.num_programs(1) - 1
    pl.when(i < last)(lambda: sweep(ROW_BLK))
    pl.when(i == last)(lambda: sweep(last_rows))


def _attn_fwd(qkv, n_tokens):
    t = qkv.shape[0]
    assert t // ROW_BLK <= LANES

    def body(q_ref, k_ref, v_ref, o_ref, c_ref, acc_ref, run_ref, qm_ref, z_ref):
        i = pl.program_id(1)
        q = q_ref[...]
        heads = range(HEADS_PER_BLOCK)
        for hh in heads:
            qm_ref[hh] = jnp.where(_head_lanes(hh), q, jnp.zeros_like(q)) * jnp.asarray(SB_SCALE, BF16)
        acc_ref[...] = jnp.zeros_like(acc_ref)
        c_ref[...] = jnp.zeros_like(c_ref)
        run_ref[...] = jnp.zeros_like(run_ref)

        def sweep(n_rows):
            rows = pl.ds(0, n_rows)
            lower = _lower_triangle()[:n_rows]
            umat = jnp.where(_lower_triangle(), 1.0, 0.0).astype(BF16)
            lane = lax.broadcasted_iota(jnp.int32, (n_rows, LANES), 1)

            def scores(jb):
                start = pl.multiple_of(jb * ROW_BLK, ROW_BLK)
                kb = k_ref[pl.ds(start, ROW_BLK), :]
                for hh in heads:
                    z_ref[hh, rows] = _dot_nt(qm_ref[hh, rows], kb)

            def block(jb, diagonal):
                start = pl.multiple_of(jb * ROW_BLK, ROW_BLK)
                vb = v_ref[pl.ds(start, ROW_BLK), :]
                logits = []
                for hh in heads:
                    ls, lb = _log_gates(z_ref[hh, rows])
                    if diagonal:
                        ls = jnp.where(lower, ls, 0.0)
                    run = run_ref[hh, rows]
                    if not diagonal:
                        c_ref[hh, rows] = jnp.where(lane == jb, run, c_ref[hh, rows])
                    logits.append(lb + jnp.concatenate([run, run], axis=1) + _tri_sum(ls, umat))
                    run_ref[hh, rows] = run + jnp.sum(ls, axis=1, keepdims=True)
                scores(jnp.maximum(jb - 1, 0))
                for hh in heads:
                    a = jnp.exp(logits[hh])
                    if diagonal:
                        a = jnp.where(lower, a, 0.0)
                    acc_ref[hh, rows] += _dot(a.astype(BF16), vb)

            scores(i)
            block(i, True)

            @pl.loop(0, i)
            def _(n):
                block(i - 1 - n, False)

        _by_block_rows(i, _last_block_rows(t, n_tokens), sweep)
        o_ref[...] = _merge_heads(acc_ref)

    per_head = (HEADS_PER_BLOCK, ROW_BLK, HEAD_BLK)
    return pl.pallas_call(
        body, name="attn_fwd", grid=(D_SB // HEAD_BLK, t // ROW_BLK),
        in_specs=_qkv_specs(t),
        out_specs=[pl.BlockSpec((ROW_BLK, HEAD_BLK), lambda hp, i: (i, hp)), _carry_spec()],
        out_shape=[jax.ShapeDtypeStruct((t, D_SB), F32),
                   jax.ShapeDtypeStruct((D_SB // HEAD_DIM, t, LANES), F32)],
        scratch_shapes=[pltpu.VMEM(per_head, F32), pltpu.VMEM((HEADS_PER_BLOCK, ROW_BLK, LANES), F32),
                        pltpu.VMEM(per_head, BF16), pltpu.VMEM((HEADS_PER_BLOCK, ROW_BLK, ROW_BLK), F32)],
        compiler_params=_cparams("arbitrary", "arbitrary"),
    )(qkv, qkv, qkv)


def _outproj_fwd(h, cout, sraw, sbg, w_out, g_post):
    t = h.shape[0]
    blk = _dense_rows(t)

    def body(h_ref, c_ref, s_ref, g_ref, w_ref, gp_ref, hn_ref, mixed_ref, mix_ref):
        gate = g_ref[...]
        mix_ref[:, 0:D_CONV] = c_ref[...]
        mix_ref[:, D_CONV:] = (s_ref[...] * (gate * _sigmoid(gate))).astype(BF16)
        mixed = _dot(mix_ref[...], w_ref[...])
        mixed_ref[...] = mixed
        r = lax.rsqrt(jnp.mean(mixed * mixed, axis=-1, keepdims=True) + RMS_EPS)
        hn_ref[...] = h_ref[...] + mixed * r * gp_ref[...]

    return pl.pallas_call(
        body, name="outproj_fwd", grid=(t // blk,),
        in_specs=[_rows(D_MODEL, rows=blk), _rows(D_CONV, rows=blk), _rows(D_SB, rows=blk), _rows(D_SB, rows=blk),
                  _whole((D_MODEL, D_MODEL)), _whole((1, D_MODEL))],
        out_specs=[_rows(D_MODEL, rows=blk)] * 3,
        out_shape=[jax.ShapeDtypeStruct((t, D_MODEL), F32), jax.ShapeDtypeStruct((t, D_MODEL), F32),
                   jax.ShapeDtypeStruct((t, D_MODEL), BF16)],
        compiler_params=_cparams("parallel"),
    )(h, cout, sraw, sbg, w_out, g_post)


def _loss_and_grad(h, target, seq):
    t = h.shape[0]
    blk = _dense_rows(t)

    def body(h_ref, t_ref, loss_ref, dh_ref):
        i = pl.program_id(0)

        @pl.when(i == 0)
        def _():
            loss_ref[...] = jnp.zeros_like(loss_ref)

        row = i * blk + lax.broadcasted_iota(jnp.int32, (blk, D_MODEL), 0)
        real = (row >= N_META) & (row < N_META + seq)
        diff = jnp.where(real, h_ref[...] - t_ref[...], 0.0)
        sq = jnp.sum(jnp.sum(diff * diff, axis=1, keepdims=True), axis=0, keepdims=True)
        loss_ref[...] += (0.5 / D_MODEL) * sq
        dh_ref[...] = diff * (1.0 / D_MODEL)

    return pl.pallas_call(
        body, name="loss", grid=(t // blk,),
        in_specs=[_rows(D_MODEL, rows=blk), _rows(D_MODEL, rows=blk)],
        out_specs=[_whole((1, 1)), _rows(D_MODEL, rows=blk)],
        out_shape=[jax.ShapeDtypeStruct((1, 1), F32), jax.ShapeDtypeStruct((t, D_MODEL), F32)],
        compiler_params=_cparams("arbitrary"),
    )(h, target)


def _outproj_bwd(dh, mixed, g_post, sraw, sbg, w_out_t):
    t = dh.shape[0]
    blk = _dense_rows(t)

    def body(dh_ref, mixed_ref, gp_ref, s_ref, g_ref, wt_ref, dc_ref, ds_ref, dg_ref, dmb_ref, dgp_ref):
        @pl.when(pl.program_id(0) == 0)
        def _():
            dgp_ref[...] = jnp.zeros_like(dgp_ref)

        mixed = mixed_ref[...]
        r = lax.rsqrt(jnp.mean(mixed * mixed, axis=-1, keepdims=True) + RMS_EPS)
        nh = mixed * r
        dy = dh_ref[...]
        dgp_ref[...] += jnp.sum(dy * nh, axis=0, keepdims=True)
        dn = dy * gp_ref[...]
        dmixed = (r * (dn - nh * jnp.mean(dn * nh, axis=-1, keepdims=True))).astype(BF16)
        dmb_ref[...] = dmixed
        dmix = _dot_nt(dmixed, wt_ref[...])
        dc_ref[...] = dmix[:, 0:D_CONV]
        dsg = dmix[:, D_CONV:]
        gate = g_ref[...]
        sg = _sigmoid(gate)
        ds_ref[...] = dsg * (gate * sg)
        dg_ref[...] = (dsg * s_ref[...] * _dsilu(gate, sg)).astype(BF16)

    return pl.pallas_call(
        body, name="outproj_bwd", grid=(t // blk,),
        in_specs=[_rows(D_MODEL, rows=blk), _rows(D_MODEL, rows=blk), _whole((1, D_MODEL)), _rows(D_SB, rows=blk),
                  _rows(D_SB, rows=blk), _whole((D_MODEL, D_MODEL))],
        out_specs=[_rows(D_CONV, rows=blk), _rows(D_SB, rows=blk), _rows(D_SB, rows=blk), _rows(D_MODEL, rows=blk),
                   _whole((1, D_MODEL))],
        out_shape=[jax.ShapeDtypeStruct((t, D_CONV), F32), jax.ShapeDtypeStruct((t, D_SB), F32),
                   jax.ShapeDtypeStruct((t, D_SB), BF16), jax.ShapeDtypeStruct((t, D_MODEL), BF16),
                   jax.ShapeDtypeStruct((1, D_MODEL), F32)],
        compiler_params=_cparams("arbitrary"),
    )(dh, mixed, g_post, sraw, sbg, w_out_t)


def _attn_bwd(qkv, carries, do, n_tokens):
    t = qkv.shape[0]

    def body(q_ref, k_ref, v_ref, c_ref, do_ref, dq_ref, dk_ref, dv_ref, acc_ref, seen_ref, qm_ref, dom_ref, z_ref,
             da_ref, dz_ref, a_ref):
        i = pl.program_id(1)

        @pl.when(i == 0)
        def _():
            dk_ref[...] = jnp.zeros_like(dk_ref)
            dv_ref[...] = jnp.zeros_like(dv_ref)

        q = q_ref[...]
        dof = do_ref[...]
        heads = range(HEADS_PER_BLOCK)
        for hh in heads:
            qm_ref[hh] = jnp.where(_head_lanes(hh), q, jnp.zeros_like(q)) * jnp.asarray(SB_SCALE, BF16)
            dom_ref[hh] = jnp.where(_head_lanes(hh), dof, 0.0).astype(BF16)
        acc_ref[...] = jnp.zeros_like(acc_ref)
        seen_ref[...] = jnp.zeros_like(seen_ref)

        def sweep(n_rows):
            rows = pl.ds(0, n_rows)
            lower = _lower_triangle()[:n_rows]
            umat = jnp.where(_lower_triangle(), 1.0, 0.0).astype(BF16)
            umat_t = jnp.where(_lower_triangle_t(), 1.0, 0.0).astype(BF16)
            lane = lax.broadcasted_iota(jnp.int32, (n_rows, LANES), 1)

            def scores(jb):
                start = pl.multiple_of(jb * ROW_BLK, ROW_BLK)
                kb = k_ref[pl.ds(start, ROW_BLK), :]
                for hh in heads:
                    z_ref[hh, rows] = _dot_nt(qm_ref[hh, rows], kb)

            def value_grads(jb):
                start = pl.multiple_of(jb * ROW_BLK, ROW_BLK)
                vb = v_ref[pl.ds(start, ROW_BLK), :]
                for hh in heads:
                    da_ref[hh, rows] = _dot_nt(dom_ref[hh, rows], vb)

            def products(jb, hh):
                start = pl.multiple_of(jb * ROW_BLK, ROW_BLK)
                dzb = dz_ref[hh, rows]
                acc_ref[hh, rows] += _dot(dzb, k_ref[pl.ds(start, ROW_BLK), :])
                dk_ref[pl.ds(start, ROW_BLK), :] += _dot_tn(dzb, qm_ref[hh, rows])
                dv_ref[pl.ds(start, ROW_BLK), :] += _dot_tn(a_ref[hh, rows], dom_ref[hh, rows])

            def block(jb, diagonal):
                before = jnp.maximum(jb - 1, 0)
                lbs, logits = [], []
                for hh in heads:
                    products(before, hh)
                    ls, lb = _log_gates(z_ref[hh, rows])
                    if diagonal:
                        ls = jnp.where(lower, ls, 0.0)
                        logits.append(lb + _tri_sum(ls, umat))
                    else:
                        right = jnp.sum(jnp.where(lane == jb, c_ref[hh, rows], 0.0), axis=1, keepdims=True)
                        logits.append(lb + right + _tri_sum(ls, umat))
                    lbs.append(lb)
                if not diagonal:
                    scores(jb + 1)
                gs, befores = [], []
                for hh in heads:
                    a = jnp.exp(logits[hh])
                    if diagonal:
                        a = jnp.where(lower, a, 0.0)
                    g = da_ref[hh, rows] * a
                    seen = seen_ref[hh, rows]
                    befores.append(jnp.concatenate([seen, seen], axis=1) + _tri_sum(g, umat_t))
                    seen_ref[hh, rows] = seen + jnp.sum(g, axis=1, keepdims=True)
                    a_ref[hh, rows] = a.astype(BF16)
                    gs.append(g)
                if not diagonal:
                    value_grads(jb + 1)
                for hh in heads:
                    dz = gs[hh] - jnp.exp(lbs[hh]) * (gs[hh] + befores[hh])
                    if diagonal:
                        dz = jnp.where(lower, dz, 0.0)
                    dz_ref[hh, rows] = dz.astype(BF16)

            dz_ref[...] = jnp.zeros_like(dz_ref)
            a_ref[...] = jnp.zeros_like(a_ref)
            scores(0)
            value_grads(0)

            @pl.loop(0, i)
            def _(jb):
                block(jb, False)

            block(i, True)
            for hh in heads:
                products(i, hh)

        _by_block_rows(i, _last_block_rows(t, n_tokens), sweep)
        dq_ref[...] = (_merge_heads(acc_ref) * SB_SCALE).astype(BF16)

    blk = pl.BlockSpec((ROW_BLK, HEAD_BLK), lambda hp, i: (i, hp))
    full = pl.BlockSpec((t, HEAD_BLK), lambda hp, i: (0, hp))
    per_head = (HEADS_PER_BLOCK, ROW_BLK, HEAD_BLK)
    return pl.pallas_call(
        body, name="attn_bwd", grid=(D_SB // HEAD_BLK, t // ROW_BLK),
        in_specs=_qkv_specs(t) + [_carry_spec(), blk],
        out_specs=[blk, full, full],
        out_shape=[jax.ShapeDtypeStruct((t, D_SB), BF16)] + [jax.ShapeDtypeStruct((t, D_SB), F32)] * 2,
        scratch_shapes=[pltpu.VMEM(per_head, F32), pltpu.VMEM((HEADS_PER_BLOCK, ROW_BLK, LANES), F32),
                        pltpu.VMEM(per_head, BF16), pltpu.VMEM(per_head, BF16)]
                       + [pltpu.VMEM((HEADS_PER_BLOCK, ROW_BLK, ROW_BLK), dtype) for dtype in (F32, F32, BF16, BF16)],
        compiler_params=_cparams("arbitrary", "arbitrary"),
    )(qkv, qkv, qkv, carries, do)


def _conv_bwd_rows(dcout, pc, cv, p, ln_g, ln_b, w_pw2_t):
    t = dcout.shape[0]
    blk = _dense_rows(t)

    def body(dc_ref, gate_ref, cv_ref, p_ref, lg_ref, lb_ref, wt_ref, dcv_ref, dgate_ref, dpb_ref, vec_ref):
        @pl.when(pl.program_id(0) == 0)
        def _():
            vec_ref[...] = jnp.zeros_like(vec_ref)

        dc = dc_ref[...]
        gate = gate_ref[...]
        sg = _sigmoid(gate)
        dp = dc * (gate * sg)
        dgate_ref[...] = (dc * p_ref[...] * _dsilu(gate, sg)).astype(BF16)
        dpb = dp.astype(BF16)
        dpb_ref[...] = dpb
        xh, rstd = _layer_norm_stats(cv_ref[...])
        ln = xh * lg_ref[...] + lb_ref[...]
        s2 = _sigmoid(ln)
        dln = _dot_nt(dpb, wt_ref[...]) * _dsilu(ln, s2)
        dxh = dln * lg_ref[...]
        dcv = rstd * (dxh - jnp.mean(dxh, axis=-1, keepdims=True)
                      - xh * jnp.mean(dxh * xh, axis=-1, keepdims=True))
        dcv_ref[...] = dcv
        vec_ref[0:1, :] += jnp.sum(dp, axis=0, keepdims=True)
        vec_ref[1:2, :] += jnp.sum(dln * xh, axis=0, keepdims=True)
        vec_ref[2:3, :] += jnp.sum(dln, axis=0, keepdims=True)
        vec_ref[3:4, :] += jnp.sum(dcv, axis=0, keepdims=True)

    vec = _whole((1, D_CONV))
    return pl.pallas_call(
        body, name="conv_bwd_rows", grid=(t // blk,),
        in_specs=[_rows(D_CONV, rows=blk), _rows(D_CONV, 2, rows=blk), _rows(D_CONV, rows=blk), _rows(D_CONV, rows=blk),
                  vec, vec, _whole((D_CONV, D_CONV))],
        out_specs=[_rows(D_CONV, rows=blk), _rows(D_CONV, rows=blk), _rows(D_CONV, rows=blk), _whole((8, D_CONV))],
        out_shape=[jax.ShapeDtypeStruct((t, D_CONV), F32), jax.ShapeDtypeStruct((t, D_CONV), BF16),
                   jax.ShapeDtypeStruct((t, D_CONV), BF16), jax.ShapeDtypeStruct((8, D_CONV), F32)],
        compiler_params=_cparams("arbitrary"),
    )(dcout, pc, cv, p, ln_g, ln_b, w_pw2_t)


def _conv_bwd_taps(dcv, pc, conv_w):
    t = dcv.shape[0]
    n_halo = t // HALO
    per = ROW_BLK // HALO

    def body(d_ref, dn_ref, a_ref, b_ref, ha_ref, hb_ref, cw_ref, da_ref, db_ref, dw_ref, cbuf, dbuf, cshifts, dshifts):
        i = pl.program_id(0)

        @pl.when(i == 0)
        def _():
            dw_ref[...] = jnp.zeros_like(dw_ref)

        _fill_glu(cbuf, i, a_ref, b_ref, ha_ref, hb_ref)
        dbuf[0:ROW_BLK, :] = d_ref[...]
        dbuf[ROW_BLK:ROW_BLK + HALO, :] = jnp.where(i < pl.num_programs(0) - 1, dn_ref[...], 0.0)
        _fill_shifts(cshifts, cbuf)
        _fill_shifts(dshifts, dbuf)
        for lanes in TAP_LANE_TILES:
            for rows in TAP_ROW_CHUNKS:
                acc = jnp.zeros((TAP_ROWS, LANES), F32)
                for j in range(CONV_WIDTH):
                    acc = acc + cw_ref[j:j + 1, lanes] * _window(dbuf, dshifts, CONV_WIDTH - 1 - j, rows, lanes)
                sb = _sigmoid(b_ref[rows, lanes])
                da_ref[rows, lanes] = (acc * sb).astype(BF16)
                db_ref[rows, lanes] = (acc * a_ref[rows, lanes] * sb * (1.0 - sb)).astype(BF16)
            for j in range(CONV_WIDTH):
                acc = jnp.zeros((TAP_ROWS, LANES), F32)
                for rows in TAP_ROW_CHUNKS:
                    acc = acc + d_ref[rows, lanes] * _window(cbuf, cshifts, HALO - (CONV_WIDTH - 1) + j, rows, lanes)
                dw_ref[j:j + 1, lanes] += jnp.sum(acc, axis=0, keepdims=True)

    return pl.pallas_call(
        body, name="conv_bwd_taps", grid=(t // ROW_BLK,),
        in_specs=[_rows(D_CONV),
                  pl.BlockSpec((HALO, D_CONV), lambda i: (jnp.minimum((i + 1) * per, n_halo - 1), 0)),
                  _rows(D_CONV, 0), _rows(D_CONV, 1), _prev_halo(0), _prev_halo(1),
                  _whole((CONV_WIDTH, D_CONV))],
        out_specs=[_rows(D_CONV), _rows(D_CONV), _whole((32, D_CONV))],
        out_shape=[jax.ShapeDtypeStruct((t, D_CONV), BF16), jax.ShapeDtypeStruct((t, D_CONV), BF16),
                   jax.ShapeDtypeStruct((32, D_CONV), F32)],
        scratch_shapes=[pltpu.VMEM((HALO + ROW_BLK, D_CONV), F32), pltpu.VMEM((ROW_BLK + HALO, D_CONV), F32),
                        pltpu.VMEM((SUBLANES - 1, SHIFT_ROWS, D_CONV), F32),
                        pltpu.VMEM((SUBLANES - 1, SHIFT_ROWS, D_CONV), F32)],
        compiler_params=_cparams("arbitrary"),
    )(dcv, dcv, pc, pc, pc, pc, conv_w)


def _inproj_bwd(dh_out, h, g_pre, pieces, w_in_t):
    t = h.shape[0]
    blk = _dense_rows(t)

    def body(dh_ref, h_ref, g_ref, *rest):
        piece_refs, (wt_ref, dhin_ref, dproj_ref, dg_ref) = rest[:7], rest[7:]

        @pl.when(pl.program_id(0) == 0)
        def _():
            dg_ref[...] = jnp.zeros_like(dg_ref)

        for k, ref in enumerate(piece_refs):
            dproj_ref[:, 512 * k:512 * (k + 1)] = ref[...].astype(BF16)
        du = _dot_nt(dproj_ref[...], wt_ref[...])
        x = h_ref[...]
        r = lax.rsqrt(jnp.mean(x * x, axis=-1, keepdims=True) + RMS_EPS)
        xh = x * r
        dg_ref[...] += jnp.sum(du * xh, axis=0, keepdims=True)
        dxh = du * g_ref[...]
        dhin_ref[...] = dh_ref[...] + r * (dxh - xh * jnp.mean(dxh * xh, axis=-1, keepdims=True))

    return pl.pallas_call(
        body, name="inproj_bwd", grid=(t // blk,),
        in_specs=[_rows(D_MODEL, rows=blk), _rows(D_MODEL, rows=blk), _whole((1, D_MODEL))] + [_rows(512, rows=blk)] * 7
                 + [_whole((D_MODEL, D_IN))],
        out_specs=[_rows(D_MODEL, rows=blk), _rows(D_IN, rows=blk), _whole((1, D_MODEL))],
        out_shape=[jax.ShapeDtypeStruct((t, D_MODEL), F32), jax.ShapeDtypeStruct((t, D_IN), BF16),
                   jax.ShapeDtypeStruct((1, D_MODEL), F32)],
        compiler_params=_cparams("arbitrary"),
    )(dh_out, h, g_pre, *pieces, w_in_t)


def _weight_grad(xb, dyb, name):
    t, k = xb.shape
    n = dyb.shape[1]

    def body(x_ref, dy_ref, o_ref, acc_ref):
        i = pl.program_id(0)

        @pl.when(i == 0)
        def _():
            acc_ref[...] = jnp.zeros_like(acc_ref)

        acc_ref[...] += _dot_tn(x_ref[...], dy_ref[...])

        @pl.when(i == pl.num_programs(0) - 1)
        def _():
            o_ref[...] = acc_ref[...].astype(BF16)

    return pl.pallas_call(
        body, name=name, grid=(t // ROW_BLK,),
        in_specs=[_rows(k), _rows(n)], out_specs=_whole((k, n)), out_shape=jax.ShapeDtypeStruct((k, n), BF16),
        scratch_shapes=[pltpu.VMEM((k, n), F32)],
        compiler_params=_cparams("arbitrary"),
    )(xb, dyb)


def _position():
    return lax.axis_index("x"), lax.axis_index("y"), lax.axis_index("c")


def _comm_call(body, name, ins, out_shapes):
    n = len(ins)
    hbm = pl.BlockSpec(memory_space=pltpu.HBM)
    return pl.pallas_call(
        functools.partial(body, n), name=name, in_specs=[hbm] * n, out_specs=[hbm] * n, out_shape=out_shapes,
        scratch_shapes=[pltpu.SemaphoreType.DMA((n, N_DEV - 1)), pltpu.SemaphoreType.DMA((n, N_DEV - 1)),
                        pltpu.SemaphoreType.DMA((n,))],
    )(*ins)


def _all_gather(blocks, name):
    def body(n, *refs):
        x_refs, out_refs, (send_sems, recv_sems, local_sems) = refs[:n], refs[n:2 * n], refs[2 * n:]
        x, y, c = _position()
        me, sibling = (x, y, c), (x, y, 1 - c)
        chips = [(1 - x, y), (x, 1 - y), (1 - x, 1 - y)]

        def slot(a, px, py, pc):
            return out_refs[a].at[4 * px + 2 * py + pc]

        def copy(a, k, origin, to, own=False):
            return pltpu.make_async_remote_copy(
                src_ref=x_refs[a] if own else slot(a, *origin), dst_ref=slot(a, *origin),
                send_sem=send_sems.at[a, k], recv_sem=recv_sems.at[a, k], device_id=to, device_id_type=MESH)

        arrays = range(n)
        mine = [pltpu.make_async_copy(x_refs[a], slot(a, *me), local_sems.at[a]) for a in arrays]
        first = [copy(a, 1 + j, me, (*chip, c), own=True) for j, chip in enumerate(chips) for a in arrays]
        first += [copy(a, 0, me, sibling, own=True) for a in arrays]
        for cp in mine + first:
            cp.start()
        passed = []
        for j, chip in enumerate(chips):
            for a in arrays:
                copy(a, 1 + j, (*chip, c), me).wait_recv()
                passed.append(copy(a, 4 + j, (*chip, c), sibling))
                passed[-1].start()
        for a in arrays:
            copy(a, 0, sibling, me).wait_recv()
            for j, chip in enumerate(chips):
                copy(a, 4 + j, (*chip, 1 - c), me).wait_recv()
        for cp in first + passed:
            cp.wait_send()
        for cp in mine:
            cp.wait()

    return _comm_call(body, name, blocks, [jax.ShapeDtypeStruct((N_DEV,) + b.shape, b.dtype) for b in blocks])


def _exchange_copies(g_refs, land_refs, sems, gather):
    x, y, c = _position()
    me = 4 * x + 2 * y + c
    out = []
    for g_ref, land_ref, (send_sem, recv_sem, local_sem) in zip(g_refs, land_refs, sems):
        def mine(slot, g_ref=g_ref):
            return g_ref if gather else g_ref.at[slot]

        def remote(src, dst, dev):
            return pltpu.make_async_remote_copy(src_ref=src, dst_ref=dst, send_sem=send_sem, recv_sem=recv_sem,
                                                device_id=dev, device_id_type=MESH)

        sends = []
        for k in range(1, N_DEV):
            px = 1 - x if k & 4 else x
            py = 1 - y if k & 2 else y
            pc = 1 - c if k & 1 else c
            sends.append(remote(mine(4 * px + 2 * py + pc), land_ref.at[me], (px, py, pc)))
        seven = land_ref.at[pl.ds(0, N_DEV - 1)]
        out.append((pltpu.make_async_copy(mine(me), land_ref.at[me], local_sem), sends, remote(seven, seven, (x, y, c))))
    return out


_HBM = pl.BlockSpec(memory_space=pltpu.HBM)
_SEM = pl.BlockSpec(memory_space=pltpu.SEMAPHORE)
_ORDERED = pltpu.CompilerParams(has_side_effects=pltpu.SideEffectType.DATAFLOW_SIDE_EFFECTING)
SEMS_PER_ARRAY = 3


def _exchange_start(arrays, after, name, gather):
    n = len(arrays)
    n_sems = SEMS_PER_ARRAY * n

    def body(*refs):
        g_refs, land_refs, sems, token = refs[:n], refs[n:2 * n], refs[2 * n + 1:2 * n + 1 + n_sems], refs[-1]
        sems = [sems[SEMS_PER_ARRAY * a:SEMS_PER_ARRAY * (a + 1)] for a in range(n)]
        for local, sends, _ in _exchange_copies(g_refs, land_refs, sems, gather):
            local.start()
            for cp in sends:
                cp.start()
        token[...] = jnp.zeros_like(token)

    buffers = list(arrays) + [lax.empty((N_DEV,) + g.shape if gather else g.shape, g.dtype) for g in arrays]
    outs = pl.pallas_call(
        body, name=name, in_specs=[_HBM] * (2 * n) + [pl.BlockSpec(memory_space=pl.ANY)],
        out_specs=[_SEM] * n_sems + [_HBM] * (2 * n) + [pl.BlockSpec(memory_space=pltpu.VMEM)],
        out_shape=[pltpu.SemaphoreType.DMA(())] * n_sems + [pltpu.HBM(b.shape, b.dtype) for b in buffers]
                  + [jax.ShapeDtypeStruct((8, LANES), F32)],
        input_output_aliases={a: n_sems + a for a in range(2 * n)}, compiler_params=_ORDERED,
    )(*[pltpu.with_memory_space_constraint(b, pltpu.HBM) for b in buffers], after)
    return outs[:n_sems], outs[n_sems:n_sems + n], outs[n_sems + n:n_sems + 2 * n], outs[-1]


def _exchange_wait(sems, arrays, landings, after, name, gather):
    n = len(arrays)
    n_sems = SEMS_PER_ARRAY * n

    def body(*refs):
        g_refs, land_refs, sems = refs[:n], refs[n:2 * n], refs[2 * n:2 * n + n_sems]
        sems = [sems[SEMS_PER_ARRAY * a:SEMS_PER_ARRAY * (a + 1)] for a in range(n)]
        for local, _, all_seven in _exchange_copies(g_refs, land_refs, sems, gather):
            all_seven.wait_recv()
            all_seven.wait_send()
            local.wait()

    buffers = list(arrays) + list(landings)
    outs = pl.pallas_call(
        body, name=name, in_specs=[_HBM] * (2 * n) + [_SEM] * n_sems + [pl.BlockSpec(memory_space=pl.ANY)],
        out_specs=[_HBM] * (2 * n), out_shape=[pltpu.HBM(b.shape, b.dtype) for b in buffers],
        input_output_aliases={a: a for a in range(2 * n)}, compiler_params=_ORDERED,
    )(*buffers, *sems, after)
    return outs[n:]


def _block_rows(r, row_bytes, budget=1 << 20):
    cap = max(8, budget // row_bytes)
    return max(d for d in range(8, min(r, cap) + 1, 8) if r % d == 0)


def _sum_adamw(parts, w, m, v, name):
    n_parts, r, c = parts.shape
    br = _block_rows(r, 4 * c)

    def body(p_ref, w_ref, m_ref, v_ref, g_out, d_out, m_out, v_out):
        g = p_ref[0].astype(F32)
        for s in range(1, n_parts):
            g = g + p_ref[s].astype(F32)
        m_new = ADAM_B1 * m_ref[...] + (1.0 - ADAM_B1) * g
        v_new = ADAM_B2 * v_ref[...] + (1.0 - ADAM_B2) * (g * g)
        m_hat = m_new / (1.0 - ADAM_B1 ** ADAM_STEP)
        v_hat = v_new / (1.0 - ADAM_B2 ** ADAM_STEP)
        g_out[...] = g
        d_out[...] = -ADAM_LR * (m_hat / (jnp.sqrt(v_hat) + ADAM_EPS) + ADAM_WD * w_ref[...])
        m_out[...] = m_new
        v_out[...] = v_new

    row = pl.BlockSpec((br, c), lambda i: (i, 0))
    return pl.pallas_call(
        body, name=name, grid=(r // br,),
        in_specs=[pl.BlockSpec((n_parts, br, c), lambda i: (0, i, 0)), row, row, row],
        out_specs=[row] * 4, out_shape=[jax.ShapeDtypeStruct((r, c), F32)] * 4,
        compiler_params=_cparams("parallel"),
    )(parts, w, m, v)


def _sum_parts(parts, name):
    n_parts, r, c = parts.shape

    def body(p_ref, o_ref):
        g = p_ref[0]
        for s in range(1, n_parts):
            g = g + p_ref[s]
        o_ref[...] = g

    return pl.pallas_call(
        body, name=name, in_specs=[pl.BlockSpec(memory_space=pltpu.VMEM)],
        out_specs=pl.BlockSpec(memory_space=pltpu.VMEM), out_shape=jax.ShapeDtypeStruct((r, c), F32),
    )(parts)


def _pack(arrays):
    flat = jnp.concatenate([a.reshape(-1) for a in arrays])
    pad = -flat.shape[0] % (8 * LANES)
    if pad:
        flat = jnp.pad(flat, (0, pad))
    return flat.reshape(-1, LANES)


def _unpack(buf, shapes):
    flat = buf.reshape(-1)
    out, at = [], 0
    for shape in shapes:
        size = 1
        for d in shape:
            size *= d
        out.append(lax.slice_in_dim(flat, at, at + size).reshape(shape))
        at += size
    return out


def _local_step(x, target, meta, pre_g, post_g, conv_w, conv_b, ln_g, ln_b, b_pw2, weights, ship, ship_small):
    depth = pre_g.shape[0]
    seq = x.shape[0]
    t = -(-(N_META + seq) // ROW_BLK) * ROW_BLK
    tail = t - N_META - seq
    h = jnp.concatenate([meta, x, jnp.zeros((tail, D_MODEL), F32)], axis=0)
    target = jnp.pad(target, ((N_META, tail), (0, 0)))
    row = lambda a, l: a[l][None, :]

    saved = []
    for l in range(depth):
        w_in, w_pw2, w_out = weights(l, h)
        pc, qkv, sbg, u = _inproj_fwd(h, row(pre_g, l), w_in)
        cout, cv, p, sl = _conv_fwd(pc, conv_w[l], row(conv_b, l), row(ln_g, l), row(ln_b, l), w_pw2, row(b_pw2, l))
        sraw, carries = _attn_fwd(qkv, N_META + seq)
        h_new, mixed, mix = _outproj_fwd(h, cout, sraw, sbg, w_out, row(post_g, l))
        saved.append((h, pc, qkv, sbg, u, cv, p, sl, sraw, carries, mixed, mix, w_in, w_pw2, w_out))
        h = h_new

    loss, dh = _loss_and_grad(h, target, seq)

    grads = {k: [None] * depth for k in ("pre_g", "post_g", "conv_w", "conv_b", "ln_g", "ln_b", "b_pw2")}
    token = jnp.zeros((8, LANES), F32)
    for l in reversed(range(depth)):
        h_in, pc, qkv, sbg, u, cv, p, sl, sraw, carries, mixed, mix, w_in_t, w_pw2_t, w_out_t = saved[l]
        dcout, dsraw, dsbg, dmixed, dg_post = _outproj_bwd(dh, mixed, row(post_g, l) + token[:1, :1], sraw, sbg, w_out_t)
        dq, dk, dv = _attn_bwd(qkv, carries, dsraw, N_META + seq)
        dcv, dgate, dpb, vecs = _conv_bwd_rows(dcout, pc, cv, p, row(ln_g, l), row(ln_b, l), w_pw2_t)
        late = {"w_pw2": (sl, dpb), "w_out": (mix, dmixed)}
        taps = conv_w[l]
        if l == 0:
            token = ship(l, dcv, {k: _weight_grad(*late.pop(k), k + "_grad") for k in ("w_pw2", "w_out")})
            taps = taps + token[:1, :1]
        da, db, dconv_w = _conv_bwd_taps(dcv, pc, taps)
        dh, dproj, dg_pre = _inproj_bwd(dh, h_in, row(pre_g, l), (da, db, dgate, dq, dk, dv, dsbg), w_in_t)
        late["w_in"] = (u, dproj)
        grads["pre_g"][l] = dg_pre[0]
        grads["post_g"][l] = dg_post[0]
        grads["b_pw2"][l], grads["ln_g"][l], grads["ln_b"][l], grads["conv_b"][l] = vecs[0], vecs[1], vecs[2], vecs[3]
        grads["conv_w"][l] = dconv_w[:CONV_WIDTH]
        after = dh
        if l == 0:
            grads = {k: jnp.stack(v) for k, v in grads.items()}
            grads["meta"] = dh[:N_META]
            after = ship_small(grads, loss[0, 0])
        token = ship(l, after, {k: _weight_grad(*late[k], k + "_grad") for k in ("w_in", "w_pw2", "w_out") if k in late})

    return dh[N_META:N_META + seq], token


def _shard_major(full, axis):
    shape = full.shape
    split = full.reshape(shape[:axis] + (N_DEV, shape[axis] // N_DEV) + shape[axis + 1:])
    return jnp.moveaxis(split, axis, 0)


def _whole_from_shards(shards, axis):
    moved = jnp.moveaxis(shards, 0, axis)
    shape = moved.shape
    return moved.reshape(shape[:axis] + (shape[axis] * shape[axis + 1],) + shape[axis + 2:])


def kernel(x, meta_tokens, pre_norm_g, post_norm_g, w_in, conv_w, conv_b, conv_ln_g, conv_ln_b, w_pw2, b_pw2, w_out, loss_target, m_meta_tokens, m_pre_norm_g, m_post_norm_g, m_w_in, m_conv_w, m_conv_b, m_conv_ln_g, m_conv_ln_b, m_w_pw2, m_b_pw2, m_w_out, v_meta_tokens, v_pre_norm_g, v_post_norm_g, v_w_in, v_conv_w, v_conv_b, v_conv_ln_g, v_conv_ln_b, v_w_pw2, v_b_pw2, v_w_out):
    me = 4 * lax.axis_index("x") + 2 * lax.axis_index("y") + lax.axis_index("c")

    depth = w_in.shape[0]
    big = [w.astype(BF16) for w in (w_in, w_pw2, w_out)]
    *first, conv_w_s, meta_s = _all_gather([w[0] for w in big] + [conv_w, meta_tokens], "gather_first_layer")
    *gathering, token = _exchange_start([w[l] for l in range(1, depth) for w in big], meta_s, "gather_start", gather=True)
    conv_w_full = _whole_from_shards(conv_w_s, 2)
    meta_full = _whole_from_shards(meta_s, 1)
    shard_axis = (1, 0, 0)
    later = []

    def weights(l, h):
        if l == 0:
            return [_whole_from_shards(s, axis) for s, axis in zip(first, shard_axis)]
        if not later:
            later.extend(_exchange_wait(*gathering, h, "gather_wait", gather=True))
        return [_whole_from_shards(s, axis) for s, axis in zip(later[len(big) * (l - 1):len(big) * l], shard_axis)]

    in_flight = [[] for _ in range(depth)]
    grad_axis = dict(zip(("w_in", "w_pw2", "w_out"), shard_axis))

    def ship(l, after, dws):
        slabs = [_shard_major(dw, grad_axis[k]) for k, dw in dws.items()]
        sems, arrays, landings, token = _exchange_start(slabs, after, f"exchange_start_{l}_{len(in_flight[l])}",
                                                        gather=False)
        in_flight[l].append((list(dws), sems, arrays, landings))
        return token

    small_names = ("pre_g", "post_g", "conv_b", "ln_g", "ln_b", "b_pw2", "conv_w", "meta")
    small_in_flight, small_shapes_full = [], []

    def ship_small(grads, loss):
        small_full = [grads[k] for k in small_names] + [loss.reshape(1)]
        small_shapes_full.extend(a.shape for a in small_full)
        *in_flight_now, token = _exchange_start([_pack(small_full)], grads["meta"], "small_grads_start", gather=True)
        small_in_flight.extend(in_flight_now)
        return token

    dx, shipped = _local_step(x[0], loss_target[0], meta_full, pre_norm_g + token[:1, :1], post_norm_g, conv_w_full,
                              conv_b, conv_ln_g, conv_ln_b, b_pw2, weights, ship, ship_small)

    updated = [None] * depth
    done = shipped

    def update_layer(l):
        landed = {}
        for k, (names, sems, arrays, landings) in enumerate(in_flight[l]):
            landed.update(zip(names, _exchange_wait(sems, arrays, landings, done, f"exchange_wait_{l}_{k}", gather=False)))
        return [_sum_adamw(landed[name], w[l], m[l], v[l], "adamw_" + name) for name, w, m, v in (
            ("w_in", w_in, m_w_in, v_w_in), ("w_pw2", w_pw2, m_w_pw2, v_w_pw2), ("w_out", w_out, m_w_out, v_w_out))]

    for l in reversed(range(1, depth)):
        updated[l] = update_layer(l)
        done = updated[l][0][1]

    gathered, = _exchange_wait(*small_in_flight, done, "small_grads_wait", gather=True)
    summed = _unpack(_sum_parts(gathered, "sum_small_grads"), small_shapes_full)
    loss = summed[-1][0]
    g_small = dict(zip(small_names, summed))
    g_small["conv_w"] = lax.dynamic_slice_in_dim(g_small["conv_w"], me * conv_w.shape[2], conv_w.shape[2], axis=2)
    g_small["meta"] = lax.dynamic_slice_in_dim(g_small["meta"], me * meta_tokens.shape[1], meta_tokens.shape[1], axis=1)
    small_w = dict(zip(small_names, (pre_norm_g, post_norm_g, conv_b, conv_ln_g, conv_ln_b, b_pw2, conv_w, meta_tokens)))
    small_m = (m_pre_norm_g, m_post_norm_g, m_conv_b, m_conv_ln_g, m_conv_ln_b, m_b_pw2, m_conv_w, m_meta_tokens)
    small_v = (v_pre_norm_g, v_post_norm_g, v_conv_b, v_conv_ln_g, v_conv_ln_b, v_b_pw2, v_conv_w, v_meta_tokens)
    small_shapes = [small_w[k].shape for k in small_names]
    outs = _sum_adamw(_pack([g_small[k] for k in small_names])[None], _pack([small_w[k] for k in small_names]),
                      _pack(small_m), _pack(small_v), "adamw_small_weights")
    g_s, d_s, nm_s, nv_s = [dict(zip(small_names, _unpack(o, small_shapes))) for o in outs]

    done = outs[1]
    updated[0] = update_layer(0)
    (g_w_in, d_w_in, nm_w_in, nv_w_in), (g_w_pw2, d_w_pw2, nm_w_pw2, nv_w_pw2), (g_w_out, d_w_out, nm_w_out, nv_w_out) = [
        [jnp.stack([updated[l][a][k] for l in range(depth)]) for k in range(4)] for a in range(3)]

    def ordered(s, w_in_, w_pw2_, w_out_):
        return (s["meta"], s["pre_g"], s["post_g"], w_in_, s["conv_w"], s["conv_b"], s["ln_g"], s["ln_b"], w_pw2_,
                s["b_pw2"], w_out_)

    return (loss, dx[None], *ordered(g_s, g_w_in, g_w_pw2, g_w_out), *ordered(d_s, d_w_in, d_w_pw2, d_w_out),
            *ordered(nm_s, nm_w_in, nm_w_pw2, nm_w_out), *ordered(nv_s, nv_w_in, nv_w_pw2, nv_w_out))
```

```python
import functools

import jax
import jax.numpy as jnp
from jax import lax
from jax.experimental import pallas as pl
from jax.experimental.pallas import tpu as pltpu

F32 = jnp.float32
BF16 = jnp.bfloat16

D_MODEL = 1024
D_CONV = 512
D_SB = 512
HEAD_DIM = 64
HEADS_PER_BLOCK = 4
HEAD_BLK = HEADS_PER_BLOCK * HEAD_DIM
CONV_WIDTH = 31
N_META = 16
D_IN = 3 * D_CONV + 4 * D_SB
RMS_EPS = 1e-6
LN_EPS = 1e-5
SB_SCALE = HEAD_DIM ** -0.5

ADAM_LR = 0.001
ADAM_B1 = 0.9
ADAM_B2 = 0.999
ADAM_EPS = 1e-08
ADAM_WD = 0.01
ADAM_STEP = 10

N_DEV = 8
LANES = 128
ROW_BLK = 256
DENSE_ROWS_MAX = 544
HALO = 32
VMEM_LIMIT = 56 * 1024 * 1024
MESH = pl.DeviceIdType.MESH


def _cparams(*sem):
    return pltpu.CompilerParams(dimension_semantics=sem, vmem_limit_bytes=VMEM_LIMIT)


def _rows(n_cols, col=0, rows=ROW_BLK):
    return pl.BlockSpec((rows, n_cols), lambda i, col=col: (i, col))


def _dense_rows(t):
    packed_rows = 16
    return max(d for d in range(packed_rows, min(t, DENSE_ROWS_MAX) + 1, packed_rows) if t % d == 0)


def _whole(shape):
    return pl.BlockSpec(shape, lambda i: (0,) * len(shape))


def _sigmoid(x):
    return jax.nn.sigmoid(x)


def _dsilu(x, s):
    return s * (1.0 + x * (1.0 - s))


def _dot(a, b):
    return jnp.dot(a, b, preferred_element_type=F32)


def _dot_nt(a, b):
    return lax.dot_general(a, b, (((1,), (1,)), ((), ())), preferred_element_type=F32)


def _dot_tn(a, b):
    return lax.dot_general(a, b, (((0,), (0,)), ((), ())), preferred_element_type=F32)


def _inproj_fwd(h, g_pre, w_in):
    t = h.shape[0]
    blk = _dense_rows(t)

    def body(h_ref, g_ref, w_ref, pc_ref, qkv_ref, sbg_ref, u_ref):
        x = h_ref[...]
        r = lax.rsqrt(jnp.mean(x * x, axis=-1, keepdims=True) + RMS_EPS)
        u = (x * r * g_ref[...]).astype(BF16)
        u_ref[...] = u
        pc_ref[...] = _dot(u, w_ref[:, 0:1536])
        qkv_ref[...] = _dot(u, w_ref[:, 1536:3072]).astype(BF16)
        sbg_ref[...] = _dot(u, w_ref[:, 3072:3584])

    return pl.pallas_call(
        body, name="inproj_fwd", grid=(t // blk,),
        in_specs=[_rows(D_MODEL, rows=blk), _whole((1, D_MODEL)), _whole((D_MODEL, D_IN))],
        out_specs=[_rows(1536, rows=blk), _rows(1536, rows=blk), _rows(D_SB, rows=blk), _rows(D_MODEL, rows=blk)],
        out_shape=[jax.ShapeDtypeStruct((t, 1536), F32), jax.ShapeDtypeStruct((t, 1536), BF16),
                   jax.ShapeDtypeStruct((t, D_SB), F32), jax.ShapeDtypeStruct((t, D_MODEL), BF16)],
        compiler_params=_cparams("parallel"),
    )(h, g_pre, w_in)


def _prev_halo(col):
    per = ROW_BLK // HALO
    return pl.BlockSpec((HALO, D_CONV), lambda i, col=col: (jnp.maximum(i * per - 1, 0), col))


def _fill_glu(buf, i, a_ref, b_ref, ha_ref, hb_ref):
    halo = ha_ref[...] * _sigmoid(hb_ref[...])
    buf[0:HALO, :] = jnp.where(i > 0, halo, 0.0)
    buf[HALO:HALO + ROW_BLK, :] = a_ref[...] * _sigmoid(b_ref[...])


SUBLANES = 8
TAP_ROWS = 64
SHIFT_ROWS = HALO + ROW_BLK - SUBLANES


def _fill_shifts(shifts, buf):
    for b in range(1, SUBLANES):
        shifts[b - 1] = buf[pl.ds(b, SHIFT_ROWS), :]


def _window(buf, shifts, first, rows, lanes):
    whole, part = divmod(first, SUBLANES)
    src = buf if part == 0 else shifts.at[part - 1]
    return src[pl.ds(rows.start + SUBLANES * whole, rows.size), lanes]


TAP_ROW_CHUNKS = [pl.ds(r, TAP_ROWS) for r in range(0, ROW_BLK, TAP_ROWS)]
TAP_LANE_TILES = [pl.ds(c, LANES) for c in range(0, D_CONV, LANES)]


def _layer_norm_stats(cv):
    mu = jnp.mean(cv, axis=-1, keepdims=True)
    xc = cv - mu
    rstd = lax.rsqrt(jnp.mean(xc * xc, axis=-1, keepdims=True) + LN_EPS)
    return xc * rstd, rstd


def _conv_fwd(pc, conv_w, conv_b, ln_g, ln_b, w_pw2, b_pw2):
    t = pc.shape[0]

    def body(a_ref, b_ref, gate_ref, ha_ref, hb_ref, cw_ref, cb_ref, lg_ref, lb_ref, wp_ref, bp_ref,
             cout_ref, cv_ref, p_ref, sl_ref, buf, shifts):
        i = pl.program_id(0)
        _fill_glu(buf, i, a_ref, b_ref, ha_ref, hb_ref)
        _fill_shifts(shifts, buf)
        for lanes in TAP_LANE_TILES:
            for rows in TAP_ROW_CHUNKS:
                acc = jnp.zeros((TAP_ROWS, LANES), F32) + cb_ref[:, lanes]
                for j in range(CONV_WIDTH):
                    acc = acc + cw_ref[j:j + 1, lanes] * _window(buf, shifts, HALO - (CONV_WIDTH - 1) + j, rows, lanes)
                cv_ref[rows, lanes] = acc
        xh, _ = _layer_norm_stats(cv_ref[...])
        ln = xh * lg_ref[...] + lb_ref[...]
        sl = (ln * _sigmoid(ln)).astype(BF16)
        sl_ref[...] = sl
        p = _dot(sl, wp_ref[...]) + bp_ref[...]
        p_ref[...] = p
        gate = gate_ref[...]
        cout_ref[...] = (p * (gate * _sigmoid(gate))).astype(BF16)

    vec = _whole((1, D_CONV))
    return pl.pallas_call(
        body, name="conv_fwd", grid=(t // ROW_BLK,),
        in_specs=[_rows(D_CONV, 0), _rows(D_CONV, 1), _rows(D_CONV, 2), _prev_halo(0), _prev_halo(1),
                  _whole((CONV_WIDTH, D_CONV)), vec, vec, vec, _whole((D_CONV, D_CONV)), vec],
        out_specs=[_rows(D_CONV)] * 4,
        out_shape=[jax.ShapeDtypeStruct((t, D_CONV), BF16), jax.ShapeDtypeStruct((t, D_CONV), F32),
                   jax.ShapeDtypeStruct((t, D_CONV), F32), jax.ShapeDtypeStruct((t, D_CONV), BF16)],
        scratch_shapes=[pltpu.VMEM((HALO + ROW_BLK, D_CONV), F32), pltpu.VMEM((SUBLANES - 1, SHIFT_ROWS, D_CONV), F32)],
        compiler_params=_cparams("parallel"),
    )(pc, pc, pc, pc, pc, conv_w, conv_b, ln_g, ln_b, w_pw2, b_pw2)


def _lower_triangle():
    row = lax.broadcasted_iota(jnp.int32, (ROW_BLK, ROW_BLK), 0)
    col = lax.broadcasted_iota(jnp.int32, (ROW_BLK, ROW_BLK), 1)
    return row > col


def _lower_triangle_t():
    row = lax.broadcasted_iota(jnp.int32, (ROW_BLK, ROW_BLK), 0)
    col = lax.broadcasted_iota(jnp.int32, (ROW_BLK, ROW_BLK), 1)
    return row < col


def _tri_sum(x, umat):
    return _dot(x.astype(BF16), umat)


def _log_gates(z):
    ls = -(jnp.maximum(z, 0.0) + jnp.log(1.0 + jnp.exp(-jnp.abs(z))))
    return ls, z + ls


def _head_lanes(hh):
    lane = lax.broadcasted_iota(jnp.int32, (ROW_BLK, HEAD_BLK), 1)
    return (lane >= HEAD_DIM * hh) & (lane < HEAD_DIM * (hh + 1))


def _merge_heads(acc_ref):
    out = acc_ref[HEADS_PER_BLOCK - 1]
    for hh in range(HEADS_PER_BLOCK - 1):
        out = jnp.where(_head_lanes(hh), acc_ref[hh], out)
    return out


def _qkv_specs(t):
    n_blk = D_SB // HEAD_BLK
    return [pl.BlockSpec((ROW_BLK, HEAD_BLK), lambda hp, i: (i, hp)),
            pl.BlockSpec((t, HEAD_BLK), lambda hp, i: (0, n_blk + hp)),
            pl.BlockSpec((t, HEAD_BLK), lambda hp, i: (0, 2 * n_blk + hp))]


def _carry_spec():
    return pl.BlockSpec((HEADS_PER_BLOCK, ROW_BLK, LANES), lambda hp, i: (hp, i, 0))


def _last_block_rows(t, n_tokens):
    packed_rows = 16
    return -(-(n_tokens - (t - ROW_BLK)) // packed_rows) * packed_rows


def _by_block_rows(i, last_rows, sweep):
    if last_rows == ROW_BLK:
        sweep(ROW_BLK)
        return
    last = pl.num_programs(1) - 1
    pl.when(i < last)(lambda: sweep(ROW_BLK))
    pl.when(i == last)(lambda: sweep(last_rows))


def _attn_fwd(qkv, n_tokens):
    t = qkv.shape[0]
    assert t // ROW_BLK <= LANES

    def body(q_ref, k_ref, v_ref, o_ref, c_ref, acc_ref, run_ref, qm_ref, z_ref):
        i = pl.program_id(1)
        q = q_ref[...]
        heads = range(HEADS_PER_BLOCK)
        for hh in heads:
            qm_ref[hh] = jnp.where(_head_lanes(hh), q, jnp.zeros_like(q)) * jnp.asarray(SB_SCALE, BF16)
        acc_ref[...] = jnp.zeros_like(acc_ref)
        c_ref[...] = jnp.zeros_like(c_ref)
        run_ref[...] = jnp.zeros_like(run_ref)

        def sweep(n_rows):
            rows = pl.ds(0, n_rows)
            lower = _lower_triangle()[:n_rows]
            umat = jnp.where(_lower_triangle(), 1.0, 0.0).astype(BF16)
            lane = lax.broadcasted_iota(jnp.int32, (n_rows, LANES), 1)

            def scores(jb):
                start = pl.multiple_of(jb * ROW_BLK, ROW_BLK)
                kb = k_ref[pl.ds(start, ROW_BLK), :]
                for hh in heads:
                    z_ref[hh, rows] = _dot_nt(qm_ref[hh, rows], kb)

            def block(jb, diagonal):
                start = pl.multiple_of(jb * ROW_BLK, ROW_BLK)
                vb = v_ref[pl.ds(start, ROW_BLK), :]
                logits = []
                for hh in heads:
                    ls, lb = _log_gates(z_ref[hh, rows])
                    if diagonal:
                        ls = jnp.where(lower, ls, 0.0)
                    run = run_ref[hh, rows]
                    if not diagonal:
                        c_ref[hh, rows] = jnp.where(lane == jb, run, c_ref[hh, rows])
                    logits.append(lb + jnp.concatenate([run, run], axis=1) + _tri_sum(ls, umat))
                    run_ref[hh, rows] = run + jnp.sum(ls, axis=1, keepdims=True)
                scores(jnp.maximum(jb - 1, 0))
                for hh in heads:
                    a = jnp.exp(logits[hh])
                    if diagonal:
                        a = jnp.where(lower, a, 0.0)
                    acc_ref[hh, rows] += _dot(a.astype(BF16), vb)

            scores(i)
            block(i, True)

            @pl.loop(0, i)
            def _(n):
                block(i - 1 - n, False)

        _by_block_rows(i, _last_block_rows(t, n_tokens), sweep)
        o_ref[...] = _merge_heads(acc_ref)

    per_head = (HEADS_PER_BLOCK, ROW_BLK, HEAD_BLK)
    return pl.pallas_call(
        body, name="attn_fwd", grid=(D_SB // HEAD_BLK, t // ROW_BLK),
        in_specs=_qkv_specs(t),
        out_specs=[pl.BlockSpec((ROW_BLK, HEAD_BLK), lambda hp, i: (i, hp)), _carry_spec()],
        out_shape=[jax.ShapeDtypeStruct((t, D_SB), F32),
                   jax.ShapeDtypeStruct((D_SB // HEAD_DIM, t, LANES), F32)],
        scratch_shapes=[pltpu.VMEM(per_head, F32), pltpu.VMEM((HEADS_PER_BLOCK, ROW_BLK, LANES), F32),
                        pltpu.VMEM(per_head, BF16), pltpu.VMEM((HEADS_PER_BLOCK, ROW_BLK, ROW_BLK), F32)],
        compiler_params=_cparams("arbitrary", "arbitrary"),
    )(qkv, qkv, qkv)


def _outproj_fwd(h, cout, sraw, sbg, w_out, g_post):
    t = h.shape[0]
    blk = _dense_rows(t)

    def body(h_ref, c_ref, s_ref, g_ref, w_ref, gp_ref, hn_ref, mixed_ref, mix_ref):
        gate = g_ref[...]
        mix_ref[:, 0:D_CONV] = c_ref[...]
        mix_ref[:, D_CONV:] = (s_ref[...] * (gate * _sigmoid(gate))).astype(BF16)
        mixed = _dot(mix_ref[...], w_ref[...])
        mixed_ref[...] = mixed
        r = lax.rsqrt(jnp.mean(mixed * mixed, axis=-1, keepdims=True) + RMS_EPS)
        hn_ref[...] = h_ref[...] + mixed * r * gp_ref[...]

    return pl.pallas_call(
        body, name="outproj_fwd", grid=(t // blk,),
        in_specs=[_rows(D_MODEL, rows=blk), _rows(D_CONV, rows=blk), _rows(D_SB, rows=blk), _rows(D_SB, rows=blk),
                  _whole((D_MODEL, D_MODEL)), _whole((1, D_MODEL))],
        out_specs=[_rows(D_MODEL, rows=blk)] * 3,
        out_shape=[jax.ShapeDtypeStruct((t, D_MODEL), F32), jax.ShapeDtypeStruct((t, D_MODEL), F32),
                   jax.ShapeDtypeStruct((t, D_MODEL), BF16)],
        compiler_params=_cparams("parallel"),
    )(h, cout, sraw, sbg, w_out, g_post)


def _loss_and_grad(h, target, seq):
    t = h.shape[0]
    blk = _dense_rows(t)

    def body(h_ref, t_ref, loss_ref, dh_ref):
        i = pl.program_id(0)

        @pl.when(i == 0)
        def _():
            loss_ref[...] = jnp.zeros_like(loss_ref)

        row = i * blk + lax.broadcasted_iota(jnp.int32, (blk, D_MODEL), 0)
        real = (row >= N_META) & (row < N_META + seq)
        diff = jnp.where(real, h_ref[...] - t_ref[...], 0.0)
        sq = jnp.sum(jnp.sum(diff * diff, axis=1, keepdims=True), axis=0, keepdims=True)
        loss_ref[...] += (0.5 / D_MODEL) * sq
        dh_ref[...] = diff * (1.0 / D_MODEL)

    return pl.pallas_call(
        body, name="loss", grid=(t // blk,),
        in_specs=[_rows(D_MODEL, rows=blk), _rows(D_MODEL, rows=blk)],
        out_specs=[_whole((1, 1)), _rows(D_MODEL, rows=blk)],
        out_shape=[jax.ShapeDtypeStruct((1, 1), F32), jax.ShapeDtypeStruct((t, D_MODEL), F32)],
        compiler_params=_cparams("arbitrary"),
    )(h, target)


def _outproj_bwd(dh, mixed, g_post, sraw, sbg, w_out_t):
    t = dh.shape[0]
    blk = _dense_rows(t)

    def body(dh_ref, mixed_ref, gp_ref, s_ref, g_ref, wt_ref, dc_ref, ds_ref, dg_ref, dmb_ref, dgp_ref):
        @pl.when(pl.program_id(0) == 0)
        def _():
            dgp_ref[...] = jnp.zeros_like(dgp_ref)

        mixed = mixed_ref[...]
        r = lax.rsqrt(jnp.mean(mixed * mixed, axis=-1, keepdims=True) + RMS_EPS)
        nh = mixed * r
        dy = dh_ref[...]
        dgp_ref[...] += jnp.sum(dy * nh, axis=0, keepdims=True)
        dn = dy * gp_ref[...]
        dmixed = (r * (dn - nh * jnp.mean(dn * nh, axis=-1, keepdims=True))).astype(BF16)
        dmb_ref[...] = dmixed
        dmix = _dot_nt(dmixed, wt_ref[...])
        dc_ref[...] = dmix[:, 0:D_CONV]
        dsg = dmix[:, D_CONV:]
        gate = g_ref[...]
        sg = _sigmoid(gate)
        ds_ref[...] = dsg * (gate * sg)
        dg_ref[...] = (dsg * s_ref[...] * _dsilu(gate, sg)).astype(BF16)

    return pl.pallas_call(
        body, name="outproj_bwd", grid=(t // blk,),
        in_specs=[_rows(D_MODEL, rows=blk), _rows(D_MODEL, rows=blk), _whole((1, D_MODEL)), _rows(D_SB, rows=blk),
                  _rows(D_SB, rows=blk), _whole((D_MODEL, D_MODEL))],
        out_specs=[_rows(D_CONV, rows=blk), _rows(D_SB, rows=blk), _rows(D_SB, rows=blk), _rows(D_MODEL, rows=blk),
                   _whole((1, D_MODEL))],
        out_shape=[jax.ShapeDtypeStruct((t, D_CONV), F32), jax.ShapeDtypeStruct((t, D_SB), F32),
                   jax.ShapeDtypeStruct((t, D_SB), BF16), jax.ShapeDtypeStruct((t, D_MODEL), BF16),
                   jax.ShapeDtypeStruct((1, D_MODEL), F32)],
        compiler_params=_cparams("arbitrary"),
    )(dh, mixed, g_post, sraw, sbg, w_out_t)


def _attn_bwd(qkv, carries, do, n_tokens):
    t = qkv.shape[0]

    def body(q_ref, k_ref, v_ref, c_ref, do_ref, dq_ref, dk_ref, dv_ref, acc_ref, seen_ref, qm_ref, dom_ref, z_ref,
             da_ref, dz_ref, a_ref):
        i = pl.program_id(1)

        @pl.when(i == 0)
        def _():
            dk_ref[...] = jnp.zeros_like(dk_ref)
            dv_ref[...] = jnp.zeros_like(dv_ref)

        q = q_ref[...]
        dof = do_ref[...]
        heads = range(HEADS_PER_BLOCK)
        for hh in heads:
            qm_ref[hh] = jnp.where(_head_lanes(hh), q, jnp.zeros_like(q)) * jnp.asarray(SB_SCALE, BF16)
            dom_ref[hh] = jnp.where(_head_lanes(hh), dof, 0.0).astype(BF16)
        acc_ref[...] = jnp.zeros_like(acc_ref)
        seen_ref[...] = jnp.zeros_like(seen_ref)

        def sweep(n_rows):
            rows = pl.ds(0, n_rows)
            lower = _lower_triangle()[:n_rows]
            umat = jnp.where(_lower_triangle(), 1.0, 0.0).astype(BF16)
            umat_t = jnp.where(_lower_triangle_t(), 1.0, 0.0).astype(BF16)
            lane = lax.broadcasted_iota(jnp.int32, (n_rows, LANES), 1)

            def scores(jb):
                start = pl.multiple_of(jb * ROW_BLK, ROW_BLK)
                kb = k_ref[pl.ds(start, ROW_BLK), :]
                for hh in heads:
                    z_ref[hh, rows] = _dot_nt(qm_ref[hh, rows], kb)

            def value_grads(jb):
                start = pl.multiple_of(jb * ROW_BLK, ROW_BLK)
                vb = v_ref[pl.ds(start, ROW_BLK), :]
                for hh in heads:
                    da_ref[hh, rows] = _dot_nt(dom_ref[hh, rows], vb)

            def products(jb, hh):
                start = pl.multiple_of(jb * ROW_BLK, ROW_BLK)
                dzb = dz_ref[hh, rows]
                acc_ref[hh, rows] += _dot(dzb, k_ref[pl.ds(start, ROW_BLK), :])
                dk_ref[pl.ds(start, ROW_BLK), :] += _dot_tn(dzb, qm_ref[hh, rows])
                dv_ref[pl.ds(start, ROW_BLK), :] += _dot_tn(a_ref[hh, rows], dom_ref[hh, rows])

            def block(jb, diagonal):
                before = jnp.maximum(jb - 1, 0)
                lbs, logits = [], []
                for hh in heads:
                    products(before, hh)
                    ls, lb = _log_gates(z_ref[hh, rows])
                    if diagonal:
                        ls = jnp.where(lower, ls, 0.0)
                        logits.append(lb + _tri_sum(ls, umat))
                    else:
                        right = jnp.sum(jnp.where(lane == jb, c_ref[hh, rows], 0.0), axis=1, keepdims=True)
                        logits.append(lb + right + _tri_sum(ls, umat))
                    lbs.append(lb)
                if not diagonal:
                    scores(jb + 1)
                gs, befores = [], []
                for hh in heads:
                    a = jnp.exp(logits[hh])
                    if diagonal:
                        a = jnp.where(lower, a, 0.0)
                    g = da_ref[hh, rows] * a
                    seen = seen_ref[hh, rows]
                    befores.append(jnp.concatenate([seen, seen], axis=1) + _tri_sum(g, umat_t))
                    seen_ref[hh, rows] = seen + jnp.sum(g, axis=1, keepdims=True)
                    a_ref[hh, rows] = a.astype(BF16)
                    gs.append(g)
                if not diagonal:
                    value_grads(jb + 1)
                for hh in heads:
                    dz = gs[hh] - jnp.exp(lbs[hh]) * (gs[hh] + befores[hh])
                    if diagonal:
                        dz = jnp.where(lower, dz, 0.0)
                    dz_ref[hh, rows] = dz.astype(BF16)

            dz_ref[...] = jnp.zeros_like(dz_ref)
            a_ref[...] = jnp.zeros_like(a_ref)
            scores(0)
            value_grads(0)

            @pl.loop(0, i)
            def _(jb):
                block(jb, False)

            block(i, True)
            for hh in heads:
                products(i, hh)

        _by_block_rows(i, _last_block_rows(t, n_tokens), sweep)
        dq_ref[...] = (_merge_heads(acc_ref) * SB_SCALE).astype(BF16)

    blk = pl.BlockSpec((ROW_BLK, HEAD_BLK), lambda hp, i: (i, hp))
    full = pl.BlockSpec((t, HEAD_BLK), lambda hp, i: (0, hp))
    per_head = (HEADS_PER_BLOCK, ROW_BLK, HEAD_BLK)
    return pl.pallas_call(
        body, name="attn_bwd", grid=(D_SB // HEAD_BLK, t // ROW_BLK),
        in_specs=_qkv_specs(t) + [_carry_spec(), blk],
        out_specs=[blk, full, full],
        out_shape=[jax.ShapeDtypeStruct((t, D_SB), BF16)] + [jax.ShapeDtypeStruct((t, D_SB), F32)] * 2,
        scratch_shapes=[pltpu.VMEM(per_head, F32), pltpu.VMEM((HEADS_PER_BLOCK, ROW_BLK, LANES), F32),
                        pltpu.VMEM(per_head, BF16), pltpu.VMEM(per_head, BF16)]
                       + [pltpu.VMEM((HEADS_PER_BLOCK, ROW_BLK, ROW_BLK), dtype) for dtype in (F32, F32, BF16, BF16)],
        compiler_params=_cparams("arbitrary", "arbitrary"),
    )(qkv, qkv, qkv, carries, do)


def _conv_bwd_rows(dcout, pc, cv, p, ln_g, ln_b, w_pw2_t):
    t = dcout.shape[0]
    blk = _dense_rows(t)

    def body(dc_ref, gate_ref, cv_ref, p_ref, lg_ref, lb_ref, wt_ref, dcv_ref, dgate_ref, dpb_ref, vec_ref):
        @pl.when(pl.program_id(0) == 0)
        def _():
            vec_ref[...] = jnp.zeros_like(vec_ref)

        dc = dc_ref[...]
        gate = gate_ref[...]
        sg = _sigmoid(gate)
        dp = dc * (gate * sg)
        dgate_ref[...] = (dc * p_ref[...] * _dsilu(gate, sg)).astype(BF16)
        dpb = dp.astype(BF16)
        dpb_ref[...] = dpb
        xh, rstd = _layer_norm_stats(cv_ref[...])
        ln = xh * lg_ref[...] + lb_ref[...]
        s2 = _sigmoid(ln)
        dln = _dot_nt(dpb, wt_ref[...]) * _dsilu(ln, s2)
        dxh = dln * lg_ref[...]
        dcv = rstd * (dxh - jnp.mean(dxh, axis=-1, keepdims=True)
                      - xh * jnp.mean(dxh * xh, axis=-1, keepdims=True))
        dcv_ref[...] = dcv
        vec_ref[0:1, :] += jnp.sum(dp, axis=0, keepdims=True)
        vec_ref[1:2, :] += jnp.sum(dln * xh, axis=0, keepdims=True)
        vec_ref[2:3, :] += jnp.sum(dln, axis=0, keepdims=True)
        vec_ref[3:4, :] += jnp.sum(dcv, axis=0, keepdims=True)

    vec = _whole((1, D_CONV))
    return pl.pallas_call(
        body, name="conv_bwd_rows", grid=(t // blk,),
        in_specs=[_rows(D_CONV, rows=blk), _rows(D_CONV, 2, rows=blk), _rows(D_CONV, rows=blk), _rows(D_CONV, rows=blk),
                  vec, vec, _whole((D_CONV, D_CONV))],
        out_specs=[_rows(D_CONV, rows=blk), _rows(D_CONV, rows=blk), _rows(D_CONV, rows=blk), _whole((8, D_CONV))],
        out_shape=[jax.ShapeDtypeStruct((t, D_CONV), F32), jax.ShapeDtypeStruct((t, D_CONV), BF16),
                   jax.ShapeDtypeStruct((t, D_CONV), BF16), jax.ShapeDtypeStruct((8, D_CONV), F32)],
        compiler_params=_cparams("arbitrary"),
    )(dcout, pc, cv, p, ln_g, ln_b, w_pw2_t)


def _conv_bwd_taps(dcv, pc, conv_w):
    t = dcv.shape[0]
    n_halo = t // HALO
    per = ROW_BLK // HALO

    def body(d_ref, dn_ref, a_ref, b_ref, ha_ref, hb_ref, cw_ref, da_ref, db_ref, dw_ref, cbuf, dbuf, cshifts, dshifts):
        i = pl.program_id(0)

        @pl.when(i == 0)
        def _():
            dw_ref[...] = jnp.zeros_like(dw_ref)

        _fill_glu(cbuf, i, a_ref, b_ref, ha_ref, hb_ref)
        dbuf[0:ROW_BLK, :] = d_ref[...]
        dbuf[ROW_BLK:ROW_BLK + HALO, :] = jnp.where(i < pl.num_programs(0) - 1, dn_ref[...], 0.0)
        _fill_shifts(cshifts, cbuf)
        _fill_shifts(dshifts, dbuf)
        for lanes in TAP_LANE_TILES:
            for rows in TAP_ROW_CHUNKS:
                acc = jnp.zeros((TAP_ROWS, LANES), F32)
                for j in range(CONV_WIDTH):
                    acc = acc + cw_ref[j:j + 1, lanes] * _window(dbuf, dshifts, CONV_WIDTH - 1 - j, rows, lanes)
                sb = _sigmoid(b_ref[rows, lanes])
                da_ref[rows, lanes] = (acc * sb).astype(BF16)
                db_ref[rows, lanes] = (acc * a_ref[rows, lanes] * sb * (1.0 - sb)).astype(BF16)
            for j in range(CONV_WIDTH):
                acc = jnp.zeros((TAP_ROWS, LANES), F32)
                for rows in TAP_ROW_CHUNKS:
                    acc = acc + d_ref[rows, lanes] * _window(cbuf, cshifts, HALO - (CONV_WIDTH - 1) + j, rows, lanes)
                dw_ref[j:j + 1, lanes] += jnp.sum(acc, axis=0, keepdims=True)

    return pl.pallas_call(
        body, name="conv_bwd_taps", grid=(t // ROW_BLK,),
        in_specs=[_rows(D_CONV),
                  pl.BlockSpec((HALO, D_CONV), lambda i: (jnp.minimum((i + 1) * per, n_halo - 1), 0)),
                  _rows(D_CONV, 0), _rows(D_CONV, 1), _prev_halo(0), _prev_halo(1),
                  _whole((CONV_WIDTH, D_CONV))],
        out_specs=[_rows(D_CONV), _rows(D_CONV), _whole((32, D_CONV))],
        out_shape=[jax.ShapeDtypeStruct((t, D_CONV), BF16), jax.ShapeDtypeStruct((t, D_CONV), BF16),
                   jax.ShapeDtypeStruct((32, D_CONV), F32)],
        scratch_shapes=[pltpu.VMEM((HALO + ROW_BLK, D_CONV), F32), pltpu.VMEM((ROW_BLK + HALO, D_CONV), F32),
                        pltpu.VMEM((SUBLANES - 1, SHIFT_ROWS, D_CONV), F32),
                        pltpu.VMEM((SUBLANES - 1, SHIFT_ROWS, D_CONV), F32)],
        compiler_params=_cparams("arbitrary"),
    )(dcv, dcv, pc, pc, pc, pc, conv_w)


def _inproj_bwd(dh_out, h, g_pre, pieces, w_in_t):
    t = h.shape[0]
    blk = _dense_rows(t)

    def body(dh_ref, h_ref, g_ref, *rest):
        piece_refs, (wt_ref, dhin_ref, dproj_ref, dg_ref) = rest[:7], rest[7:]

        @pl.when(pl.program_id(0) == 0)
        def _():
            dg_ref[...] = jnp.zeros_like(dg_ref)

        for k, ref in enumerate(piece_refs):
            dproj_ref[:, 512 * k:512 * (k + 1)] = ref[...].astype(BF16)
        du = _dot_nt(dproj_ref[...], wt_ref[...])
        x = h_ref[...]
        r = lax.rsqrt(jnp.mean(x * x, axis=-1, keepdims=True) + RMS_EPS)
        xh = x * r
        dg_ref[...] += jnp.sum(du * xh, axis=0, keepdims=True)
        dxh = du * g_ref[...]
        dhin_ref[...] = dh_ref[...] + r * (dxh - xh * jnp.mean(dxh * xh, axis=-1, keepdims=True))

    return pl.pallas_call(
        body, name="inproj_bwd", grid=(t // blk,),
        in_specs=[_rows(D_MODEL, rows=blk), _rows(D_MODEL, rows=blk), _whole((1, D_MODEL))] + [_rows(512, rows=blk)] * 7
                 + [_whole((D_MODEL, D_IN))],
        out_specs=[_rows(D_MODEL, rows=blk), _rows(D_IN, rows=blk), _whole((1, D_MODEL))],
        out_shape=[jax.ShapeDtypeStruct((t, D_MODEL), F32), jax.ShapeDtypeStruct((t, D_IN), BF16),
                   jax.ShapeDtypeStruct((1, D_MODEL), F32)],
        compiler_params=_cparams("arbitrary"),
    )(dh_out, h, g_pre, *pieces, w_in_t)


def _weight_grad(xb, dyb, name):
    t, k = xb.shape
    n = dyb.shape[1]

    def body(x_ref, dy_ref, o_ref, acc_ref):
        i = pl.program_id(0)

        @pl.when(i == 0)
        def _():
            acc_ref[...] = jnp.zeros_like(acc_ref)

        acc_ref[...] += _dot_tn(x_ref[...], dy_ref[...])

        @pl.when(i == pl.num_programs(0) - 1)
        def _():
            o_ref[...] = acc_ref[...].astype(BF16)

    return pl.pallas_call(
        body, name=name, grid=(t // ROW_BLK,),
        in_specs=[_rows(k), _rows(n)], out_specs=_whole((k, n)), out_shape=jax.ShapeDtypeStruct((k, n), BF16),
        scratch_shapes=[pltpu.VMEM((k, n), F32)],
        compiler_params=_cparams("arbitrary"),
    )(xb, dyb)


def _position():
    return lax.axis_index("x"), lax.axis_index("y"), lax.axis_index("c")


def _comm_call(body, name, ins, out_shapes):
    n = len(ins)
    hbm = pl.BlockSpec(memory_space=pltpu.HBM)
    return pl.pallas_call(
        functools.partial(body, n), name=name, in_specs=[hbm] * n, out_specs=[hbm] * n, out_shape=out_shapes,
        scratch_shapes=[pltpu.SemaphoreType.DMA((n, N_DEV - 1)), pltpu.SemaphoreType.DMA((n, N_DEV - 1)),
                        pltpu.SemaphoreType.DMA((n,))],
    )(*ins)


def _all_gather(blocks, name):
    def body(n, *refs):
        x_refs, out_refs, (send_sems, recv_sems, local_sems) = refs[:n], refs[n:2 * n], refs[2 * n:]
        x, y, c = _position()
        me, sibling = (x, y, c), (x, y, 1 - c)
        chips = [(1 - x, y), (x, 1 - y), (1 - x, 1 - y)]

        def slot(a, px, py, pc):
            return out_refs[a].at[4 * px + 2 * py + pc]

        def copy(a, k, origin, to, own=False):
            return pltpu.make_async_remote_copy(
                src_ref=x_refs[a] if own else slot(a, *origin), dst_ref=slot(a, *origin),
                send_sem=send_sems.at[a, k], recv_sem=recv_sems.at[a, k], device_id=to, device_id_type=MESH)

        arrays = range(n)
        mine = [pltpu.make_async_copy(x_refs[a], slot(a, *me), local_sems.at[a]) for a in arrays]
        first = [copy(a, 1 + j, me, (*chip, c), own=True) for j, chip in enumerate(chips) for a in arrays]
        first += [copy(a, 0, me, sibling, own=True) for a in arrays]
        for cp in mine + first:
            cp.start()
        passed = []
        for j, chip in enumerate(chips):
            for a in arrays:
                copy(a, 1 + j, (*chip, c), me).wait_recv()
                passed.append(copy(a, 4 + j, (*chip, c), sibling))
                passed[-1].start()
        for a in arrays:
            copy(a, 0, sibling, me).wait_recv()
            for j, chip in enumerate(chips):
                copy(a, 4 + j, (*chip, 1 - c), me).wait_recv()
        for cp in first + passed:
            cp.wait_send()
        for cp in mine:
            cp.wait()

    return _comm_call(body, name, blocks, [jax.ShapeDtypeStruct((N_DEV,) + b.shape, b.dtype) for b in blocks])


def _exchange_copies(g_refs, land_refs, sems, gather):
    x, y, c = _position()
    me = 4 * x + 2 * y + c
    out = []
    for g_ref, land_ref, (send_sem, recv_sem, local_sem) in zip(g_refs, land_refs, sems):
        def mine(slot, g_ref=g_ref):
            return g_ref if gather else g_ref.at[slot]

        def remote(src, dst, dev):
            return pltpu.make_async_remote_copy(src_ref=src, dst_ref=dst, send_sem=send_sem, recv_sem=recv_sem,
                                                device_id=dev, device_id_type=MESH)

        sends = []
        for k in range(1, N_DEV):
            px = 1 - x if k & 4 else x
            py = 1 - y if k & 2 else y
            pc = 1 - c if k & 1 else c
            sends.append(remote(mine(4 * px + 2 * py + pc), land_ref.at[me], (px, py, pc)))
        seven = land_ref.at[pl.ds(0, N_DEV - 1)]
        out.append((pltpu.make_async_copy(mine(me), land_ref.at[me], local_sem), sends, remote(seven, seven, (x, y, c))))
    return out


_HBM = pl.BlockSpec(memory_space=pltpu.HBM)
_SEM = pl.BlockSpec(memory_space=pltpu.SEMAPHORE)
_ORDERED = pltpu.CompilerParams(has_side_effects=pltpu.SideEffectType.DATAFLOW_SIDE_EFFECTING)
SEMS_PER_ARRAY = 3


def _exchange_start(arrays, after, name, gather):
    n = len(arrays)
    n_sems = SEMS_PER_ARRAY * n

    def body(*refs):
        g_refs, land_refs, sems, token = refs[:n], refs[n:2 * n], refs[2 * n + 1:2 * n + 1 + n_sems], refs[-1]
        sems = [sems[SEMS_PER_ARRAY * a:SEMS_PER_ARRAY * (a + 1)] for a in range(n)]
        for local, sends, _ in _exchange_copies(g_refs, land_refs, sems, gather):
            local.start()
            for cp in sends:
                cp.start()
        token[...] = jnp.zeros_like(token)

    buffers = list(arrays) + [lax.empty((N_DEV,) + g.shape if gather else g.shape, g.dtype) for g in arrays]
    outs = pl.pallas_call(
        body, name=name, in_specs=[_HBM] * (2 * n) + [pl.BlockSpec(memory_space=pl.ANY)],
        out_specs=[_SEM] * n_sems + [_HBM] * (2 * n) + [pl.BlockSpec(memory_space=pltpu.VMEM)],
        out_shape=[pltpu.SemaphoreType.DMA(())] * n_sems + [pltpu.HBM(b.shape, b.dtype) for b in buffers]
                  + [jax.ShapeDtypeStruct((8, LANES), F32)],
        input_output_aliases={a: n_sems + a for a in range(2 * n)}, compiler_params=_ORDERED,
    )(*[pltpu.with_memory_space_constraint(b, pltpu.HBM) for b in buffers], after)
    return outs[:n_sems], outs[n_sems:n_sems + n], outs[n_sems + n:n_sems + 2 * n], outs[-1]


def _exchange_wait(sems, arrays, landings, after, name, gather):
    n = len(arrays)
    n_sems = SEMS_PER_ARRAY * n

    def body(*refs):
        g_refs, land_refs, sems = refs[:n], refs[n:2 * n], refs[2 * n:2 * n + n_sems]
        sems = [sems[SEMS_PER_ARRAY * a:SEMS_PER_ARRAY * (a + 1)] for a in range(n)]
        for local, _, all_seven in _exchange_copies(g_refs, land_refs, sems, gather):
            all_seven.wait_recv()
            all_seven.wait_send()
            local.wait()

    buffers = list(arrays) + list(landings)
    outs = pl.pallas_call(
        body, name=name, in_specs=[_HBM] * (2 * n) + [_SEM] * n_sems + [pl.BlockSpec(memory_space=pl.ANY)],
        out_specs=[_HBM] * (2 * n), out_shape=[pltpu.HBM(b.shape, b.dtype) for b in buffers],
        input_output_aliases={a: a for a in range(2 * n)}, compiler_params=_ORDERED,
    )(*buffers, *sems, after)
    return outs[n:]


def _block_rows(r, row_bytes, budget=1 << 20):
    cap = max(8, budget // row_bytes)
    return max(d for d in range(8, min(r, cap) + 1, 8) if r % d == 0)


def _sum_adamw(parts, w, m, v, name):
    n_parts, r, c = parts.shape
    br = _block_rows(r, 4 * c)

    def body(p_ref, w_ref, m_ref, v_ref, g_out, d_out, m_out, v_out):
        g = p_ref[0].astype(F32)
        for s in range(1, n_parts):
            g = g + p_ref[s].astype(F32)
        m_new = ADAM_B1 * m_ref[...] + (1.0 - ADAM_B1) * g
        v_new = ADAM_B2 * v_ref[...] + (1.0 - ADAM_B2) * (g * g)
        m_hat = m_new / (1.0 - ADAM_B1 ** ADAM_STEP)
        v_hat = v_new / (1.0 - ADAM_B2 ** ADAM_STEP)
        g_out[...] = g
        d_out[...] = -ADAM_LR * (m_hat / (jnp.sqrt(v_hat) + ADAM_EPS) + ADAM_WD * w_ref[...])
        m_out[...] = m_new
        v_out[...] = v_new

    row = pl.BlockSpec((br, c), lambda i: (i, 0))
    return pl.pallas_call(
        body, name=name, grid=(r // br,),
        in_specs=[pl.BlockSpec((n_parts, br, c), lambda i: (0, i, 0)), row, row, row],
        out_specs=[row] * 4, out_shape=[jax.ShapeDtypeStruct((r, c), F32)] * 4,
        compiler_params=_cparams("parallel"),
    )(parts, w, m, v)


def _sum_parts(parts, name):
    n_parts, r, c = parts.shape

    def body(p_ref, o_ref):
        g = p_ref[0]
        for s in range(1, n_parts):
            g = g + p_ref[s]
        o_ref[...] = g

    return pl.pallas_call(
        body, name=name, in_specs=[pl.BlockSpec(memory_space=pltpu.VMEM)],
        out_specs=pl.BlockSpec(memory_space=pltpu.VMEM), out_shape=jax.ShapeDtypeStruct((r, c), F32),
    )(parts)


def _pack(arrays):
    flat = jnp.concatenate([a.reshape(-1) for a in arrays])
    pad = -flat.shape[0] % (8 * LANES)
    if pad:
        flat = jnp.pad(flat, (0, pad))
    return flat.reshape(-1, LANES)


def _unpack(buf, shapes):
    flat = buf.reshape(-1)
    out, at = [], 0
    for shape in shapes:
        size = 1
        for d in shape:
            size *= d
        out.append(lax.slice_in_dim(flat, at, at + size).reshape(shape))
        at += size
    return out


def _local_step(x, target, meta, pre_g, post_g, conv_w, conv_b, ln_g, ln_b, b_pw2, weights, ship, ship_small):
    depth = pre_g.shape[0]
    seq = x.shape[0]
    t = -(-(N_META + seq) // ROW_BLK) * ROW_BLK
    tail = t - N_META - seq
    h = jnp.concatenate([meta, x, jnp.zeros((tail, D_MODEL), F32)], axis=0)
    target = jnp.pad(target, ((N_META, tail), (0, 0)))
    row = lambda a, l: a[l][None, :]

    saved = []
    for l in range(depth):
        w_in, w_pw2, w_out = weights(l, h)
        pc, qkv, sbg, u = _inproj_fwd(h, row(pre_g, l), w_in)
        cout, cv, p, sl = _conv_fwd(pc, conv_w[l], row(conv_b, l), row(ln_g, l), row(ln_b, l), w_pw2, row(b_pw2, l))
        sraw, carries = _attn_fwd(qkv, N_META + seq)
        h_new, mixed, mix = _outproj_fwd(h, cout, sraw, sbg, w_out, row(post_g, l))
        saved.append((h, pc, qkv, sbg, u, cv, p, sl, sraw, carries, mixed, mix, w_in, w_pw2, w_out))
        h = h_new

    loss, dh = _loss_and_grad(h, target, seq)

    grads = {k: [None] * depth for k in ("pre_g", "post_g", "conv_w", "conv_b", "ln_g", "ln_b", "b_pw2")}
    token = jnp.zeros((8, LANES), F32)
    for l in reversed(range(depth)):
        h_in, pc, qkv, sbg, u, cv, p, sl, sraw, carries, mixed, mix, w_in_t, w_pw2_t, w_out_t = saved[l]
        dcout, dsraw, dsbg, dmixed, dg_post = _outproj_bwd(dh, mixed, row(post_g, l) + token[:1, :1], sraw, sbg, w_out_t)
        dq, dk, dv = _attn_bwd(qkv, carries, dsraw, N_META + seq)
        dcv, dgate, dpb, vecs = _conv_bwd_rows(dcout, pc, cv, p, row(ln_g, l), row(ln_b, l), w_pw2_t)
        late = {"w_pw2": (sl, dpb), "w_out": (mix, dmixed)}
        taps = conv_w[l]
        if l == 0:
            token = ship(l, dcv, {k: _weight_grad(*late.pop(k), k + "_grad") for k in ("w_pw2", "w_out")})
            taps = taps + token[:1, :1]
        da, db, dconv_w = _conv_bwd_taps(dcv, pc, taps)
        pieces = (da, db, dgate, dq, dk, dv, dsbg)
        gain = row(pre_g, l)
        if l == 0:
            dproj = jnp.concatenate([piece.astype(BF16) for piece in pieces], axis=1)
            token = ship(l, dproj, {"w_in": _weight_grad(u, dproj, "w_in_grad")})
            gain = gain + token[:1, :1]
        dh, dproj, dg_pre = _inproj_bwd(dh, h_in, gain, pieces, w_in_t)
        grads["pre_g"][l] = dg_pre[0]
        grads["post_g"][l] = dg_post[0]
        grads["b_pw2"][l], grads["ln_g"][l], grads["ln_b"][l], grads["conv_b"][l] = vecs[0], vecs[1], vecs[2], vecs[3]
        grads["conv_w"][l] = dconv_w[:CONV_WIDTH]
        if l == 0:
            grads = {k: jnp.stack(v) for k, v in grads.items()}
            grads["meta"] = dh[:N_META]
            token = ship_small(grads, loss[0, 0])
        else:
            late["w_in"] = (u, dproj)
            token = ship(l, dh, {k: _weight_grad(*late[k], k + "_grad") for k in ("w_in", "w_pw2", "w_out")})

    return dh[N_META:N_META + seq], token


def _shard_major(full, axis):
    shape = full.shape
    split = full.reshape(shape[:axis] + (N_DEV, shape[axis] // N_DEV) + shape[axis + 1:])
    return jnp.moveaxis(split, axis, 0)


def _whole_from_shards(shards, axis):
    moved = jnp.moveaxis(shards, 0, axis)
    shape = moved.shape
    return moved.reshape(shape[:axis] + (shape[axis] * shape[axis + 1],) + shape[axis + 2:])


def kernel(x, meta_tokens, pre_norm_g, post_norm_g, w_in, conv_w, conv_b, conv_ln_g, conv_ln_b, w_pw2, b_pw2, w_out, loss_target, m_meta_tokens, m_pre_norm_g, m_post_norm_g, m_w_in, m_conv_w, m_conv_b, m_conv_ln_g, m_conv_ln_b, m_w_pw2, m_b_pw2, m_w_out, v_meta_tokens, v_pre_norm_g, v_post_norm_g, v_w_in, v_conv_w, v_conv_b, v_conv_ln_g, v_conv_ln_b, v_w_pw2, v_b_pw2, v_w_out):
    me = 4 * lax.axis_index("x") + 2 * lax.axis_index("y") + lax.axis_index("c")

    depth = w_in.shape[0]
    big = [w.astype(BF16) for w in (w_in, w_pw2, w_out)]
    *first, conv_w_s, meta_s = _all_gather([w[0] for w in big] + [conv_w, meta_tokens], "gather_first_layer")
    *gathering, token = _exchange_start([w[l] for l in range(1, depth) for w in big], meta_s, "gather_start", gather=True)
    conv_w_full = _whole_from_shards(conv_w_s, 2)
    meta_full = _whole_from_shards(meta_s, 1)
    shard_axis = (1, 0, 0)
    later = []

    def weights(l, h):
        if l == 0:
            return [_whole_from_shards(s, axis) for s, axis in zip(first, shard_axis)]
        if not later:
            later.extend(_exchange_wait(*gathering, h, "gather_wait", gather=True))
        return [_whole_from_shards(s, axis) for s, axis in zip(later[len(big) * (l - 1):len(big) * l], shard_axis)]

    in_flight = [[] for _ in range(depth)]
    grad_axis = dict(zip(("w_in", "w_pw2", "w_out"), shard_axis))

    def ship(l, after, dws):
        slabs = [_shard_major(dw, grad_axis[k]) for k, dw in dws.items()]
        sems, arrays, landings, token = _exchange_start(slabs, after, f"exchange_start_{l}_{len(in_flight[l])}",
                                                        gather=False)
        in_flight[l].append((list(dws), sems, arrays, landings))
        return token

    small_names = ("pre_g", "post_g", "conv_b", "ln_g", "ln_b", "b_pw2", "conv_w", "meta")
    small_in_flight, small_shapes_full = [], []

    def ship_small(grads, loss):
        small_full = [grads[k] for k in small_names] + [loss.reshape(1)]
        small_shapes_full.extend(a.shape for a in small_full)
        *in_flight_now, token = _exchange_start([_pack(small_full)], grads["meta"], "small_grads_start", gather=True)
        small_in_flight.extend(in_flight_now)
        return token

    dx, shipped = _local_step(x[0], loss_target[0], meta_full, pre_norm_g + token[:1, :1], post_norm_g, conv_w_full,
                              conv_b, conv_ln_g, conv_ln_b, b_pw2, weights, ship, ship_small)

    updated = [None] * depth
    done = shipped

    def update_layer(l):
        landed = {}
        for k, (names, sems, arrays, landings) in enumerate(in_flight[l]):
            landed.update(zip(names, _exchange_wait(sems, arrays, landings, done, f"exchange_wait_{l}_{k}", gather=False)))
        return [_sum_adamw(landed[name], w[l], m[l], v[l], "adamw_" + name) for name, w, m, v in (
            ("w_in", w_in, m_w_in, v_w_in), ("w_pw2", w_pw2, m_w_pw2, v_w_pw2), ("w_out", w_out, m_w_out, v_w_out))]

    for l in reversed(range(1, depth)):
        updated[l] = update_layer(l)
        done = updated[l][0][1]

    gathered, = _exchange_wait(*small_in_flight, done, "small_grads_wait", gather=True)
    summed = _unpack(_sum_parts(gathered, "sum_small_grads"), small_shapes_full)
    loss = summed[-1][0]
    g_small = dict(zip(small_names, summed))
    g_small["conv_w"] = lax.dynamic_slice_in_dim(g_small["conv_w"], me * conv_w.shape[2], conv_w.shape[2], axis=2)
    g_small["meta"] = lax.dynamic_slice_in_dim(g_small["meta"], me * meta_tokens.shape[1], meta_tokens.shape[1], axis=1)
    small_w = dict(zip(small_names, (pre_norm_g, post_norm_g, conv_b, conv_ln_g, conv_ln_b, b_pw2, conv_w, meta_tokens)))
    small_m = (m_pre_norm_g, m_post_norm_g, m_conv_b, m_conv_ln_g, m_conv_ln_b, m_b_pw2, m_conv_w, m_meta_tokens)
    small_v = (v_pre_norm_g, v_post_norm_g, v_conv_b, v_conv_ln_g, v_conv_ln_b, v_b_pw2, v_conv_w, v_meta_tokens)
    small_shapes = [small_w[k].shape for k in small_names]
    outs = _sum_adamw(_pack([g_small[k] for k in small_names])[None], _pack([small_w[k] for k in small_names]),
                      _pack(small_m), _pack(small_v), "adamw_small_weights")
    g_s, d_s, nm_s, nv_s = [dict(zip(small_names, _unpack(o, small_shapes))) for o in outs]

    done = outs[1]
    updated[0] = update_layer(0)
    (g_w_in, d_w_in, nm_w_in, nv_w_in), (g_w_pw2, d_w_pw2, nm_w_pw2, nv_w_pw2), (g_w_out, d_w_out, nm_w_out, nv_w_out) = [
        [jnp.stack([updated[l][a][k] for l in range(depth)]) for k in range(4)] for a in range(3)]

    def ordered(s, w_in_, w_pw2_, w_out_):
        return (s["meta"], s["pre_g"], s["post_g"], w_in_, s["conv_w"], s["conv_b"], s["ln_g"], s["ln_b"], w_pw2_,
                s["b_pw2"], w_out_)

    return (loss, dx[None], *ordered(g_s, g_w_in, g_w_pw2, g_w_out), *ordered(d_s, d_w_in, d_w_pw2, d_w_out),
            *ordered(nm_s, nm_w_in, nm_w_pw2, nm_w_out), *ordered(nv_s, nv_w_in, nv_w_pw2, nv_w_out))
```

```python
import functools

import jax
import jax.numpy as jnp
from jax import lax
from jax.experimental import pallas as pl
from jax.experimental.pallas import tpu as pltpu

F32 = jnp.float32
BF16 = jnp.bfloat16

D_MODEL = 1024
D_CONV = 512
D_SB = 512
HEAD_DIM = 64
HEADS_PER_BLOCK = 4
HEAD_BLK = HEADS_PER_BLOCK * HEAD_DIM
CONV_WIDTH = 31
N_META = 16
D_IN = 3 * D_CONV + 4 * D_SB
RMS_EPS = 1e-6
LN_EPS = 1e-5
SB_SCALE = HEAD_DIM ** -0.5

ADAM_LR = 0.001
ADAM_B1 = 0.9
ADAM_B2 = 0.999
ADAM_EPS = 1e-08
ADAM_WD = 0.01
ADAM_STEP = 10

N_DEV = 8
LANES = 128
ROW_BLK = 256
DENSE_ROWS_MAX = 544
HALO = 32
VMEM_LIMIT = 56 * 1024 * 1024
MESH = pl.DeviceIdType.MESH


def _cparams(*sem):
    return pltpu.CompilerParams(dimension_semantics=sem, vmem_limit_bytes=VMEM_LIMIT)


def _rows(n_cols, col=0, rows=ROW_BLK):
    return pl.BlockSpec((rows, n_cols), lambda i, col=col: (i, col))


def _dense_rows(t):
    packed_rows = 16
    return max(d for d in range(packed_rows, min(t, DENSE_ROWS_MAX) + 1, packed_rows) if t % d == 0)


def _whole(shape):
    return pl.BlockSpec(shape, lambda i: (0,) * len(shape))


def _sigmoid(x):
    return jax.nn.sigmoid(x)


def _dsilu(x, s):
    return s * (1.0 + x * (1.0 - s))


def _dot(a, b):
    return jnp.dot(a, b, preferred_element_type=F32)


def _dot_nt(a, b):
    return lax.dot_general(a, b, (((1,), (1,)), ((), ())), preferred_element_type=F32)


def _dot_tn(a, b):
    return lax.dot_general(a, b, (((0,), (0,)), ((), ())), preferred_element_type=F32)


def _inproj_fwd(h, g_pre, w_in):
    t = h.shape[0]
    blk = _dense_rows(t)

    def body(h_ref, g_ref, w_ref, pc_ref, qkv_ref, sbg_ref, u_ref):
        x = h_ref[...]
        r = lax.rsqrt(jnp.mean(x * x, axis=-1, keepdims=True) + RMS_EPS)
        u = (x * r * g_ref[...]).astype(BF16)
        u_ref[...] = u
        pc_ref[...] = _dot(u, w_ref[:, 0:1536])
        qkv_ref[...] = _dot(u, w_ref[:, 1536:3072]).astype(BF16)
        sbg_ref[...] = _dot(u, w_ref[:, 3072:3584])

    return pl.pallas_call(
        body, name="inproj_fwd", grid=(t // blk,),
        in_specs=[_rows(D_MODEL, rows=blk), _whole((1, D_MODEL)), _whole((D_MODEL, D_IN))],
        out_specs=[_rows(1536, rows=blk), _rows(1536, rows=blk), _rows(D_SB, rows=blk), _rows(D_MODEL, rows=blk)],
        out_shape=[jax.ShapeDtypeStruct((t, 1536), F32), jax.ShapeDtypeStruct((t, 1536), BF16),
                   jax.ShapeDtypeStruct((t, D_SB), F32), jax.ShapeDtypeStruct((t, D_MODEL), BF16)],
        compiler_params=_cparams("parallel"),
    )(h, g_pre, w_in)


def _prev_halo(col):
    per = ROW_BLK // HALO
    return pl.BlockSpec((HALO, D_CONV), lambda i, col=col: (jnp.maximum(i * per - 1, 0), col))


def _fill_glu(buf, i, a_ref, b_ref, ha_ref, hb_ref):
    halo = ha_ref[...] * _sigmoid(hb_ref[...])
    buf[0:HALO, :] = jnp.where(i > 0, halo, 0.0)
    buf[HALO:HALO + ROW_BLK, :] = a_ref[...] * _sigmoid(b_ref[...])


SUBLANES = 8
TAP_ROWS = 64
SHIFT_ROWS = HALO + ROW_BLK - SUBLANES


def _fill_shifts(shifts, buf):
    for b in range(1, SUBLANES):
        shifts[b - 1] = buf[pl.ds(b, SHIFT_ROWS), :]


def _window(buf, shifts, first, rows, lanes):
    whole, part = divmod(first, SUBLANES)
    src = buf if part == 0 else shifts.at[part - 1]
    return src[pl.ds(rows.start + SUBLANES * whole, rows.size), lanes]


TAP_ROW_CHUNKS = [pl.ds(r, TAP_ROWS) for r in range(0, ROW_BLK, TAP_ROWS)]
TAP_LANE_TILES = [pl.ds(c, LANES) for c in range(0, D_CONV, LANES)]


def _layer_norm_stats(cv):
    mu = jnp.mean(cv, axis=-1, keepdims=True)
    xc = cv - mu
    rstd = lax.rsqrt(jnp.mean(xc * xc, axis=-1, keepdims=True) + LN_EPS)
    return xc * rstd, rstd


def _conv_fwd(pc, conv_w, conv_b, ln_g, ln_b, w_pw2, b_pw2):
    t = pc.shape[0]

    def body(a_ref, b_ref, gate_ref, ha_ref, hb_ref, cw_ref, cb_ref, lg_ref, lb_ref, wp_ref, bp_ref,
             cout_ref, cv_ref, p_ref, sl_ref, buf, shifts):
        i = pl.program_id(0)
        _fill_glu(buf, i, a_ref, b_ref, ha_ref, hb_ref)
        _fill_shifts(shifts, buf)
        for lanes in TAP_LANE_TILES:
            for rows in TAP_ROW_CHUNKS:
                acc = jnp.zeros((TAP_ROWS, LANES), F32) + cb_ref[:, lanes]
                for j in range(CONV_WIDTH):
                    acc = acc + cw_ref[j:j + 1, lanes] * _window(buf, shifts, HALO - (CONV_WIDTH - 1) + j, rows, lanes)
                cv_ref[rows, lanes] = acc
        xh, _ = _layer_norm_stats(cv_ref[...])
        ln = xh * lg_ref[...] + lb_ref[...]
        sl = (ln * _sigmoid(ln)).astype(BF16)
        sl_ref[...] = sl
        p = _dot(sl, wp_ref[...]) + bp_ref[...]
        p_ref[...] = p
        gate = gate_ref[...]
        cout_ref[...] = (p * (gate * _sigmoid(gate))).astype(BF16)

    vec = _whole((1, D_CONV))
    return pl.pallas_call(
        body, name="conv_fwd", grid=(t // ROW_BLK,),
        in_specs=[_rows(D_CONV, 0), _rows(D_CONV, 1), _rows(D_CONV, 2), _prev_halo(0), _prev_halo(1),
                  _whole((CONV_WIDTH, D_CONV)), vec, vec, vec, _whole((D_CONV, D_CONV)), vec],
        out_specs=[_rows(D_CONV)] * 4,
        out_shape=[jax.ShapeDtypeStruct((t, D_CONV), BF16), jax.ShapeDtypeStruct((t, D_CONV), F32),
                   jax.ShapeDtypeStruct((t, D_CONV), F32), jax.ShapeDtypeStruct((t, D_CONV), BF16)],
        scratch_shapes=[pltpu.VMEM((HALO + ROW_BLK, D_CONV), F32), pltpu.VMEM((SUBLANES - 1, SHIFT_ROWS, D_CONV), F32)],
        compiler_params=_cparams("parallel"),
    )(pc, pc, pc, pc, pc, conv_w, conv_b, ln_g, ln_b, w_pw2, b_pw2)


def _lower_triangle():
    row = lax.broadcasted_iota(jnp.int32, (ROW_BLK, ROW_BLK), 0)
    col = lax.broadcasted_iota(jnp.int32, (ROW_BLK, ROW_BLK), 1)
    return row > col


def _lower_triangle_t():
    row = lax.broadcasted_iota(jnp.int32, (ROW_BLK, ROW_BLK), 0)
    col = lax.broadcasted_iota(jnp.int32, (ROW_BLK, ROW_BLK), 1)
    return row < col


def _tri_sum(x, umat):
    return _dot(x.astype(BF16), umat)


def _log_gates(z):
    ls = -(jnp.maximum(z, 0.0) + jnp.log(1.0 + jnp.exp(-jnp.abs(z))))
    return ls, z + ls


def _head_lanes(hh):
    lane = lax.broadcasted_iota(jnp.int32, (ROW_BLK, HEAD_BLK), 1)
    return (lane >= HEAD_DIM * hh) & (lane < HEAD_DIM * (hh + 1))


def _merge_heads(acc_ref):
    out = acc_ref[HEADS_PER_BLOCK - 1]
    for hh in range(HEADS_PER_BLOCK - 1):
        out = jnp.where(_head_lanes(hh), acc_ref[hh], out)
    return out


def _qkv_specs(t):
    n_blk = D_SB // HEAD_BLK
    return [pl.BlockSpec((ROW_BLK, HEAD_BLK), lambda hp, i: (i, hp)),
            pl.BlockSpec((t, HEAD_BLK), lambda hp, i: (0, n_blk + hp)),
            pl.BlockSpec((t, HEAD_BLK), lambda hp, i: (0, 2 * n_blk + hp))]


def _carry_spec():
    return pl.BlockSpec((HEADS_PER_BLOCK, ROW_BLK, LANES), lambda hp, i: (hp, i, 0))


def _last_block_rows(t, n_tokens):
    packed_rows = 16
    return -(-(n_tokens - (t - ROW_BLK)) // packed_rows) * packed_rows


def _by_block_rows(i, last_rows, sweep):
    if last_rows == ROW_BLK:
        sweep(ROW_BLK)
        return
    last = pl.num_programs(1) - 1
    pl.when(i < last)(lambda: sweep(ROW_BLK))
    pl.when(i == last)(lambda: sweep(last_rows))


def _attn_fwd(qkv, n_tokens):
    t = qkv.shape[0]
    assert t // ROW_BLK <= LANES

    def body(q_ref, k_ref, v_ref, o_ref, c_ref, acc_ref, run_ref, qm_ref, z_ref):
        i = pl.program_id(1)
        q = q_ref[...]
        heads = range(HEADS_PER_BLOCK)
        for hh in heads:
            qm_ref[hh] = jnp.where(_head_lanes(hh), q, jnp.zeros_like(q)) * jnp.asarray(SB_SCALE, BF16)
        acc_ref[...] = jnp.zeros_like(acc_ref)
        c_ref[...] = jnp.zeros_like(c_ref)
        run_ref[...] = jnp.zeros_like(run_ref)

        def sweep(n_rows):
            rows = pl.ds(0, n_rows)
            lower = _lower_triangle()[:n_rows]
            umat = jnp.where(_lower_triangle(), 1.0, 0.0).astype(BF16)
            lane = lax.broadcasted_iota(jnp.int32, (n_rows, LANES), 1)

            def scores(jb):
                start = pl.multiple_of(jb * ROW_BLK, ROW_BLK)
                kb = k_ref[pl.ds(start, ROW_BLK), :]
                for hh in heads:
                    z_ref[hh, rows] = _dot_nt(qm_ref[hh, rows], kb)

            def block(jb, diagonal):
                start = pl.multiple_of(jb * ROW_BLK, ROW_BLK)
                vb = v_ref[pl.ds(start, ROW_BLK), :]
                logits = []
                for hh in heads:
                    ls, lb = _log_gates(z_ref[hh, rows])
                    if diagonal:
                        ls = jnp.where(lower, ls, 0.0)
                    run = run_ref[hh, rows]
                    if not diagonal:
                        c_ref[hh, rows] = jnp.where(lane == jb, run, c_ref[hh, rows])
                    logits.append(lb + jnp.concatenate([run, run], axis=1) + _tri_sum(ls, umat))
                    run_ref[hh, rows] = run + jnp.sum(ls, axis=1, keepdims=True)
                scores(jnp.maximum(jb - 1, 0))
                for hh in heads:
                    a = jnp.exp(logits[hh])
                    if diagonal:
                        a = jnp.where(lower, a, 0.0)
                    acc_ref[hh, rows] += _dot(a.astype(BF16), vb)

            scores(i)
            block(i, True)

            @pl.loop(0, i)
            def _(n):
                block(i - 1 - n, False)

        _by_block_rows(i, _last_block_rows(t, n_tokens), sweep)
        o_ref[...] = _merge_heads(acc_ref)

    per_head = (HEADS_PER_BLOCK, ROW_BLK, HEAD_BLK)
    return pl.pallas_call(
        body, name="attn_fwd", grid=(D_SB // HEAD_BLK, t // ROW_BLK),
        in_specs=_qkv_specs(t),
        out_specs=[pl.BlockSpec((ROW_BLK, HEAD_BLK), lambda hp, i: (i, hp)), _carry_spec()],
        out_shape=[jax.ShapeDtypeStruct((t, D_SB), F32),
                   jax.ShapeDtypeStruct((D_SB // HEAD_DIM, t, LANES), F32)],
        scratch_shapes=[pltpu.VMEM(per_head, F32), pltpu.VMEM((HEADS_PER_BLOCK, ROW_BLK, LANES), F32),
                        pltpu.VMEM(per_head, BF16), pltpu.VMEM((HEADS_PER_BLOCK, ROW_BLK, ROW_BLK), F32)],
        compiler_params=_cparams("arbitrary", "arbitrary"),
    )(qkv, qkv, qkv)


def _outproj_fwd(h, cout, sraw, sbg, w_out, g_post):
    t = h.shape[0]
    blk = _dense_rows(t)

    def body(h_ref, c_ref, s_ref, g_ref, w_ref, gp_ref, hn_ref, mixed_ref, mix_ref):
        gate = g_ref[...]
        mix_ref[:, 0:D_CONV] = c_ref[...]
        mix_ref[:, D_CONV:] = (s_ref[...] * (gate * _sigmoid(gate))).astype(BF16)
        mixed = _dot(mix_ref[...], w_ref[...])
        mixed_ref[...] = mixed
        r = lax.rsqrt(jnp.mean(mixed * mixed, axis=-1, keepdims=True) + RMS_EPS)
        hn_ref[...] = h_ref[...] + mixed * r * gp_ref[...]

    return pl.pallas_call(
        body, name="outproj_fwd", grid=(t // blk,),
        in_specs=[_rows(D_MODEL, rows=blk), _rows(D_CONV, rows=blk), _rows(D_SB, rows=blk), _rows(D_SB, rows=blk),
                  _whole((D_MODEL, D_MODEL)), _whole((1, D_MODEL))],
        out_specs=[_rows(D_MODEL, rows=blk)] * 3,
        out_shape=[jax.ShapeDtypeStruct((t, D_MODEL), F32), jax.ShapeDtypeStruct((t, D_MODEL), F32),
                   jax.ShapeDtypeStruct((t, D_MODEL), BF16)],
        compiler_params=_cparams("parallel"),
    )(h, cout, sraw, sbg, w_out, g_post)


def _loss_and_grad(h, target, seq):
    t = h.shape[0]
    blk = _dense_rows(t)

    def body(h_ref, t_ref, loss_ref, dh_ref):
        i = pl.program_id(0)

        @pl.when(i == 0)
        def _():
            loss_ref[...] = jnp.zeros_like(loss_ref)

        row = i * blk + lax.broadcasted_iota(jnp.int32, (blk, D_MODEL), 0)
        real = (row >= N_META) & (row < N_META + seq)
        diff = jnp.where(real, h_ref[...] - t_ref[...], 0.0)
        sq = jnp.sum(jnp.sum(diff * diff, axis=1, keepdims=True), axis=0, keepdims=True)
        loss_ref[...] += (0.5 / D_MODEL) * sq
        dh_ref[...] = diff * (1.0 / D_MODEL)

    return pl.pallas_call(
        body, name="loss", grid=(t // blk,),
        in_specs=[_rows(D_MODEL, rows=blk), _rows(D_MODEL, rows=blk)],
        out_specs=[_whole((1, 1)), _rows(D_MODEL, rows=blk)],
        out_shape=[jax.ShapeDtypeStruct((1, 1), F32), jax.ShapeDtypeStruct((t, D_MODEL), F32)],
        compiler_params=_cparams("arbitrary"),
    )(h, target)


def _outproj_bwd(dh, mixed, g_post, sraw, sbg, w_out_t):
    t = dh.shape[0]
    blk = _dense_rows(t)

    def body(dh_ref, mixed_ref, gp_ref, s_ref, g_ref, wt_ref, dc_ref, ds_ref, dg_ref, dmb_ref, dgp_ref):
        @pl.when(pl.program_id(0) == 0)
        def _():
            dgp_ref[...] = jnp.zeros_like(dgp_ref)

        mixed = mixed_ref[...]
        r = lax.rsqrt(jnp.mean(mixed * mixed, axis=-1, keepdims=True) + RMS_EPS)
        nh = mixed * r
        dy = dh_ref[...]
        dgp_ref[...] += jnp.sum(dy * nh, axis=0, keepdims=True)
        dn = dy * gp_ref[...]
        dmixed = (r * (dn - nh * jnp.mean(dn * nh, axis=-1, keepdims=True))).astype(BF16)
        dmb_ref[...] = dmixed
        dmix = _dot_nt(dmixed, wt_ref[...])
        dc_ref[...] = dmix[:, 0:D_CONV]
        dsg = dmix[:, D_CONV:]
        gate = g_ref[...]
        sg = _sigmoid(gate)
        ds_ref[...] = dsg * (gate * sg)
        dg_ref[...] = (dsg * s_ref[...] * _dsilu(gate, sg)).astype(BF16)

    return pl.pallas_call(
        body, name="outproj_bwd", grid=(t // blk,),
        in_specs=[_rows(D_MODEL, rows=blk), _rows(D_MODEL, rows=blk), _whole((1, D_MODEL)), _rows(D_SB, rows=blk),
                  _rows(D_SB, rows=blk), _whole((D_MODEL, D_MODEL))],
        out_specs=[_rows(D_CONV, rows=blk), _rows(D_SB, rows=blk), _rows(D_SB, rows=blk), _rows(D_MODEL, rows=blk),
                   _whole((1, D_MODEL))],
        out_shape=[jax.ShapeDtypeStruct((t, D_CONV), F32), jax.ShapeDtypeStruct((t, D_SB), F32),
                   jax.ShapeDtypeStruct((t, D_SB), BF16), jax.ShapeDtypeStruct((t, D_MODEL), BF16),
                   jax.ShapeDtypeStruct((1, D_MODEL), F32)],
        compiler_params=_cparams("arbitrary"),
    )(dh, mixed, g_post, sraw, sbg, w_out_t)


def _attn_bwd(qkv, carries, do, n_tokens):
    t = qkv.shape[0]

    def body(q_ref, k_ref, v_ref, c_ref, do_ref, dq_ref, dk_ref, dv_ref, acc_ref, seen_ref, qm_ref, dom_ref, z_ref,
             da_ref, dz_ref, a_ref):
        i = pl.program_id(1)

        @pl.when(i == 0)
        def _():
            dk_ref[...] = jnp.zeros_like(dk_ref)
            dv_ref[...] = jnp.zeros_like(dv_ref)

        q = q_ref[...]
        dof = do_ref[...]
        heads = range(HEADS_PER_BLOCK)
        for hh in heads:
            qm_ref[hh] = jnp.where(_head_lanes(hh), q, jnp.zeros_like(q)) * jnp.asarray(SB_SCALE, BF16)
            dom_ref[hh] = jnp.where(_head_lanes(hh), dof, 0.0).astype(BF16)
        acc_ref[...] = jnp.zeros_like(acc_ref)
        seen_ref[...] = jnp.zeros_like(seen_ref)

        def sweep(n_rows):
            rows = pl.ds(0, n_rows)
            lower = _lower_triangle()[:n_rows]
            umat = jnp.where(_lower_triangle(), 1.0, 0.0).astype(BF16)
            umat_t = jnp.where(_lower_triangle_t(), 1.0, 0.0).astype(BF16)
            lane = lax.broadcasted_iota(jnp.int32, (n_rows, LANES), 1)

            def scores(jb):
                start = pl.multiple_of(jb * ROW_BLK, ROW_BLK)
                kb = k_ref[pl.ds(start, ROW_BLK), :]
                for hh in heads:
                    z_ref[hh, rows] = _dot_nt(qm_ref[hh, rows], kb)

            def value_grads(jb):
                start = pl.multiple_of(jb * ROW_BLK, ROW_BLK)
                vb = v_ref[pl.ds(start, ROW_BLK), :]
                for hh in heads:
                    da_ref[hh, rows] = _dot_nt(dom_ref[hh, rows], vb)

            def products(jb, hh):
                start = pl.multiple_of(jb * ROW_BLK, ROW_BLK)
                dzb = dz_ref[hh, rows]
                acc_ref[hh, rows] += _dot(dzb, k_ref[pl.ds(start, ROW_BLK), :])
                dk_ref[pl.ds(start, ROW_BLK), :] += _dot_tn(dzb, qm_ref[hh, rows])
                dv_ref[pl.ds(start, ROW_BLK), :] += _dot_tn(a_ref[hh, rows], dom_ref[hh, rows])

            def block(jb, diagonal):
                before = jnp.maximum(jb - 1, 0)
                lbs, logits = [], []
                for hh in heads:
                    products(before, hh)
                    ls, lb = _log_gates(z_ref[hh, rows])
                    if diagonal:
                        ls = jnp.where(lower, ls, 0.0)
                        logits.append(lb + _tri_sum(ls, umat))
                    else:
                        right = jnp.sum(jnp.where(lane == jb, c_ref[hh, rows], 0.0), axis=1, keepdims=True)
                        logits.append(lb + right + _tri_sum(ls, umat))
                    lbs.append(lb)
                if not diagonal:
                    scores(jb + 1)
                gs, befores = [], []
                for hh in heads:
                    a = jnp.exp(logits[hh])
                    if diagonal:
                        a = jnp.where(lower, a, 0.0)
                    g = da_ref[hh, rows] * a
                    seen = seen_ref[hh, rows]
                    befores.append(jnp.concatenate([seen, seen], axis=1) + _tri_sum(g, umat_t))
                    seen_ref[hh, rows] = seen + jnp.sum(g, axis=1, keepdims=True)
                    a_ref[hh, rows] = a.astype(BF16)
                    gs.append(g)
                if not diagonal:
                    value_grads(jb + 1)
                for hh in heads:
                    dz = gs[hh] - jnp.exp(lbs[hh]) * (gs[hh] + befores[hh])
                    if diagonal:
                        dz = jnp.where(lower, dz, 0.0)
                    dz_ref[hh, rows] = dz.astype(BF16)

            dz_ref[...] = jnp.zeros_like(dz_ref)
            a_ref[...] = jnp.zeros_like(a_ref)
            scores(0)
            value_grads(0)

            @pl.loop(0, i)
            def _(jb):
                block(jb, False)

            block(i, True)
            for hh in heads:
                products(i, hh)

        _by_block_rows(i, _last_block_rows(t, n_tokens), sweep)
        dq_ref[...] = (_merge_heads(acc_ref) * SB_SCALE).astype(BF16)

    blk = pl.BlockSpec((ROW_BLK, HEAD_BLK), lambda hp, i: (i, hp))
    full = pl.BlockSpec((t, HEAD_BLK), lambda hp, i: (0, hp))
    per_head = (HEADS_PER_BLOCK, ROW_BLK, HEAD_BLK)
    return pl.pallas_call(
        body, name="attn_bwd", grid=(D_SB // HEAD_BLK, t // ROW_BLK),
        in_specs=_qkv_specs(t) + [_carry_spec(), blk],
        out_specs=[blk, full, full],
        out_shape=[jax.ShapeDtypeStruct((t, D_SB), BF16)] + [jax.ShapeDtypeStruct((t, D_SB), F32)] * 2,
        scratch_shapes=[pltpu.VMEM(per_head, F32), pltpu.VMEM((HEADS_PER_BLOCK, ROW_BLK, LANES), F32),
                        pltpu.VMEM(per_head, BF16), pltpu.VMEM(per_head, BF16)]
                       + [pltpu.VMEM((HEADS_PER_BLOCK, ROW_BLK, ROW_BLK), dtype) for dtype in (F32, F32, BF16, BF16)],
        compiler_params=_cparams("arbitrary", "arbitrary"),
    )(qkv, qkv, qkv, carries, do)


def _conv_bwd_rows(dcout, pc, cv, p, ln_g, ln_b, w_pw2_t):
    t = dcout.shape[0]
    blk = _dense_rows(t)

    def body(dc_ref, gate_ref, cv_ref, p_ref, lg_ref, lb_ref, wt_ref, dcv_ref, dgate_ref, dpb_ref, vec_ref):
        @pl.when(pl.program_id(0) == 0)
        def _():
            vec_ref[...] = jnp.zeros_like(vec_ref)

        dc = dc_ref[...]
        gate = gate_ref[...]
        sg = _sigmoid(gate)
        dp = dc * (gate * sg)
        dgate_ref[...] = (dc * p_ref[...] * _dsilu(gate, sg)).astype(BF16)
        dpb = dp.astype(BF16)
        dpb_ref[...] = dpb
        xh, rstd = _layer_norm_stats(cv_ref[...])
        ln = xh * lg_ref[...] + lb_ref[...]
        s2 = _sigmoid(ln)
        dln = _dot_nt(dpb, wt_ref[...]) * _dsilu(ln, s2)
        dxh = dln * lg_ref[...]
        dcv = rstd * (dxh - jnp.mean(dxh, axis=-1, keepdims=True)
                      - xh * jnp.mean(dxh * xh, axis=-1, keepdims=True))
        dcv_ref[...] = dcv
        vec_ref[0:1, :] += jnp.sum(dp, axis=0, keepdims=True)
        vec_ref[1:2, :] += jnp.sum(dln * xh, axis=0, keepdims=True)
        vec_ref[2:3, :] += jnp.sum(dln, axis=0, keepdims=True)
        vec_ref[3:4, :] += jnp.sum(dcv, axis=0, keepdims=True)

    vec = _whole((1, D_CONV))
    return pl.pallas_call(
        body, name="conv_bwd_rows", grid=(t // blk,),
        in_specs=[_rows(D_CONV, rows=blk), _rows(D_CONV, 2, rows=blk), _rows(D_CONV, rows=blk), _rows(D_CONV, rows=blk),
                  vec, vec, _whole((D_CONV, D_CONV))],
        out_specs=[_rows(D_CONV, rows=blk), _rows(D_CONV, rows=blk), _rows(D_CONV, rows=blk), _whole((8, D_CONV))],
        out_shape=[jax.ShapeDtypeStruct((t, D_CONV), F32), jax.ShapeDtypeStruct((t, D_CONV), BF16),
                   jax.ShapeDtypeStruct((t, D_CONV), BF16), jax.ShapeDtypeStruct((8, D_CONV), F32)],
        compiler_params=_cparams("arbitrary"),
    )(dcout, pc, cv, p, ln_g, ln_b, w_pw2_t)


def _conv_bwd_taps(dcv, pc, conv_w):
    t = dcv.shape[0]
    n_halo = t // HALO
    per = ROW_BLK // HALO

    def body(d_ref, dn_ref, a_ref, b_ref, ha_ref, hb_ref, cw_ref, da_ref, db_ref, dw_ref, cbuf, dbuf, cshifts, dshifts):
        i = pl.program_id(0)

        @pl.when(i == 0)
        def _():
            dw_ref[...] = jnp.zeros_like(dw_ref)

        _fill_glu(cbuf, i, a_ref, b_ref, ha_ref, hb_ref)
        dbuf[0:ROW_BLK, :] = d_ref[...]
        dbuf[ROW_BLK:ROW_BLK + HALO, :] = jnp.where(i < pl.num_programs(0) - 1, dn_ref[...], 0.0)
        _fill_shifts(cshifts, cbuf)
        _fill_shifts(dshifts, dbuf)
        for lanes in TAP_LANE_TILES:
            for rows in TAP_ROW_CHUNKS:
                acc = jnp.zeros((TAP_ROWS, LANES), F32)
                for j in range(CONV_WIDTH):
                    acc = acc + cw_ref[j:j + 1, lanes] * _window(dbuf, dshifts, CONV_WIDTH - 1 - j, rows, lanes)
                sb = _sigmoid(b_ref[rows, lanes])
                da_ref[rows, lanes] = (acc * sb).astype(BF16)
                db_ref[rows, lanes] = (acc * a_ref[rows, lanes] * sb * (1.0 - sb)).astype(BF16)
            for j in range(CONV_WIDTH):
                acc = jnp.zeros((TAP_ROWS, LANES), F32)
                for rows in TAP_ROW_CHUNKS:
                    acc = acc + d_ref[rows, lanes] * _window(cbuf, cshifts, HALO - (CONV_WIDTH - 1) + j, rows, lanes)
                dw_ref[j:j + 1, lanes] += jnp.sum(acc, axis=0, keepdims=True)

    return pl.pallas_call(
        body, name="conv_bwd_taps", grid=(t // ROW_BLK,),
        in_specs=[_rows(D_CONV),
                  pl.BlockSpec((HALO, D_CONV), lambda i: (jnp.minimum((i + 1) * per, n_halo - 1), 0)),
                  _rows(D_CONV, 0), _rows(D_CONV, 1), _prev_halo(0), _prev_halo(1),
                  _whole((CONV_WIDTH, D_CONV))],
        out_specs=[_rows(D_CONV), _rows(D_CONV), _whole((32, D_CONV))],
        out_shape=[jax.ShapeDtypeStruct((t, D_CONV), BF16), jax.ShapeDtypeStruct((t, D_CONV), BF16),
                   jax.ShapeDtypeStruct((32, D_CONV), F32)],
        scratch_shapes=[pltpu.VMEM((HALO + ROW_BLK, D_CONV), F32), pltpu.VMEM((ROW_BLK + HALO, D_CONV), F32),
                        pltpu.VMEM((SUBLANES - 1, SHIFT_ROWS, D_CONV), F32),
                        pltpu.VMEM((SUBLANES - 1, SHIFT_ROWS, D_CONV), F32)],
        compiler_params=_cparams("arbitrary"),
    )(dcv, dcv, pc, pc, pc, pc, conv_w)


def _inproj_bwd(dh_out, h, g_pre, pieces, w_in_t):
    t = h.shape[0]
    blk = _dense_rows(t)

    def body(dh_ref, h_ref, g_ref, *rest):
        piece_refs, (wt_ref, dhin_ref, dproj_ref, dg_ref) = rest[:7], rest[7:]

        @pl.when(pl.program_id(0) == 0)
        def _():
            dg_ref[...] = jnp.zeros_like(dg_ref)

        for k, ref in enumerate(piece_refs):
            dproj_ref[:, 512 * k:512 * (k + 1)] = ref[...].astype(BF16)
        du = _dot_nt(dproj_ref[...], wt_ref[...])
        x = h_ref[...]
        r = lax.rsqrt(jnp.mean(x * x, axis=-1, keepdims=True) + RMS_EPS)
        xh = x * r
        dg_ref[...] += jnp.sum(du * xh, axis=0, keepdims=True)
        dxh = du * g_ref[...]
        dhin_ref[...] = dh_ref[...] + r * (dxh - xh * jnp.mean(dxh * xh, axis=-1, keepdims=True))

    return pl.pallas_call(
        body, name="inproj_bwd", grid=(t // blk,),
        in_specs=[_rows(D_MODEL, rows=blk), _rows(D_MODEL, rows=blk), _whole((1, D_MODEL))] + [_rows(512, rows=blk)] * 7
                 + [_whole((D_MODEL, D_IN))],
        out_specs=[_rows(D_MODEL, rows=blk), _rows(D_IN, rows=blk), _whole((1, D_MODEL))],
        out_shape=[jax.ShapeDtypeStruct((t, D_MODEL), F32), jax.ShapeDtypeStruct((t, D_IN), BF16),
                   jax.ShapeDtypeStruct((1, D_MODEL), F32)],
        compiler_params=_cparams("arbitrary"),
    )(dh_out, h, g_pre, *pieces, w_in_t)


def _weight_grad(xb, dys, name):
    t, k = xb.shape
    widths = [dy.shape[1] for dy in dys]
    n = sum(widths)

    def body(x_ref, *rest):
        dy_refs, (o_ref, acc_ref, dy_buf) = rest[:len(dys)], rest[len(dys):]
        i = pl.program_id(0)

        @pl.when(i == 0)
        def _():
            acc_ref[...] = jnp.zeros_like(acc_ref)

        at = 0
        for ref, width in zip(dy_refs, widths):
            dy_buf[:, at:at + width] = ref[...].astype(BF16)
            at += width
        acc_ref[...] += _dot_tn(x_ref[...], dy_buf[...])

        @pl.when(i == pl.num_programs(0) - 1)
        def _():
            o_ref[...] = acc_ref[...].astype(BF16)

    return pl.pallas_call(
        body, name=name, grid=(t // ROW_BLK,),
        in_specs=[_rows(k)] + [_rows(width) for width in widths], out_specs=_whole((k, n)),
        out_shape=jax.ShapeDtypeStruct((k, n), BF16),
        scratch_shapes=[pltpu.VMEM((k, n), F32), pltpu.VMEM((ROW_BLK, n), BF16)],
        compiler_params=_cparams("arbitrary"),
    )(xb, *dys)


def _position():
    return lax.axis_index("x"), lax.axis_index("y"), lax.axis_index("c")


def _comm_call(body, name, ins, out_shapes):
    n = len(ins)
    hbm = pl.BlockSpec(memory_space=pltpu.HBM)
    return pl.pallas_call(
        functools.partial(body, n), name=name, in_specs=[hbm] * n, out_specs=[hbm] * n, out_shape=out_shapes,
        scratch_shapes=[pltpu.SemaphoreType.DMA((n, N_DEV - 1)), pltpu.SemaphoreType.DMA((n, N_DEV - 1)),
                        pltpu.SemaphoreType.DMA((n,))],
    )(*ins)


def _all_gather(blocks, name):
    def body(n, *refs):
        x_refs, out_refs, (send_sems, recv_sems, local_sems) = refs[:n], refs[n:2 * n], refs[2 * n:]
        x, y, c = _position()
        me, sibling = (x, y, c), (x, y, 1 - c)
        chips = [(1 - x, y), (x, 1 - y), (1 - x, 1 - y)]

        def slot(a, px, py, pc):
            return out_refs[a].at[4 * px + 2 * py + pc]

        def copy(a, k, origin, to, own=False):
            return pltpu.make_async_remote_copy(
                src_ref=x_refs[a] if own else slot(a, *origin), dst_ref=slot(a, *origin),
                send_sem=send_sems.at[a, k], recv_sem=recv_sems.at[a, k], device_id=to, device_id_type=MESH)

        arrays = range(n)
        mine = [pltpu.make_async_copy(x_refs[a], slot(a, *me), local_sems.at[a]) for a in arrays]
        first = [copy(a, 1 + j, me, (*chip, c), own=True) for j, chip in enumerate(chips) for a in arrays]
        first += [copy(a, 0, me, sibling, own=True) for a in arrays]
        for cp in mine + first:
            cp.start()
        passed = []
        for j, chip in enumerate(chips):
            for a in arrays:
                copy(a, 1 + j, (*chip, c), me).wait_recv()
                passed.append(copy(a, 4 + j, (*chip, c), sibling))
                passed[-1].start()
        for a in arrays:
            copy(a, 0, sibling, me).wait_recv()
            for j, chip in enumerate(chips):
                copy(a, 4 + j, (*chip, 1 - c), me).wait_recv()
        for cp in first + passed:
            cp.wait_send()
        for cp in mine:
            cp.wait()

    return _comm_call(body, name, blocks, [jax.ShapeDtypeStruct((N_DEV,) + b.shape, b.dtype) for b in blocks])


def _exchange_copies(g_refs, land_refs, sems, gather):
    x, y, c = _position()
    me = 4 * x + 2 * y + c
    out = []
    for g_ref, land_ref, (send_sem, recv_sem, local_sem) in zip(g_refs, land_refs, sems):
        def mine(slot, g_ref=g_ref):
            return g_ref if gather else g_ref.at[slot]

        def remote(src, dst, dev):
            return pltpu.make_async_remote_copy(src_ref=src, dst_ref=dst, send_sem=send_sem, recv_sem=recv_sem,
                                                device_id=dev, device_id_type=MESH)

        sends = []
        for k in range(1, N_DEV):
            px = 1 - x if k & 4 else x
            py = 1 - y if k & 2 else y
            pc = 1 - c if k & 1 else c
            sends.append(remote(mine(4 * px + 2 * py + pc), land_ref.at[me], (px, py, pc)))
        seven = land_ref.at[pl.ds(0, N_DEV - 1)]
        out.append((pltpu.make_async_copy(mine(me), land_ref.at[me], local_sem), sends, remote(seven, seven, (x, y, c))))
    return out


_HBM = pl.BlockSpec(memory_space=pltpu.HBM)
_SEM = pl.BlockSpec(memory_space=pltpu.SEMAPHORE)
_ORDERED = pltpu.CompilerParams(has_side_effects=pltpu.SideEffectType.DATAFLOW_SIDE_EFFECTING)
SEMS_PER_ARRAY = 3


def _exchange_start(arrays, after, name, gather):
    n = len(arrays)
    n_sems = SEMS_PER_ARRAY * n

    def body(*refs):
        g_refs, land_refs, sems, token = refs[:n], refs[n:2 * n], refs[2 * n + 1:2 * n + 1 + n_sems], refs[-1]
        sems = [sems[SEMS_PER_ARRAY * a:SEMS_PER_ARRAY * (a + 1)] for a in range(n)]
        for local, sends, _ in _exchange_copies(g_refs, land_refs, sems, gather):
            local.start()
            for cp in sends:
                cp.start()
        token[...] = jnp.zeros_like(token)

    buffers = list(arrays) + [lax.empty((N_DEV,) + g.shape if gather else g.shape, g.dtype) for g in arrays]
    outs = pl.pallas_call(
        body, name=name, in_specs=[_HBM] * (2 * n) + [pl.BlockSpec(memory_space=pl.ANY)],
        out_specs=[_SEM] * n_sems + [_HBM] * (2 * n) + [pl.BlockSpec(memory_space=pltpu.VMEM)],
        out_shape=[pltpu.SemaphoreType.DMA(())] * n_sems + [pltpu.HBM(b.shape, b.dtype) for b in buffers]
                  + [jax.ShapeDtypeStruct((8, LANES), F32)],
        input_output_aliases={a: n_sems + a for a in range(2 * n)}, compiler_params=_ORDERED,
    )(*[pltpu.with_memory_space_constraint(b, pltpu.HBM) for b in buffers], after)
    return outs[:n_sems], outs[n_sems:n_sems + n], outs[n_sems + n:n_sems + 2 * n], outs[-1]


def _exchange_wait(sems, arrays, landings, after, name, gather):
    n = len(arrays)
    n_sems = SEMS_PER_ARRAY * n

    def body(*refs):
        g_refs, land_refs, sems = refs[:n], refs[n:2 * n], refs[2 * n:2 * n + n_sems]
        sems = [sems[SEMS_PER_ARRAY * a:SEMS_PER_ARRAY * (a + 1)] for a in range(n)]
        for local, _, all_seven in _exchange_copies(g_refs, land_refs, sems, gather):
            all_seven.wait_recv()
            all_seven.wait_send()
            local.wait()

    buffers = list(arrays) + list(landings)
    outs = pl.pallas_call(
        body, name=name, in_specs=[_HBM] * (2 * n) + [_SEM] * n_sems + [pl.BlockSpec(memory_space=pl.ANY)],
        out_specs=[_HBM] * (2 * n), out_shape=[pltpu.HBM(b.shape, b.dtype) for b in buffers],
        input_output_aliases={a: a for a in range(2 * n)}, compiler_params=_ORDERED,
    )(*buffers, *sems, after)
    return outs[n:]


def _block_rows(r, row_bytes, budget=1 << 20):
    cap = max(8, budget // row_bytes)
    return max(d for d in range(8, min(r, cap) + 1, 8) if r % d == 0)


def _sum_adamw(parts, w, m, v, name):
    n_parts, r, c = parts.shape
    br = _block_rows(r, 4 * c)

    def body(p_ref, w_ref, m_ref, v_ref, g_out, d_out, m_out, v_out):
        g = p_ref[0].astype(F32)
        for s in range(1, n_parts):
            g = g + p_ref[s].astype(F32)
        m_new = ADAM_B1 * m_ref[...] + (1.0 - ADAM_B1) * g
        v_new = ADAM_B2 * v_ref[...] + (1.0 - ADAM_B2) * (g * g)
        m_hat = m_new / (1.0 - ADAM_B1 ** ADAM_STEP)
        v_hat = v_new / (1.0 - ADAM_B2 ** ADAM_STEP)
        g_out[...] = g
        d_out[...] = -ADAM_LR * (m_hat / (jnp.sqrt(v_hat) + ADAM_EPS) + ADAM_WD * w_ref[...])
        m_out[...] = m_new
        v_out[...] = v_new

    row = pl.BlockSpec((br, c), lambda i: (i, 0))
    return pl.pallas_call(
        body, name=name, grid=(r // br,),
        in_specs=[pl.BlockSpec((n_parts, br, c), lambda i: (0, i, 0)), row, row, row],
        out_specs=[row] * 4, out_shape=[jax.ShapeDtypeStruct((r, c), F32)] * 4,
        compiler_params=_cparams("parallel"),
    )(parts, w, m, v)


def _sum_parts(parts, name):
    n_parts, r, c = parts.shape

    def body(p_ref, o_ref):
        g = p_ref[0]
        for s in range(1, n_parts):
            g = g + p_ref[s]
        o_ref[...] = g

    return pl.pallas_call(
        body, name=name, in_specs=[pl.BlockSpec(memory_space=pltpu.VMEM)],
        out_specs=pl.BlockSpec(memory_space=pltpu.VMEM), out_shape=jax.ShapeDtypeStruct((r, c), F32),
    )(parts)


def _pack(arrays):
    flat = jnp.concatenate([a.reshape(-1) for a in arrays])
    pad = -flat.shape[0] % (8 * LANES)
    if pad:
        flat = jnp.pad(flat, (0, pad))
    return flat.reshape(-1, LANES)


def _unpack(buf, shapes):
    flat = buf.reshape(-1)
    out, at = [], 0
    for shape in shapes:
        size = 1
        for d in shape:
            size *= d
        out.append(lax.slice_in_dim(flat, at, at + size).reshape(shape))
        at += size
    return out


def _local_step(x, target, meta, pre_g, post_g, conv_w, conv_b, ln_g, ln_b, b_pw2, weights, ship, ship_small):
    depth = pre_g.shape[0]
    seq = x.shape[0]
    t = -(-(N_META + seq) // ROW_BLK) * ROW_BLK
    tail = t - N_META - seq
    h = jnp.concatenate([meta, x, jnp.zeros((tail, D_MODEL), F32)], axis=0)
    target = jnp.pad(target, ((N_META, tail), (0, 0)))
    row = lambda a, l: a[l][None, :]

    saved = []
    for l in range(depth):
        w_in, w_pw2, w_out = weights(l, h)
        pc, qkv, sbg, u = _inproj_fwd(h, row(pre_g, l), w_in)
        cout, cv, p, sl = _conv_fwd(pc, conv_w[l], row(conv_b, l), row(ln_g, l), row(ln_b, l), w_pw2, row(b_pw2, l))
        sraw, carries = _attn_fwd(qkv, N_META + seq)
        h_new, mixed, mix = _outproj_fwd(h, cout, sraw, sbg, w_out, row(post_g, l))
        saved.append((h, pc, qkv, sbg, u, cv, p, sl, sraw, carries, mixed, mix, w_in, w_pw2, w_out))
        h = h_new

    loss, dh = _loss_and_grad(h, target, seq)

    grads = {k: [None] * depth for k in ("pre_g", "post_g", "conv_w", "conv_b", "ln_g", "ln_b", "b_pw2")}
    token = jnp.zeros((8, LANES), F32)
    for l in reversed(range(depth)):
        h_in, pc, qkv, sbg, u, cv, p, sl, sraw, carries, mixed, mix, w_in_t, w_pw2_t, w_out_t = saved[l]
        dcout, dsraw, dsbg, dmixed, dg_post = _outproj_bwd(dh, mixed, row(post_g, l) + token[:1, :1], sraw, sbg, w_out_t)
        dq, dk, dv = _attn_bwd(qkv, carries, dsraw, N_META + seq)
        dcv, dgate, dpb, vecs = _conv_bwd_rows(dcout, pc, cv, p, row(ln_g, l), row(ln_b, l), w_pw2_t)
        late = {"w_pw2": (sl, [dpb]), "w_out": (mix, [dmixed])}
        taps = conv_w[l]
        if l == 0:
            token = ship(l, dcv, {k: _weight_grad(*late.pop(k), k + "_grad") for k in ("w_pw2", "w_out")})
            taps = taps + token[:1, :1]
        da, db, dconv_w = _conv_bwd_taps(dcv, pc, taps)
        pieces = (da, db, dgate, dq, dk, dv, dsbg)
        gain = row(pre_g, l)
        if l == 0:
            token = ship(l, da, {"w_in": _weight_grad(u, pieces, "w_in_grad")})
            gain = gain + token[:1, :1]
        dh, dproj, dg_pre = _inproj_bwd(dh, h_in, gain, pieces, w_in_t)
        grads["pre_g"][l] = dg_pre[0]
        grads["post_g"][l] = dg_post[0]
        grads["b_pw2"][l], grads["ln_g"][l], grads["ln_b"][l], grads["conv_b"][l] = vecs[0], vecs[1], vecs[2], vecs[3]
        grads["conv_w"][l] = dconv_w[:CONV_WIDTH]
        if l == 0:
            grads = {k: jnp.stack(v) for k, v in grads.items()}
            grads["meta"] = dh[:N_META]
            token = ship_small(grads, loss[0, 0])
        else:
            late["w_in"] = (u, [dproj])
            token = ship(l, dh, {k: _weight_grad(*late[k], k + "_grad") for k in ("w_in", "w_pw2", "w_out")})

    return dh[N_META:N_META + seq], token


def _shard_major(full, axis):
    shape = full.shape
    split = full.reshape(shape[:axis] + (N_DEV, shape[axis] // N_DEV) + shape[axis + 1:])
    return jnp.moveaxis(split, axis, 0)


def _whole_from_shards(shards, axis):
    moved = jnp.moveaxis(shards, 0, axis)
    shape = moved.shape
    return moved.reshape(shape[:axis] + (shape[axis] * shape[axis + 1],) + shape[axis + 2:])


def kernel(x, meta_tokens, pre_norm_g, post_norm_g, w_in, conv_w, conv_b, conv_ln_g, conv_ln_b, w_pw2, b_pw2, w_out, loss_target, m_meta_tokens, m_pre_norm_g, m_post_norm_g, m_w_in, m_conv_w, m_conv_b, m_conv_ln_g, m_conv_ln_b, m_w_pw2, m_b_pw2, m_w_out, v_meta_tokens, v_pre_norm_g, v_post_norm_g, v_w_in, v_conv_w, v_conv_b, v_conv_ln_g, v_conv_ln_b, v_w_pw2, v_b_pw2, v_w_out):
    me = 4 * lax.axis_index("x") + 2 * lax.axis_index("y") + lax.axis_index("c")

    depth = w_in.shape[0]
    big = [w.astype(BF16) for w in (w_in, w_pw2, w_out)]
    *first, conv_w_s, meta_s = _all_gather([w[0] for w in big] + [conv_w, meta_tokens], "gather_first_layer")
    *gathering, token = _exchange_start([w[l] for l in range(1, depth) for w in big], meta_s, "gather_start", gather=True)
    conv_w_full = _whole_from_shards(conv_w_s, 2)
    meta_full = _whole_from_shards(meta_s, 1)
    shard_axis = (1, 0, 0)
    later = []

    def weights(l, h):
        if l == 0:
            return [_whole_from_shards(s, axis) for s, axis in zip(first, shard_axis)]
        if not later:
            later.extend(_exchange_wait(*gathering, h, "gather_wait", gather=True))
        return [_whole_from_shards(s, axis) for s, axis in zip(later[len(big) * (l - 1):len(big) * l], shard_axis)]

    in_flight = [[] for _ in range(depth)]
    grad_axis = dict(zip(("w_in", "w_pw2", "w_out"), shard_axis))

    def ship(l, after, dws):
        slabs = [_shard_major(dw, grad_axis[k]) for k, dw in dws.items()]
        sems, arrays, landings, token = _exchange_start(slabs, after, f"exchange_start_{l}_{len(in_flight[l])}",
                                                        gather=False)
        in_flight[l].append((list(dws), sems, arrays, landings))
        return token

    small_names = ("pre_g", "post_g", "conv_b", "ln_g", "ln_b", "b_pw2", "conv_w", "meta")
    small_in_flight, small_shapes_full = [], []

    def ship_small(grads, loss):
        small_full = [grads[k] for k in small_names] + [loss.reshape(1)]
        small_shapes_full.extend(a.shape for a in small_full)
        *in_flight_now, token = _exchange_start([_pack(small_full)], grads["meta"], "small_grads_start", gather=True)
        small_in_flight.extend(in_flight_now)
        return token

    dx, shipped = _local_step(x[0], loss_target[0], meta_full, pre_norm_g + token[:1, :1], post_norm_g, conv_w_full,
                              conv_b, conv_ln_g, conv_ln_b, b_pw2, weights, ship, ship_small)

    updated = [None] * depth
    done = shipped

    def update_layer(l):
        landed = {}
        for k, (names, sems, arrays, landings) in enumerate(in_flight[l]):
            landed.update(zip(names, _exchange_wait(sems, arrays, landings, done, f"exchange_wait_{l}_{k}", gather=False)))
        return [_sum_adamw(landed[name], w[l], m[l], v[l], "adamw_" + name) for name, w, m, v in (
            ("w_in", w_in, m_w_in, v_w_in), ("w_pw2", w_pw2, m_w_pw2, v_w_pw2), ("w_out", w_out, m_w_out, v_w_out))]

    for l in reversed(range(1, depth)):
        updated[l] = update_layer(l)
        done = updated[l][0][1]

    gathered, = _exchange_wait(*small_in_flight, done, "small_grads_wait", gather=True)
    summed = _unpack(_sum_parts(gathered, "sum_small_grads"), small_shapes_full)
    loss = summed[-1][0]
    g_small = dict(zip(small_names, summed))
    g_small["conv_w"] = lax.dynamic_slice_in_dim(g_small["conv_w"], me * conv_w.shape[2], conv_w.shape[2], axis=2)
    g_small["meta"] = lax.dynamic_slice_in_dim(g_small["meta"], me * meta_tokens.shape[1], meta_tokens.shape[1], axis=1)
    small_w = dict(zip(small_names, (pre_norm_g, post_norm_g, conv_b, conv_ln_g, conv_ln_b, b_pw2, conv_w, meta_tokens)))
    small_m = (m_pre_norm_g, m_post_norm_g, m_conv_b, m_conv_ln_g, m_conv_ln_b, m_b_pw2, m_conv_w, m_meta_tokens)
    small_v = (v_pre_norm_g, v_post_norm_g, v_conv_b, v_conv_ln_g, v_conv_ln_b, v_b_pw2, v_conv_w, v_meta_tokens)
    small_shapes = [small_w[k].shape for k in small_names]
    outs = _sum_adamw(_pack([g_small[k] for k in small_names])[None], _pack([small_w[k] for k in small_names]),
                      _pack(small_m), _pack(small_v), "adamw_small_weights")
    g_s, d_s, nm_s, nv_s = [dict(zip(small_names, _unpack(o, small_shapes))) for o in outs]

    done = outs[1]
    updated[0] = update_layer(0)
    (g_w_in, d_w_in, nm_w_in, nv_w_in), (g_w_pw2, d_w_pw2, nm_w_pw2, nv_w_pw2), (g_w_out, d_w_out, nm_w_out, nv_w_out) = [
        [jnp.stack([updated[l][a][k] for l in range(depth)]) for k in range(4)] for a in range(3)]

    def ordered(s, w_in_, w_pw2_, w_out_):
        return (s["meta"], s["pre_g"], s["post_g"], w_in_, s["conv_w"], s["conv_b"], s["ln_g"], s["ln_b"], w_pw2_,
                s["b_pw2"], w_out_)

    return (loss, dx[None], *ordered(g_s, g_w_in, g_w_pw2, g_w_out), *ordered(d_s, d_w_in, d_w_pw2, d_w_out),
            *ordered(nm_s, nm_w_in, nm_w_pw2, nm_w_out), *ordered(nv_s, nv_w_in, nv_w_pw2, nv_w_out))
```

```python
import functools

import jax
import jax.numpy as jnp
from jax import lax
from jax.experimental import pallas as pl
from jax.experimental.pallas import tpu as pltpu

F32 = jnp.float32
BF16 = jnp.bfloat16

D_MODEL = 1024
D_CONV = 512
D_SB = 512
HEAD_DIM = 64
HEADS_PER_BLOCK = 4
HEAD_BLK = HEADS_PER_BLOCK * HEAD_DIM
CONV_WIDTH = 31
N_META = 16
D_IN = 3 * D_CONV + 4 * D_SB
RMS_EPS = 1e-6
LN_EPS = 1e-5
SB_SCALE = HEAD_DIM ** -0.5

ADAM_LR = 0.001
ADAM_B1 = 0.9
ADAM_B2 = 0.999
ADAM_EPS = 1e-08
ADAM_WD = 0.01
ADAM_STEP = 10

N_DEV = 8
LANES = 128
ROW_BLK = 256
DENSE_ROWS_MAX = 544
HALO = 32
VMEM_LIMIT = 56 * 1024 * 1024
MESH = pl.DeviceIdType.MESH


def _cparams(*sem):
    return pltpu.CompilerParams(dimension_semantics=sem, vmem_limit_bytes=VMEM_LIMIT)


def _rows(n_cols, col=0, rows=ROW_BLK):
    return pl.BlockSpec((rows, n_cols), lambda i, col=col: (i, col))


def _dense_rows(t):
    packed_rows = 16
    return max(d for d in range(packed_rows, min(t, DENSE_ROWS_MAX) + 1, packed_rows) if t % d == 0)


def _whole(shape):
    return pl.BlockSpec(shape, lambda i: (0,) * len(shape))


def _sigmoid(x):
    return jax.nn.sigmoid(x)


def _dsilu(x, s):
    return s * (1.0 + x * (1.0 - s))


def _dot(a, b):
    return jnp.dot(a, b, preferred_element_type=F32)


def _dot_nt(a, b):
    return lax.dot_general(a, b, (((1,), (1,)), ((), ())), preferred_element_type=F32)


def _dot_tn(a, b):
    return lax.dot_general(a, b, (((0,), (0,)), ((), ())), preferred_element_type=F32)


def _inproj_fwd(h, g_pre, w_in):
    t = h.shape[0]
    blk = _dense_rows(t)

    def body(h_ref, g_ref, w_ref, pc_ref, qkv_ref, sbg_ref, u_ref):
        x = h_ref[...]
        r = lax.rsqrt(jnp.mean(x * x, axis=-1, keepdims=True) + RMS_EPS)
        u = (x * r * g_ref[...]).astype(BF16)
        u_ref[...] = u
        pc_ref[...] = _dot(u, w_ref[:, 0:1536])
        qkv_ref[...] = _dot(u, w_ref[:, 1536:3072]).astype(BF16)
        sbg_ref[...] = _dot(u, w_ref[:, 3072:3584])

    return pl.pallas_call(
        body, name="inproj_fwd", grid=(t // blk,),
        in_specs=[_rows(D_MODEL, rows=blk), _whole((1, D_MODEL)), _whole((D_MODEL, D_IN))],
        out_specs=[_rows(1536, rows=blk), _rows(1536, rows=blk), _rows(D_SB, rows=blk), _rows(D_MODEL, rows=blk)],
        out_shape=[jax.ShapeDtypeStruct((t, 1536), F32), jax.ShapeDtypeStruct((t, 1536), BF16),
                   jax.ShapeDtypeStruct((t, D_SB), F32), jax.ShapeDtypeStruct((t, D_MODEL), BF16)],
        compiler_params=_cparams("parallel"),
    )(h, g_pre, w_in)


def _prev_halo(col):
    per = ROW_BLK // HALO
    return pl.BlockSpec((HALO, D_CONV), lambda i, col=col: (jnp.maximum(i * per - 1, 0), col))


def _fill_glu(buf, i, a_ref, b_ref, ha_ref, hb_ref):
    halo = ha_ref[...] * _sigmoid(hb_ref[...])
    buf[0:HALO, :] = jnp.where(i > 0, halo, 0.0)
    buf[HALO:HALO + ROW_BLK, :] = a_ref[...] * _sigmoid(b_ref[...])


SUBLANES = 8
TAP_ROWS = 64
SHIFT_ROWS = HALO + ROW_BLK - SUBLANES


def _fill_shifts(shifts, buf):
    for b in range(1, SUBLANES):
        shifts[b - 1] = buf[pl.ds(b, SHIFT_ROWS), :]


def _window(buf, shifts, first, rows, lanes):
    whole, part = divmod(first, SUBLANES)
    src = buf if part == 0 else shifts.at[part - 1]
    return src[pl.ds(rows.start + SUBLANES * whole, rows.size), lanes]


TAP_ROW_CHUNKS = [pl.ds(r, TAP_ROWS) for r in range(0, ROW_BLK, TAP_ROWS)]
TAP_LANE_TILES = [pl.ds(c, LANES) for c in range(0, D_CONV, LANES)]


def _layer_norm_stats(cv):
    mu = jnp.mean(cv, axis=-1, keepdims=True)
    xc = cv - mu
    rstd = lax.rsqrt(jnp.mean(xc * xc, axis=-1, keepdims=True) + LN_EPS)
    return xc * rstd, rstd


def _conv_fwd(pc, conv_w, conv_b, ln_g, ln_b, w_pw2, b_pw2):
    t = pc.shape[0]

    def body(a_ref, b_ref, gate_ref, ha_ref, hb_ref, cw_ref, cb_ref, lg_ref, lb_ref, wp_ref, bp_ref,
             cout_ref, cv_ref, p_ref, sl_ref, buf, shifts):
        i = pl.program_id(0)
        _fill_glu(buf, i, a_ref, b_ref, ha_ref, hb_ref)
        _fill_shifts(shifts, buf)
        for lanes in TAP_LANE_TILES:
            for rows in TAP_ROW_CHUNKS:
                acc = jnp.zeros((TAP_ROWS, LANES), F32) + cb_ref[:, lanes]
                for j in range(CONV_WIDTH):
                    acc = acc + cw_ref[j:j + 1, lanes] * _window(buf, shifts, HALO - (CONV_WIDTH - 1) + j, rows, lanes)
                cv_ref[rows, lanes] = acc
        xh, _ = _layer_norm_stats(cv_ref[...])
        ln = xh * lg_ref[...] + lb_ref[...]
        sl = (ln * _sigmoid(ln)).astype(BF16)
        sl_ref[...] = sl
        p = _dot(sl, wp_ref[...]) + bp_ref[...]
        p_ref[...] = p
        gate = gate_ref[...]
        cout_ref[...] = (p * (gate * _sigmoid(gate))).astype(BF16)

    vec = _whole((1, D_CONV))
    return pl.pallas_call(
        body, name="conv_fwd", grid=(t // ROW_BLK,),
        in_specs=[_rows(D_CONV, 0), _rows(D_CONV, 1), _rows(D_CONV, 2), _prev_halo(0), _prev_halo(1),
                  _whole((CONV_WIDTH, D_CONV)), vec, vec, vec, _whole((D_CONV, D_CONV)), vec],
        out_specs=[_rows(D_CONV)] * 4,
        out_shape=[jax.ShapeDtypeStruct((t, D_CONV), BF16), jax.ShapeDtypeStruct((t, D_CONV), F32),
                   jax.ShapeDtypeStruct((t, D_CONV), F32), jax.ShapeDtypeStruct((t, D_CONV), BF16)],
        scratch_shapes=[pltpu.VMEM((HALO + ROW_BLK, D_CONV), F32), pltpu.VMEM((SUBLANES - 1, SHIFT_ROWS, D_CONV), F32)],
        compiler_params=_cparams("parallel"),
    )(pc, pc, pc, pc, pc, conv_w, conv_b, ln_g, ln_b, w_pw2, b_pw2)


def _lower_triangle():
    row = lax.broadcasted_iota(jnp.int32, (ROW_BLK, ROW_BLK), 0)
    col = lax.broadcasted_iota(jnp.int32, (ROW_BLK, ROW_BLK), 1)
    return row > col


def _lower_triangle_t():
    row = lax.broadcasted_iota(jnp.int32, (ROW_BLK, ROW_BLK), 0)
    col = lax.broadcasted_iota(jnp.int32, (ROW_BLK, ROW_BLK), 1)
    return row < col


def _tri_sum(x, umat):
    return _dot(x.astype(BF16), umat)


def _log_gates(z):
    ls = -(jnp.maximum(z, 0.0) + jnp.log(1.0 + jnp.exp(-jnp.abs(z))))
    return ls, z + ls


def _head_lanes(hh):
    lane = lax.broadcasted_iota(jnp.int32, (ROW_BLK, HEAD_BLK), 1)
    return (lane >= HEAD_DIM * hh) & (lane < HEAD_DIM * (hh + 1))


def _merge_heads(acc_ref):
    out = acc_ref[HEADS_PER_BLOCK - 1]
    for hh in range(HEADS_PER_BLOCK - 1):
        out = jnp.where(_head_lanes(hh), acc_ref[hh], out)
    return out


def _qkv_specs(t):
    n_blk = D_SB // HEAD_BLK
    return [pl.BlockSpec((ROW_BLK, HEAD_BLK), lambda hp, i: (i, hp)),
            pl.BlockSpec((t, HEAD_BLK), lambda hp, i: (0, n_blk + hp)),
            pl.BlockSpec((t, HEAD_BLK), lambda hp, i: (0, 2 * n_blk + hp))]


def _carry_spec():
    return pl.BlockSpec((HEADS_PER_BLOCK, ROW_BLK, LANES), lambda hp, i: (hp, i, 0))


def _last_block_rows(t, n_tokens):
    packed_rows = 16
    return -(-(n_tokens - (t - ROW_BLK)) // packed_rows) * packed_rows


def _by_block_rows(i, last_rows, sweep):
    if last_rows == ROW_BLK:
        sweep(ROW_BLK)
        return
    last = pl.num_programs(1) - 1
    pl.when(i < last)(lambda: sweep(ROW_BLK))
    pl.when(i == last)(lambda: sweep(last_rows))


def _attn_fwd(qkv, n_tokens):
    t = qkv.shape[0]
    assert t // ROW_BLK <= LANES

    def body(q_ref, k_ref, v_ref, o_ref, c_ref, acc_ref, run_ref, qm_ref, z_ref):
        i = pl.program_id(1)
        q = q_ref[...]
        heads = range(HEADS_PER_BLOCK)
        for hh in heads:
            qm_ref[hh] = jnp.where(_head_lanes(hh), q, jnp.zeros_like(q)) * jnp.asarray(SB_SCALE, BF16)
        acc_ref[...] = jnp.zeros_like(acc_ref)
        c_ref[...] = jnp.zeros_like(c_ref)
        run_ref[...] = jnp.zeros_like(run_ref)

        def sweep(n_rows):
            rows = pl.ds(0, n_rows)
            lower = _lower_triangle()[:n_rows]
            umat = jnp.where(_lower_triangle(), 1.0, 0.0).astype(BF16)
            lane = lax.broadcasted_iota(jnp.int32, (n_rows, LANES), 1)

            def scores(jb):
                start = pl.multiple_of(jb * ROW_BLK, ROW_BLK)
                kb = k_ref[pl.ds(start, ROW_BLK), :]
                for hh in heads:
                    z_ref[hh, rows] = _dot_nt(qm_ref[hh, rows], kb)

            def block(jb, diagonal):
                start = pl.multiple_of(jb * ROW_BLK, ROW_BLK)
                vb = v_ref[pl.ds(start, ROW_BLK), :]
                logits = []
                for hh in heads:
                    ls, lb = _log_gates(z_ref[hh, rows])
                    if diagonal:
                        ls = jnp.where(lower, ls, 0.0)
                    run = run_ref[hh, rows]
                    if not diagonal:
                        c_ref[hh, rows] = jnp.where(lane == jb, run, c_ref[hh, rows])
                    logits.append(lb + jnp.concatenate([run, run], axis=1) + _tri_sum(ls, umat))
                    run_ref[hh, rows] = run + jnp.sum(ls, axis=1, keepdims=True)
                scores(jnp.maximum(jb - 1, 0))
                for hh in heads:
                    a = jnp.exp(logits[hh])
                    if diagonal:
                        a = jnp.where(lower, a, 0.0)
                    acc_ref[hh, rows] += _dot(a.astype(BF16), vb)

            scores(i)
            block(i, True)

            @pl.loop(0, i)
            def _(n):
                block(i - 1 - n, False)

        _by_block_rows(i, _last_block_rows(t, n_tokens), sweep)
        o_ref[...] = _merge_heads(acc_ref)

    per_head = (HEADS_PER_BLOCK, ROW_BLK, HEAD_BLK)
    return pl.pallas_call(
        body, name="attn_fwd", grid=(D_SB // HEAD_BLK, t // ROW_BLK),
        in_specs=_qkv_specs(t),
        out_specs=[pl.BlockSpec((ROW_BLK, HEAD_BLK), lambda hp, i: (i, hp)), _carry_spec()],
        out_shape=[jax.ShapeDtypeStruct((t, D_SB), F32),
                   jax.ShapeDtypeStruct((D_SB // HEAD_DIM, t, LANES), F32)],
        scratch_shapes=[pltpu.VMEM(per_head, F32), pltpu.VMEM((HEADS_PER_BLOCK, ROW_BLK, LANES), F32),
                        pltpu.VMEM(per_head, BF16), pltpu.VMEM((HEADS_PER_BLOCK, ROW_BLK, ROW_BLK), F32)],
        compiler_params=_cparams("arbitrary", "arbitrary"),
    )(qkv, qkv, qkv)


def _outproj_fwd(h, cout, sraw, sbg, w_out, g_post):
    t = h.shape[0]
    blk = _dense_rows(t)

    def body(h_ref, c_ref, s_ref, g_ref, w_ref, gp_ref, hn_ref, mixed_ref, mix_ref):
        gate = g_ref[...]
        mix_ref[:, 0:D_CONV] = c_ref[...]
        mix_ref[:, D_CONV:] = (s_ref[...] * (gate * _sigmoid(gate))).astype(BF16)
        mixed = _dot(mix_ref[...], w_ref[...])
        mixed_ref[...] = mixed
        r = lax.rsqrt(jnp.mean(mixed * mixed, axis=-1, keepdims=True) + RMS_EPS)
        hn_ref[...] = h_ref[...] + mixed * r * gp_ref[...]

    return pl.pallas_call(
        body, name="outproj_fwd", grid=(t // blk,),
        in_specs=[_rows(D_MODEL, rows=blk), _rows(D_CONV, rows=blk), _rows(D_SB, rows=blk), _rows(D_SB, rows=blk),
                  _whole((D_MODEL, D_MODEL)), _whole((1, D_MODEL))],
        out_specs=[_rows(D_MODEL, rows=blk)] * 3,
        out_shape=[jax.ShapeDtypeStruct((t, D_MODEL), F32), jax.ShapeDtypeStruct((t, D_MODEL), F32),
                   jax.ShapeDtypeStruct((t, D_MODEL), BF16)],
        compiler_params=_cparams("parallel"),
    )(h, cout, sraw, sbg, w_out, g_post)


def _loss_and_grad(h, target, seq):
    t = h.shape[0]
    blk = _dense_rows(t)

    def body(h_ref, t_ref, loss_ref, dh_ref):
        i = pl.program_id(0)

        @pl.when(i == 0)
        def _():
            loss_ref[...] = jnp.zeros_like(loss_ref)

        row = i * blk + lax.broadcasted_iota(jnp.int32, (blk, D_MODEL), 0)
        real = (row >= N_META) & (row < N_META + seq)
        diff = jnp.where(real, h_ref[...] - t_ref[...], 0.0)
        sq = jnp.sum(jnp.sum(diff * diff, axis=1, keepdims=True), axis=0, keepdims=True)
        loss_ref[...] += (0.5 / D_MODEL) * sq
        dh_ref[...] = diff * (1.0 / D_MODEL)

    return pl.pallas_call(
        body, name="loss", grid=(t // blk,),
        in_specs=[_rows(D_MODEL, rows=blk), _rows(D_MODEL, rows=blk)],
        out_specs=[_whole((1, 1)), _rows(D_MODEL, rows=blk)],
        out_shape=[jax.ShapeDtypeStruct((1, 1), F32), jax.ShapeDtypeStruct((t, D_MODEL), F32)],
        compiler_params=_cparams("arbitrary"),
    )(h, target)


def _outproj_bwd(dh, mixed, g_post, sraw, sbg, w_out_t):
    t = dh.shape[0]
    blk = _dense_rows(t)

    def body(dh_ref, mixed_ref, gp_ref, s_ref, g_ref, wt_ref, dc_ref, ds_ref, dg_ref, dmb_ref, dgp_ref):
        @pl.when(pl.program_id(0) == 0)
        def _():
            dgp_ref[...] = jnp.zeros_like(dgp_ref)

        mixed = mixed_ref[...]
        r = lax.rsqrt(jnp.mean(mixed * mixed, axis=-1, keepdims=True) + RMS_EPS)
        nh = mixed * r
        dy = dh_ref[...]
        dgp_ref[...] += jnp.sum(dy * nh, axis=0, keepdims=True)
        dn = dy * gp_ref[...]
        dmixed = (r * (dn - nh * jnp.mean(dn * nh, axis=-1, keepdims=True))).astype(BF16)
        dmb_ref[...] = dmixed
        dmix = _dot_nt(dmixed, wt_ref[...])
        dc_ref[...] = dmix[:, 0:D_CONV]
        dsg = dmix[:, D_CONV:]
        gate = g_ref[...]
        sg = _sigmoid(gate)
        ds_ref[...] = dsg * (gate * sg)
        dg_ref[...] = (dsg * s_ref[...] * _dsilu(gate, sg)).astype(BF16)

    return pl.pallas_call(
        body, name="outproj_bwd", grid=(t // blk,),
        in_specs=[_rows(D_MODEL, rows=blk), _rows(D_MODEL, rows=blk), _whole((1, D_MODEL)), _rows(D_SB, rows=blk),
                  _rows(D_SB, rows=blk), _whole((D_MODEL, D_MODEL))],
        out_specs=[_rows(D_CONV, rows=blk), _rows(D_SB, rows=blk), _rows(D_SB, rows=blk), _rows(D_MODEL, rows=blk),
                   _whole((1, D_MODEL))],
        out_shape=[jax.ShapeDtypeStruct((t, D_CONV), F32), jax.ShapeDtypeStruct((t, D_SB), F32),
                   jax.ShapeDtypeStruct((t, D_SB), BF16), jax.ShapeDtypeStruct((t, D_MODEL), BF16),
                   jax.ShapeDtypeStruct((1, D_MODEL), F32)],
        compiler_params=_cparams("arbitrary"),
    )(dh, mixed, g_post, sraw, sbg, w_out_t)


def _attn_bwd(qkv, carries, do, n_tokens):
    t = qkv.shape[0]

    def body(q_ref, k_ref, v_ref, c_ref, do_ref, dq_ref, dk_ref, dv_ref, acc_ref, seen_ref, qm_ref, dom_ref, z_ref,
             da_ref, dz_ref, a_ref):
        i = pl.program_id(1)

        @pl.when(i == 0)
        def _():
            dk_ref[...] = jnp.zeros_like(dk_ref)
            dv_ref[...] = jnp.zeros_like(dv_ref)

        q = q_ref[...]
        dof = do_ref[...]
        heads = range(HEADS_PER_BLOCK)
        for hh in heads:
            qm_ref[hh] = jnp.where(_head_lanes(hh), q, jnp.zeros_like(q)) * jnp.asarray(SB_SCALE, BF16)
            dom_ref[hh] = jnp.where(_head_lanes(hh), dof, 0.0).astype(BF16)
        acc_ref[...] = jnp.zeros_like(acc_ref)
        seen_ref[...] = jnp.zeros_like(seen_ref)

        def sweep(n_rows):
            rows = pl.ds(0, n_rows)
            lower = _lower_triangle()[:n_rows]
            umat = jnp.where(_lower_triangle(), 1.0, 0.0).astype(BF16)
            umat_t = jnp.where(_lower_triangle_t(), 1.0, 0.0).astype(BF16)
            lane = lax.broadcasted_iota(jnp.int32, (n_rows, LANES), 1)

            def scores(jb):
                start = pl.multiple_of(jb * ROW_BLK, ROW_BLK)
                kb = k_ref[pl.ds(start, ROW_BLK), :]
                for hh in heads:
                    z_ref[hh, rows] = _dot_nt(qm_ref[hh, rows], kb)

            def value_grads(jb):
                start = pl.multiple_of(jb * ROW_BLK, ROW_BLK)
                vb = v_ref[pl.ds(start, ROW_BLK), :]
                for hh in heads:
                    da_ref[hh, rows] = _dot_nt(dom_ref[hh, rows], vb)

            def products(jb, hh):
                start = pl.multiple_of(jb * ROW_BLK, ROW_BLK)
                dzb = dz_ref[hh, rows]
                acc_ref[hh, rows] += _dot(dzb, k_ref[pl.ds(start, ROW_BLK), :])
                dk_ref[pl.ds(start, ROW_BLK), :] += _dot_tn(dzb, qm_ref[hh, rows])
                dv_ref[pl.ds(start, ROW_BLK), :] += _dot_tn(a_ref[hh, rows], dom_ref[hh, rows])

            def block(jb, diagonal):
                before = jnp.maximum(jb - 1, 0)
                lbs, logits = [], []
                for hh in heads:
                    products(before, hh)
                    ls, lb = _log_gates(z_ref[hh, rows])
                    if diagonal:
                        ls = jnp.where(lower, ls, 0.0)
                        logits.append(lb + _tri_sum(ls, umat))
                    else:
                        right = jnp.sum(jnp.where(lane == jb, c_ref[hh, rows], 0.0), axis=1, keepdims=True)
                        logits.append(lb + right + _tri_sum(ls, umat))
                    lbs.append(lb)
                if not diagonal:
                    scores(jb + 1)
                gs, befores = [], []
                for hh in heads:
                    a = jnp.exp(logits[hh])
                    if diagonal:
                        a = jnp.where(lower, a, 0.0)
                    g = da_ref[hh, rows] * a
                    seen = seen_ref[hh, rows]
                    befores.append(jnp.concatenate([seen, seen], axis=1) + _tri_sum(g, umat_t))
                    seen_ref[hh, rows] = seen + jnp.sum(g, axis=1, keepdims=True)
                    a_ref[hh, rows] = a.astype(BF16)
                    gs.append(g)
                if not diagonal:
                    value_grads(jb + 1)
                for hh in heads:
                    dz = gs[hh] - jnp.exp(lbs[hh]) * (gs[hh] + befores[hh])
                    if diagonal:
                        dz = jnp.where(lower, dz, 0.0)
                    dz_ref[hh, rows] = dz.astype(BF16)

            dz_ref[...] = jnp.zeros_like(dz_ref)
            a_ref[...] = jnp.zeros_like(a_ref)
            scores(0)
            value_grads(0)

            @pl.loop(0, i)
            def _(jb):
                block(jb, False)

            block(i, True)
            for hh in heads:
                products(i, hh)

        _by_block_rows(i, _last_block_rows(t, n_tokens), sweep)
        dq_ref[...] = (_merge_heads(acc_ref) * SB_SCALE).astype(BF16)

    blk = pl.BlockSpec((ROW_BLK, HEAD_BLK), lambda hp, i: (i, hp))
    full = pl.BlockSpec((t, HEAD_BLK), lambda hp, i: (0, hp))
    per_head = (HEADS_PER_BLOCK, ROW_BLK, HEAD_BLK)
    return pl.pallas_call(
        body, name="attn_bwd", grid=(D_SB // HEAD_BLK, t // ROW_BLK),
        in_specs=_qkv_specs(t) + [_carry_spec(), blk],
        out_specs=[blk, full, full],
        out_shape=[jax.ShapeDtypeStruct((t, D_SB), BF16)] + [jax.ShapeDtypeStruct((t, D_SB), F32)] * 2,
        scratch_shapes=[pltpu.VMEM(per_head, F32), pltpu.VMEM((HEADS_PER_BLOCK, ROW_BLK, LANES), F32),
                        pltpu.VMEM(per_head, BF16), pltpu.VMEM(per_head, BF16)]
                       + [pltpu.VMEM((HEADS_PER_BLOCK, ROW_BLK, ROW_BLK), dtype) for dtype in (F32, F32, BF16, BF16)],
        compiler_params=_cparams("arbitrary", "arbitrary"),
    )(qkv, qkv, qkv, carries, do)


def _conv_bwd_rows(dcout, pc, cv, p, ln_g, ln_b, w_pw2_t):
    t = dcout.shape[0]
    blk = _dense_rows(t)

    def body(dc_ref, gate_ref, cv_ref, p_ref, lg_ref, lb_ref, wt_ref, dcv_ref, dgate_ref, dpb_ref, vec_ref):
        @pl.when(pl.program_id(0) == 0)
        def _():
            vec_ref[...] = jnp.zeros_like(vec_ref)

        dc = dc_ref[...]
        gate = gate_ref[...]
        sg = _sigmoid(gate)
        dp = dc * (gate * sg)
        dgate_ref[...] = (dc * p_ref[...] * _dsilu(gate, sg)).astype(BF16)
        dpb = dp.astype(BF16)
        dpb_ref[...] = dpb
        xh, rstd = _layer_norm_stats(cv_ref[...])
        ln = xh * lg_ref[...] + lb_ref[...]
        s2 = _sigmoid(ln)
        dln = _dot_nt(dpb, wt_ref[...]) * _dsilu(ln, s2)
        dxh = dln * lg_ref[...]
        dcv = rstd * (dxh - jnp.mean(dxh, axis=-1, keepdims=True)
                      - xh * jnp.mean(dxh * xh, axis=-1, keepdims=True))
        dcv_ref[...] = dcv
        vec_ref[0:1, :] += jnp.sum(dp, axis=0, keepdims=True)
        vec_ref[1:2, :] += jnp.sum(dln * xh, axis=0, keepdims=True)
        vec_ref[2:3, :] += jnp.sum(dln, axis=0, keepdims=True)
        vec_ref[3:4, :] += jnp.sum(dcv, axis=0, keepdims=True)

    vec = _whole((1, D_CONV))
    return pl.pallas_call(
        body, name="conv_bwd_rows", grid=(t // blk,),
        in_specs=[_rows(D_CONV, rows=blk), _rows(D_CONV, 2, rows=blk), _rows(D_CONV, rows=blk), _rows(D_CONV, rows=blk),
                  vec, vec, _whole((D_CONV, D_CONV))],
        out_specs=[_rows(D_CONV, rows=blk), _rows(D_CONV, rows=blk), _rows(D_CONV, rows=blk), _whole((8, D_CONV))],
        out_shape=[jax.ShapeDtypeStruct((t, D_CONV), F32), jax.ShapeDtypeStruct((t, D_CONV), BF16),
                   jax.ShapeDtypeStruct((t, D_CONV), BF16), jax.ShapeDtypeStruct((8, D_CONV), F32)],
        compiler_params=_cparams("arbitrary"),
    )(dcout, pc, cv, p, ln_g, ln_b, w_pw2_t)


def _conv_bwd_taps(dcv, pc, conv_w):
    t = dcv.shape[0]
    n_halo = t // HALO
    per = ROW_BLK // HALO

    def body(d_ref, dn_ref, a_ref, b_ref, ha_ref, hb_ref, cw_ref, da_ref, db_ref, dw_ref, cbuf, dbuf, cshifts, dshifts):
        i = pl.program_id(0)

        @pl.when(i == 0)
        def _():
            dw_ref[...] = jnp.zeros_like(dw_ref)

        _fill_glu(cbuf, i, a_ref, b_ref, ha_ref, hb_ref)
        dbuf[0:ROW_BLK, :] = d_ref[...]
        dbuf[ROW_BLK:ROW_BLK + HALO, :] = jnp.where(i < pl.num_programs(0) - 1, dn_ref[...], 0.0)
        _fill_shifts(cshifts, cbuf)
        _fill_shifts(dshifts, dbuf)
        for lanes in TAP_LANE_TILES:
            for rows in TAP_ROW_CHUNKS:
                acc = jnp.zeros((TAP_ROWS, LANES), F32)
                for j in range(CONV_WIDTH):
                    acc = acc + cw_ref[j:j + 1, lanes] * _window(dbuf, dshifts, CONV_WIDTH - 1 - j, rows, lanes)
                sb = _sigmoid(b_ref[rows, lanes])
                da_ref[rows, lanes] = (acc * sb).astype(BF16)
                db_ref[rows, lanes] = (acc * a_ref[rows, lanes] * sb * (1.0 - sb)).astype(BF16)
            for j in range(CONV_WIDTH):
                acc = jnp.zeros((TAP_ROWS, LANES), F32)
                for rows in TAP_ROW_CHUNKS:
                    acc = acc + d_ref[rows, lanes] * _window(cbuf, cshifts, HALO - (CONV_WIDTH - 1) + j, rows, lanes)
                dw_ref[j:j + 1, lanes] += jnp.sum(acc, axis=0, keepdims=True)

    return pl.pallas_call(
        body, name="conv_bwd_taps", grid=(t // ROW_BLK,),
        in_specs=[_rows(D_CONV),
                  pl.BlockSpec((HALO, D_CONV), lambda i: (jnp.minimum((i + 1) * per, n_halo - 1), 0)),
                  _rows(D_CONV, 0), _rows(D_CONV, 1), _prev_halo(0), _prev_halo(1),
                  _whole((CONV_WIDTH, D_CONV))],
        out_specs=[_rows(D_CONV), _rows(D_CONV), _whole((32, D_CONV))],
        out_shape=[jax.ShapeDtypeStruct((t, D_CONV), BF16), jax.ShapeDtypeStruct((t, D_CONV), BF16),
                   jax.ShapeDtypeStruct((32, D_CONV), F32)],
        scratch_shapes=[pltpu.VMEM((HALO + ROW_BLK, D_CONV), F32), pltpu.VMEM((ROW_BLK + HALO, D_CONV), F32),
                        pltpu.VMEM((SUBLANES - 1, SHIFT_ROWS, D_CONV), F32),
                        pltpu.VMEM((SUBLANES - 1, SHIFT_ROWS, D_CONV), F32)],
        compiler_params=_cparams("arbitrary"),
    )(dcv, dcv, pc, pc, pc, pc, conv_w)


def _inproj_bwd(dh_out, h, g_pre, pieces, w_in_t):
    t = h.shape[0]
    blk = _dense_rows(t)

    def body(dh_ref, h_ref, g_ref, *rest):
        piece_refs, (wt_ref, dhin_ref, dproj_ref, dg_ref) = rest[:7], rest[7:]

        @pl.when(pl.program_id(0) == 0)
        def _():
            dg_ref[...] = jnp.zeros_like(dg_ref)

        for k, ref in enumerate(piece_refs):
            dproj_ref[:, 512 * k:512 * (k + 1)] = ref[...].astype(BF16)
        du = _dot_nt(dproj_ref[...], wt_ref[...])
        x = h_ref[...]
        r = lax.rsqrt(jnp.mean(x * x, axis=-1, keepdims=True) + RMS_EPS)
        xh = x * r
        dg_ref[...] += jnp.sum(du * xh, axis=0, keepdims=True)
        dxh = du * g_ref[...]
        dhin_ref[...] = dh_ref[...] + r * (dxh - xh * jnp.mean(dxh * xh, axis=-1, keepdims=True))

    return pl.pallas_call(
        body, name="inproj_bwd", grid=(t // blk,),
        in_specs=[_rows(D_MODEL, rows=blk), _rows(D_MODEL, rows=blk), _whole((1, D_MODEL))] + [_rows(512, rows=blk)] * 7
                 + [_whole((D_MODEL, D_IN))],
        out_specs=[_rows(D_MODEL, rows=blk), _rows(D_IN, rows=blk), _whole((1, D_MODEL))],
        out_shape=[jax.ShapeDtypeStruct((t, D_MODEL), F32), jax.ShapeDtypeStruct((t, D_IN), BF16),
                   jax.ShapeDtypeStruct((1, D_MODEL), F32)],
        compiler_params=_cparams("arbitrary"),
    )(dh_out, h, g_pre, *pieces, w_in_t)


def _weight_grad(xb, dys, name):
    t, k = xb.shape
    widths = [dy.shape[1] for dy in dys]
    n = sum(widths)

    def body(x_ref, *rest):
        dy_refs, (o_ref, acc_ref, dy_buf) = rest[:len(dys)], rest[len(dys):]
        i = pl.program_id(0)

        @pl.when(i == 0)
        def _():
            acc_ref[...] = jnp.zeros_like(acc_ref)

        at = 0
        for ref, width in zip(dy_refs, widths):
            dy_buf[:, at:at + width] = ref[...].astype(BF16)
            at += width
        acc_ref[...] += _dot_tn(x_ref[...], dy_buf[...])

        @pl.when(i == pl.num_programs(0) - 1)
        def _():
            o_ref[...] = acc_ref[...].astype(BF16)

    return pl.pallas_call(
        body, name=name, grid=(t // ROW_BLK,),
        in_specs=[_rows(k)] + [_rows(width) for width in widths], out_specs=_whole((k, n)),
        out_shape=jax.ShapeDtypeStruct((k, n), BF16),
        scratch_shapes=[pltpu.VMEM((k, n), F32), pltpu.VMEM((ROW_BLK, n), BF16)],
        compiler_params=_cparams("arbitrary"),
    )(xb, *dys)


def _position():
    return lax.axis_index("x"), lax.axis_index("y"), lax.axis_index("c")


def _comm_call(body, name, ins, out_shapes):
    n = len(ins)
    hbm = pl.BlockSpec(memory_space=pltpu.HBM)
    return pl.pallas_call(
        functools.partial(body, n), name=name, in_specs=[hbm] * n, out_specs=[hbm] * n, out_shape=out_shapes,
        scratch_shapes=[pltpu.SemaphoreType.DMA((n, N_DEV - 1)), pltpu.SemaphoreType.DMA((n, N_DEV - 1)),
                        pltpu.SemaphoreType.DMA((n,))],
    )(*ins)


def _all_gather(blocks, name):
    def body(n, *refs):
        x_refs, out_refs, (send_sems, recv_sems, local_sems) = refs[:n], refs[n:2 * n], refs[2 * n:]
        x, y, c = _position()
        me, sibling = (x, y, c), (x, y, 1 - c)
        chips = [(1 - x, y), (x, 1 - y), (1 - x, 1 - y)]

        def slot(a, px, py, pc):
            return out_refs[a].at[4 * px + 2 * py + pc]

        def copy(a, k, origin, to, own=False):
            return pltpu.make_async_remote_copy(
                src_ref=x_refs[a] if own else slot(a, *origin), dst_ref=slot(a, *origin),
                send_sem=send_sems.at[a, k], recv_sem=recv_sems.at[a, k], device_id=to, device_id_type=MESH)

        arrays = range(n)
        mine = [pltpu.make_async_copy(x_refs[a], slot(a, *me), local_sems.at[a]) for a in arrays]
        first = [copy(a, 1 + j, me, (*chip, c), own=True) for j, chip in enumerate(chips) for a in arrays]
        first += [copy(a, 0, me, sibling, own=True) for a in arrays]
        for cp in mine + first:
            cp.start()
        passed = []
        for j, chip in enumerate(chips):
            for a in arrays:
                copy(a, 1 + j, (*chip, c), me).wait_recv()
                passed.append(copy(a, 4 + j, (*chip, c), sibling))
                passed[-1].start()
        for a in arrays:
            copy(a, 0, sibling, me).wait_recv()
            for j, chip in enumerate(chips):
                copy(a, 4 + j, (*chip, 1 - c), me).wait_recv()
        for cp in first + passed:
            cp.wait_send()
        for cp in mine:
            cp.wait()

    return _comm_call(body, name, blocks, [jax.ShapeDtypeStruct((N_DEV,) + b.shape, b.dtype) for b in blocks])


def _exchange_copies(g_refs, land_refs, sems, gather):
    x, y, c = _position()
    me = 4 * x + 2 * y + c
    out = []
    for g_ref, land_ref, (send_sem, recv_sem, local_sem) in zip(g_refs, land_refs, sems):
        def mine(slot, g_ref=g_ref):
            return g_ref if gather else g_ref.at[slot]

        def remote(src, dst, dev):
            return pltpu.make_async_remote_copy(src_ref=src, dst_ref=dst, send_sem=send_sem, recv_sem=recv_sem,
                                                device_id=dev, device_id_type=MESH)

        sends = []
        for k in range(1, N_DEV):
            px = 1 - x if k & 4 else x
            py = 1 - y if k & 2 else y
            pc = 1 - c if k & 1 else c
            sends.append(remote(mine(4 * px + 2 * py + pc), land_ref.at[me], (px, py, pc)))
        seven = land_ref.at[pl.ds(0, N_DEV - 1)]
        out.append((pltpu.make_async_copy(mine(me), land_ref.at[me], local_sem), sends, remote(seven, seven, (x, y, c))))
    return out


_HBM = pl.BlockSpec(memory_space=pltpu.HBM)
_SEM = pl.BlockSpec(memory_space=pltpu.SEMAPHORE)
_ORDERED = pltpu.CompilerParams(has_side_effects=pltpu.SideEffectType.DATAFLOW_SIDE_EFFECTING)
SEMS_PER_ARRAY = 3


def _exchange_start(arrays, after, name, gather):
    n = len(arrays)
    n_sems = SEMS_PER_ARRAY * n

    def body(*refs):
        g_refs, land_refs, sems, token = refs[:n], refs[n:2 * n], refs[2 * n + 1:2 * n + 1 + n_sems], refs[-1]
        sems = [sems[SEMS_PER_ARRAY * a:SEMS_PER_ARRAY * (a + 1)] for a in range(n)]
        for local, sends, _ in _exchange_copies(g_refs, land_refs, sems, gather):
            local.start()
            for cp in sends:
                cp.start()
        token[...] = jnp.zeros_like(token)

    buffers = list(arrays) + [lax.empty((N_DEV,) + g.shape if gather else g.shape, g.dtype) for g in arrays]
    outs = pl.pallas_call(
        body, name=name, in_specs=[_HBM] * (2 * n) + [pl.BlockSpec(memory_space=pl.ANY)],
        out_specs=[_SEM] * n_sems + [_HBM] * (2 * n) + [pl.BlockSpec(memory_space=pltpu.VMEM)],
        out_shape=[pltpu.SemaphoreType.DMA(())] * n_sems + [pltpu.HBM(b.shape, b.dtype) for b in buffers]
                  + [jax.ShapeDtypeStruct((8, LANES), F32)],
        input_output_aliases={a: n_sems + a for a in range(2 * n)}, compiler_params=_ORDERED,
    )(*[pltpu.with_memory_space_constraint(b, pltpu.HBM) for b in buffers], after)
    return outs[:n_sems], outs[n_sems:n_sems + n], outs[n_sems + n:n_sems + 2 * n], outs[-1]


def _exchange_wait(sems, arrays, landings, after, name, gather):
    n = len(arrays)
    n_sems = SEMS_PER_ARRAY * n

    def body(*refs):
        g_refs, land_refs, sems = refs[:n], refs[n:2 * n], refs[2 * n:2 * n + n_sems]
        sems = [sems[SEMS_PER_ARRAY * a:SEMS_PER_ARRAY * (a + 1)] for a in range(n)]
        for local, _, all_seven in _exchange_copies(g_refs, land_refs, sems, gather):
            all_seven.wait_recv()
            all_seven.wait_send()
            local.wait()

    buffers = list(arrays) + list(landings)
    outs = pl.pallas_call(
        body, name=name, in_specs=[_HBM] * (2 * n) + [_SEM] * n_sems + [pl.BlockSpec(memory_space=pl.ANY)],
        out_specs=[_HBM] * (2 * n), out_shape=[pltpu.HBM(b.shape, b.dtype) for b in buffers],
        input_output_aliases={a: a for a in range(2 * n)}, compiler_params=_ORDERED,
    )(*buffers, *sems, after)
    return outs[n:]


def _block_rows(r, row_bytes, budget=1 << 20):
    cap = max(8, budget // row_bytes)
    return max(d for d in range(8, min(r, cap) + 1, 8) if r % d == 0)


def _sum_adamw(parts, w, m, v, name):
    n_parts, r, c = parts.shape
    br = _block_rows(r, 4 * c)

    def body(p_ref, w_ref, m_ref, v_ref, g_out, d_out, m_out, v_out):
        g = p_ref[0].astype(F32)
        for s in range(1, n_parts):
            g = g + p_ref[s].astype(F32)
        m_new = ADAM_B1 * m_ref[...] + (1.0 - ADAM_B1) * g
        v_new = ADAM_B2 * v_ref[...] + (1.0 - ADAM_B2) * (g * g)
        m_hat = m_new / (1.0 - ADAM_B1 ** ADAM_STEP)
        v_hat = v_new / (1.0 - ADAM_B2 ** ADAM_STEP)
        g_out[...] = g
        d_out[...] = -ADAM_LR * (m_hat / (jnp.sqrt(v_hat) + ADAM_EPS) + ADAM_WD * w_ref[...])
        m_out[...] = m_new
        v_out[...] = v_new

    row = pl.BlockSpec((br, c), lambda i: (i, 0))
    return pl.pallas_call(
        body, name=name, grid=(r // br,),
        in_specs=[pl.BlockSpec((n_parts, br, c), lambda i: (0, i, 0)), row, row, row],
        out_specs=[row] * 4, out_shape=[jax.ShapeDtypeStruct((r, c), F32)] * 4,
        compiler_params=_cparams("parallel"),
    )(parts, w, m, v)


def _sum_parts(parts, name):
    n_parts, r, c = parts.shape

    def body(p_ref, o_ref):
        g = p_ref[0]
        for s in range(1, n_parts):
            g = g + p_ref[s]
        o_ref[...] = g

    return pl.pallas_call(
        body, name=name, in_specs=[pl.BlockSpec(memory_space=pltpu.VMEM)],
        out_specs=pl.BlockSpec(memory_space=pltpu.VMEM), out_shape=jax.ShapeDtypeStruct((r, c), F32),
    )(parts)


def _pack(arrays):
    flat = jnp.concatenate([a.reshape(-1) for a in arrays])
    pad = -flat.shape[0] % (8 * LANES)
    if pad:
        flat = jnp.pad(flat, (0, pad))
    return flat.reshape(-1, LANES)


def _unpack(buf, shapes):
    flat = buf.reshape(-1)
    out, at = [], 0
    for shape in shapes:
        size = 1
        for d in shape:
            size *= d
        out.append(lax.slice_in_dim(flat, at, at + size).reshape(shape))
        at += size
    return out


def _local_step(x, target, meta, pre_g, post_g, conv_w, conv_b, ln_g, ln_b, b_pw2, weights, ship, ship_small):
    depth = pre_g.shape[0]
    seq = x.shape[0]
    t = -(-(N_META + seq) // ROW_BLK) * ROW_BLK
    tail = t - N_META - seq
    h = jnp.concatenate([meta, x, jnp.zeros((tail, D_MODEL), F32)], axis=0)
    target = jnp.pad(target, ((N_META, tail), (0, 0)))
    row = lambda a, l: a[l][None, :]

    saved = []
    for l in range(depth):
        w_in, w_pw2, w_out = weights(l, h)
        pc, qkv, sbg, u = _inproj_fwd(h, row(pre_g, l), w_in)
        cout, cv, p, sl = _conv_fwd(pc, conv_w[l], row(conv_b, l), row(ln_g, l), row(ln_b, l), w_pw2, row(b_pw2, l))
        sraw, carries = _attn_fwd(qkv, N_META + seq)
        h_new, mixed, mix = _outproj_fwd(h, cout, sraw, sbg, w_out, row(post_g, l))
        saved.append((h, pc, qkv, sbg, u, cv, p, sl, sraw, carries, mixed, mix, w_in, w_pw2, w_out))
        h = h_new

    loss, dh = _loss_and_grad(h, target, seq)

    grads = {k: [None] * depth for k in ("pre_g", "post_g", "conv_w", "conv_b", "ln_g", "ln_b", "b_pw2")}
    token = jnp.zeros((8, LANES), F32)
    for l in reversed(range(depth)):
        h_in, pc, qkv, sbg, u, cv, p, sl, sraw, carries, mixed, mix, w_in_t, w_pw2_t, w_out_t = saved[l]
        dcout, dsraw, dsbg, dmixed, dg_post = _outproj_bwd(dh, mixed, row(post_g, l) + token[:1, :1], sraw, sbg, w_out_t)
        dq, dk, dv = _attn_bwd(qkv, carries, dsraw, N_META + seq)
        dcv, dgate, dpb, vecs = _conv_bwd_rows(dcout, pc, cv, p, row(ln_g, l), row(ln_b, l), w_pw2_t)
        late = {"w_pw2": (sl, [dpb]), "w_out": (mix, [dmixed])}
        taps = conv_w[l]
        if l == 0:
            token = ship(l, dcv, {k: _weight_grad(*late.pop(k), k + "_grad") for k in ("w_pw2", "w_out")})
            taps = taps + token[:1, :1]
        da, db, dconv_w = _conv_bwd_taps(dcv, pc, taps)
        pieces = (da, db, dgate, dq, dk, dv, dsbg)
        gain = row(pre_g, l)
        if l == 0:
            token = ship(l, da, {"w_in": _weight_grad(u, pieces, "w_in_grad")})
            gain = gain + token[:1, :1]
        dh, dproj, dg_pre = _inproj_bwd(dh, h_in, gain, pieces, w_in_t)
        grads["pre_g"][l] = dg_pre[0]
        grads["post_g"][l] = dg_post[0]
        grads["b_pw2"][l], grads["ln_g"][l], grads["ln_b"][l], grads["conv_b"][l] = vecs[0], vecs[1], vecs[2], vecs[3]
        grads["conv_w"][l] = dconv_w[:CONV_WIDTH]
        if l == 0:
            grads = {k: jnp.stack(v) for k, v in grads.items()}
            grads["meta"] = dh[:N_META]
            token = ship_small(grads, loss[0, 0])
        else:
            late["w_in"] = (u, [dproj])
            token = ship(l, dh, {k: _weight_grad(*late[k], k + "_grad") for k in ("w_in", "w_pw2", "w_out")})

    return dh[N_META:N_META + seq], token


def _shard_major(full, axis):
    shape = full.shape
    split = full.reshape(shape[:axis] + (N_DEV, shape[axis] // N_DEV) + shape[axis + 1:])
    return jnp.moveaxis(split, axis, 0)


def _whole_from_shards(shards, axis):
    moved = jnp.moveaxis(shards, 0, axis)
    shape = moved.shape
    return moved.reshape(shape[:axis] + (shape[axis] * shape[axis + 1],) + shape[axis + 2:])


def kernel(x, meta_tokens, pre_norm_g, post_norm_g, w_in, conv_w, conv_b, conv_ln_g, conv_ln_b, w_pw2, b_pw2, w_out, loss_target, m_meta_tokens, m_pre_norm_g, m_post_norm_g, m_w_in, m_conv_w, m_conv_b, m_conv_ln_g, m_conv_ln_b, m_w_pw2, m_b_pw2, m_w_out, v_meta_tokens, v_pre_norm_g, v_post_norm_g, v_w_in, v_conv_w, v_conv_b, v_conv_ln_g, v_conv_ln_b, v_w_pw2, v_b_pw2, v_w_out):
    me = 4 * lax.axis_index("x") + 2 * lax.axis_index("y") + lax.axis_index("c")

    depth = w_in.shape[0]
    big = [w.astype(BF16) for w in (w_in, w_pw2, w_out)]
    *first, conv_w_s, meta_s = _all_gather([w[0] for w in big] + [conv_w, meta_tokens], "gather_first_layer")
    *gathering, token = _exchange_start([w[l] for l in range(1, depth) for w in big], meta_s, "gather_start", gather=True)
    conv_w_full = _whole_from_shards(conv_w_s, 2)
    meta_full = _whole_from_shards(meta_s, 1)
    shard_axis = (1, 0, 0)
    later = []

    def weights(l, h):
        if l == 0:
            return [_whole_from_shards(s, axis) for s, axis in zip(first, shard_axis)]
        if not later:
            later.extend(_exchange_wait(*gathering, h, "gather_wait", gather=True))
        return [_whole_from_shards(s, axis) for s, axis in zip(later[len(big) * (l - 1):len(big) * l], shard_axis)]

    in_flight = [[] for _ in range(depth)]
    grad_axis = dict(zip(("w_in", "w_pw2", "w_out"), shard_axis))

    def ship(l, after, dws):
        slabs = [_shard_major(dw, grad_axis[k]) for k, dw in dws.items()]
        sems, arrays, landings, token = _exchange_start(slabs, after, f"exchange_start_{l}_{len(in_flight[l])}",
                                                        gather=False)
        in_flight[l].append((list(dws), sems, arrays, landings))
        return token

    small_names = ("pre_g", "post_g", "conv_b", "ln_g", "ln_b", "b_pw2", "conv_w", "meta")
    small_in_flight, small_shapes_full = [], []

    def ship_small(grads, loss):
        small_full = [grads[k] for k in small_names] + [loss.reshape(1)]
        small_shapes_full.extend(a.shape for a in small_full)
        *in_flight_now, token = _exchange_start([_pack(small_full)], grads["meta"], "small_grads_start", gather=True)
        small_in_flight.extend(in_flight_now)
        return token

    dx, shipped = _local_step(x[0], loss_target[0], meta_full, pre_norm_g + token[:1, :1], post_norm_g, conv_w_full,
                              conv_b, conv_ln_g, conv_ln_b, b_pw2, weights, ship, ship_small)

    updated = [None] * depth
    done = shipped

    def update_layer(l):
        landed = {}
        for k, (names, sems, arrays, landings) in enumerate(in_flight[l]):
            landed.update(zip(names, _exchange_wait(sems, arrays, landings, done, f"exchange_wait_{l}_{k}", gather=False)))
        return [_sum_adamw(landed[name], w[l], m[l], v[l], "adamw_" + name) for name, w, m, v in (
            ("w_in", w_in, m_w_in, v_w_in), ("w_pw2", w_pw2, m_w_pw2, v_w_pw2), ("w_out", w_out, m_w_out, v_w_out))]

    for l in reversed(range(depth)):
        updated[l] = update_layer(l)
        done = updated[l][0][1]

    gathered, = _exchange_wait(*small_in_flight, done, "small_grads_wait", gather=True)
    summed = _unpack(_sum_parts(gathered, "sum_small_grads"), small_shapes_full)
    loss = summed[-1][0]
    g_small = dict(zip(small_names, summed))
    g_small["conv_w"] = lax.dynamic_slice_in_dim(g_small["conv_w"], me * conv_w.shape[2], conv_w.shape[2], axis=2)
    g_small["meta"] = lax.dynamic_slice_in_dim(g_small["meta"], me * meta_tokens.shape[1], meta_tokens.shape[1], axis=1)
    small_w = dict(zip(small_names, (pre_norm_g, post_norm_g, conv_b, conv_ln_g, conv_ln_b, b_pw2, conv_w, meta_tokens)))
    small_m = (m_pre_norm_g, m_post_norm_g, m_conv_b, m_conv_ln_g, m_conv_ln_b, m_b_pw2, m_conv_w, m_meta_tokens)
    small_v = (v_pre_norm_g, v_post_norm_g, v_conv_b, v_conv_ln_g, v_conv_ln_b, v_b_pw2, v_conv_w, v_meta_tokens)
    small_shapes = [small_w[k].shape for k in small_names]
    outs = _sum_adamw(_pack([g_small[k] for k in small_names])[None], _pack([small_w[k] for k in small_names]),
                      _pack(small_m), _pack(small_v), "adamw_small_weights")
    g_s, d_s, nm_s, nv_s = [dict(zip(small_names, _unpack(o, small_shapes))) for o in outs]

    (g_w_in, d_w_in, nm_w_in, nv_w_in), (g_w_pw2, d_w_pw2, nm_w_pw2, nv_w_pw2), (g_w_out, d_w_out, nm_w_out, nv_w_out) = [
        [jnp.stack([updated[l][a][k] for l in range(depth)]) for k in range(4)] for a in range(3)]

    def ordered(s, w_in_, w_pw2_, w_out_):
        return (s["meta"], s["pre_g"], s["post_g"], w_in_, s["conv_w"], s["conv_b"], s["ln_g"], s["ln_b"], w_pw2_,
                s["b_pw2"], w_out_)

    return (loss, dx[None], *ordered(g_s, g_w_in, g_w_pw2, g_w_out), *ordered(d_s, d_w_in, d_w_pw2, d_w_out),
            *ordered(nm_s, nm_w_in, nm_w_pw2, nm_w_out), *ordered(nv_s, nv_w_in, nv_w_pw2, nv_w_out))
```
